```python
import math
import jax, jax.numpy as jnp
from jax import lax
import numpy as np

D_MODEL = 2048
BATCH = 2
SEQ = 4096
DEPTH = 2
DEC_BATCH = 8
DEC_SEQ = 1
PAST_LEN = 16384
PAGE_SIZE = 128

HEAD_DIM = 128
MIX_WIDTH = D_MODEL
N_MEM = 256
N_MEM_HEADS = 4
MEM_WIDTH = N_MEM_HEADS * HEAD_DIM
MIXER_WIDTH = MIX_WIDTH - MEM_WIDTH
CHUNK = 128
N_GROUPS_A = 4
GROUP_DIM_A = MIXER_WIDTH // N_GROUPS_A
SWA_PATTERN = ((128, 1), (512, 4), (2048, 16))
N_SWA_GROUPS = 3
HEADS_PER_GROUP = MIXER_WIDTH // (HEAD_DIM * N_SWA_GROUPS)
N_HEADS_B = HEADS_PER_GROUP * N_SWA_GROUPS
N_BUCKETS = 32
MAX_EXACT = N_BUCKETS // 2
MAX_DISTANCE = 2048
D_FF = -(-8 * D_MODEL // (3 * 256)) * 256
N_MIXERS = 2
N_GMLP_LAYERS = (DEPTH + 1) // 2
N_SWA_LAYERS = DEPTH // 2
EPS = 1e-6
NEG_INF = -1e30

kernel_name = 'hybrid_gmlp_dilated_swa_decode_step'


def rms_norm(x, g):
    xf = x.astype(jnp.float32)
    y = xf * lax.rsqrt(jnp.mean(xf * xf, axis=-1, keepdims=True) + EPS)
    return (y * g.astype(jnp.float32)).astype(x.dtype)


def t5_bucket(dist):
    nf = jnp.maximum(dist, MAX_EXACT).astype(jnp.float32)
    large = MAX_EXACT + (jnp.log(nf / MAX_EXACT) / math.log(MAX_DISTANCE / MAX_EXACT) * (N_BUCKETS - MAX_EXACT)).astype(jnp.int32)
    large = jnp.minimum(large, N_BUCKETS - 1)
    return jnp.where(dist < MAX_EXACT, dist, large)


def group_bias(rel_bias, g, dil, n_back):
    dist = jnp.arange(n_back + 1, dtype=jnp.int32) * dil
    b = rel_bias[t5_bucket(dist)][:, g * HEADS_PER_GROUP:(g + 1) * HEADS_PER_GROUP]
    return b.T.astype(jnp.float32)


def spatial_gating(z, g_v, w_s, b_s):
    B, T, _ = z.shape
    uv = jax.nn.gelu(z[..., :2 * MIXER_WIDTH])
    u, v = uv[..., :MIXER_WIDTH], uv[..., MIXER_WIDTH:]
    v = rms_norm(v, g_v)
    n_chunks = -(-T // CHUNK)
    vp = jnp.pad(v, ((0, 0), (0, n_chunks * CHUNK - T), (0, 0)))
    vp = vp.reshape(B, n_chunks, CHUNK, N_GROUPS_A, GROUP_DIM_A)
    w = w_s * jnp.tril(jnp.ones((CHUNK, CHUNK), w_s.dtype))
    s = jnp.einsum('gij,bnjgc->bnigc', w, vp) + b_s.T[None, None, :, :, None]
    s = s.reshape(B, n_chunks * CHUNK, MIXER_WIDTH)[:, :T]
    return u * s, v


def memory_kv(mem, g_mem, w_mem_kv):
    B = mem.shape[0]
    return (rms_norm(mem, g_mem) @ w_mem_kv).reshape(B, N_MEM, 2, N_MEM_HEADS, HEAD_DIM)


def memory_attention(q_cols, kv):
    B, T, _ = q_cols.shape
    q = q_cols.reshape(B, T, N_MEM_HEADS, HEAD_DIM)
    s = jnp.einsum('bthd,bmhd->bhtm', q, kv[:, :, 0]).astype(jnp.float32) * (HEAD_DIM ** -0.5)
    p = jax.nn.softmax(s, axis=-1).astype(q.dtype)
    return jnp.einsum('bhtm,bmhd->bthd', p, kv[:, :, 1]).reshape(B, T, MEM_WIDTH)


def swa_qkv(z):
    B, T, _ = z.shape
    qkv = z[..., :3 * MIXER_WIDTH].reshape(B, T, 3, N_HEADS_B, HEAD_DIM)
    return qkv[:, :, 0], qkv[:, :, 1], qkv[:, :, 2]


def dilated_attention_prompt(q, k, v, dil, n_back, bias_j):
    B, T, H, Dh = q.shape
    L = T // dil
    blk = n_back
    n_blk = -(-L // blk)
    Lp = n_blk * blk

    def to_sub(a):
        a = a.reshape(B, L, dil, H, Dh).transpose(0, 2, 1, 3, 4)
        return jnp.pad(a, ((0, 0), (0, 0), (0, Lp - L), (0, 0), (0, 0)))

    def band(a):
        a = jnp.pad(a, ((0, 0), (0, 0), (blk, 0), (0, 0), (0, 0))).reshape(B, dil, n_blk + 1, blk, H, Dh)
        return jnp.concatenate([a[:, :, :-1], a[:, :, 1:]], axis=3)

    qb = to_sub(q).reshape(B, dil, n_blk, blk, H, Dh)
    kb, vb = band(to_sub(k)), band(to_sub(v))
    qi = jnp.arange(blk)[:, None]
    kj = jnp.arange(2 * blk)[None, :]
    j = qi + blk - kj
    key_idx = jnp.arange(n_blk)[:, None, None] * blk - blk + kj[None]
    valid = (j >= 0) & (j <= n_back) & (key_idx >= 0)
    bias = bias_j[:, jnp.clip(j, 0, n_back)]
    s = jnp.einsum('brnqhd,brnkhd->brnhqk', qb, kb).astype(jnp.float32) * (Dh ** -0.5) + bias
    s = jnp.where(valid[None, None, :, None], s, NEG_INF)
    m = jnp.max(s, axis=-1, keepdims=True)
    p = jnp.exp(s - m)
    den = jnp.sum(p, axis=-1, keepdims=True)
    o = jnp.einsum('brnhqk,brnkhd->brnqhd', (p / den).astype(v.dtype), vb)
    lse = (m + jnp.log(den))[..., 0]
    o = o.reshape(B, dil, Lp, H, Dh)[:, :, :L].transpose(0, 2, 1, 3, 4).reshape(B, T, H, Dh)
    lse = lse.transpose(0, 1, 2, 4, 3).reshape(B, dil, Lp, H)[:, :, :L].transpose(0, 2, 1, 3).reshape(B, T, H)
    return o, lse


def dilated_attention_sample(q, k_all, v_all, n_buf, dil, n_back, bias_j):
    S = q.shape[1]
    idx = n_buf + jnp.arange(S)[:, None] - jnp.arange(n_back + 1)[None, :] * dil
    valid = idx >= 0
    idx = jnp.maximum(idx, 0)
    kg, vg = k_all[:, idx], v_all[:, idx]
    s = jnp.einsum('bshd,bsjhd->bhsj', q, kg).astype(jnp.float32) * (q.shape[-1] ** -0.5) + bias_j[None, :, None, :]
    s = jnp.where(valid[None, None], s, NEG_INF)
    lse = jax.nn.logsumexp(s, axis=-1)
    p = jnp.exp(s - lse[..., None]).astype(v_all.dtype)
    o = jnp.einsum('bhsj,bsjhd->bshd', p, vg)
    return o, lse.transpose(0, 2, 1)


def merge_groups(outs, lses):
    alpha = jax.nn.softmax(jnp.stack(lses, axis=0), axis=0)
    o = jnp.stack(outs, axis=0) * alpha[..., None].astype(outs[0].dtype)
    G, B, T, H, Dh = o.shape
    return o.transpose(1, 2, 0, 3, 4).reshape(B, T, G * H * Dh)


def swa_prompt(z, bias_groups):
    q, k, v = swa_qkv(z)
    T = z.shape[1]
    outs, lses, rows = [], [], []
    for g, (win, dil) in enumerate(SWA_PATTERN):
        hs = slice(g * HEADS_PER_GROUP, (g + 1) * HEADS_PER_GROUP)
        qg, kg, vg = q[:, :, hs], k[:, :, hs], v[:, :, hs]
        o, lse = dilated_attention_prompt(qg, kg, vg, dil, win // dil, bias_groups[g])
        outs.append(o)
        lses.append(lse)
        n_keep = min(win, T)
        rows.append(jnp.stack([kg, vg], axis=2)[:, T - n_keep:])
    return merge_groups(outs, lses), rows


def swa_sample(z, bufs, bias_groups):
    q, k, v = swa_qkv(z)
    outs, lses, rows = [], [], []
    for g, (win, dil) in enumerate(SWA_PATTERN):
        hs = slice(g * HEADS_PER_GROUP, (g + 1) * HEADS_PER_GROUP)
        qg, kg, vg = q[:, :, hs], k[:, :, hs], v[:, :, hs]
        buf = bufs[g]
        k_all = jnp.concatenate([buf[:, :, 0], kg], axis=1)
        v_all = jnp.concatenate([buf[:, :, 1], vg], axis=1)
        o, lse = dilated_attention_sample(qg, k_all, v_all, buf.shape[1], dil, win // dil, bias_groups[g])
        outs.append(o)
        lses.append(lse)
        rows.append(jnp.stack([kg, vg], axis=2))
    return merge_groups(outs, lses), rows


def mix_residual(x, mixer_out, q_mem, mem_kv, w_out, g_post):
    o = jnp.concatenate([mixer_out, memory_attention(q_mem, mem_kv)], axis=-1) @ w_out
    return x + rms_norm(o, g_post)


def ffn_residual(x, g_pre, g_post, w_up, w_down):
    h = rms_norm(x, g_pre) @ w_up
    return x + rms_norm((jax.nn.silu(h[..., :D_FF]) * h[..., D_FF:]) @ w_down, g_post)


def setup_inputs(seed: int = 0) -> dict:
    key = jax.random.key(seed)
    ks = jax.random.split(key, 24)
    f32 = jnp.float32

    def nrm(k, shape, scale):
        return jax.random.normal(k, shape, f32) * scale

    def gain(k, shape):
        return 1.0 + 0.05 * jax.random.normal(k, shape, f32)

    win_rows = [min(w, PAST_LEN) for w, _ in SWA_PATTERN]
    kv_tail = (2, HEADS_PER_GROUP, HEAD_DIM)
    return {
        'x_prompt': nrm(ks[0], (BATCH, SEQ, D_MODEL), 1.0),
        'x_sample': nrm(ks[1], (DEC_BATCH, DEC_SEQ, D_MODEL), 1.0),
        'mem_prompt': nrm(ks[2], (BATCH, N_MEM, D_MODEL), 1.0),
        'cache_mem_kv': nrm(ks[3], (DEPTH, DEC_BATCH, N_MEM, 2, N_MEM_HEADS, HEAD_DIM), 1.0),
        'cache_win128_kv': nrm(ks[4], (N_SWA_LAYERS, DEC_BATCH, win_rows[0]) + kv_tail, 1.0),
        'cache_win512_kv': nrm(ks[5], (N_SWA_LAYERS, DEC_BATCH, win_rows[1]) + kv_tail, 1.0),
        'cache_win2048_kv': nrm(ks[6], (N_SWA_LAYERS, DEC_BATCH, win_rows[2]) + kv_tail, 1.0),
        'rel_bias': nrm(ks[7], (N_BUCKETS, N_HEADS_B), 0.5),
        'norm_mix_pre': gain(ks[8], (DEPTH, D_MODEL)),
        'norm_mix_post': gain(ks[9], (DEPTH, D_MODEL)),
        'norm_ffn_pre': gain(ks[10], (DEPTH, D_MODEL)),
        'norm_ffn_post': gain(ks[11], (DEPTH, D_MODEL)),
        'norm_mem': gain(ks[12], (DEPTH, D_MODEL)),
        'w_mem_kv': nrm(ks[13], (DEPTH, D_MODEL, 2 * MEM_WIDTH), D_MODEL ** -0.5),
        'w_in_a': nrm(ks[14], (N_GMLP_LAYERS, D_MODEL, 2 * MIXER_WIDTH + MEM_WIDTH), D_MODEL ** -0.5),
        'norm_v_a': gain(ks[15], (N_GMLP_LAYERS, MIXER_WIDTH)),
        'w_spatial_a': nrm(ks[16], (N_GMLP_LAYERS, N_GROUPS_A, CHUNK, CHUNK), CHUNK ** -0.5),
        'b_spatial_a': 1.0 + 0.1 * jax.random.normal(ks[17], (N_GMLP_LAYERS, N_GROUPS_A, CHUNK), f32),
        'w_in_b': nrm(ks[18], (N_SWA_LAYERS, D_MODEL, 3 * MIXER_WIDTH + MEM_WIDTH), D_MODEL ** -0.5),
        'w_out': nrm(ks[19], (DEPTH, MIX_WIDTH, D_MODEL), MIX_WIDTH ** -0.5),
        'w_ffn_up': nrm(ks[20], (DEPTH, D_MODEL, 2 * D_FF), D_MODEL ** -0.5),
        'w_ffn_down': nrm(ks[21], (DEPTH, D_FF, D_MODEL), D_FF ** -0.5),
    }


def reference(x_prompt, x_sample, mem_prompt, cache_mem_kv, cache_win128_kv, cache_win512_kv,
              cache_win2048_kv, rel_bias, norm_mix_pre, norm_mix_post, norm_ffn_pre, norm_ffn_post,
              norm_mem, w_mem_kv, w_in_a, norm_v_a, w_spatial_a, b_spatial_a, w_in_b, w_out,
              w_ffn_up, w_ffn_down):
    bias_groups = [group_bias(rel_bias, g, dil, win // dil) for g, (win, dil) in enumerate(SWA_PATTERN)]
    win_caches = (cache_win128_kv, cache_win512_kv, cache_win2048_kv)
    yp, ys = x_prompt, x_sample
    mem_kv_p, chunk_v_s = [], []
    win_p = [[] for _ in SWA_PATTERN]
    win_s = [[] for _ in SWA_PATTERN]
    for i in range(DEPTH):
        li = i // N_MIXERS
        kv_p = memory_kv(mem_prompt, norm_mem[i], w_mem_kv[i])
        kv_s = cache_mem_kv[i]
        mem_kv_p.append(kv_p)
        hp = rms_norm(yp, norm_mix_pre[i])
        hs = rms_norm(ys, norm_mix_pre[i])
        if i % N_MIXERS == 0:
            zp = hp @ w_in_a[li]
            zs = hs @ w_in_a[li]
            mp, _ = spatial_gating(zp, norm_v_a[li], w_spatial_a[li], b_spatial_a[li])
            ms, v_rows = spatial_gating(zs, norm_v_a[li], w_spatial_a[li], b_spatial_a[li])
            chunk_v_s.append(v_rows)
            qp, qs = zp[..., 2 * MIXER_WIDTH:], zs[..., 2 * MIXER_WIDTH:]
        else:
            zp = hp @ w_in_b[li]
            zs = hs @ w_in_b[li]
            mp, rows_p = swa_prompt(zp, bias_groups)
            ms, rows_s = swa_sample(zs, [c[li] for c in win_caches], bias_groups)
            for g in range(N_SWA_GROUPS):
                win_p[g].append(rows_p[g])
                win_s[g].append(rows_s[g])
            qp, qs = zp[..., 3 * MIXER_WIDTH:], zs[..., 3 * MIXER_WIDTH:]
        yp = mix_residual(yp, mp, qp, kv_p, w_out[i], norm_mix_post[i])
        ys = mix_residual(ys, ms, qs, kv_s, w_out[i], norm_mix_post[i])
        yp = ffn_residual(yp, norm_ffn_pre[i], norm_ffn_post[i], w_ffn_up[i], w_ffn_down[i])
        ys = ffn_residual(ys, norm_ffn_pre[i], norm_ffn_post[i], w_ffn_up[i], w_ffn_down[i])
    new_mem_kv_prompt = jnp.stack(mem_kv_p, axis=0)
    new_chunk_v_sample = jnp.stack(chunk_v_s, axis=0)
    new_win128_kv_prompt = jnp.stack(win_p[0], axis=0)
    new_win512_kv_prompt = jnp.stack(win_p[1], axis=0)
    new_win2048_kv_prompt = jnp.stack(win_p[2], axis=0)
    new_win128_kv_sample = jnp.stack(win_s[0], axis=0)
    new_win512_kv_sample = jnp.stack(win_s[1], axis=0)
    new_win2048_kv_sample = jnp.stack(win_s[2], axis=0)
    return (yp, ys, new_mem_kv_prompt, new_chunk_v_sample,
            new_win128_kv_prompt, new_win512_kv_prompt, new_win2048_kv_prompt,
            new_win128_kv_sample, new_win512_kv_sample, new_win2048_kv_sample)
```

```python
import functools
import math

import jax
import jax.numpy as jnp
from jax import lax
from jax.experimental import pallas as pl
from jax.experimental.pallas import tpu as pltpu

F32 = jnp.float32
BF16 = jnp.bfloat16

D_MODEL = 2048
HEAD_DIM = 128
N_MEM = 256
N_MEM_HEADS = 4
MEM_WIDTH = N_MEM_HEADS * HEAD_DIM
MIXER_WIDTH = D_MODEL - MEM_WIDTH
CHUNK = 128
N_GROUPS_A = 4
GROUP_DIM_A = MIXER_WIDTH // N_GROUPS_A
SWA_PATTERN = ((128, 1), (512, 4), (2048, 16))
N_SWA_GROUPS = len(SWA_PATTERN)
HEADS_PER_GROUP = 4
GROUP_WIDTH = HEADS_PER_GROUP * HEAD_DIM
N_BACK = 128
N_BUCKETS = 32
MAX_EXACT = N_BUCKETS // 2
MAX_DISTANCE = 2048
D_FF = 5632
EPS = 1e-6
NEG_INF = -1e30
ATTN_SCALE = HEAD_DIM ** -0.5
SAMPLE_PAD = 16

VMEM_LIMIT = 56 * 1024 * 1024


def _params(*sem):
    return pltpu.CompilerParams(dimension_semantics=sem, vmem_limit_bytes=VMEM_LIMIT)


def _gelu(x):
    return 0.5 * x * (1.0 + jnp.tanh(0.7978845608028654 * (x + 0.044715 * (x * x * x))))


def _rms(x, g):
    return x * lax.rsqrt(jnp.mean(x * x, axis=-1, keepdims=True) + EPS) * g


def _norm_matmul_kernel(x_ref, g_ref, w_ref, o_ref, xn_ref, *, n_gelu_tiles):
    n = pl.program_id(1)

    @pl.when(n == 0)
    def _():
        xn_ref[...] = _rms(x_ref[...], g_ref[...]).astype(BF16)

    acc = jnp.dot(xn_ref[...], w_ref[...], preferred_element_type=F32)
    if n_gelu_tiles == 0:
        o_ref[...] = acc.astype(o_ref.dtype)
    else:
        @pl.when(n < n_gelu_tiles)
        def _():
            o_ref[...] = _gelu(acc).astype(o_ref.dtype)

        @pl.when(n >= n_gelu_tiles)
        def _():
            o_ref[...] = acc.astype(o_ref.dtype)


def _norm_matmul(x, g, w, *, tm, tn, gelu_cols, out_dtype, name):
    m, k = x.shape
    n = w.shape[1]
    return pl.pallas_call(
        functools.partial(_norm_matmul_kernel, n_gelu_tiles=gelu_cols // tn),
        grid=(m // tm, n // tn),
        in_specs=[
            pl.BlockSpec((tm, k), lambda i, j: (i, 0)),
            pl.BlockSpec((1, k), lambda i, j: (0, 0)),
            pl.BlockSpec((k, tn), lambda i, j: (0, j)),
        ],
        out_specs=pl.BlockSpec((tm, tn), lambda i, j: (i, j)),
        out_shape=jax.ShapeDtypeStruct((m, n), out_dtype),
        scratch_shapes=[pltpu.VMEM((tm, k), BF16)],
        compiler_params=_params("parallel", "arbitrary"),
        name=name,
    )(x, g.reshape(1, k), w)


def _mem_attention_tile(q_ref, kv_ref, o_ref, col0):
    for h in range(N_MEM_HEADS):
        lo, hi = h * HEAD_DIM, (h + 1) * HEAD_DIM
        q = q_ref[:, lo:hi]
        k = kv_ref[:, lo:hi].astype(BF16)
        v = kv_ref[:, MEM_WIDTH + lo:MEM_WIDTH + hi].astype(BF16)
        s = lax.dot_general(q, k, (((1,), (1,)), ((), ())), preferred_element_type=F32) * ATTN_SCALE
        m = jnp.max(s, axis=1, keepdims=True)
        p = jnp.exp(s - m)
        den = jnp.sum(p, axis=1, keepdims=True)
        o = jnp.dot(p.astype(BF16), v, preferred_element_type=F32) / den
        o_ref[:, col0 + lo:col0 + hi] = o.astype(o_ref.dtype)


def _gmlp_mix_kernel(u_ref, v_ref, q_ref, gv_ref, ws_ref, bs_ref, kv_ref, o_ref, vn_ref):
    tm = u_ref.shape[0]
    v = v_ref[...].astype(F32)
    vn_ref[...] = _rms(v, gv_ref[...]).astype(BF16)
    row = lax.broadcasted_iota(jnp.int32, (CHUNK, CHUNK), 0)
    col = lax.broadcasted_iota(jnp.int32, (CHUNK, CHUNK), 1)
    causal = row >= col
    for g in range(N_GROUPS_A):
        w = jnp.where(causal, ws_ref[g], 0.0).astype(BF16)
        b = bs_ref[:, g:g + 1]
        c0, c1 = g * GROUP_DIM_A, (g + 1) * GROUP_DIM_A
        for c in range(tm // CHUNK):
            r0, r1 = c * CHUNK, (c + 1) * CHUNK
            s = jnp.dot(w, vn_ref[r0:r1, c0:c1], preferred_element_type=F32) + b
            o_ref[r0:r1, c0:c1] = (u_ref[r0:r1, c0:c1].astype(F32) * s).astype(o_ref.dtype)
    _mem_attention_tile(q_ref, kv_ref, o_ref, MIXER_WIDTH)


def _gmlp_mix(zact, g_v, w_s, b_s, mem_kv, *, tm, rows_per_batch):
    m = zact.shape[0]
    tiles_per_batch = rows_per_batch // tm
    return pl.pallas_call(
        _gmlp_mix_kernel,
        grid=(m // tm,),
        in_specs=[
            pl.BlockSpec((tm, MIXER_WIDTH), lambda i: (i, 0)),
            pl.BlockSpec((tm, MIXER_WIDTH), lambda i: (i, 1)),
            pl.BlockSpec((tm, MEM_WIDTH), lambda i: (i, 2 * MIXER_WIDTH // MEM_WIDTH)),
            pl.BlockSpec((1, MIXER_WIDTH), lambda i: (0, 0)),
            pl.BlockSpec((N_GROUPS_A, CHUNK, CHUNK), lambda i: (0, 0, 0)),
            pl.BlockSpec((CHUNK, N_GROUPS_A), lambda i: (0, 0)),
            pl.BlockSpec((None, N_MEM, 2 * MEM_WIDTH), lambda i: (i // tiles_per_batch, 0, 0)),
        ],
        out_specs=pl.BlockSpec((tm, D_MODEL), lambda i: (i, 0)),
        out_shape=jax.ShapeDtypeStruct((m, D_MODEL), BF16),
        scratch_shapes=[pltpu.VMEM((tm, MIXER_WIDTH), BF16)],
        compiler_params=_params("parallel"),
        name="gmlp_mix",
    )(zact, zact, zact, g_v.reshape(1, MIXER_WIDTH), w_s, b_s.T, mem_kv)


def _out_proj_kernel(mix_ref, w_ref, x_ref, g_ref, o_ref):
    o = jnp.dot(mix_ref[...], w_ref[...], preferred_element_type=F32)
    o_ref[...] = x_ref[...] + _rms(o, g_ref[...])


def _out_proj(mix, w, x, g, *, tm):
    m = x.shape[0]
    return pl.pallas_call(
        _out_proj_kernel,
        grid=(m // tm,),
        in_specs=[
            pl.BlockSpec((tm, D_MODEL), lambda i: (i, 0)),
            pl.BlockSpec((D_MODEL, D_MODEL), lambda i: (0, 0)),
            pl.BlockSpec((tm, D_MODEL), lambda i: (i, 0)),
            pl.BlockSpec((1, D_MODEL), lambda i: (0, 0)),
        ],
        out_specs=pl.BlockSpec((tm, D_MODEL), lambda i: (i, 0)),
        out_shape=jax.ShapeDtypeStruct((m, D_MODEL), F32),
        compiler_params=_params("parallel"),
        name="out_proj",
    )(mix, w, x, g.reshape(1, D_MODEL))


def _ffn_kernel(x_ref, gpre_ref, gpost_ref, wg_ref, wl_ref, wd_ref, o_ref, xn_ref):
    f = pl.program_id(1)

    @pl.when(f == 0)
    def _():
        xn_ref[...] = _rms(x_ref[...], gpre_ref[...]).astype(BF16)

    xn = xn_ref[...]
    hg = jnp.dot(xn, wg_ref[...], preferred_element_type=F32)
    hl = jnp.dot(xn, wl_ref[...], preferred_element_type=F32)
    a = (hg * jax.nn.sigmoid(hg) * hl).astype(BF16)
    part = jnp.dot(a, wd_ref[...], preferred_element_type=F32)

    @pl.when(f == 0)
    def _():
        o_ref[...] = part

    @pl.when(f > 0)
    def _():
        o_ref[...] += part

    @pl.when(f == pl.num_programs(1) - 1)
    def _():
        o_ref[...] = x_ref[...] + _rms(o_ref[...], gpost_ref[...])


def _ffn(x, g_pre, g_post, w_up, w_down, *, tm, tf):
    m = x.shape[0]
    nf = D_FF // tf
    return pl.pallas_call(
        _ffn_kernel,
        grid=(m // tm, nf),
        in_specs=[
            pl.BlockSpec((tm, D_MODEL), lambda i, f: (i, 0)),
            pl.BlockSpec((1, D_MODEL), lambda i, f: (0, 0)),
            pl.BlockSpec((1, D_MODEL), lambda i, f: (0, 0)),
            pl.BlockSpec((D_MODEL, tf), lambda i, f: (0, f)),
            pl.BlockSpec((D_MODEL, tf), lambda i, f: (0, nf + f)),
            pl.BlockSpec((tf, D_MODEL), lambda i, f: (f, 0)),
        ],
        out_specs=pl.BlockSpec((tm, D_MODEL), lambda i, f: (i, 0)),
        out_shape=jax.ShapeDtypeStruct((m, D_MODEL), F32),
        scratch_shapes=[pltpu.VMEM((tm, D_MODEL), BF16)],
        compiler_params=_params("parallel", "arbitrary"),
        name="ffn",
    )(x, g_pre.reshape(1, D_MODEL), g_post.reshape(1, D_MODEL), w_up, w_up, w_down)


def _swa_kernel(q_ref, k_ref, v_ref, tb_ref, o_ref, lse_ref):
    n_blk = q_ref.shape[0] // N_BACK
    lane = lax.broadcasted_iota(jnp.int32, (N_BACK, HEAD_DIM), 1)

    def block(qs, ks, table):
        lse_tile = jnp.zeros((N_BACK, HEAD_DIM), F32)
        for h in range(HEADS_PER_GROUP):
            lo, hi = h * HEAD_DIM, (h + 1) * HEAD_DIM
            q = q_ref[pl.ds(qs, N_BACK), lo:hi]
            kw = k_ref[pl.ds(ks, 2 * N_BACK), lo:hi]
            vw = v_ref[pl.ds(ks, 2 * N_BACK), lo:hi]
            s = lax.dot_general(q, kw, (((1,), (1,)), ((), ())), preferred_element_type=F32)
            s = s * ATTN_SCALE + tb_ref[table, h]
            m = jnp.max(s, axis=1, keepdims=True)
            p = jnp.exp(s - m)
            den = jnp.sum(p, axis=1, keepdims=True)
            o = jnp.dot(p.astype(BF16), vw, preferred_element_type=F32) / den
            o_ref[pl.ds(qs, N_BACK), lo:hi] = o.astype(o_ref.dtype)
            lse_tile = jnp.where(lane == h, m + jnp.log(den), lse_tile)
        lse_ref[pl.ds(qs, N_BACK), :] = lse_tile

    block(0, 0, 0)

    def body(n, carry):
        qs = pl.multiple_of(n * N_BACK, N_BACK)
        ks = pl.multiple_of((n - 1) * N_BACK, N_BACK)
        block(qs, ks, 1)
        return carry

    lax.fori_loop(1, n_blk, body, 0)


def _swa_group(z3, tables, g, dil):
    b, l, width = z3.shape
    blocks_per_row = width // dil // GROUP_WIDTH
    q_blk, k_blk, v_blk = g, 3 + g, 6 + g
    return pl.pallas_call(
        _swa_kernel,
        grid=(b, dil),
        in_specs=[
            pl.BlockSpec((None, l, GROUP_WIDTH), lambda i, r: (i, 0, r * blocks_per_row + q_blk)),
            pl.BlockSpec((None, l, GROUP_WIDTH), lambda i, r: (i, 0, r * blocks_per_row + k_blk)),
            pl.BlockSpec((None, l, GROUP_WIDTH), lambda i, r: (i, 0, r * blocks_per_row + v_blk)),
            pl.BlockSpec((2, HEADS_PER_GROUP, N_BACK, 2 * N_BACK), lambda i, r: (0, 0, 0, 0)),
        ],
        out_specs=[
            pl.BlockSpec((None, l, GROUP_WIDTH), lambda i, r: (i, 0, r)),
            pl.BlockSpec((None, l, HEAD_DIM), lambda i, r: (i, 0, r)),
        ],
        out_shape=[
            jax.ShapeDtypeStruct((b, l, dil * GROUP_WIDTH), BF16),
            jax.ShapeDtypeStruct((b, l, dil * HEAD_DIM), F32),
        ],
        compiler_params=_params("parallel", "parallel"),
        name=f"swa_group{g}",
    )(z3, z3, z3, tables)


def _swa_merge_kernel(o0_ref, o1_ref, o2_ref, l0_ref, l1_ref, l2_ref, q_ref, kv_ref, o_ref):
    o_refs = (o0_ref, o1_ref, o2_ref)
    l_refs = (l0_ref, l1_ref, l2_ref)
    for h in range(HEADS_PER_GROUP):
        lo, hi = h * HEAD_DIM, (h + 1) * HEAD_DIM
        ls = [l[:, h:h + 1] for l in l_refs]
        mx = jnp.maximum(jnp.maximum(ls[0], ls[1]), ls[2])
        es = [jnp.exp(l - mx) for l in ls]
        tot = es[0] + es[1] + es[2]
        for g in range(N_SWA_GROUPS):
            alpha = es[g] / tot
            o_ref[:, g * GROUP_WIDTH + lo:g * GROUP_WIDTH + hi] = (
                o_refs[g][:, lo:hi].astype(F32) * alpha).astype(o_ref.dtype)
    _mem_attention_tile(q_ref, kv_ref, o_ref, MIXER_WIDTH)


def _swa_merge(outs, lses, z, mem_kv, *, tm, rows_per_batch):
    m = z.shape[0]
    tiles_per_batch = rows_per_batch // tm
    q_blk = 3 * MIXER_WIDTH // MEM_WIDTH
    return pl.pallas_call(
        _swa_merge_kernel,
        grid=(m // tm,),
        in_specs=(
            [pl.BlockSpec((tm, GROUP_WIDTH), lambda i: (i, 0))] * 3
            + [pl.BlockSpec((tm, HEAD_DIM), lambda i: (i, 0))] * 3
            + [pl.BlockSpec((tm, MEM_WIDTH), lambda i: (i, q_blk)),
               pl.BlockSpec((None, N_MEM, 2 * MEM_WIDTH), lambda i: (i // tiles_per_batch, 0, 0))]
        ),
        out_specs=pl.BlockSpec((tm, D_MODEL), lambda i: (i, 0)),
        out_shape=jax.ShapeDtypeStruct((m, D_MODEL), BF16),
        compiler_params=_params("parallel"),
        name="swa_merge",
    )(*outs, *lses, z, mem_kv)


def _row_attention(q, k, v):
    return jnp.sum(k * q, axis=1, keepdims=True) * ATTN_SCALE


def _sample_mem_attention(q_row, kv_ref, o_ref, col0):
    for h in range(N_MEM_HEADS):
        lo, hi = h * HEAD_DIM, (h + 1) * HEAD_DIM
        q = q_row[:, lo:hi]
        k = kv_ref[:, lo:hi]
        v = kv_ref[:, MEM_WIDTH + lo:MEM_WIDTH + hi]
        s = jnp.sum(k * q, axis=1, keepdims=True) * ATTN_SCALE
        m = jnp.max(s, axis=0, keepdims=True)
        p = jnp.exp(s - m)
        den = jnp.sum(p, axis=0, keepdims=True)
        o_ref[:, col0 + lo:col0 + hi] = jnp.sum(p * v, axis=0, keepdims=True) / den


def _sample_mix_a_kernel(z_ref, gv_ref, w0_ref, b0_ref, kv_ref, o_ref, vrow_ref):
    u = z_ref[:, 0:MIXER_WIDTH]
    v = _rms(z_ref[:, MIXER_WIDTH:2 * MIXER_WIDTH], gv_ref[...])
    vrow_ref[...] = v
    o_ref[:, 0:MIXER_WIDTH] = u * (w0_ref[...] * v + b0_ref[...])
    _sample_mem_attention(z_ref[:, 2 * MIXER_WIDTH:2 * MIXER_WIDTH + MEM_WIDTH], kv_ref, o_ref, MIXER_WIDTH)


def _sample_mix_a(z, g_v, w_s, b_s, mem_kv):
    bd = mem_kv.shape[0]
    w0 = jnp.repeat(w_s[:, 0, 0], GROUP_DIM_A).reshape(1, MIXER_WIDTH)
    b0 = jnp.repeat(b_s[:, 0], GROUP_DIM_A).reshape(1, MIXER_WIDTH)
    width = z.shape[1]
    vec = lambda i: (0, 0)
    return pl.pallas_call(
        _sample_mix_a_kernel,
        grid=(bd,),
        in_specs=[
            pl.BlockSpec((None, 1, width), lambda i: (i, 0, 0)),
            pl.BlockSpec((1, MIXER_WIDTH), vec),
            pl.BlockSpec((1, MIXER_WIDTH), vec),
            pl.BlockSpec((1, MIXER_WIDTH), vec),
            pl.BlockSpec((None, N_MEM, 2 * MEM_WIDTH), lambda i: (i, 0, 0)),
        ],
        out_specs=[
            pl.BlockSpec((None, 1, D_MODEL), lambda i: (i, 0, 0)),
            pl.BlockSpec((None, 1, MIXER_WIDTH), lambda i: (i, 0, 0)),
        ],
        out_shape=[
            jax.ShapeDtypeStruct((bd, 1, D_MODEL), F32),
            jax.ShapeDtypeStruct((bd, 1, MIXER_WIDTH), F32),
        ],
        compiler_params=_params("parallel"),
        name="sample_mix_a",
    )(z[:bd].reshape(bd, 1, width), g_v.reshape(1, MIXER_WIDTH), w0, b0, mem_kv)


def _sample_mix_b_kernel(z_ref, c0_ref, c1_ref, c2_ref, bcol_ref, bnew_ref, kv_ref, o_ref):
    caches = (c0_ref, c1_ref, c2_ref)
    outs = [[None] * HEADS_PER_GROUP for _ in range(N_SWA_GROUPS)]
    lses = [[None] * HEADS_PER_GROUP for _ in range(N_SWA_GROUPS)]
    for g in range(N_SWA_GROUPS):
        for h in range(HEADS_PER_GROUP):
            hd = g * HEADS_PER_GROUP + h
            lo, hi = h * HEAD_DIM, (h + 1) * HEAD_DIM
            q = z_ref[:, hd * HEAD_DIM:(hd + 1) * HEAD_DIM]
            k_new = z_ref[:, MIXER_WIDTH + hd * HEAD_DIM:MIXER_WIDTH + (hd + 1) * HEAD_DIM]
            v_new = z_ref[:, 2 * MIXER_WIDTH + hd * HEAD_DIM:2 * MIXER_WIDTH + (hd + 1) * HEAD_DIM]
            kc = caches[g][:, lo:hi]
            vc = caches[g][:, GROUP_WIDTH + lo:GROUP_WIDTH + hi]
            s_c = jnp.sum(kc * q, axis=1, keepdims=True) * ATTN_SCALE + bcol_ref[g][:, h:h + 1]
            s_n = jnp.sum(k_new * q, axis=1, keepdims=True) * ATTN_SCALE + bnew_ref[:, hd:hd + 1]
            m = jnp.maximum(jnp.max(s_c, axis=0, keepdims=True), s_n)
            p_c = jnp.exp(s_c - m)
            p_n = jnp.exp(s_n - m)
            den = jnp.sum(p_c, axis=0, keepdims=True) + p_n
            outs[g][h] = (jnp.sum(p_c * vc, axis=0, keepdims=True) + p_n * v_new) / den
            lses[g][h] = m + jnp.log(den)
    for h in range(HEADS_PER_GROUP):
        ls = [lses[g][h] for g in range(N_SWA_GROUPS)]
        mx = jnp.maximum(jnp.maximum(ls[0], ls[1]), ls[2])
        es = [jnp.exp(l - mx) for l in ls]
        tot = es[0] + es[1] + es[2]
        for g in range(N_SWA_GROUPS):
            c0 = g * GROUP_WIDTH + h * HEAD_DIM
            o_ref[:, c0:c0 + HEAD_DIM] = outs[g][h] * (es[g] / tot)
    _sample_mem_attention(z_ref[:, 3 * MIXER_WIDTH:3 * MIXER_WIDTH + MEM_WIDTH], kv_ref, o_ref, MIXER_WIDTH)


def _sample_mix_b(z, win_caches, bias_groups, mem_kv):
    bd = mem_kv.shape[0]
    width = z.shape[1]
    kv_width = 2 * GROUP_WIDTH
    cache_views, cache_specs = [], []
    for g, (win, dil) in enumerate(SWA_PATTERN):
        cache_views.append(win_caches[g].reshape(bd, win // dil, dil * kv_width))
        cache_specs.append(pl.BlockSpec((None, N_BACK, kv_width), lambda i: (i, 0, 0)))
    bcol = jnp.stack([bg[:, N_BACK:0:-1].T for bg in bias_groups], axis=0)
    bnew = jnp.concatenate([bg[:, 0] for bg in bias_groups])
    bnew = jnp.pad(bnew, (0, HEAD_DIM - bnew.shape[0])).reshape(1, HEAD_DIM)
    return pl.pallas_call(
        _sample_mix_b_kernel,
        grid=(bd,),
        in_specs=[pl.BlockSpec((None, 1, width), lambda i: (i, 0, 0))] + cache_specs + [
            pl.BlockSpec((N_SWA_GROUPS, N_BACK, HEADS_PER_GROUP), lambda i: (0, 0, 0)),
            pl.BlockSpec((1, HEAD_DIM), lambda i: (0, 0)),
            pl.BlockSpec((None, N_MEM, 2 * MEM_WIDTH), lambda i: (i, 0, 0)),
        ],
        out_specs=pl.BlockSpec((None, 1, D_MODEL), lambda i: (i, 0, 0)),
        out_shape=jax.ShapeDtypeStruct((bd, 1, D_MODEL), F32),
        compiler_params=_params("parallel"),
        name="sample_mix_b",
    )(z[:bd].reshape(bd, 1, width), *cache_views, bcol, bnew, mem_kv)


def _t5_bucket(dist):
    nf = jnp.maximum(dist, MAX_EXACT).astype(F32)
    large = MAX_EXACT + (jnp.log(nf / MAX_EXACT) / math.log(MAX_DISTANCE / MAX_EXACT)
                         * (N_BUCKETS - MAX_EXACT)).astype(jnp.int32)
    large = jnp.minimum(large, N_BUCKETS - 1)
    return jnp.where(dist < MAX_EXACT, dist, large)


def _group_bias(rel_bias, g, dil):
    dist = jnp.arange(N_BACK + 1, dtype=jnp.int32) * dil
    b = rel_bias[_t5_bucket(dist)][:, g * HEADS_PER_GROUP:(g + 1) * HEADS_PER_GROUP]
    return b.T.astype(F32)


def _band_tables(bias_j):
    qi = jnp.arange(N_BACK)[:, None]
    kj = jnp.arange(2 * N_BACK)[None, :]
    tabs = []
    for off in (0, N_BACK):
        j = qi + off - kj
        valid = (j >= 0) & (j <= N_BACK)
        tabs.append(jnp.where(valid[None], bias_j[:, jnp.clip(j, 0, N_BACK)], NEG_INF))
    return jnp.stack(tabs, axis=0)


def kernel(x_prompt, x_sample, mem_prompt, cache_mem_kv, cache_win128_kv, cache_win512_kv, cache_win2048_kv, rel_bias, norm_mix_pre, norm_mix_post, norm_ffn_pre, norm_ffn_post, norm_mem, w_mem_kv, w_in_a, norm_v_a, w_spatial_a, b_spatial_a, w_in_b, w_out, w_ffn_up, w_ffn_down):
    batch, seq, _ = x_prompt.shape
    bd = x_sample.shape[0]
    depth = w_out.shape[0]
    m_p = batch * seq
    win_caches = (cache_win128_kv, cache_win512_kv, cache_win2048_kv)

    w_mem_kv_b = w_mem_kv.astype(BF16)
    w_in_a_b = w_in_a.astype(BF16)
    w_in_b_b = w_in_b.astype(BF16)
    w_out_b = w_out.astype(BF16)
    w_up_b = w_ffn_up.astype(BF16)
    w_down_b = w_ffn_down.astype(BF16)

    bias_groups = [_group_bias(rel_bias, g, dil) for g, (_, dil) in enumerate(SWA_PATTERN)]
    band_tables = [_band_tables(bg) for bg in bias_groups]

    yp = x_prompt.reshape(m_p, D_MODEL)
    ys = jnp.pad(x_sample.reshape(bd, D_MODEL), ((0, SAMPLE_PAD - bd), (0, 0)))
    mem_rows = mem_prompt.reshape(batch * N_MEM, D_MODEL)

    mem_kv_p, chunk_v_s = [], []
    win_p = [[] for _ in SWA_PATTERN]
    win_s = [[] for _ in SWA_PATTERN]
    for i in range(depth):
        li = i // 2
        kv_p = _norm_matmul(mem_rows, norm_mem[i], w_mem_kv_b[i], tm=batch * N_MEM, tn=512,
                            gelu_cols=0, out_dtype=F32, name="mem_kv").reshape(batch, N_MEM, 2 * MEM_WIDTH)
        kv_s = cache_mem_kv[i].reshape(bd, N_MEM, 2 * MEM_WIDTH)
        mem_kv_p.append(kv_p.reshape(batch, N_MEM, 2, N_MEM_HEADS, HEAD_DIM))
        if i % 2 == 0:
            zp = _norm_matmul(yp, norm_mix_pre[i], w_in_a_b[li], tm=1024, tn=512,
                              gelu_cols=2 * MIXER_WIDTH, out_dtype=BF16, name="in_proj_a")
            zs = _norm_matmul(ys, norm_mix_pre[i], w_in_a_b[li], tm=SAMPLE_PAD, tn=512,
                              gelu_cols=2 * MIXER_WIDTH, out_dtype=F32, name="in_proj_a_s")
            mix_p = _gmlp_mix(zp, norm_v_a[li], w_spatial_a[li], b_spatial_a[li], kv_p,
                              tm=512, rows_per_batch=seq)
            mix_s, v_rows = _sample_mix_a(zs, norm_v_a[li], w_spatial_a[li], b_spatial_a[li], kv_s)
            chunk_v_s.append(v_rows)
        else:
            zp = _norm_matmul(yp, norm_mix_pre[i], w_in_b_b[li], tm=1024, tn=512,
                              gelu_cols=0, out_dtype=BF16, name="in_proj_b")
            zs = _norm_matmul(ys, norm_mix_pre[i], w_in_b_b[li], tm=SAMPLE_PAD, tn=512,
                              gelu_cols=0, out_dtype=F32, name="in_proj_b_s")
            width = zp.shape[1]
            outs, lses = [], []
            for g, (win, dil) in enumerate(SWA_PATTERN):
                o, lse = _swa_group(zp.reshape(batch, seq // dil, dil * width), band_tables[g], g, dil)
                outs.append(o.reshape(m_p, GROUP_WIDTH))
                lses.append(lse.reshape(m_p, HEAD_DIM))
                kv_cols = zp.reshape(batch, seq, width)[:, seq - win:, MIXER_WIDTH:3 * MIXER_WIDTH]
                kv_cols = kv_cols.reshape(batch, win, 2, N_SWA_GROUPS, HEADS_PER_GROUP, HEAD_DIM)[:, :, :, g]
                win_p[g].append(kv_cols.astype(F32))
                kv_new = zs[:bd, MIXER_WIDTH:3 * MIXER_WIDTH]
                kv_new = kv_new.reshape(bd, 1, 2, N_SWA_GROUPS, HEADS_PER_GROUP, HEAD_DIM)[:, :, :, g]
                win_s[g].append(kv_new)
            mix_p = _swa_merge(outs, lses, zp, kv_p, tm=512, rows_per_batch=seq)
            mix_s = _sample_mix_b(zs, [c[li] for c in win_caches], bias_groups, kv_s)
        mix_s = jnp.pad(mix_s.reshape(bd, D_MODEL), ((0, SAMPLE_PAD - bd), (0, 0))).astype(BF16)
        yp = _out_proj(mix_p, w_out_b[i], yp, norm_mix_post[i], tm=512)
        ys = _out_proj(mix_s, w_out_b[i], ys, norm_mix_post[i], tm=SAMPLE_PAD)
        yp = _ffn(yp, norm_ffn_pre[i], norm_ffn_post[i], w_up_b[i], w_down_b[i], tm=512, tf=512)
        ys = _ffn(ys, norm_ffn_pre[i], norm_ffn_post[i], w_up_b[i], w_down_b[i], tm=SAMPLE_PAD, tf=512)

    return (
        yp.reshape(batch, seq, D_MODEL),
        ys[:bd].reshape(bd, 1, D_MODEL),
        jnp.stack(mem_kv_p, axis=0),
        jnp.stack(chunk_v_s, axis=0),
        jnp.stack(win_p[0], axis=0),
        jnp.stack(win_p[1], axis=0),
        jnp.stack(win_p[2], axis=0),
        jnp.stack(win_s[0], axis=0),
        jnp.stack(win_s[1], axis=0),
        jnp.stack(win_s[2], axis=0),
    )
```

```python
import functools
import math

import jax
import jax.numpy as jnp
from jax import lax
from jax.experimental import pallas as pl
from jax.experimental.pallas import tpu as pltpu

F32 = jnp.float32
BF16 = jnp.bfloat16

D_MODEL = 2048
HEAD_DIM = 128
N_MEM = 256
N_MEM_HEADS = 4
MEM_WIDTH = N_MEM_HEADS * HEAD_DIM
MIXER_WIDTH = D_MODEL - MEM_WIDTH
CHUNK = 128
N_GROUPS_A = 4
GROUP_DIM_A = MIXER_WIDTH // N_GROUPS_A
SWA_PATTERN = ((128, 1), (512, 4), (2048, 16))
N_SWA_GROUPS = len(SWA_PATTERN)
HEADS_PER_GROUP = 4
GROUP_WIDTH = HEADS_PER_GROUP * HEAD_DIM
N_BACK = 128
N_BUCKETS = 32
MAX_EXACT = N_BUCKETS // 2
MAX_DISTANCE = 2048
D_FF = 5632
EPS = 1e-6
NEG_INF = -1e30
ATTN_SCALE = HEAD_DIM ** -0.5
SAMPLE_PAD = 16
SWA_TILE = 512

VMEM_LIMIT = 56 * 1024 * 1024


def _params(*sem):
    return pltpu.CompilerParams(dimension_semantics=sem, vmem_limit_bytes=VMEM_LIMIT)


def _gelu(x):
    return 0.5 * x * (1.0 + jnp.tanh(0.7978845608028654 * (x + 0.044715 * (x * x * x))))


def _rms(x, g):
    return x * lax.rsqrt(jnp.mean(x * x, axis=-1, keepdims=True) + EPS) * g


def _log2(n):
    assert n & (n - 1) == 0
    return n.bit_length() - 1


def _residue_major_perm(tm, dil, transpose=False):
    n = tm // dil
    row = lax.broadcasted_iota(jnp.int32, (tm, tm), 0)
    col = lax.broadcasted_iota(jnp.int32, (tm, tm), 1)
    dst, src = (col, row) if transpose else (row, col)
    want = lax.shift_left(jnp.bitwise_and(dst, n - 1), _log2(dil)) + lax.shift_right_logical(dst, _log2(n))
    return (src == want).astype(BF16)


def _norm_matmul_kernel(x_ref, g_ref, w_ref, o_ref, xn_ref, *, n_gelu_tiles):
    n = pl.program_id(1)

    @pl.when(n == 0)
    def _():
        xn_ref[...] = _rms(x_ref[...], g_ref[...]).astype(BF16)

    acc = jnp.dot(xn_ref[...], w_ref[...], preferred_element_type=F32)
    if n_gelu_tiles == 0:
        o_ref[...] = acc.astype(o_ref.dtype)
    else:
        @pl.when(n < n_gelu_tiles)
        def _():
            o_ref[...] = _gelu(acc).astype(o_ref.dtype)

        @pl.when(n >= n_gelu_tiles)
        def _():
            o_ref[...] = acc.astype(o_ref.dtype)


def _norm_matmul(x, g, w, layer, *, tm, tn, gelu_cols, out_dtype, name):
    m, k = x.shape
    n = w.shape[2]
    return pl.pallas_call(
        functools.partial(_norm_matmul_kernel, n_gelu_tiles=gelu_cols // tn),
        grid=(m // tm, n // tn),
        in_specs=[
            pl.BlockSpec((tm, k), lambda i, j: (i, 0)),
            pl.BlockSpec((1, k), lambda i, j: (0, 0)),
            pl.BlockSpec((None, k, tn), lambda i, j: (layer, 0, j)),
        ],
        out_specs=pl.BlockSpec((tm, tn), lambda i, j: (i, j)),
        out_shape=jax.ShapeDtypeStruct((m, n), out_dtype),
        scratch_shapes=[pltpu.VMEM((tm, k), BF16)],
        compiler_params=_params("parallel", "arbitrary"),
        name=name,
    )(x, g.reshape(1, k), w)


def _in_proj_b_kernel(x_ref, g_ref, w_ref, *refs):
    out_refs, xn_ref = refs[:-1], refs[-1]
    tm = x_ref.shape[0]
    j = pl.program_id(1)

    @pl.when(j == 0)
    def _():
        xn = _rms(x_ref[...], g_ref[...]).astype(BF16)
        xn_ref[0] = xn
        for g in range(1, N_SWA_GROUPS):
            perm = _residue_major_perm(tm, SWA_PATTERN[g][1])
            xn_ref[g] = jnp.dot(perm, xn, preferred_element_type=F32).astype(BF16)

    src = jnp.where(j < 3 * N_SWA_GROUPS, j // 3, 0)
    acc = jnp.dot(xn_ref[src], w_ref[...], preferred_element_type=F32).astype(BF16)
    for step, o_ref in enumerate(out_refs):
        @pl.when(j == step)
        def _(o_ref=o_ref):
            if len(o_ref.shape) == 2:
                o_ref[...] = acc
            else:
                n = o_ref.shape[1]
                for r in range(o_ref.shape[0]):
                    o_ref[r] = acc[r * n:(r + 1) * n, :]


def _in_proj_b(x, g, w, layer, *, batch, seq):
    m, k = x.shape
    tm = SWA_TILE
    tiles_per_batch = seq // tm
    n_steps = 3 * N_SWA_GROUPS + 1

    def w_map(i, j):
        return (layer, 0, jnp.where(j < 3 * N_SWA_GROUPS, (j % 3) * N_SWA_GROUPS + j // 3, 3 * N_SWA_GROUPS))

    out_specs, out_shape = [], []
    for _, dil in SWA_PATTERN:
        for _ in range(3):
            out_specs.append(pl.BlockSpec((None, dil, tm // dil, GROUP_WIDTH),
                                          lambda i, j: (i // tiles_per_batch, 0, i % tiles_per_batch, 0)))
            out_shape.append(jax.ShapeDtypeStruct((batch, dil, seq // dil, GROUP_WIDTH), BF16))
    out_specs.append(pl.BlockSpec((tm, MEM_WIDTH), lambda i, j: (i, 0)))
    out_shape.append(jax.ShapeDtypeStruct((m, MEM_WIDTH), BF16))
    outs = pl.pallas_call(
        _in_proj_b_kernel,
        grid=(m // tm, n_steps),
        in_specs=[
            pl.BlockSpec((tm, k), lambda i, j: (i, 0)),
            pl.BlockSpec((1, k), lambda i, j: (0, 0)),
            pl.BlockSpec((None, k, GROUP_WIDTH), w_map),
        ],
        out_specs=out_specs,
        out_shape=out_shape,
        scratch_shapes=[pltpu.VMEM((N_SWA_GROUPS, tm, k), BF16)],
        compiler_params=_params("parallel", "arbitrary"),
        name="in_proj_b",
    )(x, g.reshape(1, k), w)
    return [outs[3 * g:3 * g + 3] for g in range(N_SWA_GROUPS)], outs[-1]


def _mem_attention_tile(q_ref, kv_ref, o_ref, col0):
    for h in range(N_MEM_HEADS):
        lo, hi = h * HEAD_DIM, (h + 1) * HEAD_DIM
        q = q_ref[:, lo:hi]
        k = kv_ref[:, lo:hi].astype(BF16)
        v = kv_ref[:, MEM_WIDTH + lo:MEM_WIDTH + hi].astype(BF16)
        s = lax.dot_general(q, k, (((1,), (1,)), ((), ())), preferred_element_type=F32) * ATTN_SCALE
        m = jnp.max(s, axis=1, keepdims=True)
        p = jnp.exp(s - m)
        den = jnp.sum(p, axis=1, keepdims=True)
        o = jnp.dot(p.astype(BF16), v, preferred_element_type=F32) / den
        o_ref[:, col0 + lo:col0 + hi] = o.astype(o_ref.dtype)


def _gmlp_mix_kernel(u_ref, v_ref, q_ref, gv_ref, ws_ref, bs_ref, kv_ref, o_ref, vn_ref):
    tm = u_ref.shape[0]
    v = v_ref[...].astype(F32)
    vn_ref[...] = _rms(v, gv_ref[...]).astype(BF16)
    row = lax.broadcasted_iota(jnp.int32, (CHUNK, CHUNK), 0)
    col = lax.broadcasted_iota(jnp.int32, (CHUNK, CHUNK), 1)
    causal = row >= col
    for g in range(N_GROUPS_A):
        w = jnp.where(causal, ws_ref[g], 0.0).astype(BF16)
        b = bs_ref[:, g:g + 1]
        c0, c1 = g * GROUP_DIM_A, (g + 1) * GROUP_DIM_A
        for c in range(tm // CHUNK):
            r0, r1 = c * CHUNK, (c + 1) * CHUNK
            s = jnp.dot(w, vn_ref[r0:r1, c0:c1], preferred_element_type=F32) + b
            o_ref[r0:r1, c0:c1] = (u_ref[r0:r1, c0:c1].astype(F32) * s).astype(o_ref.dtype)
    _mem_attention_tile(q_ref, kv_ref, o_ref, MIXER_WIDTH)


def _gmlp_mix(zact, g_v, w_s, b_s, mem_kv, *, tm, rows_per_batch):
    m = zact.shape[0]
    tiles_per_batch = rows_per_batch // tm
    return pl.pallas_call(
        _gmlp_mix_kernel,
        grid=(m // tm,),
        in_specs=[
            pl.BlockSpec((tm, MIXER_WIDTH), lambda i: (i, 0)),
            pl.BlockSpec((tm, MIXER_WIDTH), lambda i: (i, 1)),
            pl.BlockSpec((tm, MEM_WIDTH), lambda i: (i, 2 * MIXER_WIDTH // MEM_WIDTH)),
            pl.BlockSpec((1, MIXER_WIDTH), lambda i: (0, 0)),
            pl.BlockSpec((N_GROUPS_A, CHUNK, CHUNK), lambda i: (0, 0, 0)),
            pl.BlockSpec((CHUNK, N_GROUPS_A), lambda i: (0, 0)),
            pl.BlockSpec((None, N_MEM, 2 * MEM_WIDTH), lambda i: (i // tiles_per_batch, 0, 0)),
        ],
        out_specs=pl.BlockSpec((tm, D_MODEL), lambda i: (i, 0)),
        out_shape=jax.ShapeDtypeStruct((m, D_MODEL), BF16),
        scratch_shapes=[pltpu.VMEM((tm, MIXER_WIDTH), BF16)],
        compiler_params=_params("parallel"),
        name="gmlp_mix",
    )(zact, zact, zact, g_v.reshape(1, MIXER_WIDTH), w_s, b_s.T, mem_kv)


def _out_proj_kernel(mix_ref, w_ref, x_ref, g_ref, o_ref):
    o = jnp.dot(mix_ref[...], w_ref[...], preferred_element_type=F32)
    o_ref[...] = x_ref[...] + _rms(o, g_ref[...])


def _out_proj(mix, w, layer, x, g, *, tm):
    m = x.shape[0]
    return pl.pallas_call(
        _out_proj_kernel,
        grid=(m // tm,),
        in_specs=[
            pl.BlockSpec((tm, D_MODEL), lambda i: (i, 0)),
            pl.BlockSpec((None, D_MODEL, D_MODEL), lambda i: (layer, 0, 0)),
            pl.BlockSpec((tm, D_MODEL), lambda i: (i, 0)),
            pl.BlockSpec((1, D_MODEL), lambda i: (0, 0)),
        ],
        out_specs=pl.BlockSpec((tm, D_MODEL), lambda i: (i, 0)),
        out_shape=jax.ShapeDtypeStruct((m, D_MODEL), F32),
        compiler_params=_params("parallel"),
        name="out_proj",
    )(mix, w, x, g.reshape(1, D_MODEL))


def _ffn_kernel(x_ref, gpre_ref, gpost_ref, wg_ref, wl_ref, wd_ref, o_ref, xn_ref):
    f = pl.program_id(1)

    @pl.when(f == 0)
    def _():
        xn_ref[...] = _rms(x_ref[...], gpre_ref[...]).astype(BF16)

    xn = xn_ref[...]
    hg = jnp.dot(xn, wg_ref[...], preferred_element_type=F32)
    hl = jnp.dot(xn, wl_ref[...], preferred_element_type=F32)
    a = (hg * jax.nn.sigmoid(hg) * hl).astype(BF16)
    part = jnp.dot(a, wd_ref[...], preferred_element_type=F32)

    @pl.when(f == 0)
    def _():
        o_ref[...] = part

    @pl.when(f > 0)
    def _():
        o_ref[...] += part

    @pl.when(f == pl.num_programs(1) - 1)
    def _():
        o_ref[...] = x_ref[...] + _rms(o_ref[...], gpost_ref[...])


def _ffn(x, g_pre, g_post, w_up, w_down, layer, *, tm, tf, single_buffer):
    m = x.shape[0]
    nf = D_FF // tf
    mode = dict(pipeline_mode=pl.Buffered(1)) if single_buffer else {}
    return pl.pallas_call(
        _ffn_kernel,
        grid=(m // tm, nf),
        in_specs=[
            pl.BlockSpec((tm, D_MODEL), lambda i, f: (i, 0), **mode),
            pl.BlockSpec((1, D_MODEL), lambda i, f: (0, 0)),
            pl.BlockSpec((1, D_MODEL), lambda i, f: (0, 0)),
            pl.BlockSpec((None, D_MODEL, tf), lambda i, f: (layer, 0, f)),
            pl.BlockSpec((None, D_MODEL, tf), lambda i, f: (layer, 0, nf + f)),
            pl.BlockSpec((None, tf, D_MODEL), lambda i, f: (layer, f, 0)),
        ],
        out_specs=pl.BlockSpec((tm, D_MODEL), lambda i, f: (i, 0), **mode),
        out_shape=jax.ShapeDtypeStruct((m, D_MODEL), F32),
        scratch_shapes=[pltpu.VMEM((tm, D_MODEL), BF16)],
        compiler_params=_params("parallel", "arbitrary"),
        name="ffn",
    )(x, g_pre.reshape(1, D_MODEL), g_post.reshape(1, D_MODEL), w_up, w_up, w_down)


def _swa_kernel(q_ref, k_ref, v_ref, tb_ref, o_ref, lse_ref):
    n_blk = q_ref.shape[0] // N_BACK
    lane = lax.broadcasted_iota(jnp.int32, (N_BACK, HEAD_DIM), 1)

    def block(qs, ks, table):
        n_keys = N_BACK if table == 0 else 2 * N_BACK
        lse_tile = jnp.zeros((N_BACK, HEAD_DIM), F32)
        for h in range(HEADS_PER_GROUP):
            lo, hi = h * HEAD_DIM, (h + 1) * HEAD_DIM
            q = q_ref[pl.ds(qs, N_BACK), lo:hi]
            kw = k_ref[pl.ds(ks, n_keys), lo:hi]
            vw = v_ref[pl.ds(ks, n_keys), lo:hi]
            s = lax.dot_general(q, kw, (((1,), (1,)), ((), ())), preferred_element_type=F32)
            s = s * ATTN_SCALE + tb_ref[table, h][:, :n_keys]
            m = jnp.max(s, axis=1, keepdims=True)
            p = jnp.exp(s - m)
            den = jnp.sum(p, axis=1, keepdims=True)
            o = jnp.dot(p.astype(BF16), vw, preferred_element_type=F32) / den
            o_ref[pl.ds(qs, N_BACK), lo:hi] = o.astype(o_ref.dtype)
            lse_tile = jnp.where(lane == h, m + jnp.log(den), lse_tile)
        lse_ref[pl.ds(qs, N_BACK), :] = lse_tile

    block(0, 0, 0)

    def body(n, carry):
        qs = pl.multiple_of(n * N_BACK, N_BACK)
        ks = pl.multiple_of((n - 1) * N_BACK, N_BACK)
        block(qs, ks, 1)
        return carry

    lax.fori_loop(1, n_blk, body, 0)


def _swa_group(q, k, v, tables, g):
    b, dil, l, _ = q.shape
    rows = pl.BlockSpec((None, None, l, GROUP_WIDTH), lambda i, r: (i, r, 0, 0))
    return pl.pallas_call(
        _swa_kernel,
        grid=(b, dil),
        in_specs=[rows, rows, rows,
                  pl.BlockSpec((None, 2, HEADS_PER_GROUP, N_BACK, 2 * N_BACK), lambda i, r: (g, 0, 0, 0, 0))],
        out_specs=[rows, pl.BlockSpec((None, None, l, HEAD_DIM), lambda i, r: (i, r, 0, 0))],
        out_shape=[
            jax.ShapeDtypeStruct((b, dil, l, GROUP_WIDTH), BF16),
            jax.ShapeDtypeStruct((b, dil, l, HEAD_DIM), F32),
        ],
        compiler_params=_params("parallel", "parallel"),
        name=f"swa_group{g}",
    )(q, k, v, tables)


def _split3(x):
    hi = x.astype(BF16)
    rest = x - hi.astype(F32)
    mid = rest.astype(BF16)
    lo = (rest - mid.astype(F32)).astype(BF16)
    return hi, mid, lo


def _swa_merge_kernel(o0_ref, o1_ref, o2_ref, l0_ref, l1_ref, l2_ref, q_ref, kv_ref, o_ref):
    tm = o_ref.shape[0]
    outs, lses = [], []
    for g, (o_g, l_g) in enumerate(((o0_ref, l0_ref), (o1_ref, l1_ref), (o2_ref, l2_ref))):
        dil = SWA_PATTERN[g][1]
        o = jnp.concatenate([o_g[r] for r in range(dil)], axis=0)
        l = jnp.concatenate([l_g[r] for r in range(dil)], axis=0)
        if dil > 1:
            inv = _residue_major_perm(tm, dil, transpose=True)
            o = jnp.dot(inv, o, preferred_element_type=F32)
            l = sum(jnp.dot(inv, t, preferred_element_type=F32) for t in _split3(l))
        outs.append(o.astype(F32))
        lses.append(l)
    for h in range(HEADS_PER_GROUP):
        lo, hi = h * HEAD_DIM, (h + 1) * HEAD_DIM
        ls = [l[:, h:h + 1] for l in lses]
        mx = jnp.maximum(jnp.maximum(ls[0], ls[1]), ls[2])
        es = [jnp.exp(l - mx) for l in ls]
        tot = es[0] + es[1] + es[2]
        for g in range(N_SWA_GROUPS):
            alpha = es[g] / tot
            o_ref[:, g * GROUP_WIDTH + lo:g * GROUP_WIDTH + hi] = (outs[g][:, lo:hi] * alpha).astype(o_ref.dtype)
    _mem_attention_tile(q_ref, kv_ref, o_ref, MIXER_WIDTH)


def _swa_merge(outs, lses, q_mem, mem_kv, *, rows_per_batch):
    m = q_mem.shape[0]
    tm = SWA_TILE
    tiles_per_batch = rows_per_batch // tm

    def tile(width, dil):
        return pl.BlockSpec((None, dil, tm // dil, width),
                            lambda i: (i // tiles_per_batch, 0, i % tiles_per_batch, 0))

    return pl.pallas_call(
        _swa_merge_kernel,
        grid=(m // tm,),
        in_specs=(
            [tile(GROUP_WIDTH, dil) for _, dil in SWA_PATTERN]
            + [tile(HEAD_DIM, dil) for _, dil in SWA_PATTERN]
            + [pl.BlockSpec((tm, MEM_WIDTH), lambda i: (i, 0)),
               pl.BlockSpec((None, N_MEM, 2 * MEM_WIDTH), lambda i: (i // tiles_per_batch, 0, 0))]
        ),
        out_specs=pl.BlockSpec((tm, D_MODEL), lambda i: (i, 0)),
        out_shape=jax.ShapeDtypeStruct((m, D_MODEL), BF16),
        compiler_params=_params("parallel"),
        name="swa_merge",
    )(*outs, *lses, q_mem, mem_kv)


def _sample_mem_attention(q_row, kv_ref, o_ref, col0):
    for h in range(N_MEM_HEADS):
        lo, hi = h * HEAD_DIM, (h + 1) * HEAD_DIM
        q = q_row[:, lo:hi]
        k = kv_ref[:, lo:hi]
        v = kv_ref[:, MEM_WIDTH + lo:MEM_WIDTH + hi]
        s = jnp.sum(k * q, axis=1, keepdims=True) * ATTN_SCALE
        m = jnp.max(s, axis=0, keepdims=True)
        p = jnp.exp(s - m)
        den = jnp.sum(p, axis=0, keepdims=True)
        o_ref[:, col0 + lo:col0 + hi] = jnp.sum(p * v, axis=0, keepdims=True) / den


def _sample_mix_a_kernel(z_ref, gv_ref, w0_ref, b0_ref, kv_ref, o_ref, vrow_ref):
    u = z_ref[:, 0:MIXER_WIDTH]
    v = _rms(z_ref[:, MIXER_WIDTH:2 * MIXER_WIDTH], gv_ref[...])
    vrow_ref[...] = v
    o_ref[:, 0:MIXER_WIDTH] = u * (w0_ref[...] * v + b0_ref[...])
    _sample_mem_attention(z_ref[:, 2 * MIXER_WIDTH:2 * MIXER_WIDTH + MEM_WIDTH], kv_ref, o_ref, MIXER_WIDTH)


def _sample_mix_a(z, g_v, w_s, b_s, mem_kv):
    bd = mem_kv.shape[0]
    w0 = jnp.repeat(w_s[:, 0, 0], GROUP_DIM_A).reshape(1, MIXER_WIDTH)
    b0 = jnp.repeat(b_s[:, 0], GROUP_DIM_A).reshape(1, MIXER_WIDTH)
    width = z.shape[1]
    vec = lambda i: (0, 0)
    return pl.pallas_call(
        _sample_mix_a_kernel,
        grid=(bd,),
        in_specs=[
            pl.BlockSpec((None, 1, width), lambda i: (i, 0, 0)),
            pl.BlockSpec((1, MIXER_WIDTH), vec),
            pl.BlockSpec((1, MIXER_WIDTH), vec),
            pl.BlockSpec((1, MIXER_WIDTH), vec),
            pl.BlockSpec((None, N_MEM, 2 * MEM_WIDTH), lambda i: (i, 0, 0)),
        ],
        out_specs=[
            pl.BlockSpec((None, 1, D_MODEL), lambda i: (i, 0, 0)),
            pl.BlockSpec((None, 1, MIXER_WIDTH), lambda i: (i, 0, 0)),
        ],
        out_shape=[
            jax.ShapeDtypeStruct((bd, 1, D_MODEL), F32),
            jax.ShapeDtypeStruct((bd, 1, MIXER_WIDTH), F32),
        ],
        compiler_params=_params("parallel"),
        name="sample_mix_a",
    )(z[:bd].reshape(bd, 1, width), g_v.reshape(1, MIXER_WIDTH), w0, b0, mem_kv)


def _sample_mix_b_kernel(z_ref, c0_ref, c1_ref, c2_ref, bcol_ref, bnew_ref, kv_ref, o_ref):
    caches = (c0_ref, c1_ref, c2_ref)
    outs = [[None] * HEADS_PER_GROUP for _ in range(N_SWA_GROUPS)]
    lses = [[None] * HEADS_PER_GROUP for _ in range(N_SWA_GROUPS)]
    for g in range(N_SWA_GROUPS):
        for h in range(HEADS_PER_GROUP):
            hd = g * HEADS_PER_GROUP + h
            lo, hi = h * HEAD_DIM, (h + 1) * HEAD_DIM
            q = z_ref[:, hd * HEAD_DIM:(hd + 1) * HEAD_DIM]
            k_new = z_ref[:, MIXER_WIDTH + hd * HEAD_DIM:MIXER_WIDTH + (hd + 1) * HEAD_DIM]
            v_new = z_ref[:, 2 * MIXER_WIDTH + hd * HEAD_DIM:2 * MIXER_WIDTH + (hd + 1) * HEAD_DIM]
            kc = caches[g][:, lo:hi]
            vc = caches[g][:, GROUP_WIDTH + lo:GROUP_WIDTH + hi]
            s_c = jnp.sum(kc * q, axis=1, keepdims=True) * ATTN_SCALE + bcol_ref[g][:, h:h + 1]
            s_n = jnp.sum(k_new * q, axis=1, keepdims=True) * ATTN_SCALE + bnew_ref[:, hd:hd + 1]
            m = jnp.maximum(jnp.max(s_c, axis=0, keepdims=True), s_n)
            p_c = jnp.exp(s_c - m)
            p_n = jnp.exp(s_n - m)
            den = jnp.sum(p_c, axis=0, keepdims=True) + p_n
            outs[g][h] = (jnp.sum(p_c * vc, axis=0, keepdims=True) + p_n * v_new) / den
            lses[g][h] = m + jnp.log(den)
    for h in range(HEADS_PER_GROUP):
        ls = [lses[g][h] for g in range(N_SWA_GROUPS)]
        mx = jnp.maximum(jnp.maximum(ls[0], ls[1]), ls[2])
        es = [jnp.exp(l - mx) for l in ls]
        tot = es[0] + es[1] + es[2]
        for g in range(N_SWA_GROUPS):
            c0 = g * GROUP_WIDTH + h * HEAD_DIM
            o_ref[:, c0:c0 + HEAD_DIM] = outs[g][h] * (es[g] / tot)
    _sample_mem_attention(z_ref[:, 3 * MIXER_WIDTH:3 * MIXER_WIDTH + MEM_WIDTH], kv_ref, o_ref, MIXER_WIDTH)


def _sample_mix_b(z, win_caches, bias_groups, mem_kv):
    bd = mem_kv.shape[0]
    width = z.shape[1]
    kv_width = 2 * GROUP_WIDTH
    cache_views, cache_specs = [], []
    for g, (win, dil) in enumerate(SWA_PATTERN):
        cache_views.append(win_caches[g].reshape(bd, win // dil, dil * kv_width))
        cache_specs.append(pl.BlockSpec((None, N_BACK, kv_width), lambda i: (i, 0, 0)))
    bcol = jnp.stack([bg[:, N_BACK:0:-1].T for bg in bias_groups], axis=0)
    bnew = jnp.concatenate([bg[:, 0] for bg in bias_groups])
    bnew = jnp.pad(bnew, (0, HEAD_DIM - bnew.shape[0])).reshape(1, HEAD_DIM)
    return pl.pallas_call(
        _sample_mix_b_kernel,
        grid=(bd,),
        in_specs=[pl.BlockSpec((None, 1, width), lambda i: (i, 0, 0))] + cache_specs + [
            pl.BlockSpec((N_SWA_GROUPS, N_BACK, HEADS_PER_GROUP), lambda i: (0, 0, 0)),
            pl.BlockSpec((1, HEAD_DIM), lambda i: (0, 0)),
            pl.BlockSpec((None, N_MEM, 2 * MEM_WIDTH), lambda i: (i, 0, 0)),
        ],
        out_specs=pl.BlockSpec((None, 1, D_MODEL), lambda i: (i, 0, 0)),
        out_shape=jax.ShapeDtypeStruct((bd, 1, D_MODEL), F32),
        compiler_params=_params("parallel"),
        name="sample_mix_b",
    )(z[:bd].reshape(bd, 1, width), *cache_views, bcol, bnew, mem_kv)


def _t5_bucket(dist):
    nf = jnp.maximum(dist, MAX_EXACT).astype(F32)
    large = MAX_EXACT + (jnp.log(nf / MAX_EXACT) / math.log(MAX_DISTANCE / MAX_EXACT)
                         * (N_BUCKETS - MAX_EXACT)).astype(jnp.int32)
    large = jnp.minimum(large, N_BUCKETS - 1)
    return jnp.where(dist < MAX_EXACT, dist, large)


def _group_bias(rel_bias, g, dil):
    dist = jnp.arange(N_BACK + 1, dtype=jnp.int32) * dil
    b = rel_bias[_t5_bucket(dist)][:, g * HEADS_PER_GROUP:(g + 1) * HEADS_PER_GROUP]
    return b.T.astype(F32)


def _band_tables(bias_groups):
    width = 2 * N_BACK
    rows = []
    for bias_j in bias_groups:
        masked = jnp.full((HEADS_PER_GROUP, N_BACK - 1), NEG_INF, F32)
        rows.append(jnp.concatenate([bias_j[:, :1], masked, bias_j[:, N_BACK:0:-1]], axis=1))
        rows.append(jnp.concatenate([bias_j[:, ::-1], masked], axis=1))
    base = jnp.stack(rows, axis=0).reshape(-1, width)
    pitch = 2 * width - 1
    flat = jnp.tile(base, (1, width))[:, :N_BACK * pitch]
    tabs = flat.reshape(-1, N_BACK, pitch)[:, :, :width]
    return tabs.reshape(N_SWA_GROUPS, 2, HEADS_PER_GROUP, N_BACK, width)


def _tail_rows(a, win, dil):
    b = a.shape[0]
    n = win // dil
    tail = a[:, :, a.shape[2] - n:, :]
    return tail.transpose(0, 2, 1, 3).reshape(b, win, HEADS_PER_GROUP, HEAD_DIM).astype(F32)


def kernel(x_prompt, x_sample, mem_prompt, cache_mem_kv, cache_win128_kv, cache_win512_kv, cache_win2048_kv, rel_bias, norm_mix_pre, norm_mix_post, norm_ffn_pre, norm_ffn_post, norm_mem, w_mem_kv, w_in_a, norm_v_a, w_spatial_a, b_spatial_a, w_in_b, w_out, w_ffn_up, w_ffn_down):
    batch, seq, _ = x_prompt.shape
    bd = x_sample.shape[0]
    depth = w_out.shape[0]
    m_p = batch * seq
    win_caches = (cache_win128_kv, cache_win512_kv, cache_win2048_kv)

    w_mem_kv_b = w_mem_kv.astype(BF16)
    w_in_a_b = w_in_a.astype(BF16)
    w_in_b_b = w_in_b.astype(BF16)
    w_out_b = w_out.astype(BF16)
    w_up_b = w_ffn_up.astype(BF16)
    w_down_b = w_ffn_down.astype(BF16)

    bias_groups = [_group_bias(rel_bias, g, dil) for g, (_, dil) in enumerate(SWA_PATTERN)]
    band_tables = _band_tables(bias_groups)

    yp = x_prompt.reshape(m_p, D_MODEL)
    ys = jnp.pad(x_sample.reshape(bd, D_MODEL), ((0, SAMPLE_PAD - bd), (0, 0)))
    mem_rows = mem_prompt.reshape(batch * N_MEM, D_MODEL)

    mem_kv_p, chunk_v_s = [], []
    win_p = [[] for _ in SWA_PATTERN]
    win_s = [[] for _ in SWA_PATTERN]
    for i in range(depth):
        li = i // 2
        kv_p = _norm_matmul(mem_rows, norm_mem[i], w_mem_kv_b, i, tm=batch * N_MEM, tn=512,
                            gelu_cols=0, out_dtype=F32, name="mem_kv").reshape(batch, N_MEM, 2 * MEM_WIDTH)
        kv_s = cache_mem_kv[i].reshape(bd, N_MEM, 2 * MEM_WIDTH)
        mem_kv_p.append(kv_p.reshape(batch, N_MEM, 2, N_MEM_HEADS, HEAD_DIM))
        if i % 2 == 0:
            zp = _norm_matmul(yp, norm_mix_pre[i], w_in_a_b, li, tm=1024, tn=512,
                              gelu_cols=2 * MIXER_WIDTH, out_dtype=BF16, name="in_proj_a")
            zs = _norm_matmul(ys, norm_mix_pre[i], w_in_a_b, li, tm=SAMPLE_PAD, tn=512,
                              gelu_cols=2 * MIXER_WIDTH, out_dtype=F32, name="in_proj_a_s")
            mix_p = _gmlp_mix(zp, norm_v_a[li], w_spatial_a[li], b_spatial_a[li], kv_p,
                              tm=512, rows_per_batch=seq)
            mix_s, v_rows = _sample_mix_a(zs, norm_v_a[li], w_spatial_a[li], b_spatial_a[li], kv_s)
            chunk_v_s.append(v_rows)
        else:
            qkv, q_mem = _in_proj_b(yp, norm_mix_pre[i], w_in_b_b, li, batch=batch, seq=seq)
            zs = _norm_matmul(ys, norm_mix_pre[i], w_in_b_b, li, tm=SAMPLE_PAD, tn=512,
                              gelu_cols=0, out_dtype=F32, name="in_proj_b_s")
            outs, lses = [], []
            for g, (win, dil) in enumerate(SWA_PATTERN):
                q, k, v = qkv[g]
                o, lse = _swa_group(q, k, v, band_tables, g)
                outs.append(o)
                lses.append(lse)
                win_p[g].append(jnp.stack([_tail_rows(k, win, dil), _tail_rows(v, win, dil)], axis=2))
                kv_new = zs[:bd, MIXER_WIDTH:3 * MIXER_WIDTH]
                kv_new = kv_new.reshape(bd, 1, 2, N_SWA_GROUPS, HEADS_PER_GROUP, HEAD_DIM)[:, :, :, g]
                win_s[g].append(kv_new)
            mix_p = _swa_merge(outs, lses, q_mem, kv_p, rows_per_batch=seq)
            mix_s = _sample_mix_b(zs, [c[li] for c in win_caches], bias_groups, kv_s)
        mix_s = jnp.pad(mix_s.reshape(bd, D_MODEL), ((0, SAMPLE_PAD - bd), (0, 0))).astype(BF16)
        yp = _out_proj(mix_p, w_out_b, i, yp, norm_mix_post[i], tm=512)
        ys = _out_proj(mix_s, w_out_b, i, ys, norm_mix_post[i], tm=SAMPLE_PAD)
        yp = _ffn(yp, norm_ffn_pre[i], norm_ffn_post[i], w_up_b, w_down_b, i, tm=1024, tf=512,
                  single_buffer=True)
        ys = _ffn(ys, norm_ffn_pre[i], norm_ffn_post[i], w_up_b, w_down_b, i, tm=SAMPLE_PAD, tf=512,
                  single_buffer=False)

    return (
        yp.reshape(batch, seq, D_MODEL),
        ys[:bd].reshape(bd, 1, D_MODEL),
        jnp.stack(mem_kv_p, axis=0),
        jnp.stack(chunk_v_s, axis=0),
        jnp.stack(win_p[0], axis=0),
        jnp.stack(win_p[1], axis=0),
        jnp.stack(win_p[2], axis=0),
        jnp.stack(win_s[0], axis=0),
        jnp.stack(win_s[1], axis=0),
        jnp.stack(win_s[2], axis=0),
    )
```

```python
import functools
import math

import jax
import jax.numpy as jnp
from jax import lax
from jax.experimental import pallas as pl
from jax.experimental.pallas import tpu as pltpu

F32 = jnp.float32
BF16 = jnp.bfloat16

D_MODEL = 2048
HEAD_DIM = 128
N_MEM = 256
N_MEM_HEADS = 4
MEM_WIDTH = N_MEM_HEADS * HEAD_DIM
MIXER_WIDTH = D_MODEL - MEM_WIDTH
CHUNK = 128
N_GROUPS_A = 4
GROUP_DIM_A = MIXER_WIDTH // N_GROUPS_A
SWA_PATTERN = ((128, 1), (512, 4), (2048, 16))
N_SWA_GROUPS = len(SWA_PATTERN)
HEADS_PER_GROUP = 4
GROUP_WIDTH = HEADS_PER_GROUP * HEAD_DIM
N_BACK = 128
N_BUCKETS = 32
MAX_EXACT = N_BUCKETS // 2
MAX_DISTANCE = 2048
D_FF = 5632
EPS = 1e-6
NEG_INF = -1e30
ATTN_SCALE = HEAD_DIM ** -0.5
SAMPLE_PAD = 16
PERM_BLOCK = 256
MERGE_TILE = 512

VMEM_LIMIT = 56 * 1024 * 1024


def _params(*sem):
    return pltpu.CompilerParams(dimension_semantics=sem, vmem_limit_bytes=VMEM_LIMIT)


def _gelu(x):
    return 0.5 * x * (1.0 + jnp.tanh(0.7978845608028654 * (x + 0.044715 * (x * x * x))))


def _rms(x, g):
    return x * lax.rsqrt(jnp.mean(x * x, axis=-1, keepdims=True) + EPS) * g


def _log2(n):
    assert n & (n - 1) == 0
    return n.bit_length() - 1


def _residue_major_perm(tm, dil, transpose=False):
    n = tm // dil
    row = lax.broadcasted_iota(jnp.int32, (tm, tm), 0)
    col = lax.broadcasted_iota(jnp.int32, (tm, tm), 1)
    dst, src = (col, row) if transpose else (row, col)
    want = lax.shift_left(jnp.bitwise_and(dst, n - 1), _log2(dil)) + lax.shift_right_logical(dst, _log2(n))
    return (src == want).astype(BF16)


def _norm_matmul_kernel(x_ref, g_ref, w_ref, o_ref, xn_ref, *, n_gelu_tiles):
    n = pl.program_id(1)

    @pl.when(n == 0)
    def _():
        xn_ref[...] = _rms(x_ref[...], g_ref[...]).astype(BF16)

    acc = jnp.dot(xn_ref[...], w_ref[...], preferred_element_type=F32)
    if n_gelu_tiles > 0:
        acc = jnp.where(n < n_gelu_tiles, _gelu(acc), acc)
    o_ref[...] = acc.astype(o_ref.dtype)


def _norm_matmul(x, g, w, layer, *, tm, tn, gelu_cols, out_dtype, name):
    m, k = x.shape
    n = w.shape[2]
    return pl.pallas_call(
        functools.partial(_norm_matmul_kernel, n_gelu_tiles=gelu_cols // tn),
        grid=(m // tm, n // tn),
        in_specs=[
            pl.BlockSpec((tm, k), lambda i, j: (i, 0)),
            pl.BlockSpec((1, k), lambda i, j: (0, 0)),
            pl.BlockSpec((None, k, tn), lambda i, j: (layer, 0, j)),
        ],
        out_specs=pl.BlockSpec((tm, tn), lambda i, j: (i, j)),
        out_shape=jax.ShapeDtypeStruct((m, n), out_dtype),
        scratch_shapes=[pltpu.VMEM((tm, k), BF16)],
        compiler_params=_params("parallel", "arbitrary"),
        name=name,
    )(x, g.reshape(1, k), w)


def _in_proj_b_kernel(x_ref, g_ref, w_ref, o_ref, xn_ref):
    tm = x_ref.shape[0]
    j = pl.program_id(1)

    @pl.when(j == 0)
    def _():
        xn = _rms(x_ref[...], g_ref[...]).astype(BF16)
        xn_ref[0] = xn
        for g in range(1, N_SWA_GROUPS):
            perm = _residue_major_perm(PERM_BLOCK, SWA_PATTERN[g][1])
            for s in range(0, tm, PERM_BLOCK):
                xn_ref[g, s:s + PERM_BLOCK, :] = jnp.dot(
                    perm, xn[s:s + PERM_BLOCK, :], preferred_element_type=F32).astype(BF16)

    src = jnp.where(j < 3 * N_SWA_GROUPS, j % N_SWA_GROUPS, 0)
    o_ref[...] = jnp.dot(xn_ref[src], w_ref[...], preferred_element_type=F32).astype(o_ref.dtype)


def _in_proj_b(x, g, w, layer, *, tm):
    m, k = x.shape
    n_tiles = w.shape[2] // GROUP_WIDTH
    return pl.pallas_call(
        _in_proj_b_kernel,
        grid=(m // tm, n_tiles),
        in_specs=[
            pl.BlockSpec((tm, k), lambda i, j: (i, 0)),
            pl.BlockSpec((1, k), lambda i, j: (0, 0)),
            pl.BlockSpec((None, k, GROUP_WIDTH), lambda i, j: (layer, 0, j)),
        ],
        out_specs=pl.BlockSpec((None, tm, GROUP_WIDTH), lambda i, j: (j, i, 0)),
        out_shape=jax.ShapeDtypeStruct((n_tiles, m, GROUP_WIDTH), BF16),
        scratch_shapes=[pltpu.VMEM((N_SWA_GROUPS, tm, k), BF16)],
        compiler_params=_params("parallel", "arbitrary"),
        name="in_proj_b",
    )(x, g.reshape(1, k), w)


def _mem_attention_tile(q_ref, kv_ref, o_ref, col0):
    for h in range(N_MEM_HEADS):
        lo, hi = h * HEAD_DIM, (h + 1) * HEAD_DIM
        q = q_ref[:, lo:hi]
        k = kv_ref[:, lo:hi].astype(BF16)
        v = kv_ref[:, MEM_WIDTH + lo:MEM_WIDTH + hi].astype(BF16)
        s = lax.dot_general(q, k, (((1,), (1,)), ((), ())), preferred_element_type=F32) * ATTN_SCALE
        m = jnp.max(s, axis=1, keepdims=True)
        p = jnp.exp(s - m)
        den = jnp.sum(p, axis=1, keepdims=True)
        o = jnp.dot(p.astype(BF16), v, preferred_element_type=F32) / den
        o_ref[:, col0 + lo:col0 + hi] = o.astype(o_ref.dtype)


def _gmlp_mix_kernel(u_ref, v_ref, q_ref, gv_ref, ws_ref, bs_ref, kv_ref, o_ref, vn_ref):
    tm = u_ref.shape[0]
    v = v_ref[...].astype(F32)
    vn_ref[...] = _rms(v, gv_ref[...]).astype(BF16)
    row = lax.broadcasted_iota(jnp.int32, (CHUNK, CHUNK), 0)
    col = lax.broadcasted_iota(jnp.int32, (CHUNK, CHUNK), 1)
    causal = row >= col
    for g in range(N_GROUPS_A):
        w = jnp.where(causal, ws_ref[g], 0.0).astype(BF16)
        b = bs_ref[:, g:g + 1]
        c0, c1 = g * GROUP_DIM_A, (g + 1) * GROUP_DIM_A
        for c in range(tm // CHUNK):
            r0, r1 = c * CHUNK, (c + 1) * CHUNK
            s = jnp.dot(w, vn_ref[r0:r1, c0:c1], preferred_element_type=F32) + b
            o_ref[r0:r1, c0:c1] = (u_ref[r0:r1, c0:c1].astype(F32) * s).astype(o_ref.dtype)
    _mem_attention_tile(q_ref, kv_ref, o_ref, MIXER_WIDTH)


def _gmlp_mix(zact, g_v, w_s, b_s, mem_kv, *, tm, rows_per_batch):
    m = zact.shape[0]
    tiles_per_batch = rows_per_batch // tm
    return pl.pallas_call(
        _gmlp_mix_kernel,
        grid=(m // tm,),
        in_specs=[
            pl.BlockSpec((tm, MIXER_WIDTH), lambda i: (i, 0)),
            pl.BlockSpec((tm, MIXER_WIDTH), lambda i: (i, 1)),
            pl.BlockSpec((tm, MEM_WIDTH), lambda i: (i, 2 * MIXER_WIDTH // MEM_WIDTH)),
            pl.BlockSpec((1, MIXER_WIDTH), lambda i: (0, 0)),
            pl.BlockSpec((N_GROUPS_A, CHUNK, CHUNK), lambda i: (0, 0, 0)),
            pl.BlockSpec((CHUNK, N_GROUPS_A), lambda i: (0, 0)),
            pl.BlockSpec((None, N_MEM, 2 * MEM_WIDTH), lambda i: (i // tiles_per_batch, 0, 0)),
        ],
        out_specs=pl.BlockSpec((tm, D_MODEL), lambda i: (i, 0)),
        out_shape=jax.ShapeDtypeStruct((m, D_MODEL), BF16),
        scratch_shapes=[pltpu.VMEM((tm, MIXER_WIDTH), BF16)],
        compiler_params=_params("parallel"),
        name="gmlp_mix",
    )(zact, zact, zact, g_v.reshape(1, MIXER_WIDTH), w_s, b_s.T, mem_kv)


def _out_proj_kernel(mix_ref, w_ref, x_ref, g_ref, o_ref):
    o = jnp.dot(mix_ref[...], w_ref[...], preferred_element_type=F32)
    o_ref[...] = x_ref[...] + _rms(o, g_ref[...])


def _out_proj(mix, w, layer, x, g, *, tm):
    m = x.shape[0]
    return pl.pallas_call(
        _out_proj_kernel,
        grid=(m // tm,),
        in_specs=[
            pl.BlockSpec((tm, D_MODEL), lambda i: (i, 0)),
            pl.BlockSpec((None, D_MODEL, D_MODEL), lambda i: (layer, 0, 0)),
            pl.BlockSpec((tm, D_MODEL), lambda i: (i, 0)),
            pl.BlockSpec((1, D_MODEL), lambda i: (0, 0)),
        ],
        out_specs=pl.BlockSpec((tm, D_MODEL), lambda i: (i, 0)),
        out_shape=jax.ShapeDtypeStruct((m, D_MODEL), F32),
        compiler_params=_params("parallel"),
        name="out_proj",
    )(mix, w, x, g.reshape(1, D_MODEL))


def _ffn_kernel(x_ref, gpre_ref, gpost_ref, wg_ref, wl_ref, wd_ref, o_ref, xn_ref):
    f = pl.program_id(1)

    @pl.when(f == 0)
    def _():
        xn_ref[...] = _rms(x_ref[...], gpre_ref[...]).astype(BF16)
        o_ref[...] = jnp.zeros_like(o_ref)

    xn = xn_ref[...]
    hg = jnp.dot(xn, wg_ref[...], preferred_element_type=F32)
    hl = jnp.dot(xn, wl_ref[...], preferred_element_type=F32)
    a = (hg * jax.nn.sigmoid(hg) * hl).astype(BF16)
    o_ref[...] += jnp.dot(a, wd_ref[...], preferred_element_type=F32)

    @pl.when(f == pl.num_programs(1) - 1)
    def _():
        o_ref[...] = x_ref[...] + _rms(o_ref[...], gpost_ref[...])


def _ffn(x, g_pre, g_post, w_up, w_down, layer, *, tm, tf, single_buffer):
    m = x.shape[0]
    nf = D_FF // tf
    mode = dict(pipeline_mode=pl.Buffered(1)) if single_buffer else {}
    return pl.pallas_call(
        _ffn_kernel,
        grid=(m // tm, nf),
        in_specs=[
            pl.BlockSpec((tm, D_MODEL), lambda i, f: (i, 0), **mode),
            pl.BlockSpec((1, D_MODEL), lambda i, f: (0, 0)),
            pl.BlockSpec((1, D_MODEL), lambda i, f: (0, 0)),
            pl.BlockSpec((None, D_MODEL, tf), lambda i, f: (layer, 0, f)),
            pl.BlockSpec((None, D_MODEL, tf), lambda i, f: (layer, 0, nf + f)),
            pl.BlockSpec((None, tf, D_MODEL), lambda i, f: (layer, f, 0)),
        ],
        out_specs=pl.BlockSpec((tm, D_MODEL), lambda i, f: (i, 0), **mode),
        out_shape=jax.ShapeDtypeStruct((m, D_MODEL), F32),
        scratch_shapes=[pltpu.VMEM((tm, D_MODEL), BF16)],
        compiler_params=_params("parallel", "arbitrary"),
        name="ffn",
    )(x, g_pre.reshape(1, D_MODEL), g_post.reshape(1, D_MODEL), w_up, w_up, w_down)


def _swa_kernel(q_ref, k_ref, v_ref, tb_ref, o_ref, lse_ref):
    n_units, u, _ = q_ref.shape
    per_blk = N_BACK // u
    lane = lax.broadcasted_iota(jnp.int32, (N_BACK, HEAD_DIM), 1)

    def rows(ref, unit0, n_rows, lo, hi):
        return ref[pl.ds(unit0, n_rows // u), :, lo:hi].reshape(n_rows, hi - lo)

    def block(qu, ku, table):
        n_keys = N_BACK if table == 0 else 2 * N_BACK
        lse_tile = jnp.zeros((N_BACK, HEAD_DIM), F32)
        for h in range(HEADS_PER_GROUP):
            lo, hi = h * HEAD_DIM, (h + 1) * HEAD_DIM
            q = rows(q_ref, qu, N_BACK, lo, hi)
            kw = rows(k_ref, ku, n_keys, lo, hi)
            vw = rows(v_ref, ku, n_keys, lo, hi)
            s = lax.dot_general(q, kw, (((1,), (1,)), ((), ())), preferred_element_type=F32)
            s = s * ATTN_SCALE + tb_ref[table, h][:, :n_keys]
            m = jnp.max(s, axis=1, keepdims=True)
            p = jnp.exp(s - m)
            den = jnp.sum(p, axis=1, keepdims=True)
            o = jnp.dot(p.astype(BF16), vw, preferred_element_type=F32) / den
            o_ref[pl.ds(qu, per_blk), :, lo:hi] = o.reshape(per_blk, u, HEAD_DIM).astype(o_ref.dtype)
            lse_tile = jnp.where(lane == h, m + jnp.log(den), lse_tile)
        lse_ref[pl.ds(qu, per_blk), :, :] = lse_tile.reshape(per_blk, u, HEAD_DIM)

    block(0, 0, 0)

    def body(n, carry):
        block(n * per_blk, (n - 1) * per_blk, 1)
        return carry

    lax.fori_loop(1, n_units // per_blk, body, 0)


def _sub_block(dil):
    return N_BACK if dil == 1 else PERM_BLOCK


def _swa_group(zb, tables, g, *, batch, seq):
    dil = SWA_PATTERN[g][1]
    sub = _sub_block(dil)
    n_units, u = seq // sub, sub // dil
    view = zb.reshape(zb.shape[0], batch, n_units, dil, u, GROUP_WIDTH)

    def rows_in(tile):
        return pl.BlockSpec((None, None, n_units, None, u, GROUP_WIDTH), lambda i, r: (tile, i, 0, r, 0, 0))

    def rows_out(width):
        return pl.BlockSpec((None, n_units, None, u, width), lambda i, r: (i, 0, r, 0, 0))

    return pl.pallas_call(
        _swa_kernel,
        grid=(batch, dil),
        in_specs=[rows_in(g), rows_in(N_SWA_GROUPS + g), rows_in(2 * N_SWA_GROUPS + g),
                  pl.BlockSpec((None, 2, HEADS_PER_GROUP, N_BACK, 2 * N_BACK), lambda i, r: (g, 0, 0, 0, 0))],
        out_specs=[rows_out(GROUP_WIDTH), rows_out(HEAD_DIM)],
        out_shape=[
            jax.ShapeDtypeStruct((batch, n_units, dil, u, GROUP_WIDTH), BF16),
            jax.ShapeDtypeStruct((batch, n_units, dil, u, HEAD_DIM), F32),
        ],
        compiler_params=_params("parallel", "parallel"),
        name=f"swa_group{g}",
    )(view, view, view, tables)


def _split3(x):
    hi = x.astype(BF16)
    rest = x - hi.astype(F32)
    mid = rest.astype(BF16)
    lo = (rest - mid.astype(F32)).astype(BF16)
    return hi, mid, lo


def _swa_merge_kernel(o0_ref, o1_ref, o2_ref, l0_ref, l1_ref, l2_ref, q_ref, kv_ref, o_ref):
    tm = o_ref.shape[0]
    outs, lses = [], []
    for g, (o_g, l_g) in enumerate(((o0_ref, l0_ref), (o1_ref, l1_ref), (o2_ref, l2_ref))):
        dil = SWA_PATTERN[g][1]
        o = o_g[...].reshape(tm, GROUP_WIDTH)
        l = l_g[...].reshape(tm, HEAD_DIM)
        if dil > 1:
            inv = _residue_major_perm(PERM_BLOCK, dil, transpose=True)
            l3 = _split3(l)
            o_nat, l_nat = [], []
            for s in range(0, tm, PERM_BLOCK):
                o_nat.append(jnp.dot(inv, o[s:s + PERM_BLOCK, :], preferred_element_type=F32))
                l_nat.append(sum(jnp.dot(inv, t[s:s + PERM_BLOCK, :], preferred_element_type=F32) for t in l3))
            o = jnp.concatenate(o_nat, axis=0)
            l = jnp.concatenate(l_nat, axis=0)
        outs.append(o.astype(F32))
        lses.append(l)
    for h in range(HEADS_PER_GROUP):
        lo, hi = h * HEAD_DIM, (h + 1) * HEAD_DIM
        ls = [l[:, h:h + 1] for l in lses]
        mx = jnp.maximum(jnp.maximum(ls[0], ls[1]), ls[2])
        es = [jnp.exp(l - mx) for l in ls]
        tot = es[0] + es[1] + es[2]
        for g in range(N_SWA_GROUPS):
            alpha = es[g] / tot
            o_ref[:, g * GROUP_WIDTH + lo:g * GROUP_WIDTH + hi] = (outs[g][:, lo:hi] * alpha).astype(o_ref.dtype)
    _mem_attention_tile(q_ref, kv_ref, o_ref, MIXER_WIDTH)


def _swa_merge(outs, lses, zb, mem_kv, *, rows_per_batch):
    m = zb.shape[1]
    tm = MERGE_TILE
    tiles_per_batch = rows_per_batch // tm

    def tile(width, dil):
        sub = _sub_block(dil)
        return pl.BlockSpec((None, tm // sub, dil, sub // dil, width),
                            lambda i: (i // tiles_per_batch, i % tiles_per_batch, 0, 0, 0))

    return pl.pallas_call(
        _swa_merge_kernel,
        grid=(m // tm,),
        in_specs=(
            [tile(GROUP_WIDTH, dil) for _, dil in SWA_PATTERN]
            + [tile(HEAD_DIM, dil) for _, dil in SWA_PATTERN]
            + [pl.BlockSpec((None, tm, MEM_WIDTH), lambda i: (3 * N_SWA_GROUPS, i, 0)),
               pl.BlockSpec((None, N_MEM, 2 * MEM_WIDTH), lambda i: (i // tiles_per_batch, 0, 0))]
        ),
        out_specs=pl.BlockSpec((tm, D_MODEL), lambda i: (i, 0)),
        out_shape=jax.ShapeDtypeStruct((m, D_MODEL), BF16),
        compiler_params=_params("parallel"),
        name="swa_merge",
    )(*outs, *lses, zb, mem_kv)


def _sample_mem_attention(q_row, kv_ref, o_ref, col0):
    for h in range(N_MEM_HEADS):
        lo, hi = h * HEAD_DIM, (h + 1) * HEAD_DIM
        q = q_row[:, lo:hi]
        k = kv_ref[:, lo:hi]
        v = kv_ref[:, MEM_WIDTH + lo:MEM_WIDTH + hi]
        s = jnp.sum(k * q, axis=1, keepdims=True) * ATTN_SCALE
        m = jnp.max(s, axis=0, keepdims=True)
        p = jnp.exp(s - m)
        den = jnp.sum(p, axis=0, keepdims=True)
        o_ref[:, col0 + lo:col0 + hi] = jnp.sum(p * v, axis=0, keepdims=True) / den


def _sample_mix_a_kernel(z_ref, gv_ref, w0_ref, b0_ref, kv_ref, o_ref, vrow_ref):
    u = z_ref[:, 0:MIXER_WIDTH]
    v = _rms(z_ref[:, MIXER_WIDTH:2 * MIXER_WIDTH], gv_ref[...])
    vrow_ref[...] = v
    o_ref[:, 0:MIXER_WIDTH] = u * (w0_ref[...] * v + b0_ref[...])
    _sample_mem_attention(z_ref[:, 2 * MIXER_WIDTH:2 * MIXER_WIDTH + MEM_WIDTH], kv_ref, o_ref, MIXER_WIDTH)


def _sample_mix_a(z, g_v, w_s, b_s, mem_kv):
    bd = mem_kv.shape[0]
    w0 = jnp.repeat(w_s[:, 0, 0], GROUP_DIM_A).reshape(1, MIXER_WIDTH)
    b0 = jnp.repeat(b_s[:, 0], GROUP_DIM_A).reshape(1, MIXER_WIDTH)
    width = z.shape[1]
    vec = lambda i: (0, 0)
    return pl.pallas_call(
        _sample_mix_a_kernel,
        grid=(bd,),
        in_specs=[
            pl.BlockSpec((None, 1, width), lambda i: (i, 0, 0)),
            pl.BlockSpec((1, MIXER_WIDTH), vec),
            pl.BlockSpec((1, MIXER_WIDTH), vec),
            pl.BlockSpec((1, MIXER_WIDTH), vec),
            pl.BlockSpec((None, N_MEM, 2 * MEM_WIDTH), lambda i: (i, 0, 0)),
        ],
        out_specs=[
            pl.BlockSpec((None, 1, D_MODEL), lambda i: (i, 0, 0)),
            pl.BlockSpec((None, 1, MIXER_WIDTH), lambda i: (i, 0, 0)),
        ],
        out_shape=[
            jax.ShapeDtypeStruct((bd, 1, D_MODEL), F32),
            jax.ShapeDtypeStruct((bd, 1, MIXER_WIDTH), F32),
        ],
        compiler_params=_params("parallel"),
        name="sample_mix_a",
    )(z[:bd].reshape(bd, 1, width), g_v.reshape(1, MIXER_WIDTH), w0, b0, mem_kv)


def _sample_mix_b_kernel(z_ref, c0_ref, c1_ref, c2_ref, bcol_ref, bnew_ref, kv_ref, o_ref):
    caches = (c0_ref, c1_ref, c2_ref)
    outs = [[None] * HEADS_PER_GROUP for _ in range(N_SWA_GROUPS)]
    lses = [[None] * HEADS_PER_GROUP for _ in range(N_SWA_GROUPS)]
    for g in range(N_SWA_GROUPS):
        for h in range(HEADS_PER_GROUP):
            hd = g * HEADS_PER_GROUP + h
            lo, hi = h * HEAD_DIM, (h + 1) * HEAD_DIM
            q = z_ref[:, hd * HEAD_DIM:(hd + 1) * HEAD_DIM]
            k_new = z_ref[:, MIXER_WIDTH + hd * HEAD_DIM:MIXER_WIDTH + (hd + 1) * HEAD_DIM]
            v_new = z_ref[:, 2 * MIXER_WIDTH + hd * HEAD_DIM:2 * MIXER_WIDTH + (hd + 1) * HEAD_DIM]
            kc = caches[g][:, lo:hi]
            vc = caches[g][:, GROUP_WIDTH + lo:GROUP_WIDTH + hi]
            s_c = jnp.sum(kc * q, axis=1, keepdims=True) * ATTN_SCALE + bcol_ref[g][:, h:h + 1]
            s_n = jnp.sum(k_new * q, axis=1, keepdims=True) * ATTN_SCALE + bnew_ref[:, hd:hd + 1]
            m = jnp.maximum(jnp.max(s_c, axis=0, keepdims=True), s_n)
            p_c = jnp.exp(s_c - m)
            p_n = jnp.exp(s_n - m)
            den = jnp.sum(p_c, axis=0, keepdims=True) + p_n
            outs[g][h] = (jnp.sum(p_c * vc, axis=0, keepdims=True) + p_n * v_new) / den
            lses[g][h] = m + jnp.log(den)
    for h in range(HEADS_PER_GROUP):
        ls = [lses[g][h] for g in range(N_SWA_GROUPS)]
        mx = jnp.maximum(jnp.maximum(ls[0], ls[1]), ls[2])
        es = [jnp.exp(l - mx) for l in ls]
        tot = es[0] + es[1] + es[2]
        for g in range(N_SWA_GROUPS):
            c0 = g * GROUP_WIDTH + h * HEAD_DIM
            o_ref[:, c0:c0 + HEAD_DIM] = outs[g][h] * (es[g] / tot)
    _sample_mem_attention(z_ref[:, 3 * MIXER_WIDTH:3 * MIXER_WIDTH + MEM_WIDTH], kv_ref, o_ref, MIXER_WIDTH)


def _sample_mix_b(z, win_caches, bias_groups, mem_kv):
    bd = mem_kv.shape[0]
    width = z.shape[1]
    kv_width = 2 * GROUP_WIDTH
    cache_views, cache_specs = [], []
    for g, (win, dil) in enumerate(SWA_PATTERN):
        cache_views.append(win_caches[g][:, ::dil].reshape(bd, win // dil, kv_width))
        cache_specs.append(pl.BlockSpec((None, N_BACK, kv_width), lambda i: (i, 0, 0)))
    bcol = jnp.stack([bg[:, N_BACK:0:-1].T for bg in bias_groups], axis=0)
    bnew = jnp.concatenate([bg[:, 0] for bg in bias_groups])
    bnew = jnp.pad(bnew, (0, HEAD_DIM - bnew.shape[0])).reshape(1, HEAD_DIM)
    return pl.pallas_call(
        _sample_mix_b_kernel,
        grid=(bd,),
        in_specs=[pl.BlockSpec((None, 1, width), lambda i: (i, 0, 0))] + cache_specs + [
            pl.BlockSpec((N_SWA_GROUPS, N_BACK, HEADS_PER_GROUP), lambda i: (0, 0, 0)),
            pl.BlockSpec((1, HEAD_DIM), lambda i: (0, 0)),
            pl.BlockSpec((None, N_MEM, 2 * MEM_WIDTH), lambda i: (i, 0, 0)),
        ],
        out_specs=pl.BlockSpec((None, 1, D_MODEL), lambda i: (i, 0, 0)),
        out_shape=jax.ShapeDtypeStruct((bd, 1, D_MODEL), F32),
        compiler_params=_params("parallel"),
        name="sample_mix_b",
    )(z[:bd].reshape(bd, 1, width), *cache_views, bcol, bnew, mem_kv)


def _t5_bucket(dist):
    nf = jnp.maximum(dist, MAX_EXACT).astype(F32)
    large = MAX_EXACT + (jnp.log(nf / MAX_EXACT) / math.log(MAX_DISTANCE / MAX_EXACT)
                         * (N_BUCKETS - MAX_EXACT)).astype(jnp.int32)
    large = jnp.minimum(large, N_BUCKETS - 1)
    return jnp.where(dist < MAX_EXACT, dist, large)


def _group_bias(rel_bias, g, dil):
    dist = jnp.arange(N_BACK + 1, dtype=jnp.int32) * dil
    b = rel_bias[_t5_bucket(dist)][:, g * HEADS_PER_GROUP:(g + 1) * HEADS_PER_GROUP]
    return b.T.astype(F32)


def _band_tables(bias_groups):
    width = 2 * N_BACK
    rows = []
    for bias_j in bias_groups:
        masked = jnp.full((HEADS_PER_GROUP, N_BACK - 1), NEG_INF, F32)
        rows.append(jnp.concatenate([bias_j[:, :1], masked, bias_j[:, N_BACK:0:-1]], axis=1))
        rows.append(jnp.concatenate([bias_j[:, ::-1], masked], axis=1))
    base = jnp.stack(rows, axis=0).reshape(-1, width)
    pitch = 2 * width - 1
    flat = jnp.tile(base, (1, width))[:, :N_BACK * pitch]
    tabs = flat.reshape(-1, N_BACK, pitch)[:, :, :width]
    return tabs.reshape(N_SWA_GROUPS, 2, HEADS_PER_GROUP, N_BACK, width)


def _tail_rows(rows, win, dil, *, batch, seq):
    sub = _sub_block(dil)
    a = rows.reshape(batch, seq // sub, dil, sub // dil, GROUP_WIDTH)[:, (seq - win) // sub:]
    return a.transpose(0, 1, 3, 2, 4).reshape(batch, win, HEADS_PER_GROUP, HEAD_DIM).astype(F32)


def kernel(x_prompt, x_sample, mem_prompt, cache_mem_kv, cache_win128_kv, cache_win512_kv, cache_win2048_kv, rel_bias, norm_mix_pre, norm_mix_post, norm_ffn_pre, norm_ffn_post, norm_mem, w_mem_kv, w_in_a, norm_v_a, w_spatial_a, b_spatial_a, w_in_b, w_out, w_ffn_up, w_ffn_down):
    batch, seq, _ = x_prompt.shape
    bd = x_sample.shape[0]
    depth = w_out.shape[0]
    m_p = batch * seq
    win_caches = (cache_win128_kv, cache_win512_kv, cache_win2048_kv)

    w_mem_kv_b = w_mem_kv.astype(BF16)
    w_in_a_b = w_in_a.astype(BF16)
    w_in_b_b = w_in_b.astype(BF16)
    w_out_b = w_out.astype(BF16)
    w_up_b = w_ffn_up.astype(BF16)
    w_down_b = w_ffn_down.astype(BF16)

    bias_groups = [_group_bias(rel_bias, g, dil) for g, (_, dil) in enumerate(SWA_PATTERN)]
    band_tables = _band_tables(bias_groups)

    yp = x_prompt.reshape(m_p, D_MODEL)
    ys = jnp.pad(x_sample.reshape(bd, D_MODEL), ((0, SAMPLE_PAD - bd), (0, 0)))
    mem_rows = mem_prompt.reshape(batch * N_MEM, D_MODEL)

    mem_kv_p, chunk_v_s = [], []
    win_p = [[] for _ in SWA_PATTERN]
    win_s = [[] for _ in SWA_PATTERN]
    for i in range(depth):
        li = i // 2
        kv_p = _norm_matmul(mem_rows, norm_mem[i], w_mem_kv_b, i, tm=batch * N_MEM, tn=512,
                            gelu_cols=0, out_dtype=F32, name="mem_kv").reshape(batch, N_MEM, 2 * MEM_WIDTH)
        kv_s = cache_mem_kv[i].reshape(bd, N_MEM, 2 * MEM_WIDTH)
        mem_kv_p.append(kv_p.reshape(batch, N_MEM, 2, N_MEM_HEADS, HEAD_DIM))
        if i % 2 == 0:
            zp = _norm_matmul(yp, norm_mix_pre[i], w_in_a_b, li, tm=1024, tn=512,
                              gelu_cols=2 * MIXER_WIDTH, out_dtype=BF16, name="in_proj_a")
            zs = _norm_matmul(ys, norm_mix_pre[i], w_in_a_b, li, tm=SAMPLE_PAD, tn=512,
                              gelu_cols=2 * MIXER_WIDTH, out_dtype=F32, name="in_proj_a_s")
            mix_p = _gmlp_mix(zp, norm_v_a[li], w_spatial_a[li], b_spatial_a[li], kv_p,
                              tm=512, rows_per_batch=seq)
            mix_s, v_rows = _sample_mix_a(zs, norm_v_a[li], w_spatial_a[li], b_spatial_a[li], kv_s)
            chunk_v_s.append(v_rows)
        else:
            zb = _in_proj_b(yp, norm_mix_pre[i], w_in_b_b, li, tm=1024)
            zs = _norm_matmul(ys, norm_mix_pre[i], w_in_b_b, li, tm=SAMPLE_PAD, tn=512,
                              gelu_cols=0, out_dtype=F32, name="in_proj_b_s")
            outs, lses = [], []
            for g, (win, dil) in enumerate(SWA_PATTERN):
                o, lse = _swa_group(zb, band_tables, g, batch=batch, seq=seq)
                outs.append(o)
                lses.append(lse)
                k_tail = _tail_rows(zb[N_SWA_GROUPS + g], win, dil, batch=batch, seq=seq)
                v_tail = _tail_rows(zb[2 * N_SWA_GROUPS + g], win, dil, batch=batch, seq=seq)
                win_p[g].append(jnp.stack([k_tail, v_tail], axis=2))
                kv_new = zs[:bd, MIXER_WIDTH:3 * MIXER_WIDTH]
                kv_new = kv_new.reshape(bd, 1, 2, N_SWA_GROUPS, HEADS_PER_GROUP, HEAD_DIM)[:, :, :, g]
                win_s[g].append(kv_new)
            mix_p = _swa_merge(outs, lses, zb, kv_p, rows_per_batch=seq)
            mix_s = _sample_mix_b(zs, [c[li] for c in win_caches], bias_groups, kv_s)
        mix_s = jnp.pad(mix_s.reshape(bd, D_MODEL), ((0, SAMPLE_PAD - bd), (0, 0))).astype(BF16)
        yp = _out_proj(mix_p, w_out_b, i, yp, norm_mix_post[i], tm=512)
        ys = _out_proj(mix_s, w_out_b, i, ys, norm_mix_post[i], tm=SAMPLE_PAD)
        yp = _ffn(yp, norm_ffn_pre[i], norm_ffn_post[i], w_up_b, w_down_b, i, tm=1024, tf=512,
                  single_buffer=True)
        ys = _ffn(ys, norm_ffn_pre[i], norm_ffn_post[i], w_up_b, w_down_b, i, tm=SAMPLE_PAD, tf=512,
                  single_buffer=False)

    return (
        yp.reshape(batch, seq, D_MODEL),
        ys[:bd].reshape(bd, 1, D_MODEL),
        jnp.stack(mem_kv_p, axis=0),
        jnp.stack(chunk_v_s, axis=0),
        jnp.stack(win_p[0], axis=0),
        jnp.stack(win_p[1], axis=0),
        jnp.stack(win_p[2], axis=0),
        jnp.stack(win_s[0], axis=0),
        jnp.stack(win_s[1], axis=0),
        jnp.stack(win_s[2], axis=0),
    )
```

```python
import functools
import math

import jax
import jax.numpy as jnp
from jax import lax
from jax.experimental import pallas as pl
from jax.experimental.pallas import tpu as pltpu

F32 = jnp.float32
BF16 = jnp.bfloat16

D_MODEL = 2048
HEAD_DIM = 128
N_MEM = 256
N_MEM_HEADS = 4
MEM_WIDTH = N_MEM_HEADS * HEAD_DIM
MIXER_WIDTH = D_MODEL - MEM_WIDTH
CHUNK = 128
N_GROUPS_A = 4
GROUP_DIM_A = MIXER_WIDTH // N_GROUPS_A
SWA_PATTERN = ((128, 1), (512, 4), (2048, 16))
N_SWA_GROUPS = len(SWA_PATTERN)
HEADS_PER_GROUP = 4
GROUP_WIDTH = HEADS_PER_GROUP * HEAD_DIM
N_BACK = 128
N_BUCKETS = 32
MAX_EXACT = N_BUCKETS // 2
MAX_DISTANCE = 2048
D_FF = 5632
EPS = 1e-6
NEG_INF = -1e30
ATTN_SCALE = HEAD_DIM ** -0.5
SAMPLE_PAD = 16
PERM_BLOCK = 256
MERGE_TILE = 512

VMEM_LIMIT = 56 * 1024 * 1024


def _params(*sem):
    return pltpu.CompilerParams(dimension_semantics=sem, vmem_limit_bytes=VMEM_LIMIT)


def _gelu(x):
    return 0.5 * x * (1.0 + jnp.tanh(0.7978845608028654 * (x + 0.044715 * (x * x * x))))


def _rms(x, g):
    return x * lax.rsqrt(jnp.mean(x * x, axis=-1, keepdims=True) + EPS) * g


def _log2(n):
    assert n & (n - 1) == 0
    return n.bit_length() - 1


def _residue_major_perm(tm, dil, transpose=False):
    n = tm // dil
    row = lax.broadcasted_iota(jnp.int32, (tm, tm), 0)
    col = lax.broadcasted_iota(jnp.int32, (tm, tm), 1)
    dst, src = (col, row) if transpose else (row, col)
    want = lax.shift_left(jnp.bitwise_and(dst, n - 1), _log2(dil)) + lax.shift_right_logical(dst, _log2(n))
    return (src == want).astype(BF16)


def _norm_matmul_kernel(x_ref, g_ref, w_ref, o_ref, xn_ref, *, n_gelu_tiles):
    n = pl.program_id(1)

    @pl.when(n == 0)
    def _():
        xn_ref[...] = _rms(x_ref[...], g_ref[...]).astype(BF16)

    acc = jnp.dot(xn_ref[...], w_ref[...], preferred_element_type=F32)
    if n_gelu_tiles > 0:
        acc = jnp.where(n < n_gelu_tiles, _gelu(acc), acc)
    o_ref[...] = acc.astype(o_ref.dtype)


def _norm_matmul(x, g, w, layer, *, tm, tn, gelu_cols, out_dtype, name):
    m, k = x.shape
    n = w.shape[2]
    return pl.pallas_call(
        functools.partial(_norm_matmul_kernel, n_gelu_tiles=gelu_cols // tn),
        grid=(m // tm, n // tn),
        in_specs=[
            pl.BlockSpec((tm, k), lambda i, j: (i, 0)),
            pl.BlockSpec((1, k), lambda i, j: (0, 0)),
            pl.BlockSpec((None, k, tn), lambda i, j: (layer, 0, j)),
        ],
        out_specs=pl.BlockSpec((tm, tn), lambda i, j: (i, j)),
        out_shape=jax.ShapeDtypeStruct((m, n), out_dtype),
        scratch_shapes=[pltpu.VMEM((tm, k), BF16)],
        compiler_params=_params("parallel", "arbitrary"),
        name=name,
    )(x, g.reshape(1, k), w)


def _in_proj_b_kernel(x_ref, g_ref, w_ref, o_ref, xn_ref):
    tm = x_ref.shape[0]
    j = pl.program_id(1)

    @pl.when(j == 0)
    def _():
        xn = _rms(x_ref[...], g_ref[...]).astype(BF16)
        xn_ref[0] = xn
        for g in range(1, N_SWA_GROUPS):
            perm = _residue_major_perm(PERM_BLOCK, SWA_PATTERN[g][1])
            for s in range(0, tm, PERM_BLOCK):
                xn_ref[g, s:s + PERM_BLOCK, :] = jnp.dot(
                    perm, xn[s:s + PERM_BLOCK, :], preferred_element_type=F32).astype(BF16)

    src = jnp.where(j < 3 * N_SWA_GROUPS, j % N_SWA_GROUPS, 0)
    o_ref[...] = jnp.dot(xn_ref[src], w_ref[...], preferred_element_type=F32).astype(o_ref.dtype)


def _in_proj_b(x, g, w, layer, *, tm):
    m, k = x.shape
    n_tiles = w.shape[2] // GROUP_WIDTH
    return pl.pallas_call(
        _in_proj_b_kernel,
        grid=(m // tm, n_tiles),
        in_specs=[
            pl.BlockSpec((tm, k), lambda i, j: (i, 0)),
            pl.BlockSpec((1, k), lambda i, j: (0, 0)),
            pl.BlockSpec((None, k, GROUP_WIDTH), lambda i, j: (layer, 0, j)),
        ],
        out_specs=pl.BlockSpec((None, tm, GROUP_WIDTH), lambda i, j: (j, i, 0)),
        out_shape=jax.ShapeDtypeStruct((n_tiles, m, GROUP_WIDTH), BF16),
        scratch_shapes=[pltpu.VMEM((N_SWA_GROUPS, tm, k), BF16)],
        compiler_params=_params("parallel", "arbitrary"),
        name="in_proj_b",
    )(x, g.reshape(1, k), w)


def _mem_attention_tile(q_ref, kv_ref, o_ref, col0):
    ones = jnp.ones((N_MEM, HEAD_DIM), BF16)
    for h in range(N_MEM_HEADS):
        lo, hi = h * HEAD_DIM, (h + 1) * HEAD_DIM
        q = q_ref[:, lo:hi]
        k = kv_ref[:, lo:hi].astype(BF16)
        v = kv_ref[:, MEM_WIDTH + lo:MEM_WIDTH + hi].astype(BF16)
        s = lax.dot_general(q, k, (((1,), (1,)), ((), ())), preferred_element_type=F32) * ATTN_SCALE
        m = jnp.max(s, axis=1, keepdims=True)
        p = jnp.exp(s - m).astype(BF16)
        ov = jnp.dot(p, jnp.concatenate([v, ones], axis=1), preferred_element_type=F32)
        o_ref[:, col0 + lo:col0 + hi] = (ov[:, :HEAD_DIM] / ov[:, HEAD_DIM:]).astype(o_ref.dtype)


def _gmlp_mix_kernel(u_ref, v_ref, q_ref, gv_ref, ws_ref, bs_ref, kv_ref, o_ref, vn_ref):
    tm = u_ref.shape[0]
    v = v_ref[...].astype(F32)
    vn_ref[...] = _rms(v, gv_ref[...]).astype(BF16)
    row = lax.broadcasted_iota(jnp.int32, (CHUNK, CHUNK), 0)
    col = lax.broadcasted_iota(jnp.int32, (CHUNK, CHUNK), 1)
    causal = row >= col
    for g in range(N_GROUPS_A):
        w = jnp.where(causal, ws_ref[g], 0.0).astype(BF16)
        b = bs_ref[:, g:g + 1]
        c0, c1 = g * GROUP_DIM_A, (g + 1) * GROUP_DIM_A
        for c in range(tm // CHUNK):
            r0, r1 = c * CHUNK, (c + 1) * CHUNK
            s = jnp.dot(w, vn_ref[r0:r1, c0:c1], preferred_element_type=F32) + b
            o_ref[r0:r1, c0:c1] = (u_ref[r0:r1, c0:c1].astype(F32) * s).astype(o_ref.dtype)
    _mem_attention_tile(q_ref, kv_ref, o_ref, MIXER_WIDTH)


def _gmlp_mix(zact, g_v, w_s, b_s, mem_kv, *, tm, rows_per_batch):
    m = zact.shape[0]
    tiles_per_batch = rows_per_batch // tm
    return pl.pallas_call(
        _gmlp_mix_kernel,
        grid=(m // tm,),
        in_specs=[
            pl.BlockSpec((tm, MIXER_WIDTH), lambda i: (i, 0)),
            pl.BlockSpec((tm, MIXER_WIDTH), lambda i: (i, 1)),
            pl.BlockSpec((tm, MEM_WIDTH), lambda i: (i, 2 * MIXER_WIDTH // MEM_WIDTH)),
            pl.BlockSpec((1, MIXER_WIDTH), lambda i: (0, 0)),
            pl.BlockSpec((N_GROUPS_A, CHUNK, CHUNK), lambda i: (0, 0, 0)),
            pl.BlockSpec((CHUNK, N_GROUPS_A), lambda i: (0, 0)),
            pl.BlockSpec((None, N_MEM, 2 * MEM_WIDTH), lambda i: (i // tiles_per_batch, 0, 0)),
        ],
        out_specs=pl.BlockSpec((tm, D_MODEL), lambda i: (i, 0)),
        out_shape=jax.ShapeDtypeStruct((m, D_MODEL), BF16),
        scratch_shapes=[pltpu.VMEM((tm, MIXER_WIDTH), BF16)],
        compiler_params=_params("parallel"),
        name="gmlp_mix",
    )(zact, zact, zact, g_v.reshape(1, MIXER_WIDTH), w_s, b_s.T, mem_kv)


def _out_proj_kernel(mix_ref, w_ref, x_ref, g_ref, o_ref):
    o = jnp.dot(mix_ref[...], w_ref[...], preferred_element_type=F32)
    o_ref[...] = x_ref[...] + _rms(o, g_ref[...])


def _out_proj(mix, w, layer, x, g, *, tm):
    m = x.shape[0]
    return pl.pallas_call(
        _out_proj_kernel,
        grid=(m // tm,),
        in_specs=[
            pl.BlockSpec((tm, D_MODEL), lambda i: (i, 0)),
            pl.BlockSpec((None, D_MODEL, D_MODEL), lambda i: (layer, 0, 0)),
            pl.BlockSpec((tm, D_MODEL), lambda i: (i, 0)),
            pl.BlockSpec((1, D_MODEL), lambda i: (0, 0)),
        ],
        out_specs=pl.BlockSpec((tm, D_MODEL), lambda i: (i, 0)),
        out_shape=jax.ShapeDtypeStruct((m, D_MODEL), F32),
        compiler_params=_params("parallel"),
        name="out_proj",
    )(mix, w, x, g.reshape(1, D_MODEL))


def _ffn_kernel(x_ref, gpre_ref, gpost_ref, wg_ref, wl_ref, wd_ref, o_ref, xn_ref):
    f = pl.program_id(1)

    @pl.when(f == 0)
    def _():
        xn_ref[...] = _rms(x_ref[...], gpre_ref[...]).astype(BF16)
        o_ref[...] = jnp.zeros_like(o_ref)

    xn = xn_ref[...]
    hg = jnp.dot(xn, wg_ref[...], preferred_element_type=F32)
    hl = jnp.dot(xn, wl_ref[...], preferred_element_type=F32)
    a = (hg * jax.nn.sigmoid(hg) * hl).astype(BF16)
    o_ref[...] += jnp.dot(a, wd_ref[...], preferred_element_type=F32)

    @pl.when(f == pl.num_programs(1) - 1)
    def _():
        o_ref[...] = x_ref[...] + _rms(o_ref[...], gpost_ref[...])


def _ffn(x, g_pre, g_post, w_up, w_down, layer, *, tm, tf, single_buffer):
    m = x.shape[0]
    nf = D_FF // tf
    mode = dict(pipeline_mode=pl.Buffered(1)) if single_buffer else {}
    return pl.pallas_call(
        _ffn_kernel,
        grid=(m // tm, nf),
        in_specs=[
            pl.BlockSpec((tm, D_MODEL), lambda i, f: (i, 0)),
            pl.BlockSpec((1, D_MODEL), lambda i, f: (0, 0)),
            pl.BlockSpec((1, D_MODEL), lambda i, f: (0, 0)),
            pl.BlockSpec((None, D_MODEL, tf), lambda i, f: (layer, 0, f)),
            pl.BlockSpec((None, D_MODEL, tf), lambda i, f: (layer, 0, nf + f)),
            pl.BlockSpec((None, tf, D_MODEL), lambda i, f: (layer, f, 0)),
        ],
        out_specs=pl.BlockSpec((tm, D_MODEL), lambda i, f: (i, 0), **mode),
        out_shape=jax.ShapeDtypeStruct((m, D_MODEL), F32),
        scratch_shapes=[pltpu.VMEM((tm, D_MODEL), BF16)],
        compiler_params=_params("parallel", "arbitrary"),
        name="ffn",
    )(x, g_pre.reshape(1, D_MODEL), g_post.reshape(1, D_MODEL), w_up, w_up, w_down)


def _swa_kernel(q_ref, k_ref, v_ref, tb_ref, o_ref, lse_ref):
    n_units, u, _ = q_ref.shape
    per_blk = N_BACK // u
    n_blk = n_units // per_blk
    lane = lax.broadcasted_iota(jnp.int32, (N_BACK, HEAD_DIM), 1)
    ones = jnp.ones((2 * N_BACK, HEAD_DIM), BF16)

    def rows(ref, unit0, n_rows, lo, hi):
        return ref[pl.ds(unit0, n_rows // u), :, lo:hi].reshape(n_rows, hi - lo)

    def block(qu, ku, table):
        n_keys = N_BACK if table == 0 else 2 * N_BACK
        lse_tile = jnp.zeros((N_BACK, HEAD_DIM), F32)
        for h in range(HEADS_PER_GROUP):
            lo, hi = h * HEAD_DIM, (h + 1) * HEAD_DIM
            q = rows(q_ref, qu, N_BACK, lo, hi)
            kw = rows(k_ref, ku, n_keys, lo, hi)
            vw = rows(v_ref, ku, n_keys, lo, hi)
            s = lax.dot_general(q, kw, (((1,), (1,)), ((), ())), preferred_element_type=F32)
            s = s * ATTN_SCALE + tb_ref[table, h][:, :n_keys]
            m = jnp.max(s, axis=1, keepdims=True)
            p = jnp.exp(s - m).astype(BF16)
            ov = jnp.dot(p, jnp.concatenate([vw, ones[:n_keys]], axis=1), preferred_element_type=F32)
            den = ov[:, HEAD_DIM:]
            o = ov[:, :HEAD_DIM] / den
            o_ref[pl.ds(qu, per_blk), :, lo:hi] = o.reshape(per_blk, u, HEAD_DIM).astype(o_ref.dtype)
            lse_tile = jnp.where(lane == h, m + jnp.log(den), lse_tile)
        lse_ref[pl.ds(qu, per_blk), :, :] = lse_tile.reshape(per_blk, u, HEAD_DIM)

    def block_at(n):
        block(n * per_blk, (n - 1) * per_blk, 1)

    block(0, 0, 0)

    def body(i, carry):
        block_at(1 + 2 * i)
        block_at(2 + 2 * i)
        return carry

    n_pairs = (n_blk - 1) // 2
    if n_pairs > 0:
        lax.fori_loop(0, n_pairs, body, 0)
    if (n_blk - 1) % 2 == 1:
        block_at(n_blk - 1)


def _sub_block(dil):
    return N_BACK if dil == 1 else PERM_BLOCK


def _swa_group(zb, tables, g, *, batch, seq):
    dil = SWA_PATTERN[g][1]
    sub = _sub_block(dil)
    n_units, u = seq // sub, sub // dil
    view = zb.reshape(zb.shape[0], batch, n_units, dil, u, GROUP_WIDTH)

    def rows_in(tile):
        return pl.BlockSpec((None, None, n_units, None, u, GROUP_WIDTH), lambda i, r: (tile, i, 0, r, 0, 0))

    def rows_out(width):
        return pl.BlockSpec((None, n_units, None, u, width), lambda i, r: (i, 0, r, 0, 0))

    return pl.pallas_call(
        _swa_kernel,
        grid=(batch, dil),
        in_specs=[rows_in(g), rows_in(N_SWA_GROUPS + g), rows_in(2 * N_SWA_GROUPS + g),
                  pl.BlockSpec((None, 2, HEADS_PER_GROUP, N_BACK, 2 * N_BACK), lambda i, r: (g, 0, 0, 0, 0))],
        out_specs=[rows_out(GROUP_WIDTH), rows_out(HEAD_DIM)],
        out_shape=[
            jax.ShapeDtypeStruct((batch, n_units, dil, u, GROUP_WIDTH), BF16),
            jax.ShapeDtypeStruct((batch, n_units, dil, u, HEAD_DIM), F32),
        ],
        compiler_params=_params("parallel", "parallel"),
        name=f"swa_group{g}",
    )(view, view, view, tables)


def _split3(x):
    hi = x.astype(BF16)
    rest = x - hi.astype(F32)
    mid = rest.astype(BF16)
    lo = (rest - mid.astype(F32)).astype(BF16)
    return hi, mid, lo


def _swa_merge_kernel(o0_ref, o1_ref, o2_ref, l0_ref, l1_ref, l2_ref, q_ref, kv_ref, o_ref):
    tm = o_ref.shape[0]
    outs, lses = [], []
    for g, (o_g, l_g) in enumerate(((o0_ref, l0_ref), (o1_ref, l1_ref), (o2_ref, l2_ref))):
        dil = SWA_PATTERN[g][1]
        o = o_g[...].reshape(tm, GROUP_WIDTH)
        l = l_g[...].reshape(tm, HEAD_DIM)
        if dil > 1:
            inv = _residue_major_perm(PERM_BLOCK, dil, transpose=True)
            l3 = _split3(l)
            o_nat, l_nat = [], []
            for s in range(0, tm, PERM_BLOCK):
                o_nat.append(jnp.dot(inv, o[s:s + PERM_BLOCK, :], preferred_element_type=F32))
                l_nat.append(sum(jnp.dot(inv, t[s:s + PERM_BLOCK, :], preferred_element_type=F32) for t in l3))
            o = jnp.concatenate(o_nat, axis=0)
            l = jnp.concatenate(l_nat, axis=0)
        outs.append(o.astype(F32))
        lses.append(l)
    for h in range(HEADS_PER_GROUP):
        lo, hi = h * HEAD_DIM, (h + 1) * HEAD_DIM
        ls = [l[:, h:h + 1] for l in lses]
        mx = jnp.maximum(jnp.maximum(ls[0], ls[1]), ls[2])
        es = [jnp.exp(l - mx) for l in ls]
        tot = es[0] + es[1] + es[2]
        for g in range(N_SWA_GROUPS):
            alpha = es[g] / tot
            o_ref[:, g * GROUP_WIDTH + lo:g * GROUP_WIDTH + hi] = (outs[g][:, lo:hi] * alpha).astype(o_ref.dtype)
    _mem_attention_tile(q_ref, kv_ref, o_ref, MIXER_WIDTH)


def _swa_merge(outs, lses, zb, mem_kv, *, rows_per_batch):
    m = zb.shape[1]
    tm = MERGE_TILE
    tiles_per_batch = rows_per_batch // tm

    def tile(width, dil):
        sub = _sub_block(dil)
        return pl.BlockSpec((None, tm // sub, dil, sub // dil, width),
                            lambda i: (i // tiles_per_batch, i % tiles_per_batch, 0, 0, 0))

    return pl.pallas_call(
        _swa_merge_kernel,
        grid=(m // tm,),
        in_specs=(
            [tile(GROUP_WIDTH, dil) for _, dil in SWA_PATTERN]
            + [tile(HEAD_DIM, dil) for _, dil in SWA_PATTERN]
            + [pl.BlockSpec((None, tm, MEM_WIDTH), lambda i: (3 * N_SWA_GROUPS, i, 0)),
               pl.BlockSpec((None, N_MEM, 2 * MEM_WIDTH), lambda i: (i // tiles_per_batch, 0, 0))]
        ),
        out_specs=pl.BlockSpec((tm, D_MODEL), lambda i: (i, 0)),
        out_shape=jax.ShapeDtypeStruct((m, D_MODEL), BF16),
        compiler_params=_params("parallel"),
        name="swa_merge",
    )(*outs, *lses, zb, mem_kv)


def _sample_mem_attention(q_row, kv_ref, o_ref, col0):
    for h in range(N_MEM_HEADS):
        lo, hi = h * HEAD_DIM, (h + 1) * HEAD_DIM
        q = q_row[:, lo:hi]
        k = kv_ref[:, lo:hi]
        v = kv_ref[:, MEM_WIDTH + lo:MEM_WIDTH + hi]
        s = jnp.sum(k * q, axis=1, keepdims=True) * ATTN_SCALE
        m = jnp.max(s, axis=0, keepdims=True)
        p = jnp.exp(s - m)
        den = jnp.sum(p, axis=0, keepdims=True)
        o_ref[:, col0 + lo:col0 + hi] = jnp.sum(p * v, axis=0, keepdims=True) / den


def _sample_mix_a_kernel(z_ref, gv_ref, w0_ref, b0_ref, kv_ref, o_ref, vrow_ref):
    u = z_ref[:, 0:MIXER_WIDTH]
    v = _rms(z_ref[:, MIXER_WIDTH:2 * MIXER_WIDTH], gv_ref[...])
    vrow_ref[...] = v
    o_ref[:, 0:MIXER_WIDTH] = u * (w0_ref[...] * v + b0_ref[...])
    _sample_mem_attention(z_ref[:, 2 * MIXER_WIDTH:2 * MIXER_WIDTH + MEM_WIDTH], kv_ref, o_ref, MIXER_WIDTH)


def _sample_mix_a(z, g_v, w_s, b_s, mem_kv):
    bd = mem_kv.shape[0]
    w0 = jnp.repeat(w_s[:, 0, 0], GROUP_DIM_A).reshape(1, MIXER_WIDTH)
    b0 = jnp.repeat(b_s[:, 0], GROUP_DIM_A).reshape(1, MIXER_WIDTH)
    width = z.shape[1]
    vec = lambda i: (0, 0)
    return pl.pallas_call(
        _sample_mix_a_kernel,
        grid=(bd,),
        in_specs=[
            pl.BlockSpec((None, 1, width), lambda i: (i, 0, 0)),
            pl.BlockSpec((1, MIXER_WIDTH), vec),
            pl.BlockSpec((1, MIXER_WIDTH), vec),
            pl.BlockSpec((1, MIXER_WIDTH), vec),
            pl.BlockSpec((None, N_MEM, 2 * MEM_WIDTH), lambda i: (i, 0, 0)),
        ],
        out_specs=[
            pl.BlockSpec((None, 1, D_MODEL), lambda i: (i, 0, 0)),
            pl.BlockSpec((None, 1, MIXER_WIDTH), lambda i: (i, 0, 0)),
        ],
        out_shape=[
            jax.ShapeDtypeStruct((bd, 1, D_MODEL), F32),
            jax.ShapeDtypeStruct((bd, 1, MIXER_WIDTH), F32),
        ],
        compiler_params=_params("parallel"),
        name="sample_mix_a",
    )(z[:bd].reshape(bd, 1, width), g_v.reshape(1, MIXER_WIDTH), w0, b0, mem_kv)


def _sample_mix_b_kernel(z_ref, c0_ref, c1_ref, c2_ref, bcol_ref, bnew_ref, kv_ref, o_ref):
    caches = (c0_ref, c1_ref, c2_ref)
    outs = [[None] * HEADS_PER_GROUP for _ in range(N_SWA_GROUPS)]
    lses = [[None] * HEADS_PER_GROUP for _ in range(N_SWA_GROUPS)]
    for g in range(N_SWA_GROUPS):
        for h in range(HEADS_PER_GROUP):
            hd = g * HEADS_PER_GROUP + h
            lo, hi = h * HEAD_DIM, (h + 1) * HEAD_DIM
            q = z_ref[:, hd * HEAD_DIM:(hd + 1) * HEAD_DIM]
            k_new = z_ref[:, MIXER_WIDTH + hd * HEAD_DIM:MIXER_WIDTH + (hd + 1) * HEAD_DIM]
            v_new = z_ref[:, 2 * MIXER_WIDTH + hd * HEAD_DIM:2 * MIXER_WIDTH + (hd + 1) * HEAD_DIM]
            kc = caches[g][:, lo:hi]
            vc = caches[g][:, GROUP_WIDTH + lo:GROUP_WIDTH + hi]
            s_c = jnp.sum(kc * q, axis=1, keepdims=True) * ATTN_SCALE + bcol_ref[g][:, h:h + 1]
            s_n = jnp.sum(k_new * q, axis=1, keepdims=True) * ATTN_SCALE + bnew_ref[:, hd:hd + 1]
            m = jnp.maximum(jnp.max(s_c, axis=0, keepdims=True), s_n)
            p_c = jnp.exp(s_c - m)
            p_n = jnp.exp(s_n - m)
            den = jnp.sum(p_c, axis=0, keepdims=True) + p_n
            outs[g][h] = (jnp.sum(p_c * vc, axis=0, keepdims=True) + p_n * v_new) / den
            lses[g][h] = m + jnp.log(den)
    for h in range(HEADS_PER_GROUP):
        ls = [lses[g][h] for g in range(N_SWA_GROUPS)]
        mx = jnp.maximum(jnp.maximum(ls[0], ls[1]), ls[2])
        es = [jnp.exp(l - mx) for l in ls]
        tot = es[0] + es[1] + es[2]
        for g in range(N_SWA_GROUPS):
            c0 = g * GROUP_WIDTH + h * HEAD_DIM
            o_ref[:, c0:c0 + HEAD_DIM] = outs[g][h] * (es[g] / tot)
    _sample_mem_attention(z_ref[:, 3 * MIXER_WIDTH:3 * MIXER_WIDTH + MEM_WIDTH], kv_ref, o_ref, MIXER_WIDTH)


def _sample_mix_b(z, win_caches, bias_groups, mem_kv):
    bd = mem_kv.shape[0]
    width = z.shape[1]
    kv_width = 2 * GROUP_WIDTH
    cache_views, cache_specs = [], []
    for g, (win, dil) in enumerate(SWA_PATTERN):
        cache_views.append(win_caches[g][:, ::dil].reshape(bd, win // dil, kv_width))
        cache_specs.append(pl.BlockSpec((None, N_BACK, kv_width), lambda i: (i, 0, 0)))
    bcol = jnp.stack([bg[:, N_BACK:0:-1].T for bg in bias_groups], axis=0)
    bnew = jnp.concatenate([bg[:, 0] for bg in bias_groups])
    bnew = jnp.pad(bnew, (0, HEAD_DIM - bnew.shape[0])).reshape(1, HEAD_DIM)
    return pl.pallas_call(
        _sample_mix_b_kernel,
        grid=(bd,),
        in_specs=[pl.BlockSpec((None, 1, width), lambda i: (i, 0, 0))] + cache_specs + [
            pl.BlockSpec((N_SWA_GROUPS, N_BACK, HEADS_PER_GROUP), lambda i: (0, 0, 0)),
            pl.BlockSpec((1, HEAD_DIM), lambda i: (0, 0)),
            pl.BlockSpec((None, N_MEM, 2 * MEM_WIDTH), lambda i: (i, 0, 0)),
        ],
        out_specs=pl.BlockSpec((None, 1, D_MODEL), lambda i: (i, 0, 0)),
        out_shape=jax.ShapeDtypeStruct((bd, 1, D_MODEL), F32),
        compiler_params=_params("parallel"),
        name="sample_mix_b",
    )(z[:bd].reshape(bd, 1, width), *cache_views, bcol, bnew, mem_kv)


def _t5_bucket(dist):
    nf = jnp.maximum(dist, MAX_EXACT).astype(F32)
    large = MAX_EXACT + (jnp.log(nf / MAX_EXACT) / math.log(MAX_DISTANCE / MAX_EXACT)
                         * (N_BUCKETS - MAX_EXACT)).astype(jnp.int32)
    large = jnp.minimum(large, N_BUCKETS - 1)
    return jnp.where(dist < MAX_EXACT, dist, large)


def _group_bias(rel_bias, g, dil):
    dist = jnp.arange(N_BACK + 1, dtype=jnp.int32) * dil
    b = rel_bias[_t5_bucket(dist)][:, g * HEADS_PER_GROUP:(g + 1) * HEADS_PER_GROUP]
    return b.T.astype(F32)


def _band_tables(bias_groups):
    width = 2 * N_BACK
    rows = []
    for bias_j in bias_groups:
        masked = jnp.full((HEADS_PER_GROUP, N_BACK - 1), NEG_INF, F32)
        rows.append(jnp.concatenate([bias_j[:, :1], masked, bias_j[:, N_BACK:0:-1]], axis=1))
        rows.append(jnp.concatenate([bias_j[:, ::-1], masked], axis=1))
    base = jnp.stack(rows, axis=0).reshape(-1, width)
    pitch = 2 * width - 1
    flat = jnp.tile(base, (1, width))[:, :N_BACK * pitch]
    tabs = flat.reshape(-1, N_BACK, pitch)[:, :, :width]
    return tabs.reshape(N_SWA_GROUPS, 2, HEADS_PER_GROUP, N_BACK, width)


def _tail_rows(rows, win, dil, *, batch, seq):
    sub = _sub_block(dil)
    a = rows.reshape(batch, seq // sub, dil, sub // dil, GROUP_WIDTH)[:, (seq - win) // sub:]
    return a.transpose(0, 1, 3, 2, 4).reshape(batch, win, HEADS_PER_GROUP, HEAD_DIM).astype(F32)


def kernel(x_prompt, x_sample, mem_prompt, cache_mem_kv, cache_win128_kv, cache_win512_kv, cache_win2048_kv, rel_bias, norm_mix_pre, norm_mix_post, norm_ffn_pre, norm_ffn_post, norm_mem, w_mem_kv, w_in_a, norm_v_a, w_spatial_a, b_spatial_a, w_in_b, w_out, w_ffn_up, w_ffn_down):
    batch, seq, _ = x_prompt.shape
    bd = x_sample.shape[0]
    depth = w_out.shape[0]
    m_p = batch * seq
    win_caches = (cache_win128_kv, cache_win512_kv, cache_win2048_kv)

    w_mem_kv_b = w_mem_kv.astype(BF16)
    w_in_a_b = w_in_a.astype(BF16)
    w_in_b_b = w_in_b.astype(BF16)
    w_out_b = w_out.astype(BF16)
    w_up_b = w_ffn_up.astype(BF16)
    w_down_b = w_ffn_down.astype(BF16)

    bias_groups = [_group_bias(rel_bias, g, dil) for g, (_, dil) in enumerate(SWA_PATTERN)]
    band_tables = _band_tables(bias_groups)

    yp = x_prompt.reshape(m_p, D_MODEL)
    ys = jnp.pad(x_sample.reshape(bd, D_MODEL), ((0, SAMPLE_PAD - bd), (0, 0)))
    mem_rows = mem_prompt.reshape(batch * N_MEM, D_MODEL)

    mem_kv_p, chunk_v_s = [], []
    win_p = [[] for _ in SWA_PATTERN]
    win_s = [[] for _ in SWA_PATTERN]
    for i in range(depth):
        li = i // 2
        kv_p = _norm_matmul(mem_rows, norm_mem[i], w_mem_kv_b, i, tm=batch * N_MEM, tn=512,
                            gelu_cols=0, out_dtype=F32, name="mem_kv").reshape(batch, N_MEM, 2 * MEM_WIDTH)
        kv_s = cache_mem_kv[i].reshape(bd, N_MEM, 2 * MEM_WIDTH)
        mem_kv_p.append(kv_p.reshape(batch, N_MEM, 2, N_MEM_HEADS, HEAD_DIM))
        if i % 2 == 0:
            zp = _norm_matmul(yp, norm_mix_pre[i], w_in_a_b, li, tm=1024, tn=512,
                              gelu_cols=2 * MIXER_WIDTH, out_dtype=BF16, name="in_proj_a")
            zs = _norm_matmul(ys, norm_mix_pre[i], w_in_a_b, li, tm=SAMPLE_PAD, tn=512,
                              gelu_cols=2 * MIXER_WIDTH, out_dtype=F32, name="in_proj_a_s")
            mix_p = _gmlp_mix(zp, norm_v_a[li], w_spatial_a[li], b_spatial_a[li], kv_p,
                              tm=512, rows_per_batch=seq)
            mix_s, v_rows = _sample_mix_a(zs, norm_v_a[li], w_spatial_a[li], b_spatial_a[li], kv_s)
            chunk_v_s.append(v_rows)
        else:
            zb = _in_proj_b(yp, norm_mix_pre[i], w_in_b_b, li, tm=1024)
            zs = _norm_matmul(ys, norm_mix_pre[i], w_in_b_b, li, tm=SAMPLE_PAD, tn=512,
                              gelu_cols=0, out_dtype=F32, name="in_proj_b_s")
            outs, lses = [], []
            for g, (win, dil) in enumerate(SWA_PATTERN):
                o, lse = _swa_group(zb, band_tables, g, batch=batch, seq=seq)
                outs.append(o)
                lses.append(lse)
                k_tail = _tail_rows(zb[N_SWA_GROUPS + g], win, dil, batch=batch, seq=seq)
                v_tail = _tail_rows(zb[2 * N_SWA_GROUPS + g], win, dil, batch=batch, seq=seq)
                win_p[g].append(jnp.stack([k_tail, v_tail], axis=2))
                kv_new = zs[:bd, MIXER_WIDTH:3 * MIXER_WIDTH]
                kv_new = kv_new.reshape(bd, 1, 2, N_SWA_GROUPS, HEADS_PER_GROUP, HEAD_DIM)[:, :, :, g]
                win_s[g].append(kv_new)
            mix_p = _swa_merge(outs, lses, zb, kv_p, rows_per_batch=seq)
            mix_s = _sample_mix_b(zs, [c[li] for c in win_caches], bias_groups, kv_s)
        mix_s = jnp.pad(mix_s.reshape(bd, D_MODEL), ((0, SAMPLE_PAD - bd), (0, 0))).astype(BF16)
        yp = _out_proj(mix_p, w_out_b, i, yp, norm_mix_post[i], tm=512)
        ys = _out_proj(mix_s, w_out_b, i, ys, norm_mix_post[i], tm=SAMPLE_PAD)
        yp = _ffn(yp, norm_ffn_pre[i], norm_ffn_post[i], w_up_b, w_down_b, i, tm=1024, tf=512,
                  single_buffer=True)
        ys = _ffn(ys, norm_ffn_pre[i], norm_ffn_post[i], w_up_b, w_down_b, i, tm=SAMPLE_PAD, tf=512,
                  single_buffer=False)

    return (
        yp.reshape(batch, seq, D_MODEL),
        ys[:bd].reshape(bd, 1, D_MODEL),
        jnp.stack(mem_kv_p, axis=0),
        jnp.stack(chunk_v_s, axis=0),
        jnp.stack(win_p[0], axis=0),
        jnp.stack(win_p[1], axis=0),
        jnp.stack(win_p[2], axis=0),
        jnp.stack(win_s[0], axis=0),
        jnp.stack(win_s[1], axis=0),
        jnp.stack(win_s[2], axis=0),
    )
```

```python
import functools
import math

import jax
import jax.numpy as jnp
from jax import lax
from jax.experimental import pallas as pl
from jax.experimental.pallas import tpu as pltpu

F32 = jnp.float32
BF16 = jnp.bfloat16

D_MODEL = 2048
HEAD_DIM = 128
N_MEM = 256
N_MEM_HEADS = 4
MEM_WIDTH = N_MEM_HEADS * HEAD_DIM
MIXER_WIDTH = D_MODEL - MEM_WIDTH
CHUNK = 128
N_GROUPS_A = 4
GROUP_DIM_A = MIXER_WIDTH // N_GROUPS_A
SWA_PATTERN = ((128, 1), (512, 4), (2048, 16))
N_SWA_GROUPS = len(SWA_PATTERN)
HEADS_PER_GROUP = 4
GROUP_WIDTH = HEADS_PER_GROUP * HEAD_DIM
N_BACK = 128
N_BUCKETS = 32
MAX_EXACT = N_BUCKETS // 2
MAX_DISTANCE = 2048
D_FF = 5632
EPS = 1e-6
NEG_INF = -1e30
ATTN_SCALE = HEAD_DIM ** -0.5
SAMPLE_PAD = 16
PERM_BLOCK = 256
MERGE_TILE = 512

VMEM_LIMIT = 56 * 1024 * 1024


def _params(*sem):
    return pltpu.CompilerParams(dimension_semantics=sem, vmem_limit_bytes=VMEM_LIMIT)


def _gelu(x):
    return 0.5 * x * (1.0 + jnp.tanh(0.7978845608028654 * (x + 0.044715 * (x * x * x))))


def _rms(x, g):
    return x * lax.rsqrt(jnp.mean(x * x, axis=-1, keepdims=True) + EPS) * g


def _log2(n):
    assert n & (n - 1) == 0
    return n.bit_length() - 1


def _residue_major_perm(tm, dil, transpose=False):
    n = tm // dil
    row = lax.broadcasted_iota(jnp.int32, (tm, tm), 0)
    col = lax.broadcasted_iota(jnp.int32, (tm, tm), 1)
    dst, src = (col, row) if transpose else (row, col)
    want = lax.shift_left(jnp.bitwise_and(dst, n - 1), _log2(dil)) + lax.shift_right_logical(dst, _log2(n))
    return (src == want).astype(BF16)


def _norm_matmul_kernel(x_ref, g_ref, w_ref, o_ref, *rest, n_gelu_tiles, emit_w):
    xn_ref = rest[-1]
    n = pl.program_id(1)

    @pl.when(n == 0)
    def _():
        xn_ref[...] = _rms(x_ref[...], g_ref[...]).astype(BF16)

    w = w_ref[...].astype(BF16)
    if emit_w:
        rest[0][...] = w
    acc = jnp.dot(xn_ref[...], w, preferred_element_type=F32)
    if n_gelu_tiles > 0:
        acc = jnp.where(n < n_gelu_tiles, _gelu(acc), acc)
    o_ref[...] = acc.astype(o_ref.dtype)


def _weight_spec(w, layer, block, index):
    if layer is None:
        return pl.BlockSpec(block, index)
    return pl.BlockSpec((None,) + block, lambda *ids: (layer,) + index(*ids))


def _norm_matmul(x, g, w, layer, *, tm, tn, gelu_cols, out_dtype, name, emit_w=False):
    m, k = x.shape
    n = w.shape[-1]
    out_specs = [pl.BlockSpec((tm, tn), lambda i, j: (i, j))]
    out_shape = [jax.ShapeDtypeStruct((m, n), out_dtype)]
    if emit_w:
        assert m == tm
        out_specs.append(pl.BlockSpec((k, tn), lambda i, j: (0, j)))
        out_shape.append(jax.ShapeDtypeStruct((k, n), BF16))
    outs = pl.pallas_call(
        functools.partial(_norm_matmul_kernel, n_gelu_tiles=gelu_cols // tn, emit_w=emit_w),
        grid=(m // tm, n // tn),
        in_specs=[
            pl.BlockSpec((tm, k), lambda i, j: (i, 0)),
            pl.BlockSpec((1, k), lambda i, j: (0, 0)),
            _weight_spec(w, layer, (k, tn), lambda i, j: (0, j)),
        ],
        out_specs=out_specs,
        out_shape=out_shape,
        scratch_shapes=[pltpu.VMEM((tm, k), BF16)],
        compiler_params=_params("parallel", "arbitrary"),
        name=name,
    )(x, g.reshape(1, k), w)
    return outs if emit_w else outs[0]


def _in_proj_b_kernel(x_ref, g_ref, w_ref, o_ref, xn_ref):
    tm = x_ref.shape[0]
    j = pl.program_id(1)

    @pl.when(j == 0)
    def _():
        xn = _rms(x_ref[...], g_ref[...]).astype(BF16)
        xn_ref[0] = xn
        for g in range(1, N_SWA_GROUPS):
            perm = _residue_major_perm(PERM_BLOCK, SWA_PATTERN[g][1])
            for s in range(0, tm, PERM_BLOCK):
                xn_ref[g, s:s + PERM_BLOCK, :] = jnp.dot(
                    perm, xn[s:s + PERM_BLOCK, :], preferred_element_type=F32).astype(BF16)

    src = jnp.where(j < 3 * N_SWA_GROUPS, j % N_SWA_GROUPS, 0)
    o_ref[...] = jnp.dot(xn_ref[src], w_ref[...], preferred_element_type=F32).astype(o_ref.dtype)


def _in_proj_b(x, g, w, *, tm):
    m, k = x.shape
    n_tiles = w.shape[1] // GROUP_WIDTH
    return pl.pallas_call(
        _in_proj_b_kernel,
        grid=(m // tm, n_tiles),
        in_specs=[
            pl.BlockSpec((tm, k), lambda i, j: (i, 0)),
            pl.BlockSpec((1, k), lambda i, j: (0, 0)),
            pl.BlockSpec((k, GROUP_WIDTH), lambda i, j: (0, j)),
        ],
        out_specs=pl.BlockSpec((None, tm, GROUP_WIDTH), lambda i, j: (j, i, 0)),
        out_shape=jax.ShapeDtypeStruct((n_tiles, m, GROUP_WIDTH), BF16),
        scratch_shapes=[pltpu.VMEM((N_SWA_GROUPS, tm, k), BF16)],
        compiler_params=_params("parallel", "arbitrary"),
        name="in_proj_b",
    )(x, g.reshape(1, k), w)


def _mem_attention_tile(q_ref, kv_ref, o_ref, col0):
    ones = jnp.ones((N_MEM, HEAD_DIM), BF16)
    for h in range(N_MEM_HEADS):
        lo, hi = h * HEAD_DIM, (h + 1) * HEAD_DIM
        q = q_ref[:, lo:hi]
        k = kv_ref[:, lo:hi].astype(BF16)
        v = kv_ref[:, MEM_WIDTH + lo:MEM_WIDTH + hi].astype(BF16)
        s = lax.dot_general(q, k, (((1,), (1,)), ((), ())), preferred_element_type=F32) * ATTN_SCALE
        m = jnp.max(s, axis=1, keepdims=True)
        p = jnp.exp(s - m).astype(BF16)
        ov = jnp.dot(p, jnp.concatenate([v, ones], axis=1), preferred_element_type=F32)
        o_ref[:, col0 + lo:col0 + hi] = (ov[:, :HEAD_DIM] / ov[:, HEAD_DIM:]).astype(o_ref.dtype)


def _gmlp_mix_kernel(u_ref, v_ref, q_ref, gv_ref, ws_ref, bs_ref, kv_ref, o_ref, vn_ref):
    tm = u_ref.shape[0]
    v = v_ref[...].astype(F32)
    vn_ref[...] = _rms(v, gv_ref[...]).astype(BF16)
    row = lax.broadcasted_iota(jnp.int32, (CHUNK, CHUNK), 0)
    col = lax.broadcasted_iota(jnp.int32, (CHUNK, CHUNK), 1)
    causal = row >= col
    for g in range(N_GROUPS_A):
        w = jnp.where(causal, ws_ref[g], 0.0).astype(BF16)
        b = bs_ref[:, g:g + 1]
        c0, c1 = g * GROUP_DIM_A, (g + 1) * GROUP_DIM_A
        for c in range(tm // CHUNK):
            r0, r1 = c * CHUNK, (c + 1) * CHUNK
            s = jnp.dot(w, vn_ref[r0:r1, c0:c1], preferred_element_type=F32) + b
            o_ref[r0:r1, c0:c1] = (u_ref[r0:r1, c0:c1].astype(F32) * s).astype(o_ref.dtype)
    _mem_attention_tile(q_ref, kv_ref, o_ref, MIXER_WIDTH)


def _gmlp_mix(zact, g_v, w_s, b_s, mem_kv, *, tm, rows_per_batch):
    m = zact.shape[0]
    tiles_per_batch = rows_per_batch // tm
    return pl.pallas_call(
        _gmlp_mix_kernel,
        grid=(m // tm,),
        in_specs=[
            pl.BlockSpec((tm, MIXER_WIDTH), lambda i: (i, 0)),
            pl.BlockSpec((tm, MIXER_WIDTH), lambda i: (i, 1)),
            pl.BlockSpec((tm, MEM_WIDTH), lambda i: (i, 2 * MIXER_WIDTH // MEM_WIDTH)),
            pl.BlockSpec((1, MIXER_WIDTH), lambda i: (0, 0)),
            pl.BlockSpec((N_GROUPS_A, CHUNK, CHUNK), lambda i: (0, 0, 0)),
            pl.BlockSpec((CHUNK, N_GROUPS_A), lambda i: (0, 0)),
            pl.BlockSpec((None, N_MEM, 2 * MEM_WIDTH), lambda i: (i // tiles_per_batch, 0, 0)),
        ],
        out_specs=pl.BlockSpec((tm, D_MODEL), lambda i: (i, 0)),
        out_shape=jax.ShapeDtypeStruct((m, D_MODEL), BF16),
        scratch_shapes=[pltpu.VMEM((tm, MIXER_WIDTH), BF16)],
        compiler_params=_params("parallel"),
        name="gmlp_mix",
    )(zact, zact, zact, g_v.reshape(1, MIXER_WIDTH), w_s, b_s.T, mem_kv)


def _out_proj_kernel(mix_ref, w_ref, x_ref, g_ref, o_ref):
    o = jnp.dot(mix_ref[...], w_ref[...], preferred_element_type=F32)
    o_ref[...] = x_ref[...] + _rms(o, g_ref[...])


def _out_proj(mix, w, x, g, *, tm):
    m = x.shape[0]
    return pl.pallas_call(
        _out_proj_kernel,
        grid=(m // tm,),
        in_specs=[
            pl.BlockSpec((tm, D_MODEL), lambda i: (i, 0)),
            pl.BlockSpec((D_MODEL, D_MODEL), lambda i: (0, 0)),
            pl.BlockSpec((tm, D_MODEL), lambda i: (i, 0)),
            pl.BlockSpec((1, D_MODEL), lambda i: (0, 0)),
        ],
        out_specs=pl.BlockSpec((tm, D_MODEL), lambda i: (i, 0)),
        out_shape=jax.ShapeDtypeStruct((m, D_MODEL), F32),
        compiler_params=_params("parallel"),
        name="out_proj",
    )(mix, w, x, g.reshape(1, D_MODEL))


def _out_proj_cast_kernel(mix_ref, w_ref, x_ref, g_ref, o_ref, wb_ref, acc_ref):
    j = pl.program_id(0)
    w = w_ref[...].astype(BF16)
    wb_ref[...] = w
    acc_ref[j] = jnp.dot(mix_ref[...], w, preferred_element_type=F32)

    @pl.when(j == pl.num_programs(0) - 1)
    def _():
        o = jnp.concatenate([acc_ref[t] for t in range(acc_ref.shape[0])], axis=1)
        o_ref[...] = x_ref[...] + _rms(o, g_ref[...])


def _out_proj_cast(mix, w, layer, x, g, *, tn):
    m = x.shape[0]
    n_tiles = D_MODEL // tn
    return pl.pallas_call(
        _out_proj_cast_kernel,
        grid=(n_tiles,),
        in_specs=[
            pl.BlockSpec((m, D_MODEL), lambda j: (0, 0)),
            pl.BlockSpec((None, D_MODEL, tn), lambda j: (layer, 0, j)),
            pl.BlockSpec((m, D_MODEL), lambda j: (0, 0)),
            pl.BlockSpec((1, D_MODEL), lambda j: (0, 0)),
        ],
        out_specs=[
            pl.BlockSpec((m, D_MODEL), lambda j: (0, 0)),
            pl.BlockSpec((D_MODEL, tn), lambda j: (0, j)),
        ],
        out_shape=[
            jax.ShapeDtypeStruct((m, D_MODEL), F32),
            jax.ShapeDtypeStruct((D_MODEL, D_MODEL), BF16),
        ],
        scratch_shapes=[pltpu.VMEM((n_tiles, m, tn), F32)],
        compiler_params=_params("arbitrary"),
        name="out_proj_cast",
    )(mix, w, x, g.reshape(1, D_MODEL))


def _ffn_kernel(x_ref, gpre_ref, gpost_ref, wg_ref, wl_ref, wd_ref, o_ref, *rest, emit_w):
    xn_ref = rest[-1]
    f = pl.program_id(1)

    @pl.when(f == 0)
    def _():
        xn_ref[...] = _rms(x_ref[...], gpre_ref[...]).astype(BF16)
        o_ref[...] = jnp.zeros_like(o_ref)

    wg, wl, wd = (r[...].astype(BF16) for r in (wg_ref, wl_ref, wd_ref))
    if emit_w:
        for r, w in zip(rest[:3], (wg, wl, wd)):
            r[...] = w
    xn = xn_ref[...]
    hg = jnp.dot(xn, wg, preferred_element_type=F32)
    hl = jnp.dot(xn, wl, preferred_element_type=F32)
    a = (hg * jax.nn.sigmoid(hg) * hl).astype(BF16)
    o_ref[...] += jnp.dot(a, wd, preferred_element_type=F32)

    @pl.when(f == pl.num_programs(1) - 1)
    def _():
        o_ref[...] = x_ref[...] + _rms(o_ref[...], gpost_ref[...])


def _ffn(x, g_pre, g_post, weights, *, tm, tf, single_buffer):
    m = x.shape[0]
    nf = D_FF // tf
    mode = dict(pipeline_mode=pl.Buffered(1)) if single_buffer else {}
    return pl.pallas_call(
        functools.partial(_ffn_kernel, emit_w=False),
        grid=(m // tm, nf),
        in_specs=[
            pl.BlockSpec((tm, D_MODEL), lambda i, f: (i, 0)),
            pl.BlockSpec((1, D_MODEL), lambda i, f: (0, 0)),
            pl.BlockSpec((1, D_MODEL), lambda i, f: (0, 0)),
            pl.BlockSpec((D_MODEL, tf), lambda i, f: (0, f)),
            pl.BlockSpec((D_MODEL, tf), lambda i, f: (0, f)),
            pl.BlockSpec((tf, D_MODEL), lambda i, f: (f, 0)),
        ],
        out_specs=pl.BlockSpec((tm, D_MODEL), lambda i, f: (i, 0), **mode),
        out_shape=jax.ShapeDtypeStruct((m, D_MODEL), F32),
        scratch_shapes=[pltpu.VMEM((tm, D_MODEL), BF16)],
        compiler_params=_params("parallel", "arbitrary"),
        name="ffn",
    )(x, g_pre.reshape(1, D_MODEL), g_post.reshape(1, D_MODEL), *weights)


def _ffn_cast(x, g_pre, g_post, w_up, w_down, layer, *, tf):
    m = x.shape[0]
    nf = D_FF // tf
    outs = pl.pallas_call(
        functools.partial(_ffn_kernel, emit_w=True),
        grid=(1, nf),
        in_specs=[
            pl.BlockSpec((m, D_MODEL), lambda i, f: (0, 0)),
            pl.BlockSpec((1, D_MODEL), lambda i, f: (0, 0)),
            pl.BlockSpec((1, D_MODEL), lambda i, f: (0, 0)),
            pl.BlockSpec((None, D_MODEL, tf), lambda i, f: (layer, 0, f)),
            pl.BlockSpec((None, D_MODEL, tf), lambda i, f: (layer, 0, nf + f)),
            pl.BlockSpec((None, tf, D_MODEL), lambda i, f: (layer, f, 0)),
        ],
        out_specs=[
            pl.BlockSpec((m, D_MODEL), lambda i, f: (0, 0)),
            pl.BlockSpec((D_MODEL, tf), lambda i, f: (0, f)),
            pl.BlockSpec((D_MODEL, tf), lambda i, f: (0, f)),
            pl.BlockSpec((tf, D_MODEL), lambda i, f: (f, 0)),
        ],
        out_shape=[
            jax.ShapeDtypeStruct((m, D_MODEL), F32),
            jax.ShapeDtypeStruct((D_MODEL, D_FF), BF16),
            jax.ShapeDtypeStruct((D_MODEL, D_FF), BF16),
            jax.ShapeDtypeStruct((D_FF, D_MODEL), BF16),
        ],
        scratch_shapes=[pltpu.VMEM((m, D_MODEL), BF16)],
        compiler_params=_params("parallel", "arbitrary"),
        name="ffn_cast",
    )(x, g_pre.reshape(1, D_MODEL), g_post.reshape(1, D_MODEL), w_up, w_up, w_down)
    return outs[0], tuple(outs[1:])


def _swa_kernel(q_ref, k_ref, v_ref, tb_ref, o_ref, lse_ref):
    n_units, n_res, u, _ = q_ref.shape
    per_blk = N_BACK // u
    n_blk = n_units // per_blk
    lane = lax.broadcasted_iota(jnp.int32, (N_BACK, HEAD_DIM), 1)
    ones = jnp.ones((2 * N_BACK, HEAD_DIM), BF16)

    def rows(ref, res, unit0, n_rows, lo, hi):
        return ref[pl.ds(unit0, n_rows // u), res, :, lo:hi].reshape(n_rows, hi - lo)

    def block(res, qu, ku, table):
        n_keys = N_BACK if table == 0 else 2 * N_BACK
        lse_tile = jnp.zeros((N_BACK, HEAD_DIM), F32)
        for h in range(HEADS_PER_GROUP):
            lo, hi = h * HEAD_DIM, (h + 1) * HEAD_DIM
            q = rows(q_ref, res, qu, N_BACK, lo, hi)
            kw = rows(k_ref, res, ku, n_keys, lo, hi)
            vw = rows(v_ref, res, ku, n_keys, lo, hi)
            s = lax.dot_general(q, kw, (((1,), (1,)), ((), ())), preferred_element_type=F32)
            s = s * ATTN_SCALE + tb_ref[table, h][:, :n_keys]
            m = jnp.max(s, axis=1, keepdims=True)
            p = jnp.exp(s - m).astype(BF16)
            ov = jnp.dot(p, jnp.concatenate([vw, ones[:n_keys]], axis=1), preferred_element_type=F32)
            den = ov[:, HEAD_DIM:]
            o = ov[:, :HEAD_DIM] / den
            o_ref[pl.ds(qu, per_blk), res, :, lo:hi] = o.reshape(per_blk, u, HEAD_DIM).astype(o_ref.dtype)
            lse_tile = jnp.where(lane == h, m + jnp.log(den), lse_tile)
        lse_ref[pl.ds(qu, per_blk), res, :, :] = lse_tile.reshape(per_blk, u, HEAD_DIM)

    def block_at(res, n):
        block(res, n * per_blk, (n - 1) * per_blk, 1)

    n_pairs = (n_blk - 1) // 2
    for res in range(n_res):
        block(res, 0, 0, 0)

        def body(i, carry, res=res):
            block_at(res, 1 + 2 * i)
            block_at(res, 2 + 2 * i)
            return carry

        if n_pairs > 0:
            lax.fori_loop(0, n_pairs, body, 0)
        if (n_blk - 1) % 2 == 1:
            block_at(res, n_blk - 1)


def _sub_block(dil):
    return N_BACK if dil == 1 else PERM_BLOCK


def _swa_group(zb, tables, g, *, batch, seq):
    dil = SWA_PATTERN[g][1]
    sub = _sub_block(dil)
    n_units, u = seq // sub, sub // dil
    n_res = 4 if n_units * u == 2 * N_BACK else 1
    view = zb.reshape(zb.shape[0], batch, n_units, dil, u, GROUP_WIDTH)

    def rows_in(tile):
        return pl.BlockSpec((None, None, n_units, n_res, u, GROUP_WIDTH), lambda i, r: (tile, i, 0, r, 0, 0))

    def rows_out(width):
        return pl.BlockSpec((None, n_units, n_res, u, width), lambda i, r: (i, 0, r, 0, 0))

    return pl.pallas_call(
        _swa_kernel,
        grid=(batch, dil // n_res),
        in_specs=[rows_in(g), rows_in(N_SWA_GROUPS + g), rows_in(2 * N_SWA_GROUPS + g),
                  pl.BlockSpec((None, 2, HEADS_PER_GROUP, N_BACK, 2 * N_BACK), lambda i, r: (g, 0, 0, 0, 0))],
        out_specs=[rows_out(GROUP_WIDTH), rows_out(HEAD_DIM)],
        out_shape=[
            jax.ShapeDtypeStruct((batch, n_units, dil, u, GROUP_WIDTH), BF16),
            jax.ShapeDtypeStruct((batch, n_units, dil, u, HEAD_DIM), F32),
        ],
        compiler_params=_params("parallel", "parallel"),
        name=f"swa_group{g}",
    )(view, view, view, tables)


def _split3(x):
    hi = x.astype(BF16)
    rest = x - hi.astype(F32)
    mid = rest.astype(BF16)
    lo = (rest - mid.astype(F32)).astype(BF16)
    return hi, mid, lo


def _swa_merge_kernel(o0_ref, o1_ref, o2_ref, l0_ref, l1_ref, l2_ref, q_ref, kv_ref, o_ref):
    tm = o_ref.shape[0]
    outs, lses = [], []
    for g, (o_g, l_g) in enumerate(((o0_ref, l0_ref), (o1_ref, l1_ref), (o2_ref, l2_ref))):
        dil = SWA_PATTERN[g][1]
        o = o_g[...].reshape(tm, GROUP_WIDTH)
        l = l_g[...].reshape(tm, HEAD_DIM)
        if dil > 1:
            inv = _residue_major_perm(PERM_BLOCK, dil, transpose=True)
            l3 = _split3(l)
            o_nat, l_nat = [], []
            for s in range(0, tm, PERM_BLOCK):
                o_nat.append(jnp.dot(inv, o[s:s + PERM_BLOCK, :], preferred_element_type=F32))
                l_nat.append(sum(jnp.dot(inv, t[s:s + PERM_BLOCK, :], preferred_element_type=F32) for t in l3))
            o = jnp.concatenate(o_nat, axis=0)
            l = jnp.concatenate(l_nat, axis=0)
        outs.append(o.astype(F32))
        lses.append(l)
    for h in range(HEADS_PER_GROUP):
        lo, hi = h * HEAD_DIM, (h + 1) * HEAD_DIM
        ls = [l[:, h:h + 1] for l in lses]
        mx = jnp.maximum(jnp.maximum(ls[0], ls[1]), ls[2])
        es = [jnp.exp(l - mx) for l in ls]
        tot = es[0] + es[1] + es[2]
        for g in range(N_SWA_GROUPS):
            alpha = es[g] / tot
            o_ref[:, g * GROUP_WIDTH + lo:g * GROUP_WIDTH + hi] = (outs[g][:, lo:hi] * alpha).astype(o_ref.dtype)
    _mem_attention_tile(q_ref, kv_ref, o_ref, MIXER_WIDTH)


def _swa_merge(outs, lses, zb, mem_kv, *, rows_per_batch):
    m = zb.shape[1]
    tm = MERGE_TILE
    tiles_per_batch = rows_per_batch // tm

    def tile(width, dil):
        sub = _sub_block(dil)
        return pl.BlockSpec((None, tm // sub, dil, sub // dil, width),
                            lambda i: (i // tiles_per_batch, i % tiles_per_batch, 0, 0, 0))

    return pl.pallas_call(
        _swa_merge_kernel,
        grid=(m // tm,),
        in_specs=(
            [tile(GROUP_WIDTH, dil) for _, dil in SWA_PATTERN]
            + [tile(HEAD_DIM, dil) for _, dil in SWA_PATTERN]
            + [pl.BlockSpec((None, tm, MEM_WIDTH), lambda i: (3 * N_SWA_GROUPS, i, 0)),
               pl.BlockSpec((None, N_MEM, 2 * MEM_WIDTH), lambda i: (i // tiles_per_batch, 0, 0))]
        ),
        out_specs=pl.BlockSpec((tm, D_MODEL), lambda i: (i, 0)),
        out_shape=jax.ShapeDtypeStruct((m, D_MODEL), BF16),
        compiler_params=_params("parallel"),
        name="swa_merge",
    )(*outs, *lses, zb, mem_kv)


def _sample_mem_attention(q_row, kv_ref, o_ref, col0):
    for h in range(N_MEM_HEADS):
        lo, hi = h * HEAD_DIM, (h + 1) * HEAD_DIM
        q = q_row[:, lo:hi]
        k = kv_ref[:, lo:hi]
        v = kv_ref[:, MEM_WIDTH + lo:MEM_WIDTH + hi]
        s = jnp.sum(k * q, axis=1, keepdims=True) * ATTN_SCALE
        m = jnp.max(s, axis=0, keepdims=True)
        p = jnp.exp(s - m)
        den = jnp.sum(p, axis=0, keepdims=True)
        o_ref[:, col0 + lo:col0 + hi] = jnp.sum(p * v, axis=0, keepdims=True) / den


def _sample_mix_a_kernel(z_ref, gv_ref, w0_ref, b0_ref, kv_ref, o_ref, vrow_ref):
    u = z_ref[:, 0:MIXER_WIDTH]
    v = _rms(z_ref[:, MIXER_WIDTH:2 * MIXER_WIDTH], gv_ref[...])
    vrow_ref[...] = v
    o_ref[:, 0:MIXER_WIDTH] = u * (w0_ref[...] * v + b0_ref[...])
    _sample_mem_attention(z_ref[:, 2 * MIXER_WIDTH:2 * MIXER_WIDTH + MEM_WIDTH], kv_ref, o_ref, MIXER_WIDTH)


def _sample_mix_a(z, g_v, w_s, b_s, mem_kv):
    bd = mem_kv.shape[0]
    w0 = jnp.repeat(w_s[:, 0, 0], GROUP_DIM_A).reshape(1, MIXER_WIDTH)
    b0 = jnp.repeat(b_s[:, 0], GROUP_DIM_A).reshape(1, MIXER_WIDTH)
    width = z.shape[1]
    vec = lambda i: (0, 0)
    return pl.pallas_call(
        _sample_mix_a_kernel,
        grid=(bd,),
        in_specs=[
            pl.BlockSpec((None, 1, width), lambda i: (i, 0, 0)),
            pl.BlockSpec((1, MIXER_WIDTH), vec),
            pl.BlockSpec((1, MIXER_WIDTH), vec),
            pl.BlockSpec((1, MIXER_WIDTH), vec),
            pl.BlockSpec((None, N_MEM, 2 * MEM_WIDTH), lambda i: (i, 0, 0)),
        ],
        out_specs=[
            pl.BlockSpec((None, 1, D_MODEL), lambda i: (i, 0, 0)),
            pl.BlockSpec((None, 1, MIXER_WIDTH), lambda i: (i, 0, 0)),
        ],
        out_shape=[
            jax.ShapeDtypeStruct((bd, 1, D_MODEL), F32),
            jax.ShapeDtypeStruct((bd, 1, MIXER_WIDTH), F32),
        ],
        compiler_params=_params("parallel"),
        name="sample_mix_a",
    )(z[:bd].reshape(bd, 1, width), g_v.reshape(1, MIXER_WIDTH), w0, b0, mem_kv)


def _sample_mix_b_kernel(z_ref, c0_ref, c1_ref, c2_ref, bcol_ref, bnew_ref, kv_ref, o_ref):
    caches = (c0_ref, c1_ref, c2_ref)
    outs = [[None] * HEADS_PER_GROUP for _ in range(N_SWA_GROUPS)]
    lses = [[None] * HEADS_PER_GROUP for _ in range(N_SWA_GROUPS)]
    for g in range(N_SWA_GROUPS):
        for h in range(HEADS_PER_GROUP):
            hd = g * HEADS_PER_GROUP + h
            lo, hi = h * HEAD_DIM, (h + 1) * HEAD_DIM
            q = z_ref[:, hd * HEAD_DIM:(hd + 1) * HEAD_DIM]
            k_new = z_ref[:, MIXER_WIDTH + hd * HEAD_DIM:MIXER_WIDTH + (hd + 1) * HEAD_DIM]
            v_new = z_ref[:, 2 * MIXER_WIDTH + hd * HEAD_DIM:2 * MIXER_WIDTH + (hd + 1) * HEAD_DIM]
            kc = caches[g][:, lo:hi]
            vc = caches[g][:, GROUP_WIDTH + lo:GROUP_WIDTH + hi]
            s_c = jnp.sum(kc * q, axis=1, keepdims=True) * ATTN_SCALE + bcol_ref[g][:, h:h + 1]
            s_n = jnp.sum(k_new * q, axis=1, keepdims=True) * ATTN_SCALE + bnew_ref[:, hd:hd + 1]
            m = jnp.maximum(jnp.max(s_c, axis=0, keepdims=True), s_n)
            p_c = jnp.exp(s_c - m)
            p_n = jnp.exp(s_n - m)
            den = jnp.sum(p_c, axis=0, keepdims=True) + p_n
            outs[g][h] = (jnp.sum(p_c * vc, axis=0, keepdims=True) + p_n * v_new) / den
            lses[g][h] = m + jnp.log(den)
    for h in range(HEADS_PER_GROUP):
        ls = [lses[g][h] for g in range(N_SWA_GROUPS)]
        mx = jnp.maximum(jnp.maximum(ls[0], ls[1]), ls[2])
        es = [jnp.exp(l - mx) for l in ls]
        tot = es[0] + es[1] + es[2]
        for g in range(N_SWA_GROUPS):
            c0 = g * GROUP_WIDTH + h * HEAD_DIM
            o_ref[:, c0:c0 + HEAD_DIM] = outs[g][h] * (es[g] / tot)
    _sample_mem_attention(z_ref[:, 3 * MIXER_WIDTH:3 * MIXER_WIDTH + MEM_WIDTH], kv_ref, o_ref, MIXER_WIDTH)


def _sample_mix_b(z, win_caches, bias_groups, mem_kv):
    bd = mem_kv.shape[0]
    width = z.shape[1]
    kv_width = 2 * GROUP_WIDTH
    cache_views, cache_specs = [], []
    for g, (win, dil) in enumerate(SWA_PATTERN):
        cache_views.append(win_caches[g][:, ::dil].reshape(bd, win // dil, kv_width))
        cache_specs.append(pl.BlockSpec((None, N_BACK, kv_width), lambda i: (i, 0, 0)))
    bcol = jnp.stack([bg[:, N_BACK:0:-1].T for bg in bias_groups], axis=0)
    bnew = jnp.concatenate([bg[:, 0] for bg in bias_groups])
    bnew = jnp.pad(bnew, (0, HEAD_DIM - bnew.shape[0])).reshape(1, HEAD_DIM)
    return pl.pallas_call(
        _sample_mix_b_kernel,
        grid=(bd,),
        in_specs=[pl.BlockSpec((None, 1, width), lambda i: (i, 0, 0))] + cache_specs + [
            pl.BlockSpec((N_SWA_GROUPS, N_BACK, HEADS_PER_GROUP), lambda i: (0, 0, 0)),
            pl.BlockSpec((1, HEAD_DIM), lambda i: (0, 0)),
            pl.BlockSpec((None, N_MEM, 2 * MEM_WIDTH), lambda i: (i, 0, 0)),
        ],
        out_specs=pl.BlockSpec((None, 1, D_MODEL), lambda i: (i, 0, 0)),
        out_shape=jax.ShapeDtypeStruct((bd, 1, D_MODEL), F32),
        compiler_params=_params("parallel"),
        name="sample_mix_b",
    )(z[:bd].reshape(bd, 1, width), *cache_views, bcol, bnew, mem_kv)


def _t5_bucket(dist):
    nf = jnp.maximum(dist, MAX_EXACT).astype(F32)
    large = MAX_EXACT + (jnp.log(nf / MAX_EXACT) / math.log(MAX_DISTANCE / MAX_EXACT)
                         * (N_BUCKETS - MAX_EXACT)).astype(jnp.int32)
    large = jnp.minimum(large, N_BUCKETS - 1)
    return jnp.where(dist < MAX_EXACT, dist, large)


def _group_bias(rel_bias, g, dil):
    dist = jnp.arange(N_BACK + 1, dtype=jnp.int32) * dil
    b = rel_bias[_t5_bucket(dist)][:, g * HEADS_PER_GROUP:(g + 1) * HEADS_PER_GROUP]
    return b.T.astype(F32)


def _band_tables(bias_groups):
    width = 2 * N_BACK
    rows = []
    for bias_j in bias_groups:
        masked = jnp.full((HEADS_PER_GROUP, N_BACK - 1), NEG_INF, F32)
        rows.append(jnp.concatenate([bias_j[:, :1], masked, bias_j[:, N_BACK:0:-1]], axis=1))
        rows.append(jnp.concatenate([bias_j[:, ::-1], masked], axis=1))
    base = jnp.stack(rows, axis=0).reshape(-1, width)
    pitch = 2 * width - 1
    flat = jnp.tile(base, (1, width))[:, :N_BACK * pitch]
    tabs = flat.reshape(-1, N_BACK, pitch)[:, :, :width]
    return tabs.reshape(N_SWA_GROUPS, 2, HEADS_PER_GROUP, N_BACK, width)


def _tail_rows(rows, win, dil, *, batch, seq):
    sub = _sub_block(dil)
    a = rows.reshape(batch, seq // sub, dil, sub // dil, GROUP_WIDTH)[:, (seq - win) // sub:]
    return a.transpose(0, 1, 3, 2, 4).reshape(batch, win, HEADS_PER_GROUP, HEAD_DIM).astype(F32)


def kernel(x_prompt, x_sample, mem_prompt, cache_mem_kv, cache_win128_kv, cache_win512_kv, cache_win2048_kv, rel_bias, norm_mix_pre, norm_mix_post, norm_ffn_pre, norm_ffn_post, norm_mem, w_mem_kv, w_in_a, norm_v_a, w_spatial_a, b_spatial_a, w_in_b, w_out, w_ffn_up, w_ffn_down):
    batch, seq, _ = x_prompt.shape
    bd = x_sample.shape[0]
    depth = w_out.shape[0]
    m_p = batch * seq
    win_caches = (cache_win128_kv, cache_win512_kv, cache_win2048_kv)

    bias_groups = [_group_bias(rel_bias, g, dil) for g, (_, dil) in enumerate(SWA_PATTERN)]
    band_tables = _band_tables(bias_groups)

    yp = x_prompt.reshape(m_p, D_MODEL)
    ys = jnp.pad(x_sample.reshape(bd, D_MODEL), ((0, SAMPLE_PAD - bd), (0, 0)))
    mem_rows = mem_prompt.reshape(batch * N_MEM, D_MODEL)

    mem_kv_p, chunk_v_s = [], []
    win_p = [[] for _ in SWA_PATTERN]
    win_s = [[] for _ in SWA_PATTERN]
    for i in range(depth):
        li = i // 2
        kv_p = _norm_matmul(mem_rows, norm_mem[i], w_mem_kv, i, tm=batch * N_MEM, tn=512,
                            gelu_cols=0, out_dtype=F32, name="mem_kv").reshape(batch, N_MEM, 2 * MEM_WIDTH)
        kv_s = cache_mem_kv[i].reshape(bd, N_MEM, 2 * MEM_WIDTH)
        mem_kv_p.append(kv_p.reshape(batch, N_MEM, 2, N_MEM_HEADS, HEAD_DIM))
        if i % 2 == 0:
            zs, w_in = _norm_matmul(ys, norm_mix_pre[i], w_in_a, li, tm=SAMPLE_PAD, tn=512, emit_w=True,
                                    gelu_cols=2 * MIXER_WIDTH, out_dtype=F32, name="in_proj_a_s")
            zp = _norm_matmul(yp, norm_mix_pre[i], w_in, None, tm=1024, tn=512,
                              gelu_cols=2 * MIXER_WIDTH, out_dtype=BF16, name="in_proj_a")
            mix_p = _gmlp_mix(zp, norm_v_a[li], w_spatial_a[li], b_spatial_a[li], kv_p,
                              tm=512, rows_per_batch=seq)
            mix_s, v_rows = _sample_mix_a(zs, norm_v_a[li], w_spatial_a[li], b_spatial_a[li], kv_s)
            chunk_v_s.append(v_rows)
        else:
            zs, w_in = _norm_matmul(ys, norm_mix_pre[i], w_in_b, li, tm=SAMPLE_PAD, tn=512, emit_w=True,
                                    gelu_cols=0, out_dtype=F32, name="in_proj_b_s")
            zb = _in_proj_b(yp, norm_mix_pre[i], w_in, tm=1024)
            outs, lses = [], []
            for g, (win, dil) in enumerate(SWA_PATTERN):
                o, lse = _swa_group(zb, band_tables, g, batch=batch, seq=seq)
                outs.append(o)
                lses.append(lse)
                k_tail = _tail_rows(zb[N_SWA_GROUPS + g], win, dil, batch=batch, seq=seq)
                v_tail = _tail_rows(zb[2 * N_SWA_GROUPS + g], win, dil, batch=batch, seq=seq)
                win_p[g].append(jnp.stack([k_tail, v_tail], axis=2))
                kv_new = zs[:bd, MIXER_WIDTH:3 * MIXER_WIDTH]
                kv_new = kv_new.reshape(bd, 1, 2, N_SWA_GROUPS, HEADS_PER_GROUP, HEAD_DIM)[:, :, :, g]
                win_s[g].append(kv_new)
            mix_p = _swa_merge(outs, lses, zb, kv_p, rows_per_batch=seq)
            mix_s = _sample_mix_b(zs, [c[li] for c in win_caches], bias_groups, kv_s)
        mix_s = jnp.pad(mix_s.reshape(bd, D_MODEL), ((0, SAMPLE_PAD - bd), (0, 0))).astype(BF16)
        ys, w_o = _out_proj_cast(mix_s, w_out, i, ys, norm_mix_post[i], tn=512)
        yp = _out_proj(mix_p, w_o, yp, norm_mix_post[i], tm=512)
        ys, w_ffn = _ffn_cast(ys, norm_ffn_pre[i], norm_ffn_post[i], w_ffn_up, w_ffn_down, i, tf=512)
        yp = _ffn(yp, norm_ffn_pre[i], norm_ffn_post[i], w_ffn, tm=1024, tf=512, single_buffer=True)

    return (
        yp.reshape(batch, seq, D_MODEL),
        ys[:bd].reshape(bd, 1, D_MODEL),
        jnp.stack(mem_kv_p, axis=0),
        jnp.stack(chunk_v_s, axis=0),
        jnp.stack(win_p[0], axis=0),
        jnp.stack(win_p[1], axis=0),
        jnp.stack(win_p[2], axis=0),
        jnp.stack(win_s[0], axis=0),
        jnp.stack(win_s[1], axis=0),
        jnp.stack(win_s[2], axis=0),
    )
```

```python
import functools
import math

import jax
import jax.numpy as jnp
from jax import lax
from jax.experimental import pallas as pl
from jax.experimental.pallas import tpu as pltpu

F32 = jnp.float32
BF16 = jnp.bfloat16

D_MODEL = 2048
HEAD_DIM = 128
N_MEM = 256
N_MEM_HEADS = 4
MEM_WIDTH = N_MEM_HEADS * HEAD_DIM
MIXER_WIDTH = D_MODEL - MEM_WIDTH
CHUNK = 128
N_GROUPS_A = 4
GROUP_DIM_A = MIXER_WIDTH // N_GROUPS_A
SWA_PATTERN = ((128, 1), (512, 4), (2048, 16))
N_SWA_GROUPS = len(SWA_PATTERN)
HEADS_PER_GROUP = 4
GROUP_WIDTH = HEADS_PER_GROUP * HEAD_DIM
N_BACK = 128
N_BUCKETS = 32
MAX_EXACT = N_BUCKETS // 2
MAX_DISTANCE = 2048
D_FF = 5632
EPS = 1e-6
NEG_INF = -1e30
ATTN_SCALE = HEAD_DIM ** -0.5
SAMPLE_PAD = 16
PERM_BLOCK = 256
MERGE_TILE = 512

VMEM_LIMIT = 56 * 1024 * 1024


def _params(*sem):
    return pltpu.CompilerParams(dimension_semantics=sem, vmem_limit_bytes=VMEM_LIMIT)


def _gelu(x):
    return 0.5 * x * (1.0 + jnp.tanh(0.7978845608028654 * (x + 0.044715 * (x * x * x))))


def _rms(x, g):
    return x * lax.rsqrt(jnp.mean(x * x, axis=-1, keepdims=True) + EPS) * g


def _log2(n):
    assert n & (n - 1) == 0
    return n.bit_length() - 1


def _residue_major_perm(tm, dil, transpose=False):
    n = tm // dil
    row = lax.broadcasted_iota(jnp.int32, (tm, tm), 0)
    col = lax.broadcasted_iota(jnp.int32, (tm, tm), 1)
    dst, src = (col, row) if transpose else (row, col)
    want = lax.shift_left(jnp.bitwise_and(dst, n - 1), _log2(dil)) + lax.shift_right_logical(dst, _log2(n))
    return (src == want).astype(BF16)


def _norm_matmul_kernel(x_ref, g_ref, w_ref, o_ref, *rest, gelu_cols, emit_w):
    xn_ref = rest[-1]
    n = pl.program_id(1)

    @pl.when(n == 0)
    def _():
        xn_ref[...] = _rms(x_ref[...], g_ref[...]).astype(BF16)

    w = w_ref[...].astype(BF16)
    if emit_w:
        rest[0][...] = w
    acc = jnp.dot(xn_ref[...], w, preferred_element_type=F32)
    if gelu_cols > 0:
        col = lax.broadcasted_iota(jnp.int32, acc.shape, 1)
        acc = jnp.where(col < gelu_cols - n * acc.shape[1], _gelu(acc), acc)
    o_ref[...] = acc.astype(o_ref.dtype)


def _weight_spec(w, layer, block, index):
    if layer is None:
        return pl.BlockSpec(block, index)
    return pl.BlockSpec((None,) + block, lambda *ids: (layer,) + index(*ids))


def _norm_matmul(x, g, w, layer, *, tm, tn, gelu_cols, out_dtype, name, emit_w=False):
    m, k = x.shape
    n = w.shape[-1]
    out_specs = [pl.BlockSpec((tm, tn), lambda i, j: (i, j))]
    out_shape = [jax.ShapeDtypeStruct((m, n), out_dtype)]
    if emit_w:
        assert m == tm
        out_specs.append(pl.BlockSpec((k, tn), lambda i, j: (0, j)))
        out_shape.append(jax.ShapeDtypeStruct((k, n), BF16))
    outs = pl.pallas_call(
        functools.partial(_norm_matmul_kernel, gelu_cols=gelu_cols, emit_w=emit_w),
        grid=(m // tm, n // tn),
        in_specs=[
            pl.BlockSpec((tm, k), lambda i, j: (i, 0)),
            pl.BlockSpec((1, k), lambda i, j: (0, 0)),
            _weight_spec(w, layer, (k, tn), lambda i, j: (0, j)),
        ],
        out_specs=out_specs,
        out_shape=out_shape,
        scratch_shapes=[pltpu.VMEM((tm, k), BF16)],
        compiler_params=_params("parallel", "arbitrary"),
        name=name,
    )(x, g.reshape(1, k), w)
    return outs if emit_w else outs[0]


def _in_proj_b_kernel(x_ref, g_ref, w_ref, o_ref, xn_ref):
    tm = x_ref.shape[0]
    j = pl.program_id(1)

    @pl.when(j == 0)
    def _():
        xn = _rms(x_ref[...], g_ref[...]).astype(BF16)
        xn_ref[0] = xn
        for g in range(1, N_SWA_GROUPS):
            perm = _residue_major_perm(PERM_BLOCK, SWA_PATTERN[g][1])
            for s in range(0, tm, PERM_BLOCK):
                xn_ref[g, s:s + PERM_BLOCK, :] = jnp.dot(
                    perm, xn[s:s + PERM_BLOCK, :], preferred_element_type=F32).astype(BF16)

    for t in range(o_ref.shape[0]):
        tile = j * o_ref.shape[0] + t
        src = jnp.where(tile < 3 * N_SWA_GROUPS, tile % N_SWA_GROUPS, 0)
        w = w_ref[:, t * GROUP_WIDTH:(t + 1) * GROUP_WIDTH]
        o_ref[t] = jnp.dot(xn_ref[src], w, preferred_element_type=F32).astype(o_ref.dtype)


def _in_proj_b(x, g, w, *, tm, tiles_per_step):
    m, k = x.shape
    n_tiles = w.shape[1] // GROUP_WIDTH
    return pl.pallas_call(
        _in_proj_b_kernel,
        grid=(m // tm, n_tiles // tiles_per_step),
        in_specs=[
            pl.BlockSpec((tm, k), lambda i, j: (i, 0)),
            pl.BlockSpec((1, k), lambda i, j: (0, 0)),
            pl.BlockSpec((k, tiles_per_step * GROUP_WIDTH), lambda i, j: (0, j)),
        ],
        out_specs=pl.BlockSpec((tiles_per_step, tm, GROUP_WIDTH), lambda i, j: (j, i, 0)),
        out_shape=jax.ShapeDtypeStruct((n_tiles, m, GROUP_WIDTH), BF16),
        scratch_shapes=[pltpu.VMEM((N_SWA_GROUPS, tm, k), BF16)],
        compiler_params=_params("parallel", "arbitrary"),
        name="in_proj_b",
    )(x, g.reshape(1, k), w)


def _mem_attention_tile(q_ref, kv_ref, o_ref, col0):
    ones = jnp.ones((N_MEM, HEAD_DIM), BF16)
    for h in range(N_MEM_HEADS):
        lo, hi = h * HEAD_DIM, (h + 1) * HEAD_DIM
        q = q_ref[:, lo:hi]
        k = kv_ref[:, lo:hi].astype(BF16)
        v = kv_ref[:, MEM_WIDTH + lo:MEM_WIDTH + hi].astype(BF16)
        s = lax.dot_general(q, k, (((1,), (1,)), ((), ())), preferred_element_type=F32) * ATTN_SCALE
        m = jnp.max(s, axis=1, keepdims=True)
        p = jnp.exp(s - m).astype(BF16)
        ov = jnp.dot(p, jnp.concatenate([v, ones], axis=1), preferred_element_type=F32)
        o_ref[:, col0 + lo:col0 + hi] = (ov[:, :HEAD_DIM] / ov[:, HEAD_DIM:]).astype(o_ref.dtype)


def _gmlp_mix_kernel(u_ref, v_ref, q_ref, gv_ref, ws_ref, bs_ref, kv_ref, o_ref, vn_ref):
    tm = u_ref.shape[0]
    v = v_ref[...].astype(F32)
    vn_ref[...] = _rms(v, gv_ref[...]).astype(BF16)
    row = lax.broadcasted_iota(jnp.int32, (CHUNK, CHUNK), 0)
    col = lax.broadcasted_iota(jnp.int32, (CHUNK, CHUNK), 1)
    causal = row >= col
    for g in range(N_GROUPS_A):
        w = jnp.where(causal, ws_ref[g], 0.0).astype(BF16)
        b = bs_ref[:, g:g + 1]
        c0, c1 = g * GROUP_DIM_A, (g + 1) * GROUP_DIM_A
        for c in range(tm // CHUNK):
            r0, r1 = c * CHUNK, (c + 1) * CHUNK
            s = jnp.dot(w, vn_ref[r0:r1, c0:c1], preferred_element_type=F32) + b
            o_ref[r0:r1, c0:c1] = (u_ref[r0:r1, c0:c1].astype(F32) * s).astype(o_ref.dtype)
    _mem_attention_tile(q_ref, kv_ref, o_ref, MIXER_WIDTH)


def _gmlp_mix(zact, g_v, w_s, b_s, mem_kv, *, tm, rows_per_batch):
    m = zact.shape[0]
    tiles_per_batch = rows_per_batch // tm
    return pl.pallas_call(
        _gmlp_mix_kernel,
        grid=(m // tm,),
        in_specs=[
            pl.BlockSpec((tm, MIXER_WIDTH), lambda i: (i, 0)),
            pl.BlockSpec((tm, MIXER_WIDTH), lambda i: (i, 1)),
            pl.BlockSpec((tm, MEM_WIDTH), lambda i: (i, 2 * MIXER_WIDTH // MEM_WIDTH)),
            pl.BlockSpec((1, MIXER_WIDTH), lambda i: (0, 0)),
            pl.BlockSpec((N_GROUPS_A, CHUNK, CHUNK), lambda i: (0, 0, 0)),
            pl.BlockSpec((CHUNK, N_GROUPS_A), lambda i: (0, 0)),
            pl.BlockSpec((None, N_MEM, 2 * MEM_WIDTH), lambda i: (i // tiles_per_batch, 0, 0)),
        ],
        out_specs=pl.BlockSpec((tm, D_MODEL), lambda i: (i, 0)),
        out_shape=jax.ShapeDtypeStruct((m, D_MODEL), BF16),
        scratch_shapes=[pltpu.VMEM((tm, MIXER_WIDTH), BF16)],
        compiler_params=_params("parallel"),
        name="gmlp_mix",
    )(zact, zact, zact, g_v.reshape(1, MIXER_WIDTH), w_s, b_s.T, mem_kv)


def _out_proj_kernel(mix_ref, w_ref, x_ref, g_ref, o_ref):
    o = jnp.dot(mix_ref[...], w_ref[...], preferred_element_type=F32)
    o_ref[...] = x_ref[...] + _rms(o, g_ref[...])


def _out_proj(mix, w, x, g, *, tm):
    m = x.shape[0]
    return pl.pallas_call(
        _out_proj_kernel,
        grid=(m // tm,),
        in_specs=[
            pl.BlockSpec((tm, D_MODEL), lambda i: (i, 0)),
            pl.BlockSpec((D_MODEL, D_MODEL), lambda i: (0, 0)),
            pl.BlockSpec((tm, D_MODEL), lambda i: (i, 0)),
            pl.BlockSpec((1, D_MODEL), lambda i: (0, 0)),
        ],
        out_specs=pl.BlockSpec((tm, D_MODEL), lambda i: (i, 0)),
        out_shape=jax.ShapeDtypeStruct((m, D_MODEL), F32),
        compiler_params=_params("parallel"),
        name="out_proj",
    )(mix, w, x, g.reshape(1, D_MODEL))


def _out_proj_cast_kernel(mix_ref, w_ref, x_ref, g_ref, o_ref, wb_ref, acc_ref):
    j = pl.program_id(0)
    w = w_ref[...].astype(BF16)
    wb_ref[...] = w
    acc_ref[j] = jnp.dot(mix_ref[...], w, preferred_element_type=F32)

    @pl.when(j == pl.num_programs(0) - 1)
    def _():
        o = jnp.concatenate([acc_ref[t] for t in range(acc_ref.shape[0])], axis=1)
        o_ref[...] = x_ref[...] + _rms(o, g_ref[...])


def _out_proj_cast(mix, w, layer, x, g, *, tn):
    m = x.shape[0]
    n_tiles = D_MODEL // tn
    return pl.pallas_call(
        _out_proj_cast_kernel,
        grid=(n_tiles,),
        in_specs=[
            pl.BlockSpec((m, D_MODEL), lambda j: (0, 0)),
            pl.BlockSpec((None, D_MODEL, tn), lambda j: (layer, 0, j)),
            pl.BlockSpec((m, D_MODEL), lambda j: (0, 0)),
            pl.BlockSpec((1, D_MODEL), lambda j: (0, 0)),
        ],
        out_specs=[
            pl.BlockSpec((m, D_MODEL), lambda j: (0, 0)),
            pl.BlockSpec((D_MODEL, tn), lambda j: (0, j)),
        ],
        out_shape=[
            jax.ShapeDtypeStruct((m, D_MODEL), F32),
            jax.ShapeDtypeStruct((D_MODEL, D_MODEL), BF16),
        ],
        scratch_shapes=[pltpu.VMEM((n_tiles, m, tn), F32)],
        compiler_params=_params("arbitrary"),
        name="out_proj_cast",
    )(mix, w, x, g.reshape(1, D_MODEL))


def _ffn_kernel(x_ref, gpre_ref, gpost_ref, wg_ref, wl_ref, wd_ref, o_ref, *rest, emit_w):
    xn_ref = rest[-1]
    f = pl.program_id(1)

    @pl.when(f == 0)
    def _():
        xn_ref[...] = _rms(x_ref[...], gpre_ref[...]).astype(BF16)
        o_ref[...] = jnp.zeros_like(o_ref)

    wg, wl, wd = (r[...].astype(BF16) for r in (wg_ref, wl_ref, wd_ref))
    if emit_w:
        for r, w in zip(rest[:3], (wg, wl, wd)):
            r[...] = w
    xn = xn_ref[...]
    hg = jnp.dot(xn, wg, preferred_element_type=F32)
    hl = jnp.dot(xn, wl, preferred_element_type=F32)
    a = (hg * jax.nn.sigmoid(hg) * hl).astype(BF16)
    o_ref[...] += jnp.dot(a, wd, preferred_element_type=F32)

    @pl.when(f == pl.num_programs(1) - 1)
    def _():
        o_ref[...] = x_ref[...] + _rms(o_ref[...], gpost_ref[...])


def _ffn(x, g_pre, g_post, weights, *, tm, tf, single_buffer):
    m = x.shape[0]
    nf = D_FF // tf
    mode = dict(pipeline_mode=pl.Buffered(1)) if single_buffer else {}
    return pl.pallas_call(
        functools.partial(_ffn_kernel, emit_w=False),
        grid=(m // tm, nf),
        in_specs=[
            pl.BlockSpec((tm, D_MODEL), lambda i, f: (i, 0)),
            pl.BlockSpec((1, D_MODEL), lambda i, f: (0, 0)),
            pl.BlockSpec((1, D_MODEL), lambda i, f: (0, 0)),
            pl.BlockSpec((D_MODEL, tf), lambda i, f: (0, f)),
            pl.BlockSpec((D_MODEL, tf), lambda i, f: (0, f)),
            pl.BlockSpec((tf, D_MODEL), lambda i, f: (f, 0)),
        ],
        out_specs=pl.BlockSpec((tm, D_MODEL), lambda i, f: (i, 0), **mode),
        out_shape=jax.ShapeDtypeStruct((m, D_MODEL), F32),
        scratch_shapes=[pltpu.VMEM((tm, D_MODEL), BF16)],
        compiler_params=_params("parallel", "arbitrary"),
        name="ffn",
    )(x, g_pre.reshape(1, D_MODEL), g_post.reshape(1, D_MODEL), *weights)


def _ffn_cast(x, g_pre, g_post, w_up, w_down, layer, *, tf):
    m = x.shape[0]
    nf = D_FF // tf
    outs = pl.pallas_call(
        functools.partial(_ffn_kernel, emit_w=True),
        grid=(1, nf),
        in_specs=[
            pl.BlockSpec((m, D_MODEL), lambda i, f: (0, 0)),
            pl.BlockSpec((1, D_MODEL), lambda i, f: (0, 0)),
            pl.BlockSpec((1, D_MODEL), lambda i, f: (0, 0)),
            pl.BlockSpec((None, D_MODEL, tf), lambda i, f: (layer, 0, f)),
            pl.BlockSpec((None, D_MODEL, tf), lambda i, f: (layer, 0, nf + f)),
            pl.BlockSpec((None, tf, D_MODEL), lambda i, f: (layer, f, 0)),
        ],
        out_specs=[
            pl.BlockSpec((m, D_MODEL), lambda i, f: (0, 0)),
            pl.BlockSpec((D_MODEL, tf), lambda i, f: (0, f)),
            pl.BlockSpec((D_MODEL, tf), lambda i, f: (0, f)),
            pl.BlockSpec((tf, D_MODEL), lambda i, f: (f, 0)),
        ],
        out_shape=[
            jax.ShapeDtypeStruct((m, D_MODEL), F32),
            jax.ShapeDtypeStruct((D_MODEL, D_FF), BF16),
            jax.ShapeDtypeStruct((D_MODEL, D_FF), BF16),
            jax.ShapeDtypeStruct((D_FF, D_MODEL), BF16),
        ],
        scratch_shapes=[pltpu.VMEM((m, D_MODEL), BF16)],
        compiler_params=_params("parallel", "arbitrary"),
        name="ffn_cast",
    )(x, g_pre.reshape(1, D_MODEL), g_post.reshape(1, D_MODEL), w_up, w_up, w_down)
    return outs[0], tuple(outs[1:])


def _swa_kernel(q_ref, k_ref, v_ref, tb_ref, o_ref, lse_ref):
    n_units, n_res, u, _ = q_ref.shape
    per_blk = N_BACK // u
    n_blk = n_units // per_blk
    lane = lax.broadcasted_iota(jnp.int32, (N_BACK, HEAD_DIM), 1)
    ones = jnp.ones((2 * N_BACK, HEAD_DIM), BF16)

    def rows(ref, res, unit0, n_rows, lo, hi):
        return ref[pl.ds(unit0, n_rows // u), res, :, lo:hi].reshape(n_rows, hi - lo)

    def block(res, qu, ku, table):
        n_keys = N_BACK if table == 0 else 2 * N_BACK
        lse_tile = jnp.zeros((N_BACK, HEAD_DIM), F32)
        for h in range(HEADS_PER_GROUP):
            lo, hi = h * HEAD_DIM, (h + 1) * HEAD_DIM
            q = rows(q_ref, res, qu, N_BACK, lo, hi)
            kw = rows(k_ref, res, ku, n_keys, lo, hi)
            vw = rows(v_ref, res, ku, n_keys, lo, hi)
            s = lax.dot_general(q, kw, (((1,), (1,)), ((), ())), preferred_element_type=F32)
            s = s * ATTN_SCALE + tb_ref[table, h][:, :n_keys]
            m = jnp.max(s, axis=1, keepdims=True)
            p = jnp.exp(s - m).astype(BF16)
            ov = jnp.dot(p, jnp.concatenate([vw, ones[:n_keys]], axis=1), preferred_element_type=F32)
            den = ov[:, HEAD_DIM:]
            o = ov[:, :HEAD_DIM] / den
            o_ref[pl.ds(qu, per_blk), res, :, lo:hi] = o.reshape(per_blk, u, HEAD_DIM).astype(o_ref.dtype)
            lse_tile = jnp.where(lane == h, m + jnp.log(den), lse_tile)
        lse_ref[pl.ds(qu, per_blk), res, :, :] = lse_tile.reshape(per_blk, u, HEAD_DIM)

    def block_at(res, n):
        block(res, n * per_blk, (n - 1) * per_blk, 1)

    n_pairs = (n_blk - 1) // 2
    for res in range(n_res):
        block(res, 0, 0, 0)

        def body(i, carry, res=res):
            block_at(res, 1 + 2 * i)
            block_at(res, 2 + 2 * i)
            return carry

        if n_pairs > 0:
            lax.fori_loop(0, n_pairs, body, 0)
        if (n_blk - 1) % 2 == 1:
            block_at(res, n_blk - 1)


def _sub_block(dil):
    return N_BACK if dil == 1 else PERM_BLOCK


def _swa_group(zb, tables, g, *, batch, seq):
    dil = SWA_PATTERN[g][1]
    sub = _sub_block(dil)
    n_units, u = seq // sub, sub // dil
    n_res = 4 if n_units * u == 2 * N_BACK else 1
    view = zb.reshape(zb.shape[0], batch, n_units, dil, u, GROUP_WIDTH)

    def rows_in(tile):
        return pl.BlockSpec((None, None, n_units, n_res, u, GROUP_WIDTH), lambda i, r: (tile, i, 0, r, 0, 0))

    def rows_out(width):
        return pl.BlockSpec((None, n_units, n_res, u, width), lambda i, r: (i, 0, r, 0, 0))

    return pl.pallas_call(
        _swa_kernel,
        grid=(batch, dil // n_res),
        in_specs=[rows_in(g), rows_in(N_SWA_GROUPS + g), rows_in(2 * N_SWA_GROUPS + g),
                  pl.BlockSpec((None, 2, HEADS_PER_GROUP, N_BACK, 2 * N_BACK), lambda i, r: (g, 0, 0, 0, 0))],
        out_specs=[rows_out(GROUP_WIDTH), rows_out(HEAD_DIM)],
        out_shape=[
            jax.ShapeDtypeStruct((batch, n_units, dil, u, GROUP_WIDTH), BF16),
            jax.ShapeDtypeStruct((batch, n_units, dil, u, HEAD_DIM), F32),
        ],
        compiler_params=_params("parallel", "parallel"),
        name=f"swa_group{g}",
    )(view, view, view, tables)


def _split3(x):
    hi = x.astype(BF16)
    rest = x - hi.astype(F32)
    mid = rest.astype(BF16)
    lo = (rest - mid.astype(F32)).astype(BF16)
    return hi, mid, lo


def _swa_merge_kernel(o0_ref, o1_ref, o2_ref, l0_ref, l1_ref, l2_ref, q_ref, kv_ref, o_ref):
    tm = o_ref.shape[0]
    outs, lses = [], []
    for g, (o_g, l_g) in enumerate(((o0_ref, l0_ref), (o1_ref, l1_ref), (o2_ref, l2_ref))):
        dil = SWA_PATTERN[g][1]
        o = o_g[...].reshape(tm, GROUP_WIDTH)
        l = l_g[...].reshape(tm, HEAD_DIM)
        if dil > 1:
            inv = _residue_major_perm(PERM_BLOCK, dil, transpose=True)
            l3 = _split3(l)
            o_nat, l_nat = [], []
            for s in range(0, tm, PERM_BLOCK):
                o_nat.append(jnp.dot(inv, o[s:s + PERM_BLOCK, :], preferred_element_type=F32))
                l_nat.append(sum(jnp.dot(inv, t[s:s + PERM_BLOCK, :], preferred_element_type=F32) for t in l3))
            o = jnp.concatenate(o_nat, axis=0)
            l = jnp.concatenate(l_nat, axis=0)
        outs.append(o.astype(F32))
        lses.append(l)
    for h in range(HEADS_PER_GROUP):
        lo, hi = h * HEAD_DIM, (h + 1) * HEAD_DIM
        ls = [l[:, h:h + 1] for l in lses]
        mx = jnp.maximum(jnp.maximum(ls[0], ls[1]), ls[2])
        es = [jnp.exp(l - mx) for l in ls]
        tot = es[0] + es[1] + es[2]
        for g in range(N_SWA_GROUPS):
            alpha = es[g] / tot
            o_ref[:, g * GROUP_WIDTH + lo:g * GROUP_WIDTH + hi] = (outs[g][:, lo:hi] * alpha).astype(o_ref.dtype)
    _mem_attention_tile(q_ref, kv_ref, o_ref, MIXER_WIDTH)


def _swa_merge(outs, lses, zb, mem_kv, *, rows_per_batch):
    m = zb.shape[1]
    tm = MERGE_TILE
    tiles_per_batch = rows_per_batch // tm

    def tile(width, dil):
        sub = _sub_block(dil)
        return pl.BlockSpec((None, tm // sub, dil, sub // dil, width),
                            lambda i: (i // tiles_per_batch, i % tiles_per_batch, 0, 0, 0))

    return pl.pallas_call(
        _swa_merge_kernel,
        grid=(m // tm,),
        in_specs=(
            [tile(GROUP_WIDTH, dil) for _, dil in SWA_PATTERN]
            + [tile(HEAD_DIM, dil) for _, dil in SWA_PATTERN]
            + [pl.BlockSpec((None, tm, MEM_WIDTH), lambda i: (3 * N_SWA_GROUPS, i, 0)),
               pl.BlockSpec((None, N_MEM, 2 * MEM_WIDTH), lambda i: (i // tiles_per_batch, 0, 0))]
        ),
        out_specs=pl.BlockSpec((tm, D_MODEL), lambda i: (i, 0)),
        out_shape=jax.ShapeDtypeStruct((m, D_MODEL), BF16),
        compiler_params=_params("parallel"),
        name="swa_merge",
    )(*outs, *lses, zb, mem_kv)


def _sample_mem_attention(q_row, kv_ref, o_ref, col0):
    for h in range(N_MEM_HEADS):
        lo, hi = h * HEAD_DIM, (h + 1) * HEAD_DIM
        q = q_row[:, lo:hi]
        k = kv_ref[:, lo:hi]
        v = kv_ref[:, MEM_WIDTH + lo:MEM_WIDTH + hi]
        s = jnp.sum(k * q, axis=1, keepdims=True) * ATTN_SCALE
        m = jnp.max(s, axis=0, keepdims=True)
        p = jnp.exp(s - m)
        den = jnp.sum(p, axis=0, keepdims=True)
        o_ref[:, col0 + lo:col0 + hi] = jnp.sum(p * v, axis=0, keepdims=True) / den


def _sample_mix_a_kernel(z_ref, gv_ref, w0_ref, b0_ref, kv_ref, o_ref, vrow_ref):
    u = z_ref[:, 0:MIXER_WIDTH]
    v = _rms(z_ref[:, MIXER_WIDTH:2 * MIXER_WIDTH], gv_ref[...])
    vrow_ref[...] = v
    o_ref[:, 0:MIXER_WIDTH] = u * (w0_ref[...] * v + b0_ref[...])
    _sample_mem_attention(z_ref[:, 2 * MIXER_WIDTH:2 * MIXER_WIDTH + MEM_WIDTH], kv_ref, o_ref, MIXER_WIDTH)


def _sample_mix_a(z, g_v, w_s, b_s, mem_kv):
    bd = mem_kv.shape[0]
    w0 = jnp.repeat(w_s[:, 0, 0], GROUP_DIM_A).reshape(1, MIXER_WIDTH)
    b0 = jnp.repeat(b_s[:, 0], GROUP_DIM_A).reshape(1, MIXER_WIDTH)
    width = z.shape[1]
    vec = lambda i: (0, 0)
    return pl.pallas_call(
        _sample_mix_a_kernel,
        grid=(bd,),
        in_specs=[
            pl.BlockSpec((None, 1, width), lambda i: (i, 0, 0)),
            pl.BlockSpec((1, MIXER_WIDTH), vec),
            pl.BlockSpec((1, MIXER_WIDTH), vec),
            pl.BlockSpec((1, MIXER_WIDTH), vec),
            pl.BlockSpec((None, N_MEM, 2 * MEM_WIDTH), lambda i: (i, 0, 0)),
        ],
        out_specs=[
            pl.BlockSpec((None, 1, D_MODEL), lambda i: (i, 0, 0)),
            pl.BlockSpec((None, 1, MIXER_WIDTH), lambda i: (i, 0, 0)),
        ],
        out_shape=[
            jax.ShapeDtypeStruct((bd, 1, D_MODEL), F32),
            jax.ShapeDtypeStruct((bd, 1, MIXER_WIDTH), F32),
        ],
        compiler_params=_params("parallel"),
        name="sample_mix_a",
    )(z[:bd].reshape(bd, 1, width), g_v.reshape(1, MIXER_WIDTH), w0, b0, mem_kv)


def _sample_mix_b_kernel(z_ref, c0_ref, c1_ref, c2_ref, bcol_ref, bnew_ref, kv_ref, o_ref):
    caches = (c0_ref, c1_ref, c2_ref)
    outs = [[None] * HEADS_PER_GROUP for _ in range(N_SWA_GROUPS)]
    lses = [[None] * HEADS_PER_GROUP for _ in range(N_SWA_GROUPS)]
    for g in range(N_SWA_GROUPS):
        for h in range(HEADS_PER_GROUP):
            hd = g * HEADS_PER_GROUP + h
            lo, hi = h * HEAD_DIM, (h + 1) * HEAD_DIM
            q = z_ref[:, hd * HEAD_DIM:(hd + 1) * HEAD_DIM]
            k_new = z_ref[:, MIXER_WIDTH + hd * HEAD_DIM:MIXER_WIDTH + (hd + 1) * HEAD_DIM]
            v_new = z_ref[:, 2 * MIXER_WIDTH + hd * HEAD_DIM:2 * MIXER_WIDTH + (hd + 1) * HEAD_DIM]
            kc = caches[g][:, lo:hi]
            vc = caches[g][:, GROUP_WIDTH + lo:GROUP_WIDTH + hi]
            s_c = jnp.sum(kc * q, axis=1, keepdims=True) * ATTN_SCALE + bcol_ref[g][:, h:h + 1]
            s_n = jnp.sum(k_new * q, axis=1, keepdims=True) * ATTN_SCALE + bnew_ref[:, hd:hd + 1]
            m = jnp.maximum(jnp.max(s_c, axis=0, keepdims=True), s_n)
            p_c = jnp.exp(s_c - m)
            p_n = jnp.exp(s_n - m)
            den = jnp.sum(p_c, axis=0, keepdims=True) + p_n
            outs[g][h] = (jnp.sum(p_c * vc, axis=0, keepdims=True) + p_n * v_new) / den
            lses[g][h] = m + jnp.log(den)
    for h in range(HEADS_PER_GROUP):
        ls = [lses[g][h] for g in range(N_SWA_GROUPS)]
        mx = jnp.maximum(jnp.maximum(ls[0], ls[1]), ls[2])
        es = [jnp.exp(l - mx) for l in ls]
        tot = es[0] + es[1] + es[2]
        for g in range(N_SWA_GROUPS):
            c0 = g * GROUP_WIDTH + h * HEAD_DIM
            o_ref[:, c0:c0 + HEAD_DIM] = outs[g][h] * (es[g] / tot)
    _sample_mem_attention(z_ref[:, 3 * MIXER_WIDTH:3 * MIXER_WIDTH + MEM_WIDTH], kv_ref, o_ref, MIXER_WIDTH)


def _sample_mix_b(z, win_caches, bias_groups, mem_kv):
    bd = mem_kv.shape[0]
    width = z.shape[1]
    kv_width = 2 * GROUP_WIDTH
    cache_views, cache_specs = [], []
    for g, (win, dil) in enumerate(SWA_PATTERN):
        cache_views.append(win_caches[g][:, ::dil].reshape(bd, win // dil, kv_width))
        cache_specs.append(pl.BlockSpec((None, N_BACK, kv_width), lambda i: (i, 0, 0)))
    bcol = jnp.stack([bg[:, N_BACK:0:-1].T for bg in bias_groups], axis=0)
    bnew = jnp.concatenate([bg[:, 0] for bg in bias_groups])
    bnew = jnp.pad(bnew, (0, HEAD_DIM - bnew.shape[0])).reshape(1, HEAD_DIM)
    return pl.pallas_call(
        _sample_mix_b_kernel,
        grid=(bd,),
        in_specs=[pl.BlockSpec((None, 1, width), lambda i: (i, 0, 0))] + cache_specs + [
            pl.BlockSpec((N_SWA_GROUPS, N_BACK, HEADS_PER_GROUP), lambda i: (0, 0, 0)),
            pl.BlockSpec((1, HEAD_DIM), lambda i: (0, 0)),
            pl.BlockSpec((None, N_MEM, 2 * MEM_WIDTH), lambda i: (i, 0, 0)),
        ],
        out_specs=pl.BlockSpec((None, 1, D_MODEL), lambda i: (i, 0, 0)),
        out_shape=jax.ShapeDtypeStruct((bd, 1, D_MODEL), F32),
        compiler_params=_params("parallel"),
        name="sample_mix_b",
    )(z[:bd].reshape(bd, 1, width), *cache_views, bcol, bnew, mem_kv)


def _t5_bucket(dist):
    nf = jnp.maximum(dist, MAX_EXACT).astype(F32)
    large = MAX_EXACT + (jnp.log(nf / MAX_EXACT) / math.log(MAX_DISTANCE / MAX_EXACT)
                         * (N_BUCKETS - MAX_EXACT)).astype(jnp.int32)
    large = jnp.minimum(large, N_BUCKETS - 1)
    return jnp.where(dist < MAX_EXACT, dist, large)


def _group_bias(rel_bias, g, dil):
    dist = jnp.arange(N_BACK + 1, dtype=jnp.int32) * dil
    b = rel_bias[_t5_bucket(dist)][:, g * HEADS_PER_GROUP:(g + 1) * HEADS_PER_GROUP]
    return b.T.astype(F32)


def _band_tables(bias_groups):
    width = 2 * N_BACK
    rows = []
    for bias_j in bias_groups:
        masked = jnp.full((HEADS_PER_GROUP, N_BACK - 1), NEG_INF, F32)
        rows.append(jnp.concatenate([bias_j[:, :1], masked, bias_j[:, N_BACK:0:-1]], axis=1))
        rows.append(jnp.concatenate([bias_j[:, ::-1], masked], axis=1))
    base = jnp.stack(rows, axis=0).reshape(-1, width)
    pitch = 2 * width - 1
    flat = jnp.tile(base, (1, width))[:, :N_BACK * pitch]
    tabs = flat.reshape(-1, N_BACK, pitch)[:, :, :width]
    return tabs.reshape(N_SWA_GROUPS, 2, HEADS_PER_GROUP, N_BACK, width)


def _tail_rows(rows, win, dil, *, batch, seq):
    sub = _sub_block(dil)
    a = rows.reshape(batch, seq // sub, dil, sub // dil, GROUP_WIDTH)[:, (seq - win) // sub:]
    return a.transpose(0, 1, 3, 2, 4).reshape(batch, win, HEADS_PER_GROUP, HEAD_DIM).astype(F32)


def kernel(x_prompt, x_sample, mem_prompt, cache_mem_kv, cache_win128_kv, cache_win512_kv, cache_win2048_kv, rel_bias, norm_mix_pre, norm_mix_post, norm_ffn_pre, norm_ffn_post, norm_mem, w_mem_kv, w_in_a, norm_v_a, w_spatial_a, b_spatial_a, w_in_b, w_out, w_ffn_up, w_ffn_down):
    batch, seq, _ = x_prompt.shape
    bd = x_sample.shape[0]
    depth = w_out.shape[0]
    m_p = batch * seq
    win_caches = (cache_win128_kv, cache_win512_kv, cache_win2048_kv)

    bias_groups = [_group_bias(rel_bias, g, dil) for g, (_, dil) in enumerate(SWA_PATTERN)]
    band_tables = _band_tables(bias_groups)

    yp = x_prompt.reshape(m_p, D_MODEL)
    ys = jnp.pad(x_sample.reshape(bd, D_MODEL), ((0, SAMPLE_PAD - bd), (0, 0)))
    mem_rows = mem_prompt.reshape(batch * N_MEM, D_MODEL)

    mem_kv_p, chunk_v_s = [], []
    win_p = [[] for _ in SWA_PATTERN]
    win_s = [[] for _ in SWA_PATTERN]
    for i in range(depth):
        li = i // 2
        kv_p = _norm_matmul(mem_rows, norm_mem[i], w_mem_kv, i, tm=batch * N_MEM, tn=512,
                            gelu_cols=0, out_dtype=F32, name="mem_kv").reshape(batch, N_MEM, 2 * MEM_WIDTH)
        kv_s = cache_mem_kv[i].reshape(bd, N_MEM, 2 * MEM_WIDTH)
        mem_kv_p.append(kv_p.reshape(batch, N_MEM, 2, N_MEM_HEADS, HEAD_DIM))
        if i % 2 == 0:
            zs, w_in = _norm_matmul(ys, norm_mix_pre[i], w_in_a, li, tm=SAMPLE_PAD, tn=512, emit_w=True,
                                    gelu_cols=2 * MIXER_WIDTH, out_dtype=F32, name="in_proj_a_s")
            zp = _norm_matmul(yp, norm_mix_pre[i], w_in, None, tm=512, tn=1792,
                              gelu_cols=2 * MIXER_WIDTH, out_dtype=BF16, name="in_proj_a")
            mix_p = _gmlp_mix(zp, norm_v_a[li], w_spatial_a[li], b_spatial_a[li], kv_p,
                              tm=512, rows_per_batch=seq)
            mix_s, v_rows = _sample_mix_a(zs, norm_v_a[li], w_spatial_a[li], b_spatial_a[li], kv_s)
            chunk_v_s.append(v_rows)
        else:
            zs, w_in = _norm_matmul(ys, norm_mix_pre[i], w_in_b, li, tm=SAMPLE_PAD, tn=512, emit_w=True,
                                    gelu_cols=0, out_dtype=F32, name="in_proj_b_s")
            zb = _in_proj_b(yp, norm_mix_pre[i], w_in, tm=1024, tiles_per_step=2)
            outs, lses = [], []
            for g, (win, dil) in enumerate(SWA_PATTERN):
                o, lse = _swa_group(zb, band_tables, g, batch=batch, seq=seq)
                outs.append(o)
                lses.append(lse)
                k_tail = _tail_rows(zb[N_SWA_GROUPS + g], win, dil, batch=batch, seq=seq)
                v_tail = _tail_rows(zb[2 * N_SWA_GROUPS + g], win, dil, batch=batch, seq=seq)
                win_p[g].append(jnp.stack([k_tail, v_tail], axis=2))
                kv_new = zs[:bd, MIXER_WIDTH:3 * MIXER_WIDTH]
                kv_new = kv_new.reshape(bd, 1, 2, N_SWA_GROUPS, HEADS_PER_GROUP, HEAD_DIM)[:, :, :, g]
                win_s[g].append(kv_new)
            mix_p = _swa_merge(outs, lses, zb, kv_p, rows_per_batch=seq)
            mix_s = _sample_mix_b(zs, [c[li] for c in win_caches], bias_groups, kv_s)
        mix_s = jnp.pad(mix_s.reshape(bd, D_MODEL), ((0, SAMPLE_PAD - bd), (0, 0))).astype(BF16)
        ys, w_o = _out_proj_cast(mix_s, w_out, i, ys, norm_mix_post[i], tn=512)
        yp = _out_proj(mix_p, w_o, yp, norm_mix_post[i], tm=512)
        ys, w_ffn = _ffn_cast(ys, norm_ffn_pre[i], norm_ffn_post[i], w_ffn_up, w_ffn_down, i, tf=512)
        yp = _ffn(yp, norm_ffn_pre[i], norm_ffn_post[i], w_ffn, tm=1024, tf=512, single_buffer=True)

    return (
        yp.reshape(batch, seq, D_MODEL),
        ys[:bd].reshape(bd, 1, D_MODEL),
        jnp.stack(mem_kv_p, axis=0),
        jnp.stack(chunk_v_s, axis=0),
        jnp.stack(win_p[0], axis=0),
        jnp.stack(win_p[1], axis=0),
        jnp.stack(win_p[2], axis=0),
        jnp.stack(win_s[0], axis=0),
        jnp.stack(win_s[1], axis=0),
        jnp.stack(win_s[2], axis=0),
    )
```

```python
import functools
import math

import jax
import jax.numpy as jnp
from jax import lax
from jax.experimental import pallas as pl
from jax.experimental.pallas import tpu as pltpu

F32 = jnp.float32
BF16 = jnp.bfloat16

D_MODEL = 2048
HEAD_DIM = 128
N_MEM = 256
N_MEM_HEADS = 4
MEM_WIDTH = N_MEM_HEADS * HEAD_DIM
MIXER_WIDTH = D_MODEL - MEM_WIDTH
CHUNK = 128
N_GROUPS_A = 4
GROUP_DIM_A = MIXER_WIDTH // N_GROUPS_A
SWA_PATTERN = ((128, 1), (512, 4), (2048, 16))
N_SWA_GROUPS = len(SWA_PATTERN)
HEADS_PER_GROUP = 4
GROUP_WIDTH = HEADS_PER_GROUP * HEAD_DIM
N_BACK = 128
N_BUCKETS = 32
MAX_EXACT = N_BUCKETS // 2
MAX_DISTANCE = 2048
D_FF = 5632
EPS = 1e-6
NEG_INF = -1e30
ATTN_SCALE = HEAD_DIM ** -0.5
SAMPLE_PAD = 16
PERM_BLOCK = 256
MERGE_TILE = 512

VMEM_LIMIT = 56 * 1024 * 1024


def _params(*sem):
    return pltpu.CompilerParams(dimension_semantics=sem, vmem_limit_bytes=VMEM_LIMIT)


def _gelu(x):
    return 0.5 * x * (1.0 + jnp.tanh(0.7978845608028654 * (x + 0.044715 * (x * x * x))))


def _rms(x, g):
    return x * lax.rsqrt(jnp.mean(x * x, axis=-1, keepdims=True) + EPS) * g


def _log2(n):
    assert n & (n - 1) == 0
    return n.bit_length() - 1


def _residue_major_perm(tm, dil, transpose=False):
    n = tm // dil
    row = lax.broadcasted_iota(jnp.int32, (tm, tm), 0)
    col = lax.broadcasted_iota(jnp.int32, (tm, tm), 1)
    dst, src = (col, row) if transpose else (row, col)
    want = lax.shift_left(jnp.bitwise_and(dst, n - 1), _log2(dil)) + lax.shift_right_logical(dst, _log2(n))
    return (src == want).astype(BF16)


def _norm_matmul_kernel(x_ref, g_ref, w_ref, o_ref, *rest, gelu_cols, emit_w):
    xn_ref = rest[-1]
    n = pl.program_id(1)

    @pl.when(n == 0)
    def _():
        xn_ref[...] = _rms(x_ref[...], g_ref[...]).astype(BF16)

    w = w_ref[...].astype(BF16)
    if emit_w:
        rest[0][...] = w
    acc = jnp.dot(xn_ref[...], w, preferred_element_type=F32)
    if gelu_cols > 0:
        col = lax.broadcasted_iota(jnp.int32, acc.shape, 1)
        acc = jnp.where(col < gelu_cols - n * acc.shape[1], _gelu(acc), acc)
    o_ref[...] = acc.astype(o_ref.dtype)


def _weight_spec(w, layer, block, index):
    if layer is None:
        return pl.BlockSpec(block, index)
    return pl.BlockSpec((None,) + block, lambda *ids: (layer,) + index(*ids))


def _norm_matmul(x, g, w, layer, *, tm, tn, gelu_cols, out_dtype, name, emit_w=False):
    m, k = x.shape
    n = w.shape[-1]
    out_specs = [pl.BlockSpec((tm, tn), lambda i, j: (i, j))]
    out_shape = [jax.ShapeDtypeStruct((m, n), out_dtype)]
    if emit_w:
        assert m == tm
        out_specs.append(pl.BlockSpec((k, tn), lambda i, j: (0, j)))
        out_shape.append(jax.ShapeDtypeStruct((k, n), BF16))
    outs = pl.pallas_call(
        functools.partial(_norm_matmul_kernel, gelu_cols=gelu_cols, emit_w=emit_w),
        grid=(m // tm, n // tn),
        in_specs=[
            pl.BlockSpec((tm, k), lambda i, j: (i, 0)),
            pl.BlockSpec((1, k), lambda i, j: (0, 0)),
            _weight_spec(w, layer, (k, tn), lambda i, j: (0, j)),
        ],
        out_specs=out_specs,
        out_shape=out_shape,
        scratch_shapes=[pltpu.VMEM((tm, k), BF16)],
        compiler_params=_params("parallel", "arbitrary"),
        name=name,
    )(x, g.reshape(1, k), w)
    return outs if emit_w else outs[0]


def _mem_kv_kernel(x_ref, g_ref, w_ref, o_ref, xn_ref):
    @pl.when(pl.program_id(1) == 0)
    def _():
        xn_ref[...] = _rms(x_ref[...], g_ref[...]).astype(BF16)

    acc = jnp.dot(xn_ref[...], w_ref[...].astype(BF16), preferred_element_type=F32)
    for h in range(N_MEM_HEADS):
        o_ref[:, h, :] = acc[:, h * HEAD_DIM:(h + 1) * HEAD_DIM]


def _mem_kv(mem_rows, g, w, *, batch):
    m, k = mem_rows.shape
    layers = w.shape[0]
    out = pl.pallas_call(
        _mem_kv_kernel,
        grid=(layers, 2),
        in_specs=[
            pl.BlockSpec((m, k), lambda l, j: (0, 0)),
            pl.BlockSpec((None, 1, k), lambda l, j: (l, 0, 0)),
            pl.BlockSpec((None, k, MEM_WIDTH), lambda l, j: (l, 0, j)),
        ],
        out_specs=pl.BlockSpec((None, m, None, N_MEM_HEADS, HEAD_DIM), lambda l, j: (l, 0, j, 0, 0)),
        out_shape=jax.ShapeDtypeStruct((layers, m, 2, N_MEM_HEADS, HEAD_DIM), F32),
        scratch_shapes=[pltpu.VMEM((m, k), BF16)],
        compiler_params=_params("parallel", "arbitrary"),
        name="mem_kv",
    )(mem_rows, g.reshape(layers, 1, k), w)
    return out.reshape(layers, batch, m // batch, 2, N_MEM_HEADS, HEAD_DIM)


def _in_proj_b_kernel(x_ref, g_ref, w_ref, o_ref, xn_ref):
    tm = x_ref.shape[0]
    j = pl.program_id(1)

    @pl.when(j == 0)
    def _():
        xn = _rms(x_ref[...], g_ref[...]).astype(BF16)
        xn_ref[0] = xn
        for g in range(1, N_SWA_GROUPS):
            perm = _residue_major_perm(PERM_BLOCK, SWA_PATTERN[g][1])
            for s in range(0, tm, PERM_BLOCK):
                xn_ref[g, s:s + PERM_BLOCK, :] = jnp.dot(
                    perm, xn[s:s + PERM_BLOCK, :], preferred_element_type=F32).astype(BF16)

    for t in range(o_ref.shape[0]):
        tile = j * o_ref.shape[0] + t
        src = jnp.where(tile < 3 * N_SWA_GROUPS, tile % N_SWA_GROUPS, 0)
        w = w_ref[:, t * GROUP_WIDTH:(t + 1) * GROUP_WIDTH]
        o_ref[t] = jnp.dot(xn_ref[src], w, preferred_element_type=F32).astype(o_ref.dtype)


def _in_proj_b(x, g, w, *, tm, tiles_per_step):
    m, k = x.shape
    n_tiles = w.shape[1] // GROUP_WIDTH
    return pl.pallas_call(
        _in_proj_b_kernel,
        grid=(m // tm, n_tiles // tiles_per_step),
        in_specs=[
            pl.BlockSpec((tm, k), lambda i, j: (i, 0)),
            pl.BlockSpec((1, k), lambda i, j: (0, 0)),
            pl.BlockSpec((k, tiles_per_step * GROUP_WIDTH), lambda i, j: (0, j)),
        ],
        out_specs=pl.BlockSpec((tiles_per_step, tm, GROUP_WIDTH), lambda i, j: (j, i, 0)),
        out_shape=jax.ShapeDtypeStruct((n_tiles, m, GROUP_WIDTH), BF16),
        scratch_shapes=[pltpu.VMEM((N_SWA_GROUPS, tm, k), BF16)],
        compiler_params=_params("parallel", "arbitrary"),
        name="in_proj_b",
    )(x, g.reshape(1, k), w)


def _mem_attention_tile(q_ref, kv_ref, o_ref, col0):
    ones = jnp.ones((N_MEM, HEAD_DIM), BF16)
    for h in range(N_MEM_HEADS):
        lo, hi = h * HEAD_DIM, (h + 1) * HEAD_DIM
        q = q_ref[:, lo:hi]
        k = kv_ref[:, 0, h, :].astype(BF16)
        v = kv_ref[:, 1, h, :].astype(BF16)
        s = lax.dot_general(q, k, (((1,), (1,)), ((), ())), preferred_element_type=F32) * ATTN_SCALE
        m = jnp.max(s, axis=1, keepdims=True)
        p = jnp.exp(s - m).astype(BF16)
        ov = jnp.dot(p, jnp.concatenate([v, ones], axis=1), preferred_element_type=F32)
        o_ref[:, col0 + lo:col0 + hi] = (ov[:, :HEAD_DIM] / ov[:, HEAD_DIM:]).astype(o_ref.dtype)


def _gmlp_mix_kernel(u_ref, v_ref, q_ref, gv_ref, ws_ref, bs_ref, kv_ref, o_ref, vn_ref):
    tm = u_ref.shape[0]
    v = v_ref[...].astype(F32)
    vn_ref[...] = _rms(v, gv_ref[...]).astype(BF16)
    row = lax.broadcasted_iota(jnp.int32, (CHUNK, CHUNK), 0)
    col = lax.broadcasted_iota(jnp.int32, (CHUNK, CHUNK), 1)
    causal = row >= col
    for g in range(N_GROUPS_A):
        w = jnp.where(causal, ws_ref[g], 0.0).astype(BF16)
        b = bs_ref[:, g:g + 1]
        c0, c1 = g * GROUP_DIM_A, (g + 1) * GROUP_DIM_A
        for c in range(tm // CHUNK):
            r0, r1 = c * CHUNK, (c + 1) * CHUNK
            s = jnp.dot(w, vn_ref[r0:r1, c0:c1], preferred_element_type=F32) + b
            o_ref[r0:r1, c0:c1] = (u_ref[r0:r1, c0:c1].astype(F32) * s).astype(o_ref.dtype)
    _mem_attention_tile(q_ref, kv_ref, o_ref, MIXER_WIDTH)


def _mem_kv_spec(layer, batch_of):
    return pl.BlockSpec((None, None, N_MEM, 2, N_MEM_HEADS, HEAD_DIM),
                        lambda i: (layer, batch_of(i), 0, 0, 0, 0))


def _gmlp_mix(zact, g_v, w_s, b_s, mem_kv, layer, *, tm, rows_per_batch):
    m = zact.shape[0]
    tiles_per_batch = rows_per_batch // tm
    return pl.pallas_call(
        _gmlp_mix_kernel,
        grid=(m // tm,),
        in_specs=[
            pl.BlockSpec((tm, MIXER_WIDTH), lambda i: (i, 0)),
            pl.BlockSpec((tm, MIXER_WIDTH), lambda i: (i, 1)),
            pl.BlockSpec((tm, MEM_WIDTH), lambda i: (i, 2 * MIXER_WIDTH // MEM_WIDTH)),
            pl.BlockSpec((1, MIXER_WIDTH), lambda i: (0, 0)),
            pl.BlockSpec((N_GROUPS_A, CHUNK, CHUNK), lambda i: (0, 0, 0)),
            pl.BlockSpec((CHUNK, N_GROUPS_A), lambda i: (0, 0)),
            _mem_kv_spec(layer, lambda i: i // tiles_per_batch),
        ],
        out_specs=pl.BlockSpec((tm, D_MODEL), lambda i: (i, 0)),
        out_shape=jax.ShapeDtypeStruct((m, D_MODEL), BF16),
        scratch_shapes=[pltpu.VMEM((tm, MIXER_WIDTH), BF16)],
        compiler_params=_params("parallel"),
        name="gmlp_mix",
    )(zact, zact, zact, g_v.reshape(1, MIXER_WIDTH), w_s, b_s.T, mem_kv)


def _out_proj_kernel(mix_ref, w_ref, x_ref, g_ref, o_ref):
    o = jnp.dot(mix_ref[...], w_ref[...], preferred_element_type=F32)
    o_ref[...] = x_ref[...] + _rms(o, g_ref[...])


def _out_proj(mix, w, x, g, *, tm):
    m = x.shape[0]
    return pl.pallas_call(
        _out_proj_kernel,
        grid=(m // tm,),
        in_specs=[
            pl.BlockSpec((tm, D_MODEL), lambda i: (i, 0)),
            pl.BlockSpec((D_MODEL, D_MODEL), lambda i: (0, 0)),
            pl.BlockSpec((tm, D_MODEL), lambda i: (i, 0)),
            pl.BlockSpec((1, D_MODEL), lambda i: (0, 0)),
        ],
        out_specs=pl.BlockSpec((tm, D_MODEL), lambda i: (i, 0)),
        out_shape=jax.ShapeDtypeStruct((m, D_MODEL), F32),
        compiler_params=_params("parallel"),
        name="out_proj",
    )(mix, w, x, g.reshape(1, D_MODEL))


def _out_proj_cast_kernel(mix_ref, w_ref, x_ref, g_ref, o_ref, wb_ref, acc_ref):
    j = pl.program_id(0)
    w = w_ref[...].astype(BF16)
    wb_ref[...] = w
    acc_ref[j] = jnp.dot(mix_ref[...], w, preferred_element_type=F32)

    @pl.when(j == pl.num_programs(0) - 1)
    def _():
        o = jnp.concatenate([acc_ref[t] for t in range(acc_ref.shape[0])], axis=1)
        o_ref[...] = x_ref[...] + _rms(o, g_ref[...])


def _out_proj_cast(mix, w, layer, x, g, *, tn):
    m = x.shape[0]
    n_tiles = D_MODEL // tn
    return pl.pallas_call(
        _out_proj_cast_kernel,
        grid=(n_tiles,),
        in_specs=[
            pl.BlockSpec((m, D_MODEL), lambda j: (0, 0)),
            pl.BlockSpec((None, D_MODEL, tn), lambda j: (layer, 0, j)),
            pl.BlockSpec((m, D_MODEL), lambda j: (0, 0)),
            pl.BlockSpec((1, D_MODEL), lambda j: (0, 0)),
        ],
        out_specs=[
            pl.BlockSpec((m, D_MODEL), lambda j: (0, 0)),
            pl.BlockSpec((D_MODEL, tn), lambda j: (0, j)),
        ],
        out_shape=[
            jax.ShapeDtypeStruct((m, D_MODEL), F32),
            jax.ShapeDtypeStruct((D_MODEL, D_MODEL), BF16),
        ],
        scratch_shapes=[pltpu.VMEM((n_tiles, m, tn), F32)],
        compiler_params=_params("arbitrary"),
        name="out_proj_cast",
    )(mix, w, x, g.reshape(1, D_MODEL))


def _ffn_kernel(x_ref, gpre_ref, gpost_ref, wg_ref, wl_ref, wd_ref, o_ref, *rest, emit_w):
    xn_ref = rest[-1]
    f = pl.program_id(1)

    @pl.when(f == 0)
    def _():
        xn_ref[...] = _rms(x_ref[...], gpre_ref[...]).astype(BF16)
        o_ref[...] = jnp.zeros_like(o_ref)

    wg, wl, wd = (r[...].astype(BF16) for r in (wg_ref, wl_ref, wd_ref))
    if emit_w:
        for r, w in zip(rest[:3], (wg, wl, wd)):
            r[...] = w
    xn = xn_ref[...]
    hg = jnp.dot(xn, wg, preferred_element_type=F32)
    hl = jnp.dot(xn, wl, preferred_element_type=F32)
    a = (hg * jax.nn.sigmoid(hg) * hl).astype(BF16)
    o_ref[...] += jnp.dot(a, wd, preferred_element_type=F32)

    @pl.when(f == pl.num_programs(1) - 1)
    def _():
        o_ref[...] = x_ref[...] + _rms(o_ref[...], gpost_ref[...])


def _ffn(x, g_pre, g_post, weights, *, tm, tf, single_buffer):
    m = x.shape[0]
    nf = D_FF // tf
    mode = dict(pipeline_mode=pl.Buffered(1)) if single_buffer else {}
    return pl.pallas_call(
        functools.partial(_ffn_kernel, emit_w=False),
        grid=(m // tm, nf),
        in_specs=[
            pl.BlockSpec((tm, D_MODEL), lambda i, f: (i, 0)),
            pl.BlockSpec((1, D_MODEL), lambda i, f: (0, 0)),
            pl.BlockSpec((1, D_MODEL), lambda i, f: (0, 0)),
            pl.BlockSpec((D_MODEL, tf), lambda i, f: (0, f)),
            pl.BlockSpec((D_MODEL, tf), lambda i, f: (0, f)),
            pl.BlockSpec((tf, D_MODEL), lambda i, f: (f, 0)),
        ],
        out_specs=pl.BlockSpec((tm, D_MODEL), lambda i, f: (i, 0), **mode),
        out_shape=jax.ShapeDtypeStruct((m, D_MODEL), F32),
        scratch_shapes=[pltpu.VMEM((tm, D_MODEL), BF16)],
        compiler_params=_params("parallel", "arbitrary"),
        name="ffn",
    )(x, g_pre.reshape(1, D_MODEL), g_post.reshape(1, D_MODEL), *weights)


def _ffn_cast(x, g_pre, g_post, w_up, w_down, layer, *, tf):
    m = x.shape[0]
    nf = D_FF // tf
    outs = pl.pallas_call(
        functools.partial(_ffn_kernel, emit_w=True),
        grid=(1, nf),
        in_specs=[
            pl.BlockSpec((m, D_MODEL), lambda i, f: (0, 0)),
            pl.BlockSpec((1, D_MODEL), lambda i, f: (0, 0)),
            pl.BlockSpec((1, D_MODEL), lambda i, f: (0, 0)),
            pl.BlockSpec((None, D_MODEL, tf), lambda i, f: (layer, 0, f)),
            pl.BlockSpec((None, D_MODEL, tf), lambda i, f: (layer, 0, nf + f)),
            pl.BlockSpec((None, tf, D_MODEL), lambda i, f: (layer, f, 0)),
        ],
        out_specs=[
            pl.BlockSpec((m, D_MODEL), lambda i, f: (0, 0)),
            pl.BlockSpec((D_MODEL, tf), lambda i, f: (0, f)),
            pl.BlockSpec((D_MODEL, tf), lambda i, f: (0, f)),
            pl.BlockSpec((tf, D_MODEL), lambda i, f: (f, 0)),
        ],
        out_shape=[
            jax.ShapeDtypeStruct((m, D_MODEL), F32),
            jax.ShapeDtypeStruct((D_MODEL, D_FF), BF16),
            jax.ShapeDtypeStruct((D_MODEL, D_FF), BF16),
            jax.ShapeDtypeStruct((D_FF, D_MODEL), BF16),
        ],
        scratch_shapes=[pltpu.VMEM((m, D_MODEL), BF16)],
        compiler_params=_params("parallel", "arbitrary"),
        name="ffn_cast",
    )(x, g_pre.reshape(1, D_MODEL), g_post.reshape(1, D_MODEL), w_up, w_up, w_down)
    return outs[0], tuple(outs[1:])


def _swa_kernel(q_ref, k_ref, v_ref, tb_ref, o_ref, lse_ref):
    n_units, n_res, u, _ = q_ref.shape
    per_blk = N_BACK // u
    n_blk = n_units // per_blk
    lane = lax.broadcasted_iota(jnp.int32, (N_BACK, HEAD_DIM), 1)
    ones = jnp.ones((2 * N_BACK, HEAD_DIM), BF16)

    def rows(ref, res, unit0, n_rows, lo, hi):
        return ref[pl.ds(unit0, n_rows // u), res, :, lo:hi].reshape(n_rows, hi - lo)

    def block(res, qu, ku, table):
        n_keys = N_BACK if table == 0 else 2 * N_BACK
        lse_tile = jnp.zeros((N_BACK, HEAD_DIM), F32)
        for h in range(HEADS_PER_GROUP):
            lo, hi = h * HEAD_DIM, (h + 1) * HEAD_DIM
            q = rows(q_ref, res, qu, N_BACK, lo, hi)
            kw = rows(k_ref, res, ku, n_keys, lo, hi)
            vw = rows(v_ref, res, ku, n_keys, lo, hi)
            s = lax.dot_general(q, kw, (((1,), (1,)), ((), ())), preferred_element_type=F32)
            s = s * ATTN_SCALE + tb_ref[table, h][:, :n_keys]
            m = jnp.max(s, axis=1, keepdims=True)
            p = jnp.exp(s - m).astype(BF16)
            ov = jnp.dot(p, jnp.concatenate([vw, ones[:n_keys]], axis=1), preferred_element_type=F32)
            den = ov[:, HEAD_DIM:]
            o = ov[:, :HEAD_DIM] / den
            o_ref[pl.ds(qu, per_blk), res, :, lo:hi] = o.reshape(per_blk, u, HEAD_DIM).astype(o_ref.dtype)
            lse_tile = jnp.where(lane == h, m + jnp.log(den), lse_tile)
        lse_ref[pl.ds(qu, per_blk), res, :, :] = lse_tile.reshape(per_blk, u, HEAD_DIM)

    def block_at(res, n):
        block(res, n * per_blk, (n - 1) * per_blk, 1)

    n_pairs = (n_blk - 1) // 2
    for res in range(n_res):
        block(res, 0, 0, 0)

        def body(i, carry, res=res):
            block_at(res, 1 + 2 * i)
            block_at(res, 2 + 2 * i)
            return carry

        if n_pairs > 0:
            lax.fori_loop(0, n_pairs, body, 0)
        if (n_blk - 1) % 2 == 1:
            block_at(res, n_blk - 1)


def _sub_block(dil):
    return N_BACK if dil == 1 else PERM_BLOCK


def _swa_group(zb, tables, g, *, batch, seq):
    dil = SWA_PATTERN[g][1]
    sub = _sub_block(dil)
    n_units, u = seq // sub, sub // dil
    n_res = 4 if n_units * u == 2 * N_BACK else 1
    view = zb.reshape(zb.shape[0], batch, n_units, dil, u, GROUP_WIDTH)

    def rows_in(tile):
        return pl.BlockSpec((None, None, n_units, n_res, u, GROUP_WIDTH), lambda i, r: (tile, i, 0, r, 0, 0))

    def rows_out(width):
        return pl.BlockSpec((None, n_units, n_res, u, width), lambda i, r: (i, 0, r, 0, 0))

    return pl.pallas_call(
        _swa_kernel,
        grid=(batch, dil // n_res),
        in_specs=[rows_in(g), rows_in(N_SWA_GROUPS + g), rows_in(2 * N_SWA_GROUPS + g),
                  pl.BlockSpec((None, 2, HEADS_PER_GROUP, N_BACK, 2 * N_BACK), lambda i, r: (g, 0, 0, 0, 0))],
        out_specs=[rows_out(GROUP_WIDTH), rows_out(HEAD_DIM)],
        out_shape=[
            jax.ShapeDtypeStruct((batch, n_units, dil, u, GROUP_WIDTH), BF16),
            jax.ShapeDtypeStruct((batch, n_units, dil, u, HEAD_DIM), F32),
        ],
        compiler_params=_params("parallel", "parallel"),
        name=f"swa_group{g}",
    )(view, view, view, tables)


def _split3(x):
    hi = x.astype(BF16)
    rest = x - hi.astype(F32)
    mid = rest.astype(BF16)
    lo = (rest - mid.astype(F32)).astype(BF16)
    return hi, mid, lo


def _swa_merge_kernel(o0_ref, o1_ref, o2_ref, l0_ref, l1_ref, l2_ref, q_ref, kv_ref, o_ref):
    tm = o_ref.shape[0]
    outs, lses = [], []
    for g, (o_g, l_g) in enumerate(((o0_ref, l0_ref), (o1_ref, l1_ref), (o2_ref, l2_ref))):
        dil = SWA_PATTERN[g][1]
        o = o_g[...].reshape(tm, GROUP_WIDTH)
        l = l_g[...].reshape(tm, HEAD_DIM)
        if dil > 1:
            inv = _residue_major_perm(PERM_BLOCK, dil, transpose=True)
            l3 = _split3(l)
            o_nat, l_nat = [], []
            for s in range(0, tm, PERM_BLOCK):
                o_nat.append(jnp.dot(inv, o[s:s + PERM_BLOCK, :], preferred_element_type=F32))
                l_nat.append(sum(jnp.dot(inv, t[s:s + PERM_BLOCK, :], preferred_element_type=F32) for t in l3))
            o = jnp.concatenate(o_nat, axis=0)
            l = jnp.concatenate(l_nat, axis=0)
        outs.append(o.astype(F32))
        lses.append(l)
    for h in range(HEADS_PER_GROUP):
        lo, hi = h * HEAD_DIM, (h + 1) * HEAD_DIM
        ls = [l[:, h:h + 1] for l in lses]
        mx = jnp.maximum(jnp.maximum(ls[0], ls[1]), ls[2])
        es = [jnp.exp(l - mx) for l in ls]
        tot = es[0] + es[1] + es[2]
        for g in range(N_SWA_GROUPS):
            alpha = es[g] / tot
            o_ref[:, g * GROUP_WIDTH + lo:g * GROUP_WIDTH + hi] = (outs[g][:, lo:hi] * alpha).astype(o_ref.dtype)
    _mem_attention_tile(q_ref, kv_ref, o_ref, MIXER_WIDTH)


def _swa_merge(outs, lses, zb, mem_kv, layer, *, rows_per_batch):
    m = zb.shape[1]
    tm = MERGE_TILE
    tiles_per_batch = rows_per_batch // tm

    def tile(width, dil):
        sub = _sub_block(dil)
        return pl.BlockSpec((None, tm // sub, dil, sub // dil, width),
                            lambda i: (i // tiles_per_batch, i % tiles_per_batch, 0, 0, 0))

    return pl.pallas_call(
        _swa_merge_kernel,
        grid=(m // tm,),
        in_specs=(
            [tile(GROUP_WIDTH, dil) for _, dil in SWA_PATTERN]
            + [tile(HEAD_DIM, dil) for _, dil in SWA_PATTERN]
            + [pl.BlockSpec((None, tm, MEM_WIDTH), lambda i: (3 * N_SWA_GROUPS, i, 0)),
               _mem_kv_spec(layer, lambda i: i // tiles_per_batch)]
        ),
        out_specs=pl.BlockSpec((tm, D_MODEL), lambda i: (i, 0)),
        out_shape=jax.ShapeDtypeStruct((m, D_MODEL), BF16),
        compiler_params=_params("parallel"),
        name="swa_merge",
    )(*outs, *lses, zb, mem_kv)


def _sample_mem_attention(q_row, kv_ref, o_ref, col0):
    for h in range(N_MEM_HEADS):
        lo, hi = h * HEAD_DIM, (h + 1) * HEAD_DIM
        q = q_row[:, lo:hi]
        k = kv_ref[:, 0, h, :]
        v = kv_ref[:, 1, h, :]
        s = jnp.sum(k * q, axis=1, keepdims=True) * ATTN_SCALE
        m = jnp.max(s, axis=0, keepdims=True)
        p = jnp.exp(s - m)
        den = jnp.sum(p, axis=0, keepdims=True)
        o_ref[:, col0 + lo:col0 + hi] = jnp.sum(p * v, axis=0, keepdims=True) / den


def _sample_mix_a_kernel(z_ref, gv_ref, w0_ref, b0_ref, kv_ref, o_ref, vrow_ref):
    u = z_ref[:, 0:MIXER_WIDTH]
    v = _rms(z_ref[:, MIXER_WIDTH:2 * MIXER_WIDTH], gv_ref[...])
    vrow_ref[...] = v
    o_ref[:, 0:MIXER_WIDTH] = u * (w0_ref[...] * v + b0_ref[...])
    _sample_mem_attention(z_ref[:, 2 * MIXER_WIDTH:2 * MIXER_WIDTH + MEM_WIDTH], kv_ref, o_ref, MIXER_WIDTH)


def _sample_mix_a(z, g_v, w_s, b_s, mem_kv, layer):
    bd = mem_kv.shape[1]
    w0 = jnp.repeat(w_s[:, 0, 0], GROUP_DIM_A).reshape(1, MIXER_WIDTH)
    b0 = jnp.repeat(b_s[:, 0], GROUP_DIM_A).reshape(1, MIXER_WIDTH)
    width = z.shape[1]
    vec = lambda i: (0, 0)
    return pl.pallas_call(
        _sample_mix_a_kernel,
        grid=(bd,),
        in_specs=[
            pl.BlockSpec((None, 1, width), lambda i: (i, 0, 0)),
            pl.BlockSpec((1, MIXER_WIDTH), vec),
            pl.BlockSpec((1, MIXER_WIDTH), vec),
            pl.BlockSpec((1, MIXER_WIDTH), vec),
            _mem_kv_spec(layer, lambda i: i),
        ],
        out_specs=[
            pl.BlockSpec((None, 1, D_MODEL), lambda i: (i, 0, 0)),
            pl.BlockSpec((None, 1, MIXER_WIDTH), lambda i: (i, 0, 0)),
        ],
        out_shape=[
            jax.ShapeDtypeStruct((bd, 1, D_MODEL), F32),
            jax.ShapeDtypeStruct((bd, 1, MIXER_WIDTH), F32),
        ],
        compiler_params=_params("parallel"),
        name="sample_mix_a",
    )(z[:bd].reshape(bd, 1, width), g_v.reshape(1, MIXER_WIDTH), w0, b0, mem_kv)


def _sample_mix_b_kernel(z_ref, c0_ref, c1_ref, c2_ref, bcol_ref, bnew_ref, kv_ref, o_ref):
    caches = (c0_ref, c1_ref, c2_ref)
    outs = [[None] * HEADS_PER_GROUP for _ in range(N_SWA_GROUPS)]
    lses = [[None] * HEADS_PER_GROUP for _ in range(N_SWA_GROUPS)]
    for g in range(N_SWA_GROUPS):
        for h in range(HEADS_PER_GROUP):
            hd = g * HEADS_PER_GROUP + h
            lo, hi = h * HEAD_DIM, (h + 1) * HEAD_DIM
            q = z_ref[:, hd * HEAD_DIM:(hd + 1) * HEAD_DIM]
            k_new = z_ref[:, MIXER_WIDTH + hd * HEAD_DIM:MIXER_WIDTH + (hd + 1) * HEAD_DIM]
            v_new = z_ref[:, 2 * MIXER_WIDTH + hd * HEAD_DIM:2 * MIXER_WIDTH + (hd + 1) * HEAD_DIM]
            kc = caches[g][:, 0, h, :]
            vc = caches[g][:, 1, h, :]
            s_c = jnp.sum(kc * q, axis=1, keepdims=True) * ATTN_SCALE + bcol_ref[g][:, h:h + 1]
            s_n = jnp.sum(k_new * q, axis=1, keepdims=True) * ATTN_SCALE + bnew_ref[:, hd:hd + 1]
            m = jnp.maximum(jnp.max(s_c, axis=0, keepdims=True), s_n)
            p_c = jnp.exp(s_c - m)
            p_n = jnp.exp(s_n - m)
            den = jnp.sum(p_c, axis=0, keepdims=True) + p_n
            outs[g][h] = (jnp.sum(p_c * vc, axis=0, keepdims=True) + p_n * v_new) / den
            lses[g][h] = m + jnp.log(den)
    for h in range(HEADS_PER_GROUP):
        ls = [lses[g][h] for g in range(N_SWA_GROUPS)]
        mx = jnp.maximum(jnp.maximum(ls[0], ls[1]), ls[2])
        es = [jnp.exp(l - mx) for l in ls]
        tot = es[0] + es[1] + es[2]
        for g in range(N_SWA_GROUPS):
            c0 = g * GROUP_WIDTH + h * HEAD_DIM
            o_ref[:, c0:c0 + HEAD_DIM] = outs[g][h] * (es[g] / tot)
    _sample_mem_attention(z_ref[:, 3 * MIXER_WIDTH:3 * MIXER_WIDTH + MEM_WIDTH], kv_ref, o_ref, MIXER_WIDTH)


def _sample_mix_b(z, win_caches, swa_layer, bias_groups, mem_kv, layer):
    bd = mem_kv.shape[1]
    width = z.shape[1]
    cache_views, cache_specs = [], []
    for g, (win, dil) in enumerate(SWA_PATTERN):
        c = win_caches[g]
        cache_views.append(c.reshape(c.shape[0], bd, win // dil, dil, 2, HEADS_PER_GROUP, HEAD_DIM))
        cache_specs.append(pl.BlockSpec((None, None, N_BACK, None, 2, HEADS_PER_GROUP, HEAD_DIM),
                                        lambda i: (swa_layer, i, 0, 0, 0, 0, 0)))
    bcol = jnp.stack([bg[:, N_BACK:0:-1].T for bg in bias_groups], axis=0)
    bnew = jnp.concatenate([bg[:, 0] for bg in bias_groups])
    bnew = jnp.pad(bnew, (0, HEAD_DIM - bnew.shape[0])).reshape(1, HEAD_DIM)
    return pl.pallas_call(
        _sample_mix_b_kernel,
        grid=(bd,),
        in_specs=[pl.BlockSpec((None, 1, width), lambda i: (i, 0, 0))] + cache_specs + [
            pl.BlockSpec((N_SWA_GROUPS, N_BACK, HEADS_PER_GROUP), lambda i: (0, 0, 0)),
            pl.BlockSpec((1, HEAD_DIM), lambda i: (0, 0)),
            _mem_kv_spec(layer, lambda i: i),
        ],
        out_specs=pl.BlockSpec((None, 1, D_MODEL), lambda i: (i, 0, 0)),
        out_shape=jax.ShapeDtypeStruct((bd, 1, D_MODEL), F32),
        compiler_params=_params("parallel"),
        name="sample_mix_b",
    )(z[:bd].reshape(bd, 1, width), *cache_views, bcol, bnew, mem_kv)


def _t5_bucket(dist):
    nf = jnp.maximum(dist, MAX_EXACT).astype(F32)
    large = MAX_EXACT + (jnp.log(nf / MAX_EXACT) / math.log(MAX_DISTANCE / MAX_EXACT)
                         * (N_BUCKETS - MAX_EXACT)).astype(jnp.int32)
    large = jnp.minimum(large, N_BUCKETS - 1)
    return jnp.where(dist < MAX_EXACT, dist, large)


def _group_bias(rel_bias, g, dil):
    dist = jnp.arange(N_BACK + 1, dtype=jnp.int32) * dil
    b = rel_bias[_t5_bucket(dist)][:, g * HEADS_PER_GROUP:(g + 1) * HEADS_PER_GROUP]
    return b.T.astype(F32)


def _band_tables(bias_groups):
    width = 2 * N_BACK
    rows = []
    for bias_j in bias_groups:
        masked = jnp.full((HEADS_PER_GROUP, N_BACK - 1), NEG_INF, F32)
        rows.append(jnp.concatenate([bias_j[:, :1], masked, bias_j[:, N_BACK:0:-1]], axis=1))
        rows.append(jnp.concatenate([bias_j[:, ::-1], masked], axis=1))
    base = jnp.stack(rows, axis=0).reshape(-1, width)
    pitch = 2 * width - 1
    flat = jnp.tile(base, (1, width))[:, :N_BACK * pitch]
    tabs = flat.reshape(-1, N_BACK, pitch)[:, :, :width]
    return tabs.reshape(N_SWA_GROUPS, 2, HEADS_PER_GROUP, N_BACK, width)


def _kv_tail_kernel(k_ref, v_ref, o_ref, *, dil):
    rows = k_ref.shape[0]
    for kv, ref in enumerate((k_ref, v_ref)):
        x = ref[...]
        if dil > 1:
            inv = _residue_major_perm(PERM_BLOCK, dil, transpose=True)
            x = jnp.concatenate([jnp.dot(inv, x[s:s + PERM_BLOCK, :], preferred_element_type=F32)
                                 for s in range(0, rows, PERM_BLOCK)], axis=0)
        x = x.astype(F32)
        for h in range(HEADS_PER_GROUP):
            o_ref[:, kv, h, :] = x[:, h * HEAD_DIM:(h + 1) * HEAD_DIM]


def _kv_tail(zb, g, *, batch, seq):
    win, dil = SWA_PATTERN[g]
    rows = min(win, PERM_BLOCK)
    first = (seq - win) // rows
    per_batch = seq // rows

    def tile(t):
        return pl.BlockSpec((None, rows, GROUP_WIDTH), lambda b, s: (t, b * per_batch + first + s, 0))

    return pl.pallas_call(
        functools.partial(_kv_tail_kernel, dil=dil),
        grid=(batch, win // rows),
        in_specs=[tile(N_SWA_GROUPS + g), tile(2 * N_SWA_GROUPS + g)],
        out_specs=pl.BlockSpec((None, rows, 2, HEADS_PER_GROUP, HEAD_DIM), lambda b, s: (b, s, 0, 0, 0)),
        out_shape=jax.ShapeDtypeStruct((batch, win, 2, HEADS_PER_GROUP, HEAD_DIM), F32),
        compiler_params=_params("parallel", "parallel"),
        name=f"kv_tail{g}",
    )(zb, zb)


def kernel(x_prompt, x_sample, mem_prompt, cache_mem_kv, cache_win128_kv, cache_win512_kv, cache_win2048_kv, rel_bias, norm_mix_pre, norm_mix_post, norm_ffn_pre, norm_ffn_post, norm_mem, w_mem_kv, w_in_a, norm_v_a, w_spatial_a, b_spatial_a, w_in_b, w_out, w_ffn_up, w_ffn_down):
    batch, seq, _ = x_prompt.shape
    bd = x_sample.shape[0]
    depth = w_out.shape[0]
    m_p = batch * seq
    win_caches = (cache_win128_kv, cache_win512_kv, cache_win2048_kv)

    bias_groups = [_group_bias(rel_bias, g, dil) for g, (_, dil) in enumerate(SWA_PATTERN)]
    band_tables = _band_tables(bias_groups)

    yp = x_prompt.reshape(m_p, D_MODEL)
    ys = jnp.pad(x_sample.reshape(bd, D_MODEL), ((0, SAMPLE_PAD - bd), (0, 0)))
    mem_rows = mem_prompt.reshape(batch * N_MEM, D_MODEL)

    mem_kv_p = _mem_kv(mem_rows, norm_mem, w_mem_kv, batch=batch)
    chunk_v_s = []
    win_p = [[] for _ in SWA_PATTERN]
    win_s = [[] for _ in SWA_PATTERN]
    for i in range(depth):
        li = i // 2
        if i % 2 == 0:
            zs, w_in = _norm_matmul(ys, norm_mix_pre[i], w_in_a, li, tm=SAMPLE_PAD, tn=512, emit_w=True,
                                    gelu_cols=2 * MIXER_WIDTH, out_dtype=F32, name="in_proj_a_s")
            zp = _norm_matmul(yp, norm_mix_pre[i], w_in, None, tm=512, tn=1792,
                              gelu_cols=2 * MIXER_WIDTH, out_dtype=BF16, name="in_proj_a")
            mix_p = _gmlp_mix(zp, norm_v_a[li], w_spatial_a[li], b_spatial_a[li], mem_kv_p, i,
                              tm=512, rows_per_batch=seq)
            mix_s, v_rows = _sample_mix_a(zs, norm_v_a[li], w_spatial_a[li], b_spatial_a[li], cache_mem_kv, i)
            chunk_v_s.append(v_rows)
        else:
            zs, w_in = _norm_matmul(ys, norm_mix_pre[i], w_in_b, li, tm=SAMPLE_PAD, tn=512, emit_w=True,
                                    gelu_cols=0, out_dtype=F32, name="in_proj_b_s")
            zb = _in_proj_b(yp, norm_mix_pre[i], w_in, tm=1024, tiles_per_step=2)
            outs, lses = [], []
            for g, (win, dil) in enumerate(SWA_PATTERN):
                o, lse = _swa_group(zb, band_tables, g, batch=batch, seq=seq)
                outs.append(o)
                lses.append(lse)
                win_p[g].append(_kv_tail(zb, g, batch=batch, seq=seq))
                kv_new = zs[:bd, MIXER_WIDTH:3 * MIXER_WIDTH]
                kv_new = kv_new.reshape(bd, 1, 2, N_SWA_GROUPS, HEADS_PER_GROUP, HEAD_DIM)[:, :, :, g]
                win_s[g].append(kv_new)
            mix_p = _swa_merge(outs, lses, zb, mem_kv_p, i, rows_per_batch=seq)
            mix_s = _sample_mix_b(zs, win_caches, li, bias_groups, cache_mem_kv, i)
        mix_s = jnp.pad(mix_s.reshape(bd, D_MODEL), ((0, SAMPLE_PAD - bd), (0, 0))).astype(BF16)
        ys, w_o = _out_proj_cast(mix_s, w_out, i, ys, norm_mix_post[i], tn=512)
        yp = _out_proj(mix_p, w_o, yp, norm_mix_post[i], tm=512)
        ys, w_ffn = _ffn_cast(ys, norm_ffn_pre[i], norm_ffn_post[i], w_ffn_up, w_ffn_down, i, tf=512)
        yp = _ffn(yp, norm_ffn_pre[i], norm_ffn_post[i], w_ffn, tm=1024, tf=512, single_buffer=True)

    return (
        yp.reshape(batch, seq, D_MODEL),
        ys[:bd].reshape(bd, 1, D_MODEL),
        mem_kv_p,
        jnp.stack(chunk_v_s, axis=0),
        jnp.stack(win_p[0], axis=0),
        jnp.stack(win_p[1], axis=0),
        jnp.stack(win_p[2], axis=0),
        jnp.stack(win_s[0], axis=0),
        jnp.stack(win_s[1], axis=0),
        jnp.stack(win_s[2], axis=0),
    )
```

```python
import functools
import math

import jax
import jax.numpy as jnp
from jax import lax
from jax.experimental import pallas as pl
from jax.experimental.pallas import tpu as pltpu

F32 = jnp.float32
BF16 = jnp.bfloat16

D_MODEL = 2048
HEAD_DIM = 128
N_MEM = 256
N_MEM_HEADS = 4
MEM_WIDTH = N_MEM_HEADS * HEAD_DIM
MIXER_WIDTH = D_MODEL - MEM_WIDTH
CHUNK = 128
N_GROUPS_A = 4
GROUP_DIM_A = MIXER_WIDTH // N_GROUPS_A
SWA_PATTERN = ((128, 1), (512, 4), (2048, 16))
N_SWA_GROUPS = len(SWA_PATTERN)
HEADS_PER_GROUP = 4
GROUP_WIDTH = HEADS_PER_GROUP * HEAD_DIM
N_BACK = 128
N_BUCKETS = 32
MAX_EXACT = N_BUCKETS // 2
MAX_DISTANCE = 2048
D_FF = 5632
EPS = 1e-6
NEG_INF = -1e30
ATTN_SCALE = HEAD_DIM ** -0.5
SAMPLE_PAD = 16
PERM_BLOCK = 256
MERGE_TILE = 512

VMEM_LIMIT = 56 * 1024 * 1024
FFN_VMEM_LIMIT = 62 * 1024 * 1024


def _params(*sem, vmem_limit=VMEM_LIMIT):
    return pltpu.CompilerParams(dimension_semantics=sem, vmem_limit_bytes=vmem_limit)


def _gelu(x):
    return 0.5 * x * (1.0 + jnp.tanh(0.7978845608028654 * (x + 0.044715 * (x * x * x))))


def _rms(x, g):
    return x * lax.rsqrt(jnp.mean(x * x, axis=-1, keepdims=True) + EPS) * g


def _log2(n):
    assert n & (n - 1) == 0
    return n.bit_length() - 1


def _residue_major_perm(tm, dil, transpose=False):
    n = tm // dil
    row = lax.broadcasted_iota(jnp.int32, (tm, tm), 0)
    col = lax.broadcasted_iota(jnp.int32, (tm, tm), 1)
    dst, src = (col, row) if transpose else (row, col)
    want = lax.shift_left(jnp.bitwise_and(dst, n - 1), _log2(dil)) + lax.shift_right_logical(dst, _log2(n))
    return (src == want).astype(BF16)


def _norm_matmul_kernel(x_ref, g_ref, w_ref, o_ref, *rest, gelu_cols, emit_w):
    xn_ref = rest[-1]
    n = pl.program_id(1)

    @pl.when(n == 0)
    def _():
        xn_ref[...] = _rms(x_ref[...], g_ref[...]).astype(BF16)

    w = w_ref[...].astype(BF16)
    if emit_w:
        rest[0][...] = w
    acc = jnp.dot(xn_ref[...], w, preferred_element_type=F32)
    if gelu_cols > 0:
        col = lax.broadcasted_iota(jnp.int32, acc.shape, 1)
        acc = jnp.where(col < gelu_cols - n * acc.shape[1], _gelu(acc), acc)
    o_ref[...] = acc.astype(o_ref.dtype)


def _weight_spec(w, layer, block, index):
    if layer is None:
        return pl.BlockSpec(block, index)
    return pl.BlockSpec((None,) + block, lambda *ids: (layer,) + index(*ids))


def _norm_matmul(x, g, w, layer, *, tm, tn, gelu_cols, out_dtype, name, emit_w=False):
    m, k = x.shape
    n = w.shape[-1]
    out_specs = [pl.BlockSpec((tm, tn), lambda i, j: (i, j))]
    out_shape = [jax.ShapeDtypeStruct((m, n), out_dtype)]
    if emit_w:
        assert m == tm
        out_specs.append(pl.BlockSpec((k, tn), lambda i, j: (0, j)))
        out_shape.append(jax.ShapeDtypeStruct((k, n), BF16))
    outs = pl.pallas_call(
        functools.partial(_norm_matmul_kernel, gelu_cols=gelu_cols, emit_w=emit_w),
        grid=(m // tm, n // tn),
        in_specs=[
            pl.BlockSpec((tm, k), lambda i, j: (i, 0)),
            pl.BlockSpec((1, k), lambda i, j: (0, 0)),
            _weight_spec(w, layer, (k, tn), lambda i, j: (0, j)),
        ],
        out_specs=out_specs,
        out_shape=out_shape,
        scratch_shapes=[pltpu.VMEM((tm, k), BF16)],
        compiler_params=_params("parallel", "arbitrary"),
        name=name,
    )(x, g.reshape(1, k), w)
    return outs if emit_w else outs[0]


def _mem_kv_kernel(x_ref, g_ref, w_ref, o_ref, xn_ref):
    @pl.when(pl.program_id(1) == 0)
    def _():
        xn_ref[...] = _rms(x_ref[...], g_ref[...]).astype(BF16)

    acc = jnp.dot(xn_ref[...], w_ref[...].astype(BF16), preferred_element_type=F32)
    for h in range(N_MEM_HEADS):
        o_ref[:, h, :] = acc[:, h * HEAD_DIM:(h + 1) * HEAD_DIM]


def _mem_kv(mem_rows, g, w, *, batch):
    m, k = mem_rows.shape
    layers = w.shape[0]
    out = pl.pallas_call(
        _mem_kv_kernel,
        grid=(layers, 2),
        in_specs=[
            pl.BlockSpec((m, k), lambda l, j: (0, 0)),
            pl.BlockSpec((None, 1, k), lambda l, j: (l, 0, 0)),
            pl.BlockSpec((None, k, MEM_WIDTH), lambda l, j: (l, 0, j)),
        ],
        out_specs=pl.BlockSpec((None, m, None, N_MEM_HEADS, HEAD_DIM), lambda l, j: (l, 0, j, 0, 0)),
        out_shape=jax.ShapeDtypeStruct((layers, m, 2, N_MEM_HEADS, HEAD_DIM), F32),
        scratch_shapes=[pltpu.VMEM((m, k), BF16)],
        compiler_params=_params("parallel", "arbitrary"),
        name="mem_kv",
    )(mem_rows, g.reshape(layers, 1, k), w)
    return out.reshape(layers, batch, m // batch, 2, N_MEM_HEADS, HEAD_DIM)


def _in_proj_b_kernel(x_ref, g_ref, w_ref, o_ref, xn_ref):
    tm = x_ref.shape[0]
    j = pl.program_id(1)

    @pl.when(j == 0)
    def _():
        xn = _rms(x_ref[...], g_ref[...]).astype(BF16)
        xn_ref[0] = xn
        for g in range(1, N_SWA_GROUPS):
            perm = _residue_major_perm(PERM_BLOCK, SWA_PATTERN[g][1])
            for s in range(0, tm, PERM_BLOCK):
                xn_ref[g, s:s + PERM_BLOCK, :] = jnp.dot(
                    perm, xn[s:s + PERM_BLOCK, :], preferred_element_type=F32).astype(BF16)

    for t in range(o_ref.shape[0]):
        tile = j * o_ref.shape[0] + t
        src = jnp.where(tile < 3 * N_SWA_GROUPS, tile % N_SWA_GROUPS, 0)
        w = w_ref[:, t * GROUP_WIDTH:(t + 1) * GROUP_WIDTH]
        o_ref[t] = jnp.dot(xn_ref[src], w, preferred_element_type=F32).astype(o_ref.dtype)


def _in_proj_b(x, g, w, *, tm, tiles_per_step):
    m, k = x.shape
    n_tiles = w.shape[1] // GROUP_WIDTH
    return pl.pallas_call(
        _in_proj_b_kernel,
        grid=(m // tm, n_tiles // tiles_per_step),
        in_specs=[
            pl.BlockSpec((tm, k), lambda i, j: (i, 0)),
            pl.BlockSpec((1, k), lambda i, j: (0, 0)),
            pl.BlockSpec((k, tiles_per_step * GROUP_WIDTH), lambda i, j: (0, j)),
        ],
        out_specs=pl.BlockSpec((tiles_per_step, tm, GROUP_WIDTH), lambda i, j: (j, i, 0)),
        out_shape=jax.ShapeDtypeStruct((n_tiles, m, GROUP_WIDTH), BF16),
        scratch_shapes=[pltpu.VMEM((N_SWA_GROUPS, tm, k), BF16)],
        compiler_params=_params("parallel", "arbitrary"),
        name="in_proj_b",
    )(x, g.reshape(1, k), w)


MEM_KV_SCRATCH = [pltpu.VMEM((N_MEM_HEADS, N_MEM, HEAD_DIM), BF16),
                  pltpu.VMEM((N_MEM_HEADS, N_MEM, 2 * HEAD_DIM), BF16)]


def _mem_attention_tile(q_ref, kv_ref, kb_ref, vb_ref, o_ref, col0, *, tiles_per_batch):
    @pl.when(pl.program_id(0) % tiles_per_batch == 0)
    def _():
        for h in range(N_MEM_HEADS):
            kb_ref[h] = kv_ref[:, 0, h, :].astype(BF16)
            vb_ref[h, :, :HEAD_DIM] = kv_ref[:, 1, h, :].astype(BF16)
            vb_ref[h, :, HEAD_DIM:] = jnp.ones((N_MEM, HEAD_DIM), BF16)

    for h in range(N_MEM_HEADS):
        lo, hi = h * HEAD_DIM, (h + 1) * HEAD_DIM
        s = lax.dot_general(q_ref[:, lo:hi], kb_ref[h], (((1,), (1,)), ((), ())),
                            preferred_element_type=F32) * ATTN_SCALE
        m = jnp.max(s, axis=1, keepdims=True)
        p = jnp.exp(s - m).astype(BF16)
        ov = jnp.dot(p, vb_ref[h], preferred_element_type=F32)
        o_ref[:, col0 + lo:col0 + hi] = (ov[:, :HEAD_DIM] / ov[:, HEAD_DIM:]).astype(o_ref.dtype)


def _gmlp_mix_kernel(u_ref, v_ref, q_ref, gv_ref, ws_ref, bs_ref, kv_ref, o_ref, vn_ref, kb_ref, vb_ref, *,
                     tiles_per_batch):
    tm = u_ref.shape[0]
    v = v_ref[...].astype(F32)
    vn_ref[...] = _rms(v, gv_ref[...]).astype(BF16)
    row = lax.broadcasted_iota(jnp.int32, (CHUNK, CHUNK), 0)
    col = lax.broadcasted_iota(jnp.int32, (CHUNK, CHUNK), 1)
    causal = row >= col
    for g in range(N_GROUPS_A):
        w = jnp.where(causal, ws_ref[g], 0.0).astype(BF16)
        b = bs_ref[:, g:g + 1]
        c0, c1 = g * GROUP_DIM_A, (g + 1) * GROUP_DIM_A
        for c in range(tm // CHUNK):
            r0, r1 = c * CHUNK, (c + 1) * CHUNK
            s = jnp.dot(w, vn_ref[r0:r1, c0:c1], preferred_element_type=F32) + b
            o_ref[r0:r1, c0:c1] = (u_ref[r0:r1, c0:c1].astype(F32) * s).astype(o_ref.dtype)
    _mem_attention_tile(q_ref, kv_ref, kb_ref, vb_ref, o_ref, MIXER_WIDTH, tiles_per_batch=tiles_per_batch)


def _mem_kv_spec(layer, batch_of):
    return pl.BlockSpec((None, None, N_MEM, 2, N_MEM_HEADS, HEAD_DIM),
                        lambda i: (layer, batch_of(i), 0, 0, 0, 0))


def _gmlp_mix(zact, g_v, w_s, b_s, mem_kv, layer, *, tm, rows_per_batch):
    m = zact.shape[0]
    tiles_per_batch = rows_per_batch // tm
    return pl.pallas_call(
        functools.partial(_gmlp_mix_kernel, tiles_per_batch=tiles_per_batch),
        grid=(m // tm,),
        in_specs=[
            pl.BlockSpec((tm, MIXER_WIDTH), lambda i: (i, 0)),
            pl.BlockSpec((tm, MIXER_WIDTH), lambda i: (i, 1)),
            pl.BlockSpec((tm, MEM_WIDTH), lambda i: (i, 2 * MIXER_WIDTH // MEM_WIDTH)),
            pl.BlockSpec((1, MIXER_WIDTH), lambda i: (0, 0)),
            pl.BlockSpec((N_GROUPS_A, CHUNK, CHUNK), lambda i: (0, 0, 0)),
            pl.BlockSpec((CHUNK, N_GROUPS_A), lambda i: (0, 0)),
            _mem_kv_spec(layer, lambda i: i // tiles_per_batch),
        ],
        out_specs=pl.BlockSpec((tm, D_MODEL), lambda i: (i, 0)),
        out_shape=jax.ShapeDtypeStruct((m, D_MODEL), BF16),
        scratch_shapes=[pltpu.VMEM((tm, MIXER_WIDTH), BF16)] + MEM_KV_SCRATCH,
        compiler_params=_params("arbitrary"),
        name="gmlp_mix",
    )(zact, zact, zact, g_v.reshape(1, MIXER_WIDTH), w_s, b_s.T, mem_kv)


def _out_proj_kernel(mix_ref, w_ref, x_ref, g_ref, o_ref):
    o = jnp.dot(mix_ref[...], w_ref[...], preferred_element_type=F32)
    o_ref[...] = x_ref[...] + _rms(o, g_ref[...])


def _out_proj(mix, w, x, g, *, tm):
    m = x.shape[0]
    return pl.pallas_call(
        _out_proj_kernel,
        grid=(m // tm,),
        in_specs=[
            pl.BlockSpec((tm, D_MODEL), lambda i: (i, 0)),
            pl.BlockSpec((D_MODEL, D_MODEL), lambda i: (0, 0)),
            pl.BlockSpec((tm, D_MODEL), lambda i: (i, 0)),
            pl.BlockSpec((1, D_MODEL), lambda i: (0, 0)),
        ],
        out_specs=pl.BlockSpec((tm, D_MODEL), lambda i: (i, 0)),
        out_shape=jax.ShapeDtypeStruct((m, D_MODEL), F32),
        compiler_params=_params("parallel"),
        name="out_proj",
    )(mix, w, x, g.reshape(1, D_MODEL))


def _out_proj_cast_kernel(mix_ref, w_ref, x_ref, g_ref, o_ref, wb_ref, acc_ref):
    j = pl.program_id(0)
    w = w_ref[...].astype(BF16)
    wb_ref[...] = w
    acc_ref[j] = jnp.dot(mix_ref[...], w, preferred_element_type=F32)

    @pl.when(j == pl.num_programs(0) - 1)
    def _():
        o = jnp.concatenate([acc_ref[t] for t in range(acc_ref.shape[0])], axis=1)
        o_ref[...] = x_ref[...] + _rms(o, g_ref[...])


def _out_proj_cast(mix, w, layer, x, g, *, tn):
    m = x.shape[0]
    n_tiles = D_MODEL // tn
    return pl.pallas_call(
        _out_proj_cast_kernel,
        grid=(n_tiles,),
        in_specs=[
            pl.BlockSpec((m, D_MODEL), lambda j: (0, 0)),
            pl.BlockSpec((None, D_MODEL, tn), lambda j: (layer, 0, j)),
            pl.BlockSpec((m, D_MODEL), lambda j: (0, 0)),
            pl.BlockSpec((1, D_MODEL), lambda j: (0, 0)),
        ],
        out_specs=[
            pl.BlockSpec((m, D_MODEL), lambda j: (0, 0)),
            pl.BlockSpec((D_MODEL, tn), lambda j: (0, j)),
        ],
        out_shape=[
            jax.ShapeDtypeStruct((m, D_MODEL), F32),
            jax.ShapeDtypeStruct((D_MODEL, D_MODEL), BF16),
        ],
        scratch_shapes=[pltpu.VMEM((n_tiles, m, tn), F32)],
        compiler_params=_params("arbitrary"),
        name="out_proj_cast",
    )(mix, w, x, g.reshape(1, D_MODEL))


def _ffn_kernel(x_ref, gpre_ref, gpost_ref, wg_ref, wl_ref, wd_ref, o_ref, *rest, emit_w):
    xn_ref = rest[-1]
    f = pl.program_id(1)

    @pl.when(f == 0)
    def _():
        xn_ref[...] = _rms(x_ref[...], gpre_ref[...]).astype(BF16)
        o_ref[...] = jnp.zeros_like(o_ref)

    wg, wl, wd = (r[...].astype(BF16) for r in (wg_ref, wl_ref, wd_ref))
    if emit_w:
        for r, w in zip(rest[:3], (wg, wl, wd)):
            r[...] = w
    xn = xn_ref[...]
    hg = jnp.dot(xn, wg, preferred_element_type=F32)
    hl = jnp.dot(xn, wl, preferred_element_type=F32)
    a = (hg * jax.nn.sigmoid(hg) * hl).astype(BF16)
    o_ref[...] += jnp.dot(a, wd, preferred_element_type=F32)

    @pl.when(f == pl.num_programs(1) - 1)
    def _():
        o_ref[...] = x_ref[...] + _rms(o_ref[...], gpost_ref[...])


def _ffn(x, g_pre, g_post, weights, *, tm, tf, single_buffer):
    m = x.shape[0]
    nf = D_FF // tf
    mode = dict(pipeline_mode=pl.Buffered(1)) if single_buffer else {}
    return pl.pallas_call(
        functools.partial(_ffn_kernel, emit_w=False),
        grid=(m // tm, nf),
        in_specs=[
            pl.BlockSpec((tm, D_MODEL), lambda i, f: (i, 0)),
            pl.BlockSpec((1, D_MODEL), lambda i, f: (0, 0)),
            pl.BlockSpec((1, D_MODEL), lambda i, f: (0, 0)),
            pl.BlockSpec((D_MODEL, tf), lambda i, f: (0, f)),
            pl.BlockSpec((D_MODEL, tf), lambda i, f: (0, f)),
            pl.BlockSpec((tf, D_MODEL), lambda i, f: (f, 0)),
        ],
        out_specs=pl.BlockSpec((tm, D_MODEL), lambda i, f: (i, 0), **mode),
        out_shape=jax.ShapeDtypeStruct((m, D_MODEL), F32),
        scratch_shapes=[pltpu.VMEM((tm, D_MODEL), BF16)],
        compiler_params=_params("parallel", "arbitrary", vmem_limit=FFN_VMEM_LIMIT),
        name="ffn",
    )(x, g_pre.reshape(1, D_MODEL), g_post.reshape(1, D_MODEL), *weights)


def _ffn_cast(x, g_pre, g_post, w_up, w_down, layer, *, tf):
    m = x.shape[0]
    nf = D_FF // tf
    outs = pl.pallas_call(
        functools.partial(_ffn_kernel, emit_w=True),
        grid=(1, nf),
        in_specs=[
            pl.BlockSpec((m, D_MODEL), lambda i, f: (0, 0)),
            pl.BlockSpec((1, D_MODEL), lambda i, f: (0, 0)),
            pl.BlockSpec((1, D_MODEL), lambda i, f: (0, 0)),
            pl.BlockSpec((None, D_MODEL, tf), lambda i, f: (layer, 0, f)),
            pl.BlockSpec((None, D_MODEL, tf), lambda i, f: (layer, 0, nf + f)),
            pl.BlockSpec((None, tf, D_MODEL), lambda i, f: (layer, f, 0)),
        ],
        out_specs=[
            pl.BlockSpec((m, D_MODEL), lambda i, f: (0, 0)),
            pl.BlockSpec((D_MODEL, tf), lambda i, f: (0, f)),
            pl.BlockSpec((D_MODEL, tf), lambda i, f: (0, f)),
            pl.BlockSpec((tf, D_MODEL), lambda i, f: (f, 0)),
        ],
        out_shape=[
            jax.ShapeDtypeStruct((m, D_MODEL), F32),
            jax.ShapeDtypeStruct((D_MODEL, D_FF), BF16),
            jax.ShapeDtypeStruct((D_MODEL, D_FF), BF16),
            jax.ShapeDtypeStruct((D_FF, D_MODEL), BF16),
        ],
        scratch_shapes=[pltpu.VMEM((m, D_MODEL), BF16)],
        compiler_params=_params("parallel", "arbitrary"),
        name="ffn_cast",
    )(x, g_pre.reshape(1, D_MODEL), g_post.reshape(1, D_MODEL), w_up, w_up, w_down)
    return outs[0], tuple(outs[1:])


def _swa_kernel(q_ref, k_ref, v_ref, tb_ref, o_ref, lse_ref):
    n_units, n_res, u, _ = q_ref.shape
    per_blk = N_BACK // u
    n_blk = n_units // per_blk
    lane = lax.broadcasted_iota(jnp.int32, (N_BACK, HEAD_DIM), 1)
    ones = jnp.ones((2 * N_BACK, HEAD_DIM), BF16)

    def rows(ref, res, unit0, n_rows, lo, hi):
        return ref[pl.ds(unit0, n_rows // u), res, :, lo:hi].reshape(n_rows, hi - lo)

    def block(res, qu, ku, table):
        n_keys = N_BACK if table == 0 else 2 * N_BACK
        lse_tile = jnp.zeros((N_BACK, HEAD_DIM), F32)
        for h in range(HEADS_PER_GROUP):
            lo, hi = h * HEAD_DIM, (h + 1) * HEAD_DIM
            q = rows(q_ref, res, qu, N_BACK, lo, hi)
            kw = rows(k_ref, res, ku, n_keys, lo, hi)
            vw = rows(v_ref, res, ku, n_keys, lo, hi)
            s = lax.dot_general(q, kw, (((1,), (1,)), ((), ())), preferred_element_type=F32)
            s = s * ATTN_SCALE + tb_ref[table, h][:, :n_keys]
            m = jnp.max(s, axis=1, keepdims=True)
            p = jnp.exp(s - m).astype(BF16)
            ov = jnp.dot(p, jnp.concatenate([vw, ones[:n_keys]], axis=1), preferred_element_type=F32)
            den = ov[:, HEAD_DIM:]
            o = ov[:, :HEAD_DIM] / den
            o_ref[pl.ds(qu, per_blk), res, :, lo:hi] = o.reshape(per_blk, u, HEAD_DIM).astype(o_ref.dtype)
            lse_tile = jnp.where(lane == h, m + jnp.log(den), lse_tile)
        lse_ref[pl.ds(qu, per_blk), res, :, :] = lse_tile.reshape(per_blk, u, HEAD_DIM)

    def block_at(res, n):
        block(res, n * per_blk, (n - 1) * per_blk, 1)

    n_pairs = (n_blk - 1) // 2
    for res in range(n_res):
        block(res, 0, 0, 0)

        def body(i, carry, res=res):
            block_at(res, 1 + 2 * i)
            block_at(res, 2 + 2 * i)
            return carry

        if n_pairs > 0:
            lax.fori_loop(0, n_pairs, body, 0)
        if (n_blk - 1) % 2 == 1:
            block_at(res, n_blk - 1)


def _sub_block(dil):
    return N_BACK if dil == 1 else PERM_BLOCK


def _swa_group(zb, tables, g, *, batch, seq):
    dil = SWA_PATTERN[g][1]
    sub = _sub_block(dil)
    n_units, u = seq // sub, sub // dil
    n_res = 4 if n_units * u == 2 * N_BACK else 1
    view = zb.reshape(zb.shape[0], batch, n_units, dil, u, GROUP_WIDTH)

    def rows_in(tile):
        return pl.BlockSpec((None, None, n_units, n_res, u, GROUP_WIDTH), lambda i, r: (tile, i, 0, r, 0, 0))

    def rows_out(width):
        return pl.BlockSpec((None, n_units, n_res, u, width), lambda i, r: (i, 0, r, 0, 0))

    return pl.pallas_call(
        _swa_kernel,
        grid=(batch, dil // n_res),
        in_specs=[rows_in(g), rows_in(N_SWA_GROUPS + g), rows_in(2 * N_SWA_GROUPS + g),
                  pl.BlockSpec((None, 2, HEADS_PER_GROUP, N_BACK, 2 * N_BACK), lambda i, r: (g, 0, 0, 0, 0))],
        out_specs=[rows_out(GROUP_WIDTH), rows_out(HEAD_DIM)],
        out_shape=[
            jax.ShapeDtypeStruct((batch, n_units, dil, u, GROUP_WIDTH), BF16),
            jax.ShapeDtypeStruct((batch, n_units, dil, u, HEAD_DIM), F32),
        ],
        compiler_params=_params("parallel", "parallel"),
        name=f"swa_group{g}",
    )(view, view, view, tables)


def _split3(x):
    hi = x.astype(BF16)
    rest = x - hi.astype(F32)
    mid = rest.astype(BF16)
    lo = (rest - mid.astype(F32)).astype(BF16)
    return hi, mid, lo


def _swa_merge_kernel(o0_ref, o1_ref, o2_ref, l0_ref, l1_ref, l2_ref, q_ref, kv_ref, o_ref, kb_ref, vb_ref, *,
                      tiles_per_batch):
    tm = o_ref.shape[0]
    outs, lses = [], []
    for g, (o_g, l_g) in enumerate(((o0_ref, l0_ref), (o1_ref, l1_ref), (o2_ref, l2_ref))):
        dil = SWA_PATTERN[g][1]
        o = o_g[...].reshape(tm, GROUP_WIDTH)
        l = l_g[...].reshape(tm, HEAD_DIM)
        if dil > 1:
            inv = _residue_major_perm(PERM_BLOCK, dil, transpose=True)
            l3 = _split3(l)
            o_nat, l_nat = [], []
            for s in range(0, tm, PERM_BLOCK):
                o_nat.append(jnp.dot(inv, o[s:s + PERM_BLOCK, :], preferred_element_type=F32))
                l_nat.append(sum(jnp.dot(inv, t[s:s + PERM_BLOCK, :], preferred_element_type=F32) for t in l3))
            o = jnp.concatenate(o_nat, axis=0)
            l = jnp.concatenate(l_nat, axis=0)
        outs.append(o.astype(F32))
        lses.append(l)
    for h in range(HEADS_PER_GROUP):
        lo, hi = h * HEAD_DIM, (h + 1) * HEAD_DIM
        ls = [l[:, h:h + 1] for l in lses]
        mx = jnp.maximum(jnp.maximum(ls[0], ls[1]), ls[2])
        es = [jnp.exp(l - mx) for l in ls]
        tot = es[0] + es[1] + es[2]
        for g in range(N_SWA_GROUPS):
            alpha = es[g] / tot
            o_ref[:, g * GROUP_WIDTH + lo:g * GROUP_WIDTH + hi] = (outs[g][:, lo:hi] * alpha).astype(o_ref.dtype)
    _mem_attention_tile(q_ref, kv_ref, kb_ref, vb_ref, o_ref, MIXER_WIDTH, tiles_per_batch=tiles_per_batch)


def _swa_merge(outs, lses, zb, mem_kv, layer, *, rows_per_batch):
    m = zb.shape[1]
    tm = MERGE_TILE
    tiles_per_batch = rows_per_batch // tm

    def tile(width, dil):
        sub = _sub_block(dil)
        return pl.BlockSpec((None, tm // sub, dil, sub // dil, width),
                            lambda i: (i // tiles_per_batch, i % tiles_per_batch, 0, 0, 0))

    return pl.pallas_call(
        functools.partial(_swa_merge_kernel, tiles_per_batch=tiles_per_batch),
        grid=(m // tm,),
        in_specs=(
            [tile(GROUP_WIDTH, dil) for _, dil in SWA_PATTERN]
            + [tile(HEAD_DIM, dil) for _, dil in SWA_PATTERN]
            + [pl.BlockSpec((None, tm, MEM_WIDTH), lambda i: (3 * N_SWA_GROUPS, i, 0)),
               _mem_kv_spec(layer, lambda i: i // tiles_per_batch)]
        ),
        out_specs=pl.BlockSpec((tm, D_MODEL), lambda i: (i, 0)),
        out_shape=jax.ShapeDtypeStruct((m, D_MODEL), BF16),
        scratch_shapes=MEM_KV_SCRATCH,
        compiler_params=_params("arbitrary"),
        name="swa_merge",
    )(*outs, *lses, zb, mem_kv)


def _sample_mem_attention(q_row, kv_ref, o_ref, col0):
    for h in range(N_MEM_HEADS):
        lo, hi = h * HEAD_DIM, (h + 1) * HEAD_DIM
        q = q_row[:, lo:hi]
        k = kv_ref[:, 0, h, :]
        v = kv_ref[:, 1, h, :]
        s = jnp.sum(k * q, axis=1, keepdims=True) * ATTN_SCALE
        m = jnp.max(s, axis=0, keepdims=True)
        p = jnp.exp(s - m)
        den = jnp.sum(p, axis=0, keepdims=True)
        o_ref[:, col0 + lo:col0 + hi] = jnp.sum(p * v, axis=0, keepdims=True) / den


def _sample_mix_a_kernel(z_ref, gv_ref, w0_ref, b0_ref, kv_ref, o_ref, vrow_ref):
    u = z_ref[:, 0:MIXER_WIDTH]
    v = _rms(z_ref[:, MIXER_WIDTH:2 * MIXER_WIDTH], gv_ref[...])
    vrow_ref[...] = v
    o_ref[:, 0:MIXER_WIDTH] = u * (w0_ref[...] * v + b0_ref[...])
    _sample_mem_attention(z_ref[:, 2 * MIXER_WIDTH:2 * MIXER_WIDTH + MEM_WIDTH], kv_ref, o_ref, MIXER_WIDTH)


def _sample_mix_a(z, g_v, w_s, b_s, mem_kv, layer):
    bd = mem_kv.shape[1]
    w0 = jnp.repeat(w_s[:, 0, 0], GROUP_DIM_A).reshape(1, MIXER_WIDTH)
    b0 = jnp.repeat(b_s[:, 0], GROUP_DIM_A).reshape(1, MIXER_WIDTH)
    width = z.shape[1]
    vec = lambda i: (0, 0)
    return pl.pallas_call(
        _sample_mix_a_kernel,
        grid=(bd,),
        in_specs=[
            pl.BlockSpec((None, 1, width), lambda i: (i, 0, 0)),
            pl.BlockSpec((1, MIXER_WIDTH), vec),
            pl.BlockSpec((1, MIXER_WIDTH), vec),
            pl.BlockSpec((1, MIXER_WIDTH), vec),
            _mem_kv_spec(layer, lambda i: i),
        ],
        out_specs=[
            pl.BlockSpec((None, 1, D_MODEL), lambda i: (i, 0, 0)),
            pl.BlockSpec((None, 1, MIXER_WIDTH), lambda i: (i, 0, 0)),
        ],
        out_shape=[
            jax.ShapeDtypeStruct((bd, 1, D_MODEL), F32),
            jax.ShapeDtypeStruct((bd, 1, MIXER_WIDTH), F32),
        ],
        compiler_params=_params("parallel"),
        name="sample_mix_a",
    )(z[:bd].reshape(bd, 1, width), g_v.reshape(1, MIXER_WIDTH), w0, b0, mem_kv)


def _sample_mix_b_kernel(z_ref, c0_ref, c1_ref, c2_ref, bcol_ref, bnew_ref, kv_ref, o_ref):
    caches = (c0_ref, c1_ref, c2_ref)
    outs = [[None] * HEADS_PER_GROUP for _ in range(N_SWA_GROUPS)]
    lses = [[None] * HEADS_PER_GROUP for _ in range(N_SWA_GROUPS)]
    for g in range(N_SWA_GROUPS):
        for h in range(HEADS_PER_GROUP):
            hd = g * HEADS_PER_GROUP + h
            lo, hi = h * HEAD_DIM, (h + 1) * HEAD_DIM
            q = z_ref[:, hd * HEAD_DIM:(hd + 1) * HEAD_DIM]
            k_new = z_ref[:, MIXER_WIDTH + hd * HEAD_DIM:MIXER_WIDTH + (hd + 1) * HEAD_DIM]
            v_new = z_ref[:, 2 * MIXER_WIDTH + hd * HEAD_DIM:2 * MIXER_WIDTH + (hd + 1) * HEAD_DIM]
            kc = caches[g][:, 0, h, :]
            vc = caches[g][:, 1, h, :]
            s_c = jnp.sum(kc * q, axis=1, keepdims=True) * ATTN_SCALE + bcol_ref[g][:, h:h + 1]
            s_n = jnp.sum(k_new * q, axis=1, keepdims=True) * ATTN_SCALE + bnew_ref[:, hd:hd + 1]
            m = jnp.maximum(jnp.max(s_c, axis=0, keepdims=True), s_n)
            p_c = jnp.exp(s_c - m)
            p_n = jnp.exp(s_n - m)
            den = jnp.sum(p_c, axis=0, keepdims=True) + p_n
            outs[g][h] = (jnp.sum(p_c * vc, axis=0, keepdims=True) + p_n * v_new) / den
            lses[g][h] = m + jnp.log(den)
    for h in range(HEADS_PER_GROUP):
        ls = [lses[g][h] for g in range(N_SWA_GROUPS)]
        mx = jnp.maximum(jnp.maximum(ls[0], ls[1]), ls[2])
        es = [jnp.exp(l - mx) for l in ls]
        tot = es[0] + es[1] + es[2]
        for g in range(N_SWA_GROUPS):
            c0 = g * GROUP_WIDTH + h * HEAD_DIM
            o_ref[:, c0:c0 + HEAD_DIM] = outs[g][h] * (es[g] / tot)
    _sample_mem_attention(z_ref[:, 3 * MIXER_WIDTH:3 * MIXER_WIDTH + MEM_WIDTH], kv_ref, o_ref, MIXER_WIDTH)


def _sample_mix_b(z, win_caches, swa_layer, bias_groups, mem_kv, layer):
    bd = mem_kv.shape[1]
    width = z.shape[1]
    cache_views, cache_specs = [], []
    for g, (win, dil) in enumerate(SWA_PATTERN):
        c = win_caches[g]
        cache_views.append(c.reshape(c.shape[0], bd, win // dil, dil, 2, HEADS_PER_GROUP, HEAD_DIM))
        cache_specs.append(pl.BlockSpec((None, None, N_BACK, None, 2, HEADS_PER_GROUP, HEAD_DIM),
                                        lambda i: (swa_layer, i, 0, 0, 0, 0, 0)))
    bcol = jnp.stack([bg[:, N_BACK:0:-1].T for bg in bias_groups], axis=0)
    bnew = jnp.concatenate([bg[:, 0] for bg in bias_groups])
    bnew = jnp.pad(bnew, (0, HEAD_DIM - bnew.shape[0])).reshape(1, HEAD_DIM)
    return pl.pallas_call(
        _sample_mix_b_kernel,
        grid=(bd,),
        in_specs=[pl.BlockSpec((None, 1, width), lambda i: (i, 0, 0))] + cache_specs + [
            pl.BlockSpec((N_SWA_GROUPS, N_BACK, HEADS_PER_GROUP), lambda i: (0, 0, 0)),
            pl.BlockSpec((1, HEAD_DIM), lambda i: (0, 0)),
            _mem_kv_spec(layer, lambda i: i),
        ],
        out_specs=pl.BlockSpec((None, 1, D_MODEL), lambda i: (i, 0, 0)),
        out_shape=jax.ShapeDtypeStruct((bd, 1, D_MODEL), F32),
        compiler_params=_params("parallel"),
        name="sample_mix_b",
    )(z[:bd].reshape(bd, 1, width), *cache_views, bcol, bnew, mem_kv)


def _t5_bucket(dist):
    nf = jnp.maximum(dist, MAX_EXACT).astype(F32)
    large = MAX_EXACT + (jnp.log(nf / MAX_EXACT) / math.log(MAX_DISTANCE / MAX_EXACT)
                         * (N_BUCKETS - MAX_EXACT)).astype(jnp.int32)
    large = jnp.minimum(large, N_BUCKETS - 1)
    return jnp.where(dist < MAX_EXACT, dist, large)


def _group_bias(rel_bias, g, dil):
    dist = jnp.arange(N_BACK + 1, dtype=jnp.int32) * dil
    b = rel_bias[_t5_bucket(dist)][:, g * HEADS_PER_GROUP:(g + 1) * HEADS_PER_GROUP]
    return b.T.astype(F32)


def _band_tables_kernel(b_ref, o_ref):
    n, rows, width = o_ref.shape
    for x in range(n):
        row = jnp.broadcast_to(b_ref[x:x + 1, :], (rows, width))
        o_ref[x] = pltpu.roll(row, 0, 1, stride=1, stride_axis=0)


def _band_tables(bias_groups):
    width = 2 * N_BACK
    rows = []
    for bias_j in bias_groups:
        masked = jnp.full((HEADS_PER_GROUP, N_BACK - 1), NEG_INF, F32)
        rows.append(jnp.concatenate([bias_j[:, :1], masked, bias_j[:, N_BACK:0:-1]], axis=1))
        rows.append(jnp.concatenate([bias_j[:, ::-1], masked], axis=1))
    base = jnp.stack(rows, axis=0).reshape(-1, width)
    tabs = pl.pallas_call(
        _band_tables_kernel,
        out_shape=jax.ShapeDtypeStruct((base.shape[0], N_BACK, width), F32),
        name="band_tables",
    )(base)
    return tabs.reshape(N_SWA_GROUPS, 2, HEADS_PER_GROUP, N_BACK, width)


def _kv_tail_kernel(k_ref, v_ref, o_ref, *, dil):
    rows = k_ref.shape[0]
    for kv, ref in enumerate((k_ref, v_ref)):
        x = ref[...]
        if dil > 1:
            inv = _residue_major_perm(PERM_BLOCK, dil, transpose=True)
            x = jnp.concatenate([jnp.dot(inv, x[s:s + PERM_BLOCK, :], preferred_element_type=F32)
                                 for s in range(0, rows, PERM_BLOCK)], axis=0)
        x = x.astype(F32)
        for h in range(HEADS_PER_GROUP):
            o_ref[:, kv, h, :] = x[:, h * HEAD_DIM:(h + 1) * HEAD_DIM]


def _kv_tail(zb, g, *, batch, seq):
    win, dil = SWA_PATTERN[g]
    rows = min(win, PERM_BLOCK)
    first = (seq - win) // rows
    per_batch = seq // rows

    def tile(t):
        return pl.BlockSpec((None, rows, GROUP_WIDTH), lambda b, s: (t, b * per_batch + first + s, 0))

    return pl.pallas_call(
        functools.partial(_kv_tail_kernel, dil=dil),
        grid=(batch, win // rows),
        in_specs=[tile(N_SWA_GROUPS + g), tile(2 * N_SWA_GROUPS + g)],
        out_specs=pl.BlockSpec((None, rows, 2, HEADS_PER_GROUP, HEAD_DIM), lambda b, s: (b, s, 0, 0, 0)),
        out_shape=jax.ShapeDtypeStruct((batch, win, 2, HEADS_PER_GROUP, HEAD_DIM), F32),
        compiler_params=_params("parallel", "parallel"),
        name=f"kv_tail{g}",
    )(zb, zb)


def kernel(x_prompt, x_sample, mem_prompt, cache_mem_kv, cache_win128_kv, cache_win512_kv, cache_win2048_kv, rel_bias, norm_mix_pre, norm_mix_post, norm_ffn_pre, norm_ffn_post, norm_mem, w_mem_kv, w_in_a, norm_v_a, w_spatial_a, b_spatial_a, w_in_b, w_out, w_ffn_up, w_ffn_down):
    batch, seq, _ = x_prompt.shape
    bd = x_sample.shape[0]
    depth = w_out.shape[0]
    m_p = batch * seq
    win_caches = (cache_win128_kv, cache_win512_kv, cache_win2048_kv)

    bias_groups = [_group_bias(rel_bias, g, dil) for g, (_, dil) in enumerate(SWA_PATTERN)]
    band_tables = _band_tables(bias_groups)

    yp = x_prompt.reshape(m_p, D_MODEL)
    ys = jnp.pad(x_sample.reshape(bd, D_MODEL), ((0, SAMPLE_PAD - bd), (0, 0)))
    mem_rows = mem_prompt.reshape(batch * N_MEM, D_MODEL)

    mem_kv_p = _mem_kv(mem_rows, norm_mem, w_mem_kv, batch=batch)
    chunk_v_s = []
    win_p = [[] for _ in SWA_PATTERN]
    win_s = [[] for _ in SWA_PATTERN]
    for i in range(depth):
        li = i // 2
        if i % 2 == 0:
            zs, w_in = _norm_matmul(ys, norm_mix_pre[i], w_in_a, li, tm=SAMPLE_PAD, tn=512, emit_w=True,
                                    gelu_cols=2 * MIXER_WIDTH, out_dtype=F32, name="in_proj_a_s")
            zp = _norm_matmul(yp, norm_mix_pre[i], w_in, None, tm=512, tn=1792,
                              gelu_cols=2 * MIXER_WIDTH, out_dtype=BF16, name="in_proj_a")
            mix_p = _gmlp_mix(zp, norm_v_a[li], w_spatial_a[li], b_spatial_a[li], mem_kv_p, i,
                              tm=512, rows_per_batch=seq)
            mix_s, v_rows = _sample_mix_a(zs, norm_v_a[li], w_spatial_a[li], b_spatial_a[li], cache_mem_kv, i)
            chunk_v_s.append(v_rows)
        else:
            zs, w_in = _norm_matmul(ys, norm_mix_pre[i], w_in_b, li, tm=SAMPLE_PAD, tn=512, emit_w=True,
                                    gelu_cols=0, out_dtype=F32, name="in_proj_b_s")
            zb = _in_proj_b(yp, norm_mix_pre[i], w_in, tm=1024, tiles_per_step=2)
            outs, lses = [], []
            for g, (win, dil) in enumerate(SWA_PATTERN):
                o, lse = _swa_group(zb, band_tables, g, batch=batch, seq=seq)
                outs.append(o)
                lses.append(lse)
                win_p[g].append(_kv_tail(zb, g, batch=batch, seq=seq))
                kv_new = zs[:bd, MIXER_WIDTH:3 * MIXER_WIDTH]
                kv_new = kv_new.reshape(bd, 1, 2, N_SWA_GROUPS, HEADS_PER_GROUP, HEAD_DIM)[:, :, :, g]
                win_s[g].append(kv_new)
            mix_p = _swa_merge(outs, lses, zb, mem_kv_p, i, rows_per_batch=seq)
            mix_s = _sample_mix_b(zs, win_caches, li, bias_groups, cache_mem_kv, i)
        mix_s = jnp.pad(mix_s.reshape(bd, D_MODEL), ((0, SAMPLE_PAD - bd), (0, 0))).astype(BF16)
        ys, w_o = _out_proj_cast(mix_s, w_out, i, ys, norm_mix_post[i], tn=512)
        yp = _out_proj(mix_p, w_o, yp, norm_mix_post[i], tm=512)
        ys, w_ffn = _ffn_cast(ys, norm_ffn_pre[i], norm_ffn_post[i], w_ffn_up, w_ffn_down, i, tf=512)
        yp = _ffn(yp, norm_ffn_pre[i], norm_ffn_post[i], w_ffn, tm=1024, tf=512, single_buffer=False)

    return (
        yp.reshape(batch, seq, D_MODEL),
        ys[:bd].reshape(bd, 1, D_MODEL),
        mem_kv_p,
        jnp.stack(chunk_v_s, axis=0),
        jnp.stack(win_p[0], axis=0),
        jnp.stack(win_p[1], axis=0),
        jnp.stack(win_p[2], axis=0),
        jnp.stack(win_s[0], axis=0),
        jnp.stack(win_s[1], axis=0),
        jnp.stack(win_s[2], axis=0),
    )
```

```python
import functools
import math

import jax
import jax.numpy as jnp
from jax import lax
from jax.experimental import pallas as pl
from jax.experimental.pallas import tpu as pltpu

F32 = jnp.float32
BF16 = jnp.bfloat16

D_MODEL = 2048
HEAD_DIM = 128
N_MEM = 256
N_MEM_HEADS = 4
MEM_WIDTH = N_MEM_HEADS * HEAD_DIM
MIXER_WIDTH = D_MODEL - MEM_WIDTH
CHUNK = 128
N_GROUPS_A = 4
GROUP_DIM_A = MIXER_WIDTH // N_GROUPS_A
SWA_PATTERN = ((128, 1), (512, 4), (2048, 16))
N_SWA_GROUPS = len(SWA_PATTERN)
HEADS_PER_GROUP = 4
GROUP_WIDTH = HEADS_PER_GROUP * HEAD_DIM
N_BACK = 128
N_BUCKETS = 32
MAX_EXACT = N_BUCKETS // 2
MAX_DISTANCE = 2048
D_FF = 5632
EPS = 1e-6
NEG_INF = -1e30
ATTN_SCALE = HEAD_DIM ** -0.5
SAMPLE_PAD = 16
PERM_BLOCK = 256
OUT_CHUNK = 256

ROW_TILE = 512
FFN_ROW_TILE = 1024
FF_TILE = 512
CAST_TILE = 512

V7X_VMEM_BYTES = 64 * 1024 * 1024
VMEM_LIMIT = V7X_VMEM_BYTES - 8 * 1024 * 1024
FFN_VMEM_LIMIT = V7X_VMEM_BYTES - 2 * 1024 * 1024


def _params(*sem, vmem_limit=VMEM_LIMIT):
    return pltpu.CompilerParams(dimension_semantics=sem, vmem_limit_bytes=vmem_limit)


def _gelu(x):
    return 0.5 * x * (1.0 + jnp.tanh(0.7978845608028654 * (x + 0.044715 * (x * x * x))))


def _rms(x, g):
    return x * lax.rsqrt(jnp.mean(x * x, axis=-1, keepdims=True) + EPS) * g


def _log2(n):
    assert n & (n - 1) == 0
    return n.bit_length() - 1


def _residue_major_perm(tm, dil, transpose=False):
    n = tm // dil
    row = lax.broadcasted_iota(jnp.int32, (tm, tm), 0)
    col = lax.broadcasted_iota(jnp.int32, (tm, tm), 1)
    dst, src = (col, row) if transpose else (row, col)
    want = lax.shift_left(jnp.bitwise_and(dst, n - 1), _log2(dil)) + lax.shift_right_logical(dst, _log2(n))
    return (src == want).astype(BF16)


def _norm_matmul_kernel(x_ref, g_ref, w_ref, o_ref, *rest, gelu_cols, scale_rest, emit_w):
    xn_ref = rest[-1]
    n = pl.program_id(1)

    @pl.when(n == 0)
    def _():
        xn_ref[...] = _rms(x_ref[...], g_ref[...]).astype(BF16)

    w = w_ref[...].astype(BF16)
    if emit_w:
        rest[0][...] = w
    acc = jnp.dot(xn_ref[...], w, preferred_element_type=F32)
    if gelu_cols > 0:
        col = lax.broadcasted_iota(jnp.int32, acc.shape, 1)
        other = acc * ATTN_SCALE if scale_rest else acc
        acc = jnp.where(col < gelu_cols - n * acc.shape[1], _gelu(acc), other)
    o_ref[...] = acc.astype(o_ref.dtype)


def _weight_spec(w, layer, block, index):
    if layer is None:
        mode = dict(pipeline_mode=pl.Buffered(1)) if tuple(block) == tuple(w.shape) else {}
        return pl.BlockSpec(block, index, **mode)
    return pl.BlockSpec((None,) + block, lambda *ids: (layer,) + index(*ids))


def _norm_matmul(x, g, w, layer, *, tm, tn, gelu_cols, out_dtype, name, emit_w=False, scale_rest=False):
    m, k = x.shape
    n = w.shape[-1]
    out_specs = [pl.BlockSpec((tm, tn), lambda i, j: (i, j))]
    out_shape = [jax.ShapeDtypeStruct((m, n), out_dtype)]
    if emit_w:
        assert m == tm
        out_specs.append(pl.BlockSpec((k, tn), lambda i, j: (0, j)))
        out_shape.append(jax.ShapeDtypeStruct((k, n), BF16))
    outs = pl.pallas_call(
        functools.partial(_norm_matmul_kernel, gelu_cols=gelu_cols, scale_rest=scale_rest, emit_w=emit_w),
        grid=(m // tm, n // tn),
        in_specs=[
            pl.BlockSpec((tm, k), lambda i, j: (i, 0)),
            pl.BlockSpec((1, k), lambda i, j: (0, 0)),
            _weight_spec(w, layer, (k, tn), lambda i, j: (0, j)),
        ],
        out_specs=out_specs,
        out_shape=out_shape,
        scratch_shapes=[pltpu.VMEM((tm, k), BF16)],
        compiler_params=_params("parallel", "arbitrary"),
        name=name,
    )(x, g.reshape(1, k), w)
    return outs if emit_w else outs[0]


def _mem_kv_kernel(x_ref, g_ref, w_ref, o_ref):
    xn = _rms(x_ref[...], g_ref[...]).astype(BF16)
    acc = jnp.dot(xn, w_ref[...].astype(BF16), preferred_element_type=F32)
    for kv in range(2):
        for h in range(N_MEM_HEADS):
            c0 = (kv * N_MEM_HEADS + h) * HEAD_DIM
            o_ref[:, kv, h, :] = acc[:, c0:c0 + HEAD_DIM]


def _mem_kv(mem_rows, g, w, *, batch):
    m, k = mem_rows.shape
    layers = w.shape[0]
    out = pl.pallas_call(
        _mem_kv_kernel,
        grid=(layers,),
        in_specs=[
            pl.BlockSpec((m, k), lambda l: (0, 0)),
            pl.BlockSpec((None, 1, k), lambda l: (l, 0, 0)),
            pl.BlockSpec((None, k, 2 * MEM_WIDTH), lambda l: (l, 0, 0)),
        ],
        out_specs=pl.BlockSpec((None, m, 2, N_MEM_HEADS, HEAD_DIM), lambda l: (l, 0, 0, 0, 0)),
        out_shape=jax.ShapeDtypeStruct((layers, m, 2, N_MEM_HEADS, HEAD_DIM), F32),
        compiler_params=_params("parallel"),
        name="mem_kv",
    )(mem_rows, g.reshape(layers, 1, k), w)
    return out.reshape(layers, batch, m // batch, 2, N_MEM_HEADS, HEAD_DIM)


def _in_proj_b_kernel(x_ref, g_ref, w_ref, o_ref, xn_ref):
    tm = x_ref.shape[0]
    n_qkv = 3 * N_SWA_GROUPS
    xn = _rms(x_ref[...], g_ref[...]).astype(BF16)
    xn_ref[0] = xn
    for g in range(1, N_SWA_GROUPS):
        perm = _residue_major_perm(PERM_BLOCK, SWA_PATTERN[g][1])
        for s in range(0, tm, PERM_BLOCK):
            xn_ref[g, s:s + PERM_BLOCK, :] = jnp.dot(
                perm, xn[s:s + PERM_BLOCK, :], preferred_element_type=F32).astype(BF16)

    for t in range(o_ref.shape[0]):
        src = t % N_SWA_GROUPS if t < n_qkv else 0
        w = w_ref[:, t * GROUP_WIDTH:(t + 1) * GROUP_WIDTH]
        acc = jnp.dot(xn_ref[src], w, preferred_element_type=F32)
        if t < N_SWA_GROUPS or t == n_qkv:
            acc = acc * ATTN_SCALE
        o_ref[t] = acc.astype(o_ref.dtype)


def _in_proj_b(x, g, w, *, tm):
    m, k = x.shape
    n_tiles = w.shape[1] // GROUP_WIDTH
    return pl.pallas_call(
        _in_proj_b_kernel,
        grid=(m // tm,),
        in_specs=[
            pl.BlockSpec((tm, k), lambda i: (i, 0)),
            pl.BlockSpec((1, k), lambda i: (0, 0)),
            _weight_spec(w, None, w.shape, lambda i: (0, 0)),
        ],
        out_specs=pl.BlockSpec((n_tiles, tm, GROUP_WIDTH), lambda i: (0, i, 0)),
        out_shape=jax.ShapeDtypeStruct((n_tiles, m, GROUP_WIDTH), BF16),
        scratch_shapes=[pltpu.VMEM((N_SWA_GROUPS, tm, k), BF16)],
        compiler_params=_params("parallel"),
        name="in_proj_b",
    )(x, g.reshape(1, k), w)


MEM_KV_SCRATCH = [pltpu.VMEM((N_MEM_HEADS, N_MEM, HEAD_DIM), BF16),
                  pltpu.VMEM((N_MEM_HEADS, N_MEM, 2 * HEAD_DIM), BF16)]


def _prepare_mem_kv(kv_ref, kb_ref, vb_ref):
    for h in range(N_MEM_HEADS):
        kb_ref[h] = kv_ref[:, 0, h, :].astype(BF16)
        vb_ref[h, :, :HEAD_DIM] = kv_ref[:, 1, h, :].astype(BF16)
        vb_ref[h, :, HEAD_DIM:] = jnp.ones((N_MEM, HEAD_DIM), BF16)


def _mem_attention_parts(q_ref, kb_ref, vb_ref, o_ref, col0):
    def head(h):
        lo, hi = h * HEAD_DIM, (h + 1) * HEAD_DIM
        s = lax.dot_general(q_ref[:, lo:hi], kb_ref[h], (((1,), (1,)), ((), ())), preferred_element_type=F32)
        m = jnp.max(s, axis=1, keepdims=True)
        p = jnp.exp(s - m).astype(BF16)
        ov = jnp.dot(p, vb_ref[h], preferred_element_type=F32)
        o_ref[:, col0 + lo:col0 + hi] = (ov[:, :HEAD_DIM] / ov[:, HEAD_DIM:]).astype(o_ref.dtype)

    return [functools.partial(head, h) for h in range(N_MEM_HEADS)]


def _gmlp_stage(u_ref, v_ref, q_ref, gv_ref, ws_ref, bs_ref, kb_ref, vb_ref, vn_ref, o_ref):
    tm = u_ref.shape[0]

    def norm_v():
        vn_ref[...] = _rms(v_ref[...].astype(F32), gv_ref[...]).astype(BF16)

    def group(g):
        row = lax.broadcasted_iota(jnp.int32, (CHUNK, CHUNK), 0)
        col = lax.broadcasted_iota(jnp.int32, (CHUNK, CHUNK), 1)
        w = jnp.where(row >= col, ws_ref[g], 0.0).astype(BF16)
        b = bs_ref[:, g:g + 1]
        c0, c1 = g * GROUP_DIM_A, (g + 1) * GROUP_DIM_A
        for c in range(tm // CHUNK):
            r0, r1 = c * CHUNK, (c + 1) * CHUNK
            s = jnp.dot(w, vn_ref[r0:r1, c0:c1], preferred_element_type=F32) + b
            o_ref[r0:r1, c0:c1] = (u_ref[r0:r1, c0:c1].astype(F32) * s).astype(o_ref.dtype)

    return ([norm_v] + [functools.partial(group, g) for g in range(N_GROUPS_A)]
            + _mem_attention_parts(q_ref, kb_ref, vb_ref, o_ref, MIXER_WIDTH))


def _mem_kv_spec(layer, batch_of):
    return pl.BlockSpec((None, None, N_MEM, 2, N_MEM_HEADS, HEAD_DIM),
                        lambda i: (layer, batch_of(i), 0, 0, 0, 0))


def _mix_out_kernel(*refs, stage, n_in, n_tiles, tiles_per_batch):
    mix_in = refs[:n_in]
    kv_ref, w_ref, x_ref, g_ref, o_ref, buf0_ref, buf1_ref, acc_ref, kb_ref, vb_ref = refs[n_in:n_in + 10]
    extra = refs[n_in + 10:]
    s = pl.program_id(0)

    @pl.when(s == 0)
    def _():
        buf1_ref[...] = jnp.zeros_like(buf1_ref)

    @pl.when(jnp.minimum(s, n_tiles - 1) % tiles_per_batch == 0)
    def _():
        _prepare_mem_kv(kv_ref, kb_ref, vb_ref)

    def step(dst_ref, src_ref):
        parts = stage(*mix_in, kb_ref, vb_ref, *extra, dst_ref)
        n_chunks = D_MODEL // OUT_CHUNK
        for c in range(n_chunks):
            cols = slice(c * OUT_CHUNK, (c + 1) * OUT_CHUNK)
            acc_ref[:, cols] = jnp.dot(src_ref[...], w_ref[:, cols], preferred_element_type=F32)
            for part in parts[c * len(parts) // n_chunks:(c + 1) * len(parts) // n_chunks]:
                part()
        o_ref[...] = x_ref[...] + _rms(acc_ref[...], g_ref[...])

    @pl.when(s % 2 == 0)
    def _():
        step(buf0_ref, buf1_ref)

    @pl.when(s % 2 == 1)
    def _():
        step(buf1_ref, buf0_ref)


def _mix_out(stage, mix_inputs, mix_specs, extra_scratch, mem_kv, layer, w, x, g, *, tm, rows_per_batch, name):
    m = x.shape[0]
    n_tiles = m // tm
    tiles_per_batch = rows_per_batch // tm
    mix_tile = lambda s: jnp.minimum(s, n_tiles - 1)
    out_tile = lambda s: jnp.maximum(s - 1, 0)
    row_spec = pl.BlockSpec((tm, D_MODEL), lambda s: (out_tile(s), 0))
    return pl.pallas_call(
        functools.partial(_mix_out_kernel, stage=stage, n_in=len(mix_inputs), n_tiles=n_tiles,
                          tiles_per_batch=tiles_per_batch),
        grid=(n_tiles + 1,),
        in_specs=[spec(mix_tile) for spec in mix_specs] + [
            _mem_kv_spec(layer, lambda s: mix_tile(s) // tiles_per_batch),
            pl.BlockSpec((D_MODEL, D_MODEL), lambda s: (0, 0), pipeline_mode=pl.Buffered(1)),
            row_spec,
            pl.BlockSpec((1, D_MODEL), lambda s: (0, 0)),
        ],
        out_specs=row_spec,
        out_shape=jax.ShapeDtypeStruct((m, D_MODEL), F32),
        scratch_shapes=[pltpu.VMEM((tm, D_MODEL), BF16), pltpu.VMEM((tm, D_MODEL), BF16),
                        pltpu.VMEM((tm, D_MODEL), F32)] + MEM_KV_SCRATCH + list(extra_scratch),
        compiler_params=_params("arbitrary"),
        name=name,
    )(*mix_inputs, mem_kv, w, x, g.reshape(1, D_MODEL))


def _gmlp_mix_out(zact, g_v, w_s, b_s, mem_kv, layer, w, x, g, *, tm, rows_per_batch):
    const = lambda shape: (lambda tile: pl.BlockSpec(shape, lambda s: (0,) * len(shape)))
    specs = [
        lambda tile: pl.BlockSpec((tm, MIXER_WIDTH), lambda s: (tile(s), 0)),
        lambda tile: pl.BlockSpec((tm, MIXER_WIDTH), lambda s: (tile(s), 1)),
        lambda tile: pl.BlockSpec((tm, MEM_WIDTH), lambda s: (tile(s), 2 * MIXER_WIDTH // MEM_WIDTH)),
        const((1, MIXER_WIDTH)),
        const((N_GROUPS_A, CHUNK, CHUNK)),
        const((CHUNK, N_GROUPS_A)),
    ]
    return _mix_out(_gmlp_stage, (zact, zact, zact, g_v.reshape(1, MIXER_WIDTH), w_s, b_s.T), specs,
                    [pltpu.VMEM((tm, MIXER_WIDTH), BF16)], mem_kv, layer, w, x, g,
                    tm=tm, rows_per_batch=rows_per_batch, name="gmlp_mix_out")


def _out_proj_cast_kernel(mix_ref, w_ref, x_ref, g_ref, o_ref, wb_ref, acc_ref):
    j = pl.program_id(0)
    w = w_ref[...].astype(BF16)
    wb_ref[...] = w
    acc_ref[j] = jnp.dot(mix_ref[...], w, preferred_element_type=F32)

    @pl.when(j == pl.num_programs(0) - 1)
    def _():
        o = jnp.concatenate([acc_ref[t] for t in range(acc_ref.shape[0])], axis=1)
        o_ref[...] = x_ref[...] + _rms(o, g_ref[...])


def _sample_out_proj(mix, w, layer, x, g):
    bd = mix.shape[0]
    mix = jnp.pad(mix.reshape(bd, D_MODEL), ((0, x.shape[0] - bd), (0, 0))).astype(BF16)
    return _out_proj_cast(mix, w, layer, x, g, tn=CAST_TILE)


def _out_proj_cast(mix, w, layer, x, g, *, tn):
    m = x.shape[0]
    n_tiles = D_MODEL // tn
    return pl.pallas_call(
        _out_proj_cast_kernel,
        grid=(n_tiles,),
        in_specs=[
            pl.BlockSpec((m, D_MODEL), lambda j: (0, 0)),
            pl.BlockSpec((None, D_MODEL, tn), lambda j: (layer, 0, j)),
            pl.BlockSpec((m, D_MODEL), lambda j: (0, 0)),
            pl.BlockSpec((1, D_MODEL), lambda j: (0, 0)),
        ],
        out_specs=[
            pl.BlockSpec((m, D_MODEL), lambda j: (0, 0)),
            pl.BlockSpec((D_MODEL, tn), lambda j: (0, j)),
        ],
        out_shape=[
            jax.ShapeDtypeStruct((m, D_MODEL), F32),
            jax.ShapeDtypeStruct((D_MODEL, D_MODEL), BF16),
        ],
        scratch_shapes=[pltpu.VMEM((n_tiles, m, tn), F32)],
        compiler_params=_params("arbitrary"),
        name="out_proj_cast",
    )(mix, w, x, g.reshape(1, D_MODEL))


def _ffn_kernel(x_ref, gpre_ref, gpost_ref, wg_ref, wl_ref, wd_ref, o_ref, *rest, emit_w):
    xn_ref = rest[-1]
    f = pl.program_id(1)

    @pl.when(f == 0)
    def _():
        xn_ref[...] = _rms(x_ref[...], gpre_ref[...]).astype(BF16)
        o_ref[...] = jnp.zeros_like(o_ref)

    wg, wl, wd = (r[...].astype(BF16) for r in (wg_ref, wl_ref, wd_ref))
    if emit_w:
        for r, w in zip(rest[:3], (wg, wl, wd)):
            r[...] = w
    xn = xn_ref[...]
    hg = jnp.dot(xn, wg, preferred_element_type=F32)
    hl = jnp.dot(xn, wl, preferred_element_type=F32)
    a = (hg * jax.nn.sigmoid(hg) * hl).astype(BF16)
    o_ref[...] += jnp.dot(a, wd, preferred_element_type=F32)

    @pl.when(f == pl.num_programs(1) - 1)
    def _():
        o_ref[...] = x_ref[...] + _rms(o_ref[...], gpost_ref[...])


def _ffn(x, g_pre, g_post, weights, *, tm, tf):
    m = x.shape[0]
    nf = D_FF // tf
    return pl.pallas_call(
        functools.partial(_ffn_kernel, emit_w=False),
        grid=(m // tm, nf),
        in_specs=[
            pl.BlockSpec((tm, D_MODEL), lambda i, f: (i, 0)),
            pl.BlockSpec((1, D_MODEL), lambda i, f: (0, 0)),
            pl.BlockSpec((1, D_MODEL), lambda i, f: (0, 0)),
            pl.BlockSpec((D_MODEL, tf), lambda i, f: (0, f)),
            pl.BlockSpec((D_MODEL, tf), lambda i, f: (0, f)),
            pl.BlockSpec((tf, D_MODEL), lambda i, f: (f, 0)),
        ],
        out_specs=pl.BlockSpec((tm, D_MODEL), lambda i, f: (i, 0)),
        out_shape=jax.ShapeDtypeStruct((m, D_MODEL), F32),
        scratch_shapes=[pltpu.VMEM((tm, D_MODEL), BF16)],
        compiler_params=_params("parallel", "arbitrary", vmem_limit=FFN_VMEM_LIMIT),
        name="ffn",
    )(x, g_pre.reshape(1, D_MODEL), g_post.reshape(1, D_MODEL), *weights)


def _ffn_cast(x, g_pre, g_post, w_up, w_down, layer, *, tf):
    m = x.shape[0]
    nf = D_FF // tf
    outs = pl.pallas_call(
        functools.partial(_ffn_kernel, emit_w=True),
        grid=(1, nf),
        in_specs=[
            pl.BlockSpec((m, D_MODEL), lambda i, f: (0, 0)),
            pl.BlockSpec((1, D_MODEL), lambda i, f: (0, 0)),
            pl.BlockSpec((1, D_MODEL), lambda i, f: (0, 0)),
            pl.BlockSpec((None, D_MODEL, tf), lambda i, f: (layer, 0, f)),
            pl.BlockSpec((None, D_MODEL, tf), lambda i, f: (layer, 0, nf + f)),
            pl.BlockSpec((None, tf, D_MODEL), lambda i, f: (layer, f, 0)),
        ],
        out_specs=[
            pl.BlockSpec((m, D_MODEL), lambda i, f: (0, 0)),
            pl.BlockSpec((D_MODEL, tf), lambda i, f: (0, f)),
            pl.BlockSpec((D_MODEL, tf), lambda i, f: (0, f)),
            pl.BlockSpec((tf, D_MODEL), lambda i, f: (f, 0)),
        ],
        out_shape=[
            jax.ShapeDtypeStruct((m, D_MODEL), F32),
            jax.ShapeDtypeStruct((D_MODEL, D_FF), BF16),
            jax.ShapeDtypeStruct((D_MODEL, D_FF), BF16),
            jax.ShapeDtypeStruct((D_FF, D_MODEL), BF16),
        ],
        scratch_shapes=[pltpu.VMEM((m, D_MODEL), BF16)],
        compiler_params=_params("parallel", "arbitrary"),
        name="ffn_cast",
    )(x, g_pre.reshape(1, D_MODEL), g_post.reshape(1, D_MODEL), w_up, w_up, w_down)
    return outs[0], tuple(outs[1:])


def _swa_kernel(q_ref, k_ref, v_ref, tb_ref, o_ref, lse_ref):
    n_units, n_res, u, _ = q_ref.shape
    per_blk = N_BACK // u
    n_blk = n_units // per_blk
    lane = lax.broadcasted_iota(jnp.int32, (N_BACK, HEAD_DIM), 1)
    ones = jnp.ones((2 * N_BACK, HEAD_DIM), BF16)

    def rows(ref, res, unit0, n_rows, lo, hi):
        return ref[pl.ds(unit0, n_rows // u), res, :, lo:hi].reshape(n_rows, hi - lo)

    def block(res, qu, ku, table):
        n_keys = N_BACK if table == 0 else 2 * N_BACK
        lse_tile = jnp.zeros((N_BACK, HEAD_DIM), F32)
        for h in range(HEADS_PER_GROUP):
            lo, hi = h * HEAD_DIM, (h + 1) * HEAD_DIM
            q = rows(q_ref, res, qu, N_BACK, lo, hi)
            kw = rows(k_ref, res, ku, n_keys, lo, hi)
            vw = rows(v_ref, res, ku, n_keys, lo, hi)
            s = lax.dot_general(q, kw, (((1,), (1,)), ((), ())), preferred_element_type=F32)
            s = s + tb_ref[table, h][:, :n_keys]
            m = jnp.max(s, axis=1, keepdims=True)
            p = jnp.exp(s - m).astype(BF16)
            ov = jnp.dot(p, jnp.concatenate([vw, ones[:n_keys]], axis=1), preferred_element_type=F32)
            den = ov[:, HEAD_DIM:]
            o = ov[:, :HEAD_DIM] / den
            o_ref[pl.ds(qu, per_blk), res, :, lo:hi] = o.reshape(per_blk, u, HEAD_DIM).astype(o_ref.dtype)
            lse_tile = jnp.where(lane == h, m + jnp.log(den), lse_tile)
        lse_ref[pl.ds(qu, per_blk), res, :, :] = lse_tile.reshape(per_blk, u, HEAD_DIM)

    def block_at(res, n):
        block(res, n * per_blk, (n - 1) * per_blk, 1)

    n_pairs = (n_blk - 1) // 2
    for res in range(n_res):
        block(res, 0, 0, 0)

        def body(i, carry, res=res):
            block_at(res, 1 + 2 * i)
            block_at(res, 2 + 2 * i)
            return carry

        if n_pairs > 0:
            lax.fori_loop(0, n_pairs, body, 0)
        if (n_blk - 1) % 2 == 1:
            block_at(res, n_blk - 1)


def _sub_block(dil):
    return N_BACK if dil == 1 else PERM_BLOCK


def _swa_group(zb, tables, g, *, batch, seq):
    dil = SWA_PATTERN[g][1]
    sub = _sub_block(dil)
    n_units, u = seq // sub, sub // dil
    n_res = 4 if n_units * u == 2 * N_BACK else 1
    view = zb.reshape(zb.shape[0], batch, n_units, dil, u, GROUP_WIDTH)

    def rows_in(tile):
        return pl.BlockSpec((None, None, n_units, n_res, u, GROUP_WIDTH), lambda i, r: (tile, i, 0, r, 0, 0))

    def rows_out(width):
        return pl.BlockSpec((None, n_units, n_res, u, width), lambda i, r: (i, 0, r, 0, 0))

    return pl.pallas_call(
        _swa_kernel,
        grid=(batch, dil // n_res),
        in_specs=[rows_in(g), rows_in(N_SWA_GROUPS + g), rows_in(2 * N_SWA_GROUPS + g),
                  pl.BlockSpec((None, 2, HEADS_PER_GROUP, N_BACK, 2 * N_BACK), lambda i, r: (g, 0, 0, 0, 0))],
        out_specs=[rows_out(GROUP_WIDTH), rows_out(HEAD_DIM)],
        out_shape=[
            jax.ShapeDtypeStruct((batch, n_units, dil, u, GROUP_WIDTH), BF16),
            jax.ShapeDtypeStruct((batch, n_units, dil, u, HEAD_DIM), F32),
        ],
        compiler_params=_params("parallel", "parallel"),
        name=f"swa_group{g}",
    )(view, view, view, tables)


def _split3(x):
    hi = x.astype(BF16)
    rest = x - hi.astype(F32)
    mid = rest.astype(BF16)
    lo = (rest - mid.astype(F32)).astype(BF16)
    return hi, mid, lo


def _merge_stage(o0_ref, o1_ref, o2_ref, l0_ref, l1_ref, l2_ref, q_ref, kb_ref, vb_ref, o_ref):
    tm = o_ref.shape[0]
    group_refs = ((o0_ref, l0_ref), (o1_ref, l1_ref), (o2_ref, l2_ref))
    outs, lses = [None] * N_SWA_GROUPS, [None] * N_SWA_GROUPS

    def token_order(g):
        o_g, l_g = group_refs[g]
        dil = SWA_PATTERN[g][1]
        o = o_g[...].reshape(tm, GROUP_WIDTH)
        l = l_g[...].reshape(tm, HEAD_DIM)
        if dil > 1:
            inv = _residue_major_perm(PERM_BLOCK, dil, transpose=True)
            l3 = _split3(l)
            o_nat, l_nat = [], []
            for s in range(0, tm, PERM_BLOCK):
                o_nat.append(jnp.dot(inv, o[s:s + PERM_BLOCK, :], preferred_element_type=F32))
                l_nat.append(sum(jnp.dot(inv, t[s:s + PERM_BLOCK, :], preferred_element_type=F32) for t in l3))
            o = jnp.concatenate(o_nat, axis=0)
            l = jnp.concatenate(l_nat, axis=0)
        outs[g] = o.astype(F32)
        lses[g] = l

    def merge_head(h):
        lo, hi = h * HEAD_DIM, (h + 1) * HEAD_DIM
        ls = [l[:, h:h + 1] for l in lses]
        mx = jnp.maximum(jnp.maximum(ls[0], ls[1]), ls[2])
        es = [jnp.exp(l - mx) for l in ls]
        tot = es[0] + es[1] + es[2]
        for g in range(N_SWA_GROUPS):
            alpha = es[g] / tot
            o_ref[:, g * GROUP_WIDTH + lo:g * GROUP_WIDTH + hi] = (outs[g][:, lo:hi] * alpha).astype(o_ref.dtype)

    return ([functools.partial(token_order, g) for g in range(N_SWA_GROUPS)]
            + [functools.partial(merge_head, h) for h in range(HEADS_PER_GROUP)]
            + _mem_attention_parts(q_ref, kb_ref, vb_ref, o_ref, MIXER_WIDTH))


def _swa_merge_out(outs, lses, zb, mem_kv, layer, w, x, g, *, rows_per_batch):
    tm = ROW_TILE
    tiles_per_batch = rows_per_batch // tm

    def group_tile(width, dil):
        sub = _sub_block(dil)
        return lambda tile: pl.BlockSpec(
            (None, tm // sub, dil, sub // dil, width),
            lambda s: (tile(s) // tiles_per_batch, tile(s) % tiles_per_batch, 0, 0, 0))

    specs = ([group_tile(GROUP_WIDTH, dil) for _, dil in SWA_PATTERN]
             + [group_tile(HEAD_DIM, dil) for _, dil in SWA_PATTERN]
             + [lambda tile: pl.BlockSpec((None, tm, MEM_WIDTH), lambda s: (3 * N_SWA_GROUPS, tile(s), 0))])
    return _mix_out(_merge_stage, (*outs, *lses, zb), specs, [], mem_kv, layer, w, x, g,
                    tm=tm, rows_per_batch=rows_per_batch, name="swa_merge_out")


def _sample_mem_attention(q_row, kv_ref, o_ref, col0):
    for h in range(N_MEM_HEADS):
        lo, hi = h * HEAD_DIM, (h + 1) * HEAD_DIM
        q = q_row[:, lo:hi]
        k = kv_ref[:, 0, h, :]
        v = kv_ref[:, 1, h, :]
        s = jnp.sum(k * q, axis=1, keepdims=True) * ATTN_SCALE
        m = jnp.max(s, axis=0, keepdims=True)
        p = jnp.exp(s - m)
        den = jnp.sum(p, axis=0, keepdims=True)
        o_ref[:, col0 + lo:col0 + hi] = jnp.sum(p * v, axis=0, keepdims=True) / den


def _sample_mix_a_kernel(z_ref, gv_ref, w0_ref, b0_ref, kv_ref, o_ref, vrow_ref):
    u = z_ref[:, 0:MIXER_WIDTH]
    v = _rms(z_ref[:, MIXER_WIDTH:2 * MIXER_WIDTH], gv_ref[...])
    vrow_ref[...] = v
    o_ref[:, 0:MIXER_WIDTH] = u * (w0_ref[...] * v + b0_ref[...])
    _sample_mem_attention(z_ref[:, 2 * MIXER_WIDTH:2 * MIXER_WIDTH + MEM_WIDTH], kv_ref, o_ref, MIXER_WIDTH)


def _sample_mix_a(z, g_v, w_s, b_s, mem_kv, layer):
    bd = mem_kv.shape[1]
    w0 = jnp.repeat(w_s[:, 0, 0], GROUP_DIM_A).reshape(1, MIXER_WIDTH)
    b0 = jnp.repeat(b_s[:, 0], GROUP_DIM_A).reshape(1, MIXER_WIDTH)
    width = z.shape[1]
    vec = lambda i: (0, 0)
    return pl.pallas_call(
        _sample_mix_a_kernel,
        grid=(bd,),
        in_specs=[
            pl.BlockSpec((None, 1, width), lambda i: (i, 0, 0)),
            pl.BlockSpec((1, MIXER_WIDTH), vec),
            pl.BlockSpec((1, MIXER_WIDTH), vec),
            pl.BlockSpec((1, MIXER_WIDTH), vec),
            _mem_kv_spec(layer, lambda i: i),
        ],
        out_specs=[
            pl.BlockSpec((None, 1, D_MODEL), lambda i: (i, 0, 0)),
            pl.BlockSpec((None, 1, MIXER_WIDTH), lambda i: (i, 0, 0)),
        ],
        out_shape=[
            jax.ShapeDtypeStruct((bd, 1, D_MODEL), F32),
            jax.ShapeDtypeStruct((bd, 1, MIXER_WIDTH), F32),
        ],
        compiler_params=_params("parallel"),
        name="sample_mix_a",
    )(z[:bd].reshape(bd, 1, width), g_v.reshape(1, MIXER_WIDTH), w0, b0, mem_kv)


def _sample_mix_b_kernel(z_ref, c0_ref, c1_ref, c2_ref, bcol_ref, bnew_ref, kv_ref, o_ref):
    caches = (c0_ref, c1_ref, c2_ref)
    outs = [[None] * HEADS_PER_GROUP for _ in range(N_SWA_GROUPS)]
    lses = [[None] * HEADS_PER_GROUP for _ in range(N_SWA_GROUPS)]
    for g in range(N_SWA_GROUPS):
        for h in range(HEADS_PER_GROUP):
            hd = g * HEADS_PER_GROUP + h
            lo, hi = h * HEAD_DIM, (h + 1) * HEAD_DIM
            q = z_ref[:, hd * HEAD_DIM:(hd + 1) * HEAD_DIM]
            k_new = z_ref[:, MIXER_WIDTH + hd * HEAD_DIM:MIXER_WIDTH + (hd + 1) * HEAD_DIM]
            v_new = z_ref[:, 2 * MIXER_WIDTH + hd * HEAD_DIM:2 * MIXER_WIDTH + (hd + 1) * HEAD_DIM]
            kc = caches[g][:, 0, h, :]
            vc = caches[g][:, 1, h, :]
            s_c = jnp.sum(kc * q, axis=1, keepdims=True) * ATTN_SCALE + bcol_ref[g][:, h:h + 1]
            s_n = jnp.sum(k_new * q, axis=1, keepdims=True) * ATTN_SCALE + bnew_ref[:, hd:hd + 1]
            m = jnp.maximum(jnp.max(s_c, axis=0, keepdims=True), s_n)
            p_c = jnp.exp(s_c - m)
            p_n = jnp.exp(s_n - m)
            den = jnp.sum(p_c, axis=0, keepdims=True) + p_n
            outs[g][h] = (jnp.sum(p_c * vc, axis=0, keepdims=True) + p_n * v_new) / den
            lses[g][h] = m + jnp.log(den)
    for h in range(HEADS_PER_GROUP):
        ls = [lses[g][h] for g in range(N_SWA_GROUPS)]
        mx = jnp.maximum(jnp.maximum(ls[0], ls[1]), ls[2])
        es = [jnp.exp(l - mx) for l in ls]
        tot = es[0] + es[1] + es[2]
        for g in range(N_SWA_GROUPS):
            c0 = g * GROUP_WIDTH + h * HEAD_DIM
            o_ref[:, c0:c0 + HEAD_DIM] = outs[g][h] * (es[g] / tot)
    _sample_mem_attention(z_ref[:, 3 * MIXER_WIDTH:3 * MIXER_WIDTH + MEM_WIDTH], kv_ref, o_ref, MIXER_WIDTH)


def _sample_mix_b(z, win_caches, swa_layer, bias_groups, mem_kv, layer):
    bd = mem_kv.shape[1]
    width = z.shape[1]
    cache_views, cache_specs = [], []
    for g, (win, dil) in enumerate(SWA_PATTERN):
        c = win_caches[g]
        cache_views.append(c.reshape(c.shape[0], bd, win // dil, dil, 2, HEADS_PER_GROUP, HEAD_DIM))
        cache_specs.append(pl.BlockSpec((None, None, N_BACK, None, 2, HEADS_PER_GROUP, HEAD_DIM),
                                        lambda i: (swa_layer, i, 0, 0, 0, 0, 0)))
    bcol = jnp.stack([bg[:, N_BACK:0:-1].T for bg in bias_groups], axis=0)
    bnew = jnp.concatenate([bg[:, 0] for bg in bias_groups])
    bnew = jnp.pad(bnew, (0, HEAD_DIM - bnew.shape[0])).reshape(1, HEAD_DIM)
    return pl.pallas_call(
        _sample_mix_b_kernel,
        grid=(bd,),
        in_specs=[pl.BlockSpec((None, 1, width), lambda i: (i, 0, 0))] + cache_specs + [
            pl.BlockSpec((N_SWA_GROUPS, N_BACK, HEADS_PER_GROUP), lambda i: (0, 0, 0)),
            pl.BlockSpec((1, HEAD_DIM), lambda i: (0, 0)),
            _mem_kv_spec(layer, lambda i: i),
        ],
        out_specs=pl.BlockSpec((None, 1, D_MODEL), lambda i: (i, 0, 0)),
        out_shape=jax.ShapeDtypeStruct((bd, 1, D_MODEL), F32),
        compiler_params=_params("parallel"),
        name="sample_mix_b",
    )(z[:bd].reshape(bd, 1, width), *cache_views, bcol, bnew, mem_kv)


def _t5_bucket(dist):
    nf = jnp.maximum(dist, MAX_EXACT).astype(F32)
    large = MAX_EXACT + (jnp.log(nf / MAX_EXACT) / math.log(MAX_DISTANCE / MAX_EXACT)
                         * (N_BUCKETS - MAX_EXACT)).astype(jnp.int32)
    large = jnp.minimum(large, N_BUCKETS - 1)
    return jnp.where(dist < MAX_EXACT, dist, large)


def _group_bias(rel_bias, g, dil):
    dist = jnp.arange(N_BACK + 1, dtype=jnp.int32) * dil
    b = rel_bias[_t5_bucket(dist)][:, g * HEADS_PER_GROUP:(g + 1) * HEADS_PER_GROUP]
    return b.T.astype(F32)


def _band_tables_kernel(b_ref, o_ref):
    n, rows, width = o_ref.shape
    for x in range(n):
        row = jnp.broadcast_to(b_ref[x:x + 1, :], (rows, width))
        o_ref[x] = pltpu.roll(row, 0, 1, stride=1, stride_axis=0)


def _band_tables(bias_groups):
    width = 2 * N_BACK
    rows = []
    for bias_j in bias_groups:
        masked = jnp.full((HEADS_PER_GROUP, N_BACK - 1), NEG_INF, F32)
        rows.append(jnp.concatenate([bias_j[:, :1], masked, bias_j[:, N_BACK:0:-1]], axis=1))
        rows.append(jnp.concatenate([bias_j[:, ::-1], masked], axis=1))
    base = jnp.stack(rows, axis=0).reshape(-1, width)
    tabs = pl.pallas_call(
        _band_tables_kernel,
        out_shape=jax.ShapeDtypeStruct((base.shape[0], N_BACK, width), F32),
        name="band_tables",
    )(base)
    return tabs.reshape(N_SWA_GROUPS, 2, HEADS_PER_GROUP, N_BACK, width)


def _kv_tail_kernel(k_ref, v_ref, o_ref, *, dil):
    rows = k_ref.shape[0]
    for kv, ref in enumerate((k_ref, v_ref)):
        x = ref[...]
        if dil > 1:
            inv = _residue_major_perm(PERM_BLOCK, dil, transpose=True)
            x = jnp.concatenate([jnp.dot(inv, x[s:s + PERM_BLOCK, :], preferred_element_type=F32)
                                 for s in range(0, rows, PERM_BLOCK)], axis=0)
        x = x.astype(F32)
        for h in range(HEADS_PER_GROUP):
            o_ref[:, kv, h, :] = x[:, h * HEAD_DIM:(h + 1) * HEAD_DIM]


def _kv_tail(zb, g, *, batch, seq):
    win, dil = SWA_PATTERN[g]
    rows = min(win, PERM_BLOCK)
    first = (seq - win) // rows
    per_batch = seq // rows

    def tile(t):
        return pl.BlockSpec((None, rows, GROUP_WIDTH), lambda b, s: (t, b * per_batch + first + s, 0))

    return pl.pallas_call(
        functools.partial(_kv_tail_kernel, dil=dil),
        grid=(batch, win // rows),
        in_specs=[tile(N_SWA_GROUPS + g), tile(2 * N_SWA_GROUPS + g)],
        out_specs=pl.BlockSpec((None, rows, 2, HEADS_PER_GROUP, HEAD_DIM), lambda b, s: (b, s, 0, 0, 0)),
        out_shape=jax.ShapeDtypeStruct((batch, win, 2, HEADS_PER_GROUP, HEAD_DIM), F32),
        compiler_params=_params("parallel", "parallel"),
        name=f"kv_tail{g}",
    )(zb, zb)


def kernel(x_prompt, x_sample, mem_prompt, cache_mem_kv, cache_win128_kv, cache_win512_kv, cache_win2048_kv, rel_bias, norm_mix_pre, norm_mix_post, norm_ffn_pre, norm_ffn_post, norm_mem, w_mem_kv, w_in_a, norm_v_a, w_spatial_a, b_spatial_a, w_in_b, w_out, w_ffn_up, w_ffn_down):
    batch, seq, _ = x_prompt.shape
    bd = x_sample.shape[0]
    depth = w_out.shape[0]
    m_p = batch * seq
    win_caches = (cache_win128_kv, cache_win512_kv, cache_win2048_kv)

    bias_groups = [_group_bias(rel_bias, g, dil) for g, (_, dil) in enumerate(SWA_PATTERN)]
    band_tables = _band_tables(bias_groups)

    yp = x_prompt.reshape(m_p, D_MODEL)
    ys = jnp.pad(x_sample.reshape(bd, D_MODEL), ((0, SAMPLE_PAD - bd), (0, 0)))
    mem_rows = mem_prompt.reshape(batch * N_MEM, D_MODEL)

    mem_kv_p = _mem_kv(mem_rows, norm_mem, w_mem_kv, batch=batch)
    chunk_v_s = []
    win_p = [[] for _ in SWA_PATTERN]
    win_s = [[] for _ in SWA_PATTERN]
    for i in range(depth):
        li = i // 2
        if i % 2 == 0:
            zs, w_in = _norm_matmul(ys, norm_mix_pre[i], w_in_a, li, tm=SAMPLE_PAD, tn=CAST_TILE, emit_w=True,
                                    gelu_cols=2 * MIXER_WIDTH, out_dtype=F32, name="in_proj_a_s")
            zp = _norm_matmul(yp, norm_mix_pre[i], w_in, None, tm=ROW_TILE, tn=w_in.shape[1], scale_rest=True,
                              gelu_cols=2 * MIXER_WIDTH, out_dtype=BF16, name="in_proj_a")
            mix_s, v_rows = _sample_mix_a(zs, norm_v_a[li], w_spatial_a[li], b_spatial_a[li], cache_mem_kv, i)
            chunk_v_s.append(v_rows)
            ys, w_o = _sample_out_proj(mix_s, w_out, i, ys, norm_mix_post[i])
            yp = _gmlp_mix_out(zp, norm_v_a[li], w_spatial_a[li], b_spatial_a[li], mem_kv_p, i,
                               w_o, yp, norm_mix_post[i], tm=ROW_TILE, rows_per_batch=seq)
        else:
            zs, w_in = _norm_matmul(ys, norm_mix_pre[i], w_in_b, li, tm=SAMPLE_PAD, tn=CAST_TILE, emit_w=True,
                                    gelu_cols=0, out_dtype=F32, name="in_proj_b_s")
            zb = _in_proj_b(yp, norm_mix_pre[i], w_in, tm=ROW_TILE)
            outs, lses = [], []
            for g, (win, dil) in enumerate(SWA_PATTERN):
                o, lse = _swa_group(zb, band_tables, g, batch=batch, seq=seq)
                outs.append(o)
                lses.append(lse)
                win_p[g].append(_kv_tail(zb, g, batch=batch, seq=seq))
                kv_new = zs[:bd, MIXER_WIDTH:3 * MIXER_WIDTH]
                kv_new = kv_new.reshape(bd, 1, 2, N_SWA_GROUPS, HEADS_PER_GROUP, HEAD_DIM)[:, :, :, g]
                win_s[g].append(kv_new)
            mix_s = _sample_mix_b(zs, win_caches, li, bias_groups, cache_mem_kv, i)
            ys, w_o = _sample_out_proj(mix_s, w_out, i, ys, norm_mix_post[i])
            yp = _swa_merge_out(outs, lses, zb, mem_kv_p, i, w_o, yp, norm_mix_post[i], rows_per_batch=seq)
        ys, w_ffn = _ffn_cast(ys, norm_ffn_pre[i], norm_ffn_post[i], w_ffn_up, w_ffn_down, i, tf=FF_TILE)
        yp = _ffn(yp, norm_ffn_pre[i], norm_ffn_post[i], w_ffn, tm=FFN_ROW_TILE, tf=FF_TILE)

    return (
        yp.reshape(batch, seq, D_MODEL),
        ys[:bd].reshape(bd, 1, D_MODEL),
        mem_kv_p,
        jnp.stack(chunk_v_s, axis=0),
        jnp.stack(win_p[0], axis=0),
        jnp.stack(win_p[1], axis=0),
        jnp.stack(win_p[2], axis=0),
        jnp.stack(win_s[0], axis=0),
        jnp.stack(win_s[1], axis=0),
        jnp.stack(win_s[2], axis=0),
    )
```

```python
import functools
import math

import jax
import jax.numpy as jnp
from jax import lax
from jax.experimental import pallas as pl
from jax.experimental.pallas import tpu as pltpu

F32 = jnp.float32
BF16 = jnp.bfloat16

D_MODEL = 2048
HEAD_DIM = 128
N_MEM = 256
N_MEM_HEADS = 4
MEM_WIDTH = N_MEM_HEADS * HEAD_DIM
MIXER_WIDTH = D_MODEL - MEM_WIDTH
CHUNK = 128
N_GROUPS_A = 4
GROUP_DIM_A = MIXER_WIDTH // N_GROUPS_A
SWA_PATTERN = ((128, 1), (512, 4), (2048, 16))
N_SWA_GROUPS = len(SWA_PATTERN)
HEADS_PER_GROUP = 4
GROUP_WIDTH = HEADS_PER_GROUP * HEAD_DIM
N_BACK = 128
N_BUCKETS = 32
MAX_EXACT = N_BUCKETS // 2
MAX_DISTANCE = 2048
D_FF = 5632
EPS = 1e-6
NEG_INF = -1e30
ATTN_SCALE = HEAD_DIM ** -0.5
SAMPLE_PAD = 16
PERM_BLOCK = 256
OUT_CHUNK = 256

ROW_TILE = 512
FFN_ROW_TILE = 1024
FF_TILE = 512
HEAD_FF_TILE = 256
CAST_TILE = 512

V7X_VMEM_BYTES = 64 * 1024 * 1024
VMEM_LIMIT = V7X_VMEM_BYTES - 8 * 1024 * 1024
FFN_VMEM_LIMIT = V7X_VMEM_BYTES - 2 * 1024 * 1024


def _params(*sem, vmem_limit=VMEM_LIMIT):
    return pltpu.CompilerParams(dimension_semantics=sem, vmem_limit_bytes=vmem_limit)


def _gelu(x):
    return 0.5 * x * (1.0 + jnp.tanh(0.7978845608028654 * (x + 0.044715 * (x * x * x))))


def _rms(x, g):
    return x * lax.rsqrt(jnp.mean(x * x, axis=-1, keepdims=True) + EPS) * g


def _log2(n):
    assert n & (n - 1) == 0
    return n.bit_length() - 1


def _residue_major_perm(tm, dil, transpose=False):
    n = tm // dil
    row = lax.broadcasted_iota(jnp.int32, (tm, tm), 0)
    col = lax.broadcasted_iota(jnp.int32, (tm, tm), 1)
    dst, src = (col, row) if transpose else (row, col)
    want = lax.shift_left(jnp.bitwise_and(dst, n - 1), _log2(dil)) + lax.shift_right_logical(dst, _log2(n))
    return (src == want).astype(BF16)


def _norm_matmul_kernel(x_ref, g_ref, w_ref, o_ref, *rest, gelu_cols, scale_rest, emit_w):
    xn_ref = rest[-1]
    n = pl.program_id(1)

    @pl.when(n == 0)
    def _():
        xn_ref[...] = _rms(x_ref[...], g_ref[...]).astype(BF16)

    w = w_ref[...].astype(BF16)
    if emit_w:
        rest[0][...] = w
    acc = jnp.dot(xn_ref[...], w, preferred_element_type=F32)
    if gelu_cols > 0:
        col = lax.broadcasted_iota(jnp.int32, acc.shape, 1)
        other = acc * ATTN_SCALE if scale_rest else acc
        acc = jnp.where(col < gelu_cols - n * acc.shape[1], _gelu(acc), other)
    o_ref[...] = acc.astype(o_ref.dtype)


def _weight_spec(w, layer, block, index):
    if layer is None:
        mode = dict(pipeline_mode=pl.Buffered(1)) if tuple(block) == tuple(w.shape) else {}
        return pl.BlockSpec(block, index, **mode)
    return pl.BlockSpec((None,) + block, lambda *ids: (layer,) + index(*ids))


def _norm_matmul(x, g, w, layer, *, tm, tn, gelu_cols, out_dtype, name, emit_w=False, scale_rest=False):
    m, k = x.shape
    n = w.shape[-1]
    out_specs = [pl.BlockSpec((tm, tn), lambda i, j: (i, j))]
    out_shape = [jax.ShapeDtypeStruct((m, n), out_dtype)]
    if emit_w:
        assert m == tm
        out_specs.append(pl.BlockSpec((k, tn), lambda i, j: (0, j)))
        out_shape.append(jax.ShapeDtypeStruct((k, n), BF16))
    outs = pl.pallas_call(
        functools.partial(_norm_matmul_kernel, gelu_cols=gelu_cols, scale_rest=scale_rest, emit_w=emit_w),
        grid=(m // tm, n // tn),
        in_specs=[
            pl.BlockSpec((tm, k), lambda i, j: (i, 0)),
            pl.BlockSpec((1, k), lambda i, j: (0, 0)),
            _weight_spec(w, layer, (k, tn), lambda i, j: (0, j)),
        ],
        out_specs=out_specs,
        out_shape=out_shape,
        scratch_shapes=[pltpu.VMEM((tm, k), BF16)],
        compiler_params=_params("parallel", "arbitrary"),
        name=name,
    )(x, g.reshape(1, k), w)
    return outs if emit_w else outs[0]


def _mem_kv_kernel(x_ref, g_ref, w_ref, o_ref):
    xn = _rms(x_ref[...], g_ref[...]).astype(BF16)
    acc = jnp.dot(xn, w_ref[...].astype(BF16), preferred_element_type=F32)
    for kv in range(2):
        for h in range(N_MEM_HEADS):
            c0 = (kv * N_MEM_HEADS + h) * HEAD_DIM
            o_ref[:, kv, h, :] = acc[:, c0:c0 + HEAD_DIM]


def _mem_kv(mem_rows, g, w, *, batch):
    m, k = mem_rows.shape
    layers = w.shape[0]
    out = pl.pallas_call(
        _mem_kv_kernel,
        grid=(layers,),
        in_specs=[
            pl.BlockSpec((m, k), lambda l: (0, 0)),
            pl.BlockSpec((None, 1, k), lambda l: (l, 0, 0)),
            pl.BlockSpec((None, k, 2 * MEM_WIDTH), lambda l: (l, 0, 0)),
        ],
        out_specs=pl.BlockSpec((None, m, 2, N_MEM_HEADS, HEAD_DIM), lambda l: (l, 0, 0, 0, 0)),
        out_shape=jax.ShapeDtypeStruct((layers, m, 2, N_MEM_HEADS, HEAD_DIM), F32),
        compiler_params=_params("parallel"),
        name="mem_kv",
    )(mem_rows, g.reshape(layers, 1, k), w)
    return out.reshape(layers, batch, m // batch, 2, N_MEM_HEADS, HEAD_DIM)


def _in_proj_b_kernel(x_ref, g_ref, w_ref, o_ref, xn_ref):
    tm = x_ref.shape[0]
    n_qkv = 3 * N_SWA_GROUPS
    xn = _rms(x_ref[...], g_ref[...]).astype(BF16)
    xn_ref[0] = xn
    for g in range(1, N_SWA_GROUPS):
        perm = _residue_major_perm(PERM_BLOCK, SWA_PATTERN[g][1])
        for s in range(0, tm, PERM_BLOCK):
            xn_ref[g, s:s + PERM_BLOCK, :] = jnp.dot(
                perm, xn[s:s + PERM_BLOCK, :], preferred_element_type=F32).astype(BF16)

    for t in range(o_ref.shape[0]):
        src = t % N_SWA_GROUPS if t < n_qkv else 0
        w = w_ref[:, t * GROUP_WIDTH:(t + 1) * GROUP_WIDTH]
        acc = jnp.dot(xn_ref[src], w, preferred_element_type=F32)
        if t < N_SWA_GROUPS or t == n_qkv:
            acc = acc * ATTN_SCALE
        o_ref[t] = acc.astype(o_ref.dtype)


def _in_proj_b(x, g, w, *, tm):
    m, k = x.shape
    n_tiles = w.shape[1] // GROUP_WIDTH
    return pl.pallas_call(
        _in_proj_b_kernel,
        grid=(m // tm,),
        in_specs=[
            pl.BlockSpec((tm, k), lambda i: (i, 0)),
            pl.BlockSpec((1, k), lambda i: (0, 0)),
            _weight_spec(w, None, w.shape, lambda i: (0, 0)),
        ],
        out_specs=pl.BlockSpec((n_tiles, tm, GROUP_WIDTH), lambda i: (0, i, 0)),
        out_shape=jax.ShapeDtypeStruct((n_tiles, m, GROUP_WIDTH), BF16),
        scratch_shapes=[pltpu.VMEM((N_SWA_GROUPS, tm, k), BF16)],
        compiler_params=_params("parallel"),
        name="in_proj_b",
    )(x, g.reshape(1, k), w)


MEM_KV_SCRATCH = [pltpu.VMEM((N_MEM_HEADS, N_MEM, HEAD_DIM), BF16),
                  pltpu.VMEM((N_MEM_HEADS, N_MEM, 2 * HEAD_DIM), BF16)]


def _prepare_mem_kv(kv_ref, kb_ref, vb_ref):
    for h in range(N_MEM_HEADS):
        kb_ref[h] = kv_ref[:, 0, h, :].astype(BF16)
        vb_ref[h, :, :HEAD_DIM] = kv_ref[:, 1, h, :].astype(BF16)
        vb_ref[h, :, HEAD_DIM:] = jnp.ones((N_MEM, HEAD_DIM), BF16)


def _mem_attention_parts(q_ref, kb_ref, vb_ref, o_ref, col0):
    def head(h):
        lo, hi = h * HEAD_DIM, (h + 1) * HEAD_DIM
        s = lax.dot_general(q_ref[:, lo:hi], kb_ref[h], (((1,), (1,)), ((), ())), preferred_element_type=F32)
        m = jnp.max(s, axis=1, keepdims=True)
        p = jnp.exp(s - m).astype(BF16)
        ov = jnp.dot(p, vb_ref[h], preferred_element_type=F32)
        o_ref[:, col0 + lo:col0 + hi] = (ov[:, :HEAD_DIM] / ov[:, HEAD_DIM:]).astype(o_ref.dtype)

    return [functools.partial(head, h) for h in range(N_MEM_HEADS)]


def _gmlp_stage(u_ref, v_ref, q_ref, gv_ref, ws_ref, bs_ref, kb_ref, vb_ref, vn_ref, o_ref):
    tm = u_ref.shape[0]

    def norm_v():
        vn_ref[...] = _rms(v_ref[...].astype(F32), gv_ref[...]).astype(BF16)

    def group(g):
        row = lax.broadcasted_iota(jnp.int32, (CHUNK, CHUNK), 0)
        col = lax.broadcasted_iota(jnp.int32, (CHUNK, CHUNK), 1)
        w = jnp.where(row >= col, ws_ref[g], 0.0).astype(BF16)
        b = bs_ref[:, g:g + 1]
        c0, c1 = g * GROUP_DIM_A, (g + 1) * GROUP_DIM_A
        for c in range(tm // CHUNK):
            r0, r1 = c * CHUNK, (c + 1) * CHUNK
            s = jnp.dot(w, vn_ref[r0:r1, c0:c1], preferred_element_type=F32) + b
            o_ref[r0:r1, c0:c1] = (u_ref[r0:r1, c0:c1].astype(F32) * s).astype(o_ref.dtype)

    return ([norm_v] + [functools.partial(group, g) for g in range(N_GROUPS_A)]
            + _mem_attention_parts(q_ref, kb_ref, vb_ref, o_ref, MIXER_WIDTH))


def _mem_kv_spec(layer, batch_of):
    return pl.BlockSpec((None, None, N_MEM, 2, N_MEM_HEADS, HEAD_DIM),
                        lambda i: (layer, batch_of(i), 0, 0, 0, 0))


def _mix_out_kernel(*refs, stage, n_in, n_tiles, tiles_per_batch):
    mix_in = refs[:n_in]
    kv_ref, w_ref, x_ref, g_ref, o_ref, buf0_ref, buf1_ref, acc_ref, kb_ref, vb_ref = refs[n_in:n_in + 10]
    extra = refs[n_in + 10:]
    s = pl.program_id(0)

    @pl.when(s == 0)
    def _():
        buf1_ref[...] = jnp.zeros_like(buf1_ref)

    @pl.when(jnp.minimum(s, n_tiles - 1) % tiles_per_batch == 0)
    def _():
        _prepare_mem_kv(kv_ref, kb_ref, vb_ref)

    def step(dst_ref, src_ref):
        parts = stage(*mix_in, kb_ref, vb_ref, *extra, dst_ref)
        n_chunks = D_MODEL // OUT_CHUNK
        for c in range(n_chunks):
            cols = slice(c * OUT_CHUNK, (c + 1) * OUT_CHUNK)
            acc_ref[:, cols] = jnp.dot(src_ref[...], w_ref[:, cols], preferred_element_type=F32)
            for part in parts[c * len(parts) // n_chunks:(c + 1) * len(parts) // n_chunks]:
                part()
        o_ref[...] = x_ref[...] + _rms(acc_ref[...], g_ref[...])

    @pl.when(s % 2 == 0)
    def _():
        step(buf0_ref, buf1_ref)

    @pl.when(s % 2 == 1)
    def _():
        step(buf1_ref, buf0_ref)


def _mix_out(stage, mix_inputs, mix_specs, extra_scratch, mem_kv, layer, w, x, g, *, tm, rows_per_batch, name):
    m = x.shape[0]
    n_tiles = m // tm
    tiles_per_batch = rows_per_batch // tm
    mix_tile = lambda s: jnp.minimum(s, n_tiles - 1)
    out_tile = lambda s: jnp.maximum(s - 1, 0)
    row_spec = pl.BlockSpec((tm, D_MODEL), lambda s: (out_tile(s), 0))
    return pl.pallas_call(
        functools.partial(_mix_out_kernel, stage=stage, n_in=len(mix_inputs), n_tiles=n_tiles,
                          tiles_per_batch=tiles_per_batch),
        grid=(n_tiles + 1,),
        in_specs=[spec(mix_tile) for spec in mix_specs] + [
            _mem_kv_spec(layer, lambda s: mix_tile(s) // tiles_per_batch),
            pl.BlockSpec((D_MODEL, D_MODEL), lambda s: (0, 0), pipeline_mode=pl.Buffered(1)),
            row_spec,
            pl.BlockSpec((1, D_MODEL), lambda s: (0, 0)),
        ],
        out_specs=row_spec,
        out_shape=jax.ShapeDtypeStruct((m, D_MODEL), F32),
        scratch_shapes=[pltpu.VMEM((tm, D_MODEL), BF16), pltpu.VMEM((tm, D_MODEL), BF16),
                        pltpu.VMEM((tm, D_MODEL), F32)] + MEM_KV_SCRATCH + list(extra_scratch),
        compiler_params=_params("arbitrary"),
        name=name,
    )(*mix_inputs, mem_kv, w, x, g.reshape(1, D_MODEL))


def _gmlp_mix_out(zact, g_v, w_s, b_s, mem_kv, layer, w, x, g, *, tm, rows_per_batch):
    const = lambda shape: (lambda tile: pl.BlockSpec(shape, lambda s: (0,) * len(shape)))
    specs = [
        lambda tile: pl.BlockSpec((tm, MIXER_WIDTH), lambda s: (tile(s), 0)),
        lambda tile: pl.BlockSpec((tm, MIXER_WIDTH), lambda s: (tile(s), 1)),
        lambda tile: pl.BlockSpec((tm, MEM_WIDTH), lambda s: (tile(s), 2 * MIXER_WIDTH // MEM_WIDTH)),
        const((1, MIXER_WIDTH)),
        const((N_GROUPS_A, CHUNK, CHUNK)),
        const((CHUNK, N_GROUPS_A)),
    ]
    return _mix_out(_gmlp_stage, (zact, zact, zact, g_v.reshape(1, MIXER_WIDTH), w_s, b_s.T), specs,
                    [pltpu.VMEM((tm, MIXER_WIDTH), BF16)], mem_kv, layer, w, x, g,
                    tm=tm, rows_per_batch=rows_per_batch, name="gmlp_mix_out")


def _out_proj_cast_kernel(mix_ref, w_ref, x_ref, g_ref, o_ref, wb_ref, acc_ref):
    j = pl.program_id(0)
    w = w_ref[...].astype(BF16)
    wb_ref[...] = w
    acc_ref[j] = jnp.dot(mix_ref[...], w, preferred_element_type=F32)

    @pl.when(j == pl.num_programs(0) - 1)
    def _():
        o = jnp.concatenate([acc_ref[t] for t in range(acc_ref.shape[0])], axis=1)
        o_ref[...] = x_ref[...] + _rms(o, g_ref[...])


def _sample_out_proj(mix, w, layer, x, g):
    bd = mix.shape[0]
    mix = jnp.pad(mix.reshape(bd, D_MODEL), ((0, x.shape[0] - bd), (0, 0))).astype(BF16)
    return _out_proj_cast(mix, w, layer, x, g, tn=CAST_TILE)


def _out_proj_cast(mix, w, layer, x, g, *, tn):
    m = x.shape[0]
    n_tiles = D_MODEL // tn
    return pl.pallas_call(
        _out_proj_cast_kernel,
        grid=(n_tiles,),
        in_specs=[
            pl.BlockSpec((m, D_MODEL), lambda j: (0, 0)),
            pl.BlockSpec((None, D_MODEL, tn), lambda j: (layer, 0, j)),
            pl.BlockSpec((m, D_MODEL), lambda j: (0, 0)),
            pl.BlockSpec((1, D_MODEL), lambda j: (0, 0)),
        ],
        out_specs=[
            pl.BlockSpec((m, D_MODEL), lambda j: (0, 0)),
            pl.BlockSpec((D_MODEL, tn), lambda j: (0, j)),
        ],
        out_shape=[
            jax.ShapeDtypeStruct((m, D_MODEL), F32),
            jax.ShapeDtypeStruct((D_MODEL, D_MODEL), BF16),
        ],
        scratch_shapes=[pltpu.VMEM((n_tiles, m, tn), F32)],
        compiler_params=_params("arbitrary"),
        name="out_proj_cast",
    )(mix, w, x, g.reshape(1, D_MODEL))


def _ffn_kernel(x_ref, gpre_ref, gpost_ref, wg_ref, wl_ref, wd_ref, o_ref, xn_ref):
    f = pl.program_id(1)

    @pl.when(f == 0)
    def _():
        xn_ref[...] = _rms(x_ref[...], gpre_ref[...]).astype(BF16)
        o_ref[...] = jnp.zeros_like(o_ref)

    xn = xn_ref[...]
    hg = jnp.dot(xn, wg_ref[...], preferred_element_type=F32)
    hl = jnp.dot(xn, wl_ref[...], preferred_element_type=F32)
    a = (hg * jax.nn.sigmoid(hg) * hl).astype(BF16)
    o_ref[...] += jnp.dot(a, wd_ref[...], preferred_element_type=F32)

    @pl.when(f == pl.num_programs(1) - 1)
    def _():
        o_ref[...] = x_ref[...] + _rms(o_ref[...], gpost_ref[...])


def _ffn(x, g_pre, g_post, weights, *, tm, tf, first_tile):
    m = x.shape[0]
    nf = D_FF // tf
    rows = pl.BlockSpec((tm, D_MODEL), lambda i, f: (i + first_tile, 0))
    return pl.pallas_call(
        _ffn_kernel,
        grid=(m // tm - first_tile, nf),
        in_specs=[
            rows,
            pl.BlockSpec((1, D_MODEL), lambda i, f: (0, 0)),
            pl.BlockSpec((1, D_MODEL), lambda i, f: (0, 0)),
            pl.BlockSpec((D_MODEL, tf), lambda i, f: (0, f)),
            pl.BlockSpec((D_MODEL, tf), lambda i, f: (0, f)),
            pl.BlockSpec((tf, D_MODEL), lambda i, f: (f, 0)),
        ],
        out_specs=rows,
        out_shape=jax.ShapeDtypeStruct((m, D_MODEL), F32),
        input_output_aliases={0: 0},
        scratch_shapes=[pltpu.VMEM((tm, D_MODEL), BF16)],
        compiler_params=_params("parallel", "arbitrary", vmem_limit=FFN_VMEM_LIMIT),
        name="ffn",
    )(x, g_pre.reshape(1, D_MODEL), g_post.reshape(1, D_MODEL), *weights)


def _ffn_head_kernel(x_ref, xs_ref, gpre_ref, gpost_ref, wg_ref, wl_ref, wd_ref,
                     o_ref, os_ref, wgb_ref, wlb_ref, wdb_ref, xn_ref):
    tm = x_ref.shape[0]
    f = pl.program_id(0)

    @pl.when(f == 0)
    def _():
        xn_ref[:tm, :] = _rms(x_ref[...], gpre_ref[...]).astype(BF16)
        xn_ref[tm:, :] = _rms(xs_ref[...], gpre_ref[...]).astype(BF16)
        o_ref[...] = jnp.zeros_like(o_ref)
        os_ref[...] = jnp.zeros_like(os_ref)

    wg, wl, wd = (r[...].astype(BF16) for r in (wg_ref, wl_ref, wd_ref))
    wgb_ref[...] = wg
    wlb_ref[...] = wl
    wdb_ref[...] = wd
    xn = xn_ref[...]
    hg = jnp.dot(xn, wg, preferred_element_type=F32)
    hl = jnp.dot(xn, wl, preferred_element_type=F32)
    a = (hg * jax.nn.sigmoid(hg) * hl).astype(BF16)
    part = jnp.dot(a, wd, preferred_element_type=F32)
    o_ref[...] += part[:tm, :]
    os_ref[...] += part[tm:, :]

    @pl.when(f == pl.num_programs(0) - 1)
    def _():
        o_ref[...] = x_ref[...] + _rms(o_ref[...], gpost_ref[...])
        os_ref[...] = xs_ref[...] + _rms(os_ref[...], gpost_ref[...])


def _ffn_head(x, xs, g_pre, g_post, w_up, w_down, layer, *, tm, tf):
    m = x.shape[0]
    ms = xs.shape[0]
    nf = D_FF // tf
    once = dict(pipeline_mode=pl.Buffered(1))
    head = pl.BlockSpec((tm, D_MODEL), lambda f: (0, 0), **once)
    sample = pl.BlockSpec((ms, D_MODEL), lambda f: (0, 0))
    vec = pl.BlockSpec((1, D_MODEL), lambda f: (0, 0))
    outs = pl.pallas_call(
        _ffn_head_kernel,
        grid=(nf,),
        in_specs=[
            head, sample, vec, vec,
            pl.BlockSpec((None, D_MODEL, tf), lambda f: (layer, 0, f)),
            pl.BlockSpec((None, D_MODEL, tf), lambda f: (layer, 0, nf + f)),
            pl.BlockSpec((None, tf, D_MODEL), lambda f: (layer, f, 0)),
        ],
        out_specs=[
            head, sample,
            pl.BlockSpec((D_MODEL, tf), lambda f: (0, f)),
            pl.BlockSpec((D_MODEL, tf), lambda f: (0, f)),
            pl.BlockSpec((tf, D_MODEL), lambda f: (f, 0)),
        ],
        out_shape=[
            jax.ShapeDtypeStruct((m, D_MODEL), F32),
            jax.ShapeDtypeStruct((ms, D_MODEL), F32),
            jax.ShapeDtypeStruct((D_MODEL, D_FF), BF16),
            jax.ShapeDtypeStruct((D_MODEL, D_FF), BF16),
            jax.ShapeDtypeStruct((D_FF, D_MODEL), BF16),
        ],
        input_output_aliases={0: 0},
        scratch_shapes=[pltpu.VMEM((tm + ms, D_MODEL), BF16)],
        compiler_params=_params("arbitrary"),
        name="ffn_head",
    )(x, xs, g_pre.reshape(1, D_MODEL), g_post.reshape(1, D_MODEL), w_up, w_up, w_down)
    return outs[0], outs[1], tuple(outs[2:])


def _swa_kernel(q_ref, k_ref, v_ref, tb_ref, o_ref, lse_ref):
    n_units, n_res, u, _ = q_ref.shape
    per_blk = N_BACK // u
    n_blk = n_units // per_blk
    lane = lax.broadcasted_iota(jnp.int32, (N_BACK, HEAD_DIM), 1)
    ones = jnp.ones((2 * N_BACK, HEAD_DIM), BF16)

    def rows(ref, res, unit0, n_rows, lo, hi):
        return ref[pl.ds(unit0, n_rows // u), res, :, lo:hi].reshape(n_rows, hi - lo)

    def block(res, qu, ku, table):
        n_keys = N_BACK if table == 0 else 2 * N_BACK
        lse_tile = jnp.zeros((N_BACK, HEAD_DIM), F32)
        for h in range(HEADS_PER_GROUP):
            lo, hi = h * HEAD_DIM, (h + 1) * HEAD_DIM
            q = rows(q_ref, res, qu, N_BACK, lo, hi)
            kw = rows(k_ref, res, ku, n_keys, lo, hi)
            vw = rows(v_ref, res, ku, n_keys, lo, hi)
            s = lax.dot_general(q, kw, (((1,), (1,)), ((), ())), preferred_element_type=F32)
            s = s + tb_ref[table, h][:, :n_keys]
            m = jnp.max(s, axis=1, keepdims=True)
            p = jnp.exp(s - m).astype(BF16)
            ov = jnp.dot(p, jnp.concatenate([vw, ones[:n_keys]], axis=1), preferred_element_type=F32)
            den = ov[:, HEAD_DIM:]
            o = ov[:, :HEAD_DIM] / den
            o_ref[pl.ds(qu, per_blk), res, :, lo:hi] = o.reshape(per_blk, u, HEAD_DIM).astype(o_ref.dtype)
            lse_tile = jnp.where(lane == h, m + jnp.log(den), lse_tile)
        lse_ref[pl.ds(qu, per_blk), res, :, :] = lse_tile.reshape(per_blk, u, HEAD_DIM)

    def block_at(res, n):
        block(res, n * per_blk, (n - 1) * per_blk, 1)

    n_pairs = (n_blk - 1) // 2
    for res in range(n_res):
        block(res, 0, 0, 0)

        def body(i, carry, res=res):
            block_at(res, 1 + 2 * i)
            block_at(res, 2 + 2 * i)
            return carry

        if n_pairs > 0:
            lax.fori_loop(0, n_pairs, body, 0)
        if (n_blk - 1) % 2 == 1:
            block_at(res, n_blk - 1)


def _sub_block(dil):
    return N_BACK if dil == 1 else PERM_BLOCK


def _swa_group(zb, tables, g, *, batch, seq):
    dil = SWA_PATTERN[g][1]
    sub = _sub_block(dil)
    n_units, u = seq // sub, sub // dil
    n_res = 4 if n_units * u == 2 * N_BACK else 1
    view = zb.reshape(zb.shape[0], batch, n_units, dil, u, GROUP_WIDTH)

    def rows_in(tile):
        return pl.BlockSpec((None, None, n_units, n_res, u, GROUP_WIDTH), lambda i, r: (tile, i, 0, r, 0, 0))

    def rows_out(width):
        return pl.BlockSpec((None, n_units, n_res, u, width), lambda i, r: (i, 0, r, 0, 0))

    return pl.pallas_call(
        _swa_kernel,
        grid=(batch, dil // n_res),
        in_specs=[rows_in(g), rows_in(N_SWA_GROUPS + g), rows_in(2 * N_SWA_GROUPS + g),
                  pl.BlockSpec((None, 2, HEADS_PER_GROUP, N_BACK, 2 * N_BACK), lambda i, r: (g, 0, 0, 0, 0))],
        out_specs=[rows_out(GROUP_WIDTH), rows_out(HEAD_DIM)],
        out_shape=[
            jax.ShapeDtypeStruct((batch, n_units, dil, u, GROUP_WIDTH), BF16),
            jax.ShapeDtypeStruct((batch, n_units, dil, u, HEAD_DIM), F32),
        ],
        compiler_params=_params("parallel", "parallel"),
        name=f"swa_group{g}",
    )(view, view, view, tables)


def _split3(x):
    hi = x.astype(BF16)
    rest = x - hi.astype(F32)
    mid = rest.astype(BF16)
    lo = (rest - mid.astype(F32)).astype(BF16)
    return hi, mid, lo


def _merge_stage(o0_ref, o1_ref, o2_ref, l0_ref, l1_ref, l2_ref, q_ref, kb_ref, vb_ref, o_ref):
    tm = o_ref.shape[0]
    group_refs = ((o0_ref, l0_ref), (o1_ref, l1_ref), (o2_ref, l2_ref))
    outs, lses = [None] * N_SWA_GROUPS, [None] * N_SWA_GROUPS

    def token_order(g):
        o_g, l_g = group_refs[g]
        dil = SWA_PATTERN[g][1]
        o = o_g[...].reshape(tm, GROUP_WIDTH)
        l = l_g[...].reshape(tm, HEAD_DIM)
        if dil > 1:
            inv = _residue_major_perm(PERM_BLOCK, dil, transpose=True)
            l3 = _split3(l)
            o_nat, l_nat = [], []
            for s in range(0, tm, PERM_BLOCK):
                o_nat.append(jnp.dot(inv, o[s:s + PERM_BLOCK, :], preferred_element_type=F32))
                l_nat.append(sum(jnp.dot(inv, t[s:s + PERM_BLOCK, :], preferred_element_type=F32) for t in l3))
            o = jnp.concatenate(o_nat, axis=0)
            l = jnp.concatenate(l_nat, axis=0)
        outs[g] = o.astype(F32)
        lses[g] = l

    def merge_head(h):
        lo, hi = h * HEAD_DIM, (h + 1) * HEAD_DIM
        ls = [l[:, h:h + 1] for l in lses]
        mx = jnp.maximum(jnp.maximum(ls[0], ls[1]), ls[2])
        es = [jnp.exp(l - mx) for l in ls]
        tot = es[0] + es[1] + es[2]
        for g in range(N_SWA_GROUPS):
            alpha = es[g] / tot
            o_ref[:, g * GROUP_WIDTH + lo:g * GROUP_WIDTH + hi] = (outs[g][:, lo:hi] * alpha).astype(o_ref.dtype)

    return ([functools.partial(token_order, g) for g in range(N_SWA_GROUPS)]
            + [functools.partial(merge_head, h) for h in range(HEADS_PER_GROUP)]
            + _mem_attention_parts(q_ref, kb_ref, vb_ref, o_ref, MIXER_WIDTH))


def _swa_merge_out(outs, lses, zb, mem_kv, layer, w, x, g, *, rows_per_batch):
    tm = ROW_TILE
    tiles_per_batch = rows_per_batch // tm

    def group_tile(width, dil):
        sub = _sub_block(dil)
        return lambda tile: pl.BlockSpec(
            (None, tm // sub, dil, sub // dil, width),
            lambda s: (tile(s) // tiles_per_batch, tile(s) % tiles_per_batch, 0, 0, 0))

    specs = ([group_tile(GROUP_WIDTH, dil) for _, dil in SWA_PATTERN]
             + [group_tile(HEAD_DIM, dil) for _, dil in SWA_PATTERN]
             + [lambda tile: pl.BlockSpec((None, tm, MEM_WIDTH), lambda s: (3 * N_SWA_GROUPS, tile(s), 0))])
    return _mix_out(_merge_stage, (*outs, *lses, zb), specs, [], mem_kv, layer, w, x, g,
                    tm=tm, rows_per_batch=rows_per_batch, name="swa_merge_out")


def _sample_mem_attention(q_row, kv_ref, o_ref, col0):
    for h in range(N_MEM_HEADS):
        lo, hi = h * HEAD_DIM, (h + 1) * HEAD_DIM
        q = q_row[:, lo:hi]
        k = kv_ref[:, 0, h, :]
        v = kv_ref[:, 1, h, :]
        s = jnp.sum(k * q, axis=1, keepdims=True) * ATTN_SCALE
        m = jnp.max(s, axis=0, keepdims=True)
        p = jnp.exp(s - m)
        den = jnp.sum(p, axis=0, keepdims=True)
        o_ref[:, col0 + lo:col0 + hi] = jnp.sum(p * v, axis=0, keepdims=True) / den


def _sample_mix_a_kernel(z_ref, gv_ref, w0_ref, b0_ref, kv_ref, o_ref, vrow_ref):
    u = z_ref[:, 0:MIXER_WIDTH]
    v = _rms(z_ref[:, MIXER_WIDTH:2 * MIXER_WIDTH], gv_ref[...])
    vrow_ref[...] = v
    o_ref[:, 0:MIXER_WIDTH] = u * (w0_ref[...] * v + b0_ref[...])
    _sample_mem_attention(z_ref[:, 2 * MIXER_WIDTH:2 * MIXER_WIDTH + MEM_WIDTH], kv_ref, o_ref, MIXER_WIDTH)


def _sample_mix_a(z, g_v, w_s, b_s, mem_kv, layer):
    bd = mem_kv.shape[1]
    w0 = jnp.repeat(w_s[:, 0, 0], GROUP_DIM_A).reshape(1, MIXER_WIDTH)
    b0 = jnp.repeat(b_s[:, 0], GROUP_DIM_A).reshape(1, MIXER_WIDTH)
    width = z.shape[1]
    vec = lambda i: (0, 0)
    return pl.pallas_call(
        _sample_mix_a_kernel,
        grid=(bd,),
        in_specs=[
            pl.BlockSpec((None, 1, width), lambda i: (i, 0, 0)),
            pl.BlockSpec((1, MIXER_WIDTH), vec),
            pl.BlockSpec((1, MIXER_WIDTH), vec),
            pl.BlockSpec((1, MIXER_WIDTH), vec),
            _mem_kv_spec(layer, lambda i: i),
        ],
        out_specs=[
            pl.BlockSpec((None, 1, D_MODEL), lambda i: (i, 0, 0)),
            pl.BlockSpec((None, 1, MIXER_WIDTH), lambda i: (i, 0, 0)),
        ],
        out_shape=[
            jax.ShapeDtypeStruct((bd, 1, D_MODEL), F32),
            jax.ShapeDtypeStruct((bd, 1, MIXER_WIDTH), F32),
        ],
        compiler_params=_params("parallel"),
        name="sample_mix_a",
    )(z[:bd].reshape(bd, 1, width), g_v.reshape(1, MIXER_WIDTH), w0, b0, mem_kv)


def _sample_mix_b_kernel(z_ref, c0_ref, c1_ref, c2_ref, bcol_ref, bnew_ref, kv_ref, o_ref):
    caches = (c0_ref, c1_ref, c2_ref)
    outs = [[None] * HEADS_PER_GROUP for _ in range(N_SWA_GROUPS)]
    lses = [[None] * HEADS_PER_GROUP for _ in range(N_SWA_GROUPS)]
    for g in range(N_SWA_GROUPS):
        for h in range(HEADS_PER_GROUP):
            hd = g * HEADS_PER_GROUP + h
            lo, hi = h * HEAD_DIM, (h + 1) * HEAD_DIM
            q = z_ref[:, hd * HEAD_DIM:(hd + 1) * HEAD_DIM]
            k_new = z_ref[:, MIXER_WIDTH + hd * HEAD_DIM:MIXER_WIDTH + (hd + 1) * HEAD_DIM]
            v_new = z_ref[:, 2 * MIXER_WIDTH + hd * HEAD_DIM:2 * MIXER_WIDTH + (hd + 1) * HEAD_DIM]
            kc = caches[g][:, 0, h, :]
            vc = caches[g][:, 1, h, :]
            s_c = jnp.sum(kc * q, axis=1, keepdims=True) * ATTN_SCALE + bcol_ref[g][:, h:h + 1]
            s_n = jnp.sum(k_new * q, axis=1, keepdims=True) * ATTN_SCALE + bnew_ref[:, hd:hd + 1]
            m = jnp.maximum(jnp.max(s_c, axis=0, keepdims=True), s_n)
            p_c = jnp.exp(s_c - m)
            p_n = jnp.exp(s_n - m)
            den = jnp.sum(p_c, axis=0, keepdims=True) + p_n
            outs[g][h] = (jnp.sum(p_c * vc, axis=0, keepdims=True) + p_n * v_new) / den
            lses[g][h] = m + jnp.log(den)
    for h in range(HEADS_PER_GROUP):
        ls = [lses[g][h] for g in range(N_SWA_GROUPS)]
        mx = jnp.maximum(jnp.maximum(ls[0], ls[1]), ls[2])
        es = [jnp.exp(l - mx) for l in ls]
        tot = es[0] + es[1] + es[2]
        for g in range(N_SWA_GROUPS):
            c0 = g * GROUP_WIDTH + h * HEAD_DIM
            o_ref[:, c0:c0 + HEAD_DIM] = outs[g][h] * (es[g] / tot)
    _sample_mem_attention(z_ref[:, 3 * MIXER_WIDTH:3 * MIXER_WIDTH + MEM_WIDTH], kv_ref, o_ref, MIXER_WIDTH)


def _sample_mix_b(z, win_caches, swa_layer, bias_groups, mem_kv, layer):
    bd = mem_kv.shape[1]
    width = z.shape[1]
    cache_views, cache_specs = [], []
    for g, (win, dil) in enumerate(SWA_PATTERN):
        c = win_caches[g]
        cache_views.append(c.reshape(c.shape[0], bd, win // dil, dil, 2, HEADS_PER_GROUP, HEAD_DIM))
        cache_specs.append(pl.BlockSpec((None, None, N_BACK, None, 2, HEADS_PER_GROUP, HEAD_DIM),
                                        lambda i: (swa_layer, i, 0, 0, 0, 0, 0)))
    bcol = jnp.stack([bg[:, N_BACK:0:-1].T for bg in bias_groups], axis=0)
    bnew = jnp.concatenate([bg[:, 0] for bg in bias_groups])
    bnew = jnp.pad(bnew, (0, HEAD_DIM - bnew.shape[0])).reshape(1, HEAD_DIM)
    return pl.pallas_call(
        _sample_mix_b_kernel,
        grid=(bd,),
        in_specs=[pl.BlockSpec((None, 1, width), lambda i: (i, 0, 0))] + cache_specs + [
            pl.BlockSpec((N_SWA_GROUPS, N_BACK, HEADS_PER_GROUP), lambda i: (0, 0, 0)),
            pl.BlockSpec((1, HEAD_DIM), lambda i: (0, 0)),
            _mem_kv_spec(layer, lambda i: i),
        ],
        out_specs=pl.BlockSpec((None, 1, D_MODEL), lambda i: (i, 0, 0)),
        out_shape=jax.ShapeDtypeStruct((bd, 1, D_MODEL), F32),
        compiler_params=_params("parallel"),
        name="sample_mix_b",
    )(z[:bd].reshape(bd, 1, width), *cache_views, bcol, bnew, mem_kv)


def _t5_bucket(dist):
    nf = jnp.maximum(dist, MAX_EXACT).astype(F32)
    large = MAX_EXACT + (jnp.log(nf / MAX_EXACT) / math.log(MAX_DISTANCE / MAX_EXACT)
                         * (N_BUCKETS - MAX_EXACT)).astype(jnp.int32)
    large = jnp.minimum(large, N_BUCKETS - 1)
    return jnp.where(dist < MAX_EXACT, dist, large)


def _group_bias(rel_bias, g, dil):
    dist = jnp.arange(N_BACK + 1, dtype=jnp.int32) * dil
    b = rel_bias[_t5_bucket(dist)][:, g * HEADS_PER_GROUP:(g + 1) * HEADS_PER_GROUP]
    return b.T.astype(F32)


def _band_tables_kernel(b_ref, o_ref):
    n, rows, width = o_ref.shape
    for x in range(n):
        row = jnp.broadcast_to(b_ref[x:x + 1, :], (rows, width))
        o_ref[x] = pltpu.roll(row, 0, 1, stride=1, stride_axis=0)


def _band_tables(bias_groups):
    width = 2 * N_BACK
    rows = []
    for bias_j in bias_groups:
        masked = jnp.full((HEADS_PER_GROUP, N_BACK - 1), NEG_INF, F32)
        rows.append(jnp.concatenate([bias_j[:, :1], masked, bias_j[:, N_BACK:0:-1]], axis=1))
        rows.append(jnp.concatenate([bias_j[:, ::-1], masked], axis=1))
    base = jnp.stack(rows, axis=0).reshape(-1, width)
    tabs = pl.pallas_call(
        _band_tables_kernel,
        out_shape=jax.ShapeDtypeStruct((base.shape[0], N_BACK, width), F32),
        name="band_tables",
    )(base)
    return tabs.reshape(N_SWA_GROUPS, 2, HEADS_PER_GROUP, N_BACK, width)


def _kv_tail_kernel(k_ref, v_ref, o_ref, *, dil):
    rows = k_ref.shape[0]
    for kv, ref in enumerate((k_ref, v_ref)):
        x = ref[...]
        if dil > 1:
            inv = _residue_major_perm(PERM_BLOCK, dil, transpose=True)
            x = jnp.concatenate([jnp.dot(inv, x[s:s + PERM_BLOCK, :], preferred_element_type=F32)
                                 for s in range(0, rows, PERM_BLOCK)], axis=0)
        x = x.astype(F32)
        for h in range(HEADS_PER_GROUP):
            o_ref[:, kv, h, :] = x[:, h * HEAD_DIM:(h + 1) * HEAD_DIM]


def _kv_tail(zb, g, *, batch, seq):
    win, dil = SWA_PATTERN[g]
    rows = min(win, PERM_BLOCK)
    first = (seq - win) // rows
    per_batch = seq // rows

    def tile(t):
        return pl.BlockSpec((None, rows, GROUP_WIDTH), lambda b, s: (t, b * per_batch + first + s, 0))

    return pl.pallas_call(
        functools.partial(_kv_tail_kernel, dil=dil),
        grid=(batch, win // rows),
        in_specs=[tile(N_SWA_GROUPS + g), tile(2 * N_SWA_GROUPS + g)],
        out_specs=pl.BlockSpec((None, rows, 2, HEADS_PER_GROUP, HEAD_DIM), lambda b, s: (b, s, 0, 0, 0)),
        out_shape=jax.ShapeDtypeStruct((batch, win, 2, HEADS_PER_GROUP, HEAD_DIM), F32),
        compiler_params=_params("parallel", "parallel"),
        name=f"kv_tail{g}",
    )(zb, zb)


def kernel(x_prompt, x_sample, mem_prompt, cache_mem_kv, cache_win128_kv, cache_win512_kv, cache_win2048_kv, rel_bias, norm_mix_pre, norm_mix_post, norm_ffn_pre, norm_ffn_post, norm_mem, w_mem_kv, w_in_a, norm_v_a, w_spatial_a, b_spatial_a, w_in_b, w_out, w_ffn_up, w_ffn_down):
    batch, seq, _ = x_prompt.shape
    bd = x_sample.shape[0]
    depth = w_out.shape[0]
    m_p = batch * seq
    win_caches = (cache_win128_kv, cache_win512_kv, cache_win2048_kv)

    bias_groups = [_group_bias(rel_bias, g, dil) for g, (_, dil) in enumerate(SWA_PATTERN)]
    band_tables = _band_tables(bias_groups)

    yp = x_prompt.reshape(m_p, D_MODEL)
    ys = jnp.pad(x_sample.reshape(bd, D_MODEL), ((0, SAMPLE_PAD - bd), (0, 0)))
    mem_rows = mem_prompt.reshape(batch * N_MEM, D_MODEL)

    mem_kv_p = _mem_kv(mem_rows, norm_mem, w_mem_kv, batch=batch)
    chunk_v_s = []
    win_p = [[] for _ in SWA_PATTERN]
    win_s = [[] for _ in SWA_PATTERN]
    for i in range(depth):
        li = i // 2
        if i % 2 == 0:
            zs, w_in = _norm_matmul(ys, norm_mix_pre[i], w_in_a, li, tm=SAMPLE_PAD, tn=CAST_TILE, emit_w=True,
                                    gelu_cols=2 * MIXER_WIDTH, out_dtype=F32, name="in_proj_a_s")
            zp = _norm_matmul(yp, norm_mix_pre[i], w_in, None, tm=ROW_TILE, tn=w_in.shape[1], scale_rest=True,
                              gelu_cols=2 * MIXER_WIDTH, out_dtype=BF16, name="in_proj_a")
            mix_s, v_rows = _sample_mix_a(zs, norm_v_a[li], w_spatial_a[li], b_spatial_a[li], cache_mem_kv, i)
            chunk_v_s.append(v_rows)
            ys, w_o = _sample_out_proj(mix_s, w_out, i, ys, norm_mix_post[i])
            yp = _gmlp_mix_out(zp, norm_v_a[li], w_spatial_a[li], b_spatial_a[li], mem_kv_p, i,
                               w_o, yp, norm_mix_post[i], tm=ROW_TILE, rows_per_batch=seq)
        else:
            zs, w_in = _norm_matmul(ys, norm_mix_pre[i], w_in_b, li, tm=SAMPLE_PAD, tn=CAST_TILE, emit_w=True,
                                    gelu_cols=0, out_dtype=F32, name="in_proj_b_s")
            zb = _in_proj_b(yp, norm_mix_pre[i], w_in, tm=ROW_TILE)
            outs, lses = [], []
            for g, (win, dil) in enumerate(SWA_PATTERN):
                o, lse = _swa_group(zb, band_tables, g, batch=batch, seq=seq)
                outs.append(o)
                lses.append(lse)
                win_p[g].append(_kv_tail(zb, g, batch=batch, seq=seq))
                kv_new = zs[:bd, MIXER_WIDTH:3 * MIXER_WIDTH]
                kv_new = kv_new.reshape(bd, 1, 2, N_SWA_GROUPS, HEADS_PER_GROUP, HEAD_DIM)[:, :, :, g]
                win_s[g].append(kv_new)
            mix_s = _sample_mix_b(zs, win_caches, li, bias_groups, cache_mem_kv, i)
            ys, w_o = _sample_out_proj(mix_s, w_out, i, ys, norm_mix_post[i])
            yp = _swa_merge_out(outs, lses, zb, mem_kv_p, i, w_o, yp, norm_mix_post[i], rows_per_batch=seq)
        yp, ys, w_ffn = _ffn_head(yp, ys, norm_ffn_pre[i], norm_ffn_post[i], w_ffn_up, w_ffn_down, i,
                                  tm=FFN_ROW_TILE, tf=HEAD_FF_TILE)
        yp = _ffn(yp, norm_ffn_pre[i], norm_ffn_post[i], w_ffn, tm=FFN_ROW_TILE, tf=FF_TILE, first_tile=1)

    return (
        yp.reshape(batch, seq, D_MODEL),
        ys[:bd].reshape(bd, 1, D_MODEL),
        mem_kv_p,
        jnp.stack(chunk_v_s, axis=0),
        jnp.stack(win_p[0], axis=0),
        jnp.stack(win_p[1], axis=0),
        jnp.stack(win_p[2], axis=0),
        jnp.stack(win_s[0], axis=0),
        jnp.stack(win_s[1], axis=0),
        jnp.stack(win_s[2], axis=0),
    )
```

```python
import functools
import math

import jax
import jax.numpy as jnp
from jax import lax
from jax.experimental import pallas as pl
from jax.experimental.pallas import tpu as pltpu

F32 = jnp.float32
BF16 = jnp.bfloat16

D_MODEL = 2048
HEAD_DIM = 128
N_MEM = 256
N_MEM_HEADS = 4
MEM_WIDTH = N_MEM_HEADS * HEAD_DIM
MIXER_WIDTH = D_MODEL - MEM_WIDTH
CHUNK = 128
N_GROUPS_A = 4
GROUP_DIM_A = MIXER_WIDTH // N_GROUPS_A
SWA_PATTERN = ((128, 1), (512, 4), (2048, 16))
N_SWA_GROUPS = len(SWA_PATTERN)
HEADS_PER_GROUP = 4
GROUP_WIDTH = HEADS_PER_GROUP * HEAD_DIM
N_BACK = 128
N_BUCKETS = 32
MAX_EXACT = N_BUCKETS // 2
MAX_DISTANCE = 2048
D_FF = 5632
EPS = 1e-6
NEG_INF = -1e30
ATTN_SCALE = HEAD_DIM ** -0.5
SAMPLE_PAD = 16
PERM_BLOCK = 256
OUT_CHUNK = 256

ROW_TILE = 512
FFN_ROW_TILE = 1024
FF_TILE = 512
HEAD_FF_TILE = 256
CAST_TILE = 512

V7X_VMEM_BYTES = 64 * 1024 * 1024
VMEM_LIMIT = V7X_VMEM_BYTES - 8 * 1024 * 1024
FFN_VMEM_LIMIT = V7X_VMEM_BYTES - 2 * 1024 * 1024


def _params(*sem, vmem_limit=VMEM_LIMIT):
    return pltpu.CompilerParams(dimension_semantics=sem, vmem_limit_bytes=vmem_limit)


def _gelu(x):
    return 0.5 * x * (1.0 + jnp.tanh(0.7978845608028654 * (x + 0.044715 * (x * x * x))))


def _rms(x, g):
    return x * lax.rsqrt(jnp.mean(x * x, axis=-1, keepdims=True) + EPS) * g


def _log2(n):
    assert n & (n - 1) == 0
    return n.bit_length() - 1


def _residue_major_perm(tm, dil, transpose=False):
    n = tm // dil
    row = lax.broadcasted_iota(jnp.int32, (tm, tm), 0)
    col = lax.broadcasted_iota(jnp.int32, (tm, tm), 1)
    dst, src = (col, row) if transpose else (row, col)
    want = lax.shift_left(jnp.bitwise_and(dst, n - 1), _log2(dil)) + lax.shift_right_logical(dst, _log2(n))
    return (src == want).astype(BF16)


def _norm_matmul_kernel(x_ref, g_ref, w_ref, o_ref, *rest, gelu_cols, scale_rest, emit_w):
    xn_ref = rest[-1]
    n = pl.program_id(1)

    @pl.when(n == 0)
    def _():
        xn_ref[...] = _rms(x_ref[...], g_ref[...]).astype(BF16)

    w = w_ref[...].astype(BF16)
    if emit_w:
        rest[0][...] = w
    acc = jnp.dot(xn_ref[...], w, preferred_element_type=F32)
    if gelu_cols > 0:
        col = lax.broadcasted_iota(jnp.int32, acc.shape, 1)
        other = acc * ATTN_SCALE if scale_rest else acc
        acc = jnp.where(col < gelu_cols - n * acc.shape[1], _gelu(acc), other)
    o_ref[...] = acc.astype(o_ref.dtype)


def _weight_spec(w, layer, block, index):
    if layer is None:
        mode = dict(pipeline_mode=pl.Buffered(1)) if tuple(block) == tuple(w.shape) else {}
        return pl.BlockSpec(block, index, **mode)
    return pl.BlockSpec((None,) + block, lambda *ids: (layer,) + index(*ids))


def _norm_matmul(x, g, w, layer, *, tm, tn, gelu_cols, out_dtype, name, emit_w=False, scale_rest=False):
    m, k = x.shape
    n = w.shape[-1]
    out_specs = [pl.BlockSpec((tm, tn), lambda i, j: (i, j))]
    out_shape = [jax.ShapeDtypeStruct((m, n), out_dtype)]
    if emit_w:
        assert m == tm
        out_specs.append(pl.BlockSpec((k, tn), lambda i, j: (0, j)))
        out_shape.append(jax.ShapeDtypeStruct((k, n), BF16))
    outs = pl.pallas_call(
        functools.partial(_norm_matmul_kernel, gelu_cols=gelu_cols, scale_rest=scale_rest, emit_w=emit_w),
        grid=(m // tm, n // tn),
        in_specs=[
            pl.BlockSpec((tm, k), lambda i, j: (i, 0)),
            pl.BlockSpec((1, k), lambda i, j: (0, 0)),
            _weight_spec(w, layer, (k, tn), lambda i, j: (0, j)),
        ],
        out_specs=out_specs,
        out_shape=out_shape,
        scratch_shapes=[pltpu.VMEM((tm, k), BF16)],
        compiler_params=_params("parallel", "arbitrary"),
        name=name,
    )(x, g.reshape(1, k), w)
    return outs if emit_w else outs[0]


def _in_proj_a_kernel(x_ref, xs_ref, g_ref, w_ref, zp_ref, zs_ref, wb_ref, xn_ref, *, n_col, gelu_cols):
    tm = x_ref.shape[0]
    tn = w_ref.shape[1]
    s = pl.program_id(0)

    def prompt_cols(acc, c0):
        col = lax.broadcasted_iota(jnp.int32, acc.shape, 1) + c0
        return jnp.where(col < gelu_cols, _gelu(acc), acc * ATTN_SCALE).astype(zp_ref.dtype)

    @pl.when(s == 0)
    def _():
        xn_ref[:tm, :] = _rms(x_ref[...], g_ref[...]).astype(BF16)
        xn_ref[tm:, :] = _rms(xs_ref[...], g_ref[...]).astype(BF16)

    for j in range(n_col):
        @pl.when(s == j)
        def _(j=j):
            cols = slice(j * tn, (j + 1) * tn)
            w = w_ref[...].astype(BF16)
            wb_ref[:, cols] = w
            acc = jnp.dot(xn_ref[...], w, preferred_element_type=F32)
            zp_ref[:, cols] = prompt_cols(acc[:tm, :], j * tn)
            sample = acc[tm:, :]
            zs_ref[:, cols] = _gelu(sample) if (j + 1) * tn <= gelu_cols else sample

    @pl.when(s >= n_col)
    def _():
        xn = _rms(x_ref[...], g_ref[...]).astype(BF16)
        acc = jnp.dot(xn, wb_ref[...], preferred_element_type=F32)
        zp_ref[...] = prompt_cols(acc, 0)


def _in_proj_a(x, xs, g, w, layer, *, tm, tn, gelu_cols):
    m, k = x.shape
    ms = xs.shape[0]
    n = w.shape[2]
    n_col, n_row = n // tn, m // tm
    assert gelu_cols % tn == 0
    row = lambda s: jnp.maximum(s - (n_col - 1), 0)
    return pl.pallas_call(
        functools.partial(_in_proj_a_kernel, n_col=n_col, gelu_cols=gelu_cols),
        grid=(n_col + n_row - 1,),
        in_specs=[
            pl.BlockSpec((tm, k), lambda s: (row(s), 0)),
            pl.BlockSpec((ms, k), lambda s: (0, 0)),
            pl.BlockSpec((1, k), lambda s: (0, 0)),
            pl.BlockSpec((None, k, tn), lambda s: (layer, 0, jnp.minimum(s, n_col - 1))),
        ],
        out_specs=[
            pl.BlockSpec((tm, n), lambda s: (row(s), 0)),
            pl.BlockSpec((ms, n), lambda s: (0, 0)),
        ],
        out_shape=[
            jax.ShapeDtypeStruct((m, n), BF16),
            jax.ShapeDtypeStruct((ms, n), F32),
        ],
        scratch_shapes=[pltpu.VMEM((k, n), BF16), pltpu.VMEM((tm + ms, k), BF16)],
        compiler_params=_params("arbitrary"),
        name="in_proj_a",
    )(x, xs, g.reshape(1, k), w)


def _mem_kv_kernel(x_ref, g_ref, w_ref, o_ref):
    xn = _rms(x_ref[...], g_ref[...]).astype(BF16)
    acc = jnp.dot(xn, w_ref[...].astype(BF16), preferred_element_type=F32)
    for kv in range(2):
        for h in range(N_MEM_HEADS):
            c0 = (kv * N_MEM_HEADS + h) * HEAD_DIM
            o_ref[:, kv, h, :] = acc[:, c0:c0 + HEAD_DIM]


def _mem_kv(mem_rows, g, w, *, batch):
    m, k = mem_rows.shape
    layers = w.shape[0]
    out = pl.pallas_call(
        _mem_kv_kernel,
        grid=(layers,),
        in_specs=[
            pl.BlockSpec((m, k), lambda l: (0, 0)),
            pl.BlockSpec((None, 1, k), lambda l: (l, 0, 0)),
            pl.BlockSpec((None, k, 2 * MEM_WIDTH), lambda l: (l, 0, 0)),
        ],
        out_specs=pl.BlockSpec((None, m, 2, N_MEM_HEADS, HEAD_DIM), lambda l: (l, 0, 0, 0, 0)),
        out_shape=jax.ShapeDtypeStruct((layers, m, 2, N_MEM_HEADS, HEAD_DIM), F32),
        compiler_params=_params("parallel"),
        name="mem_kv",
    )(mem_rows, g.reshape(layers, 1, k), w)
    return out.reshape(layers, batch, m // batch, 2, N_MEM_HEADS, HEAD_DIM)


def _in_proj_b_kernel(x_ref, g_ref, w_ref, o_ref, xn_ref):
    tm = x_ref.shape[0]
    n_qkv = 3 * N_SWA_GROUPS
    xn = _rms(x_ref[...], g_ref[...]).astype(BF16)
    xn_ref[0] = xn
    for g in range(1, N_SWA_GROUPS):
        perm = _residue_major_perm(PERM_BLOCK, SWA_PATTERN[g][1])
        for s in range(0, tm, PERM_BLOCK):
            xn_ref[g, s:s + PERM_BLOCK, :] = jnp.dot(
                perm, xn[s:s + PERM_BLOCK, :], preferred_element_type=F32).astype(BF16)

    for t in range(o_ref.shape[0]):
        src = t % N_SWA_GROUPS if t < n_qkv else 0
        w = w_ref[:, t * GROUP_WIDTH:(t + 1) * GROUP_WIDTH]
        acc = jnp.dot(xn_ref[src], w, preferred_element_type=F32)
        if t < N_SWA_GROUPS or t == n_qkv:
            acc = acc * ATTN_SCALE
        o_ref[t] = acc.astype(o_ref.dtype)


def _in_proj_b(x, g, w, *, tm):
    m, k = x.shape
    n_tiles = w.shape[1] // GROUP_WIDTH
    return pl.pallas_call(
        _in_proj_b_kernel,
        grid=(m // tm,),
        in_specs=[
            pl.BlockSpec((tm, k), lambda i: (i, 0)),
            pl.BlockSpec((1, k), lambda i: (0, 0)),
            _weight_spec(w, None, w.shape, lambda i: (0, 0)),
        ],
        out_specs=pl.BlockSpec((n_tiles, tm, GROUP_WIDTH), lambda i: (0, i, 0)),
        out_shape=jax.ShapeDtypeStruct((n_tiles, m, GROUP_WIDTH), BF16),
        scratch_shapes=[pltpu.VMEM((N_SWA_GROUPS, tm, k), BF16)],
        compiler_params=_params("parallel"),
        name="in_proj_b",
    )(x, g.reshape(1, k), w)


MEM_KV_SCRATCH = [pltpu.VMEM((N_MEM_HEADS, N_MEM, HEAD_DIM), BF16),
                  pltpu.VMEM((N_MEM_HEADS, N_MEM, 2 * HEAD_DIM), BF16)]


def _prepare_mem_kv(kv_ref, kb_ref, vb_ref):
    for h in range(N_MEM_HEADS):
        kb_ref[h] = kv_ref[:, 0, h, :].astype(BF16)
        vb_ref[h, :, :HEAD_DIM] = kv_ref[:, 1, h, :].astype(BF16)
        vb_ref[h, :, HEAD_DIM:] = jnp.ones((N_MEM, HEAD_DIM), BF16)


def _mem_attention_parts(q_ref, kb_ref, vb_ref, o_ref, col0):
    def head(h):
        lo, hi = h * HEAD_DIM, (h + 1) * HEAD_DIM
        s = lax.dot_general(q_ref[:, lo:hi], kb_ref[h], (((1,), (1,)), ((), ())), preferred_element_type=F32)
        m = jnp.max(s, axis=1, keepdims=True)
        p = jnp.exp(s - m).astype(BF16)
        ov = jnp.dot(p, vb_ref[h], preferred_element_type=F32)
        o_ref[:, col0 + lo:col0 + hi] = (ov[:, :HEAD_DIM] / ov[:, HEAD_DIM:]).astype(o_ref.dtype)

    return [functools.partial(head, h) for h in range(N_MEM_HEADS)]


def _gmlp_stage(u_ref, v_ref, q_ref, gv_ref, ws_ref, bs_ref, kb_ref, vb_ref, vn_ref, o_ref):
    tm = u_ref.shape[0]

    def norm_v():
        vn_ref[...] = _rms(v_ref[...].astype(F32), gv_ref[...]).astype(BF16)

    def group(g):
        row = lax.broadcasted_iota(jnp.int32, (CHUNK, CHUNK), 0)
        col = lax.broadcasted_iota(jnp.int32, (CHUNK, CHUNK), 1)
        w = jnp.where(row >= col, ws_ref[g], 0.0).astype(BF16)
        b = bs_ref[:, g:g + 1]
        c0, c1 = g * GROUP_DIM_A, (g + 1) * GROUP_DIM_A
        for c in range(tm // CHUNK):
            r0, r1 = c * CHUNK, (c + 1) * CHUNK
            s = jnp.dot(w, vn_ref[r0:r1, c0:c1], preferred_element_type=F32) + b
            o_ref[r0:r1, c0:c1] = (u_ref[r0:r1, c0:c1].astype(F32) * s).astype(o_ref.dtype)

    return ([norm_v] + [functools.partial(group, g) for g in range(N_GROUPS_A)]
            + _mem_attention_parts(q_ref, kb_ref, vb_ref, o_ref, MIXER_WIDTH))


def _mem_kv_spec(layer, batch_of):
    return pl.BlockSpec((None, None, N_MEM, 2, N_MEM_HEADS, HEAD_DIM),
                        lambda i: (layer, batch_of(i), 0, 0, 0, 0))


def _mix_out_kernel(*refs, stage, n_in, n_tiles, tiles_per_batch):
    mix_in = refs[:n_in]
    kv_ref, w_ref, x_ref, g_ref, o_ref, buf0_ref, buf1_ref, acc_ref, kb_ref, vb_ref = refs[n_in:n_in + 10]
    extra = refs[n_in + 10:]
    s = pl.program_id(0)

    @pl.when(s == 0)
    def _():
        buf1_ref[...] = jnp.zeros_like(buf1_ref)

    @pl.when(jnp.minimum(s, n_tiles - 1) % tiles_per_batch == 0)
    def _():
        _prepare_mem_kv(kv_ref, kb_ref, vb_ref)

    def step(dst_ref, src_ref):
        parts = stage(*mix_in, kb_ref, vb_ref, *extra, dst_ref)
        n_chunks = D_MODEL // OUT_CHUNK
        for c in range(n_chunks):
            cols = slice(c * OUT_CHUNK, (c + 1) * OUT_CHUNK)
            acc_ref[:, cols] = jnp.dot(src_ref[...], w_ref[:, cols], preferred_element_type=F32)
            for part in parts[c * len(parts) // n_chunks:(c + 1) * len(parts) // n_chunks]:
                part()
        o_ref[...] = x_ref[...] + _rms(acc_ref[...], g_ref[...])

    @pl.when(s % 2 == 0)
    def _():
        step(buf0_ref, buf1_ref)

    @pl.when(s % 2 == 1)
    def _():
        step(buf1_ref, buf0_ref)


def _mix_out(stage, mix_inputs, mix_specs, extra_scratch, mem_kv, layer, w, x, g, *, tm, rows_per_batch, name):
    m = x.shape[0]
    n_tiles = m // tm
    tiles_per_batch = rows_per_batch // tm
    mix_tile = lambda s: jnp.minimum(s, n_tiles - 1)
    out_tile = lambda s: jnp.maximum(s - 1, 0)
    row_spec = pl.BlockSpec((tm, D_MODEL), lambda s: (out_tile(s), 0))
    return pl.pallas_call(
        functools.partial(_mix_out_kernel, stage=stage, n_in=len(mix_inputs), n_tiles=n_tiles,
                          tiles_per_batch=tiles_per_batch),
        grid=(n_tiles + 1,),
        in_specs=[spec(mix_tile) for spec in mix_specs] + [
            _mem_kv_spec(layer, lambda s: mix_tile(s) // tiles_per_batch),
            pl.BlockSpec((D_MODEL, D_MODEL), lambda s: (0, 0), pipeline_mode=pl.Buffered(1)),
            row_spec,
            pl.BlockSpec((1, D_MODEL), lambda s: (0, 0)),
        ],
        out_specs=row_spec,
        out_shape=jax.ShapeDtypeStruct((m, D_MODEL), F32),
        scratch_shapes=[pltpu.VMEM((tm, D_MODEL), BF16), pltpu.VMEM((tm, D_MODEL), BF16),
                        pltpu.VMEM((tm, D_MODEL), F32)] + MEM_KV_SCRATCH + list(extra_scratch),
        compiler_params=_params("arbitrary"),
        name=name,
    )(*mix_inputs, mem_kv, w, x, g.reshape(1, D_MODEL))


def _gmlp_mix_out(zact, g_v, w_s, b_s, mem_kv, layer, w, x, g, *, tm, rows_per_batch):
    const = lambda shape: (lambda tile: pl.BlockSpec(shape, lambda s: (0,) * len(shape)))
    specs = [
        lambda tile: pl.BlockSpec((tm, MIXER_WIDTH), lambda s: (tile(s), 0)),
        lambda tile: pl.BlockSpec((tm, MIXER_WIDTH), lambda s: (tile(s), 1)),
        lambda tile: pl.BlockSpec((tm, MEM_WIDTH), lambda s: (tile(s), 2 * MIXER_WIDTH // MEM_WIDTH)),
        const((1, MIXER_WIDTH)),
        const((N_GROUPS_A, CHUNK, CHUNK)),
        const((CHUNK, N_GROUPS_A)),
    ]
    return _mix_out(_gmlp_stage, (zact, zact, zact, g_v.reshape(1, MIXER_WIDTH), w_s, b_s.T), specs,
                    [pltpu.VMEM((tm, MIXER_WIDTH), BF16)], mem_kv, layer, w, x, g,
                    tm=tm, rows_per_batch=rows_per_batch, name="gmlp_mix_out")


def _out_proj_cast_kernel(mix_ref, w_ref, x_ref, g_ref, o_ref, wb_ref, acc_ref):
    j = pl.program_id(0)
    w = w_ref[...].astype(BF16)
    wb_ref[...] = w
    acc_ref[j] = jnp.dot(mix_ref[...], w, preferred_element_type=F32)

    @pl.when(j == pl.num_programs(0) - 1)
    def _():
        o = jnp.concatenate([acc_ref[t] for t in range(acc_ref.shape[0])], axis=1)
        o_ref[...] = x_ref[...] + _rms(o, g_ref[...])


def _sample_out_proj(mix, w, layer, x, g):
    bd = mix.shape[0]
    mix = jnp.pad(mix.reshape(bd, D_MODEL), ((0, x.shape[0] - bd), (0, 0))).astype(BF16)
    return _out_proj_cast(mix, w, layer, x, g, tn=CAST_TILE)


def _out_proj_cast(mix, w, layer, x, g, *, tn):
    m = x.shape[0]
    n_tiles = D_MODEL // tn
    return pl.pallas_call(
        _out_proj_cast_kernel,
        grid=(n_tiles,),
        in_specs=[
            pl.BlockSpec((m, D_MODEL), lambda j: (0, 0)),
            pl.BlockSpec((None, D_MODEL, tn), lambda j: (layer, 0, j)),
            pl.BlockSpec((m, D_MODEL), lambda j: (0, 0)),
            pl.BlockSpec((1, D_MODEL), lambda j: (0, 0)),
        ],
        out_specs=[
            pl.BlockSpec((m, D_MODEL), lambda j: (0, 0)),
            pl.BlockSpec((D_MODEL, tn), lambda j: (0, j)),
        ],
        out_shape=[
            jax.ShapeDtypeStruct((m, D_MODEL), F32),
            jax.ShapeDtypeStruct((D_MODEL, D_MODEL), BF16),
        ],
        scratch_shapes=[pltpu.VMEM((n_tiles, m, tn), F32)],
        compiler_params=_params("arbitrary"),
        name="out_proj_cast",
    )(mix, w, x, g.reshape(1, D_MODEL))


def _ffn_kernel(x_ref, gpre_ref, gpost_ref, wg_ref, wl_ref, wd_ref, o_ref, xn_ref):
    f = pl.program_id(1)

    @pl.when(f == 0)
    def _():
        xn_ref[...] = _rms(x_ref[...], gpre_ref[...]).astype(BF16)
        o_ref[...] = jnp.zeros_like(o_ref)

    xn = xn_ref[...]
    hg = jnp.dot(xn, wg_ref[...], preferred_element_type=F32)
    hl = jnp.dot(xn, wl_ref[...], preferred_element_type=F32)
    a = (hg * jax.nn.sigmoid(hg) * hl).astype(BF16)
    o_ref[...] += jnp.dot(a, wd_ref[...], preferred_element_type=F32)

    @pl.when(f == pl.num_programs(1) - 1)
    def _():
        o_ref[...] = x_ref[...] + _rms(o_ref[...], gpost_ref[...])


def _ffn(x, g_pre, g_post, weights, *, tm, tf, first_tile):
    m = x.shape[0]
    nf = D_FF // tf
    rows = pl.BlockSpec((tm, D_MODEL), lambda i, f: (i + first_tile, 0))
    return pl.pallas_call(
        _ffn_kernel,
        grid=(m // tm - first_tile, nf),
        in_specs=[
            rows,
            pl.BlockSpec((1, D_MODEL), lambda i, f: (0, 0)),
            pl.BlockSpec((1, D_MODEL), lambda i, f: (0, 0)),
            pl.BlockSpec((D_MODEL, tf), lambda i, f: (0, f)),
            pl.BlockSpec((D_MODEL, tf), lambda i, f: (0, f)),
            pl.BlockSpec((tf, D_MODEL), lambda i, f: (f, 0)),
        ],
        out_specs=rows,
        out_shape=jax.ShapeDtypeStruct((m, D_MODEL), F32),
        input_output_aliases={0: 0},
        scratch_shapes=[pltpu.VMEM((tm, D_MODEL), BF16)],
        compiler_params=_params("parallel", "arbitrary", vmem_limit=FFN_VMEM_LIMIT),
        name="ffn",
    )(x, g_pre.reshape(1, D_MODEL), g_post.reshape(1, D_MODEL), *weights)


def _ffn_head_kernel(x_ref, xs_ref, gpre_ref, gpost_ref, wg_ref, wl_ref, wd_ref,
                     o_ref, os_ref, wgb_ref, wlb_ref, wdb_ref, xn_ref):
    tm = x_ref.shape[0]
    f = pl.program_id(0)

    @pl.when(f == 0)
    def _():
        xn_ref[:tm, :] = _rms(x_ref[...], gpre_ref[...]).astype(BF16)
        xn_ref[tm:, :] = _rms(xs_ref[...], gpre_ref[...]).astype(BF16)
        o_ref[...] = jnp.zeros_like(o_ref)
        os_ref[...] = jnp.zeros_like(os_ref)

    wg, wl, wd = (r[...].astype(BF16) for r in (wg_ref, wl_ref, wd_ref))
    wgb_ref[...] = wg
    wlb_ref[...] = wl
    wdb_ref[...] = wd
    xn = xn_ref[...]
    hg = jnp.dot(xn, wg, preferred_element_type=F32)
    hl = jnp.dot(xn, wl, preferred_element_type=F32)
    a = (hg * jax.nn.sigmoid(hg) * hl).astype(BF16)
    part = jnp.dot(a, wd, preferred_element_type=F32)
    o_ref[...] += part[:tm, :]
    os_ref[...] += part[tm:, :]

    @pl.when(f == pl.num_programs(0) - 1)
    def _():
        o_ref[...] = x_ref[...] + _rms(o_ref[...], gpost_ref[...])
        os_ref[...] = xs_ref[...] + _rms(os_ref[...], gpost_ref[...])


def _ffn_head(x, xs, g_pre, g_post, w_up, w_down, layer, *, tm, tf):
    m = x.shape[0]
    ms = xs.shape[0]
    nf = D_FF // tf
    once = dict(pipeline_mode=pl.Buffered(1))
    head = pl.BlockSpec((tm, D_MODEL), lambda f: (0, 0), **once)
    sample = pl.BlockSpec((ms, D_MODEL), lambda f: (0, 0))
    vec = pl.BlockSpec((1, D_MODEL), lambda f: (0, 0))
    outs = pl.pallas_call(
        _ffn_head_kernel,
        grid=(nf,),
        in_specs=[
            head, sample, vec, vec,
            pl.BlockSpec((None, D_MODEL, tf), lambda f: (layer, 0, f)),
            pl.BlockSpec((None, D_MODEL, tf), lambda f: (layer, 0, nf + f)),
            pl.BlockSpec((None, tf, D_MODEL), lambda f: (layer, f, 0)),
        ],
        out_specs=[
            head, sample,
            pl.BlockSpec((D_MODEL, tf), lambda f: (0, f)),
            pl.BlockSpec((D_MODEL, tf), lambda f: (0, f)),
            pl.BlockSpec((tf, D_MODEL), lambda f: (f, 0)),
        ],
        out_shape=[
            jax.ShapeDtypeStruct((m, D_MODEL), F32),
            jax.ShapeDtypeStruct((ms, D_MODEL), F32),
            jax.ShapeDtypeStruct((D_MODEL, D_FF), BF16),
            jax.ShapeDtypeStruct((D_MODEL, D_FF), BF16),
            jax.ShapeDtypeStruct((D_FF, D_MODEL), BF16),
        ],
        input_output_aliases={0: 0},
        scratch_shapes=[pltpu.VMEM((tm + ms, D_MODEL), BF16)],
        compiler_params=_params("arbitrary"),
        name="ffn_head",
    )(x, xs, g_pre.reshape(1, D_MODEL), g_post.reshape(1, D_MODEL), w_up, w_up, w_down)
    return outs[0], outs[1], tuple(outs[2:])


def _swa_kernel(q_ref, k_ref, v_ref, tb_ref, o_ref, lse_ref):
    n_units, n_res, u, _ = q_ref.shape
    per_blk = N_BACK // u
    n_blk = n_units // per_blk
    lane = lax.broadcasted_iota(jnp.int32, (N_BACK, HEAD_DIM), 1)
    ones = jnp.ones((2 * N_BACK, HEAD_DIM), BF16)

    def rows(ref, res, unit0, n_rows, lo, hi):
        return ref[pl.ds(unit0, n_rows // u), res, :, lo:hi].reshape(n_rows, hi - lo)

    def block(res, qu, ku, table):
        n_keys = N_BACK if table == 0 else 2 * N_BACK
        lse_tile = jnp.zeros((N_BACK, HEAD_DIM), F32)
        for h in range(HEADS_PER_GROUP):
            lo, hi = h * HEAD_DIM, (h + 1) * HEAD_DIM
            q = rows(q_ref, res, qu, N_BACK, lo, hi)
            kw = rows(k_ref, res, ku, n_keys, lo, hi)
            vw = rows(v_ref, res, ku, n_keys, lo, hi)
            s = lax.dot_general(q, kw, (((1,), (1,)), ((), ())), preferred_element_type=F32)
            s = s + tb_ref[table, h][:, :n_keys]
            m = jnp.max(s, axis=1, keepdims=True)
            p = jnp.exp(s - m).astype(BF16)
            ov = jnp.dot(p, jnp.concatenate([vw, ones[:n_keys]], axis=1), preferred_element_type=F32)
            den = ov[:, HEAD_DIM:]
            o = ov[:, :HEAD_DIM] / den
            o_ref[pl.ds(qu, per_blk), res, :, lo:hi] = o.reshape(per_blk, u, HEAD_DIM).astype(o_ref.dtype)
            lse_tile = jnp.where(lane == h, m + jnp.log(den), lse_tile)
        lse_ref[pl.ds(qu, per_blk), res, :, :] = lse_tile.reshape(per_blk, u, HEAD_DIM)

    def block_at(res, n):
        block(res, n * per_blk, (n - 1) * per_blk, 1)

    n_pairs = (n_blk - 1) // 2
    for res in range(n_res):
        block(res, 0, 0, 0)

        def body(i, carry, res=res):
            block_at(res, 1 + 2 * i)
            block_at(res, 2 + 2 * i)
            return carry

        if n_pairs > 0:
            lax.fori_loop(0, n_pairs, body, 0)
        if (n_blk - 1) % 2 == 1:
            block_at(res, n_blk - 1)


def _sub_block(dil):
    return N_BACK if dil == 1 else PERM_BLOCK


def _swa_group(zb, tables, g, *, batch, seq):
    dil = SWA_PATTERN[g][1]
    sub = _sub_block(dil)
    n_units, u = seq // sub, sub // dil
    n_res = 4 if n_units * u == 2 * N_BACK else 1
    view = zb.reshape(zb.shape[0], batch, n_units, dil, u, GROUP_WIDTH)

    def rows_in(tile):
        return pl.BlockSpec((None, None, n_units, n_res, u, GROUP_WIDTH), lambda i, r: (tile, i, 0, r, 0, 0))

    def rows_out(width):
        return pl.BlockSpec((None, n_units, n_res, u, width), lambda i, r: (i, 0, r, 0, 0))

    return pl.pallas_call(
        _swa_kernel,
        grid=(batch, dil // n_res),
        in_specs=[rows_in(g), rows_in(N_SWA_GROUPS + g), rows_in(2 * N_SWA_GROUPS + g),
                  pl.BlockSpec((None, 2, HEADS_PER_GROUP, N_BACK, 2 * N_BACK), lambda i, r: (g, 0, 0, 0, 0))],
        out_specs=[rows_out(GROUP_WIDTH), rows_out(HEAD_DIM)],
        out_shape=[
            jax.ShapeDtypeStruct((batch, n_units, dil, u, GROUP_WIDTH), BF16),
            jax.ShapeDtypeStruct((batch, n_units, dil, u, HEAD_DIM), F32),
        ],
        compiler_params=_params("parallel", "parallel"),
        name=f"swa_group{g}",
    )(view, view, view, tables)


def _split3(x):
    hi = x.astype(BF16)
    rest = x - hi.astype(F32)
    mid = rest.astype(BF16)
    lo = (rest - mid.astype(F32)).astype(BF16)
    return hi, mid, lo


def _merge_stage(o0_ref, o1_ref, o2_ref, l0_ref, l1_ref, l2_ref, q_ref, kb_ref, vb_ref, o_ref):
    tm = o_ref.shape[0]
    group_refs = ((o0_ref, l0_ref), (o1_ref, l1_ref), (o2_ref, l2_ref))
    outs, lses = [None] * N_SWA_GROUPS, [None] * N_SWA_GROUPS

    def token_order(g):
        o_g, l_g = group_refs[g]
        dil = SWA_PATTERN[g][1]
        o = o_g[...].reshape(tm, GROUP_WIDTH)
        l = l_g[...].reshape(tm, HEAD_DIM)
        if dil > 1:
            inv = _residue_major_perm(PERM_BLOCK, dil, transpose=True)
            l3 = _split3(l)
            o_nat, l_nat = [], []
            for s in range(0, tm, PERM_BLOCK):
                o_nat.append(jnp.dot(inv, o[s:s + PERM_BLOCK, :], preferred_element_type=F32))
                l_nat.append(sum(jnp.dot(inv, t[s:s + PERM_BLOCK, :], preferred_element_type=F32) for t in l3))
            o = jnp.concatenate(o_nat, axis=0)
            l = jnp.concatenate(l_nat, axis=0)
        outs[g] = o.astype(F32)
        lses[g] = l

    def merge_head(h):
        lo, hi = h * HEAD_DIM, (h + 1) * HEAD_DIM
        ls = [l[:, h:h + 1] for l in lses]
        mx = jnp.maximum(jnp.maximum(ls[0], ls[1]), ls[2])
        es = [jnp.exp(l - mx) for l in ls]
        tot = es[0] + es[1] + es[2]
        for g in range(N_SWA_GROUPS):
            alpha = es[g] / tot
            o_ref[:, g * GROUP_WIDTH + lo:g * GROUP_WIDTH + hi] = (outs[g][:, lo:hi] * alpha).astype(o_ref.dtype)

    return ([functools.partial(token_order, g) for g in range(N_SWA_GROUPS)]
            + [functools.partial(merge_head, h) for h in range(HEADS_PER_GROUP)]
            + _mem_attention_parts(q_ref, kb_ref, vb_ref, o_ref, MIXER_WIDTH))


def _swa_merge_out(outs, lses, zb, mem_kv, layer, w, x, g, *, rows_per_batch):
    tm = ROW_TILE
    tiles_per_batch = rows_per_batch // tm

    def group_tile(width, dil):
        sub = _sub_block(dil)
        return lambda tile: pl.BlockSpec(
            (None, tm // sub, dil, sub // dil, width),
            lambda s: (tile(s) // tiles_per_batch, tile(s) % tiles_per_batch, 0, 0, 0))

    specs = ([group_tile(GROUP_WIDTH, dil) for _, dil in SWA_PATTERN]
             + [group_tile(HEAD_DIM, dil) for _, dil in SWA_PATTERN]
             + [lambda tile: pl.BlockSpec((None, tm, MEM_WIDTH), lambda s: (3 * N_SWA_GROUPS, tile(s), 0))])
    return _mix_out(_merge_stage, (*outs, *lses, zb), specs, [], mem_kv, layer, w, x, g,
                    tm=tm, rows_per_batch=rows_per_batch, name="swa_merge_out")


def _sample_mem_attention(q_row, kv_ref, o_ref, col0):
    for h in range(N_MEM_HEADS):
        lo, hi = h * HEAD_DIM, (h + 1) * HEAD_DIM
        q = q_row[:, lo:hi]
        k = kv_ref[:, 0, h, :]
        v = kv_ref[:, 1, h, :]
        s = jnp.sum(k * q, axis=1, keepdims=True) * ATTN_SCALE
        m = jnp.max(s, axis=0, keepdims=True)
        p = jnp.exp(s - m)
        den = jnp.sum(p, axis=0, keepdims=True)
        o_ref[:, col0 + lo:col0 + hi] = jnp.sum(p * v, axis=0, keepdims=True) / den


def _sample_mix_a_kernel(z_ref, gv_ref, w0_ref, b0_ref, kv_ref, o_ref, vrow_ref):
    u = z_ref[:, 0:MIXER_WIDTH]
    v = _rms(z_ref[:, MIXER_WIDTH:2 * MIXER_WIDTH], gv_ref[...])
    vrow_ref[...] = v
    o_ref[:, 0:MIXER_WIDTH] = u * (w0_ref[...] * v + b0_ref[...])
    _sample_mem_attention(z_ref[:, 2 * MIXER_WIDTH:2 * MIXER_WIDTH + MEM_WIDTH], kv_ref, o_ref, MIXER_WIDTH)


def _sample_mix_a(z, g_v, w_s, b_s, mem_kv, layer):
    bd = mem_kv.shape[1]
    w0 = jnp.repeat(w_s[:, 0, 0], GROUP_DIM_A).reshape(1, MIXER_WIDTH)
    b0 = jnp.repeat(b_s[:, 0], GROUP_DIM_A).reshape(1, MIXER_WIDTH)
    width = z.shape[1]
    vec = lambda i: (0, 0)
    return pl.pallas_call(
        _sample_mix_a_kernel,
        grid=(bd,),
        in_specs=[
            pl.BlockSpec((None, 1, width), lambda i: (i, 0, 0)),
            pl.BlockSpec((1, MIXER_WIDTH), vec),
            pl.BlockSpec((1, MIXER_WIDTH), vec),
            pl.BlockSpec((1, MIXER_WIDTH), vec),
            _mem_kv_spec(layer, lambda i: i),
        ],
        out_specs=[
            pl.BlockSpec((None, 1, D_MODEL), lambda i: (i, 0, 0)),
            pl.BlockSpec((None, 1, MIXER_WIDTH), lambda i: (i, 0, 0)),
        ],
        out_shape=[
            jax.ShapeDtypeStruct((bd, 1, D_MODEL), F32),
            jax.ShapeDtypeStruct((bd, 1, MIXER_WIDTH), F32),
        ],
        compiler_params=_params("parallel"),
        name="sample_mix_a",
    )(z[:bd].reshape(bd, 1, width), g_v.reshape(1, MIXER_WIDTH), w0, b0, mem_kv)


def _sample_mix_b_kernel(z_ref, c0_ref, c1_ref, c2_ref, bcol_ref, bnew_ref, kv_ref, o_ref):
    caches = (c0_ref, c1_ref, c2_ref)
    outs = [[None] * HEADS_PER_GROUP for _ in range(N_SWA_GROUPS)]
    lses = [[None] * HEADS_PER_GROUP for _ in range(N_SWA_GROUPS)]
    for g in range(N_SWA_GROUPS):
        for h in range(HEADS_PER_GROUP):
            hd = g * HEADS_PER_GROUP + h
            lo, hi = h * HEAD_DIM, (h + 1) * HEAD_DIM
            q = z_ref[:, hd * HEAD_DIM:(hd + 1) * HEAD_DIM]
            k_new = z_ref[:, MIXER_WIDTH + hd * HEAD_DIM:MIXER_WIDTH + (hd + 1) * HEAD_DIM]
            v_new = z_ref[:, 2 * MIXER_WIDTH + hd * HEAD_DIM:2 * MIXER_WIDTH + (hd + 1) * HEAD_DIM]
            kc = caches[g][:, 0, h, :]
            vc = caches[g][:, 1, h, :]
            s_c = jnp.sum(kc * q, axis=1, keepdims=True) * ATTN_SCALE + bcol_ref[g][:, h:h + 1]
            s_n = jnp.sum(k_new * q, axis=1, keepdims=True) * ATTN_SCALE + bnew_ref[:, hd:hd + 1]
            m = jnp.maximum(jnp.max(s_c, axis=0, keepdims=True), s_n)
            p_c = jnp.exp(s_c - m)
            p_n = jnp.exp(s_n - m)
            den = jnp.sum(p_c, axis=0, keepdims=True) + p_n
            outs[g][h] = (jnp.sum(p_c * vc, axis=0, keepdims=True) + p_n * v_new) / den
            lses[g][h] = m + jnp.log(den)
    for h in range(HEADS_PER_GROUP):
        ls = [lses[g][h] for g in range(N_SWA_GROUPS)]
        mx = jnp.maximum(jnp.maximum(ls[0], ls[1]), ls[2])
        es = [jnp.exp(l - mx) for l in ls]
        tot = es[0] + es[1] + es[2]
        for g in range(N_SWA_GROUPS):
            c0 = g * GROUP_WIDTH + h * HEAD_DIM
            o_ref[:, c0:c0 + HEAD_DIM] = outs[g][h] * (es[g] / tot)
    _sample_mem_attention(z_ref[:, 3 * MIXER_WIDTH:3 * MIXER_WIDTH + MEM_WIDTH], kv_ref, o_ref, MIXER_WIDTH)


def _sample_mix_b(z, win_caches, swa_layer, bias_groups, mem_kv, layer):
    bd = mem_kv.shape[1]
    width = z.shape[1]
    cache_views, cache_specs = [], []
    for g, (win, dil) in enumerate(SWA_PATTERN):
        c = win_caches[g]
        cache_views.append(c.reshape(c.shape[0], bd, win // dil, dil, 2, HEADS_PER_GROUP, HEAD_DIM))
        cache_specs.append(pl.BlockSpec((None, None, N_BACK, None, 2, HEADS_PER_GROUP, HEAD_DIM),
                                        lambda i: (swa_layer, i, 0, 0, 0, 0, 0)))
    bcol = jnp.stack([bg[:, N_BACK:0:-1].T for bg in bias_groups], axis=0)
    bnew = jnp.concatenate([bg[:, 0] for bg in bias_groups])
    bnew = jnp.pad(bnew, (0, HEAD_DIM - bnew.shape[0])).reshape(1, HEAD_DIM)
    return pl.pallas_call(
        _sample_mix_b_kernel,
        grid=(bd,),
        in_specs=[pl.BlockSpec((None, 1, width), lambda i: (i, 0, 0))] + cache_specs + [
            pl.BlockSpec((N_SWA_GROUPS, N_BACK, HEADS_PER_GROUP), lambda i: (0, 0, 0)),
            pl.BlockSpec((1, HEAD_DIM), lambda i: (0, 0)),
            _mem_kv_spec(layer, lambda i: i),
        ],
        out_specs=pl.BlockSpec((None, 1, D_MODEL), lambda i: (i, 0, 0)),
        out_shape=jax.ShapeDtypeStruct((bd, 1, D_MODEL), F32),
        compiler_params=_params("parallel"),
        name="sample_mix_b",
    )(z[:bd].reshape(bd, 1, width), *cache_views, bcol, bnew, mem_kv)


def _t5_bucket(dist):
    nf = jnp.maximum(dist, MAX_EXACT).astype(F32)
    large = MAX_EXACT + (jnp.log(nf / MAX_EXACT) / math.log(MAX_DISTANCE / MAX_EXACT)
                         * (N_BUCKETS - MAX_EXACT)).astype(jnp.int32)
    large = jnp.minimum(large, N_BUCKETS - 1)
    return jnp.where(dist < MAX_EXACT, dist, large)


def _group_bias(rel_bias, g, dil):
    dist = jnp.arange(N_BACK + 1, dtype=jnp.int32) * dil
    b = rel_bias[_t5_bucket(dist)][:, g * HEADS_PER_GROUP:(g + 1) * HEADS_PER_GROUP]
    return b.T.astype(F32)


def _band_tables_kernel(b_ref, o_ref):
    n, rows, width = o_ref.shape
    for x in range(n):
        row = jnp.broadcast_to(b_ref[x:x + 1, :], (rows, width))
        o_ref[x] = pltpu.roll(row, 0, 1, stride=1, stride_axis=0)


def _band_tables(bias_groups):
    width = 2 * N_BACK
    rows = []
    for bias_j in bias_groups:
        masked = jnp.full((HEADS_PER_GROUP, N_BACK - 1), NEG_INF, F32)
        rows.append(jnp.concatenate([bias_j[:, :1], masked, bias_j[:, N_BACK:0:-1]], axis=1))
        rows.append(jnp.concatenate([bias_j[:, ::-1], masked], axis=1))
    base = jnp.stack(rows, axis=0).reshape(-1, width)
    tabs = pl.pallas_call(
        _band_tables_kernel,
        out_shape=jax.ShapeDtypeStruct((base.shape[0], N_BACK, width), F32),
        name="band_tables",
    )(base)
    return tabs.reshape(N_SWA_GROUPS, 2, HEADS_PER_GROUP, N_BACK, width)


def _kv_tail_kernel(k_ref, v_ref, o_ref, *, dil):
    rows = k_ref.shape[0]
    for kv, ref in enumerate((k_ref, v_ref)):
        x = ref[...]
        if dil > 1:
            inv = _residue_major_perm(PERM_BLOCK, dil, transpose=True)
            x = jnp.concatenate([jnp.dot(inv, x[s:s + PERM_BLOCK, :], preferred_element_type=F32)
                                 for s in range(0, rows, PERM_BLOCK)], axis=0)
        x = x.astype(F32)
        for h in range(HEADS_PER_GROUP):
            o_ref[:, kv, h, :] = x[:, h * HEAD_DIM:(h + 1) * HEAD_DIM]


def _kv_tail(zb, g, *, batch, seq):
    win, dil = SWA_PATTERN[g]
    rows = min(win, PERM_BLOCK)
    first = (seq - win) // rows
    per_batch = seq // rows

    def tile(t):
        return pl.BlockSpec((None, rows, GROUP_WIDTH), lambda b, s: (t, b * per_batch + first + s, 0))

    return pl.pallas_call(
        functools.partial(_kv_tail_kernel, dil=dil),
        grid=(batch, win // rows),
        in_specs=[tile(N_SWA_GROUPS + g), tile(2 * N_SWA_GROUPS + g)],
        out_specs=pl.BlockSpec((None, rows, 2, HEADS_PER_GROUP, HEAD_DIM), lambda b, s: (b, s, 0, 0, 0)),
        out_shape=jax.ShapeDtypeStruct((batch, win, 2, HEADS_PER_GROUP, HEAD_DIM), F32),
        compiler_params=_params("parallel", "parallel"),
        name=f"kv_tail{g}",
    )(zb, zb)


def kernel(x_prompt, x_sample, mem_prompt, cache_mem_kv, cache_win128_kv, cache_win512_kv, cache_win2048_kv, rel_bias, norm_mix_pre, norm_mix_post, norm_ffn_pre, norm_ffn_post, norm_mem, w_mem_kv, w_in_a, norm_v_a, w_spatial_a, b_spatial_a, w_in_b, w_out, w_ffn_up, w_ffn_down):
    batch, seq, _ = x_prompt.shape
    bd = x_sample.shape[0]
    depth = w_out.shape[0]
    m_p = batch * seq
    win_caches = (cache_win128_kv, cache_win512_kv, cache_win2048_kv)

    bias_groups = [_group_bias(rel_bias, g, dil) for g, (_, dil) in enumerate(SWA_PATTERN)]
    band_tables = _band_tables(bias_groups)

    yp = x_prompt.reshape(m_p, D_MODEL)
    ys = jnp.pad(x_sample.reshape(bd, D_MODEL), ((0, SAMPLE_PAD - bd), (0, 0)))
    mem_rows = mem_prompt.reshape(batch * N_MEM, D_MODEL)

    mem_kv_p = _mem_kv(mem_rows, norm_mem, w_mem_kv, batch=batch)
    chunk_v_s = []
    win_p = [[] for _ in SWA_PATTERN]
    win_s = [[] for _ in SWA_PATTERN]
    for i in range(depth):
        li = i // 2
        if i % 2 == 0:
            zp, zs = _in_proj_a(yp, ys, norm_mix_pre[i], w_in_a, li, tm=ROW_TILE, tn=CAST_TILE,
                                gelu_cols=2 * MIXER_WIDTH)
            mix_s, v_rows = _sample_mix_a(zs, norm_v_a[li], w_spatial_a[li], b_spatial_a[li], cache_mem_kv, i)
            chunk_v_s.append(v_rows)
            ys, w_o = _sample_out_proj(mix_s, w_out, i, ys, norm_mix_post[i])
            yp = _gmlp_mix_out(zp, norm_v_a[li], w_spatial_a[li], b_spatial_a[li], mem_kv_p, i,
                               w_o, yp, norm_mix_post[i], tm=ROW_TILE, rows_per_batch=seq)
        else:
            zs, w_in = _norm_matmul(ys, norm_mix_pre[i], w_in_b, li, tm=SAMPLE_PAD, tn=CAST_TILE, emit_w=True,
                                    gelu_cols=0, out_dtype=F32, name="in_proj_b_s")
            zb = _in_proj_b(yp, norm_mix_pre[i], w_in, tm=ROW_TILE)
            outs, lses = [], []
            for g, (win, dil) in enumerate(SWA_PATTERN):
                o, lse = _swa_group(zb, band_tables, g, batch=batch, seq=seq)
                outs.append(o)
                lses.append(lse)
                win_p[g].append(_kv_tail(zb, g, batch=batch, seq=seq))
                kv_new = zs[:bd, MIXER_WIDTH:3 * MIXER_WIDTH]
                kv_new = kv_new.reshape(bd, 1, 2, N_SWA_GROUPS, HEADS_PER_GROUP, HEAD_DIM)[:, :, :, g]
                win_s[g].append(kv_new)
            mix_s = _sample_mix_b(zs, win_caches, li, bias_groups, cache_mem_kv, i)
            ys, w_o = _sample_out_proj(mix_s, w_out, i, ys, norm_mix_post[i])
            yp = _swa_merge_out(outs, lses, zb, mem_kv_p, i, w_o, yp, norm_mix_post[i], rows_per_batch=seq)
        yp, ys, w_ffn = _ffn_head(yp, ys, norm_ffn_pre[i], norm_ffn_post[i], w_ffn_up, w_ffn_down, i,
                                  tm=FFN_ROW_TILE, tf=HEAD_FF_TILE)
        yp = _ffn(yp, norm_ffn_pre[i], norm_ffn_post[i], w_ffn, tm=FFN_ROW_TILE, tf=FF_TILE, first_tile=1)

    return (
        yp.reshape(batch, seq, D_MODEL),
        ys[:bd].reshape(bd, 1, D_MODEL),
        mem_kv_p,
        jnp.stack(chunk_v_s, axis=0),
        jnp.stack(win_p[0], axis=0),
        jnp.stack(win_p[1], axis=0),
        jnp.stack(win_p[2], axis=0),
        jnp.stack(win_s[0], axis=0),
        jnp.stack(win_s[1], axis=0),
        jnp.stack(win_s[2], axis=0),
    )
```

```python
import functools
import math

import jax
import jax.numpy as jnp
from jax import lax
from jax.experimental import pallas as pl
from jax.experimental.pallas import tpu as pltpu

F32 = jnp.float32
BF16 = jnp.bfloat16

D_MODEL = 2048
HEAD_DIM = 128
N_MEM = 256
N_MEM_HEADS = 4
MEM_WIDTH = N_MEM_HEADS * HEAD_DIM
MIXER_WIDTH = D_MODEL - MEM_WIDTH
CHUNK = 128
N_GROUPS_A = 4
GROUP_DIM_A = MIXER_WIDTH // N_GROUPS_A
SWA_PATTERN = ((128, 1), (512, 4), (2048, 16))
N_SWA_GROUPS = len(SWA_PATTERN)
HEADS_PER_GROUP = 4
GROUP_WIDTH = HEADS_PER_GROUP * HEAD_DIM
N_BACK = 128
N_BUCKETS = 32
MAX_EXACT = N_BUCKETS // 2
MAX_DISTANCE = 2048
D_FF = 5632
EPS = 1e-6
NEG_INF = -1e30
ATTN_SCALE = HEAD_DIM ** -0.5
SAMPLE_PAD = 16
PERM_BLOCK = 256
OUT_CHUNK = 256

ROW_TILE = 512
FFN_ROW_TILE = 1024
FF_TILE = 512
HEAD_FF_TILE = 256
CAST_TILE = 512

V7X_VMEM_BYTES = 64 * 1024 * 1024
VMEM_LIMIT = V7X_VMEM_BYTES - 8 * 1024 * 1024
FFN_VMEM_LIMIT = V7X_VMEM_BYTES - 2 * 1024 * 1024


def _params(*sem, vmem_limit=VMEM_LIMIT):
    return pltpu.CompilerParams(dimension_semantics=sem, vmem_limit_bytes=vmem_limit)


def _gelu(x):
    return 0.5 * x * (1.0 + jnp.tanh(0.7978845608028654 * (x + 0.044715 * (x * x * x))))


def _rms(x, g):
    return x * lax.rsqrt(jnp.mean(x * x, axis=-1, keepdims=True) + EPS) * g


def _log2(n):
    assert n & (n - 1) == 0
    return n.bit_length() - 1


def _residue_major_perm(tm, dil, transpose=False):
    n = tm // dil
    row = lax.broadcasted_iota(jnp.int32, (tm, tm), 0)
    col = lax.broadcasted_iota(jnp.int32, (tm, tm), 1)
    dst, src = (col, row) if transpose else (row, col)
    want = lax.shift_left(jnp.bitwise_and(dst, n - 1), _log2(dil)) + lax.shift_right_logical(dst, _log2(n))
    return (src == want).astype(BF16)


def _in_proj_a_kernel(x_ref, xs_ref, g_ref, w_ref, zp_ref, zs_ref, wb_ref, xn_ref, *, n_col, gelu_cols):
    tm = x_ref.shape[0]
    tn = w_ref.shape[1]
    s = pl.program_id(0)

    def prompt_cols(acc, c0):
        col = lax.broadcasted_iota(jnp.int32, acc.shape, 1) + c0
        return jnp.where(col < gelu_cols, _gelu(acc), acc * ATTN_SCALE).astype(zp_ref.dtype)

    @pl.when(s == 0)
    def _():
        xn_ref[:tm, :] = _rms(x_ref[...], g_ref[...]).astype(BF16)
        xn_ref[tm:, :] = _rms(xs_ref[...], g_ref[...]).astype(BF16)

    for j in range(n_col):
        @pl.when(s == j)
        def _(j=j):
            cols = slice(j * tn, (j + 1) * tn)
            w = w_ref[...].astype(BF16)
            wb_ref[:, cols] = w
            acc = jnp.dot(xn_ref[...], w, preferred_element_type=F32)
            zp_ref[:, cols] = prompt_cols(acc[:tm, :], j * tn)
            sample = acc[tm:, :]
            zs_ref[:, cols] = _gelu(sample) if (j + 1) * tn <= gelu_cols else sample

    @pl.when(s >= n_col)
    def _():
        xn = _rms(x_ref[...], g_ref[...]).astype(BF16)
        acc = jnp.dot(xn, wb_ref[...], preferred_element_type=F32)
        zp_ref[...] = prompt_cols(acc, 0)


def _in_proj_a(x, xs, g, w, layer, *, tm, tn, gelu_cols):
    m, k = x.shape
    ms = xs.shape[0]
    n = w.shape[2]
    n_col, n_row = n // tn, m // tm
    assert gelu_cols % tn == 0
    row = lambda s: jnp.maximum(s - (n_col - 1), 0)
    return pl.pallas_call(
        functools.partial(_in_proj_a_kernel, n_col=n_col, gelu_cols=gelu_cols),
        grid=(n_col + n_row - 1,),
        in_specs=[
            pl.BlockSpec((tm, k), lambda s: (row(s), 0)),
            pl.BlockSpec((ms, k), lambda s: (0, 0)),
            pl.BlockSpec((1, k), lambda s: (0, 0)),
            pl.BlockSpec((None, k, tn), lambda s: (layer, 0, jnp.minimum(s, n_col - 1))),
        ],
        out_specs=[
            pl.BlockSpec((tm, n), lambda s: (row(s), 0)),
            pl.BlockSpec((ms, n), lambda s: (0, 0)),
        ],
        out_shape=[
            jax.ShapeDtypeStruct((m, n), BF16),
            jax.ShapeDtypeStruct((ms, n), F32),
        ],
        scratch_shapes=[pltpu.VMEM((k, n), BF16), pltpu.VMEM((tm + ms, k), BF16)],
        compiler_params=_params("arbitrary"),
        name="in_proj_a",
    )(x, xs, g.reshape(1, k), w)


def _mem_kv_kernel(x_ref, g_ref, w_ref, o_ref):
    xn = _rms(x_ref[...], g_ref[...]).astype(BF16)
    acc = jnp.dot(xn, w_ref[...].astype(BF16), preferred_element_type=F32)
    for kv in range(2):
        for h in range(N_MEM_HEADS):
            c0 = (kv * N_MEM_HEADS + h) * HEAD_DIM
            o_ref[:, kv, h, :] = acc[:, c0:c0 + HEAD_DIM]


def _mem_kv(mem_rows, g, w, *, batch):
    m, k = mem_rows.shape
    layers = w.shape[0]
    out = pl.pallas_call(
        _mem_kv_kernel,
        grid=(layers,),
        in_specs=[
            pl.BlockSpec((m, k), lambda l: (0, 0)),
            pl.BlockSpec((None, 1, k), lambda l: (l, 0, 0)),
            pl.BlockSpec((None, k, 2 * MEM_WIDTH), lambda l: (l, 0, 0)),
        ],
        out_specs=pl.BlockSpec((None, m, 2, N_MEM_HEADS, HEAD_DIM), lambda l: (l, 0, 0, 0, 0)),
        out_shape=jax.ShapeDtypeStruct((layers, m, 2, N_MEM_HEADS, HEAD_DIM), F32),
        compiler_params=_params("parallel"),
        name="mem_kv",
    )(mem_rows, g.reshape(layers, 1, k), w)
    return out.reshape(layers, batch, m // batch, 2, N_MEM_HEADS, HEAD_DIM)


def _in_proj_b_kernel(x_ref, xs_ref, g_ref, w_ref, o_ref, zs_ref, wb_ref, xn_ref):
    tm = x_ref.shape[0]
    n_tiles = o_ref.shape[0]
    n_qkv = 3 * N_SWA_GROUPS
    s = pl.program_id(0)

    def row_orders():
        xn = _rms(x_ref[...], g_ref[...]).astype(BF16)
        xn_ref[0, :tm, :] = xn
        for g in range(1, N_SWA_GROUPS):
            perm = _residue_major_perm(PERM_BLOCK, SWA_PATTERN[g][1])
            for r in range(0, tm, PERM_BLOCK):
                xn_ref[g, r:r + PERM_BLOCK, :] = jnp.dot(
                    perm, xn[r:r + PERM_BLOCK, :], preferred_element_type=F32).astype(BF16)

    def prompt_tile(t, acc):
        if t < N_SWA_GROUPS or t == n_qkv:
            acc = acc * ATTN_SCALE
        o_ref[t] = acc.astype(o_ref.dtype)

    src = lambda t: t % N_SWA_GROUPS if t < n_qkv else 0

    @pl.when(s == 0)
    def _():
        row_orders()
        xs = _rms(xs_ref[...], g_ref[...]).astype(BF16)
        for g in range(N_SWA_GROUPS):
            xn_ref[g, tm:, :] = xs

    for t in range(n_tiles):
        @pl.when(s == t)
        def _(t=t):
            cols = slice(t * GROUP_WIDTH, (t + 1) * GROUP_WIDTH)
            w = w_ref[...].astype(BF16)
            wb_ref[:, cols] = w
            acc = jnp.dot(xn_ref[src(t)], w, preferred_element_type=F32)
            prompt_tile(t, acc[:tm, :])
            zs_ref[:, cols] = acc[tm:, :]

    @pl.when(s >= n_tiles)
    def _():
        row_orders()
        for t in range(n_tiles):
            w = wb_ref[:, t * GROUP_WIDTH:(t + 1) * GROUP_WIDTH]
            prompt_tile(t, jnp.dot(xn_ref[src(t), :tm, :], w, preferred_element_type=F32))


def _in_proj_b(x, xs, g, w, layer, *, tm):
    m, k = x.shape
    ms = xs.shape[0]
    n = w.shape[2]
    n_tiles = n // GROUP_WIDTH
    row = lambda s: jnp.maximum(s - (n_tiles - 1), 0)
    return pl.pallas_call(
        _in_proj_b_kernel,
        grid=(n_tiles + m // tm - 1,),
        in_specs=[
            pl.BlockSpec((tm, k), lambda s: (row(s), 0)),
            pl.BlockSpec((ms, k), lambda s: (0, 0)),
            pl.BlockSpec((1, k), lambda s: (0, 0)),
            pl.BlockSpec((None, k, GROUP_WIDTH), lambda s: (layer, 0, jnp.minimum(s, n_tiles - 1))),
        ],
        out_specs=[
            pl.BlockSpec((n_tiles, tm, GROUP_WIDTH), lambda s: (0, row(s), 0)),
            pl.BlockSpec((ms, n), lambda s: (0, 0)),
        ],
        out_shape=[
            jax.ShapeDtypeStruct((n_tiles, m, GROUP_WIDTH), BF16),
            jax.ShapeDtypeStruct((ms, n), F32),
        ],
        scratch_shapes=[pltpu.VMEM((k, n), BF16), pltpu.VMEM((N_SWA_GROUPS, tm + ms, k), BF16)],
        compiler_params=_params("arbitrary", vmem_limit=FFN_VMEM_LIMIT),
        name="in_proj_b",
    )(x, xs, g.reshape(1, k), w)


MEM_KV_SCRATCH = [pltpu.VMEM((N_MEM_HEADS, N_MEM, HEAD_DIM), BF16),
                  pltpu.VMEM((N_MEM_HEADS, N_MEM, 2 * HEAD_DIM), BF16)]


def _prepare_mem_kv(kv_ref, kb_ref, vb_ref):
    for h in range(N_MEM_HEADS):
        kb_ref[h] = kv_ref[:, 0, h, :].astype(BF16)
        vb_ref[h, :, :HEAD_DIM] = kv_ref[:, 1, h, :].astype(BF16)
        vb_ref[h, :, HEAD_DIM:] = jnp.ones((N_MEM, HEAD_DIM), BF16)


def _mem_attention_parts(q_ref, kb_ref, vb_ref, o_ref, col0):
    def head(h):
        lo, hi = h * HEAD_DIM, (h + 1) * HEAD_DIM
        s = lax.dot_general(q_ref[:, lo:hi], kb_ref[h], (((1,), (1,)), ((), ())), preferred_element_type=F32)
        m = jnp.max(s, axis=1, keepdims=True)
        p = jnp.exp(s - m).astype(BF16)
        ov = jnp.dot(p, vb_ref[h], preferred_element_type=F32)
        o_ref[:, col0 + lo:col0 + hi] = (ov[:, :HEAD_DIM] / ov[:, HEAD_DIM:]).astype(o_ref.dtype)

    return [functools.partial(head, h) for h in range(N_MEM_HEADS)]


def _gmlp_stage(u_ref, v_ref, q_ref, gv_ref, ws_ref, bs_ref, kb_ref, vb_ref, vn_ref, o_ref):
    tm = u_ref.shape[0]

    def norm_v():
        vn_ref[...] = _rms(v_ref[...].astype(F32), gv_ref[...]).astype(BF16)

    def group(g):
        row = lax.broadcasted_iota(jnp.int32, (CHUNK, CHUNK), 0)
        col = lax.broadcasted_iota(jnp.int32, (CHUNK, CHUNK), 1)
        w = jnp.where(row >= col, ws_ref[g], 0.0).astype(BF16)
        b = bs_ref[:, g:g + 1]
        c0, c1 = g * GROUP_DIM_A, (g + 1) * GROUP_DIM_A
        for c in range(tm // CHUNK):
            r0, r1 = c * CHUNK, (c + 1) * CHUNK
            s = jnp.dot(w, vn_ref[r0:r1, c0:c1], preferred_element_type=F32) + b
            o_ref[r0:r1, c0:c1] = (u_ref[r0:r1, c0:c1].astype(F32) * s).astype(o_ref.dtype)

    return ([norm_v] + [functools.partial(group, g) for g in range(N_GROUPS_A)]
            + _mem_attention_parts(q_ref, kb_ref, vb_ref, o_ref, MIXER_WIDTH))


def _mem_kv_spec(layer, batch_of):
    return pl.BlockSpec((None, None, N_MEM, 2, N_MEM_HEADS, HEAD_DIM),
                        lambda i: (layer, batch_of(i), 0, 0, 0, 0))


def _mix_out_kernel(*refs, stage, n_in, n_tiles, tiles_per_batch):
    mix_in = refs[:n_in]
    kv_ref, w_ref, x_ref, g_ref, o_ref, buf0_ref, buf1_ref, acc_ref, kb_ref, vb_ref = refs[n_in:n_in + 10]
    extra = refs[n_in + 10:]
    s = pl.program_id(0)

    @pl.when(s == 0)
    def _():
        buf1_ref[...] = jnp.zeros_like(buf1_ref)

    @pl.when(jnp.minimum(s, n_tiles - 1) % tiles_per_batch == 0)
    def _():
        _prepare_mem_kv(kv_ref, kb_ref, vb_ref)

    def step(dst_ref, src_ref):
        parts = stage(*mix_in, kb_ref, vb_ref, *extra, dst_ref)
        n_chunks = D_MODEL // OUT_CHUNK
        for c in range(n_chunks):
            cols = slice(c * OUT_CHUNK, (c + 1) * OUT_CHUNK)
            acc_ref[:, cols] = jnp.dot(src_ref[...], w_ref[:, cols], preferred_element_type=F32)
            for part in parts[c * len(parts) // n_chunks:(c + 1) * len(parts) // n_chunks]:
                part()
        o_ref[...] = x_ref[...] + _rms(acc_ref[...], g_ref[...])

    @pl.when(s % 2 == 0)
    def _():
        step(buf0_ref, buf1_ref)

    @pl.when(s % 2 == 1)
    def _():
        step(buf1_ref, buf0_ref)


def _mix_out(stage, mix_inputs, mix_specs, extra_scratch, mem_kv, layer, w, x, g, *, tm, rows_per_batch, name):
    m = x.shape[0]
    n_tiles = m // tm
    tiles_per_batch = rows_per_batch // tm
    mix_tile = lambda s: jnp.minimum(s, n_tiles - 1)
    out_tile = lambda s: jnp.maximum(s - 1, 0)
    row_spec = pl.BlockSpec((tm, D_MODEL), lambda s: (out_tile(s), 0))
    return pl.pallas_call(
        functools.partial(_mix_out_kernel, stage=stage, n_in=len(mix_inputs), n_tiles=n_tiles,
                          tiles_per_batch=tiles_per_batch),
        grid=(n_tiles + 1,),
        in_specs=[spec(mix_tile) for spec in mix_specs] + [
            _mem_kv_spec(layer, lambda s: mix_tile(s) // tiles_per_batch),
            pl.BlockSpec((D_MODEL, D_MODEL), lambda s: (0, 0), pipeline_mode=pl.Buffered(1)),
            row_spec,
            pl.BlockSpec((1, D_MODEL), lambda s: (0, 0)),
        ],
        out_specs=row_spec,
        out_shape=jax.ShapeDtypeStruct((m, D_MODEL), F32),
        scratch_shapes=[pltpu.VMEM((tm, D_MODEL), BF16), pltpu.VMEM((tm, D_MODEL), BF16),
                        pltpu.VMEM((tm, D_MODEL), F32)] + MEM_KV_SCRATCH + list(extra_scratch),
        compiler_params=_params("arbitrary"),
        name=name,
    )(*mix_inputs, mem_kv, w, x, g.reshape(1, D_MODEL))


def _gmlp_mix_out(zact, g_v, w_s, b_s, mem_kv, layer, w, x, g, *, tm, rows_per_batch):
    const = lambda shape: (lambda tile: pl.BlockSpec(shape, lambda s: (0,) * len(shape)))
    specs = [
        lambda tile: pl.BlockSpec((tm, MIXER_WIDTH), lambda s: (tile(s), 0)),
        lambda tile: pl.BlockSpec((tm, MIXER_WIDTH), lambda s: (tile(s), 1)),
        lambda tile: pl.BlockSpec((tm, MEM_WIDTH), lambda s: (tile(s), 2 * MIXER_WIDTH // MEM_WIDTH)),
        const((1, MIXER_WIDTH)),
        const((N_GROUPS_A, CHUNK, CHUNK)),
        const((CHUNK, N_GROUPS_A)),
    ]
    return _mix_out(_gmlp_stage, (zact, zact, zact, g_v.reshape(1, MIXER_WIDTH), w_s, b_s.T), specs,
                    [pltpu.VMEM((tm, MIXER_WIDTH), BF16)], mem_kv, layer, w, x, g,
                    tm=tm, rows_per_batch=rows_per_batch, name="gmlp_mix_out")


def _out_proj_cast_kernel(mix_ref, w_ref, x_ref, g_ref, o_ref, wb_ref, acc_ref):
    j = pl.program_id(0)
    w = w_ref[...].astype(BF16)
    wb_ref[...] = w
    acc_ref[j] = jnp.dot(mix_ref[...], w, preferred_element_type=F32)

    @pl.when(j == pl.num_programs(0) - 1)
    def _():
        o = jnp.concatenate([acc_ref[t] for t in range(acc_ref.shape[0])], axis=1)
        o_ref[...] = x_ref[...] + _rms(o, g_ref[...])


def _sample_out_proj(mix, w, layer, x, g):
    bd = mix.shape[0]
    mix = jnp.pad(mix.reshape(bd, D_MODEL), ((0, x.shape[0] - bd), (0, 0))).astype(BF16)
    return _out_proj_cast(mix, w, layer, x, g, tn=CAST_TILE)


def _out_proj_cast(mix, w, layer, x, g, *, tn):
    m = x.shape[0]
    n_tiles = D_MODEL // tn
    return pl.pallas_call(
        _out_proj_cast_kernel,
        grid=(n_tiles,),
        in_specs=[
            pl.BlockSpec((m, D_MODEL), lambda j: (0, 0)),
            pl.BlockSpec((None, D_MODEL, tn), lambda j: (layer, 0, j)),
            pl.BlockSpec((m, D_MODEL), lambda j: (0, 0)),
            pl.BlockSpec((1, D_MODEL), lambda j: (0, 0)),
        ],
        out_specs=[
            pl.BlockSpec((m, D_MODEL), lambda j: (0, 0)),
            pl.BlockSpec((D_MODEL, tn), lambda j: (0, j)),
        ],
        out_shape=[
            jax.ShapeDtypeStruct((m, D_MODEL), F32),
            jax.ShapeDtypeStruct((D_MODEL, D_MODEL), BF16),
        ],
        scratch_shapes=[pltpu.VMEM((n_tiles, m, tn), F32)],
        compiler_params=_params("arbitrary"),
        name="out_proj_cast",
    )(mix, w, x, g.reshape(1, D_MODEL))


def _ffn_kernel(x_ref, gpre_ref, gpost_ref, wg_ref, wl_ref, wd_ref, o_ref, xn_ref):
    f = pl.program_id(1)

    @pl.when(f == 0)
    def _():
        xn_ref[...] = _rms(x_ref[...], gpre_ref[...]).astype(BF16)
        o_ref[...] = jnp.zeros_like(o_ref)

    xn = xn_ref[...]
    hg = jnp.dot(xn, wg_ref[...], preferred_element_type=F32)
    hl = jnp.dot(xn, wl_ref[...], preferred_element_type=F32)
    a = (hg * jax.nn.sigmoid(hg) * hl).astype(BF16)
    o_ref[...] += jnp.dot(a, wd_ref[...], preferred_element_type=F32)

    @pl.when(f == pl.num_programs(1) - 1)
    def _():
        o_ref[...] = x_ref[...] + _rms(o_ref[...], gpost_ref[...])


def _ffn(x, g_pre, g_post, weights, *, tm, tf, first_tile):
    m = x.shape[0]
    nf = D_FF // tf
    rows = pl.BlockSpec((tm, D_MODEL), lambda i, f: (i + first_tile, 0))
    return pl.pallas_call(
        _ffn_kernel,
        grid=(m // tm - first_tile, nf),
        in_specs=[
            rows,
            pl.BlockSpec((1, D_MODEL), lambda i, f: (0, 0)),
            pl.BlockSpec((1, D_MODEL), lambda i, f: (0, 0)),
            pl.BlockSpec((D_MODEL, tf), lambda i, f: (0, f)),
            pl.BlockSpec((D_MODEL, tf), lambda i, f: (0, f)),
            pl.BlockSpec((tf, D_MODEL), lambda i, f: (f, 0)),
        ],
        out_specs=rows,
        out_shape=jax.ShapeDtypeStruct((m, D_MODEL), F32),
        input_output_aliases={0: 0},
        scratch_shapes=[pltpu.VMEM((tm, D_MODEL), BF16)],
        compiler_params=_params("parallel", "arbitrary", vmem_limit=FFN_VMEM_LIMIT),
        name="ffn",
    )(x, g_pre.reshape(1, D_MODEL), g_post.reshape(1, D_MODEL), *weights)


def _ffn_head_kernel(x_ref, xs_ref, gpre_ref, gpost_ref, wg_ref, wl_ref, wd_ref,
                     o_ref, os_ref, wgb_ref, wlb_ref, wdb_ref, xn_ref):
    tm = x_ref.shape[0]
    f = pl.program_id(0)

    @pl.when(f == 0)
    def _():
        xn_ref[:tm, :] = _rms(x_ref[...], gpre_ref[...]).astype(BF16)
        xn_ref[tm:, :] = _rms(xs_ref[...], gpre_ref[...]).astype(BF16)
        o_ref[...] = jnp.zeros_like(o_ref)
        os_ref[...] = jnp.zeros_like(os_ref)

    wg, wl, wd = (r[...].astype(BF16) for r in (wg_ref, wl_ref, wd_ref))
    wgb_ref[...] = wg
    wlb_ref[...] = wl
    wdb_ref[...] = wd
    xn = xn_ref[...]
    hg = jnp.dot(xn, wg, preferred_element_type=F32)
    hl = jnp.dot(xn, wl, preferred_element_type=F32)
    a = (hg * jax.nn.sigmoid(hg) * hl).astype(BF16)
    part = jnp.dot(a, wd, preferred_element_type=F32)
    o_ref[...] += part[:tm, :]
    os_ref[...] += part[tm:, :]

    @pl.when(f == pl.num_programs(0) - 1)
    def _():
        o_ref[...] = x_ref[...] + _rms(o_ref[...], gpost_ref[...])
        os_ref[...] = xs_ref[...] + _rms(os_ref[...], gpost_ref[...])


def _ffn_head(x, xs, g_pre, g_post, w_up, w_down, layer, *, tm, tf):
    m = x.shape[0]
    ms = xs.shape[0]
    nf = D_FF // tf
    once = dict(pipeline_mode=pl.Buffered(1))
    head = pl.BlockSpec((tm, D_MODEL), lambda f: (0, 0), **once)
    sample = pl.BlockSpec((ms, D_MODEL), lambda f: (0, 0))
    vec = pl.BlockSpec((1, D_MODEL), lambda f: (0, 0))
    outs = pl.pallas_call(
        _ffn_head_kernel,
        grid=(nf,),
        in_specs=[
            head, sample, vec, vec,
            pl.BlockSpec((None, D_MODEL, tf), lambda f: (layer, 0, f)),
            pl.BlockSpec((None, D_MODEL, tf), lambda f: (layer, 0, nf + f)),
            pl.BlockSpec((None, tf, D_MODEL), lambda f: (layer, f, 0)),
        ],
        out_specs=[
            head, sample,
            pl.BlockSpec((D_MODEL, tf), lambda f: (0, f)),
            pl.BlockSpec((D_MODEL, tf), lambda f: (0, f)),
            pl.BlockSpec((tf, D_MODEL), lambda f: (f, 0)),
        ],
        out_shape=[
            jax.ShapeDtypeStruct((m, D_MODEL), F32),
            jax.ShapeDtypeStruct((ms, D_MODEL), F32),
            jax.ShapeDtypeStruct((D_MODEL, D_FF), BF16),
            jax.ShapeDtypeStruct((D_MODEL, D_FF), BF16),
            jax.ShapeDtypeStruct((D_FF, D_MODEL), BF16),
        ],
        input_output_aliases={0: 0},
        scratch_shapes=[pltpu.VMEM((tm + ms, D_MODEL), BF16)],
        compiler_params=_params("arbitrary"),
        name="ffn_head",
    )(x, xs, g_pre.reshape(1, D_MODEL), g_post.reshape(1, D_MODEL), w_up, w_up, w_down)
    return outs[0], outs[1], tuple(outs[2:])


def _swa_kernel(q_ref, k_ref, v_ref, tb_ref, o_ref, lse_ref):
    n_units, n_res, u, _ = q_ref.shape
    per_blk = N_BACK // u
    n_blk = n_units // per_blk
    lane = lax.broadcasted_iota(jnp.int32, (N_BACK, HEAD_DIM), 1)
    ones = jnp.ones((2 * N_BACK, HEAD_DIM), BF16)

    def rows(ref, res, unit0, n_rows, lo, hi):
        return ref[pl.ds(unit0, n_rows // u), res, :, lo:hi].reshape(n_rows, hi - lo)

    def block(res, qu, ku, table):
        n_keys = N_BACK if table == 0 else 2 * N_BACK
        lse_tile = jnp.zeros((N_BACK, HEAD_DIM), F32)
        for h in range(HEADS_PER_GROUP):
            lo, hi = h * HEAD_DIM, (h + 1) * HEAD_DIM
            q = rows(q_ref, res, qu, N_BACK, lo, hi)
            kw = rows(k_ref, res, ku, n_keys, lo, hi)
            vw = rows(v_ref, res, ku, n_keys, lo, hi)
            s = lax.dot_general(q, kw, (((1,), (1,)), ((), ())), preferred_element_type=F32)
            s = s + tb_ref[table, h][:, :n_keys]
            m = jnp.max(s, axis=1, keepdims=True)
            p = jnp.exp(s - m).astype(BF16)
            ov = jnp.dot(p, jnp.concatenate([vw, ones[:n_keys]], axis=1), preferred_element_type=F32)
            den = ov[:, HEAD_DIM:]
            o = ov[:, :HEAD_DIM] / den
            o_ref[pl.ds(qu, per_blk), res, :, lo:hi] = o.reshape(per_blk, u, HEAD_DIM).astype(o_ref.dtype)
            lse_tile = jnp.where(lane == h, m + jnp.log(den), lse_tile)
        lse_ref[pl.ds(qu, per_blk), res, :, :] = lse_tile.reshape(per_blk, u, HEAD_DIM)

    def block_at(res, n):
        block(res, n * per_blk, (n - 1) * per_blk, 1)

    n_pairs = (n_blk - 1) // 2
    for res in range(n_res):
        block(res, 0, 0, 0)

        def body(i, carry, res=res):
            block_at(res, 1 + 2 * i)
            block_at(res, 2 + 2 * i)
            return carry

        if n_pairs > 0:
            lax.fori_loop(0, n_pairs, body, 0)
        if (n_blk - 1) % 2 == 1:
            block_at(res, n_blk - 1)


def _sub_block(dil):
    return N_BACK if dil == 1 else PERM_BLOCK


def _swa_group(zb, tables, g, *, batch, seq):
    dil = SWA_PATTERN[g][1]
    sub = _sub_block(dil)
    n_units, u = seq // sub, sub // dil
    n_res = 4 if n_units * u == 2 * N_BACK else 1
    view = zb.reshape(zb.shape[0], batch, n_units, dil, u, GROUP_WIDTH)

    def rows_in(tile):
        return pl.BlockSpec((None, None, n_units, n_res, u, GROUP_WIDTH), lambda i, r: (tile, i, 0, r, 0, 0))

    def rows_out(width):
        return pl.BlockSpec((None, n_units, n_res, u, width), lambda i, r: (i, 0, r, 0, 0))

    return pl.pallas_call(
        _swa_kernel,
        grid=(batch, dil // n_res),
        in_specs=[rows_in(g), rows_in(N_SWA_GROUPS + g), rows_in(2 * N_SWA_GROUPS + g),
                  pl.BlockSpec((None, 2, HEADS_PER_GROUP, N_BACK, 2 * N_BACK), lambda i, r: (g, 0, 0, 0, 0))],
        out_specs=[rows_out(GROUP_WIDTH), rows_out(HEAD_DIM)],
        out_shape=[
            jax.ShapeDtypeStruct((batch, n_units, dil, u, GROUP_WIDTH), BF16),
            jax.ShapeDtypeStruct((batch, n_units, dil, u, HEAD_DIM), F32),
        ],
        compiler_params=_params("parallel", "parallel"),
        name=f"swa_group{g}",
    )(view, view, view, tables)


def _split3(x):
    hi = x.astype(BF16)
    rest = x - hi.astype(F32)
    mid = rest.astype(BF16)
    lo = (rest - mid.astype(F32)).astype(BF16)
    return hi, mid, lo


def _merge_stage(o0_ref, o1_ref, o2_ref, l0_ref, l1_ref, l2_ref, q_ref, kb_ref, vb_ref, o_ref):
    tm = o_ref.shape[0]
    group_refs = ((o0_ref, l0_ref), (o1_ref, l1_ref), (o2_ref, l2_ref))
    outs, lses = [None] * N_SWA_GROUPS, [None] * N_SWA_GROUPS

    def token_order(g):
        o_g, l_g = group_refs[g]
        dil = SWA_PATTERN[g][1]
        o = o_g[...].reshape(tm, GROUP_WIDTH)
        l = l_g[...].reshape(tm, HEAD_DIM)
        if dil > 1:
            inv = _residue_major_perm(PERM_BLOCK, dil, transpose=True)
            l3 = _split3(l)
            o_nat, l_nat = [], []
            for s in range(0, tm, PERM_BLOCK):
                o_nat.append(jnp.dot(inv, o[s:s + PERM_BLOCK, :], preferred_element_type=F32))
                l_nat.append(sum(jnp.dot(inv, t[s:s + PERM_BLOCK, :], preferred_element_type=F32) for t in l3))
            o = jnp.concatenate(o_nat, axis=0)
            l = jnp.concatenate(l_nat, axis=0)
        outs[g] = o.astype(F32)
        lses[g] = l

    def merge_head(h):
        lo, hi = h * HEAD_DIM, (h + 1) * HEAD_DIM
        ls = [l[:, h:h + 1] for l in lses]
        mx = jnp.maximum(jnp.maximum(ls[0], ls[1]), ls[2])
        es = [jnp.exp(l - mx) for l in ls]
        tot = es[0] + es[1] + es[2]
        for g in range(N_SWA_GROUPS):
            alpha = es[g] / tot
            o_ref[:, g * GROUP_WIDTH + lo:g * GROUP_WIDTH + hi] = (outs[g][:, lo:hi] * alpha).astype(o_ref.dtype)

    return ([functools.partial(token_order, g) for g in range(N_SWA_GROUPS)]
            + [functools.partial(merge_head, h) for h in range(HEADS_PER_GROUP)]
            + _mem_attention_parts(q_ref, kb_ref, vb_ref, o_ref, MIXER_WIDTH))


def _swa_merge_out(outs, lses, zb, mem_kv, layer, w, x, g, *, rows_per_batch):
    tm = ROW_TILE
    tiles_per_batch = rows_per_batch // tm

    def group_tile(width, dil):
        sub = _sub_block(dil)
        return lambda tile: pl.BlockSpec(
            (None, tm // sub, dil, sub // dil, width),
            lambda s: (tile(s) // tiles_per_batch, tile(s) % tiles_per_batch, 0, 0, 0))

    specs = ([group_tile(GROUP_WIDTH, dil) for _, dil in SWA_PATTERN]
             + [group_tile(HEAD_DIM, dil) for _, dil in SWA_PATTERN]
             + [lambda tile: pl.BlockSpec((None, tm, MEM_WIDTH), lambda s: (3 * N_SWA_GROUPS, tile(s), 0))])
    return _mix_out(_merge_stage, (*outs, *lses, zb), specs, [], mem_kv, layer, w, x, g,
                    tm=tm, rows_per_batch=rows_per_batch, name="swa_merge_out")


def _sample_mem_attention(q_row, kv_ref, o_ref, col0):
    for h in range(N_MEM_HEADS):
        lo, hi = h * HEAD_DIM, (h + 1) * HEAD_DIM
        q = q_row[:, lo:hi]
        k = kv_ref[:, 0, h, :]
        v = kv_ref[:, 1, h, :]
        s = jnp.sum(k * q, axis=1, keepdims=True) * ATTN_SCALE
        m = jnp.max(s, axis=0, keepdims=True)
        p = jnp.exp(s - m)
        den = jnp.sum(p, axis=0, keepdims=True)
        o_ref[:, col0 + lo:col0 + hi] = jnp.sum(p * v, axis=0, keepdims=True) / den


def _sample_mix_a_kernel(z_ref, gv_ref, w0_ref, b0_ref, kv_ref, o_ref, vrow_ref):
    u = z_ref[:, 0:MIXER_WIDTH]
    v = _rms(z_ref[:, MIXER_WIDTH:2 * MIXER_WIDTH], gv_ref[...])
    vrow_ref[...] = v
    o_ref[:, 0:MIXER_WIDTH] = u * (w0_ref[...] * v + b0_ref[...])
    _sample_mem_attention(z_ref[:, 2 * MIXER_WIDTH:2 * MIXER_WIDTH + MEM_WIDTH], kv_ref, o_ref, MIXER_WIDTH)


def _sample_mix_a(z, g_v, w_s, b_s, mem_kv, layer):
    bd = mem_kv.shape[1]
    w0 = jnp.repeat(w_s[:, 0, 0], GROUP_DIM_A).reshape(1, MIXER_WIDTH)
    b0 = jnp.repeat(b_s[:, 0], GROUP_DIM_A).reshape(1, MIXER_WIDTH)
    width = z.shape[1]
    vec = lambda i: (0, 0)
    return pl.pallas_call(
        _sample_mix_a_kernel,
        grid=(bd,),
        in_specs=[
            pl.BlockSpec((None, 1, width), lambda i: (i, 0, 0)),
            pl.BlockSpec((1, MIXER_WIDTH), vec),
            pl.BlockSpec((1, MIXER_WIDTH), vec),
            pl.BlockSpec((1, MIXER_WIDTH), vec),
            _mem_kv_spec(layer, lambda i: i),
        ],
        out_specs=[
            pl.BlockSpec((None, 1, D_MODEL), lambda i: (i, 0, 0)),
            pl.BlockSpec((None, 1, MIXER_WIDTH), lambda i: (i, 0, 0)),
        ],
        out_shape=[
            jax.ShapeDtypeStruct((bd, 1, D_MODEL), F32),
            jax.ShapeDtypeStruct((bd, 1, MIXER_WIDTH), F32),
        ],
        compiler_params=_params("parallel"),
        name="sample_mix_a",
    )(z[:bd].reshape(bd, 1, width), g_v.reshape(1, MIXER_WIDTH), w0, b0, mem_kv)


def _sample_mix_b_kernel(z_ref, c0_ref, c1_ref, c2_ref, bcol_ref, bnew_ref, kv_ref, o_ref):
    caches = (c0_ref, c1_ref, c2_ref)
    outs = [[None] * HEADS_PER_GROUP for _ in range(N_SWA_GROUPS)]
    lses = [[None] * HEADS_PER_GROUP for _ in range(N_SWA_GROUPS)]
    for g in range(N_SWA_GROUPS):
        for h in range(HEADS_PER_GROUP):
            hd = g * HEADS_PER_GROUP + h
            lo, hi = h * HEAD_DIM, (h + 1) * HEAD_DIM
            q = z_ref[:, hd * HEAD_DIM:(hd + 1) * HEAD_DIM]
            k_new = z_ref[:, MIXER_WIDTH + hd * HEAD_DIM:MIXER_WIDTH + (hd + 1) * HEAD_DIM]
            v_new = z_ref[:, 2 * MIXER_WIDTH + hd * HEAD_DIM:2 * MIXER_WIDTH + (hd + 1) * HEAD_DIM]
            kc = caches[g][:, 0, h, :]
            vc = caches[g][:, 1, h, :]
            s_c = jnp.sum(kc * q, axis=1, keepdims=True) * ATTN_SCALE + bcol_ref[g][:, h:h + 1]
            s_n = jnp.sum(k_new * q, axis=1, keepdims=True) * ATTN_SCALE + bnew_ref[:, hd:hd + 1]
            m = jnp.maximum(jnp.max(s_c, axis=0, keepdims=True), s_n)
            p_c = jnp.exp(s_c - m)
            p_n = jnp.exp(s_n - m)
            den = jnp.sum(p_c, axis=0, keepdims=True) + p_n
            outs[g][h] = (jnp.sum(p_c * vc, axis=0, keepdims=True) + p_n * v_new) / den
            lses[g][h] = m + jnp.log(den)
    for h in range(HEADS_PER_GROUP):
        ls = [lses[g][h] for g in range(N_SWA_GROUPS)]
        mx = jnp.maximum(jnp.maximum(ls[0], ls[1]), ls[2])
        es = [jnp.exp(l - mx) for l in ls]
        tot = es[0] + es[1] + es[2]
        for g in range(N_SWA_GROUPS):
            c0 = g * GROUP_WIDTH + h * HEAD_DIM
            o_ref[:, c0:c0 + HEAD_DIM] = outs[g][h] * (es[g] / tot)
    _sample_mem_attention(z_ref[:, 3 * MIXER_WIDTH:3 * MIXER_WIDTH + MEM_WIDTH], kv_ref, o_ref, MIXER_WIDTH)


def _sample_mix_b(z, win_caches, swa_layer, bias_groups, mem_kv, layer):
    bd = mem_kv.shape[1]
    width = z.shape[1]
    cache_views, cache_specs = [], []
    for g, (win, dil) in enumerate(SWA_PATTERN):
        c = win_caches[g]
        cache_views.append(c.reshape(c.shape[0], bd, win // dil, dil, 2, HEADS_PER_GROUP, HEAD_DIM))
        cache_specs.append(pl.BlockSpec((None, None, N_BACK, None, 2, HEADS_PER_GROUP, HEAD_DIM),
                                        lambda i: (swa_layer, i, 0, 0, 0, 0, 0)))
    bcol = jnp.stack([bg[:, N_BACK:0:-1].T for bg in bias_groups], axis=0)
    bnew = jnp.concatenate([bg[:, 0] for bg in bias_groups])
    bnew = jnp.pad(bnew, (0, HEAD_DIM - bnew.shape[0])).reshape(1, HEAD_DIM)
    return pl.pallas_call(
        _sample_mix_b_kernel,
        grid=(bd,),
        in_specs=[pl.BlockSpec((None, 1, width), lambda i: (i, 0, 0))] + cache_specs + [
            pl.BlockSpec((N_SWA_GROUPS, N_BACK, HEADS_PER_GROUP), lambda i: (0, 0, 0)),
            pl.BlockSpec((1, HEAD_DIM), lambda i: (0, 0)),
            _mem_kv_spec(layer, lambda i: i),
        ],
        out_specs=pl.BlockSpec((None, 1, D_MODEL), lambda i: (i, 0, 0)),
        out_shape=jax.ShapeDtypeStruct((bd, 1, D_MODEL), F32),
        compiler_params=_params("parallel"),
        name="sample_mix_b",
    )(z[:bd].reshape(bd, 1, width), *cache_views, bcol, bnew, mem_kv)


def _t5_bucket(dist):
    nf = jnp.maximum(dist, MAX_EXACT).astype(F32)
    large = MAX_EXACT + (jnp.log(nf / MAX_EXACT) / math.log(MAX_DISTANCE / MAX_EXACT)
                         * (N_BUCKETS - MAX_EXACT)).astype(jnp.int32)
    large = jnp.minimum(large, N_BUCKETS - 1)
    return jnp.where(dist < MAX_EXACT, dist, large)


def _group_bias(rel_bias, g, dil):
    dist = jnp.arange(N_BACK + 1, dtype=jnp.int32) * dil
    b = rel_bias[_t5_bucket(dist)][:, g * HEADS_PER_GROUP:(g + 1) * HEADS_PER_GROUP]
    return b.T.astype(F32)


def _band_tables_kernel(b_ref, o_ref):
    n, rows, width = o_ref.shape
    for x in range(n):
        row = jnp.broadcast_to(b_ref[x:x + 1, :], (rows, width))
        o_ref[x] = pltpu.roll(row, 0, 1, stride=1, stride_axis=0)


def _band_tables(bias_groups):
    width = 2 * N_BACK
    rows = []
    for bias_j in bias_groups:
        masked = jnp.full((HEADS_PER_GROUP, N_BACK - 1), NEG_INF, F32)
        rows.append(jnp.concatenate([bias_j[:, :1], masked, bias_j[:, N_BACK:0:-1]], axis=1))
        rows.append(jnp.concatenate([bias_j[:, ::-1], masked], axis=1))
    base = jnp.stack(rows, axis=0).reshape(-1, width)
    tabs = pl.pallas_call(
        _band_tables_kernel,
        out_shape=jax.ShapeDtypeStruct((base.shape[0], N_BACK, width), F32),
        name="band_tables",
    )(base)
    return tabs.reshape(N_SWA_GROUPS, 2, HEADS_PER_GROUP, N_BACK, width)


def _kv_tail_kernel(k_ref, v_ref, o_ref, *, dil):
    rows = k_ref.shape[0]
    for kv, ref in enumerate((k_ref, v_ref)):
        x = ref[...]
        if dil > 1:
            inv = _residue_major_perm(PERM_BLOCK, dil, transpose=True)
            x = jnp.concatenate([jnp.dot(inv, x[s:s + PERM_BLOCK, :], preferred_element_type=F32)
                                 for s in range(0, rows, PERM_BLOCK)], axis=0)
        x = x.astype(F32)
        for h in range(HEADS_PER_GROUP):
            o_ref[:, kv, h, :] = x[:, h * HEAD_DIM:(h + 1) * HEAD_DIM]


def _kv_tail(zb, g, *, batch, seq):
    win, dil = SWA_PATTERN[g]
    rows = min(win, PERM_BLOCK)
    first = (seq - win) // rows
    per_batch = seq // rows

    def tile(t):
        return pl.BlockSpec((None, rows, GROUP_WIDTH), lambda b, s: (t, b * per_batch + first + s, 0))

    return pl.pallas_call(
        functools.partial(_kv_tail_kernel, dil=dil),
        grid=(batch, win // rows),
        in_specs=[tile(N_SWA_GROUPS + g), tile(2 * N_SWA_GROUPS + g)],
        out_specs=pl.BlockSpec((None, rows, 2, HEADS_PER_GROUP, HEAD_DIM), lambda b, s: (b, s, 0, 0, 0)),
        out_shape=jax.ShapeDtypeStruct((batch, win, 2, HEADS_PER_GROUP, HEAD_DIM), F32),
        compiler_params=_params("parallel", "parallel"),
        name=f"kv_tail{g}",
    )(zb, zb)


def kernel(x_prompt, x_sample, mem_prompt, cache_mem_kv, cache_win128_kv, cache_win512_kv, cache_win2048_kv, rel_bias, norm_mix_pre, norm_mix_post, norm_ffn_pre, norm_ffn_post, norm_mem, w_mem_kv, w_in_a, norm_v_a, w_spatial_a, b_spatial_a, w_in_b, w_out, w_ffn_up, w_ffn_down):
    batch, seq, _ = x_prompt.shape
    bd = x_sample.shape[0]
    depth = w_out.shape[0]
    m_p = batch * seq
    win_caches = (cache_win128_kv, cache_win512_kv, cache_win2048_kv)

    bias_groups = [_group_bias(rel_bias, g, dil) for g, (_, dil) in enumerate(SWA_PATTERN)]
    band_tables = _band_tables(bias_groups)

    yp = x_prompt.reshape(m_p, D_MODEL)
    ys = jnp.pad(x_sample.reshape(bd, D_MODEL), ((0, SAMPLE_PAD - bd), (0, 0)))
    mem_rows = mem_prompt.reshape(batch * N_MEM, D_MODEL)

    mem_kv_p = _mem_kv(mem_rows, norm_mem, w_mem_kv, batch=batch)
    chunk_v_s = []
    win_p = [[] for _ in SWA_PATTERN]
    win_s = [[] for _ in SWA_PATTERN]
    for i in range(depth):
        li = i // 2
        if i % 2 == 0:
            zp, zs = _in_proj_a(yp, ys, norm_mix_pre[i], w_in_a, li, tm=ROW_TILE, tn=CAST_TILE,
                                gelu_cols=2 * MIXER_WIDTH)
            mix_s, v_rows = _sample_mix_a(zs, norm_v_a[li], w_spatial_a[li], b_spatial_a[li], cache_mem_kv, i)
            chunk_v_s.append(v_rows)
            ys, w_o = _sample_out_proj(mix_s, w_out, i, ys, norm_mix_post[i])
            yp = _gmlp_mix_out(zp, norm_v_a[li], w_spatial_a[li], b_spatial_a[li], mem_kv_p, i,
                               w_o, yp, norm_mix_post[i], tm=ROW_TILE, rows_per_batch=seq)
        else:
            zb, zs = _in_proj_b(yp, ys, norm_mix_pre[i], w_in_b, li, tm=ROW_TILE)
            outs, lses = [], []
            for g, (win, dil) in enumerate(SWA_PATTERN):
                o, lse = _swa_group(zb, band_tables, g, batch=batch, seq=seq)
                outs.append(o)
                lses.append(lse)
                win_p[g].append(_kv_tail(zb, g, batch=batch, seq=seq))
                kv_new = zs[:bd, MIXER_WIDTH:3 * MIXER_WIDTH]
                kv_new = kv_new.reshape(bd, 1, 2, N_SWA_GROUPS, HEADS_PER_GROUP, HEAD_DIM)[:, :, :, g]
                win_s[g].append(kv_new)
            mix_s = _sample_mix_b(zs, win_caches, li, bias_groups, cache_mem_kv, i)
            ys, w_o = _sample_out_proj(mix_s, w_out, i, ys, norm_mix_post[i])
            yp = _swa_merge_out(outs, lses, zb, mem_kv_p, i, w_o, yp, norm_mix_post[i], rows_per_batch=seq)
        yp, ys, w_ffn = _ffn_head(yp, ys, norm_ffn_pre[i], norm_ffn_post[i], w_ffn_up, w_ffn_down, i,
                                  tm=FFN_ROW_TILE, tf=HEAD_FF_TILE)
        yp = _ffn(yp, norm_ffn_pre[i], norm_ffn_post[i], w_ffn, tm=FFN_ROW_TILE, tf=FF_TILE, first_tile=1)

    return (
        yp.reshape(batch, seq, D_MODEL),
        ys[:bd].reshape(bd, 1, D_MODEL),
        mem_kv_p,
        jnp.stack(chunk_v_s, axis=0),
        jnp.stack(win_p[0], axis=0),
        jnp.stack(win_p[1], axis=0),
        jnp.stack(win_p[2], axis=0),
        jnp.stack(win_s[0], axis=0),
        jnp.stack(win_s[1], axis=0),
        jnp.stack(win_s[2], axis=0),
    )
```

```python
import functools
import math

import jax
import jax.numpy as jnp
from jax import lax
from jax.experimental import pallas as pl
from jax.experimental.pallas import tpu as pltpu

F32 = jnp.float32
BF16 = jnp.bfloat16

D_MODEL = 2048
HEAD_DIM = 128
N_MEM = 256
N_MEM_HEADS = 4
MEM_WIDTH = N_MEM_HEADS * HEAD_DIM
MIXER_WIDTH = D_MODEL - MEM_WIDTH
CHUNK = 128
N_GROUPS_A = 4
GROUP_DIM_A = MIXER_WIDTH // N_GROUPS_A
SWA_PATTERN = ((128, 1), (512, 4), (2048, 16))
N_SWA_GROUPS = len(SWA_PATTERN)
HEADS_PER_GROUP = 4
GROUP_WIDTH = HEADS_PER_GROUP * HEAD_DIM
N_BACK = 128
N_BUCKETS = 32
MAX_EXACT = N_BUCKETS // 2
MAX_DISTANCE = 2048
D_FF = 5632
EPS = 1e-6
NEG_INF = -1e30
ATTN_SCALE = HEAD_DIM ** -0.5
SAMPLE_PAD = 16
PERM_BLOCK = 256
OUT_CHUNK = 256

ROW_TILE = 512
FFN_ROW_TILE = 1024
FF_TILE = 512
HEAD_FF_TILE = 256
CAST_TILE = 512

V7X_VMEM_BYTES = 64 * 1024 * 1024
VMEM_LIMIT = V7X_VMEM_BYTES - 8 * 1024 * 1024
FFN_VMEM_LIMIT = V7X_VMEM_BYTES - 2 * 1024 * 1024


def _params(*sem, vmem_limit=VMEM_LIMIT):
    return pltpu.CompilerParams(dimension_semantics=sem, vmem_limit_bytes=vmem_limit)


def _gelu(x):
    return 0.5 * x * (1.0 + jnp.tanh(0.7978845608028654 * (x + 0.044715 * (x * x * x))))


def _rms(x, g):
    return x * lax.rsqrt(jnp.mean(x * x, axis=-1, keepdims=True) + EPS) * g


def _log2(n):
    assert n & (n - 1) == 0
    return n.bit_length() - 1


def _residue_major_perm(tm, dil, transpose=False):
    n = tm // dil
    row = lax.broadcasted_iota(jnp.int32, (tm, tm), 0)
    col = lax.broadcasted_iota(jnp.int32, (tm, tm), 1)
    dst, src = (col, row) if transpose else (row, col)
    want = lax.shift_left(jnp.bitwise_and(dst, n - 1), _log2(dil)) + lax.shift_right_logical(dst, _log2(n))
    return (src == want).astype(BF16)


def _in_proj_a_kernel(x_ref, xs_ref, g_ref, w_ref, zp_ref, zs_ref, wb_ref, xn_ref, *, n_col, gelu_cols):
    tm = x_ref.shape[0]
    tn = w_ref.shape[1]
    s = pl.program_id(0)

    def prompt_cols(acc, c0):
        col = lax.broadcasted_iota(jnp.int32, acc.shape, 1) + c0
        return jnp.where(col < gelu_cols, _gelu(acc), acc * ATTN_SCALE).astype(zp_ref.dtype)

    @pl.when(s == 0)
    def _():
        xn_ref[:tm, :] = _rms(x_ref[...], g_ref[...]).astype(BF16)
        xn_ref[tm:, :] = _rms(xs_ref[...], g_ref[...]).astype(BF16)

    for j in range(n_col):
        @pl.when(s == j)
        def _(j=j):
            cols = slice(j * tn, (j + 1) * tn)
            w = w_ref[...].astype(BF16)
            wb_ref[:, cols] = w
            acc = jnp.dot(xn_ref[...], w, preferred_element_type=F32)
            zp_ref[:, cols] = prompt_cols(acc[:tm, :], j * tn)
            sample = acc[tm:, :]
            zs_ref[:, cols] = _gelu(sample) if (j + 1) * tn <= gelu_cols else sample

    @pl.when(s >= n_col)
    def _():
        xn = _rms(x_ref[...], g_ref[...]).astype(BF16)
        acc = jnp.dot(xn, wb_ref[...], preferred_element_type=F32)
        zp_ref[...] = prompt_cols(acc, 0)


def _in_proj_a(x, xs, g, w, layer, *, tm, tn, gelu_cols):
    m, k = x.shape
    ms = xs.shape[0]
    n = w.shape[2]
    n_col, n_row = n // tn, m // tm
    assert gelu_cols % tn == 0
    row = lambda s: jnp.maximum(s - (n_col - 1), 0)
    return pl.pallas_call(
        functools.partial(_in_proj_a_kernel, n_col=n_col, gelu_cols=gelu_cols),
        grid=(n_col + n_row - 1,),
        in_specs=[
            pl.BlockSpec((tm, k), lambda s: (row(s), 0)),
            pl.BlockSpec((ms, k), lambda s: (0, 0)),
            pl.BlockSpec((1, k), lambda s: (0, 0)),
            pl.BlockSpec((None, k, tn), lambda s: (layer, 0, jnp.minimum(s, n_col - 1))),
        ],
        out_specs=[
            pl.BlockSpec((tm, n), lambda s: (row(s), 0)),
            pl.BlockSpec((ms, n), lambda s: (0, 0)),
        ],
        out_shape=[
            jax.ShapeDtypeStruct((m, n), BF16),
            jax.ShapeDtypeStruct((ms, n), F32),
        ],
        scratch_shapes=[pltpu.VMEM((k, n), BF16), pltpu.VMEM((tm + ms, k), BF16)],
        compiler_params=_params("arbitrary"),
        name="in_proj_a",
    )(x, xs, g.reshape(1, k), w)


def _mem_kv_kernel(x_ref, g_ref, w_ref, o_ref):
    xn = _rms(x_ref[...], g_ref[...]).astype(BF16)
    acc = jnp.dot(xn, w_ref[...].astype(BF16), preferred_element_type=F32)
    for kv in range(2):
        for h in range(N_MEM_HEADS):
            c0 = (kv * N_MEM_HEADS + h) * HEAD_DIM
            o_ref[:, kv, h, :] = acc[:, c0:c0 + HEAD_DIM]


def _mem_kv(mem_rows, g, w, *, batch):
    m, k = mem_rows.shape
    layers = w.shape[0]
    out = pl.pallas_call(
        _mem_kv_kernel,
        grid=(layers,),
        in_specs=[
            pl.BlockSpec((m, k), lambda l: (0, 0)),
            pl.BlockSpec((None, 1, k), lambda l: (l, 0, 0)),
            pl.BlockSpec((None, k, 2 * MEM_WIDTH), lambda l: (l, 0, 0)),
        ],
        out_specs=pl.BlockSpec((None, m, 2, N_MEM_HEADS, HEAD_DIM), lambda l: (l, 0, 0, 0, 0)),
        out_shape=jax.ShapeDtypeStruct((layers, m, 2, N_MEM_HEADS, HEAD_DIM), F32),
        compiler_params=_params("parallel"),
        name="mem_kv",
    )(mem_rows, g.reshape(layers, 1, k), w)
    return out.reshape(layers, batch, m // batch, 2, N_MEM_HEADS, HEAD_DIM)


def _in_proj_b_kernel(x_ref, xs_ref, g_ref, w_ref, o_ref, zs_ref, wb_ref, xn_ref):
    tm = x_ref.shape[0]
    n_tiles = o_ref.shape[0]
    n_qkv = 3 * N_SWA_GROUPS
    s = pl.program_id(0)

    def row_orders():
        xn = _rms(x_ref[...], g_ref[...]).astype(BF16)
        xn_ref[0, :tm, :] = xn
        for g in range(1, N_SWA_GROUPS):
            perm = _residue_major_perm(PERM_BLOCK, SWA_PATTERN[g][1])
            for r in range(0, tm, PERM_BLOCK):
                xn_ref[g, r:r + PERM_BLOCK, :] = jnp.dot(
                    perm, xn[r:r + PERM_BLOCK, :], preferred_element_type=F32).astype(BF16)

    def prompt_tile(t, acc):
        if t < N_SWA_GROUPS or t == n_qkv:
            acc = acc * ATTN_SCALE
        o_ref[t] = acc.astype(o_ref.dtype)

    src = lambda t: t % N_SWA_GROUPS if t < n_qkv else 0

    @pl.when(s == 0)
    def _():
        row_orders()
        xs = _rms(xs_ref[...], g_ref[...]).astype(BF16)
        for g in range(N_SWA_GROUPS):
            xn_ref[g, tm:, :] = xs

    for t in range(n_tiles):
        @pl.when(s == t)
        def _(t=t):
            cols = slice(t * GROUP_WIDTH, (t + 1) * GROUP_WIDTH)
            w = w_ref[...].astype(BF16)
            wb_ref[:, cols] = w
            acc = jnp.dot(xn_ref[src(t)], w, preferred_element_type=F32)
            prompt_tile(t, acc[:tm, :])
            zs_ref[:, cols] = acc[tm:, :]

    @pl.when(s >= n_tiles)
    def _():
        row_orders()
        for t in range(n_tiles):
            w = wb_ref[:, t * GROUP_WIDTH:(t + 1) * GROUP_WIDTH]
            prompt_tile(t, jnp.dot(xn_ref[src(t), :tm, :], w, preferred_element_type=F32))


def _in_proj_b(x, xs, g, w, layer, *, tm):
    m, k = x.shape
    ms = xs.shape[0]
    n = w.shape[2]
    n_tiles = n // GROUP_WIDTH
    row = lambda s: jnp.maximum(s - (n_tiles - 1), 0)
    return pl.pallas_call(
        _in_proj_b_kernel,
        grid=(n_tiles + m // tm - 1,),
        in_specs=[
            pl.BlockSpec((tm, k), lambda s: (row(s), 0)),
            pl.BlockSpec((ms, k), lambda s: (0, 0)),
            pl.BlockSpec((1, k), lambda s: (0, 0)),
            pl.BlockSpec((None, k, GROUP_WIDTH), lambda s: (layer, 0, jnp.minimum(s, n_tiles - 1))),
        ],
        out_specs=[
            pl.BlockSpec((n_tiles, tm, GROUP_WIDTH), lambda s: (0, row(s), 0)),
            pl.BlockSpec((ms, n), lambda s: (0, 0)),
        ],
        out_shape=[
            jax.ShapeDtypeStruct((n_tiles, m, GROUP_WIDTH), BF16),
            jax.ShapeDtypeStruct((ms, n), F32),
        ],
        scratch_shapes=[pltpu.VMEM((k, n), BF16), pltpu.VMEM((N_SWA_GROUPS, tm + ms, k), BF16)],
        compiler_params=_params("arbitrary"),
        name="in_proj_b",
    )(x, xs, g.reshape(1, k), w)


MEM_KV_SCRATCH = [pltpu.VMEM((N_MEM_HEADS, N_MEM, HEAD_DIM), BF16),
                  pltpu.VMEM((N_MEM_HEADS, N_MEM, 2 * HEAD_DIM), BF16)]


def _prepare_mem_kv(kv_ref, kb_ref, vb_ref):
    for h in range(N_MEM_HEADS):
        kb_ref[h] = kv_ref[:, 0, h, :].astype(BF16)
        vb_ref[h, :, :HEAD_DIM] = kv_ref[:, 1, h, :].astype(BF16)
        vb_ref[h, :, HEAD_DIM:] = jnp.ones((N_MEM, HEAD_DIM), BF16)


def _mem_attention_parts(q_ref, kb_ref, vb_ref, o_ref, col0):
    def head(h):
        lo, hi = h * HEAD_DIM, (h + 1) * HEAD_DIM
        s = lax.dot_general(q_ref[:, lo:hi], kb_ref[h], (((1,), (1,)), ((), ())), preferred_element_type=F32)
        m = jnp.max(s, axis=1, keepdims=True)
        p = jnp.exp(s - m).astype(BF16)
        ov = jnp.dot(p, vb_ref[h], preferred_element_type=F32)
        o_ref[:, col0 + lo:col0 + hi] = (ov[:, :HEAD_DIM] / ov[:, HEAD_DIM:]).astype(o_ref.dtype)

    return [functools.partial(head, h) for h in range(N_MEM_HEADS)]


def _gmlp_stage(u_ref, v_ref, q_ref, gv_ref, ws_ref, bs_ref, kb_ref, vb_ref, vn_ref, o_ref):
    tm = u_ref.shape[0]

    def norm_v():
        vn_ref[...] = _rms(v_ref[...].astype(F32), gv_ref[...]).astype(BF16)

    def group(g):
        row = lax.broadcasted_iota(jnp.int32, (CHUNK, CHUNK), 0)
        col = lax.broadcasted_iota(jnp.int32, (CHUNK, CHUNK), 1)
        w = jnp.where(row >= col, ws_ref[g], 0.0).astype(BF16)
        b = bs_ref[:, g:g + 1]
        c0, c1 = g * GROUP_DIM_A, (g + 1) * GROUP_DIM_A
        for c in range(tm // CHUNK):
            r0, r1 = c * CHUNK, (c + 1) * CHUNK
            s = jnp.dot(w, vn_ref[r0:r1, c0:c1], preferred_element_type=F32) + b
            o_ref[r0:r1, c0:c1] = (u_ref[r0:r1, c0:c1].astype(F32) * s).astype(o_ref.dtype)

    return ([norm_v] + [functools.partial(group, g) for g in range(N_GROUPS_A)]
            + _mem_attention_parts(q_ref, kb_ref, vb_ref, o_ref, MIXER_WIDTH))


def _mem_kv_spec(layer, batch_of):
    return pl.BlockSpec((None, None, N_MEM, 2, N_MEM_HEADS, HEAD_DIM),
                        lambda i: (layer, batch_of(i), 0, 0, 0, 0))


def _mix_out_kernel(*refs, stage, n_in, n_tiles, tiles_per_batch):
    mix_in = refs[:n_in]
    kv_ref, w_ref, x_ref, g_ref, o_ref, buf0_ref, buf1_ref, acc_ref, kb_ref, vb_ref = refs[n_in:n_in + 10]
    extra = refs[n_in + 10:]
    s = pl.program_id(0)

    @pl.when(s == 0)
    def _():
        buf1_ref[...] = jnp.zeros_like(buf1_ref)

    @pl.when(jnp.minimum(s, n_tiles - 1) % tiles_per_batch == 0)
    def _():
        _prepare_mem_kv(kv_ref, kb_ref, vb_ref)

    def step(dst_ref, src_ref):
        parts = stage(*mix_in, kb_ref, vb_ref, *extra, dst_ref)
        n_chunks = D_MODEL // OUT_CHUNK
        for c in range(n_chunks):
            cols = slice(c * OUT_CHUNK, (c + 1) * OUT_CHUNK)
            acc_ref[:, cols] = jnp.dot(src_ref[...], w_ref[:, cols], preferred_element_type=F32)
            for part in parts[c * len(parts) // n_chunks:(c + 1) * len(parts) // n_chunks]:
                part()
        o_ref[...] = x_ref[...] + _rms(acc_ref[...], g_ref[...])

    @pl.when(s % 2 == 0)
    def _():
        step(buf0_ref, buf1_ref)

    @pl.when(s % 2 == 1)
    def _():
        step(buf1_ref, buf0_ref)


def _mix_out(stage, mix_inputs, mix_specs, extra_scratch, mem_kv, layer, w, x, g, *, tm, rows_per_batch, name):
    m = x.shape[0]
    n_tiles = m // tm
    tiles_per_batch = rows_per_batch // tm
    mix_tile = lambda s: jnp.minimum(s, n_tiles - 1)
    out_tile = lambda s: jnp.maximum(s - 1, 0)
    row_spec = pl.BlockSpec((tm, D_MODEL), lambda s: (out_tile(s), 0))
    return pl.pallas_call(
        functools.partial(_mix_out_kernel, stage=stage, n_in=len(mix_inputs), n_tiles=n_tiles,
                          tiles_per_batch=tiles_per_batch),
        grid=(n_tiles + 1,),
        in_specs=[spec(mix_tile) for spec in mix_specs] + [
            _mem_kv_spec(layer, lambda s: mix_tile(s) // tiles_per_batch),
            pl.BlockSpec((D_MODEL, D_MODEL), lambda s: (0, 0), pipeline_mode=pl.Buffered(1)),
            row_spec,
            pl.BlockSpec((1, D_MODEL), lambda s: (0, 0)),
        ],
        out_specs=row_spec,
        out_shape=jax.ShapeDtypeStruct((m, D_MODEL), F32),
        scratch_shapes=[pltpu.VMEM((tm, D_MODEL), BF16), pltpu.VMEM((tm, D_MODEL), BF16),
                        pltpu.VMEM((tm, D_MODEL), F32)] + MEM_KV_SCRATCH + list(extra_scratch),
        compiler_params=_params("arbitrary"),
        name=name,
    )(*mix_inputs, mem_kv, w, x, g.reshape(1, D_MODEL))


def _gmlp_mix_out(zact, g_v, w_s, b_s, mem_kv, layer, w, x, g, *, tm, rows_per_batch):
    const = lambda shape: (lambda tile: pl.BlockSpec(shape, lambda s: (0,) * len(shape)))
    specs = [
        lambda tile: pl.BlockSpec((tm, MIXER_WIDTH), lambda s: (tile(s), 0)),
        lambda tile: pl.BlockSpec((tm, MIXER_WIDTH), lambda s: (tile(s), 1)),
        lambda tile: pl.BlockSpec((tm, MEM_WIDTH), lambda s: (tile(s), 2 * MIXER_WIDTH // MEM_WIDTH)),
        const((1, MIXER_WIDTH)),
        const((N_GROUPS_A, CHUNK, CHUNK)),
        const((CHUNK, N_GROUPS_A)),
    ]
    return _mix_out(_gmlp_stage, (zact, zact, zact, g_v.reshape(1, MIXER_WIDTH), w_s, b_s.T), specs,
                    [pltpu.VMEM((tm, MIXER_WIDTH), BF16)], mem_kv, layer, w, x, g,
                    tm=tm, rows_per_batch=rows_per_batch, name="gmlp_mix_out")


def _out_proj_cast_kernel(mix_ref, w_ref, x_ref, g_ref, o_ref, wb_ref, acc_ref):
    j = pl.program_id(0)
    w = w_ref[...].astype(BF16)
    wb_ref[...] = w
    acc_ref[j] = jnp.dot(mix_ref[...], w, preferred_element_type=F32)

    @pl.when(j == pl.num_programs(0) - 1)
    def _():
        o = jnp.concatenate([acc_ref[t] for t in range(acc_ref.shape[0])], axis=1)
        o_ref[...] = x_ref[...] + _rms(o, g_ref[...])


def _sample_out_proj(mix, w, layer, x, g):
    bd = mix.shape[0]
    mix = jnp.pad(mix.reshape(bd, D_MODEL), ((0, x.shape[0] - bd), (0, 0))).astype(BF16)
    return _out_proj_cast(mix, w, layer, x, g, tn=CAST_TILE)


def _out_proj_cast(mix, w, layer, x, g, *, tn):
    m = x.shape[0]
    n_tiles = D_MODEL // tn
    return pl.pallas_call(
        _out_proj_cast_kernel,
        grid=(n_tiles,),
        in_specs=[
            pl.BlockSpec((m, D_MODEL), lambda j: (0, 0)),
            pl.BlockSpec((None, D_MODEL, tn), lambda j: (layer, 0, j)),
            pl.BlockSpec((m, D_MODEL), lambda j: (0, 0)),
            pl.BlockSpec((1, D_MODEL), lambda j: (0, 0)),
        ],
        out_specs=[
            pl.BlockSpec((m, D_MODEL), lambda j: (0, 0)),
            pl.BlockSpec((D_MODEL, tn), lambda j: (0, j)),
        ],
        out_shape=[
            jax.ShapeDtypeStruct((m, D_MODEL), F32),
            jax.ShapeDtypeStruct((D_MODEL, D_MODEL), BF16),
        ],
        scratch_shapes=[pltpu.VMEM((n_tiles, m, tn), F32)],
        compiler_params=_params("arbitrary"),
        name="out_proj_cast",
    )(mix, w, x, g.reshape(1, D_MODEL))


def _ffn_kernel(x_ref, gpre_ref, gpost_ref, wg_ref, wl_ref, wd_ref, o_ref, xn_ref):
    f = pl.program_id(1)

    @pl.when(f == 0)
    def _():
        xn_ref[...] = _rms(x_ref[...], gpre_ref[...]).astype(BF16)
        o_ref[...] = jnp.zeros_like(o_ref)

    xn = xn_ref[...]
    hg = jnp.dot(xn, wg_ref[...], preferred_element_type=F32)
    hl = jnp.dot(xn, wl_ref[...], preferred_element_type=F32)
    a = (hg * jax.nn.sigmoid(hg) * hl).astype(BF16)
    o_ref[...] += jnp.dot(a, wd_ref[...], preferred_element_type=F32)

    @pl.when(f == pl.num_programs(1) - 1)
    def _():
        o_ref[...] = x_ref[...] + _rms(o_ref[...], gpost_ref[...])


def _ffn(x, g_pre, g_post, weights, *, tm, tf, first_tile):
    m = x.shape[0]
    nf = D_FF // tf
    rows = pl.BlockSpec((tm, D_MODEL), lambda i, f: (i + first_tile, 0))
    return pl.pallas_call(
        _ffn_kernel,
        grid=(m // tm - first_tile, nf),
        in_specs=[
            rows,
            pl.BlockSpec((1, D_MODEL), lambda i, f: (0, 0)),
            pl.BlockSpec((1, D_MODEL), lambda i, f: (0, 0)),
            pl.BlockSpec((D_MODEL, tf), lambda i, f: (0, f)),
            pl.BlockSpec((D_MODEL, tf), lambda i, f: (0, f)),
            pl.BlockSpec((tf, D_MODEL), lambda i, f: (f, 0)),
        ],
        out_specs=rows,
        out_shape=jax.ShapeDtypeStruct((m, D_MODEL), F32),
        input_output_aliases={0: 0},
        scratch_shapes=[pltpu.VMEM((tm, D_MODEL), BF16)],
        compiler_params=_params("parallel", "arbitrary", vmem_limit=FFN_VMEM_LIMIT),
        name="ffn",
    )(x, g_pre.reshape(1, D_MODEL), g_post.reshape(1, D_MODEL), *weights)


def _ffn_head_kernel(x_ref, xs_ref, gpre_ref, gpost_ref, wg_ref, wl_ref, wd_ref,
                     o_ref, os_ref, wgb_ref, wlb_ref, wdb_ref, xn_ref):
    tm = x_ref.shape[0]
    f = pl.program_id(0)

    @pl.when(f == 0)
    def _():
        xn_ref[:tm, :] = _rms(x_ref[...], gpre_ref[...]).astype(BF16)
        xn_ref[tm:, :] = _rms(xs_ref[...], gpre_ref[...]).astype(BF16)
        o_ref[...] = jnp.zeros_like(o_ref)
        os_ref[...] = jnp.zeros_like(os_ref)

    wg, wl, wd = (r[...].astype(BF16) for r in (wg_ref, wl_ref, wd_ref))
    wgb_ref[...] = wg
    wlb_ref[...] = wl
    wdb_ref[...] = wd
    xn = xn_ref[...]
    hg = jnp.dot(xn, wg, preferred_element_type=F32)
    hl = jnp.dot(xn, wl, preferred_element_type=F32)
    a = (hg * jax.nn.sigmoid(hg) * hl).astype(BF16)
    part = jnp.dot(a, wd, preferred_element_type=F32)
    o_ref[...] += part[:tm, :]
    os_ref[...] += part[tm:, :]

    @pl.when(f == pl.num_programs(0) - 1)
    def _():
        o_ref[...] = x_ref[...] + _rms(o_ref[...], gpost_ref[...])
        os_ref[...] = xs_ref[...] + _rms(os_ref[...], gpost_ref[...])


def _ffn_head(x, xs, g_pre, g_post, w_up, w_down, layer, *, tm, tf):
    m = x.shape[0]
    ms = xs.shape[0]
    nf = D_FF // tf
    once = dict(pipeline_mode=pl.Buffered(1))
    head = pl.BlockSpec((tm, D_MODEL), lambda f: (0, 0), **once)
    sample = pl.BlockSpec((ms, D_MODEL), lambda f: (0, 0))
    vec = pl.BlockSpec((1, D_MODEL), lambda f: (0, 0))
    outs = pl.pallas_call(
        _ffn_head_kernel,
        grid=(nf,),
        in_specs=[
            head, sample, vec, vec,
            pl.BlockSpec((None, D_MODEL, tf), lambda f: (layer, 0, f)),
            pl.BlockSpec((None, D_MODEL, tf), lambda f: (layer, 0, nf + f)),
            pl.BlockSpec((None, tf, D_MODEL), lambda f: (layer, f, 0)),
        ],
        out_specs=[
            head, sample,
            pl.BlockSpec((D_MODEL, tf), lambda f: (0, f)),
            pl.BlockSpec((D_MODEL, tf), lambda f: (0, f)),
            pl.BlockSpec((tf, D_MODEL), lambda f: (f, 0)),
        ],
        out_shape=[
            jax.ShapeDtypeStruct((m, D_MODEL), F32),
            jax.ShapeDtypeStruct((ms, D_MODEL), F32),
            jax.ShapeDtypeStruct((D_MODEL, D_FF), BF16),
            jax.ShapeDtypeStruct((D_MODEL, D_FF), BF16),
            jax.ShapeDtypeStruct((D_FF, D_MODEL), BF16),
        ],
        input_output_aliases={0: 0},
        scratch_shapes=[pltpu.VMEM((tm + ms, D_MODEL), BF16)],
        compiler_params=_params("arbitrary"),
        name="ffn_head",
    )(x, xs, g_pre.reshape(1, D_MODEL), g_post.reshape(1, D_MODEL), w_up, w_up, w_down)
    return outs[0], outs[1], tuple(outs[2:])


def _swa_kernel(q_ref, k_ref, v_ref, tb_ref, o_ref, lse_ref):
    n_units, n_res, u, _ = q_ref.shape
    per_blk = N_BACK // u
    n_blk = n_units // per_blk
    lane = lax.broadcasted_iota(jnp.int32, (N_BACK, HEAD_DIM), 1)
    ones = jnp.ones((2 * N_BACK, HEAD_DIM), BF16)

    def rows(ref, res, unit0, n_rows, lo, hi):
        return ref[pl.ds(unit0, n_rows // u), res, :, lo:hi].reshape(n_rows, hi - lo)

    def block(res, qu, ku, table):
        n_keys = N_BACK if table == 0 else 2 * N_BACK
        lse_tile = jnp.zeros((N_BACK, HEAD_DIM), F32)
        for h in range(HEADS_PER_GROUP):
            lo, hi = h * HEAD_DIM, (h + 1) * HEAD_DIM
            q = rows(q_ref, res, qu, N_BACK, lo, hi)
            kw = rows(k_ref, res, ku, n_keys, lo, hi)
            vw = rows(v_ref, res, ku, n_keys, lo, hi)
            s = lax.dot_general(q, kw, (((1,), (1,)), ((), ())), preferred_element_type=F32)
            s = s + tb_ref[table, h][:, :n_keys]
            m = jnp.max(s, axis=1, keepdims=True)
            p = jnp.exp(s - m).astype(BF16)
            ov = jnp.dot(p, jnp.concatenate([vw, ones[:n_keys]], axis=1), preferred_element_type=F32)
            den = ov[:, HEAD_DIM:]
            o = ov[:, :HEAD_DIM] / den
            o_ref[pl.ds(qu, per_blk), res, :, lo:hi] = o.reshape(per_blk, u, HEAD_DIM).astype(o_ref.dtype)
            lse_tile = jnp.where(lane == h, m + jnp.log(den), lse_tile)
        lse_ref[pl.ds(qu, per_blk), res, :, :] = lse_tile.reshape(per_blk, u, HEAD_DIM)

    def block_at(res, n):
        block(res, n * per_blk, (n - 1) * per_blk, 1)

    n_pairs = (n_blk - 1) // 2
    for res in range(n_res):
        block(res, 0, 0, 0)

        def body(i, carry, res=res):
            block_at(res, 1 + 2 * i)
            block_at(res, 2 + 2 * i)
            return carry

        if n_pairs > 0:
            lax.fori_loop(0, n_pairs, body, 0)
        if (n_blk - 1) % 2 == 1:
            block_at(res, n_blk - 1)


def _sub_block(dil):
    return N_BACK if dil == 1 else PERM_BLOCK


def _swa_group(zb, tables, g, *, batch, seq):
    dil = SWA_PATTERN[g][1]
    sub = _sub_block(dil)
    n_units, u = seq // sub, sub // dil
    n_res = 4 if n_units * u == 2 * N_BACK else 1
    view = zb.reshape(zb.shape[0], batch, n_units, dil, u, GROUP_WIDTH)

    def rows_in(tile):
        return pl.BlockSpec((None, None, n_units, n_res, u, GROUP_WIDTH), lambda i, r: (tile, i, 0, r, 0, 0))

    def rows_out(width):
        return pl.BlockSpec((None, n_units, n_res, u, width), lambda i, r: (i, 0, r, 0, 0))

    return pl.pallas_call(
        _swa_kernel,
        grid=(batch, dil // n_res),
        in_specs=[rows_in(g), rows_in(N_SWA_GROUPS + g), rows_in(2 * N_SWA_GROUPS + g),
                  pl.BlockSpec((None, 2, HEADS_PER_GROUP, N_BACK, 2 * N_BACK), lambda i, r: (g, 0, 0, 0, 0))],
        out_specs=[rows_out(GROUP_WIDTH), rows_out(HEAD_DIM)],
        out_shape=[
            jax.ShapeDtypeStruct((batch, n_units, dil, u, GROUP_WIDTH), BF16),
            jax.ShapeDtypeStruct((batch, n_units, dil, u, HEAD_DIM), F32),
        ],
        compiler_params=_params("parallel", "parallel"),
        name=f"swa_group{g}",
    )(view, view, view, tables)


def _split3(x):
    hi = x.astype(BF16)
    rest = x - hi.astype(F32)
    mid = rest.astype(BF16)
    lo = (rest - mid.astype(F32)).astype(BF16)
    return hi, mid, lo


def _merge_stage(o0_ref, o1_ref, o2_ref, l0_ref, l1_ref, l2_ref, q_ref, kb_ref, vb_ref, o_ref):
    tm = o_ref.shape[0]
    group_refs = ((o0_ref, l0_ref), (o1_ref, l1_ref), (o2_ref, l2_ref))
    outs, lses = [None] * N_SWA_GROUPS, [None] * N_SWA_GROUPS

    def token_order(g):
        o_g, l_g = group_refs[g]
        dil = SWA_PATTERN[g][1]
        o = o_g[...].reshape(tm, GROUP_WIDTH)
        l = l_g[...].reshape(tm, HEAD_DIM)
        if dil > 1:
            inv = _residue_major_perm(PERM_BLOCK, dil, transpose=True)
            l3 = _split3(l)
            o_nat, l_nat = [], []
            for s in range(0, tm, PERM_BLOCK):
                o_nat.append(jnp.dot(inv, o[s:s + PERM_BLOCK, :], preferred_element_type=F32))
                l_nat.append(sum(jnp.dot(inv, t[s:s + PERM_BLOCK, :], preferred_element_type=F32) for t in l3))
            o = jnp.concatenate(o_nat, axis=0)
            l = jnp.concatenate(l_nat, axis=0)
        outs[g] = o.astype(F32)
        lses[g] = l

    def merge_head(h):
        lo, hi = h * HEAD_DIM, (h + 1) * HEAD_DIM
        ls = [l[:, h:h + 1] for l in lses]
        mx = jnp.maximum(jnp.maximum(ls[0], ls[1]), ls[2])
        es = [jnp.exp(l - mx) for l in ls]
        tot = es[0] + es[1] + es[2]
        for g in range(N_SWA_GROUPS):
            alpha = es[g] / tot
            o_ref[:, g * GROUP_WIDTH + lo:g * GROUP_WIDTH + hi] = (outs[g][:, lo:hi] * alpha).astype(o_ref.dtype)

    return ([functools.partial(token_order, g) for g in range(N_SWA_GROUPS)]
            + [functools.partial(merge_head, h) for h in range(HEADS_PER_GROUP)]
            + _mem_attention_parts(q_ref, kb_ref, vb_ref, o_ref, MIXER_WIDTH))


def _swa_merge_out(outs, lses, zb, mem_kv, layer, w, x, g, *, rows_per_batch):
    tm = ROW_TILE
    tiles_per_batch = rows_per_batch // tm

    def group_tile(width, dil):
        sub = _sub_block(dil)
        return lambda tile: pl.BlockSpec(
            (None, tm // sub, dil, sub // dil, width),
            lambda s: (tile(s) // tiles_per_batch, tile(s) % tiles_per_batch, 0, 0, 0))

    specs = ([group_tile(GROUP_WIDTH, dil) for _, dil in SWA_PATTERN]
             + [group_tile(HEAD_DIM, dil) for _, dil in SWA_PATTERN]
             + [lambda tile: pl.BlockSpec((None, tm, MEM_WIDTH), lambda s: (3 * N_SWA_GROUPS, tile(s), 0))])
    return _mix_out(_merge_stage, (*outs, *lses, zb), specs, [], mem_kv, layer, w, x, g,
                    tm=tm, rows_per_batch=rows_per_batch, name="swa_merge_out")


def _sample_mem_attention(q_row, kv_ref, o_ref, col0):
    for h in range(N_MEM_HEADS):
        lo, hi = h * HEAD_DIM, (h + 1) * HEAD_DIM
        q = q_row[:, lo:hi]
        k = kv_ref[:, 0, h, :]
        v = kv_ref[:, 1, h, :]
        s = jnp.sum(k * q, axis=1, keepdims=True) * ATTN_SCALE
        m = jnp.max(s, axis=0, keepdims=True)
        p = jnp.exp(s - m)
        den = jnp.sum(p, axis=0, keepdims=True)
        o_ref[:, col0 + lo:col0 + hi] = jnp.sum(p * v, axis=0, keepdims=True) / den


def _sample_mix_a_kernel(z_ref, gv_ref, w0_ref, b0_ref, kv_ref, o_ref, vrow_ref):
    u = z_ref[:, 0:MIXER_WIDTH]
    v = _rms(z_ref[:, MIXER_WIDTH:2 * MIXER_WIDTH], gv_ref[...])
    vrow_ref[...] = v
    o_ref[:, 0:MIXER_WIDTH] = u * (w0_ref[...] * v + b0_ref[...])
    _sample_mem_attention(z_ref[:, 2 * MIXER_WIDTH:2 * MIXER_WIDTH + MEM_WIDTH], kv_ref, o_ref, MIXER_WIDTH)


def _sample_mix_a(z, g_v, w_s, b_s, mem_kv, layer):
    bd = mem_kv.shape[1]
    w0 = jnp.repeat(w_s[:, 0, 0], GROUP_DIM_A).reshape(1, MIXER_WIDTH)
    b0 = jnp.repeat(b_s[:, 0], GROUP_DIM_A).reshape(1, MIXER_WIDTH)
    width = z.shape[1]
    vec = lambda i: (0, 0)
    return pl.pallas_call(
        _sample_mix_a_kernel,
        grid=(bd,),
        in_specs=[
            pl.BlockSpec((None, 1, width), lambda i: (i, 0, 0)),
            pl.BlockSpec((1, MIXER_WIDTH), vec),
            pl.BlockSpec((1, MIXER_WIDTH), vec),
            pl.BlockSpec((1, MIXER_WIDTH), vec),
            _mem_kv_spec(layer, lambda i: i),
        ],
        out_specs=[
            pl.BlockSpec((None, 1, D_MODEL), lambda i: (i, 0, 0)),
            pl.BlockSpec((None, 1, MIXER_WIDTH), lambda i: (i, 0, 0)),
        ],
        out_shape=[
            jax.ShapeDtypeStruct((bd, 1, D_MODEL), F32),
            jax.ShapeDtypeStruct((bd, 1, MIXER_WIDTH), F32),
        ],
        compiler_params=_params("parallel"),
        name="sample_mix_a",
    )(z[:bd].reshape(bd, 1, width), g_v.reshape(1, MIXER_WIDTH), w0, b0, mem_kv)


def _sample_mix_b_kernel(z_ref, c0_ref, c1_ref, c2_ref, bcol_ref, bnew_ref, kv_ref, o_ref):
    caches = (c0_ref, c1_ref, c2_ref)
    outs = [[None] * HEADS_PER_GROUP for _ in range(N_SWA_GROUPS)]
    lses = [[None] * HEADS_PER_GROUP for _ in range(N_SWA_GROUPS)]
    for g in range(N_SWA_GROUPS):
        for h in range(HEADS_PER_GROUP):
            hd = g * HEADS_PER_GROUP + h
            lo, hi = h * HEAD_DIM, (h + 1) * HEAD_DIM
            q = z_ref[:, hd * HEAD_DIM:(hd + 1) * HEAD_DIM]
            k_new = z_ref[:, MIXER_WIDTH + hd * HEAD_DIM:MIXER_WIDTH + (hd + 1) * HEAD_DIM]
            v_new = z_ref[:, 2 * MIXER_WIDTH + hd * HEAD_DIM:2 * MIXER_WIDTH + (hd + 1) * HEAD_DIM]
            kc = caches[g][:, 0, h, :]
            vc = caches[g][:, 1, h, :]
            s_c = jnp.sum(kc * q, axis=1, keepdims=True) * ATTN_SCALE + bcol_ref[g][:, h:h + 1]
            s_n = jnp.sum(k_new * q, axis=1, keepdims=True) * ATTN_SCALE + bnew_ref[:, hd:hd + 1]
            m = jnp.maximum(jnp.max(s_c, axis=0, keepdims=True), s_n)
            p_c = jnp.exp(s_c - m)
            p_n = jnp.exp(s_n - m)
            den = jnp.sum(p_c, axis=0, keepdims=True) + p_n
            outs[g][h] = (jnp.sum(p_c * vc, axis=0, keepdims=True) + p_n * v_new) / den
            lses[g][h] = m + jnp.log(den)
    for h in range(HEADS_PER_GROUP):
        ls = [lses[g][h] for g in range(N_SWA_GROUPS)]
        mx = jnp.maximum(jnp.maximum(ls[0], ls[1]), ls[2])
        es = [jnp.exp(l - mx) for l in ls]
        tot = es[0] + es[1] + es[2]
        for g in range(N_SWA_GROUPS):
            c0 = g * GROUP_WIDTH + h * HEAD_DIM
            o_ref[:, c0:c0 + HEAD_DIM] = outs[g][h] * (es[g] / tot)
    _sample_mem_attention(z_ref[:, 3 * MIXER_WIDTH:3 * MIXER_WIDTH + MEM_WIDTH], kv_ref, o_ref, MIXER_WIDTH)


def _sample_mix_b(z, win_caches, swa_layer, bias_groups, mem_kv, layer):
    bd = mem_kv.shape[1]
    width = z.shape[1]
    cache_views, cache_specs = [], []
    for g, (win, dil) in enumerate(SWA_PATTERN):
        c = win_caches[g]
        cache_views.append(c.reshape(c.shape[0], bd, win // dil, dil, 2, HEADS_PER_GROUP, HEAD_DIM))
        cache_specs.append(pl.BlockSpec((None, None, N_BACK, None, 2, HEADS_PER_GROUP, HEAD_DIM),
                                        lambda i: (swa_layer, i, 0, 0, 0, 0, 0)))
    bcol = jnp.stack([bg[:, N_BACK:0:-1].T for bg in bias_groups], axis=0)
    bnew = jnp.concatenate([bg[:, 0] for bg in bias_groups])
    bnew = jnp.pad(bnew, (0, HEAD_DIM - bnew.shape[0])).reshape(1, HEAD_DIM)
    return pl.pallas_call(
        _sample_mix_b_kernel,
        grid=(bd,),
        in_specs=[pl.BlockSpec((None, 1, width), lambda i: (i, 0, 0))] + cache_specs + [
            pl.BlockSpec((N_SWA_GROUPS, N_BACK, HEADS_PER_GROUP), lambda i: (0, 0, 0)),
            pl.BlockSpec((1, HEAD_DIM), lambda i: (0, 0)),
            _mem_kv_spec(layer, lambda i: i),
        ],
        out_specs=pl.BlockSpec((None, 1, D_MODEL), lambda i: (i, 0, 0)),
        out_shape=jax.ShapeDtypeStruct((bd, 1, D_MODEL), F32),
        compiler_params=_params("parallel"),
        name="sample_mix_b",
    )(z[:bd].reshape(bd, 1, width), *cache_views, bcol, bnew, mem_kv)


def _t5_bucket(dist):
    nf = jnp.maximum(dist, MAX_EXACT).astype(F32)
    large = MAX_EXACT + (jnp.log(nf / MAX_EXACT) / math.log(MAX_DISTANCE / MAX_EXACT)
                         * (N_BUCKETS - MAX_EXACT)).astype(jnp.int32)
    large = jnp.minimum(large, N_BUCKETS - 1)
    return jnp.where(dist < MAX_EXACT, dist, large)


def _group_bias(rel_bias, g, dil):
    dist = jnp.arange(N_BACK + 1, dtype=jnp.int32) * dil
    b = rel_bias[_t5_bucket(dist)][:, g * HEADS_PER_GROUP:(g + 1) * HEADS_PER_GROUP]
    return b.T.astype(F32)


def _band_tables_kernel(b_ref, o_ref):
    n, rows, width = o_ref.shape
    for x in range(n):
        row = jnp.broadcast_to(b_ref[x:x + 1, :], (rows, width))
        o_ref[x] = pltpu.roll(row, 0, 1, stride=1, stride_axis=0)


def _band_tables(bias_groups):
    width = 2 * N_BACK
    rows = []
    for bias_j in bias_groups:
        masked = jnp.full((HEADS_PER_GROUP, N_BACK - 1), NEG_INF, F32)
        rows.append(jnp.concatenate([bias_j[:, :1], masked, bias_j[:, N_BACK:0:-1]], axis=1))
        rows.append(jnp.concatenate([bias_j[:, ::-1], masked], axis=1))
    base = jnp.stack(rows, axis=0).reshape(-1, width)
    tabs = pl.pallas_call(
        _band_tables_kernel,
        out_shape=jax.ShapeDtypeStruct((base.shape[0], N_BACK, width), F32),
        name="band_tables",
    )(base)
    return tabs.reshape(N_SWA_GROUPS, 2, HEADS_PER_GROUP, N_BACK, width)


def _kv_tail_kernel(k_ref, v_ref, o_ref, *, dil):
    rows = k_ref.shape[0]
    for kv, ref in enumerate((k_ref, v_ref)):
        x = ref[...]
        if dil > 1:
            inv = _residue_major_perm(PERM_BLOCK, dil, transpose=True)
            x = jnp.concatenate([jnp.dot(inv, x[s:s + PERM_BLOCK, :], preferred_element_type=F32)
                                 for s in range(0, rows, PERM_BLOCK)], axis=0)
        x = x.astype(F32)
        for h in range(HEADS_PER_GROUP):
            o_ref[:, kv, h, :] = x[:, h * HEAD_DIM:(h + 1) * HEAD_DIM]


def _kv_tail(zb, g, *, batch, seq):
    win, dil = SWA_PATTERN[g]
    rows = min(win, FFN_ROW_TILE)
    first = (seq - win) // rows
    per_batch = seq // rows

    def tile(t):
        return pl.BlockSpec((None, rows, GROUP_WIDTH), lambda b, s: (t, b * per_batch + first + s, 0))

    return pl.pallas_call(
        functools.partial(_kv_tail_kernel, dil=dil),
        grid=(batch, win // rows),
        in_specs=[tile(N_SWA_GROUPS + g), tile(2 * N_SWA_GROUPS + g)],
        out_specs=pl.BlockSpec((None, rows, 2, HEADS_PER_GROUP, HEAD_DIM), lambda b, s: (b, s, 0, 0, 0)),
        out_shape=jax.ShapeDtypeStruct((batch, win, 2, HEADS_PER_GROUP, HEAD_DIM), F32),
        compiler_params=_params("parallel", "parallel"),
        name=f"kv_tail{g}",
    )(zb, zb)


def kernel(x_prompt, x_sample, mem_prompt, cache_mem_kv, cache_win128_kv, cache_win512_kv, cache_win2048_kv, rel_bias, norm_mix_pre, norm_mix_post, norm_ffn_pre, norm_ffn_post, norm_mem, w_mem_kv, w_in_a, norm_v_a, w_spatial_a, b_spatial_a, w_in_b, w_out, w_ffn_up, w_ffn_down):
    batch, seq, _ = x_prompt.shape
    bd = x_sample.shape[0]
    depth = w_out.shape[0]
    m_p = batch * seq
    win_caches = (cache_win128_kv, cache_win512_kv, cache_win2048_kv)

    bias_groups = [_group_bias(rel_bias, g, dil) for g, (_, dil) in enumerate(SWA_PATTERN)]
    band_tables = _band_tables(bias_groups)

    yp = x_prompt.reshape(m_p, D_MODEL)
    ys = jnp.pad(x_sample.reshape(bd, D_MODEL), ((0, SAMPLE_PAD - bd), (0, 0)))
    mem_rows = mem_prompt.reshape(batch * N_MEM, D_MODEL)

    mem_kv_p = _mem_kv(mem_rows, norm_mem, w_mem_kv, batch=batch)
    chunk_v_s = []
    win_p = [[] for _ in SWA_PATTERN]
    win_s = [[] for _ in SWA_PATTERN]
    for i in range(depth):
        li = i // 2
        if i % 2 == 0:
            zp, zs = _in_proj_a(yp, ys, norm_mix_pre[i], w_in_a, li, tm=ROW_TILE, tn=CAST_TILE,
                                gelu_cols=2 * MIXER_WIDTH)
            mix_s, v_rows = _sample_mix_a(zs, norm_v_a[li], w_spatial_a[li], b_spatial_a[li], cache_mem_kv, i)
            chunk_v_s.append(v_rows)
            ys, w_o = _sample_out_proj(mix_s, w_out, i, ys, norm_mix_post[i])
            yp = _gmlp_mix_out(zp, norm_v_a[li], w_spatial_a[li], b_spatial_a[li], mem_kv_p, i,
                               w_o, yp, norm_mix_post[i], tm=ROW_TILE, rows_per_batch=seq)
        else:
            zb, zs = _in_proj_b(yp, ys, norm_mix_pre[i], w_in_b, li, tm=ROW_TILE)
            outs, lses = [], []
            for g, (win, dil) in enumerate(SWA_PATTERN):
                o, lse = _swa_group(zb, band_tables, g, batch=batch, seq=seq)
                outs.append(o)
                lses.append(lse)
                win_p[g].append(_kv_tail(zb, g, batch=batch, seq=seq))
                kv_new = zs[:bd, MIXER_WIDTH:3 * MIXER_WIDTH]
                kv_new = kv_new.reshape(bd, 1, 2, N_SWA_GROUPS, HEADS_PER_GROUP, HEAD_DIM)[:, :, :, g]
                win_s[g].append(kv_new)
            mix_s = _sample_mix_b(zs, win_caches, li, bias_groups, cache_mem_kv, i)
            ys, w_o = _sample_out_proj(mix_s, w_out, i, ys, norm_mix_post[i])
            yp = _swa_merge_out(outs, lses, zb, mem_kv_p, i, w_o, yp, norm_mix_post[i], rows_per_batch=seq)
        yp, ys, w_ffn = _ffn_head(yp, ys, norm_ffn_pre[i], norm_ffn_post[i], w_ffn_up, w_ffn_down, i,
                                  tm=FFN_ROW_TILE, tf=HEAD_FF_TILE)
        yp = _ffn(yp, norm_ffn_pre[i], norm_ffn_post[i], w_ffn, tm=FFN_ROW_TILE, tf=FF_TILE, first_tile=1)

    return (
        yp.reshape(batch, seq, D_MODEL),
        ys[:bd].reshape(bd, 1, D_MODEL),
        mem_kv_p,
        jnp.stack(chunk_v_s, axis=0),
        jnp.stack(win_p[0], axis=0),
        jnp.stack(win_p[1], axis=0),
        jnp.stack(win_p[2], axis=0),
        jnp.stack(win_s[0], axis=0),
        jnp.stack(win_s[1], axis=0),
        jnp.stack(win_s[2], axis=0),
    )
```

```python
import functools
import math

import jax
import jax.numpy as jnp
from jax import lax
from jax.experimental import pallas as pl
from jax.experimental.pallas import tpu as pltpu

F32 = jnp.float32
BF16 = jnp.bfloat16

D_MODEL = 2048
HEAD_DIM = 128
N_MEM = 256
N_MEM_HEADS = 4
MEM_WIDTH = N_MEM_HEADS * HEAD_DIM
MIXER_WIDTH = D_MODEL - MEM_WIDTH
CHUNK = 128
N_GROUPS_A = 4
GROUP_DIM_A = MIXER_WIDTH // N_GROUPS_A
SWA_PATTERN = ((128, 1), (512, 4), (2048, 16))
N_SWA_GROUPS = len(SWA_PATTERN)
HEADS_PER_GROUP = 4
GROUP_WIDTH = HEADS_PER_GROUP * HEAD_DIM
N_BACK = 128
N_BUCKETS = 32
MAX_EXACT = N_BUCKETS // 2
MAX_DISTANCE = 2048
D_FF = 5632
EPS = 1e-6
NEG_INF = -1e30
ATTN_SCALE = HEAD_DIM ** -0.5
SAMPLE_PAD = 16
PERM_BLOCK = 256
OUT_CHUNK = 256

ROW_TILE = 512
FFN_ROW_TILE = 1024
FF_TILE = 512
HEAD_FF_TILE = 256
CAST_TILE = 512

V7X_VMEM_BYTES = 64 * 1024 * 1024
VMEM_LIMIT = V7X_VMEM_BYTES - 8 * 1024 * 1024
FFN_VMEM_LIMIT = V7X_VMEM_BYTES - 2 * 1024 * 1024


def _params(*sem, vmem_limit=VMEM_LIMIT):
    return pltpu.CompilerParams(dimension_semantics=sem, vmem_limit_bytes=vmem_limit)


def _gelu(x):
    return 0.5 * x * (1.0 + jnp.tanh(0.7978845608028654 * (x + 0.044715 * (x * x * x))))


def _rms(x, g):
    return x * lax.rsqrt(jnp.mean(x * x, axis=-1, keepdims=True) + EPS) * g


def _log2(n):
    assert n & (n - 1) == 0
    return n.bit_length() - 1


def _residue_major_perm(tm, dil, transpose=False):
    n = tm // dil
    row = lax.broadcasted_iota(jnp.int32, (tm, tm), 0)
    col = lax.broadcasted_iota(jnp.int32, (tm, tm), 1)
    dst, src = (col, row) if transpose else (row, col)
    want = lax.shift_left(jnp.bitwise_and(dst, n - 1), _log2(dil)) + lax.shift_right_logical(dst, _log2(n))
    return (src == want).astype(BF16)


def _in_proj_a_kernel(x_ref, xs_ref, g_ref, w_ref, zp_ref, zs_ref, wb_ref, xn_ref, *, n_col, gelu_cols):
    tm = x_ref.shape[0]
    tn = w_ref.shape[1]
    s = pl.program_id(0)

    def prompt_cols(acc, c0):
        col = lax.broadcasted_iota(jnp.int32, acc.shape, 1) + c0
        return jnp.where(col < gelu_cols, _gelu(acc), acc * ATTN_SCALE).astype(zp_ref.dtype)

    @pl.when(s == 0)
    def _():
        xn_ref[:tm, :] = _rms(x_ref[...], g_ref[...]).astype(BF16)
        xn_ref[tm:, :] = _rms(xs_ref[...], g_ref[...]).astype(BF16)

    for j in range(n_col):
        @pl.when(s == j)
        def _(j=j):
            cols = slice(j * tn, (j + 1) * tn)
            w = w_ref[...].astype(BF16)
            wb_ref[:, cols] = w
            acc = jnp.dot(xn_ref[...], w, preferred_element_type=F32)
            zp_ref[:, cols] = prompt_cols(acc[:tm, :], j * tn)
            sample = acc[tm:, :]
            zs_ref[:, cols] = _gelu(sample) if (j + 1) * tn <= gelu_cols else sample

    @pl.when(s >= n_col)
    def _():
        xn = _rms(x_ref[...], g_ref[...]).astype(BF16)
        acc = jnp.dot(xn, wb_ref[...], preferred_element_type=F32)
        zp_ref[...] = prompt_cols(acc, 0)


def _in_proj_a(x, xs, g, w, layer, *, tm, tn, gelu_cols):
    m, k = x.shape
    ms = xs.shape[0]
    n = w.shape[2]
    n_col, n_row = n // tn, m // tm
    assert gelu_cols % tn == 0
    row = lambda s: jnp.maximum(s - (n_col - 1), 0)
    return pl.pallas_call(
        functools.partial(_in_proj_a_kernel, n_col=n_col, gelu_cols=gelu_cols),
        grid=(n_col + n_row - 1,),
        in_specs=[
            pl.BlockSpec((tm, k), lambda s: (row(s), 0)),
            pl.BlockSpec((ms, k), lambda s: (0, 0)),
            pl.BlockSpec((1, k), lambda s: (0, 0)),
            pl.BlockSpec((None, k, tn), lambda s: (layer, 0, jnp.minimum(s, n_col - 1))),
        ],
        out_specs=[
            pl.BlockSpec((tm, n), lambda s: (row(s), 0)),
            pl.BlockSpec((ms, n), lambda s: (0, 0)),
        ],
        out_shape=[
            jax.ShapeDtypeStruct((m, n), BF16),
            jax.ShapeDtypeStruct((ms, n), F32),
        ],
        scratch_shapes=[pltpu.VMEM((k, n), BF16), pltpu.VMEM((tm + ms, k), BF16)],
        compiler_params=_params("arbitrary"),
        name="in_proj_a",
    )(x, xs, g.reshape(1, k), w)


def _mem_kv_kernel(x_ref, g_ref, w_ref, o_ref):
    xn = _rms(x_ref[...], g_ref[...]).astype(BF16)
    acc = jnp.dot(xn, w_ref[...].astype(BF16), preferred_element_type=F32)
    for kv in range(2):
        for h in range(N_MEM_HEADS):
            c0 = (kv * N_MEM_HEADS + h) * HEAD_DIM
            o_ref[:, kv, h, :] = acc[:, c0:c0 + HEAD_DIM]


def _mem_kv(mem_rows, g, w, *, batch):
    m, k = mem_rows.shape
    layers = w.shape[0]
    out = pl.pallas_call(
        _mem_kv_kernel,
        grid=(layers,),
        in_specs=[
            pl.BlockSpec((m, k), lambda l: (0, 0)),
            pl.BlockSpec((None, 1, k), lambda l: (l, 0, 0)),
            pl.BlockSpec((None, k, 2 * MEM_WIDTH), lambda l: (l, 0, 0)),
        ],
        out_specs=pl.BlockSpec((None, m, 2, N_MEM_HEADS, HEAD_DIM), lambda l: (l, 0, 0, 0, 0)),
        out_shape=jax.ShapeDtypeStruct((layers, m, 2, N_MEM_HEADS, HEAD_DIM), F32),
        compiler_params=_params("parallel"),
        name="mem_kv",
    )(mem_rows, g.reshape(layers, 1, k), w)
    return out.reshape(layers, batch, m // batch, 2, N_MEM_HEADS, HEAD_DIM)


def _in_proj_b_kernel(x_ref, xs_ref, g_ref, w_ref, o_ref, zs_ref, wb_ref, xn_ref):
    tm = x_ref.shape[0]
    n_tiles = o_ref.shape[0]
    n_qkv = 3 * N_SWA_GROUPS
    s = pl.program_id(0)

    def row_orders():
        xn = _rms(x_ref[...], g_ref[...]).astype(BF16)
        xn_ref[0, :tm, :] = xn
        for g in range(1, N_SWA_GROUPS):
            perm = _residue_major_perm(PERM_BLOCK, SWA_PATTERN[g][1])
            for r in range(0, tm, PERM_BLOCK):
                xn_ref[g, r:r + PERM_BLOCK, :] = jnp.dot(
                    perm, xn[r:r + PERM_BLOCK, :], preferred_element_type=F32).astype(BF16)

    def prompt_tile(t, acc):
        if t < N_SWA_GROUPS or t == n_qkv:
            acc = acc * ATTN_SCALE
        o_ref[t] = acc.astype(o_ref.dtype)

    src = lambda t: t % N_SWA_GROUPS if t < n_qkv else 0

    @pl.when(s == 0)
    def _():
        row_orders()
        xs = _rms(xs_ref[...], g_ref[...]).astype(BF16)
        for g in range(N_SWA_GROUPS):
            xn_ref[g, tm:, :] = xs

    for t in range(n_tiles):
        @pl.when(s == t)
        def _(t=t):
            cols = slice(t * GROUP_WIDTH, (t + 1) * GROUP_WIDTH)
            w = w_ref[...].astype(BF16)
            wb_ref[:, cols] = w
            acc = jnp.dot(xn_ref[src(t)], w, preferred_element_type=F32)
            prompt_tile(t, acc[:tm, :])
            zs_ref[:, cols] = acc[tm:, :]

    @pl.when(s >= n_tiles)
    def _():
        row_orders()
        for t in range(n_tiles):
            w = wb_ref[:, t * GROUP_WIDTH:(t + 1) * GROUP_WIDTH]
            prompt_tile(t, jnp.dot(xn_ref[src(t), :tm, :], w, preferred_element_type=F32))


def _in_proj_b(x, xs, g, w, layer, *, tm):
    m, k = x.shape
    ms = xs.shape[0]
    n = w.shape[2]
    n_tiles = n // GROUP_WIDTH
    row = lambda s: jnp.maximum(s - (n_tiles - 1), 0)
    return pl.pallas_call(
        _in_proj_b_kernel,
        grid=(n_tiles + m // tm - 1,),
        in_specs=[
            pl.BlockSpec((tm, k), lambda s: (row(s), 0)),
            pl.BlockSpec((ms, k), lambda s: (0, 0)),
            pl.BlockSpec((1, k), lambda s: (0, 0)),
            pl.BlockSpec((None, k, GROUP_WIDTH), lambda s: (layer, 0, jnp.minimum(s, n_tiles - 1))),
        ],
        out_specs=[
            pl.BlockSpec((n_tiles, tm, GROUP_WIDTH), lambda s: (0, row(s), 0)),
            pl.BlockSpec((ms, n), lambda s: (0, 0)),
        ],
        out_shape=[
            jax.ShapeDtypeStruct((n_tiles, m, GROUP_WIDTH), BF16),
            jax.ShapeDtypeStruct((ms, n), F32),
        ],
        scratch_shapes=[pltpu.VMEM((k, n), BF16), pltpu.VMEM((N_SWA_GROUPS, tm + ms, k), BF16)],
        compiler_params=_params("arbitrary"),
        name="in_proj_b",
    )(x, xs, g.reshape(1, k), w)


MEM_KV_SCRATCH = [pltpu.VMEM((N_MEM_HEADS, N_MEM, HEAD_DIM), BF16),
                  pltpu.VMEM((N_MEM_HEADS, N_MEM, 2 * HEAD_DIM), BF16)]


def _prepare_mem_kv(kv_ref, kb_ref, vb_ref):
    for h in range(N_MEM_HEADS):
        kb_ref[h] = kv_ref[:, 0, h, :].astype(BF16)
        vb_ref[h, :, :HEAD_DIM] = kv_ref[:, 1, h, :].astype(BF16)
        vb_ref[h, :, HEAD_DIM:] = jnp.ones((N_MEM, HEAD_DIM), BF16)


def _mem_attention_parts(q_ref, kb_ref, vb_ref, o_ref, col0):
    def head(h):
        lo, hi = h * HEAD_DIM, (h + 1) * HEAD_DIM
        s = lax.dot_general(q_ref[:, lo:hi], kb_ref[h], (((1,), (1,)), ((), ())), preferred_element_type=F32)
        m = jnp.max(s, axis=1, keepdims=True)
        p = jnp.exp(s - m).astype(BF16)
        ov = jnp.dot(p, vb_ref[h], preferred_element_type=F32)
        o_ref[:, col0 + lo:col0 + hi] = (ov[:, :HEAD_DIM] / ov[:, HEAD_DIM:]).astype(o_ref.dtype)

    return [functools.partial(head, h) for h in range(N_MEM_HEADS)]


def _gmlp_stage(u_ref, v_ref, q_ref, gv_ref, ws_ref, bs_ref, kb_ref, vb_ref, vn_ref, o_ref):
    tm = u_ref.shape[0]

    def norm_v():
        vn_ref[...] = _rms(v_ref[...].astype(F32), gv_ref[...]).astype(BF16)

    def group(g):
        row = lax.broadcasted_iota(jnp.int32, (CHUNK, CHUNK), 0)
        col = lax.broadcasted_iota(jnp.int32, (CHUNK, CHUNK), 1)
        w = jnp.where(row >= col, ws_ref[g], 0.0).astype(BF16)
        b = bs_ref[:, g:g + 1]
        c0, c1 = g * GROUP_DIM_A, (g + 1) * GROUP_DIM_A
        for c in range(tm // CHUNK):
            r0, r1 = c * CHUNK, (c + 1) * CHUNK
            s = jnp.dot(w, vn_ref[r0:r1, c0:c1], preferred_element_type=F32) + b
            o_ref[r0:r1, c0:c1] = (u_ref[r0:r1, c0:c1].astype(F32) * s).astype(o_ref.dtype)

    return ([norm_v] + [functools.partial(group, g) for g in range(N_GROUPS_A)]
            + _mem_attention_parts(q_ref, kb_ref, vb_ref, o_ref, MIXER_WIDTH))


def _mem_kv_spec(layer, batch_of):
    return pl.BlockSpec((None, None, N_MEM, 2, N_MEM_HEADS, HEAD_DIM),
                        lambda i: (layer, batch_of(i), 0, 0, 0, 0))


def _mix_out_kernel(*refs, stage, n_in, n_tiles, tiles_per_batch):
    mix_in = refs[:n_in]
    kv_ref, w_ref, x_ref, g_ref, o_ref, buf0_ref, buf1_ref, acc_ref, kb_ref, vb_ref = refs[n_in:n_in + 10]
    extra = refs[n_in + 10:]
    s = pl.program_id(0)

    @pl.when(s == 0)
    def _():
        buf1_ref[...] = jnp.zeros_like(buf1_ref)

    @pl.when(jnp.minimum(s, n_tiles - 1) % tiles_per_batch == 0)
    def _():
        _prepare_mem_kv(kv_ref, kb_ref, vb_ref)

    def step(dst_ref, src_ref):
        parts = stage(*mix_in, kb_ref, vb_ref, *extra, dst_ref)
        n_chunks = D_MODEL // OUT_CHUNK
        for c in range(n_chunks):
            cols = slice(c * OUT_CHUNK, (c + 1) * OUT_CHUNK)
            acc_ref[:, cols] = jnp.dot(src_ref[...], w_ref[:, cols], preferred_element_type=F32)
            for part in parts[c * len(parts) // n_chunks:(c + 1) * len(parts) // n_chunks]:
                part()
        o_ref[...] = x_ref[...] + _rms(acc_ref[...], g_ref[...])

    @pl.when(s % 2 == 0)
    def _():
        step(buf0_ref, buf1_ref)

    @pl.when(s % 2 == 1)
    def _():
        step(buf1_ref, buf0_ref)


def _mix_out(stage, mix_inputs, mix_specs, extra_scratch, mem_kv, layer, w, x, g, *, tm, rows_per_batch, name):
    m = x.shape[0]
    n_tiles = m // tm
    tiles_per_batch = rows_per_batch // tm
    mix_tile = lambda s: jnp.minimum(s, n_tiles - 1)
    out_tile = lambda s: jnp.maximum(s - 1, 0)
    row_spec = pl.BlockSpec((tm, D_MODEL), lambda s: (out_tile(s), 0))
    return pl.pallas_call(
        functools.partial(_mix_out_kernel, stage=stage, n_in=len(mix_inputs), n_tiles=n_tiles,
                          tiles_per_batch=tiles_per_batch),
        grid=(n_tiles + 1,),
        in_specs=[spec(mix_tile) for spec in mix_specs] + [
            _mem_kv_spec(layer, lambda s: mix_tile(s) // tiles_per_batch),
            pl.BlockSpec((D_MODEL, D_MODEL), lambda s: (0, 0), pipeline_mode=pl.Buffered(1)),
            row_spec,
            pl.BlockSpec((1, D_MODEL), lambda s: (0, 0)),
        ],
        out_specs=row_spec,
        out_shape=jax.ShapeDtypeStruct((m, D_MODEL), F32),
        scratch_shapes=[pltpu.VMEM((tm, D_MODEL), BF16), pltpu.VMEM((tm, D_MODEL), BF16),
                        pltpu.VMEM((tm, D_MODEL), F32)] + MEM_KV_SCRATCH + list(extra_scratch),
        compiler_params=_params("arbitrary"),
        name=name,
    )(*mix_inputs, mem_kv, w, x, g.reshape(1, D_MODEL))


def _gmlp_mix_out(zact, g_v, w_s, b_s, mem_kv, layer, w, x, g, *, tm, rows_per_batch):
    const = lambda shape: (lambda tile: pl.BlockSpec(shape, lambda s: (0,) * len(shape)))
    specs = [
        lambda tile: pl.BlockSpec((tm, MIXER_WIDTH), lambda s: (tile(s), 0)),
        lambda tile: pl.BlockSpec((tm, MIXER_WIDTH), lambda s: (tile(s), 1)),
        lambda tile: pl.BlockSpec((tm, MEM_WIDTH), lambda s: (tile(s), 2 * MIXER_WIDTH // MEM_WIDTH)),
        const((1, MIXER_WIDTH)),
        const((N_GROUPS_A, CHUNK, CHUNK)),
        const((CHUNK, N_GROUPS_A)),
    ]
    return _mix_out(_gmlp_stage, (zact, zact, zact, g_v.reshape(1, MIXER_WIDTH), w_s, b_s.T), specs,
                    [pltpu.VMEM((tm, MIXER_WIDTH), BF16)], mem_kv, layer, w, x, g,
                    tm=tm, rows_per_batch=rows_per_batch, name="gmlp_mix_out")


def _out_proj_cast_kernel(mix_ref, w_ref, x_ref, g_ref, o_ref, wb_ref, acc_ref):
    j = pl.program_id(0)
    w = w_ref[...].astype(BF16)
    wb_ref[...] = w
    acc_ref[j] = jnp.dot(mix_ref[...], w, preferred_element_type=F32)

    @pl.when(j == pl.num_programs(0) - 1)
    def _():
        o = jnp.concatenate([acc_ref[t] for t in range(acc_ref.shape[0])], axis=1)
        o_ref[...] = x_ref[...] + _rms(o, g_ref[...])


def _sample_out_proj(mix, w, layer, x, g):
    bd = mix.shape[0]
    mix = jnp.pad(mix.reshape(bd, D_MODEL), ((0, x.shape[0] - bd), (0, 0))).astype(BF16)
    return _out_proj_cast(mix, w, layer, x, g, tn=CAST_TILE)


def _out_proj_cast(mix, w, layer, x, g, *, tn):
    m = x.shape[0]
    n_tiles = D_MODEL // tn
    return pl.pallas_call(
        _out_proj_cast_kernel,
        grid=(n_tiles,),
        in_specs=[
            pl.BlockSpec((m, D_MODEL), lambda j: (0, 0)),
            pl.BlockSpec((None, D_MODEL, tn), lambda j: (layer, 0, j)),
            pl.BlockSpec((m, D_MODEL), lambda j: (0, 0)),
            pl.BlockSpec((1, D_MODEL), lambda j: (0, 0)),
        ],
        out_specs=[
            pl.BlockSpec((m, D_MODEL), lambda j: (0, 0)),
            pl.BlockSpec((D_MODEL, tn), lambda j: (0, j)),
        ],
        out_shape=[
            jax.ShapeDtypeStruct((m, D_MODEL), F32),
            jax.ShapeDtypeStruct((D_MODEL, D_MODEL), BF16),
        ],
        scratch_shapes=[pltpu.VMEM((n_tiles, m, tn), F32)],
        compiler_params=_params("arbitrary"),
        name="out_proj_cast",
    )(mix, w, x, g.reshape(1, D_MODEL))


def _ffn_kernel(x_ref, gpre_ref, gpost_ref, wg_ref, wl_ref, wd_ref, o_ref, xn_ref, ssq_ref):
    f = pl.program_id(1)

    @pl.when(f == 0)
    def _():
        xn_ref[...] = _rms(x_ref[...], gpre_ref[...]).astype(BF16)
        o_ref[...] = jnp.zeros_like(o_ref)

    xn = xn_ref[...]
    hg = jnp.dot(xn, wg_ref[...], preferred_element_type=F32)
    hl = jnp.dot(xn, wl_ref[...], preferred_element_type=F32)
    a = (hg * jax.nn.sigmoid(hg) * hl).astype(BF16)
    new = o_ref[...] + jnp.dot(a, wd_ref[...], preferred_element_type=F32)
    o_ref[...] = new
    sq = new * new
    ssq_ref[...] = functools.reduce(
        jnp.add, [sq[:, c:c + HEAD_DIM] for c in range(0, sq.shape[1], HEAD_DIM)])

    @pl.when(f == pl.num_programs(1) - 1)
    def _():
        ms = jnp.sum(ssq_ref[...], axis=-1, keepdims=True) * (1.0 / o_ref.shape[1])
        o_ref[...] = x_ref[...] + o_ref[...] * lax.rsqrt(ms + EPS) * gpost_ref[...]


def _ffn(x, g_pre, g_post, weights, *, tm, tf, first_tile):
    m = x.shape[0]
    nf = D_FF // tf
    rows = pl.BlockSpec((tm, D_MODEL), lambda i, f: (i + first_tile, 0))
    return pl.pallas_call(
        _ffn_kernel,
        grid=(m // tm - first_tile, nf),
        in_specs=[
            rows,
            pl.BlockSpec((1, D_MODEL), lambda i, f: (0, 0)),
            pl.BlockSpec((1, D_MODEL), lambda i, f: (0, 0)),
            pl.BlockSpec((D_MODEL, tf), lambda i, f: (0, f)),
            pl.BlockSpec((D_MODEL, tf), lambda i, f: (0, f)),
            pl.BlockSpec((tf, D_MODEL), lambda i, f: (f, 0)),
        ],
        out_specs=rows,
        out_shape=jax.ShapeDtypeStruct((m, D_MODEL), F32),
        input_output_aliases={0: 0},
        scratch_shapes=[pltpu.VMEM((tm, D_MODEL), BF16), pltpu.VMEM((tm, HEAD_DIM), F32)],
        compiler_params=_params("parallel", "arbitrary", vmem_limit=FFN_VMEM_LIMIT),
        name="ffn",
    )(x, g_pre.reshape(1, D_MODEL), g_post.reshape(1, D_MODEL), *weights)


def _ffn_head_kernel(x_ref, xs_ref, gpre_ref, gpost_ref, wg_ref, wl_ref, wd_ref,
                     o_ref, os_ref, wgb_ref, wlb_ref, wdb_ref, xn_ref):
    tm = x_ref.shape[0]
    f = pl.program_id(0)

    @pl.when(f == 0)
    def _():
        xn_ref[:tm, :] = _rms(x_ref[...], gpre_ref[...]).astype(BF16)
        xn_ref[tm:, :] = _rms(xs_ref[...], gpre_ref[...]).astype(BF16)
        o_ref[...] = jnp.zeros_like(o_ref)
        os_ref[...] = jnp.zeros_like(os_ref)

    wg, wl, wd = (r[...].astype(BF16) for r in (wg_ref, wl_ref, wd_ref))
    wgb_ref[...] = wg
    wlb_ref[...] = wl
    wdb_ref[...] = wd
    xn = xn_ref[...]
    hg = jnp.dot(xn, wg, preferred_element_type=F32)
    hl = jnp.dot(xn, wl, preferred_element_type=F32)
    a = (hg * jax.nn.sigmoid(hg) * hl).astype(BF16)
    part = jnp.dot(a, wd, preferred_element_type=F32)
    o_ref[...] += part[:tm, :]
    os_ref[...] += part[tm:, :]

    @pl.when(f == pl.num_programs(0) - 1)
    def _():
        o_ref[...] = x_ref[...] + _rms(o_ref[...], gpost_ref[...])
        os_ref[...] = xs_ref[...] + _rms(os_ref[...], gpost_ref[...])


def _ffn_head(x, xs, g_pre, g_post, w_up, w_down, layer, *, tm, tf):
    m = x.shape[0]
    ms = xs.shape[0]
    nf = D_FF // tf
    once = dict(pipeline_mode=pl.Buffered(1))
    head = pl.BlockSpec((tm, D_MODEL), lambda f: (0, 0), **once)
    sample = pl.BlockSpec((ms, D_MODEL), lambda f: (0, 0))
    vec = pl.BlockSpec((1, D_MODEL), lambda f: (0, 0))
    outs = pl.pallas_call(
        _ffn_head_kernel,
        grid=(nf,),
        in_specs=[
            head, sample, vec, vec,
            pl.BlockSpec((None, D_MODEL, tf), lambda f: (layer, 0, f)),
            pl.BlockSpec((None, D_MODEL, tf), lambda f: (layer, 0, nf + f)),
            pl.BlockSpec((None, tf, D_MODEL), lambda f: (layer, f, 0)),
        ],
        out_specs=[
            head, sample,
            pl.BlockSpec((D_MODEL, tf), lambda f: (0, f)),
            pl.BlockSpec((D_MODEL, tf), lambda f: (0, f)),
            pl.BlockSpec((tf, D_MODEL), lambda f: (f, 0)),
        ],
        out_shape=[
            jax.ShapeDtypeStruct((m, D_MODEL), F32),
            jax.ShapeDtypeStruct((ms, D_MODEL), F32),
            jax.ShapeDtypeStruct((D_MODEL, D_FF), BF16),
            jax.ShapeDtypeStruct((D_MODEL, D_FF), BF16),
            jax.ShapeDtypeStruct((D_FF, D_MODEL), BF16),
        ],
        input_output_aliases={0: 0},
        scratch_shapes=[pltpu.VMEM((tm + ms, D_MODEL), BF16)],
        compiler_params=_params("arbitrary"),
        name="ffn_head",
    )(x, xs, g_pre.reshape(1, D_MODEL), g_post.reshape(1, D_MODEL), w_up, w_up, w_down)
    return outs[0], outs[1], tuple(outs[2:])


def _swa_kernel(q_ref, k_ref, v_ref, tb_ref, o_ref, lse_ref):
    n_units, n_res, u, _ = q_ref.shape
    per_blk = N_BACK // u
    n_blk = n_units // per_blk
    lane = lax.broadcasted_iota(jnp.int32, (N_BACK, HEAD_DIM), 1)
    ones = jnp.ones((2 * N_BACK, HEAD_DIM), BF16)

    def rows(ref, res, unit0, n_rows, lo, hi):
        return ref[pl.ds(unit0, n_rows // u), res, :, lo:hi].reshape(n_rows, hi - lo)

    def block(res, qu, ku, table):
        n_keys = N_BACK if table == 0 else 2 * N_BACK
        lse_tile = jnp.zeros((N_BACK, HEAD_DIM), F32)
        for h in range(HEADS_PER_GROUP):
            lo, hi = h * HEAD_DIM, (h + 1) * HEAD_DIM
            q = rows(q_ref, res, qu, N_BACK, lo, hi)
            kw = rows(k_ref, res, ku, n_keys, lo, hi)
            vw = rows(v_ref, res, ku, n_keys, lo, hi)
            s = lax.dot_general(q, kw, (((1,), (1,)), ((), ())), preferred_element_type=F32)
            s = s + tb_ref[table, h][:, :n_keys]
            m = jnp.max(s, axis=1, keepdims=True)
            p = jnp.exp(s - m).astype(BF16)
            ov = jnp.dot(p, jnp.concatenate([vw, ones[:n_keys]], axis=1), preferred_element_type=F32)
            den = ov[:, HEAD_DIM:]
            o = ov[:, :HEAD_DIM] / den
            o_ref[pl.ds(qu, per_blk), res, :, lo:hi] = o.reshape(per_blk, u, HEAD_DIM).astype(o_ref.dtype)
            lse_tile = jnp.where(lane == h, m + jnp.log(den), lse_tile)
        lse_ref[pl.ds(qu, per_blk), res, :, :] = lse_tile.reshape(per_blk, u, HEAD_DIM)

    def block_at(res, n):
        block(res, n * per_blk, (n - 1) * per_blk, 1)

    n_pairs = (n_blk - 1) // 2
    for res in range(n_res):
        block(res, 0, 0, 0)

        def body(i, carry, res=res):
            block_at(res, 1 + 2 * i)
            block_at(res, 2 + 2 * i)
            return carry

        if n_pairs > 0:
            lax.fori_loop(0, n_pairs, body, 0)
        if (n_blk - 1) % 2 == 1:
            block_at(res, n_blk - 1)


def _sub_block(dil):
    return N_BACK if dil == 1 else PERM_BLOCK


def _swa_group(zb, tables, g, *, batch, seq):
    dil = SWA_PATTERN[g][1]
    sub = _sub_block(dil)
    n_units, u = seq // sub, sub // dil
    n_res = 4 if n_units * u == 2 * N_BACK else 1
    view = zb.reshape(zb.shape[0], batch, n_units, dil, u, GROUP_WIDTH)

    def rows_in(tile):
        return pl.BlockSpec((None, None, n_units, n_res, u, GROUP_WIDTH), lambda i, r: (tile, i, 0, r, 0, 0))

    def rows_out(width):
        return pl.BlockSpec((None, n_units, n_res, u, width), lambda i, r: (i, 0, r, 0, 0))

    return pl.pallas_call(
        _swa_kernel,
        grid=(batch, dil // n_res),
        in_specs=[rows_in(g), rows_in(N_SWA_GROUPS + g), rows_in(2 * N_SWA_GROUPS + g),
                  pl.BlockSpec((None, 2, HEADS_PER_GROUP, N_BACK, 2 * N_BACK), lambda i, r: (g, 0, 0, 0, 0))],
        out_specs=[rows_out(GROUP_WIDTH), rows_out(HEAD_DIM)],
        out_shape=[
            jax.ShapeDtypeStruct((batch, n_units, dil, u, GROUP_WIDTH), BF16),
            jax.ShapeDtypeStruct((batch, n_units, dil, u, HEAD_DIM), F32),
        ],
        compiler_params=_params("parallel", "parallel"),
        name=f"swa_group{g}",
    )(view, view, view, tables)


def _split3(x):
    hi = x.astype(BF16)
    rest = x - hi.astype(F32)
    mid = rest.astype(BF16)
    lo = (rest - mid.astype(F32)).astype(BF16)
    return hi, mid, lo


def _merge_stage(o0_ref, o1_ref, o2_ref, l0_ref, l1_ref, l2_ref, q_ref, kb_ref, vb_ref, o_ref):
    tm = o_ref.shape[0]
    group_refs = ((o0_ref, l0_ref), (o1_ref, l1_ref), (o2_ref, l2_ref))
    outs, lses = [None] * N_SWA_GROUPS, [None] * N_SWA_GROUPS

    def token_order(g):
        o_g, l_g = group_refs[g]
        dil = SWA_PATTERN[g][1]
        o = o_g[...].reshape(tm, GROUP_WIDTH)
        l = l_g[...].reshape(tm, HEAD_DIM)
        if dil > 1:
            inv = _residue_major_perm(PERM_BLOCK, dil, transpose=True)
            l3 = _split3(l)
            o_nat, l_nat = [], []
            for s in range(0, tm, PERM_BLOCK):
                o_nat.append(jnp.dot(inv, o[s:s + PERM_BLOCK, :], preferred_element_type=F32))
                l_nat.append(sum(jnp.dot(inv, t[s:s + PERM_BLOCK, :], preferred_element_type=F32) for t in l3))
            o = jnp.concatenate(o_nat, axis=0)
            l = jnp.concatenate(l_nat, axis=0)
        outs[g] = o.astype(F32)
        lses[g] = l

    def merge_head(h):
        lo, hi = h * HEAD_DIM, (h + 1) * HEAD_DIM
        ls = [l[:, h:h + 1] for l in lses]
        mx = jnp.maximum(jnp.maximum(ls[0], ls[1]), ls[2])
        es = [jnp.exp(l - mx) for l in ls]
        tot = es[0] + es[1] + es[2]
        for g in range(N_SWA_GROUPS):
            alpha = es[g] / tot
            o_ref[:, g * GROUP_WIDTH + lo:g * GROUP_WIDTH + hi] = (outs[g][:, lo:hi] * alpha).astype(o_ref.dtype)

    return ([functools.partial(token_order, g) for g in range(N_SWA_GROUPS)]
            + [functools.partial(merge_head, h) for h in range(HEADS_PER_GROUP)]
            + _mem_attention_parts(q_ref, kb_ref, vb_ref, o_ref, MIXER_WIDTH))


def _swa_merge_out(outs, lses, zb, mem_kv, layer, w, x, g, *, rows_per_batch):
    tm = ROW_TILE
    tiles_per_batch = rows_per_batch // tm

    def group_tile(width, dil):
        sub = _sub_block(dil)
        return lambda tile: pl.BlockSpec(
            (None, tm // sub, dil, sub // dil, width),
            lambda s: (tile(s) // tiles_per_batch, tile(s) % tiles_per_batch, 0, 0, 0))

    specs = ([group_tile(GROUP_WIDTH, dil) for _, dil in SWA_PATTERN]
             + [group_tile(HEAD_DIM, dil) for _, dil in SWA_PATTERN]
             + [lambda tile: pl.BlockSpec((None, tm, MEM_WIDTH), lambda s: (3 * N_SWA_GROUPS, tile(s), 0))])
    return _mix_out(_merge_stage, (*outs, *lses, zb), specs, [], mem_kv, layer, w, x, g,
                    tm=tm, rows_per_batch=rows_per_batch, name="swa_merge_out")


def _sample_mem_attention(q_row, kv_ref, o_ref, col0):
    for h in range(N_MEM_HEADS):
        lo, hi = h * HEAD_DIM, (h + 1) * HEAD_DIM
        q = q_row[:, lo:hi]
        k = kv_ref[:, 0, h, :]
        v = kv_ref[:, 1, h, :]
        s = jnp.sum(k * q, axis=1, keepdims=True) * ATTN_SCALE
        m = jnp.max(s, axis=0, keepdims=True)
        p = jnp.exp(s - m)
        den = jnp.sum(p, axis=0, keepdims=True)
        o_ref[:, col0 + lo:col0 + hi] = jnp.sum(p * v, axis=0, keepdims=True) / den


def _sample_mix_a_kernel(z_ref, gv_ref, w0_ref, b0_ref, kv_ref, o_ref, vrow_ref):
    u = z_ref[:, 0:MIXER_WIDTH]
    v = _rms(z_ref[:, MIXER_WIDTH:2 * MIXER_WIDTH], gv_ref[...])
    vrow_ref[...] = v
    o_ref[:, 0:MIXER_WIDTH] = u * (w0_ref[...] * v + b0_ref[...])
    _sample_mem_attention(z_ref[:, 2 * MIXER_WIDTH:2 * MIXER_WIDTH + MEM_WIDTH], kv_ref, o_ref, MIXER_WIDTH)


def _sample_mix_a(z, g_v, w_s, b_s, mem_kv, layer):
    bd = mem_kv.shape[1]
    w0 = jnp.repeat(w_s[:, 0, 0], GROUP_DIM_A).reshape(1, MIXER_WIDTH)
    b0 = jnp.repeat(b_s[:, 0], GROUP_DIM_A).reshape(1, MIXER_WIDTH)
    width = z.shape[1]
    vec = lambda i: (0, 0)
    return pl.pallas_call(
        _sample_mix_a_kernel,
        grid=(bd,),
        in_specs=[
            pl.BlockSpec((None, 1, width), lambda i: (i, 0, 0)),
            pl.BlockSpec((1, MIXER_WIDTH), vec),
            pl.BlockSpec((1, MIXER_WIDTH), vec),
            pl.BlockSpec((1, MIXER_WIDTH), vec),
            _mem_kv_spec(layer, lambda i: i),
        ],
        out_specs=[
            pl.BlockSpec((None, 1, D_MODEL), lambda i: (i, 0, 0)),
            pl.BlockSpec((None, 1, MIXER_WIDTH), lambda i: (i, 0, 0)),
        ],
        out_shape=[
            jax.ShapeDtypeStruct((bd, 1, D_MODEL), F32),
            jax.ShapeDtypeStruct((bd, 1, MIXER_WIDTH), F32),
        ],
        compiler_params=_params("parallel"),
        name="sample_mix_a",
    )(z[:bd].reshape(bd, 1, width), g_v.reshape(1, MIXER_WIDTH), w0, b0, mem_kv)


def _sample_mix_b_kernel(z_ref, c0_ref, c1_ref, c2_ref, bcol_ref, bnew_ref, kv_ref, o_ref):
    caches = (c0_ref, c1_ref, c2_ref)
    outs = [[None] * HEADS_PER_GROUP for _ in range(N_SWA_GROUPS)]
    lses = [[None] * HEADS_PER_GROUP for _ in range(N_SWA_GROUPS)]
    for g in range(N_SWA_GROUPS):
        for h in range(HEADS_PER_GROUP):
            hd = g * HEADS_PER_GROUP + h
            lo, hi = h * HEAD_DIM, (h + 1) * HEAD_DIM
            q = z_ref[:, hd * HEAD_DIM:(hd + 1) * HEAD_DIM]
            k_new = z_ref[:, MIXER_WIDTH + hd * HEAD_DIM:MIXER_WIDTH + (hd + 1) * HEAD_DIM]
            v_new = z_ref[:, 2 * MIXER_WIDTH + hd * HEAD_DIM:2 * MIXER_WIDTH + (hd + 1) * HEAD_DIM]
            kc = caches[g][:, 0, h, :]
            vc = caches[g][:, 1, h, :]
            s_c = jnp.sum(kc * q, axis=1, keepdims=True) * ATTN_SCALE + bcol_ref[g][:, h:h + 1]
            s_n = jnp.sum(k_new * q, axis=1, keepdims=True) * ATTN_SCALE + bnew_ref[:, hd:hd + 1]
            m = jnp.maximum(jnp.max(s_c, axis=0, keepdims=True), s_n)
            p_c = jnp.exp(s_c - m)
            p_n = jnp.exp(s_n - m)
            den = jnp.sum(p_c, axis=0, keepdims=True) + p_n
            outs[g][h] = (jnp.sum(p_c * vc, axis=0, keepdims=True) + p_n * v_new) / den
            lses[g][h] = m + jnp.log(den)
    for h in range(HEADS_PER_GROUP):
        ls = [lses[g][h] for g in range(N_SWA_GROUPS)]
        mx = jnp.maximum(jnp.maximum(ls[0], ls[1]), ls[2])
        es = [jnp.exp(l - mx) for l in ls]
        tot = es[0] + es[1] + es[2]
        for g in range(N_SWA_GROUPS):
            c0 = g * GROUP_WIDTH + h * HEAD_DIM
            o_ref[:, c0:c0 + HEAD_DIM] = outs[g][h] * (es[g] / tot)
    _sample_mem_attention(z_ref[:, 3 * MIXER_WIDTH:3 * MIXER_WIDTH + MEM_WIDTH], kv_ref, o_ref, MIXER_WIDTH)


def _sample_mix_b(z, win_caches, swa_layer, bias_groups, mem_kv, layer):
    bd = mem_kv.shape[1]
    width = z.shape[1]
    cache_views, cache_specs = [], []
    for g, (win, dil) in enumerate(SWA_PATTERN):
        c = win_caches[g]
        cache_views.append(c.reshape(c.shape[0], bd, win // dil, dil, 2, HEADS_PER_GROUP, HEAD_DIM))
        cache_specs.append(pl.BlockSpec((None, None, N_BACK, None, 2, HEADS_PER_GROUP, HEAD_DIM),
                                        lambda i: (swa_layer, i, 0, 0, 0, 0, 0)))
    bcol = jnp.stack([bg[:, N_BACK:0:-1].T for bg in bias_groups], axis=0)
    bnew = jnp.concatenate([bg[:, 0] for bg in bias_groups])
    bnew = jnp.pad(bnew, (0, HEAD_DIM - bnew.shape[0])).reshape(1, HEAD_DIM)
    return pl.pallas_call(
        _sample_mix_b_kernel,
        grid=(bd,),
        in_specs=[pl.BlockSpec((None, 1, width), lambda i: (i, 0, 0))] + cache_specs + [
            pl.BlockSpec((N_SWA_GROUPS, N_BACK, HEADS_PER_GROUP), lambda i: (0, 0, 0)),
            pl.BlockSpec((1, HEAD_DIM), lambda i: (0, 0)),
            _mem_kv_spec(layer, lambda i: i),
        ],
        out_specs=pl.BlockSpec((None, 1, D_MODEL), lambda i: (i, 0, 0)),
        out_shape=jax.ShapeDtypeStruct((bd, 1, D_MODEL), F32),
        compiler_params=_params("parallel"),
        name="sample_mix_b",
    )(z[:bd].reshape(bd, 1, width), *cache_views, bcol, bnew, mem_kv)


def _t5_bucket(dist):
    nf = jnp.maximum(dist, MAX_EXACT).astype(F32)
    large = MAX_EXACT + (jnp.log(nf / MAX_EXACT) / math.log(MAX_DISTANCE / MAX_EXACT)
                         * (N_BUCKETS - MAX_EXACT)).astype(jnp.int32)
    large = jnp.minimum(large, N_BUCKETS - 1)
    return jnp.where(dist < MAX_EXACT, dist, large)


def _group_bias(rel_bias, g, dil):
    dist = jnp.arange(N_BACK + 1, dtype=jnp.int32) * dil
    b = rel_bias[_t5_bucket(dist)][:, g * HEADS_PER_GROUP:(g + 1) * HEADS_PER_GROUP]
    return b.T.astype(F32)


def _band_tables_kernel(b_ref, o_ref):
    n, rows, width = o_ref.shape
    for x in range(n):
        row = jnp.broadcast_to(b_ref[x:x + 1, :], (rows, width))
        o_ref[x] = pltpu.roll(row, 0, 1, stride=1, stride_axis=0)


def _band_tables(bias_groups):
    width = 2 * N_BACK
    rows = []
    for bias_j in bias_groups:
        masked = jnp.full((HEADS_PER_GROUP, N_BACK - 1), NEG_INF, F32)
        rows.append(jnp.concatenate([bias_j[:, :1], masked, bias_j[:, N_BACK:0:-1]], axis=1))
        rows.append(jnp.concatenate([bias_j[:, ::-1], masked], axis=1))
    base = jnp.stack(rows, axis=0).reshape(-1, width)
    tabs = pl.pallas_call(
        _band_tables_kernel,
        out_shape=jax.ShapeDtypeStruct((base.shape[0], N_BACK, width), F32),
        name="band_tables",
    )(base)
    return tabs.reshape(N_SWA_GROUPS, 2, HEADS_PER_GROUP, N_BACK, width)


def _kv_tail_kernel(k_ref, v_ref, o_ref, *, dil):
    rows = k_ref.shape[0]
    for kv, ref in enumerate((k_ref, v_ref)):
        x = ref[...]
        if dil > 1:
            inv = _residue_major_perm(PERM_BLOCK, dil, transpose=True)
            x = jnp.concatenate([jnp.dot(inv, x[s:s + PERM_BLOCK, :], preferred_element_type=F32)
                                 for s in range(0, rows, PERM_BLOCK)], axis=0)
        x = x.astype(F32)
        for h in range(HEADS_PER_GROUP):
            o_ref[:, kv, h, :] = x[:, h * HEAD_DIM:(h + 1) * HEAD_DIM]


def _kv_tail(zb, g, *, batch, seq):
    win, dil = SWA_PATTERN[g]
    rows = min(win, FFN_ROW_TILE)
    first = (seq - win) // rows
    per_batch = seq // rows

    def tile(t):
        return pl.BlockSpec((None, rows, GROUP_WIDTH), lambda b, s: (t, b * per_batch + first + s, 0))

    return pl.pallas_call(
        functools.partial(_kv_tail_kernel, dil=dil),
        grid=(batch, win // rows),
        in_specs=[tile(N_SWA_GROUPS + g), tile(2 * N_SWA_GROUPS + g)],
        out_specs=pl.BlockSpec((None, rows, 2, HEADS_PER_GROUP, HEAD_DIM), lambda b, s: (b, s, 0, 0, 0)),
        out_shape=jax.ShapeDtypeStruct((batch, win, 2, HEADS_PER_GROUP, HEAD_DIM), F32),
        compiler_params=_params("parallel", "parallel"),
        name=f"kv_tail{g}",
    )(zb, zb)


def kernel(x_prompt, x_sample, mem_prompt, cache_mem_kv, cache_win128_kv, cache_win512_kv, cache_win2048_kv, rel_bias, norm_mix_pre, norm_mix_post, norm_ffn_pre, norm_ffn_post, norm_mem, w_mem_kv, w_in_a, norm_v_a, w_spatial_a, b_spatial_a, w_in_b, w_out, w_ffn_up, w_ffn_down):
    batch, seq, _ = x_prompt.shape
    bd = x_sample.shape[0]
    depth = w_out.shape[0]
    m_p = batch * seq
    win_caches = (cache_win128_kv, cache_win512_kv, cache_win2048_kv)

    bias_groups = [_group_bias(rel_bias, g, dil) for g, (_, dil) in enumerate(SWA_PATTERN)]
    band_tables = _band_tables(bias_groups)

    yp = x_prompt.reshape(m_p, D_MODEL)
    ys = jnp.pad(x_sample.reshape(bd, D_MODEL), ((0, SAMPLE_PAD - bd), (0, 0)))
    mem_rows = mem_prompt.reshape(batch * N_MEM, D_MODEL)

    mem_kv_p = _mem_kv(mem_rows, norm_mem, w_mem_kv, batch=batch)
    chunk_v_s = []
    win_p = [[] for _ in SWA_PATTERN]
    win_s = [[] for _ in SWA_PATTERN]
    for i in range(depth):
        li = i // 2
        if i % 2 == 0:
            zp, zs = _in_proj_a(yp, ys, norm_mix_pre[i], w_in_a, li, tm=ROW_TILE, tn=CAST_TILE,
                                gelu_cols=2 * MIXER_WIDTH)
            mix_s, v_rows = _sample_mix_a(zs, norm_v_a[li], w_spatial_a[li], b_spatial_a[li], cache_mem_kv, i)
            chunk_v_s.append(v_rows)
            ys, w_o = _sample_out_proj(mix_s, w_out, i, ys, norm_mix_post[i])
            yp = _gmlp_mix_out(zp, norm_v_a[li], w_spatial_a[li], b_spatial_a[li], mem_kv_p, i,
                               w_o, yp, norm_mix_post[i], tm=ROW_TILE, rows_per_batch=seq)
        else:
            zb, zs = _in_proj_b(yp, ys, norm_mix_pre[i], w_in_b, li, tm=ROW_TILE)
            outs, lses = [], []
            for g, (win, dil) in enumerate(SWA_PATTERN):
                o, lse = _swa_group(zb, band_tables, g, batch=batch, seq=seq)
                outs.append(o)
                lses.append(lse)
                win_p[g].append(_kv_tail(zb, g, batch=batch, seq=seq))
                kv_new = zs[:bd, MIXER_WIDTH:3 * MIXER_WIDTH]
                kv_new = kv_new.reshape(bd, 1, 2, N_SWA_GROUPS, HEADS_PER_GROUP, HEAD_DIM)[:, :, :, g]
                win_s[g].append(kv_new)
            mix_s = _sample_mix_b(zs, win_caches, li, bias_groups, cache_mem_kv, i)
            ys, w_o = _sample_out_proj(mix_s, w_out, i, ys, norm_mix_post[i])
            yp = _swa_merge_out(outs, lses, zb, mem_kv_p, i, w_o, yp, norm_mix_post[i], rows_per_batch=seq)
        yp, ys, w_ffn = _ffn_head(yp, ys, norm_ffn_pre[i], norm_ffn_post[i], w_ffn_up, w_ffn_down, i,
                                  tm=FFN_ROW_TILE, tf=HEAD_FF_TILE)
        yp = _ffn(yp, norm_ffn_pre[i], norm_ffn_post[i], w_ffn, tm=FFN_ROW_TILE, tf=FF_TILE, first_tile=1)

    return (
        yp.reshape(batch, seq, D_MODEL),
        ys[:bd].reshape(bd, 1, D_MODEL),
        mem_kv_p,
        jnp.stack(chunk_v_s, axis=0),
        jnp.stack(win_p[0], axis=0),
        jnp.stack(win_p[1], axis=0),
        jnp.stack(win_p[2], axis=0),
        jnp.stack(win_s[0], axis=0),
        jnp.stack(win_s[1], axis=0),
        jnp.stack(win_s[2], axis=0),
    )
```

```python
import functools
import math

import jax
import jax.numpy as jnp
from jax import lax
from jax.experimental import pallas as pl
from jax.experimental.pallas import tpu as pltpu

F32 = jnp.float32
BF16 = jnp.bfloat16

D_MODEL = 2048
HEAD_DIM = 128
N_MEM = 256
N_MEM_HEADS = 4
MEM_WIDTH = N_MEM_HEADS * HEAD_DIM
MIXER_WIDTH = D_MODEL - MEM_WIDTH
CHUNK = 128
N_GROUPS_A = 4
GROUP_DIM_A = MIXER_WIDTH // N_GROUPS_A
SWA_PATTERN = ((128, 1), (512, 4), (2048, 16))
N_SWA_GROUPS = len(SWA_PATTERN)
HEADS_PER_GROUP = 4
GROUP_WIDTH = HEADS_PER_GROUP * HEAD_DIM
N_BACK = 128
N_BUCKETS = 32
MAX_EXACT = N_BUCKETS // 2
MAX_DISTANCE = 2048
D_FF = 5632
EPS = 1e-6
NEG_INF = -1e30
ATTN_SCALE = HEAD_DIM ** -0.5
SAMPLE_PAD = 16
PERM_BLOCK = 256
OUT_CHUNK = 256

ROW_TILE = 512
FFN_ROW_TILE = 1024
FF_TILE = 512
HEAD_FF_TILE = 256
CAST_TILE = 512

V7X_VMEM_BYTES = 64 * 1024 * 1024
VMEM_LIMIT = V7X_VMEM_BYTES - 8 * 1024 * 1024


def _params(*sem):
    return pltpu.CompilerParams(dimension_semantics=sem, vmem_limit_bytes=VMEM_LIMIT)


def _gelu(x):
    return 0.5 * x * (1.0 + jnp.tanh(0.7978845608028654 * (x + 0.044715 * (x * x * x))))


def _rms(x, g):
    return x * lax.rsqrt(jnp.mean(x * x, axis=-1, keepdims=True) + EPS) * g


def _log2(n):
    assert n & (n - 1) == 0
    return n.bit_length() - 1


def _residue_major_perm(tm, dil, transpose=False):
    n = tm // dil
    row = lax.broadcasted_iota(jnp.int32, (tm, tm), 0)
    col = lax.broadcasted_iota(jnp.int32, (tm, tm), 1)
    dst, src = (col, row) if transpose else (row, col)
    want = lax.shift_left(jnp.bitwise_and(dst, n - 1), _log2(dil)) + lax.shift_right_logical(dst, _log2(n))
    return (src == want).astype(BF16)


def _in_proj_a_kernel(x_ref, xs_ref, g_ref, w_ref, zp_ref, zs_ref, wb_ref, xn_ref, *, n_col, gelu_cols):
    tm = x_ref.shape[0]
    tn = w_ref.shape[1]
    s = pl.program_id(0)

    def prompt_cols(acc, c0):
        col = lax.broadcasted_iota(jnp.int32, acc.shape, 1) + c0
        return jnp.where(col < gelu_cols, _gelu(acc), acc * ATTN_SCALE).astype(zp_ref.dtype)

    @pl.when(s == 0)
    def _():
        xn_ref[:tm, :] = _rms(x_ref[...], g_ref[...]).astype(BF16)
        xn_ref[tm:, :] = _rms(xs_ref[...], g_ref[...]).astype(BF16)

    for j in range(n_col):
        @pl.when(s == j)
        def _(j=j):
            cols = slice(j * tn, (j + 1) * tn)
            w = w_ref[...].astype(BF16)
            wb_ref[:, cols] = w
            acc = jnp.dot(xn_ref[...], w, preferred_element_type=F32)
            zp_ref[:, cols] = prompt_cols(acc[:tm, :], j * tn)
            sample = acc[tm:, :]
            zs_ref[:, cols] = _gelu(sample) if (j + 1) * tn <= gelu_cols else sample

    @pl.when(s >= n_col)
    def _():
        xn = _rms(x_ref[...], g_ref[...]).astype(BF16)
        acc = jnp.dot(xn, wb_ref[...], preferred_element_type=F32)
        zp_ref[...] = prompt_cols(acc, 0)


def _in_proj_a(x, xs, g, w, layer, *, tm, tn, gelu_cols):
    m, k = x.shape
    ms = xs.shape[0]
    n = w.shape[2]
    n_col, n_row = n // tn, m // tm
    assert gelu_cols % tn == 0
    row = lambda s: jnp.maximum(s - (n_col - 1), 0)
    return pl.pallas_call(
        functools.partial(_in_proj_a_kernel, n_col=n_col, gelu_cols=gelu_cols),
        grid=(n_col + n_row - 1,),
        in_specs=[
            pl.BlockSpec((tm, k), lambda s: (row(s), 0)),
            pl.BlockSpec((ms, k), lambda s: (0, 0)),
            pl.BlockSpec((1, k), lambda s: (0, 0)),
            pl.BlockSpec((None, k, tn), lambda s: (layer, 0, jnp.minimum(s, n_col - 1))),
        ],
        out_specs=[
            pl.BlockSpec((tm, n), lambda s: (row(s), 0)),
            pl.BlockSpec((ms, n), lambda s: (0, 0)),
        ],
        out_shape=[
            jax.ShapeDtypeStruct((m, n), BF16),
            jax.ShapeDtypeStruct((ms, n), F32),
        ],
        scratch_shapes=[pltpu.VMEM((k, n), BF16), pltpu.VMEM((tm + ms, k), BF16)],
        compiler_params=_params("arbitrary"),
        name="in_proj_a",
    )(x, xs, g.reshape(1, k), w)


def _mem_kv_kernel(x_ref, g_ref, w_ref, o_ref):
    xn = _rms(x_ref[...], g_ref[...]).astype(BF16)
    acc = jnp.dot(xn, w_ref[...].astype(BF16), preferred_element_type=F32)
    for kv in range(2):
        for h in range(N_MEM_HEADS):
            c0 = (kv * N_MEM_HEADS + h) * HEAD_DIM
            o_ref[:, kv, h, :] = acc[:, c0:c0 + HEAD_DIM]


def _mem_kv(mem_rows, g, w, *, batch):
    m, k = mem_rows.shape
    layers = w.shape[0]
    out = pl.pallas_call(
        _mem_kv_kernel,
        grid=(layers,),
        in_specs=[
            pl.BlockSpec((m, k), lambda l: (0, 0)),
            pl.BlockSpec((None, 1, k), lambda l: (l, 0, 0)),
            pl.BlockSpec((None, k, 2 * MEM_WIDTH), lambda l: (l, 0, 0)),
        ],
        out_specs=pl.BlockSpec((None, m, 2, N_MEM_HEADS, HEAD_DIM), lambda l: (l, 0, 0, 0, 0)),
        out_shape=jax.ShapeDtypeStruct((layers, m, 2, N_MEM_HEADS, HEAD_DIM), F32),
        compiler_params=_params("parallel"),
        name="mem_kv",
    )(mem_rows, g.reshape(layers, 1, k), w)
    return out.reshape(layers, batch, m // batch, 2, N_MEM_HEADS, HEAD_DIM)


def _in_proj_b_kernel(x_ref, xs_ref, g_ref, w_ref, o_ref, zs_ref, wb_ref, xn_ref):
    tm = x_ref.shape[0]
    n_tiles = o_ref.shape[0]
    n_qkv = 3 * N_SWA_GROUPS
    s = pl.program_id(0)

    def row_orders():
        xn = _rms(x_ref[...], g_ref[...]).astype(BF16)
        xn_ref[0, :tm, :] = xn
        for g in range(1, N_SWA_GROUPS):
            perm = _residue_major_perm(PERM_BLOCK, SWA_PATTERN[g][1])
            for r in range(0, tm, PERM_BLOCK):
                xn_ref[g, r:r + PERM_BLOCK, :] = jnp.dot(
                    perm, xn[r:r + PERM_BLOCK, :], preferred_element_type=F32).astype(BF16)

    def prompt_tile(t, acc):
        if t < N_SWA_GROUPS or t == n_qkv:
            acc = acc * ATTN_SCALE
        o_ref[t] = acc.astype(o_ref.dtype)

    src = lambda t: t % N_SWA_GROUPS if t < n_qkv else 0

    @pl.when(s == 0)
    def _():
        row_orders()
        xs = _rms(xs_ref[...], g_ref[...]).astype(BF16)
        for g in range(N_SWA_GROUPS):
            xn_ref[g, tm:, :] = xs

    for t in range(n_tiles):
        @pl.when(s == t)
        def _(t=t):
            cols = slice(t * GROUP_WIDTH, (t + 1) * GROUP_WIDTH)
            w = w_ref[...].astype(BF16)
            wb_ref[:, cols] = w
            acc = jnp.dot(xn_ref[src(t)], w, preferred_element_type=F32)
            prompt_tile(t, acc[:tm, :])
            zs_ref[:, cols] = acc[tm:, :]

    @pl.when(s >= n_tiles)
    def _():
        row_orders()
        for t in range(n_tiles):
            w = wb_ref[:, t * GROUP_WIDTH:(t + 1) * GROUP_WIDTH]
            prompt_tile(t, jnp.dot(xn_ref[src(t), :tm, :], w, preferred_element_type=F32))


def _in_proj_b(x, xs, g, w, layer, *, tm):
    m, k = x.shape
    ms = xs.shape[0]
    n = w.shape[2]
    n_tiles = n // GROUP_WIDTH
    row = lambda s: jnp.maximum(s - (n_tiles - 1), 0)
    return pl.pallas_call(
        _in_proj_b_kernel,
        grid=(n_tiles + m // tm - 1,),
        in_specs=[
            pl.BlockSpec((tm, k), lambda s: (row(s), 0)),
            pl.BlockSpec((ms, k), lambda s: (0, 0)),
            pl.BlockSpec((1, k), lambda s: (0, 0)),
            pl.BlockSpec((None, k, GROUP_WIDTH), lambda s: (layer, 0, jnp.minimum(s, n_tiles - 1))),
        ],
        out_specs=[
            pl.BlockSpec((n_tiles, tm, GROUP_WIDTH), lambda s: (0, row(s), 0)),
            pl.BlockSpec((ms, n), lambda s: (0, 0)),
        ],
        out_shape=[
            jax.ShapeDtypeStruct((n_tiles, m, GROUP_WIDTH), BF16),
            jax.ShapeDtypeStruct((ms, n), F32),
        ],
        scratch_shapes=[pltpu.VMEM((k, n), BF16), pltpu.VMEM((N_SWA_GROUPS, tm + ms, k), BF16)],
        compiler_params=_params("arbitrary"),
        name="in_proj_b",
    )(x, xs, g.reshape(1, k), w)


MEM_KV_SCRATCH = [pltpu.VMEM((N_MEM_HEADS, N_MEM, HEAD_DIM), BF16),
                  pltpu.VMEM((N_MEM_HEADS, N_MEM, 2 * HEAD_DIM), BF16)]


def _prepare_mem_kv(kv_ref, kb_ref, vb_ref):
    for h in range(N_MEM_HEADS):
        kb_ref[h] = kv_ref[:, 0, h, :].astype(BF16)
        vb_ref[h, :, :HEAD_DIM] = kv_ref[:, 1, h, :].astype(BF16)
        vb_ref[h, :, HEAD_DIM:] = jnp.ones((N_MEM, HEAD_DIM), BF16)


def _mem_attention_parts(q_ref, kb_ref, vb_ref, o_ref, col0):
    def head(h):
        lo, hi = h * HEAD_DIM, (h + 1) * HEAD_DIM
        s = lax.dot_general(q_ref[:, lo:hi], kb_ref[h], (((1,), (1,)), ((), ())), preferred_element_type=F32)
        m = jnp.max(s, axis=1, keepdims=True)
        p = jnp.exp(s - m).astype(BF16)
        ov = jnp.dot(p, vb_ref[h], preferred_element_type=F32)
        o_ref[:, col0 + lo:col0 + hi] = (ov[:, :HEAD_DIM] / ov[:, HEAD_DIM:]).astype(o_ref.dtype)

    return [functools.partial(head, h) for h in range(N_MEM_HEADS)]


def _gmlp_stage(u_ref, v_ref, q_ref, gv_ref, ws_ref, bs_ref, kb_ref, vb_ref, vn_ref, o_ref):
    tm = u_ref.shape[0]

    def norm_v():
        vn_ref[...] = _rms(v_ref[...].astype(F32), gv_ref[...]).astype(BF16)

    def group(g):
        row = lax.broadcasted_iota(jnp.int32, (CHUNK, CHUNK), 0)
        col = lax.broadcasted_iota(jnp.int32, (CHUNK, CHUNK), 1)
        w = jnp.where(row >= col, ws_ref[g], 0.0).astype(BF16)
        b = bs_ref[:, g:g + 1]
        c0, c1 = g * GROUP_DIM_A, (g + 1) * GROUP_DIM_A
        for c in range(tm // CHUNK):
            r0, r1 = c * CHUNK, (c + 1) * CHUNK
            s = jnp.dot(w, vn_ref[r0:r1, c0:c1], preferred_element_type=F32) + b
            o_ref[r0:r1, c0:c1] = (u_ref[r0:r1, c0:c1].astype(F32) * s).astype(o_ref.dtype)

    return ([norm_v] + [functools.partial(group, g) for g in range(N_GROUPS_A)]
            + _mem_attention_parts(q_ref, kb_ref, vb_ref, o_ref, MIXER_WIDTH))


def _mem_kv_spec(layer, batch_of):
    return pl.BlockSpec((None, None, N_MEM, 2, N_MEM_HEADS, HEAD_DIM),
                        lambda i: (layer, batch_of(i), 0, 0, 0, 0))


def _mix_out_kernel(*refs, stage, n_in, n_tiles, tiles_per_batch):
    mix_in = refs[:n_in]
    kv_ref, w_ref, x_ref, g_ref, o_ref, buf0_ref, buf1_ref, acc_ref, kb_ref, vb_ref = refs[n_in:n_in + 10]
    extra = refs[n_in + 10:]
    s = pl.program_id(0)

    @pl.when(s == 0)
    def _():
        buf1_ref[...] = jnp.zeros_like(buf1_ref)

    @pl.when(jnp.minimum(s, n_tiles - 1) % tiles_per_batch == 0)
    def _():
        _prepare_mem_kv(kv_ref, kb_ref, vb_ref)

    def step(dst_ref, src_ref):
        parts = stage(*mix_in, kb_ref, vb_ref, *extra, dst_ref)
        n_chunks = D_MODEL // OUT_CHUNK
        for c in range(n_chunks):
            cols = slice(c * OUT_CHUNK, (c + 1) * OUT_CHUNK)
            acc_ref[:, cols] = jnp.dot(src_ref[...], w_ref[:, cols], preferred_element_type=F32)
            for part in parts[c * len(parts) // n_chunks:(c + 1) * len(parts) // n_chunks]:
                part()
        o_ref[...] = x_ref[...] + _rms(acc_ref[...], g_ref[...])

    @pl.when(s % 2 == 0)
    def _():
        step(buf0_ref, buf1_ref)

    @pl.when(s % 2 == 1)
    def _():
        step(buf1_ref, buf0_ref)


def _mix_out(stage, mix_inputs, mix_specs, extra_scratch, mem_kv, layer, w, x, g, *, tm, rows_per_batch, name):
    m = x.shape[0]
    n_tiles = m // tm
    tiles_per_batch = rows_per_batch // tm
    mix_tile = lambda s: jnp.minimum(s, n_tiles - 1)
    out_tile = lambda s: jnp.maximum(s - 1, 0)
    row_spec = pl.BlockSpec((tm, D_MODEL), lambda s: (out_tile(s), 0))
    return pl.pallas_call(
        functools.partial(_mix_out_kernel, stage=stage, n_in=len(mix_inputs), n_tiles=n_tiles,
                          tiles_per_batch=tiles_per_batch),
        grid=(n_tiles + 1,),
        in_specs=[spec(mix_tile) for spec in mix_specs] + [
            _mem_kv_spec(layer, lambda s: mix_tile(s) // tiles_per_batch),
            pl.BlockSpec((D_MODEL, D_MODEL), lambda s: (0, 0), pipeline_mode=pl.Buffered(1)),
            row_spec,
            pl.BlockSpec((1, D_MODEL), lambda s: (0, 0)),
        ],
        out_specs=row_spec,
        out_shape=jax.ShapeDtypeStruct((m, D_MODEL), F32),
        scratch_shapes=[pltpu.VMEM((tm, D_MODEL), BF16), pltpu.VMEM((tm, D_MODEL), BF16),
                        pltpu.VMEM((tm, D_MODEL), F32)] + MEM_KV_SCRATCH + list(extra_scratch),
        compiler_params=_params("arbitrary"),
        name=name,
    )(*mix_inputs, mem_kv, w, x, g.reshape(1, D_MODEL))


def _gmlp_mix_out(zact, g_v, w_s, b_s, mem_kv, layer, w, x, g, *, tm, rows_per_batch):
    const = lambda shape: (lambda tile: pl.BlockSpec(shape, lambda s: (0,) * len(shape)))
    specs = [
        lambda tile: pl.BlockSpec((tm, MIXER_WIDTH), lambda s: (tile(s), 0)),
        lambda tile: pl.BlockSpec((tm, MIXER_WIDTH), lambda s: (tile(s), 1)),
        lambda tile: pl.BlockSpec((tm, MEM_WIDTH), lambda s: (tile(s), 2 * MIXER_WIDTH // MEM_WIDTH)),
        const((1, MIXER_WIDTH)),
        const((N_GROUPS_A, CHUNK, CHUNK)),
        const((CHUNK, N_GROUPS_A)),
    ]
    return _mix_out(_gmlp_stage, (zact, zact, zact, g_v.reshape(1, MIXER_WIDTH), w_s, b_s.T), specs,
                    [pltpu.VMEM((tm, MIXER_WIDTH), BF16)], mem_kv, layer, w, x, g,
                    tm=tm, rows_per_batch=rows_per_batch, name="gmlp_mix_out")


def _out_proj_cast_kernel(mix_ref, w_ref, x_ref, g_ref, o_ref, wb_ref, acc_ref):
    j = pl.program_id(0)
    w = w_ref[...].astype(BF16)
    wb_ref[...] = w
    acc_ref[j] = jnp.dot(mix_ref[...], w, preferred_element_type=F32)

    @pl.when(j == pl.num_programs(0) - 1)
    def _():
        o = jnp.concatenate([acc_ref[t] for t in range(acc_ref.shape[0])], axis=1)
        o_ref[...] = x_ref[...] + _rms(o, g_ref[...])


def _sample_out_proj(mix, w, layer, x, g):
    bd = mix.shape[0]
    mix = jnp.pad(mix.reshape(bd, D_MODEL), ((0, x.shape[0] - bd), (0, 0))).astype(BF16)
    return _out_proj_cast(mix, w, layer, x, g, tn=CAST_TILE)


def _out_proj_cast(mix, w, layer, x, g, *, tn):
    m = x.shape[0]
    n_tiles = D_MODEL // tn
    return pl.pallas_call(
        _out_proj_cast_kernel,
        grid=(n_tiles,),
        in_specs=[
            pl.BlockSpec((m, D_MODEL), lambda j: (0, 0)),
            pl.BlockSpec((None, D_MODEL, tn), lambda j: (layer, 0, j)),
            pl.BlockSpec((m, D_MODEL), lambda j: (0, 0)),
            pl.BlockSpec((1, D_MODEL), lambda j: (0, 0)),
        ],
        out_specs=[
            pl.BlockSpec((m, D_MODEL), lambda j: (0, 0)),
            pl.BlockSpec((D_MODEL, tn), lambda j: (0, j)),
        ],
        out_shape=[
            jax.ShapeDtypeStruct((m, D_MODEL), F32),
            jax.ShapeDtypeStruct((D_MODEL, D_MODEL), BF16),
        ],
        scratch_shapes=[pltpu.VMEM((n_tiles, m, tn), F32)],
        compiler_params=_params("arbitrary"),
        name="out_proj_cast",
    )(mix, w, x, g.reshape(1, D_MODEL))


def _accumulate(o_ref, ssq_ref, part):
    new = o_ref[...] + part
    o_ref[...] = new
    sq = new * new
    ssq_ref[...] = functools.reduce(
        jnp.add, [sq[:, c:c + HEAD_DIM] for c in range(0, sq.shape[1], HEAD_DIM)])


def _residual_norm(x_ref, o_ref, ssq_ref, g_ref):
    ms = jnp.sum(ssq_ref[...], axis=-1, keepdims=True) * (1.0 / o_ref.shape[1])
    o_ref[...] = x_ref[...] + o_ref[...] * lax.rsqrt(ms + EPS) * g_ref[...]


def _ffn_kernel(x_ref, gpre_ref, gpost_ref, wg_ref, wl_ref, wd_ref, o_ref, xn_ref, ssq_ref):
    f = pl.program_id(1)

    @pl.when(f == 0)
    def _():
        xn_ref[...] = _rms(x_ref[...], gpre_ref[...]).astype(BF16)
        o_ref[...] = jnp.zeros_like(o_ref)

    xn = xn_ref[...]
    hg = jnp.dot(xn, wg_ref[...], preferred_element_type=F32)
    hl = jnp.dot(xn, wl_ref[...], preferred_element_type=F32)
    a = (hg * jax.nn.sigmoid(hg) * hl).astype(BF16)
    _accumulate(o_ref, ssq_ref, jnp.dot(a, wd_ref[...], preferred_element_type=F32))

    @pl.when(f == pl.num_programs(1) - 1)
    def _():
        _residual_norm(x_ref, o_ref, ssq_ref, gpost_ref)


def _ffn(x, g_pre, g_post, weights, *, tm, tf, first_tile):
    m = x.shape[0]
    nf = D_FF // tf
    rows = pl.BlockSpec((tm, D_MODEL), lambda i, f: (i + first_tile, 0))
    return pl.pallas_call(
        _ffn_kernel,
        grid=(m // tm - first_tile, nf),
        in_specs=[
            rows,
            pl.BlockSpec((1, D_MODEL), lambda i, f: (0, 0)),
            pl.BlockSpec((1, D_MODEL), lambda i, f: (0, 0)),
            pl.BlockSpec((D_MODEL, tf), lambda i, f: (0, f)),
            pl.BlockSpec((D_MODEL, tf), lambda i, f: (0, f)),
            pl.BlockSpec((tf, D_MODEL), lambda i, f: (f, 0)),
        ],
        out_specs=rows,
        out_shape=jax.ShapeDtypeStruct((m, D_MODEL), F32),
        input_output_aliases={0: 0},
        scratch_shapes=[pltpu.VMEM((tm, D_MODEL), BF16), pltpu.VMEM((tm, HEAD_DIM), F32)],
        compiler_params=_params("parallel", "arbitrary"),
        name="ffn",
    )(x, g_pre.reshape(1, D_MODEL), g_post.reshape(1, D_MODEL), *weights)


def _ffn_head_kernel(x_ref, xs_ref, gpre_ref, gpost_ref, wg_ref, wl_ref, wd_ref,
                     o_ref, os_ref, wgb_ref, wlb_ref, wdb_ref, xn_ref, ssq_ref, ssqs_ref):
    tm = x_ref.shape[0]
    f = pl.program_id(0)

    @pl.when(f == 0)
    def _():
        xn_ref[:tm, :] = _rms(x_ref[...], gpre_ref[...]).astype(BF16)
        xn_ref[tm:, :] = _rms(xs_ref[...], gpre_ref[...]).astype(BF16)
        o_ref[...] = jnp.zeros_like(o_ref)
        os_ref[...] = jnp.zeros_like(os_ref)

    wg, wl, wd = (r[...].astype(BF16) for r in (wg_ref, wl_ref, wd_ref))
    wgb_ref[...] = wg
    wlb_ref[...] = wl
    wdb_ref[...] = wd
    xn = xn_ref[...]
    hg = jnp.dot(xn, wg, preferred_element_type=F32)
    hl = jnp.dot(xn, wl, preferred_element_type=F32)
    a = (hg * jax.nn.sigmoid(hg) * hl).astype(BF16)
    part = jnp.dot(a, wd, preferred_element_type=F32)
    _accumulate(o_ref, ssq_ref, part[:tm, :])
    _accumulate(os_ref, ssqs_ref, part[tm:, :])

    @pl.when(f == pl.num_programs(0) - 1)
    def _():
        _residual_norm(x_ref, o_ref, ssq_ref, gpost_ref)
        _residual_norm(xs_ref, os_ref, ssqs_ref, gpost_ref)


def _ffn_head(x, xs, g_pre, g_post, w_up, w_down, layer, *, tm, tf):
    m = x.shape[0]
    ms = xs.shape[0]
    nf = D_FF // tf
    once = dict(pipeline_mode=pl.Buffered(1))
    head = pl.BlockSpec((tm, D_MODEL), lambda f: (0, 0), **once)
    sample = pl.BlockSpec((ms, D_MODEL), lambda f: (0, 0))
    vec = pl.BlockSpec((1, D_MODEL), lambda f: (0, 0))
    outs = pl.pallas_call(
        _ffn_head_kernel,
        grid=(nf,),
        in_specs=[
            head, sample, vec, vec,
            pl.BlockSpec((None, D_MODEL, tf), lambda f: (layer, 0, f)),
            pl.BlockSpec((None, D_MODEL, tf), lambda f: (layer, 0, nf + f)),
            pl.BlockSpec((None, tf, D_MODEL), lambda f: (layer, f, 0)),
        ],
        out_specs=[
            head, sample,
            pl.BlockSpec((D_MODEL, tf), lambda f: (0, f)),
            pl.BlockSpec((D_MODEL, tf), lambda f: (0, f)),
            pl.BlockSpec((tf, D_MODEL), lambda f: (f, 0)),
        ],
        out_shape=[
            jax.ShapeDtypeStruct((m, D_MODEL), F32),
            jax.ShapeDtypeStruct((ms, D_MODEL), F32),
            jax.ShapeDtypeStruct((D_MODEL, D_FF), BF16),
            jax.ShapeDtypeStruct((D_MODEL, D_FF), BF16),
            jax.ShapeDtypeStruct((D_FF, D_MODEL), BF16),
        ],
        input_output_aliases={0: 0},
        scratch_shapes=[pltpu.VMEM((tm + ms, D_MODEL), BF16), pltpu.VMEM((tm, HEAD_DIM), F32),
                        pltpu.VMEM((ms, HEAD_DIM), F32)],
        compiler_params=_params("arbitrary"),
        name="ffn_head",
    )(x, xs, g_pre.reshape(1, D_MODEL), g_post.reshape(1, D_MODEL), w_up, w_up, w_down)
    return outs[0], outs[1], tuple(outs[2:])


def _swa_kernel(q_ref, k_ref, v_ref, tb_ref, o_ref, lse_ref):
    n_units, n_res, u, _ = q_ref.shape
    per_blk = N_BACK // u
    n_blk = n_units // per_blk
    lane = lax.broadcasted_iota(jnp.int32, (N_BACK, HEAD_DIM), 1)
    ones = jnp.ones((2 * N_BACK, HEAD_DIM), BF16)

    def rows(ref, res, unit0, n_rows, lo, hi):
        return ref[pl.ds(unit0, n_rows // u), res, :, lo:hi].reshape(n_rows, hi - lo)

    def block(res, qu, ku, table):
        n_keys = N_BACK if table == 0 else 2 * N_BACK
        lse_tile = jnp.zeros((N_BACK, HEAD_DIM), F32)
        for h in range(HEADS_PER_GROUP):
            lo, hi = h * HEAD_DIM, (h + 1) * HEAD_DIM
            q = rows(q_ref, res, qu, N_BACK, lo, hi)
            kw = rows(k_ref, res, ku, n_keys, lo, hi)
            vw = rows(v_ref, res, ku, n_keys, lo, hi)
            s = lax.dot_general(q, kw, (((1,), (1,)), ((), ())), preferred_element_type=F32)
            s = s + tb_ref[table, h][:, :n_keys]
            m = jnp.max(s, axis=1, keepdims=True)
            p = jnp.exp(s - m).astype(BF16)
            ov = jnp.dot(p, jnp.concatenate([vw, ones[:n_keys]], axis=1), preferred_element_type=F32)
            den = ov[:, HEAD_DIM:]
            o = ov[:, :HEAD_DIM] / den
            o_ref[pl.ds(qu, per_blk), res, :, lo:hi] = o.reshape(per_blk, u, HEAD_DIM).astype(o_ref.dtype)
            lse_tile = jnp.where(lane == h, m + jnp.log(den), lse_tile)
        lse_ref[pl.ds(qu, per_blk), res, :, :] = lse_tile.reshape(per_blk, u, HEAD_DIM)

    def block_at(res, n):
        block(res, n * per_blk, (n - 1) * per_blk, 1)

    n_pairs = (n_blk - 1) // 2
    for res in range(n_res):
        block(res, 0, 0, 0)

        def body(i, carry, res=res):
            block_at(res, 1 + 2 * i)
            block_at(res, 2 + 2 * i)
            return carry

        if n_pairs > 0:
            lax.fori_loop(0, n_pairs, body, 0)
        if (n_blk - 1) % 2 == 1:
            block_at(res, n_blk - 1)


def _sub_block(dil):
    return N_BACK if dil == 1 else PERM_BLOCK


def _swa_group(zb, tables, g, *, batch, seq):
    dil = SWA_PATTERN[g][1]
    sub = _sub_block(dil)
    n_units, u = seq // sub, sub // dil
    n_res = 4 if n_units * u == 2 * N_BACK else 1
    view = zb.reshape(zb.shape[0], batch, n_units, dil, u, GROUP_WIDTH)

    def rows_in(tile):
        return pl.BlockSpec((None, None, n_units, n_res, u, GROUP_WIDTH), lambda i, r: (tile, i, 0, r, 0, 0))

    def rows_out(width):
        return pl.BlockSpec((None, n_units, n_res, u, width), lambda i, r: (i, 0, r, 0, 0))

    return pl.pallas_call(
        _swa_kernel,
        grid=(batch, dil // n_res),
        in_specs=[rows_in(g), rows_in(N_SWA_GROUPS + g), rows_in(2 * N_SWA_GROUPS + g),
                  pl.BlockSpec((None, 2, HEADS_PER_GROUP, N_BACK, 2 * N_BACK), lambda i, r: (g, 0, 0, 0, 0))],
        out_specs=[rows_out(GROUP_WIDTH), rows_out(HEAD_DIM)],
        out_shape=[
            jax.ShapeDtypeStruct((batch, n_units, dil, u, GROUP_WIDTH), BF16),
            jax.ShapeDtypeStruct((batch, n_units, dil, u, HEAD_DIM), F32),
        ],
        compiler_params=_params("parallel", "parallel"),
        name=f"swa_group{g}",
    )(view, view, view, tables)


def _split3(x):
    hi = x.astype(BF16)
    rest = x - hi.astype(F32)
    mid = rest.astype(BF16)
    lo = (rest - mid.astype(F32)).astype(BF16)
    return hi, mid, lo


def _merge_stage(o0_ref, o1_ref, o2_ref, l0_ref, l1_ref, l2_ref, q_ref, kb_ref, vb_ref, o_ref):
    tm = o_ref.shape[0]
    group_refs = ((o0_ref, l0_ref), (o1_ref, l1_ref), (o2_ref, l2_ref))
    outs, lses = [None] * N_SWA_GROUPS, [None] * N_SWA_GROUPS

    def token_order(g):
        o_g, l_g = group_refs[g]
        dil = SWA_PATTERN[g][1]
        o = o_g[...].reshape(tm, GROUP_WIDTH)
        l = l_g[...].reshape(tm, HEAD_DIM)
        if dil > 1:
            inv = _residue_major_perm(PERM_BLOCK, dil, transpose=True)
            l3 = _split3(l)
            o_nat, l_nat = [], []
            for s in range(0, tm, PERM_BLOCK):
                o_nat.append(jnp.dot(inv, o[s:s + PERM_BLOCK, :], preferred_element_type=F32))
                l_nat.append(sum(jnp.dot(inv, t[s:s + PERM_BLOCK, :], preferred_element_type=F32) for t in l3))
            o = jnp.concatenate(o_nat, axis=0)
            l = jnp.concatenate(l_nat, axis=0)
        outs[g] = o.astype(F32)
        lses[g] = l

    def merge_head(h):
        lo, hi = h * HEAD_DIM, (h + 1) * HEAD_DIM
        ls = [l[:, h:h + 1] for l in lses]
        mx = jnp.maximum(jnp.maximum(ls[0], ls[1]), ls[2])
        es = [jnp.exp(l - mx) for l in ls]
        tot = es[0] + es[1] + es[2]
        for g in range(N_SWA_GROUPS):
            alpha = es[g] / tot
            o_ref[:, g * GROUP_WIDTH + lo:g * GROUP_WIDTH + hi] = (outs[g][:, lo:hi] * alpha).astype(o_ref.dtype)

    return ([functools.partial(token_order, g) for g in range(N_SWA_GROUPS)]
            + [functools.partial(merge_head, h) for h in range(HEADS_PER_GROUP)]
            + _mem_attention_parts(q_ref, kb_ref, vb_ref, o_ref, MIXER_WIDTH))


def _swa_merge_out(outs, lses, zb, mem_kv, layer, w, x, g, *, rows_per_batch):
    tm = ROW_TILE
    tiles_per_batch = rows_per_batch // tm

    def group_tile(width, dil):
        sub = _sub_block(dil)
        return lambda tile: pl.BlockSpec(
            (None, tm // sub, dil, sub // dil, width),
            lambda s: (tile(s) // tiles_per_batch, tile(s) % tiles_per_batch, 0, 0, 0))

    specs = ([group_tile(GROUP_WIDTH, dil) for _, dil in SWA_PATTERN]
             + [group_tile(HEAD_DIM, dil) for _, dil in SWA_PATTERN]
             + [lambda tile: pl.BlockSpec((None, tm, MEM_WIDTH), lambda s: (3 * N_SWA_GROUPS, tile(s), 0))])
    return _mix_out(_merge_stage, (*outs, *lses, zb), specs, [], mem_kv, layer, w, x, g,
                    tm=tm, rows_per_batch=rows_per_batch, name="swa_merge_out")


def _sample_mem_attention(q_row, kv_ref, o_ref, col0):
    for h in range(N_MEM_HEADS):
        lo, hi = h * HEAD_DIM, (h + 1) * HEAD_DIM
        q = q_row[:, lo:hi]
        k = kv_ref[:, 0, h, :]
        v = kv_ref[:, 1, h, :]
        s = jnp.sum(k * q, axis=1, keepdims=True) * ATTN_SCALE
        m = jnp.max(s, axis=0, keepdims=True)
        p = jnp.exp(s - m)
        den = jnp.sum(p, axis=0, keepdims=True)
        o_ref[:, col0 + lo:col0 + hi] = jnp.sum(p * v, axis=0, keepdims=True) / den


def _sample_mix_a_kernel(z_ref, gv_ref, w0_ref, b0_ref, kv_ref, o_ref, vrow_ref):
    u = z_ref[:, 0:MIXER_WIDTH]
    v = _rms(z_ref[:, MIXER_WIDTH:2 * MIXER_WIDTH], gv_ref[...])
    vrow_ref[...] = v
    o_ref[:, 0:MIXER_WIDTH] = u * (w0_ref[...] * v + b0_ref[...])
    _sample_mem_attention(z_ref[:, 2 * MIXER_WIDTH:2 * MIXER_WIDTH + MEM_WIDTH], kv_ref, o_ref, MIXER_WIDTH)


def _sample_mix_a(z, g_v, w_s, b_s, mem_kv, layer):
    bd = mem_kv.shape[1]
    w0 = jnp.repeat(w_s[:, 0, 0], GROUP_DIM_A).reshape(1, MIXER_WIDTH)
    b0 = jnp.repeat(b_s[:, 0], GROUP_DIM_A).reshape(1, MIXER_WIDTH)
    width = z.shape[1]
    vec = lambda i: (0, 0)
    return pl.pallas_call(
        _sample_mix_a_kernel,
        grid=(bd,),
        in_specs=[
            pl.BlockSpec((None, 1, width), lambda i: (i, 0, 0)),
            pl.BlockSpec((1, MIXER_WIDTH), vec),
            pl.BlockSpec((1, MIXER_WIDTH), vec),
            pl.BlockSpec((1, MIXER_WIDTH), vec),
            _mem_kv_spec(layer, lambda i: i),
        ],
        out_specs=[
            pl.BlockSpec((None, 1, D_MODEL), lambda i: (i, 0, 0)),
            pl.BlockSpec((None, 1, MIXER_WIDTH), lambda i: (i, 0, 0)),
        ],
        out_shape=[
            jax.ShapeDtypeStruct((bd, 1, D_MODEL), F32),
            jax.ShapeDtypeStruct((bd, 1, MIXER_WIDTH), F32),
        ],
        compiler_params=_params("parallel"),
        name="sample_mix_a",
    )(z[:bd].reshape(bd, 1, width), g_v.reshape(1, MIXER_WIDTH), w0, b0, mem_kv)


def _sample_mix_b_kernel(z_ref, c0_ref, c1_ref, c2_ref, bcol_ref, bnew_ref, kv_ref, o_ref):
    caches = (c0_ref, c1_ref, c2_ref)
    outs = [[None] * HEADS_PER_GROUP for _ in range(N_SWA_GROUPS)]
    lses = [[None] * HEADS_PER_GROUP for _ in range(N_SWA_GROUPS)]
    for g in range(N_SWA_GROUPS):
        for h in range(HEADS_PER_GROUP):
            hd = g * HEADS_PER_GROUP + h
            lo, hi = h * HEAD_DIM, (h + 1) * HEAD_DIM
            q = z_ref[:, hd * HEAD_DIM:(hd + 1) * HEAD_DIM]
            k_new = z_ref[:, MIXER_WIDTH + hd * HEAD_DIM:MIXER_WIDTH + (hd + 1) * HEAD_DIM]
            v_new = z_ref[:, 2 * MIXER_WIDTH + hd * HEAD_DIM:2 * MIXER_WIDTH + (hd + 1) * HEAD_DIM]
            kc = caches[g][:, 0, h, :]
            vc = caches[g][:, 1, h, :]
            s_c = jnp.sum(kc * q, axis=1, keepdims=True) * ATTN_SCALE + bcol_ref[g][:, h:h + 1]
            s_n = jnp.sum(k_new * q, axis=1, keepdims=True) * ATTN_SCALE + bnew_ref[:, hd:hd + 1]
            m = jnp.maximum(jnp.max(s_c, axis=0, keepdims=True), s_n)
            p_c = jnp.exp(s_c - m)
            p_n = jnp.exp(s_n - m)
            den = jnp.sum(p_c, axis=0, keepdims=True) + p_n
            outs[g][h] = (jnp.sum(p_c * vc, axis=0, keepdims=True) + p_n * v_new) / den
            lses[g][h] = m + jnp.log(den)
    for h in range(HEADS_PER_GROUP):
        ls = [lses[g][h] for g in range(N_SWA_GROUPS)]
        mx = jnp.maximum(jnp.maximum(ls[0], ls[1]), ls[2])
        es = [jnp.exp(l - mx) for l in ls]
        tot = es[0] + es[1] + es[2]
        for g in range(N_SWA_GROUPS):
            c0 = g * GROUP_WIDTH + h * HEAD_DIM
            o_ref[:, c0:c0 + HEAD_DIM] = outs[g][h] * (es[g] / tot)
    _sample_mem_attention(z_ref[:, 3 * MIXER_WIDTH:3 * MIXER_WIDTH + MEM_WIDTH], kv_ref, o_ref, MIXER_WIDTH)


def _sample_mix_b(z, win_caches, swa_layer, bias_groups, mem_kv, layer):
    bd = mem_kv.shape[1]
    width = z.shape[1]
    cache_views, cache_specs = [], []
    for g, (win, dil) in enumerate(SWA_PATTERN):
        c = win_caches[g]
        cache_views.append(c.reshape(c.shape[0], bd, win // dil, dil, 2, HEADS_PER_GROUP, HEAD_DIM))
        cache_specs.append(pl.BlockSpec((None, None, N_BACK, None, 2, HEADS_PER_GROUP, HEAD_DIM),
                                        lambda i: (swa_layer, i, 0, 0, 0, 0, 0)))
    bcol = jnp.stack([bg[:, N_BACK:0:-1].T for bg in bias_groups], axis=0)
    bnew = jnp.concatenate([bg[:, 0] for bg in bias_groups])
    bnew = jnp.pad(bnew, (0, HEAD_DIM - bnew.shape[0])).reshape(1, HEAD_DIM)
    return pl.pallas_call(
        _sample_mix_b_kernel,
        grid=(bd,),
        in_specs=[pl.BlockSpec((None, 1, width), lambda i: (i, 0, 0))] + cache_specs + [
            pl.BlockSpec((N_SWA_GROUPS, N_BACK, HEADS_PER_GROUP), lambda i: (0, 0, 0)),
            pl.BlockSpec((1, HEAD_DIM), lambda i: (0, 0)),
            _mem_kv_spec(layer, lambda i: i),
        ],
        out_specs=pl.BlockSpec((None, 1, D_MODEL), lambda i: (i, 0, 0)),
        out_shape=jax.ShapeDtypeStruct((bd, 1, D_MODEL), F32),
        compiler_params=_params("parallel"),
        name="sample_mix_b",
    )(z[:bd].reshape(bd, 1, width), *cache_views, bcol, bnew, mem_kv)


def _t5_bucket(dist):
    nf = jnp.maximum(dist, MAX_EXACT).astype(F32)
    large = MAX_EXACT + (jnp.log(nf / MAX_EXACT) / math.log(MAX_DISTANCE / MAX_EXACT)
                         * (N_BUCKETS - MAX_EXACT)).astype(jnp.int32)
    large = jnp.minimum(large, N_BUCKETS - 1)
    return jnp.where(dist < MAX_EXACT, dist, large)


def _group_bias(rel_bias, g, dil):
    dist = jnp.arange(N_BACK + 1, dtype=jnp.int32) * dil
    b = rel_bias[_t5_bucket(dist)][:, g * HEADS_PER_GROUP:(g + 1) * HEADS_PER_GROUP]
    return b.T.astype(F32)


def _band_tables_kernel(b_ref, o_ref):
    n, rows, width = o_ref.shape
    for x in range(n):
        row = jnp.broadcast_to(b_ref[x:x + 1, :], (rows, width))
        o_ref[x] = pltpu.roll(row, 0, 1, stride=1, stride_axis=0)


def _band_tables(bias_groups):
    width = 2 * N_BACK
    rows = []
    for bias_j in bias_groups:
        masked = jnp.full((HEADS_PER_GROUP, N_BACK - 1), NEG_INF, F32)
        rows.append(jnp.concatenate([bias_j[:, :1], masked, bias_j[:, N_BACK:0:-1]], axis=1))
        rows.append(jnp.concatenate([bias_j[:, ::-1], masked], axis=1))
    base = jnp.stack(rows, axis=0).reshape(-1, width)
    tabs = pl.pallas_call(
        _band_tables_kernel,
        out_shape=jax.ShapeDtypeStruct((base.shape[0], N_BACK, width), F32),
        name="band_tables",
    )(base)
    return tabs.reshape(N_SWA_GROUPS, 2, HEADS_PER_GROUP, N_BACK, width)


def _kv_tail_kernel(k_ref, v_ref, o_ref, *, dil):
    rows = k_ref.shape[0]
    for kv, ref in enumerate((k_ref, v_ref)):
        x = ref[...]
        if dil > 1:
            inv = _residue_major_perm(PERM_BLOCK, dil, transpose=True)
            x = jnp.concatenate([jnp.dot(inv, x[s:s + PERM_BLOCK, :], preferred_element_type=F32)
                                 for s in range(0, rows, PERM_BLOCK)], axis=0)
        x = x.astype(F32)
        for h in range(HEADS_PER_GROUP):
            o_ref[:, kv, h, :] = x[:, h * HEAD_DIM:(h + 1) * HEAD_DIM]


def _kv_tail(zb, g, *, batch, seq):
    win, dil = SWA_PATTERN[g]
    rows = min(win, FFN_ROW_TILE)
    first = (seq - win) // rows
    per_batch = seq // rows

    def tile(t):
        return pl.BlockSpec((None, rows, GROUP_WIDTH), lambda b, s: (t, b * per_batch + first + s, 0))

    return pl.pallas_call(
        functools.partial(_kv_tail_kernel, dil=dil),
        grid=(batch, win // rows),
        in_specs=[tile(N_SWA_GROUPS + g), tile(2 * N_SWA_GROUPS + g)],
        out_specs=pl.BlockSpec((None, rows, 2, HEADS_PER_GROUP, HEAD_DIM), lambda b, s: (b, s, 0, 0, 0)),
        out_shape=jax.ShapeDtypeStruct((batch, win, 2, HEADS_PER_GROUP, HEAD_DIM), F32),
        compiler_params=_params("parallel", "parallel"),
        name=f"kv_tail{g}",
    )(zb, zb)


def kernel(x_prompt, x_sample, mem_prompt, cache_mem_kv, cache_win128_kv, cache_win512_kv, cache_win2048_kv, rel_bias, norm_mix_pre, norm_mix_post, norm_ffn_pre, norm_ffn_post, norm_mem, w_mem_kv, w_in_a, norm_v_a, w_spatial_a, b_spatial_a, w_in_b, w_out, w_ffn_up, w_ffn_down):
    batch, seq, _ = x_prompt.shape
    bd = x_sample.shape[0]
    depth = w_out.shape[0]
    m_p = batch * seq
    win_caches = (cache_win128_kv, cache_win512_kv, cache_win2048_kv)

    bias_groups = [_group_bias(rel_bias, g, dil) for g, (_, dil) in enumerate(SWA_PATTERN)]
    band_tables = _band_tables(bias_groups)

    yp = x_prompt.reshape(m_p, D_MODEL)
    ys = jnp.pad(x_sample.reshape(bd, D_MODEL), ((0, SAMPLE_PAD - bd), (0, 0)))
    mem_rows = mem_prompt.reshape(batch * N_MEM, D_MODEL)

    mem_kv_p = _mem_kv(mem_rows, norm_mem, w_mem_kv, batch=batch)
    chunk_v_s = []
    win_p = [[] for _ in SWA_PATTERN]
    win_s = [[] for _ in SWA_PATTERN]
    for i in range(depth):
        li = i // 2
        if i % 2 == 0:
            zp, zs = _in_proj_a(yp, ys, norm_mix_pre[i], w_in_a, li, tm=ROW_TILE, tn=CAST_TILE,
                                gelu_cols=2 * MIXER_WIDTH)
            mix_s, v_rows = _sample_mix_a(zs, norm_v_a[li], w_spatial_a[li], b_spatial_a[li], cache_mem_kv, i)
            chunk_v_s.append(v_rows)
            ys, w_o = _sample_out_proj(mix_s, w_out, i, ys, norm_mix_post[i])
            yp = _gmlp_mix_out(zp, norm_v_a[li], w_spatial_a[li], b_spatial_a[li], mem_kv_p, i,
                               w_o, yp, norm_mix_post[i], tm=ROW_TILE, rows_per_batch=seq)
        else:
            zb, zs = _in_proj_b(yp, ys, norm_mix_pre[i], w_in_b, li, tm=ROW_TILE)
            outs, lses = [], []
            for g, (win, dil) in enumerate(SWA_PATTERN):
                o, lse = _swa_group(zb, band_tables, g, batch=batch, seq=seq)
                outs.append(o)
                lses.append(lse)
                win_p[g].append(_kv_tail(zb, g, batch=batch, seq=seq))
                kv_new = zs[:bd, MIXER_WIDTH:3 * MIXER_WIDTH]
                kv_new = kv_new.reshape(bd, 1, 2, N_SWA_GROUPS, HEADS_PER_GROUP, HEAD_DIM)[:, :, :, g]
                win_s[g].append(kv_new)
            mix_s = _sample_mix_b(zs, win_caches, li, bias_groups, cache_mem_kv, i)
            ys, w_o = _sample_out_proj(mix_s, w_out, i, ys, norm_mix_post[i])
            yp = _swa_merge_out(outs, lses, zb, mem_kv_p, i, w_o, yp, norm_mix_post[i], rows_per_batch=seq)
        yp, ys, w_ffn = _ffn_head(yp, ys, norm_ffn_pre[i], norm_ffn_post[i], w_ffn_up, w_ffn_down, i,
                                  tm=FFN_ROW_TILE, tf=HEAD_FF_TILE)
        yp = _ffn(yp, norm_ffn_pre[i], norm_ffn_post[i], w_ffn, tm=FFN_ROW_TILE, tf=FF_TILE, first_tile=1)

    return (
        yp.reshape(batch, seq, D_MODEL),
        ys[:bd].reshape(bd, 1, D_MODEL),
        mem_kv_p,
        jnp.stack(chunk_v_s, axis=0),
        jnp.stack(win_p[0], axis=0),
        jnp.stack(win_p[1], axis=0),
        jnp.stack(win_p[2], axis=0),
        jnp.stack(win_s[0], axis=0),
        jnp.stack(win_s[1], axis=0),
        jnp.stack(win_s[2], axis=0),
    )
```

```python
import functools
import math

import jax
import jax.numpy as jnp
from jax import lax
from jax.experimental import pallas as pl
from jax.experimental.pallas import tpu as pltpu

F32 = jnp.float32
BF16 = jnp.bfloat16

D_MODEL = 2048
HEAD_DIM = 128
N_MEM = 256
N_MEM_HEADS = 4
MEM_WIDTH = N_MEM_HEADS * HEAD_DIM
MIXER_WIDTH = D_MODEL - MEM_WIDTH
CHUNK = 128
N_GROUPS_A = 4
GROUP_DIM_A = MIXER_WIDTH // N_GROUPS_A
SWA_PATTERN = ((128, 1), (512, 4), (2048, 16))
N_SWA_GROUPS = len(SWA_PATTERN)
HEADS_PER_GROUP = 4
GROUP_WIDTH = HEADS_PER_GROUP * HEAD_DIM
N_BACK = 128
N_BUCKETS = 32
MAX_EXACT = N_BUCKETS // 2
MAX_DISTANCE = 2048
D_FF = 5632
EPS = 1e-6
NEG_INF = -1e30
ATTN_SCALE = HEAD_DIM ** -0.5
SAMPLE_PAD = 16
PERM_BLOCK = 256
OUT_CHUNK = 256

ROW_TILE = 512
FFN_ROW_TILE = 1024
FF_TILE = 512
HEAD_FF_TILE = 256
CAST_TILE = 512

V7X_VMEM_BYTES = 64 * 1024 * 1024
VMEM_LIMIT = V7X_VMEM_BYTES - 8 * 1024 * 1024
FFN_VMEM_LIMIT = V7X_VMEM_BYTES - 2 * 1024 * 1024


def _params(*sem, vmem_limit=VMEM_LIMIT):
    return pltpu.CompilerParams(dimension_semantics=sem, vmem_limit_bytes=vmem_limit)


def _gelu(x):
    return 0.5 * x * (1.0 + jnp.tanh(0.7978845608028654 * (x + 0.044715 * (x * x * x))))


def _rms(x, g):
    return x * lax.rsqrt(jnp.mean(x * x, axis=-1, keepdims=True) + EPS) * g


def _log2(n):
    assert n & (n - 1) == 0
    return n.bit_length() - 1


def _residue_major_perm(tm, dil, transpose=False):
    n = tm // dil
    row = lax.broadcasted_iota(jnp.int32, (tm, tm), 0)
    col = lax.broadcasted_iota(jnp.int32, (tm, tm), 1)
    dst, src = (col, row) if transpose else (row, col)
    want = lax.shift_left(jnp.bitwise_and(dst, n - 1), _log2(dil)) + lax.shift_right_logical(dst, _log2(n))
    return (src == want).astype(BF16)


def _in_proj_a_kernel(x_ref, xs_ref, g_ref, w_ref, zp_ref, zs_ref, wb_ref, xn_ref, *, n_col, gelu_cols):
    tm = x_ref.shape[0]
    tn = w_ref.shape[1]
    s = pl.program_id(0)

    def prompt_cols(acc, c0):
        col = lax.broadcasted_iota(jnp.int32, acc.shape, 1) + c0
        return jnp.where(col < gelu_cols, _gelu(acc), acc * ATTN_SCALE).astype(zp_ref.dtype)

    @pl.when(s == 0)
    def _():
        xn_ref[:tm, :] = _rms(x_ref[...], g_ref[...]).astype(BF16)
        xn_ref[tm:, :] = _rms(xs_ref[...], g_ref[...]).astype(BF16)

    for j in range(n_col):
        @pl.when(s == j)
        def _(j=j):
            cols = slice(j * tn, (j + 1) * tn)
            w = w_ref[...].astype(BF16)
            wb_ref[:, cols] = w
            acc = jnp.dot(xn_ref[...], w, preferred_element_type=F32)
            zp_ref[:, cols] = prompt_cols(acc[:tm, :], j * tn)
            sample = acc[tm:, :]
            zs_ref[:, cols] = _gelu(sample) if (j + 1) * tn <= gelu_cols else sample

    @pl.when(s >= n_col)
    def _():
        xn = _rms(x_ref[...], g_ref[...]).astype(BF16)
        acc = jnp.dot(xn, wb_ref[...], preferred_element_type=F32)
        zp_ref[...] = prompt_cols(acc, 0)


def _in_proj_a(x, xs, g, w, layer, *, tm, tn, gelu_cols):
    m, k = x.shape
    ms = xs.shape[0]
    n = w.shape[2]
    n_col, n_row = n // tn, m // tm
    assert gelu_cols % tn == 0
    row = lambda s: jnp.maximum(s - (n_col - 1), 0)
    return pl.pallas_call(
        functools.partial(_in_proj_a_kernel, n_col=n_col, gelu_cols=gelu_cols),
        grid=(n_col + n_row - 1,),
        in_specs=[
            pl.BlockSpec((tm, k), lambda s: (row(s), 0)),
            pl.BlockSpec((ms, k), lambda s: (0, 0)),
            pl.BlockSpec((1, k), lambda s: (0, 0)),
            pl.BlockSpec((None, k, tn), lambda s: (layer, 0, jnp.minimum(s, n_col - 1))),
        ],
        out_specs=[
            pl.BlockSpec((tm, n), lambda s: (row(s), 0)),
            pl.BlockSpec((ms, n), lambda s: (0, 0)),
        ],
        out_shape=[
            jax.ShapeDtypeStruct((m, n), BF16),
            jax.ShapeDtypeStruct((ms, n), F32),
        ],
        scratch_shapes=[pltpu.VMEM((k, n), BF16), pltpu.VMEM((tm + ms, k), BF16)],
        compiler_params=_params("arbitrary", vmem_limit=FFN_VMEM_LIMIT),
        name="in_proj_a",
    )(x, xs, g.reshape(1, k), w)


def _mem_kv_kernel(x_ref, g_ref, w_ref, o_ref):
    xn = _rms(x_ref[...], g_ref[...]).astype(BF16)
    acc = jnp.dot(xn, w_ref[...].astype(BF16), preferred_element_type=F32)
    for kv in range(2):
        for h in range(N_MEM_HEADS):
            c0 = (kv * N_MEM_HEADS + h) * HEAD_DIM
            o_ref[:, kv, h, :] = acc[:, c0:c0 + HEAD_DIM]


def _mem_kv(mem_rows, g, w, *, batch):
    m, k = mem_rows.shape
    layers = w.shape[0]
    out = pl.pallas_call(
        _mem_kv_kernel,
        grid=(layers,),
        in_specs=[
            pl.BlockSpec((m, k), lambda l: (0, 0)),
            pl.BlockSpec((None, 1, k), lambda l: (l, 0, 0)),
            pl.BlockSpec((None, k, 2 * MEM_WIDTH), lambda l: (l, 0, 0)),
        ],
        out_specs=pl.BlockSpec((None, m, 2, N_MEM_HEADS, HEAD_DIM), lambda l: (l, 0, 0, 0, 0)),
        out_shape=jax.ShapeDtypeStruct((layers, m, 2, N_MEM_HEADS, HEAD_DIM), F32),
        compiler_params=_params("parallel"),
        name="mem_kv",
    )(mem_rows, g.reshape(layers, 1, k), w)
    return out.reshape(layers, batch, m // batch, 2, N_MEM_HEADS, HEAD_DIM)


def _in_proj_b_kernel(x_ref, xs_ref, g_ref, w_ref, o_ref, zs_ref, wb_ref, xn_ref):
    tm = x_ref.shape[0]
    n_tiles = o_ref.shape[0]
    n_qkv = 3 * N_SWA_GROUPS
    s = pl.program_id(0)

    def row_orders():
        xn = _rms(x_ref[...], g_ref[...]).astype(BF16)
        xn_ref[0, :tm, :] = xn
        for g in range(1, N_SWA_GROUPS):
            perm = _residue_major_perm(PERM_BLOCK, SWA_PATTERN[g][1])
            for r in range(0, tm, PERM_BLOCK):
                xn_ref[g, r:r + PERM_BLOCK, :] = jnp.dot(
                    perm, xn[r:r + PERM_BLOCK, :], preferred_element_type=F32).astype(BF16)

    def prompt_tile(t, acc):
        if t < N_SWA_GROUPS or t == n_qkv:
            acc = acc * ATTN_SCALE
        o_ref[t] = acc.astype(o_ref.dtype)

    src = lambda t: t % N_SWA_GROUPS if t < n_qkv else 0

    @pl.when(s == 0)
    def _():
        row_orders()
        xs = _rms(xs_ref[...], g_ref[...]).astype(BF16)
        for g in range(N_SWA_GROUPS):
            xn_ref[g, tm:, :] = xs

    for t in range(n_tiles):
        @pl.when(s == t)
        def _(t=t):
            cols = slice(t * GROUP_WIDTH, (t + 1) * GROUP_WIDTH)
            w = w_ref[...].astype(BF16)
            wb_ref[:, cols] = w
            acc = jnp.dot(xn_ref[src(t)], w, preferred_element_type=F32)
            prompt_tile(t, acc[:tm, :])
            zs_ref[:, cols] = acc[tm:, :]

    @pl.when(s >= n_tiles)
    def _():
        row_orders()
        for t in range(n_tiles):
            w = wb_ref[:, t * GROUP_WIDTH:(t + 1) * GROUP_WIDTH]
            prompt_tile(t, jnp.dot(xn_ref[src(t), :tm, :], w, preferred_element_type=F32))


def _in_proj_b(x, xs, g, w, layer, *, tm):
    m, k = x.shape
    ms = xs.shape[0]
    n = w.shape[2]
    n_tiles = n // GROUP_WIDTH
    row = lambda s: jnp.maximum(s - (n_tiles - 1), 0)
    return pl.pallas_call(
        _in_proj_b_kernel,
        grid=(n_tiles + m // tm - 1,),
        in_specs=[
            pl.BlockSpec((tm, k), lambda s: (row(s), 0)),
            pl.BlockSpec((ms, k), lambda s: (0, 0)),
            pl.BlockSpec((1, k), lambda s: (0, 0)),
            pl.BlockSpec((None, k, GROUP_WIDTH), lambda s: (layer, 0, jnp.minimum(s, n_tiles - 1))),
        ],
        out_specs=[
            pl.BlockSpec((n_tiles, tm, GROUP_WIDTH), lambda s: (0, row(s), 0)),
            pl.BlockSpec((ms, n), lambda s: (0, 0)),
        ],
        out_shape=[
            jax.ShapeDtypeStruct((n_tiles, m, GROUP_WIDTH), BF16),
            jax.ShapeDtypeStruct((ms, n), F32),
        ],
        scratch_shapes=[pltpu.VMEM((k, n), BF16), pltpu.VMEM((N_SWA_GROUPS, tm + ms, k), BF16)],
        compiler_params=_params("arbitrary", vmem_limit=FFN_VMEM_LIMIT),
        name="in_proj_b",
    )(x, xs, g.reshape(1, k), w)


MEM_KV_SCRATCH = [pltpu.VMEM((N_MEM_HEADS, N_MEM, HEAD_DIM), BF16),
                  pltpu.VMEM((N_MEM_HEADS, N_MEM, 2 * HEAD_DIM), BF16)]


def _prepare_mem_kv(kv_ref, kb_ref, vb_ref):
    for h in range(N_MEM_HEADS):
        kb_ref[h] = kv_ref[:, 0, h, :].astype(BF16)
        vb_ref[h, :, :HEAD_DIM] = kv_ref[:, 1, h, :].astype(BF16)
        vb_ref[h, :, HEAD_DIM:] = jnp.ones((N_MEM, HEAD_DIM), BF16)


def _mem_attention_parts(q_ref, kb_ref, vb_ref, o_ref, col0):
    def head(h):
        lo, hi = h * HEAD_DIM, (h + 1) * HEAD_DIM
        s = lax.dot_general(q_ref[:, lo:hi], kb_ref[h], (((1,), (1,)), ((), ())), preferred_element_type=F32)
        m = jnp.max(s, axis=1, keepdims=True)
        p = jnp.exp(s - m).astype(BF16)
        ov = jnp.dot(p, vb_ref[h], preferred_element_type=F32)
        o_ref[:, col0 + lo:col0 + hi] = (ov[:, :HEAD_DIM] / ov[:, HEAD_DIM:]).astype(o_ref.dtype)

    return [functools.partial(head, h) for h in range(N_MEM_HEADS)]


def _gmlp_stage(u_ref, v_ref, q_ref, gv_ref, ws_ref, bs_ref, kb_ref, vb_ref, vn_ref, o_ref):
    tm = u_ref.shape[0]

    def norm_v():
        vn_ref[...] = _rms(v_ref[...].astype(F32), gv_ref[...]).astype(BF16)

    def group(g):
        row = lax.broadcasted_iota(jnp.int32, (CHUNK, CHUNK), 0)
        col = lax.broadcasted_iota(jnp.int32, (CHUNK, CHUNK), 1)
        w = jnp.where(row >= col, ws_ref[g], 0.0).astype(BF16)
        b = bs_ref[:, g:g + 1]
        c0, c1 = g * GROUP_DIM_A, (g + 1) * GROUP_DIM_A
        for c in range(tm // CHUNK):
            r0, r1 = c * CHUNK, (c + 1) * CHUNK
            s = jnp.dot(w, vn_ref[r0:r1, c0:c1], preferred_element_type=F32) + b
            o_ref[r0:r1, c0:c1] = (u_ref[r0:r1, c0:c1].astype(F32) * s).astype(o_ref.dtype)

    return ([norm_v] + [functools.partial(group, g) for g in range(N_GROUPS_A)]
            + _mem_attention_parts(q_ref, kb_ref, vb_ref, o_ref, MIXER_WIDTH))


def _mem_kv_spec(layer, batch_of):
    return pl.BlockSpec((None, None, N_MEM, 2, N_MEM_HEADS, HEAD_DIM),
                        lambda i: (layer, batch_of(i), 0, 0, 0, 0))


def _mix_out_kernel(*refs, stage, n_in, n_tiles, tiles_per_batch):
    mix_in = refs[:n_in]
    kv_ref, w_ref, x_ref, g_ref, o_ref, buf0_ref, buf1_ref, acc_ref, kb_ref, vb_ref = refs[n_in:n_in + 10]
    extra = refs[n_in + 10:]
    s = pl.program_id(0)

    @pl.when(s == 0)
    def _():
        buf1_ref[...] = jnp.zeros_like(buf1_ref)

    @pl.when(jnp.minimum(s, n_tiles - 1) % tiles_per_batch == 0)
    def _():
        _prepare_mem_kv(kv_ref, kb_ref, vb_ref)

    def step(dst_ref, src_ref):
        parts = stage(*mix_in, kb_ref, vb_ref, *extra, dst_ref)
        n_chunks = D_MODEL // OUT_CHUNK
        for c in range(n_chunks):
            cols = slice(c * OUT_CHUNK, (c + 1) * OUT_CHUNK)
            acc_ref[:, cols] = jnp.dot(src_ref[...], w_ref[:, cols], preferred_element_type=F32)
            for part in parts[c * len(parts) // n_chunks:(c + 1) * len(parts) // n_chunks]:
                part()
        o_ref[...] = x_ref[...] + _rms(acc_ref[...], g_ref[...])

    @pl.when(s % 2 == 0)
    def _():
        step(buf0_ref, buf1_ref)

    @pl.when(s % 2 == 1)
    def _():
        step(buf1_ref, buf0_ref)


def _mix_out(stage, mix_inputs, mix_specs, extra_scratch, mem_kv, layer, w, x, g, *, tm, rows_per_batch, name):
    m = x.shape[0]
    n_tiles = m // tm
    tiles_per_batch = rows_per_batch // tm
    mix_tile = lambda s: jnp.minimum(s, n_tiles - 1)
    out_tile = lambda s: jnp.maximum(s - 1, 0)
    row_spec = pl.BlockSpec((tm, D_MODEL), lambda s: (out_tile(s), 0))
    return pl.pallas_call(
        functools.partial(_mix_out_kernel, stage=stage, n_in=len(mix_inputs), n_tiles=n_tiles,
                          tiles_per_batch=tiles_per_batch),
        grid=(n_tiles + 1,),
        in_specs=[spec(mix_tile) for spec in mix_specs] + [
            _mem_kv_spec(layer, lambda s: mix_tile(s) // tiles_per_batch),
            pl.BlockSpec((D_MODEL, D_MODEL), lambda s: (0, 0), pipeline_mode=pl.Buffered(1)),
            row_spec,
            pl.BlockSpec((1, D_MODEL), lambda s: (0, 0)),
        ],
        out_specs=row_spec,
        out_shape=jax.ShapeDtypeStruct((m, D_MODEL), F32),
        scratch_shapes=[pltpu.VMEM((tm, D_MODEL), BF16), pltpu.VMEM((tm, D_MODEL), BF16),
                        pltpu.VMEM((tm, D_MODEL), F32)] + MEM_KV_SCRATCH + list(extra_scratch),
        compiler_params=_params("arbitrary", vmem_limit=FFN_VMEM_LIMIT),
        name=name,
    )(*mix_inputs, mem_kv, w, x, g.reshape(1, D_MODEL))


def _gmlp_mix_out(zact, g_v, w_s, b_s, mem_kv, layer, w, x, g, *, tm, rows_per_batch):
    const = lambda shape: (lambda tile: pl.BlockSpec(shape, lambda s: (0,) * len(shape)))
    specs = [
        lambda tile: pl.BlockSpec((tm, MIXER_WIDTH), lambda s: (tile(s), 0)),
        lambda tile: pl.BlockSpec((tm, MIXER_WIDTH), lambda s: (tile(s), 1)),
        lambda tile: pl.BlockSpec((tm, MEM_WIDTH), lambda s: (tile(s), 2 * MIXER_WIDTH // MEM_WIDTH)),
        const((1, MIXER_WIDTH)),
        const((N_GROUPS_A, CHUNK, CHUNK)),
        const((CHUNK, N_GROUPS_A)),
    ]
    return _mix_out(_gmlp_stage, (zact, zact, zact, g_v.reshape(1, MIXER_WIDTH), w_s, b_s.T), specs,
                    [pltpu.VMEM((tm, MIXER_WIDTH), BF16)], mem_kv, layer, w, x, g,
                    tm=tm, rows_per_batch=rows_per_batch, name="gmlp_mix_out")


def _out_proj_cast_kernel(mix_ref, w_ref, x_ref, g_ref, o_ref, wb_ref, acc_ref):
    j = pl.program_id(0)
    w = w_ref[...].astype(BF16)
    wb_ref[...] = w
    acc_ref[j] = jnp.dot(mix_ref[...], w, preferred_element_type=F32)

    @pl.when(j == pl.num_programs(0) - 1)
    def _():
        o = jnp.concatenate([acc_ref[t] for t in range(acc_ref.shape[0])], axis=1)
        o_ref[...] = x_ref[...] + _rms(o, g_ref[...])


def _sample_out_proj(mix, w, layer, x, g):
    bd = mix.shape[0]
    mix = jnp.pad(mix.reshape(bd, D_MODEL), ((0, x.shape[0] - bd), (0, 0))).astype(BF16)
    return _out_proj_cast(mix, w, layer, x, g, tn=CAST_TILE)


def _out_proj_cast(mix, w, layer, x, g, *, tn):
    m = x.shape[0]
    n_tiles = D_MODEL // tn
    return pl.pallas_call(
        _out_proj_cast_kernel,
        grid=(n_tiles,),
        in_specs=[
            pl.BlockSpec((m, D_MODEL), lambda j: (0, 0)),
            pl.BlockSpec((None, D_MODEL, tn), lambda j: (layer, 0, j)),
            pl.BlockSpec((m, D_MODEL), lambda j: (0, 0)),
            pl.BlockSpec((1, D_MODEL), lambda j: (0, 0)),
        ],
        out_specs=[
            pl.BlockSpec((m, D_MODEL), lambda j: (0, 0)),
            pl.BlockSpec((D_MODEL, tn), lambda j: (0, j)),
        ],
        out_shape=[
            jax.ShapeDtypeStruct((m, D_MODEL), F32),
            jax.ShapeDtypeStruct((D_MODEL, D_MODEL), BF16),
        ],
        scratch_shapes=[pltpu.VMEM((n_tiles, m, tn), F32)],
        compiler_params=_params("arbitrary", vmem_limit=FFN_VMEM_LIMIT),
        name="out_proj_cast",
    )(mix, w, x, g.reshape(1, D_MODEL))


def _accumulate(o_ref, ssq_ref, part):
    new = o_ref[...] + part
    o_ref[...] = new
    sq = new * new
    ssq_ref[...] = functools.reduce(
        jnp.add, [sq[:, c:c + HEAD_DIM] for c in range(0, sq.shape[1], HEAD_DIM)])


def _residual_norm(x_ref, o_ref, ssq_ref, g_ref):
    ms = jnp.sum(ssq_ref[...], axis=-1, keepdims=True) * (1.0 / o_ref.shape[1])
    o_ref[...] = x_ref[...] + o_ref[...] * lax.rsqrt(ms + EPS) * g_ref[...]


def _ffn_kernel(x_ref, gpre_ref, gpost_ref, wg_ref, wl_ref, wd_ref, o_ref, xn_ref, ssq_ref):
    f = pl.program_id(1)

    @pl.when(f == 0)
    def _():
        xn_ref[...] = _rms(x_ref[...], gpre_ref[...]).astype(BF16)
        o_ref[...] = jnp.zeros_like(o_ref)

    xn = xn_ref[...]
    hg = jnp.dot(xn, wg_ref[...], preferred_element_type=F32)
    hl = jnp.dot(xn, wl_ref[...], preferred_element_type=F32)
    a = (hg * jax.nn.sigmoid(hg) * hl).astype(BF16)
    _accumulate(o_ref, ssq_ref, jnp.dot(a, wd_ref[...], preferred_element_type=F32))

    @pl.when(f == pl.num_programs(1) - 1)
    def _():
        _residual_norm(x_ref, o_ref, ssq_ref, gpost_ref)


def _ffn(x, g_pre, g_post, weights, *, tm, tf, first_tile):
    m = x.shape[0]
    nf = D_FF // tf
    rows = pl.BlockSpec((tm, D_MODEL), lambda i, f: (i + first_tile, 0))
    return pl.pallas_call(
        _ffn_kernel,
        grid=(m // tm - first_tile, nf),
        in_specs=[
            rows,
            pl.BlockSpec((1, D_MODEL), lambda i, f: (0, 0)),
            pl.BlockSpec((1, D_MODEL), lambda i, f: (0, 0)),
            pl.BlockSpec((D_MODEL, tf), lambda i, f: (0, f)),
            pl.BlockSpec((D_MODEL, tf), lambda i, f: (0, f)),
            pl.BlockSpec((tf, D_MODEL), lambda i, f: (f, 0)),
        ],
        out_specs=rows,
        out_shape=jax.ShapeDtypeStruct((m, D_MODEL), F32),
        input_output_aliases={0: 0},
        scratch_shapes=[pltpu.VMEM((tm, D_MODEL), BF16), pltpu.VMEM((tm, HEAD_DIM), F32)],
        compiler_params=_params("parallel", "arbitrary", vmem_limit=FFN_VMEM_LIMIT),
        name="ffn",
    )(x, g_pre.reshape(1, D_MODEL), g_post.reshape(1, D_MODEL), *weights)


def _ffn_head_kernel(x_ref, xs_ref, gpre_ref, gpost_ref, wg_ref, wl_ref, wd_ref,
                     o_ref, os_ref, wgb_ref, wlb_ref, wdb_ref, xn_ref, ssq_ref, ssqs_ref):
    tm = x_ref.shape[0]
    f = pl.program_id(0)

    @pl.when(f == 0)
    def _():
        xn_ref[:tm, :] = _rms(x_ref[...], gpre_ref[...]).astype(BF16)
        xn_ref[tm:, :] = _rms(xs_ref[...], gpre_ref[...]).astype(BF16)
        o_ref[...] = jnp.zeros_like(o_ref)
        os_ref[...] = jnp.zeros_like(os_ref)

    wg, wl, wd = (r[...].astype(BF16) for r in (wg_ref, wl_ref, wd_ref))
    wgb_ref[...] = wg
    wlb_ref[...] = wl
    wdb_ref[...] = wd
    xn = xn_ref[...]
    hg = jnp.dot(xn, wg, preferred_element_type=F32)
    hl = jnp.dot(xn, wl, preferred_element_type=F32)
    a = (hg * jax.nn.sigmoid(hg) * hl).astype(BF16)
    part = jnp.dot(a, wd, preferred_element_type=F32)
    _accumulate(o_ref, ssq_ref, part[:tm, :])
    _accumulate(os_ref, ssqs_ref, part[tm:, :])

    @pl.when(f == pl.num_programs(0) - 1)
    def _():
        _residual_norm(x_ref, o_ref, ssq_ref, gpost_ref)
        _residual_norm(xs_ref, os_ref, ssqs_ref, gpost_ref)


def _ffn_head(x, xs, g_pre, g_post, w_up, w_down, layer, *, tm, tf):
    m = x.shape[0]
    ms = xs.shape[0]
    nf = D_FF // tf
    once = dict(pipeline_mode=pl.Buffered(1))
    head = pl.BlockSpec((tm, D_MODEL), lambda f: (0, 0), **once)
    sample = pl.BlockSpec((ms, D_MODEL), lambda f: (0, 0))
    vec = pl.BlockSpec((1, D_MODEL), lambda f: (0, 0))
    outs = pl.pallas_call(
        _ffn_head_kernel,
        grid=(nf,),
        in_specs=[
            head, sample, vec, vec,
            pl.BlockSpec((None, D_MODEL, tf), lambda f: (layer, 0, f)),
            pl.BlockSpec((None, D_MODEL, tf), lambda f: (layer, 0, nf + f)),
            pl.BlockSpec((None, tf, D_MODEL), lambda f: (layer, f, 0)),
        ],
        out_specs=[
            head, sample,
            pl.BlockSpec((D_MODEL, tf), lambda f: (0, f)),
            pl.BlockSpec((D_MODEL, tf), lambda f: (0, f)),
            pl.BlockSpec((tf, D_MODEL), lambda f: (f, 0)),
        ],
        out_shape=[
            jax.ShapeDtypeStruct((m, D_MODEL), F32),
            jax.ShapeDtypeStruct((ms, D_MODEL), F32),
            jax.ShapeDtypeStruct((D_MODEL, D_FF), BF16),
            jax.ShapeDtypeStruct((D_MODEL, D_FF), BF16),
            jax.ShapeDtypeStruct((D_FF, D_MODEL), BF16),
        ],
        input_output_aliases={0: 0},
        scratch_shapes=[pltpu.VMEM((tm + ms, D_MODEL), BF16), pltpu.VMEM((tm, HEAD_DIM), F32),
                        pltpu.VMEM((ms, HEAD_DIM), F32)],
        compiler_params=_params("arbitrary", vmem_limit=FFN_VMEM_LIMIT),
        name="ffn_head",
    )(x, xs, g_pre.reshape(1, D_MODEL), g_post.reshape(1, D_MODEL), w_up, w_up, w_down)
    return outs[0], outs[1], tuple(outs[2:])


def _swa_kernel(q_ref, k_ref, v_ref, tb_ref, o_ref, lse_ref):
    n_units, n_res, u, _ = q_ref.shape
    per_blk = N_BACK // u
    n_blk = n_units // per_blk
    lane = lax.broadcasted_iota(jnp.int32, (N_BACK, HEAD_DIM), 1)
    ones = jnp.ones((2 * N_BACK, HEAD_DIM), BF16)

    def rows(ref, res, unit0, n_rows, lo, hi):
        return ref[pl.ds(unit0, n_rows // u), res, :, lo:hi].reshape(n_rows, hi - lo)

    def block(res, qu, ku, table):
        n_keys = N_BACK if table == 0 else 2 * N_BACK
        lse_tile = jnp.zeros((N_BACK, HEAD_DIM), F32)
        for h in range(HEADS_PER_GROUP):
            lo, hi = h * HEAD_DIM, (h + 1) * HEAD_DIM
            q = rows(q_ref, res, qu, N_BACK, lo, hi)
            kw = rows(k_ref, res, ku, n_keys, lo, hi)
            vw = rows(v_ref, res, ku, n_keys, lo, hi)
            s = lax.dot_general(q, kw, (((1,), (1,)), ((), ())), preferred_element_type=F32)
            s = s + tb_ref[table, h][:, :n_keys]
            m = jnp.max(s, axis=1, keepdims=True)
            p = jnp.exp(s - m).astype(BF16)
            ov = jnp.dot(p, jnp.concatenate([vw, ones[:n_keys]], axis=1), preferred_element_type=F32)
            den = ov[:, HEAD_DIM:]
            o = ov[:, :HEAD_DIM] / den
            o_ref[pl.ds(qu, per_blk), res, :, lo:hi] = o.reshape(per_blk, u, HEAD_DIM).astype(o_ref.dtype)
            lse_tile = jnp.where(lane == h, m + jnp.log(den), lse_tile)
        lse_ref[pl.ds(qu, per_blk), res, :, :] = lse_tile.reshape(per_blk, u, HEAD_DIM)

    def block_at(res, n):
        block(res, n * per_blk, (n - 1) * per_blk, 1)

    n_pairs = (n_blk - 1) // 2
    for res in range(n_res):
        block(res, 0, 0, 0)

        def body(i, carry, res=res):
            block_at(res, 1 + 2 * i)
            block_at(res, 2 + 2 * i)
            return carry

        if n_pairs > 0:
            lax.fori_loop(0, n_pairs, body, 0)
        if (n_blk - 1) % 2 == 1:
            block_at(res, n_blk - 1)


def _sub_block(dil):
    return N_BACK if dil == 1 else PERM_BLOCK


def _swa_group(zb, tables, g, *, batch, seq):
    dil = SWA_PATTERN[g][1]
    sub = _sub_block(dil)
    n_units, u = seq // sub, sub // dil
    n_res = 4 if n_units * u == 2 * N_BACK else 1
    view = zb.reshape(zb.shape[0], batch, n_units, dil, u, GROUP_WIDTH)

    def rows_in(tile):
        return pl.BlockSpec((None, None, n_units, n_res, u, GROUP_WIDTH), lambda i, r: (tile, i, 0, r, 0, 0))

    def rows_out(width):
        return pl.BlockSpec((None, n_units, n_res, u, width), lambda i, r: (i, 0, r, 0, 0))

    return pl.pallas_call(
        _swa_kernel,
        grid=(batch, dil // n_res),
        in_specs=[rows_in(g), rows_in(N_SWA_GROUPS + g), rows_in(2 * N_SWA_GROUPS + g),
                  pl.BlockSpec((None, 2, HEADS_PER_GROUP, N_BACK, 2 * N_BACK), lambda i, r: (g, 0, 0, 0, 0))],
        out_specs=[rows_out(GROUP_WIDTH), rows_out(HEAD_DIM)],
        out_shape=[
            jax.ShapeDtypeStruct((batch, n_units, dil, u, GROUP_WIDTH), BF16),
            jax.ShapeDtypeStruct((batch, n_units, dil, u, HEAD_DIM), F32),
        ],
        compiler_params=_params("parallel", "parallel"),
        name=f"swa_group{g}",
    )(view, view, view, tables)


def _split3(x):
    hi = x.astype(BF16)
    rest = x - hi.astype(F32)
    mid = rest.astype(BF16)
    lo = (rest - mid.astype(F32)).astype(BF16)
    return hi, mid, lo


def _merge_stage(o0_ref, o1_ref, o2_ref, l0_ref, l1_ref, l2_ref, q_ref, kb_ref, vb_ref, o_ref):
    tm = o_ref.shape[0]
    group_refs = ((o0_ref, l0_ref), (o1_ref, l1_ref), (o2_ref, l2_ref))
    outs, lses = [None] * N_SWA_GROUPS, [None] * N_SWA_GROUPS

    def token_order(g):
        o_g, l_g = group_refs[g]
        dil = SWA_PATTERN[g][1]
        o = o_g[...].reshape(tm, GROUP_WIDTH)
        l = l_g[...].reshape(tm, HEAD_DIM)
        if dil > 1:
            inv = _residue_major_perm(PERM_BLOCK, dil, transpose=True)
            l3 = _split3(l)
            o_nat, l_nat = [], []
            for s in range(0, tm, PERM_BLOCK):
                o_nat.append(jnp.dot(inv, o[s:s + PERM_BLOCK, :], preferred_element_type=F32))
                l_nat.append(sum(jnp.dot(inv, t[s:s + PERM_BLOCK, :], preferred_element_type=F32) for t in l3))
            o = jnp.concatenate(o_nat, axis=0)
            l = jnp.concatenate(l_nat, axis=0)
        outs[g] = o.astype(F32)
        lses[g] = l

    def merge_head(h):
        lo, hi = h * HEAD_DIM, (h + 1) * HEAD_DIM
        ls = [l[:, h:h + 1] for l in lses]
        mx = jnp.maximum(jnp.maximum(ls[0], ls[1]), ls[2])
        es = [jnp.exp(l - mx) for l in ls]
        tot = es[0] + es[1] + es[2]
        for g in range(N_SWA_GROUPS):
            alpha = es[g] / tot
            o_ref[:, g * GROUP_WIDTH + lo:g * GROUP_WIDTH + hi] = (outs[g][:, lo:hi] * alpha).astype(o_ref.dtype)

    return ([functools.partial(token_order, g) for g in range(N_SWA_GROUPS)]
            + [functools.partial(merge_head, h) for h in range(HEADS_PER_GROUP)]
            + _mem_attention_parts(q_ref, kb_ref, vb_ref, o_ref, MIXER_WIDTH))


def _swa_merge_out(outs, lses, zb, mem_kv, layer, w, x, g, *, rows_per_batch):
    tm = ROW_TILE
    tiles_per_batch = rows_per_batch // tm

    def group_tile(width, dil):
        sub = _sub_block(dil)
        return lambda tile: pl.BlockSpec(
            (None, tm // sub, dil, sub // dil, width),
            lambda s: (tile(s) // tiles_per_batch, tile(s) % tiles_per_batch, 0, 0, 0))

    specs = ([group_tile(GROUP_WIDTH, dil) for _, dil in SWA_PATTERN]
             + [group_tile(HEAD_DIM, dil) for _, dil in SWA_PATTERN]
             + [lambda tile: pl.BlockSpec((None, tm, MEM_WIDTH), lambda s: (3 * N_SWA_GROUPS, tile(s), 0))])
    return _mix_out(_merge_stage, (*outs, *lses, zb), specs, [], mem_kv, layer, w, x, g,
                    tm=tm, rows_per_batch=rows_per_batch, name="swa_merge_out")


def _sample_mem_attention(q_row, kv_ref, o_ref, col0):
    for h in range(N_MEM_HEADS):
        lo, hi = h * HEAD_DIM, (h + 1) * HEAD_DIM
        q = q_row[:, lo:hi]
        k = kv_ref[:, 0, h, :]
        v = kv_ref[:, 1, h, :]
        s = jnp.sum(k * q, axis=1, keepdims=True) * ATTN_SCALE
        m = jnp.max(s, axis=0, keepdims=True)
        p = jnp.exp(s - m)
        den = jnp.sum(p, axis=0, keepdims=True)
        o_ref[:, col0 + lo:col0 + hi] = jnp.sum(p * v, axis=0, keepdims=True) / den


def _sample_mix_a_kernel(z_ref, gv_ref, w0_ref, b0_ref, kv_ref, o_ref, vrow_ref):
    u = z_ref[:, 0:MIXER_WIDTH]
    v = _rms(z_ref[:, MIXER_WIDTH:2 * MIXER_WIDTH], gv_ref[...])
    vrow_ref[...] = v
    o_ref[:, 0:MIXER_WIDTH] = u * (w0_ref[...] * v + b0_ref[...])
    _sample_mem_attention(z_ref[:, 2 * MIXER_WIDTH:2 * MIXER_WIDTH + MEM_WIDTH], kv_ref, o_ref, MIXER_WIDTH)


def _sample_mix_a(z, g_v, w_s, b_s, mem_kv, layer):
    bd = mem_kv.shape[1]
    w0 = jnp.repeat(w_s[:, 0, 0], GROUP_DIM_A).reshape(1, MIXER_WIDTH)
    b0 = jnp.repeat(b_s[:, 0], GROUP_DIM_A).reshape(1, MIXER_WIDTH)
    width = z.shape[1]
    vec = lambda i: (0, 0)
    return pl.pallas_call(
        _sample_mix_a_kernel,
        grid=(bd,),
        in_specs=[
            pl.BlockSpec((None, 1, width), lambda i: (i, 0, 0)),
            pl.BlockSpec((1, MIXER_WIDTH), vec),
            pl.BlockSpec((1, MIXER_WIDTH), vec),
            pl.BlockSpec((1, MIXER_WIDTH), vec),
            _mem_kv_spec(layer, lambda i: i),
        ],
        out_specs=[
            pl.BlockSpec((None, 1, D_MODEL), lambda i: (i, 0, 0)),
            pl.BlockSpec((None, 1, MIXER_WIDTH), lambda i: (i, 0, 0)),
        ],
        out_shape=[
            jax.ShapeDtypeStruct((bd, 1, D_MODEL), F32),
            jax.ShapeDtypeStruct((bd, 1, MIXER_WIDTH), F32),
        ],
        compiler_params=_params("parallel"),
        name="sample_mix_a",
    )(z[:bd].reshape(bd, 1, width), g_v.reshape(1, MIXER_WIDTH), w0, b0, mem_kv)


def _sample_mix_b_kernel(z_ref, c0_ref, c1_ref, c2_ref, bcol_ref, bnew_ref, kv_ref, o_ref):
    caches = (c0_ref, c1_ref, c2_ref)
    outs = [[None] * HEADS_PER_GROUP for _ in range(N_SWA_GROUPS)]
    lses = [[None] * HEADS_PER_GROUP for _ in range(N_SWA_GROUPS)]
    for g in range(N_SWA_GROUPS):
        for h in range(HEADS_PER_GROUP):
            hd = g * HEADS_PER_GROUP + h
            lo, hi = h * HEAD_DIM, (h + 1) * HEAD_DIM
            q = z_ref[:, hd * HEAD_DIM:(hd + 1) * HEAD_DIM]
            k_new = z_ref[:, MIXER_WIDTH + hd * HEAD_DIM:MIXER_WIDTH + (hd + 1) * HEAD_DIM]
            v_new = z_ref[:, 2 * MIXER_WIDTH + hd * HEAD_DIM:2 * MIXER_WIDTH + (hd + 1) * HEAD_DIM]
            kc = caches[g][:, 0, h, :]
            vc = caches[g][:, 1, h, :]
            s_c = jnp.sum(kc * q, axis=1, keepdims=True) * ATTN_SCALE + bcol_ref[g][:, h:h + 1]
            s_n = jnp.sum(k_new * q, axis=1, keepdims=True) * ATTN_SCALE + bnew_ref[:, hd:hd + 1]
            m = jnp.maximum(jnp.max(s_c, axis=0, keepdims=True), s_n)
            p_c = jnp.exp(s_c - m)
            p_n = jnp.exp(s_n - m)
            den = jnp.sum(p_c, axis=0, keepdims=True) + p_n
            outs[g][h] = (jnp.sum(p_c * vc, axis=0, keepdims=True) + p_n * v_new) / den
            lses[g][h] = m + jnp.log(den)
    for h in range(HEADS_PER_GROUP):
        ls = [lses[g][h] for g in range(N_SWA_GROUPS)]
        mx = jnp.maximum(jnp.maximum(ls[0], ls[1]), ls[2])
        es = [jnp.exp(l - mx) for l in ls]
        tot = es[0] + es[1] + es[2]
        for g in range(N_SWA_GROUPS):
            c0 = g * GROUP_WIDTH + h * HEAD_DIM
            o_ref[:, c0:c0 + HEAD_DIM] = outs[g][h] * (es[g] / tot)
    _sample_mem_attention(z_ref[:, 3 * MIXER_WIDTH:3 * MIXER_WIDTH + MEM_WIDTH], kv_ref, o_ref, MIXER_WIDTH)


def _sample_mix_b(z, win_caches, swa_layer, bias_groups, mem_kv, layer):
    bd = mem_kv.shape[1]
    width = z.shape[1]
    cache_views, cache_specs = [], []
    for g, (win, dil) in enumerate(SWA_PATTERN):
        c = win_caches[g]
        cache_views.append(c.reshape(c.shape[0], bd, win // dil, dil, 2, HEADS_PER_GROUP, HEAD_DIM))
        cache_specs.append(pl.BlockSpec((None, None, N_BACK, None, 2, HEADS_PER_GROUP, HEAD_DIM),
                                        lambda i: (swa_layer, i, 0, 0, 0, 0, 0)))
    bcol = jnp.stack([bg[:, N_BACK:0:-1].T for bg in bias_groups], axis=0)
    bnew = jnp.concatenate([bg[:, 0] for bg in bias_groups])
    bnew = jnp.pad(bnew, (0, HEAD_DIM - bnew.shape[0])).reshape(1, HEAD_DIM)
    return pl.pallas_call(
        _sample_mix_b_kernel,
        grid=(bd,),
        in_specs=[pl.BlockSpec((None, 1, width), lambda i: (i, 0, 0))] + cache_specs + [
            pl.BlockSpec((N_SWA_GROUPS, N_BACK, HEADS_PER_GROUP), lambda i: (0, 0, 0)),
            pl.BlockSpec((1, HEAD_DIM), lambda i: (0, 0)),
            _mem_kv_spec(layer, lambda i: i),
        ],
        out_specs=pl.BlockSpec((None, 1, D_MODEL), lambda i: (i, 0, 0)),
        out_shape=jax.ShapeDtypeStruct((bd, 1, D_MODEL), F32),
        compiler_params=_params("parallel"),
        name="sample_mix_b",
    )(z[:bd].reshape(bd, 1, width), *cache_views, bcol, bnew, mem_kv)


def _t5_bucket(dist):
    nf = jnp.maximum(dist, MAX_EXACT).astype(F32)
    large = MAX_EXACT + (jnp.log(nf / MAX_EXACT) / math.log(MAX_DISTANCE / MAX_EXACT)
                         * (N_BUCKETS - MAX_EXACT)).astype(jnp.int32)
    large = jnp.minimum(large, N_BUCKETS - 1)
    return jnp.where(dist < MAX_EXACT, dist, large)


def _group_bias(rel_bias, g, dil):
    dist = jnp.arange(N_BACK + 1, dtype=jnp.int32) * dil
    b = rel_bias[_t5_bucket(dist)][:, g * HEADS_PER_GROUP:(g + 1) * HEADS_PER_GROUP]
    return b.T.astype(F32)


def _band_tables_kernel(b_ref, o_ref):
    n, rows, width = o_ref.shape
    for x in range(n):
        row = jnp.broadcast_to(b_ref[x:x + 1, :], (rows, width))
        o_ref[x] = pltpu.roll(row, 0, 1, stride=1, stride_axis=0)


def _band_tables(bias_groups):
    width = 2 * N_BACK
    rows = []
    for bias_j in bias_groups:
        masked = jnp.full((HEADS_PER_GROUP, N_BACK - 1), NEG_INF, F32)
        rows.append(jnp.concatenate([bias_j[:, :1], masked, bias_j[:, N_BACK:0:-1]], axis=1))
        rows.append(jnp.concatenate([bias_j[:, ::-1], masked], axis=1))
    base = jnp.stack(rows, axis=0).reshape(-1, width)
    tabs = pl.pallas_call(
        _band_tables_kernel,
        out_shape=jax.ShapeDtypeStruct((base.shape[0], N_BACK, width), F32),
        name="band_tables",
    )(base)
    return tabs.reshape(N_SWA_GROUPS, 2, HEADS_PER_GROUP, N_BACK, width)


def _kv_tail_kernel(k_ref, v_ref, o_ref, *, dil):
    rows = k_ref.shape[0]
    for kv, ref in enumerate((k_ref, v_ref)):
        x = ref[...]
        if dil > 1:
            inv = _residue_major_perm(PERM_BLOCK, dil, transpose=True)
            x = jnp.concatenate([jnp.dot(inv, x[s:s + PERM_BLOCK, :], preferred_element_type=F32)
                                 for s in range(0, rows, PERM_BLOCK)], axis=0)
        x = x.astype(F32)
        for h in range(HEADS_PER_GROUP):
            o_ref[:, kv, h, :] = x[:, h * HEAD_DIM:(h + 1) * HEAD_DIM]


def _kv_tail(zb, g, *, batch, seq):
    win, dil = SWA_PATTERN[g]
    rows = min(win, FFN_ROW_TILE)
    first = (seq - win) // rows
    per_batch = seq // rows

    def tile(t):
        return pl.BlockSpec((None, rows, GROUP_WIDTH), lambda b, s: (t, b * per_batch + first + s, 0))

    return pl.pallas_call(
        functools.partial(_kv_tail_kernel, dil=dil),
        grid=(batch, win // rows),
        in_specs=[tile(N_SWA_GROUPS + g), tile(2 * N_SWA_GROUPS + g)],
        out_specs=pl.BlockSpec((None, rows, 2, HEADS_PER_GROUP, HEAD_DIM), lambda b, s: (b, s, 0, 0, 0)),
        out_shape=jax.ShapeDtypeStruct((batch, win, 2, HEADS_PER_GROUP, HEAD_DIM), F32),
        compiler_params=_params("parallel", "parallel"),
        name=f"kv_tail{g}",
    )(zb, zb)


def kernel(x_prompt, x_sample, mem_prompt, cache_mem_kv, cache_win128_kv, cache_win512_kv, cache_win2048_kv, rel_bias, norm_mix_pre, norm_mix_post, norm_ffn_pre, norm_ffn_post, norm_mem, w_mem_kv, w_in_a, norm_v_a, w_spatial_a, b_spatial_a, w_in_b, w_out, w_ffn_up, w_ffn_down):
    batch, seq, _ = x_prompt.shape
    bd = x_sample.shape[0]
    depth = w_out.shape[0]
    m_p = batch * seq
    win_caches = (cache_win128_kv, cache_win512_kv, cache_win2048_kv)

    bias_groups = [_group_bias(rel_bias, g, dil) for g, (_, dil) in enumerate(SWA_PATTERN)]
    band_tables = _band_tables(bias_groups)

    yp = x_prompt.reshape(m_p, D_MODEL)
    ys = jnp.pad(x_sample.reshape(bd, D_MODEL), ((0, SAMPLE_PAD - bd), (0, 0)))
    mem_rows = mem_prompt.reshape(batch * N_MEM, D_MODEL)

    mem_kv_p = _mem_kv(mem_rows, norm_mem, w_mem_kv, batch=batch)
    chunk_v_s = []
    win_p = [[] for _ in SWA_PATTERN]
    win_s = [[] for _ in SWA_PATTERN]
    for i in range(depth):
        li = i // 2
        if i % 2 == 0:
            zp, zs = _in_proj_a(yp, ys, norm_mix_pre[i], w_in_a, li, tm=ROW_TILE, tn=CAST_TILE,
                                gelu_cols=2 * MIXER_WIDTH)
            mix_s, v_rows = _sample_mix_a(zs, norm_v_a[li], w_spatial_a[li], b_spatial_a[li], cache_mem_kv, i)
            chunk_v_s.append(v_rows)
            ys, w_o = _sample_out_proj(mix_s, w_out, i, ys, norm_mix_post[i])
            yp = _gmlp_mix_out(zp, norm_v_a[li], w_spatial_a[li], b_spatial_a[li], mem_kv_p, i,
                               w_o, yp, norm_mix_post[i], tm=ROW_TILE, rows_per_batch=seq)
        else:
            zb, zs = _in_proj_b(yp, ys, norm_mix_pre[i], w_in_b, li, tm=ROW_TILE)
            outs, lses = [], []
            for g, (win, dil) in enumerate(SWA_PATTERN):
                o, lse = _swa_group(zb, band_tables, g, batch=batch, seq=seq)
                outs.append(o)
                lses.append(lse)
                win_p[g].append(_kv_tail(zb, g, batch=batch, seq=seq))
                kv_new = zs[:bd, MIXER_WIDTH:3 * MIXER_WIDTH]
                kv_new = kv_new.reshape(bd, 1, 2, N_SWA_GROUPS, HEADS_PER_GROUP, HEAD_DIM)[:, :, :, g]
                win_s[g].append(kv_new)
            mix_s = _sample_mix_b(zs, win_caches, li, bias_groups, cache_mem_kv, i)
            ys, w_o = _sample_out_proj(mix_s, w_out, i, ys, norm_mix_post[i])
            yp = _swa_merge_out(outs, lses, zb, mem_kv_p, i, w_o, yp, norm_mix_post[i], rows_per_batch=seq)
        yp, ys, w_ffn = _ffn_head(yp, ys, norm_ffn_pre[i], norm_ffn_post[i], w_ffn_up, w_ffn_down, i,
                                  tm=FFN_ROW_TILE, tf=HEAD_FF_TILE)
        yp = _ffn(yp, norm_ffn_pre[i], norm_ffn_post[i], w_ffn, tm=FFN_ROW_TILE, tf=FF_TILE, first_tile=1)

    return (
        yp.reshape(batch, seq, D_MODEL),
        ys[:bd].reshape(bd, 1, D_MODEL),
        mem_kv_p,
        jnp.stack(chunk_v_s, axis=0),
        jnp.stack(win_p[0], axis=0),
        jnp.stack(win_p[1], axis=0),
        jnp.stack(win_p[2], axis=0),
        jnp.stack(win_s[0], axis=0),
        jnp.stack(win_s[1], axis=0),
        jnp.stack(win_s[2], axis=0),
    )
```

```python
import functools
import math

import jax
import jax.numpy as jnp
from jax import lax
from jax.experimental import pallas as pl
from jax.experimental.pallas import tpu as pltpu

F32 = jnp.float32
BF16 = jnp.bfloat16

D_MODEL = 2048
HEAD_DIM = 128
N_MEM = 256
N_MEM_HEADS = 4
MEM_WIDTH = N_MEM_HEADS * HEAD_DIM
MIXER_WIDTH = D_MODEL - MEM_WIDTH
CHUNK = 128
N_GROUPS_A = 4
GROUP_DIM_A = MIXER_WIDTH // N_GROUPS_A
SWA_PATTERN = ((128, 1), (512, 4), (2048, 16))
N_SWA_GROUPS = len(SWA_PATTERN)
HEADS_PER_GROUP = 4
GROUP_WIDTH = HEADS_PER_GROUP * HEAD_DIM
N_BACK = 128
N_BUCKETS = 32
MAX_EXACT = N_BUCKETS // 2
MAX_DISTANCE = 2048
D_FF = 5632
EPS = 1e-6
NEG_INF = -1e30
ATTN_SCALE = HEAD_DIM ** -0.5
SAMPLE_PAD = 16
PERM_BLOCK = 256
SWA_BLOCKS_PER_TRIP = 4
OUT_CHUNK = 256

ROW_TILE = 512
FFN_ROW_TILE = 1024
FF_TILE = 512
HEAD_FF_TILE = 256
CAST_TILE = 512

V7X_VMEM_BYTES = 64 * 1024 * 1024
VMEM_LIMIT = V7X_VMEM_BYTES - 8 * 1024 * 1024
FFN_VMEM_LIMIT = V7X_VMEM_BYTES - 2 * 1024 * 1024


def _params(*sem, vmem_limit=VMEM_LIMIT):
    return pltpu.CompilerParams(dimension_semantics=sem, vmem_limit_bytes=vmem_limit)


def _gelu(x):
    return 0.5 * x * (1.0 + jnp.tanh(0.7978845608028654 * (x + 0.044715 * (x * x * x))))


def _rms(x, g):
    return x * lax.rsqrt(jnp.mean(x * x, axis=-1, keepdims=True) + EPS) * g


def _log2(n):
    assert n & (n - 1) == 0
    return n.bit_length() - 1


def _residue_major_perm(tm, dil, transpose=False):
    n = tm // dil
    row = lax.broadcasted_iota(jnp.int32, (tm, tm), 0)
    col = lax.broadcasted_iota(jnp.int32, (tm, tm), 1)
    dst, src = (col, row) if transpose else (row, col)
    want = lax.shift_left(jnp.bitwise_and(dst, n - 1), _log2(dil)) + lax.shift_right_logical(dst, _log2(n))
    return (src == want).astype(BF16)


def _in_proj_a_kernel(x_ref, xs_ref, g_ref, w_ref, zp_ref, zs_ref, wb_ref, xn_ref, *, n_col, gelu_cols):
    tm = x_ref.shape[0]
    tn = w_ref.shape[1]
    s = pl.program_id(0)

    def prompt_cols(acc, c0):
        col = lax.broadcasted_iota(jnp.int32, acc.shape, 1) + c0
        return jnp.where(col < gelu_cols, _gelu(acc), acc * ATTN_SCALE).astype(zp_ref.dtype)

    @pl.when(s == 0)
    def _():
        xn_ref[:tm, :] = _rms(x_ref[...], g_ref[...]).astype(BF16)
        xn_ref[tm:, :] = _rms(xs_ref[...], g_ref[...]).astype(BF16)

    for j in range(n_col):
        @pl.when(s == j)
        def _(j=j):
            cols = slice(j * tn, (j + 1) * tn)
            w = w_ref[...].astype(BF16)
            wb_ref[:, cols] = w
            acc = jnp.dot(xn_ref[...], w, preferred_element_type=F32)
            zp_ref[:, cols] = prompt_cols(acc[:tm, :], j * tn)
            sample = acc[tm:, :]
            zs_ref[:, cols] = _gelu(sample) if (j + 1) * tn <= gelu_cols else sample

    @pl.when(s >= n_col)
    def _():
        xn = _rms(x_ref[...], g_ref[...]).astype(BF16)
        acc = jnp.dot(xn, wb_ref[...], preferred_element_type=F32)
        zp_ref[...] = prompt_cols(acc, 0)


def _in_proj_a(x, xs, g, w, layer, *, tm, tn, gelu_cols):
    m, k = x.shape
    ms = xs.shape[0]
    n = w.shape[2]
    n_col, n_row = n // tn, m // tm
    assert gelu_cols % tn == 0
    row = lambda s: jnp.maximum(s - (n_col - 1), 0)
    return pl.pallas_call(
        functools.partial(_in_proj_a_kernel, n_col=n_col, gelu_cols=gelu_cols),
        grid=(n_col + n_row - 1,),
        in_specs=[
            pl.BlockSpec((tm, k), lambda s: (row(s), 0)),
            pl.BlockSpec((ms, k), lambda s: (0, 0)),
            pl.BlockSpec((1, k), lambda s: (0, 0)),
            pl.BlockSpec((None, k, tn), lambda s: (layer, 0, jnp.minimum(s, n_col - 1))),
        ],
        out_specs=[
            pl.BlockSpec((tm, n), lambda s: (row(s), 0)),
            pl.BlockSpec((ms, n), lambda s: (0, 0)),
        ],
        out_shape=[
            jax.ShapeDtypeStruct((m, n), BF16),
            jax.ShapeDtypeStruct((ms, n), F32),
        ],
        scratch_shapes=[pltpu.VMEM((k, n), BF16), pltpu.VMEM((tm + ms, k), BF16)],
        compiler_params=_params("arbitrary"),
        name="in_proj_a",
    )(x, xs, g.reshape(1, k), w)


def _mem_kv_kernel(x_ref, g_ref, w_ref, o_ref):
    xn = _rms(x_ref[...], g_ref[...]).astype(BF16)
    acc = jnp.dot(xn, w_ref[...].astype(BF16), preferred_element_type=F32)
    for kv in range(2):
        for h in range(N_MEM_HEADS):
            c0 = (kv * N_MEM_HEADS + h) * HEAD_DIM
            o_ref[:, kv, h, :] = acc[:, c0:c0 + HEAD_DIM]


def _mem_kv(mem_rows, g, w, *, batch):
    m, k = mem_rows.shape
    layers = w.shape[0]
    out = pl.pallas_call(
        _mem_kv_kernel,
        grid=(layers,),
        in_specs=[
            pl.BlockSpec((m, k), lambda l: (0, 0)),
            pl.BlockSpec((None, 1, k), lambda l: (l, 0, 0)),
            pl.BlockSpec((None, k, 2 * MEM_WIDTH), lambda l: (l, 0, 0)),
        ],
        out_specs=pl.BlockSpec((None, m, 2, N_MEM_HEADS, HEAD_DIM), lambda l: (l, 0, 0, 0, 0)),
        out_shape=jax.ShapeDtypeStruct((layers, m, 2, N_MEM_HEADS, HEAD_DIM), F32),
        compiler_params=_params("parallel"),
        name="mem_kv",
    )(mem_rows, g.reshape(layers, 1, k), w)
    return out.reshape(layers, batch, m // batch, 2, N_MEM_HEADS, HEAD_DIM)


def _in_proj_b_kernel(x_ref, xs_ref, g_ref, w_ref, o_ref, zs_ref, wb_ref, xn_ref):
    tm = x_ref.shape[0]
    n_tiles = o_ref.shape[0]
    n_qkv = 3 * N_SWA_GROUPS
    s = pl.program_id(0)

    def row_orders():
        xn = _rms(x_ref[...], g_ref[...]).astype(BF16)
        xn_ref[0, :tm, :] = xn
        for g in range(1, N_SWA_GROUPS):
            perm = _residue_major_perm(PERM_BLOCK, SWA_PATTERN[g][1])
            for r in range(0, tm, PERM_BLOCK):
                xn_ref[g, r:r + PERM_BLOCK, :] = jnp.dot(
                    perm, xn[r:r + PERM_BLOCK, :], preferred_element_type=F32).astype(BF16)

    def prompt_tile(t, acc):
        if t < N_SWA_GROUPS or t == n_qkv:
            acc = acc * ATTN_SCALE
        o_ref[t] = acc.astype(o_ref.dtype)

    src = lambda t: t % N_SWA_GROUPS if t < n_qkv else 0

    @pl.when(s == 0)
    def _():
        row_orders()
        xs = _rms(xs_ref[...], g_ref[...]).astype(BF16)
        for g in range(N_SWA_GROUPS):
            xn_ref[g, tm:, :] = xs

    for t in range(n_tiles):
        @pl.when(s == t)
        def _(t=t):
            cols = slice(t * GROUP_WIDTH, (t + 1) * GROUP_WIDTH)
            w = w_ref[...].astype(BF16)
            wb_ref[:, cols] = w
            acc = jnp.dot(xn_ref[src(t)], w, preferred_element_type=F32)
            prompt_tile(t, acc[:tm, :])
            zs_ref[:, cols] = acc[tm:, :]

    @pl.when(s >= n_tiles)
    def _():
        row_orders()
        for t in range(n_tiles):
            w = wb_ref[:, t * GROUP_WIDTH:(t + 1) * GROUP_WIDTH]
            prompt_tile(t, jnp.dot(xn_ref[src(t), :tm, :], w, preferred_element_type=F32))


def _in_proj_b(x, xs, g, w, layer, *, tm):
    m, k = x.shape
    ms = xs.shape[0]
    n = w.shape[2]
    n_tiles = n // GROUP_WIDTH
    row = lambda s: jnp.maximum(s - (n_tiles - 1), 0)
    return pl.pallas_call(
        _in_proj_b_kernel,
        grid=(n_tiles + m // tm - 1,),
        in_specs=[
            pl.BlockSpec((tm, k), lambda s: (row(s), 0)),
            pl.BlockSpec((ms, k), lambda s: (0, 0)),
            pl.BlockSpec((1, k), lambda s: (0, 0)),
            pl.BlockSpec((None, k, GROUP_WIDTH), lambda s: (layer, 0, jnp.minimum(s, n_tiles - 1))),
        ],
        out_specs=[
            pl.BlockSpec((n_tiles, tm, GROUP_WIDTH), lambda s: (0, row(s), 0)),
            pl.BlockSpec((ms, n), lambda s: (0, 0)),
        ],
        out_shape=[
            jax.ShapeDtypeStruct((n_tiles, m, GROUP_WIDTH), BF16),
            jax.ShapeDtypeStruct((ms, n), F32),
        ],
        scratch_shapes=[pltpu.VMEM((k, n), BF16), pltpu.VMEM((N_SWA_GROUPS, tm + ms, k), BF16)],
        compiler_params=_params("arbitrary"),
        name="in_proj_b",
    )(x, xs, g.reshape(1, k), w)


MEM_KV_SCRATCH = [pltpu.VMEM((N_MEM_HEADS, N_MEM, HEAD_DIM), BF16),
                  pltpu.VMEM((N_MEM_HEADS, N_MEM, 2 * HEAD_DIM), BF16)]


def _prepare_mem_kv(kv_ref, kb_ref, vb_ref):
    for h in range(N_MEM_HEADS):
        kb_ref[h] = kv_ref[:, 0, h, :].astype(BF16)
        vb_ref[h, :, :HEAD_DIM] = kv_ref[:, 1, h, :].astype(BF16)
        vb_ref[h, :, HEAD_DIM:] = jnp.ones((N_MEM, HEAD_DIM), BF16)


def _mem_attention_parts(q_ref, kb_ref, vb_ref, o_ref, col0):
    def head(h):
        lo, hi = h * HEAD_DIM, (h + 1) * HEAD_DIM
        s = lax.dot_general(q_ref[:, lo:hi], kb_ref[h], (((1,), (1,)), ((), ())), preferred_element_type=F32)
        m = jnp.max(s, axis=1, keepdims=True)
        p = jnp.exp(s - m).astype(BF16)
        ov = jnp.dot(p, vb_ref[h], preferred_element_type=F32)
        o_ref[:, col0 + lo:col0 + hi] = (ov[:, :HEAD_DIM] / ov[:, HEAD_DIM:]).astype(o_ref.dtype)

    return [functools.partial(head, h) for h in range(N_MEM_HEADS)]


def _gmlp_stage(u_ref, v_ref, q_ref, gv_ref, ws_ref, bs_ref, kb_ref, vb_ref, vn_ref, o_ref):
    tm = u_ref.shape[0]

    def norm_v():
        vn_ref[...] = _rms(v_ref[...].astype(F32), gv_ref[...]).astype(BF16)

    def group(g):
        row = lax.broadcasted_iota(jnp.int32, (CHUNK, CHUNK), 0)
        col = lax.broadcasted_iota(jnp.int32, (CHUNK, CHUNK), 1)
        w = jnp.where(row >= col, ws_ref[g], 0.0).astype(BF16)
        b = bs_ref[:, g:g + 1]
        c0, c1 = g * GROUP_DIM_A, (g + 1) * GROUP_DIM_A
        for c in range(tm // CHUNK):
            r0, r1 = c * CHUNK, (c + 1) * CHUNK
            s = jnp.dot(w, vn_ref[r0:r1, c0:c1], preferred_element_type=F32) + b
            o_ref[r0:r1, c0:c1] = (u_ref[r0:r1, c0:c1].astype(F32) * s).astype(o_ref.dtype)

    return ([norm_v] + [functools.partial(group, g) for g in range(N_GROUPS_A)]
            + _mem_attention_parts(q_ref, kb_ref, vb_ref, o_ref, MIXER_WIDTH))


def _mem_kv_spec(layer, batch_of):
    return pl.BlockSpec((None, None, N_MEM, 2, N_MEM_HEADS, HEAD_DIM),
                        lambda i: (layer, batch_of(i), 0, 0, 0, 0))


def _mix_out_kernel(*refs, stage, n_in, n_tiles, tiles_per_batch):
    mix_in = refs[:n_in]
    kv_ref, w_ref, x_ref, g_ref, o_ref, buf0_ref, buf1_ref, acc_ref, kb_ref, vb_ref = refs[n_in:n_in + 10]
    extra = refs[n_in + 10:]
    s = pl.program_id(0)

    @pl.when(s == 0)
    def _():
        buf1_ref[...] = jnp.zeros_like(buf1_ref)

    @pl.when(jnp.minimum(s, n_tiles - 1) % tiles_per_batch == 0)
    def _():
        _prepare_mem_kv(kv_ref, kb_ref, vb_ref)

    def step(dst_ref, src_ref):
        parts = stage(*mix_in, kb_ref, vb_ref, *extra, dst_ref)
        n_chunks = D_MODEL // OUT_CHUNK
        for c in range(n_chunks):
            cols = slice(c * OUT_CHUNK, (c + 1) * OUT_CHUNK)
            acc_ref[:, cols] = jnp.dot(src_ref[...], w_ref[:, cols], preferred_element_type=F32)
            for part in parts[c * len(parts) // n_chunks:(c + 1) * len(parts) // n_chunks]:
                part()
        o_ref[...] = x_ref[...] + _rms(acc_ref[...], g_ref[...])

    @pl.when(s % 2 == 0)
    def _():
        step(buf0_ref, buf1_ref)

    @pl.when(s % 2 == 1)
    def _():
        step(buf1_ref, buf0_ref)


def _mix_out(stage, mix_inputs, mix_specs, extra_scratch, mem_kv, layer, w, x, g, *, tm, rows_per_batch, name):
    m = x.shape[0]
    n_tiles = m // tm
    tiles_per_batch = rows_per_batch // tm
    mix_tile = lambda s: jnp.minimum(s, n_tiles - 1)
    out_tile = lambda s: jnp.maximum(s - 1, 0)
    row_spec = pl.BlockSpec((tm, D_MODEL), lambda s: (out_tile(s), 0))
    return pl.pallas_call(
        functools.partial(_mix_out_kernel, stage=stage, n_in=len(mix_inputs), n_tiles=n_tiles,
                          tiles_per_batch=tiles_per_batch),
        grid=(n_tiles + 1,),
        in_specs=[spec(mix_tile) for spec in mix_specs] + [
            _mem_kv_spec(layer, lambda s: mix_tile(s) // tiles_per_batch),
            pl.BlockSpec((D_MODEL, D_MODEL), lambda s: (0, 0), pipeline_mode=pl.Buffered(1)),
            row_spec,
            pl.BlockSpec((1, D_MODEL), lambda s: (0, 0)),
        ],
        out_specs=row_spec,
        out_shape=jax.ShapeDtypeStruct((m, D_MODEL), F32),
        scratch_shapes=[pltpu.VMEM((tm, D_MODEL), BF16), pltpu.VMEM((tm, D_MODEL), BF16),
                        pltpu.VMEM((tm, D_MODEL), F32)] + MEM_KV_SCRATCH + list(extra_scratch),
        compiler_params=_params("arbitrary"),
        name=name,
    )(*mix_inputs, mem_kv, w, x, g.reshape(1, D_MODEL))


def _gmlp_mix_out(zact, g_v, w_s, b_s, mem_kv, layer, w, x, g, *, tm, rows_per_batch):
    const = lambda shape: (lambda tile: pl.BlockSpec(shape, lambda s: (0,) * len(shape)))
    specs = [
        lambda tile: pl.BlockSpec((tm, MIXER_WIDTH), lambda s: (tile(s), 0)),
        lambda tile: pl.BlockSpec((tm, MIXER_WIDTH), lambda s: (tile(s), 1)),
        lambda tile: pl.BlockSpec((tm, MEM_WIDTH), lambda s: (tile(s), 2 * MIXER_WIDTH // MEM_WIDTH)),
        const((1, MIXER_WIDTH)),
        const((N_GROUPS_A, CHUNK, CHUNK)),
        const((CHUNK, N_GROUPS_A)),
    ]
    return _mix_out(_gmlp_stage, (zact, zact, zact, g_v.reshape(1, MIXER_WIDTH), w_s, b_s.T), specs,
                    [pltpu.VMEM((tm, MIXER_WIDTH), BF16)], mem_kv, layer, w, x, g,
                    tm=tm, rows_per_batch=rows_per_batch, name="gmlp_mix_out")


def _out_proj_cast_kernel(mix_ref, w_ref, x_ref, g_ref, o_ref, wb_ref, acc_ref):
    j = pl.program_id(0)
    w = w_ref[...].astype(BF16)
    wb_ref[...] = w
    acc_ref[j] = jnp.dot(mix_ref[...], w, preferred_element_type=F32)

    @pl.when(j == pl.num_programs(0) - 1)
    def _():
        o = jnp.concatenate([acc_ref[t] for t in range(acc_ref.shape[0])], axis=1)
        o_ref[...] = x_ref[...] + _rms(o, g_ref[...])


def _sample_out_proj(mix, w, layer, x, g):
    bd = mix.shape[0]
    mix = jnp.pad(mix.reshape(bd, D_MODEL), ((0, x.shape[0] - bd), (0, 0))).astype(BF16)
    return _out_proj_cast(mix, w, layer, x, g, tn=CAST_TILE)


def _out_proj_cast(mix, w, layer, x, g, *, tn):
    m = x.shape[0]
    n_tiles = D_MODEL // tn
    return pl.pallas_call(
        _out_proj_cast_kernel,
        grid=(n_tiles,),
        in_specs=[
            pl.BlockSpec((m, D_MODEL), lambda j: (0, 0)),
            pl.BlockSpec((None, D_MODEL, tn), lambda j: (layer, 0, j)),
            pl.BlockSpec((m, D_MODEL), lambda j: (0, 0)),
            pl.BlockSpec((1, D_MODEL), lambda j: (0, 0)),
        ],
        out_specs=[
            pl.BlockSpec((m, D_MODEL), lambda j: (0, 0)),
            pl.BlockSpec((D_MODEL, tn), lambda j: (0, j)),
        ],
        out_shape=[
            jax.ShapeDtypeStruct((m, D_MODEL), F32),
            jax.ShapeDtypeStruct((D_MODEL, D_MODEL), BF16),
        ],
        scratch_shapes=[pltpu.VMEM((n_tiles, m, tn), F32)],
        compiler_params=_params("arbitrary"),
        name="out_proj_cast",
    )(mix, w, x, g.reshape(1, D_MODEL))


def _accumulate(o_ref, ssq_ref, part):
    new = o_ref[...] + part
    o_ref[...] = new
    sq = new * new
    ssq_ref[...] = functools.reduce(
        jnp.add, [sq[:, c:c + HEAD_DIM] for c in range(0, sq.shape[1], HEAD_DIM)])


def _residual_norm(x_ref, o_ref, ssq_ref, g_ref):
    ms = jnp.sum(ssq_ref[...], axis=-1, keepdims=True) * (1.0 / o_ref.shape[1])
    o_ref[...] = x_ref[...] + o_ref[...] * lax.rsqrt(ms + EPS) * g_ref[...]


def _ffn_kernel(x_ref, gpre_ref, gpost_ref, wg_ref, wl_ref, wd_ref, o_ref, xn_ref, ssq_ref):
    f = pl.program_id(1)

    @pl.when(f == 0)
    def _():
        xn_ref[...] = _rms(x_ref[...], gpre_ref[...]).astype(BF16)
        o_ref[...] = jnp.zeros_like(o_ref)

    xn = xn_ref[...]
    hg = jnp.dot(xn, wg_ref[...], preferred_element_type=F32)
    hl = jnp.dot(xn, wl_ref[...], preferred_element_type=F32)
    a = (hg * jax.nn.sigmoid(hg) * hl).astype(BF16)
    _accumulate(o_ref, ssq_ref, jnp.dot(a, wd_ref[...], preferred_element_type=F32))

    @pl.when(f == pl.num_programs(1) - 1)
    def _():
        _residual_norm(x_ref, o_ref, ssq_ref, gpost_ref)


def _ffn(x, g_pre, g_post, weights, *, tm, tf, first_tile):
    m = x.shape[0]
    nf = D_FF // tf
    rows = pl.BlockSpec((tm, D_MODEL), lambda i, f: (i + first_tile, 0))
    return pl.pallas_call(
        _ffn_kernel,
        grid=(m // tm - first_tile, nf),
        in_specs=[
            rows,
            pl.BlockSpec((1, D_MODEL), lambda i, f: (0, 0)),
            pl.BlockSpec((1, D_MODEL), lambda i, f: (0, 0)),
            pl.BlockSpec((D_MODEL, tf), lambda i, f: (0, f)),
            pl.BlockSpec((D_MODEL, tf), lambda i, f: (0, f)),
            pl.BlockSpec((tf, D_MODEL), lambda i, f: (f, 0)),
        ],
        out_specs=rows,
        out_shape=jax.ShapeDtypeStruct((m, D_MODEL), F32),
        input_output_aliases={0: 0},
        scratch_shapes=[pltpu.VMEM((tm, D_MODEL), BF16), pltpu.VMEM((tm, HEAD_DIM), F32)],
        compiler_params=_params("parallel", "arbitrary", vmem_limit=FFN_VMEM_LIMIT),
        name="ffn",
    )(x, g_pre.reshape(1, D_MODEL), g_post.reshape(1, D_MODEL), *weights)


def _ffn_head_kernel(x_ref, xs_ref, gpre_ref, gpost_ref, wg_ref, wl_ref, wd_ref,
                     o_ref, os_ref, wgb_ref, wlb_ref, wdb_ref, xn_ref, ssq_ref, ssqs_ref):
    tm = x_ref.shape[0]
    f = pl.program_id(0)

    @pl.when(f == 0)
    def _():
        xn_ref[:tm, :] = _rms(x_ref[...], gpre_ref[...]).astype(BF16)
        xn_ref[tm:, :] = _rms(xs_ref[...], gpre_ref[...]).astype(BF16)
        o_ref[...] = jnp.zeros_like(o_ref)
        os_ref[...] = jnp.zeros_like(os_ref)

    wg, wl, wd = (r[...].astype(BF16) for r in (wg_ref, wl_ref, wd_ref))
    wgb_ref[...] = wg
    wlb_ref[...] = wl
    wdb_ref[...] = wd
    xn = xn_ref[...]
    hg = jnp.dot(xn, wg, preferred_element_type=F32)
    hl = jnp.dot(xn, wl, preferred_element_type=F32)
    a = (hg * jax.nn.sigmoid(hg) * hl).astype(BF16)
    part = jnp.dot(a, wd, preferred_element_type=F32)
    _accumulate(o_ref, ssq_ref, part[:tm, :])
    _accumulate(os_ref, ssqs_ref, part[tm:, :])

    @pl.when(f == pl.num_programs(0) - 1)
    def _():
        _residual_norm(x_ref, o_ref, ssq_ref, gpost_ref)
        _residual_norm(xs_ref, os_ref, ssqs_ref, gpost_ref)


def _ffn_head(x, xs, g_pre, g_post, w_up, w_down, layer, *, tm, tf):
    m = x.shape[0]
    ms = xs.shape[0]
    nf = D_FF // tf
    once = dict(pipeline_mode=pl.Buffered(1))
    head = pl.BlockSpec((tm, D_MODEL), lambda f: (0, 0), **once)
    sample = pl.BlockSpec((ms, D_MODEL), lambda f: (0, 0))
    vec = pl.BlockSpec((1, D_MODEL), lambda f: (0, 0))
    outs = pl.pallas_call(
        _ffn_head_kernel,
        grid=(nf,),
        in_specs=[
            head, sample, vec, vec,
            pl.BlockSpec((None, D_MODEL, tf), lambda f: (layer, 0, f)),
            pl.BlockSpec((None, D_MODEL, tf), lambda f: (layer, 0, nf + f)),
            pl.BlockSpec((None, tf, D_MODEL), lambda f: (layer, f, 0)),
        ],
        out_specs=[
            head, sample,
            pl.BlockSpec((D_MODEL, tf), lambda f: (0, f)),
            pl.BlockSpec((D_MODEL, tf), lambda f: (0, f)),
            pl.BlockSpec((tf, D_MODEL), lambda f: (f, 0)),
        ],
        out_shape=[
            jax.ShapeDtypeStruct((m, D_MODEL), F32),
            jax.ShapeDtypeStruct((ms, D_MODEL), F32),
            jax.ShapeDtypeStruct((D_MODEL, D_FF), BF16),
            jax.ShapeDtypeStruct((D_MODEL, D_FF), BF16),
            jax.ShapeDtypeStruct((D_FF, D_MODEL), BF16),
        ],
        input_output_aliases={0: 0},
        scratch_shapes=[pltpu.VMEM((tm + ms, D_MODEL), BF16), pltpu.VMEM((tm, HEAD_DIM), F32),
                        pltpu.VMEM((ms, HEAD_DIM), F32)],
        compiler_params=_params("arbitrary"),
        name="ffn_head",
    )(x, xs, g_pre.reshape(1, D_MODEL), g_post.reshape(1, D_MODEL), w_up, w_up, w_down)
    return outs[0], outs[1], tuple(outs[2:])


def _swa_kernel(q_ref, k_ref, v_ref, tb_ref, o_ref, lse_ref):
    n_units, n_res, u, _ = q_ref.shape
    per_blk = N_BACK // u
    n_blk = n_units // per_blk
    lane = lax.broadcasted_iota(jnp.int32, (N_BACK, HEAD_DIM), 1)
    ones = jnp.ones((2 * N_BACK, HEAD_DIM), BF16)

    def rows(ref, res, unit0, n_rows, lo, hi):
        return ref[pl.ds(unit0, n_rows // u), res, :, lo:hi].reshape(n_rows, hi - lo)

    def block(res, qu, ku, table):
        n_keys = N_BACK if table == 0 else 2 * N_BACK
        lse_tile = jnp.zeros((N_BACK, HEAD_DIM), F32)
        for h in range(HEADS_PER_GROUP):
            lo, hi = h * HEAD_DIM, (h + 1) * HEAD_DIM
            q = rows(q_ref, res, qu, N_BACK, lo, hi)
            kw = rows(k_ref, res, ku, n_keys, lo, hi)
            vw = rows(v_ref, res, ku, n_keys, lo, hi)
            s = lax.dot_general(q, kw, (((1,), (1,)), ((), ())), preferred_element_type=F32)
            s = s + tb_ref[table, h][:, :n_keys]
            m = jnp.max(s, axis=1, keepdims=True)
            p = jnp.exp(s - m).astype(BF16)
            ov = jnp.dot(p, jnp.concatenate([vw, ones[:n_keys]], axis=1), preferred_element_type=F32)
            den = ov[:, HEAD_DIM:]
            o = ov[:, :HEAD_DIM] / den
            o_ref[pl.ds(qu, per_blk), res, :, lo:hi] = o.reshape(per_blk, u, HEAD_DIM).astype(o_ref.dtype)
            lse_tile = jnp.where(lane == h, m + jnp.log(den), lse_tile)
        lse_ref[pl.ds(qu, per_blk), res, :, :] = lse_tile.reshape(per_blk, u, HEAD_DIM)

    def block_at(res, n):
        block(res, n * per_blk, (n - 1) * per_blk, 1)

    n_trips = (n_blk - 1) // SWA_BLOCKS_PER_TRIP
    for res in range(n_res):
        block(res, 0, 0, 0)

        def body(i, carry, res=res):
            for b in range(SWA_BLOCKS_PER_TRIP):
                block_at(res, 1 + SWA_BLOCKS_PER_TRIP * i + b)
            return carry

        if n_trips > 0:
            lax.fori_loop(0, n_trips, body, 0)
        for n in range(1 + n_trips * SWA_BLOCKS_PER_TRIP, n_blk):
            block_at(res, n)


def _sub_block(dil):
    return N_BACK if dil == 1 else PERM_BLOCK


def _swa_group(zb, tables, g, *, batch, seq):
    dil = SWA_PATTERN[g][1]
    sub = _sub_block(dil)
    n_units, u = seq // sub, sub // dil
    n_res = 4 if n_units * u == 2 * N_BACK else 1
    view = zb.reshape(zb.shape[0], batch, n_units, dil, u, GROUP_WIDTH)

    def rows_in(tile):
        return pl.BlockSpec((None, None, n_units, n_res, u, GROUP_WIDTH), lambda i, r: (tile, i, 0, r, 0, 0))

    def rows_out(width):
        return pl.BlockSpec((None, n_units, n_res, u, width), lambda i, r: (i, 0, r, 0, 0))

    return pl.pallas_call(
        _swa_kernel,
        grid=(batch, dil // n_res),
        in_specs=[rows_in(g), rows_in(N_SWA_GROUPS + g), rows_in(2 * N_SWA_GROUPS + g),
                  pl.BlockSpec((None, 2, HEADS_PER_GROUP, N_BACK, 2 * N_BACK), lambda i, r: (g, 0, 0, 0, 0))],
        out_specs=[rows_out(GROUP_WIDTH), rows_out(HEAD_DIM)],
        out_shape=[
            jax.ShapeDtypeStruct((batch, n_units, dil, u, GROUP_WIDTH), BF16),
            jax.ShapeDtypeStruct((batch, n_units, dil, u, HEAD_DIM), F32),
        ],
        compiler_params=_params("parallel", "parallel"),
        name=f"swa_group{g}",
    )(view, view, view, tables)


def _split3(x):
    hi = x.astype(BF16)
    rest = x - hi.astype(F32)
    mid = rest.astype(BF16)
    lo = (rest - mid.astype(F32)).astype(BF16)
    return hi, mid, lo


def _merge_stage(o0_ref, o1_ref, o2_ref, l0_ref, l1_ref, l2_ref, q_ref, kb_ref, vb_ref, o_ref):
    tm = o_ref.shape[0]
    group_refs = ((o0_ref, l0_ref), (o1_ref, l1_ref), (o2_ref, l2_ref))
    outs, lses = [None] * N_SWA_GROUPS, [None] * N_SWA_GROUPS

    def token_order(g):
        o_g, l_g = group_refs[g]
        dil = SWA_PATTERN[g][1]
        o = o_g[...].reshape(tm, GROUP_WIDTH)
        l = l_g[...].reshape(tm, HEAD_DIM)
        if dil > 1:
            inv = _residue_major_perm(PERM_BLOCK, dil, transpose=True)
            l3 = _split3(l)
            o_nat, l_nat = [], []
            for s in range(0, tm, PERM_BLOCK):
                o_nat.append(jnp.dot(inv, o[s:s + PERM_BLOCK, :], preferred_element_type=F32))
                l_nat.append(sum(jnp.dot(inv, t[s:s + PERM_BLOCK, :], preferred_element_type=F32) for t in l3))
            o = jnp.concatenate(o_nat, axis=0)
            l = jnp.concatenate(l_nat, axis=0)
        outs[g] = o.astype(F32)
        lses[g] = l

    def merge_head(h):
        lo, hi = h * HEAD_DIM, (h + 1) * HEAD_DIM
        ls = [l[:, h:h + 1] for l in lses]
        mx = jnp.maximum(jnp.maximum(ls[0], ls[1]), ls[2])
        es = [jnp.exp(l - mx) for l in ls]
        tot = es[0] + es[1] + es[2]
        for g in range(N_SWA_GROUPS):
            alpha = es[g] / tot
            o_ref[:, g * GROUP_WIDTH + lo:g * GROUP_WIDTH + hi] = (outs[g][:, lo:hi] * alpha).astype(o_ref.dtype)

    return ([functools.partial(token_order, g) for g in range(N_SWA_GROUPS)]
            + [functools.partial(merge_head, h) for h in range(HEADS_PER_GROUP)]
            + _mem_attention_parts(q_ref, kb_ref, vb_ref, o_ref, MIXER_WIDTH))


def _swa_merge_out(outs, lses, zb, mem_kv, layer, w, x, g, *, rows_per_batch):
    tm = ROW_TILE
    tiles_per_batch = rows_per_batch // tm

    def group_tile(width, dil):
        sub = _sub_block(dil)
        return lambda tile: pl.BlockSpec(
            (None, tm // sub, dil, sub // dil, width),
            lambda s: (tile(s) // tiles_per_batch, tile(s) % tiles_per_batch, 0, 0, 0))

    specs = ([group_tile(GROUP_WIDTH, dil) for _, dil in SWA_PATTERN]
             + [group_tile(HEAD_DIM, dil) for _, dil in SWA_PATTERN]
             + [lambda tile: pl.BlockSpec((None, tm, MEM_WIDTH), lambda s: (3 * N_SWA_GROUPS, tile(s), 0))])
    return _mix_out(_merge_stage, (*outs, *lses, zb), specs, [], mem_kv, layer, w, x, g,
                    tm=tm, rows_per_batch=rows_per_batch, name="swa_merge_out")


def _sample_mem_attention(q_row, kv_ref, o_ref, col0):
    for h in range(N_MEM_HEADS):
        lo, hi = h * HEAD_DIM, (h + 1) * HEAD_DIM
        q = q_row[:, lo:hi]
        k = kv_ref[:, 0, h, :]
        v = kv_ref[:, 1, h, :]
        s = jnp.sum(k * q, axis=1, keepdims=True) * ATTN_SCALE
        m = jnp.max(s, axis=0, keepdims=True)
        p = jnp.exp(s - m)
        den = jnp.sum(p, axis=0, keepdims=True)
        o_ref[:, col0 + lo:col0 + hi] = jnp.sum(p * v, axis=0, keepdims=True) / den


def _sample_mix_a_kernel(z_ref, gv_ref, w0_ref, b0_ref, kv_ref, o_ref, vrow_ref):
    u = z_ref[:, 0:MIXER_WIDTH]
    v = _rms(z_ref[:, MIXER_WIDTH:2 * MIXER_WIDTH], gv_ref[...])
    vrow_ref[...] = v
    o_ref[:, 0:MIXER_WIDTH] = u * (w0_ref[...] * v + b0_ref[...])
    _sample_mem_attention(z_ref[:, 2 * MIXER_WIDTH:2 * MIXER_WIDTH + MEM_WIDTH], kv_ref, o_ref, MIXER_WIDTH)


def _sample_mix_a(z, g_v, w_s, b_s, mem_kv, layer):
    bd = mem_kv.shape[1]
    w0 = jnp.repeat(w_s[:, 0, 0], GROUP_DIM_A).reshape(1, MIXER_WIDTH)
    b0 = jnp.repeat(b_s[:, 0], GROUP_DIM_A).reshape(1, MIXER_WIDTH)
    width = z.shape[1]
    vec = lambda i: (0, 0)
    return pl.pallas_call(
        _sample_mix_a_kernel,
        grid=(bd,),
        in_specs=[
            pl.BlockSpec((None, 1, width), lambda i: (i, 0, 0)),
            pl.BlockSpec((1, MIXER_WIDTH), vec),
            pl.BlockSpec((1, MIXER_WIDTH), vec),
            pl.BlockSpec((1, MIXER_WIDTH), vec),
            _mem_kv_spec(layer, lambda i: i),
        ],
        out_specs=[
            pl.BlockSpec((None, 1, D_MODEL), lambda i: (i, 0, 0)),
            pl.BlockSpec((None, 1, MIXER_WIDTH), lambda i: (i, 0, 0)),
        ],
        out_shape=[
            jax.ShapeDtypeStruct((bd, 1, D_MODEL), F32),
            jax.ShapeDtypeStruct((bd, 1, MIXER_WIDTH), F32),
        ],
        compiler_params=_params("parallel"),
        name="sample_mix_a",
    )(z[:bd].reshape(bd, 1, width), g_v.reshape(1, MIXER_WIDTH), w0, b0, mem_kv)


def _sample_mix_b_kernel(z_ref, c0_ref, c1_ref, c2_ref, bcol_ref, bnew_ref, kv_ref, o_ref):
    caches = (c0_ref, c1_ref, c2_ref)
    outs = [[None] * HEADS_PER_GROUP for _ in range(N_SWA_GROUPS)]
    lses = [[None] * HEADS_PER_GROUP for _ in range(N_SWA_GROUPS)]
    for g in range(N_SWA_GROUPS):
        for h in range(HEADS_PER_GROUP):
            hd = g * HEADS_PER_GROUP + h
            lo, hi = h * HEAD_DIM, (h + 1) * HEAD_DIM
            q = z_ref[:, hd * HEAD_DIM:(hd + 1) * HEAD_DIM]
            k_new = z_ref[:, MIXER_WIDTH + hd * HEAD_DIM:MIXER_WIDTH + (hd + 1) * HEAD_DIM]
            v_new = z_ref[:, 2 * MIXER_WIDTH + hd * HEAD_DIM:2 * MIXER_WIDTH + (hd + 1) * HEAD_DIM]
            kc = caches[g][:, 0, h, :]
            vc = caches[g][:, 1, h, :]
            s_c = jnp.sum(kc * q, axis=1, keepdims=True) * ATTN_SCALE + bcol_ref[g][:, h:h + 1]
            s_n = jnp.sum(k_new * q, axis=1, keepdims=True) * ATTN_SCALE + bnew_ref[:, hd:hd + 1]
            m = jnp.maximum(jnp.max(s_c, axis=0, keepdims=True), s_n)
            p_c = jnp.exp(s_c - m)
            p_n = jnp.exp(s_n - m)
            den = jnp.sum(p_c, axis=0, keepdims=True) + p_n
            outs[g][h] = (jnp.sum(p_c * vc, axis=0, keepdims=True) + p_n * v_new) / den
            lses[g][h] = m + jnp.log(den)
    for h in range(HEADS_PER_GROUP):
        ls = [lses[g][h] for g in range(N_SWA_GROUPS)]
        mx = jnp.maximum(jnp.maximum(ls[0], ls[1]), ls[2])
        es = [jnp.exp(l - mx) for l in ls]
        tot = es[0] + es[1] + es[2]
        for g in range(N_SWA_GROUPS):
            c0 = g * GROUP_WIDTH + h * HEAD_DIM
            o_ref[:, c0:c0 + HEAD_DIM] = outs[g][h] * (es[g] / tot)
    _sample_mem_attention(z_ref[:, 3 * MIXER_WIDTH:3 * MIXER_WIDTH + MEM_WIDTH], kv_ref, o_ref, MIXER_WIDTH)


def _sample_mix_b(z, win_caches, swa_layer, bias_groups, mem_kv, layer):
    bd = mem_kv.shape[1]
    width = z.shape[1]
    cache_views, cache_specs = [], []
    for g, (win, dil) in enumerate(SWA_PATTERN):
        c = win_caches[g]
        cache_views.append(c.reshape(c.shape[0], bd, win // dil, dil, 2, HEADS_PER_GROUP, HEAD_DIM))
        cache_specs.append(pl.BlockSpec((None, None, N_BACK, None, 2, HEADS_PER_GROUP, HEAD_DIM),
                                        lambda i: (swa_layer, i, 0, 0, 0, 0, 0)))
    bcol = jnp.stack([bg[:, N_BACK:0:-1].T for bg in bias_groups], axis=0)
    bnew = jnp.concatenate([bg[:, 0] for bg in bias_groups])
    bnew = jnp.pad(bnew, (0, HEAD_DIM - bnew.shape[0])).reshape(1, HEAD_DIM)
    return pl.pallas_call(
        _sample_mix_b_kernel,
        grid=(bd,),
        in_specs=[pl.BlockSpec((None, 1, width), lambda i: (i, 0, 0))] + cache_specs + [
            pl.BlockSpec((N_SWA_GROUPS, N_BACK, HEADS_PER_GROUP), lambda i: (0, 0, 0)),
            pl.BlockSpec((1, HEAD_DIM), lambda i: (0, 0)),
            _mem_kv_spec(layer, lambda i: i),
        ],
        out_specs=pl.BlockSpec((None, 1, D_MODEL), lambda i: (i, 0, 0)),
        out_shape=jax.ShapeDtypeStruct((bd, 1, D_MODEL), F32),
        compiler_params=_params("parallel"),
        name="sample_mix_b",
    )(z[:bd].reshape(bd, 1, width), *cache_views, bcol, bnew, mem_kv)


def _t5_bucket(dist):
    nf = jnp.maximum(dist, MAX_EXACT).astype(F32)
    large = MAX_EXACT + (jnp.log(nf / MAX_EXACT) / math.log(MAX_DISTANCE / MAX_EXACT)
                         * (N_BUCKETS - MAX_EXACT)).astype(jnp.int32)
    large = jnp.minimum(large, N_BUCKETS - 1)
    return jnp.where(dist < MAX_EXACT, dist, large)


def _group_bias(rel_bias, g, dil):
    dist = jnp.arange(N_BACK + 1, dtype=jnp.int32) * dil
    b = rel_bias[_t5_bucket(dist)][:, g * HEADS_PER_GROUP:(g + 1) * HEADS_PER_GROUP]
    return b.T.astype(F32)


def _band_tables_kernel(b_ref, o_ref):
    n, rows, width = o_ref.shape
    for x in range(n):
        row = jnp.broadcast_to(b_ref[x:x + 1, :], (rows, width))
        o_ref[x] = pltpu.roll(row, 0, 1, stride=1, stride_axis=0)


def _band_tables(bias_groups):
    width = 2 * N_BACK
    rows = []
    for bias_j in bias_groups:
        masked = jnp.full((HEADS_PER_GROUP, N_BACK - 1), NEG_INF, F32)
        rows.append(jnp.concatenate([bias_j[:, :1], masked, bias_j[:, N_BACK:0:-1]], axis=1))
        rows.append(jnp.concatenate([bias_j[:, ::-1], masked], axis=1))
    base = jnp.stack(rows, axis=0).reshape(-1, width)
    tabs = pl.pallas_call(
        _band_tables_kernel,
        out_shape=jax.ShapeDtypeStruct((base.shape[0], N_BACK, width), F32),
        name="band_tables",
    )(base)
    return tabs.reshape(N_SWA_GROUPS, 2, HEADS_PER_GROUP, N_BACK, width)


def _kv_tail_kernel(k_ref, v_ref, o_ref, *, dil):
    rows = k_ref.shape[0]
    for kv, ref in enumerate((k_ref, v_ref)):
        x = ref[...]
        if dil > 1:
            inv = _residue_major_perm(PERM_BLOCK, dil, transpose=True)
            x = jnp.concatenate([jnp.dot(inv, x[s:s + PERM_BLOCK, :], preferred_element_type=F32)
                                 for s in range(0, rows, PERM_BLOCK)], axis=0)
        x = x.astype(F32)
        for h in range(HEADS_PER_GROUP):
            o_ref[:, kv, h, :] = x[:, h * HEAD_DIM:(h + 1) * HEAD_DIM]


def _kv_tail(zb, g, *, batch, seq):
    win, dil = SWA_PATTERN[g]
    rows = min(win, FFN_ROW_TILE)
    first = (seq - win) // rows
    per_batch = seq // rows

    def tile(t):
        return pl.BlockSpec((None, rows, GROUP_WIDTH), lambda b, s: (t, b * per_batch + first + s, 0))

    return pl.pallas_call(
        functools.partial(_kv_tail_kernel, dil=dil),
        grid=(batch, win // rows),
        in_specs=[tile(N_SWA_GROUPS + g), tile(2 * N_SWA_GROUPS + g)],
        out_specs=pl.BlockSpec((None, rows, 2, HEADS_PER_GROUP, HEAD_DIM), lambda b, s: (b, s, 0, 0, 0)),
        out_shape=jax.ShapeDtypeStruct((batch, win, 2, HEADS_PER_GROUP, HEAD_DIM), F32),
        compiler_params=_params("parallel", "parallel"),
        name=f"kv_tail{g}",
    )(zb, zb)


def kernel(x_prompt, x_sample, mem_prompt, cache_mem_kv, cache_win128_kv, cache_win512_kv, cache_win2048_kv, rel_bias, norm_mix_pre, norm_mix_post, norm_ffn_pre, norm_ffn_post, norm_mem, w_mem_kv, w_in_a, norm_v_a, w_spatial_a, b_spatial_a, w_in_b, w_out, w_ffn_up, w_ffn_down):
    batch, seq, _ = x_prompt.shape
    bd = x_sample.shape[0]
    depth = w_out.shape[0]
    m_p = batch * seq
    win_caches = (cache_win128_kv, cache_win512_kv, cache_win2048_kv)

    bias_groups = [_group_bias(rel_bias, g, dil) for g, (_, dil) in enumerate(SWA_PATTERN)]
    band_tables = _band_tables(bias_groups)

    yp = x_prompt.reshape(m_p, D_MODEL)
    ys = jnp.pad(x_sample.reshape(bd, D_MODEL), ((0, SAMPLE_PAD - bd), (0, 0)))
    mem_rows = mem_prompt.reshape(batch * N_MEM, D_MODEL)

    mem_kv_p = _mem_kv(mem_rows, norm_mem, w_mem_kv, batch=batch)
    chunk_v_s = []
    win_p = [[] for _ in SWA_PATTERN]
    win_s = [[] for _ in SWA_PATTERN]
    for i in range(depth):
        li = i // 2
        if i % 2 == 0:
            zp, zs = _in_proj_a(yp, ys, norm_mix_pre[i], w_in_a, li, tm=ROW_TILE, tn=CAST_TILE,
                                gelu_cols=2 * MIXER_WIDTH)
            mix_s, v_rows = _sample_mix_a(zs, norm_v_a[li], w_spatial_a[li], b_spatial_a[li], cache_mem_kv, i)
            chunk_v_s.append(v_rows)
            ys, w_o = _sample_out_proj(mix_s, w_out, i, ys, norm_mix_post[i])
            yp = _gmlp_mix_out(zp, norm_v_a[li], w_spatial_a[li], b_spatial_a[li], mem_kv_p, i,
                               w_o, yp, norm_mix_post[i], tm=ROW_TILE, rows_per_batch=seq)
        else:
            zb, zs = _in_proj_b(yp, ys, norm_mix_pre[i], w_in_b, li, tm=ROW_TILE)
            outs, lses = [], []
            for g, (win, dil) in enumerate(SWA_PATTERN):
                o, lse = _swa_group(zb, band_tables, g, batch=batch, seq=seq)
                outs.append(o)
                lses.append(lse)
                win_p[g].append(_kv_tail(zb, g, batch=batch, seq=seq))
                kv_new = zs[:bd, MIXER_WIDTH:3 * MIXER_WIDTH]
                kv_new = kv_new.reshape(bd, 1, 2, N_SWA_GROUPS, HEADS_PER_GROUP, HEAD_DIM)[:, :, :, g]
                win_s[g].append(kv_new)
            mix_s = _sample_mix_b(zs, win_caches, li, bias_groups, cache_mem_kv, i)
            ys, w_o = _sample_out_proj(mix_s, w_out, i, ys, norm_mix_post[i])
            yp = _swa_merge_out(outs, lses, zb, mem_kv_p, i, w_o, yp, norm_mix_post[i], rows_per_batch=seq)
        yp, ys, w_ffn = _ffn_head(yp, ys, norm_ffn_pre[i], norm_ffn_post[i], w_ffn_up, w_ffn_down, i,
                                  tm=FFN_ROW_TILE, tf=HEAD_FF_TILE)
        yp = _ffn(yp, norm_ffn_pre[i], norm_ffn_post[i], w_ffn, tm=FFN_ROW_TILE, tf=FF_TILE, first_tile=1)

    return (
        yp.reshape(batch, seq, D_MODEL),
        ys[:bd].reshape(bd, 1, D_MODEL),
        mem_kv_p,
        jnp.stack(chunk_v_s, axis=0),
        jnp.stack(win_p[0], axis=0),
        jnp.stack(win_p[1], axis=0),
        jnp.stack(win_p[2], axis=0),
        jnp.stack(win_s[0], axis=0),
        jnp.stack(win_s[1], axis=0),
        jnp.stack(win_s[2], axis=0),
    )
```

```python
import functools
import math

import jax
import jax.numpy as jnp
from jax import lax
from jax.experimental import pallas as pl
from jax.experimental.pallas import tpu as pltpu

F32 = jnp.float32
BF16 = jnp.bfloat16

D_MODEL = 2048
HEAD_DIM = 128
N_MEM = 256
N_MEM_HEADS = 4
MEM_WIDTH = N_MEM_HEADS * HEAD_DIM
MIXER_WIDTH = D_MODEL - MEM_WIDTH
CHUNK = 128
N_GROUPS_A = 4
GROUP_DIM_A = MIXER_WIDTH // N_GROUPS_A
SWA_PATTERN = ((128, 1), (512, 4), (2048, 16))
N_SWA_GROUPS = len(SWA_PATTERN)
HEADS_PER_GROUP = 4
GROUP_WIDTH = HEADS_PER_GROUP * HEAD_DIM
N_BACK = 128
N_BUCKETS = 32
MAX_EXACT = N_BUCKETS // 2
MAX_DISTANCE = 2048
D_FF = 5632
EPS = 1e-6
NEG_INF = -1e30
ATTN_SCALE = HEAD_DIM ** -0.5
SAMPLE_PAD = 16
PERM_BLOCK = 256
SWA_BLOCKS_PER_TRIP = 8
OUT_CHUNK = 256

ROW_TILE = 512
FFN_ROW_TILE = 1024
FF_TILE = 512
HEAD_FF_TILE = 256
CAST_TILE = 512

V7X_VMEM_BYTES = 64 * 1024 * 1024
VMEM_LIMIT = V7X_VMEM_BYTES - 8 * 1024 * 1024
FFN_VMEM_LIMIT = V7X_VMEM_BYTES - 2 * 1024 * 1024


def _params(*sem, vmem_limit=VMEM_LIMIT):
    return pltpu.CompilerParams(dimension_semantics=sem, vmem_limit_bytes=vmem_limit)


def _gelu(x):
    return 0.5 * x * (1.0 + jnp.tanh(0.7978845608028654 * (x + 0.044715 * (x * x * x))))


def _rms(x, g):
    return x * lax.rsqrt(jnp.mean(x * x, axis=-1, keepdims=True) + EPS) * g


def _log2(n):
    assert n & (n - 1) == 0
    return n.bit_length() - 1


def _residue_major_perm(tm, dil, transpose=False):
    n = tm // dil
    row = lax.broadcasted_iota(jnp.int32, (tm, tm), 0)
    col = lax.broadcasted_iota(jnp.int32, (tm, tm), 1)
    dst, src = (col, row) if transpose else (row, col)
    want = lax.shift_left(jnp.bitwise_and(dst, n - 1), _log2(dil)) + lax.shift_right_logical(dst, _log2(n))
    return (src == want).astype(BF16)


def _in_proj_a_kernel(x_ref, xs_ref, g_ref, w_ref, zp_ref, zs_ref, wb_ref, xn_ref, *, n_col, gelu_cols):
    tm = x_ref.shape[0]
    tn = w_ref.shape[1]
    s = pl.program_id(0)

    def prompt_cols(acc, c0):
        col = lax.broadcasted_iota(jnp.int32, acc.shape, 1) + c0
        return jnp.where(col < gelu_cols, _gelu(acc), acc * ATTN_SCALE).astype(zp_ref.dtype)

    @pl.when(s == 0)
    def _():
        xn_ref[:tm, :] = _rms(x_ref[...], g_ref[...]).astype(BF16)
        xn_ref[tm:, :] = _rms(xs_ref[...], g_ref[...]).astype(BF16)

    for j in range(n_col):
        @pl.when(s == j)
        def _(j=j):
            cols = slice(j * tn, (j + 1) * tn)
            w = w_ref[...].astype(BF16)
            wb_ref[:, cols] = w
            acc = jnp.dot(xn_ref[...], w, preferred_element_type=F32)
            zp_ref[:, cols] = prompt_cols(acc[:tm, :], j * tn)
            sample = acc[tm:, :]
            zs_ref[:, cols] = _gelu(sample) if (j + 1) * tn <= gelu_cols else sample

    @pl.when(s >= n_col)
    def _():
        xn = _rms(x_ref[...], g_ref[...]).astype(BF16)
        acc = jnp.dot(xn, wb_ref[...], preferred_element_type=F32)
        zp_ref[...] = prompt_cols(acc, 0)


def _in_proj_a(x, xs, g, w, layer, *, tm, tn, gelu_cols):
    m, k = x.shape
    ms = xs.shape[0]
    n = w.shape[2]
    n_col, n_row = n // tn, m // tm
    assert gelu_cols % tn == 0
    row = lambda s: jnp.maximum(s - (n_col - 1), 0)
    return pl.pallas_call(
        functools.partial(_in_proj_a_kernel, n_col=n_col, gelu_cols=gelu_cols),
        grid=(n_col + n_row - 1,),
        in_specs=[
            pl.BlockSpec((tm, k), lambda s: (row(s), 0)),
            pl.BlockSpec((ms, k), lambda s: (0, 0)),
            pl.BlockSpec((1, k), lambda s: (0, 0)),
            pl.BlockSpec((None, k, tn), lambda s: (layer, 0, jnp.minimum(s, n_col - 1))),
        ],
        out_specs=[
            pl.BlockSpec((tm, n), lambda s: (row(s), 0)),
            pl.BlockSpec((ms, n), lambda s: (0, 0)),
        ],
        out_shape=[
            jax.ShapeDtypeStruct((m, n), BF16),
            jax.ShapeDtypeStruct((ms, n), F32),
        ],
        scratch_shapes=[pltpu.VMEM((k, n), BF16), pltpu.VMEM((tm + ms, k), BF16)],
        compiler_params=_params("arbitrary"),
        name="in_proj_a",
    )(x, xs, g.reshape(1, k), w)


def _mem_kv_kernel(x_ref, g_ref, w_ref, o_ref):
    xn = _rms(x_ref[...], g_ref[...]).astype(BF16)
    acc = jnp.dot(xn, w_ref[...].astype(BF16), preferred_element_type=F32)
    for kv in range(2):
        for h in range(N_MEM_HEADS):
            c0 = (kv * N_MEM_HEADS + h) * HEAD_DIM
            o_ref[:, kv, h, :] = acc[:, c0:c0 + HEAD_DIM]


def _mem_kv(mem_rows, g, w, *, batch):
    m, k = mem_rows.shape
    layers = w.shape[0]
    out = pl.pallas_call(
        _mem_kv_kernel,
        grid=(layers,),
        in_specs=[
            pl.BlockSpec((m, k), lambda l: (0, 0)),
            pl.BlockSpec((None, 1, k), lambda l: (l, 0, 0)),
            pl.BlockSpec((None, k, 2 * MEM_WIDTH), lambda l: (l, 0, 0)),
        ],
        out_specs=pl.BlockSpec((None, m, 2, N_MEM_HEADS, HEAD_DIM), lambda l: (l, 0, 0, 0, 0)),
        out_shape=jax.ShapeDtypeStruct((layers, m, 2, N_MEM_HEADS, HEAD_DIM), F32),
        compiler_params=_params("parallel"),
        name="mem_kv",
    )(mem_rows, g.reshape(layers, 1, k), w)
    return out.reshape(layers, batch, m // batch, 2, N_MEM_HEADS, HEAD_DIM)


def _in_proj_b_kernel(x_ref, xs_ref, g_ref, w_ref, o_ref, zs_ref, wb_ref, xn_ref):
    tm = x_ref.shape[0]
    n_tiles = o_ref.shape[0]
    n_qkv = 3 * N_SWA_GROUPS
    s = pl.program_id(0)

    def row_orders():
        xn = _rms(x_ref[...], g_ref[...]).astype(BF16)
        xn_ref[0, :tm, :] = xn
        for g in range(1, N_SWA_GROUPS):
            perm = _residue_major_perm(PERM_BLOCK, SWA_PATTERN[g][1])
            for r in range(0, tm, PERM_BLOCK):
                xn_ref[g, r:r + PERM_BLOCK, :] = jnp.dot(
                    perm, xn[r:r + PERM_BLOCK, :], preferred_element_type=F32).astype(BF16)

    def prompt_tile(t, acc):
        if t < N_SWA_GROUPS or t == n_qkv:
            acc = acc * ATTN_SCALE
        o_ref[t] = acc.astype(o_ref.dtype)

    src = lambda t: t % N_SWA_GROUPS if t < n_qkv else 0

    @pl.when(s == 0)
    def _():
        row_orders()
        xs = _rms(xs_ref[...], g_ref[...]).astype(BF16)
        for g in range(N_SWA_GROUPS):
            xn_ref[g, tm:, :] = xs

    for t in range(n_tiles):
        @pl.when(s == t)
        def _(t=t):
            cols = slice(t * GROUP_WIDTH, (t + 1) * GROUP_WIDTH)
            w = w_ref[...].astype(BF16)
            wb_ref[:, cols] = w
            acc = jnp.dot(xn_ref[src(t)], w, preferred_element_type=F32)
            prompt_tile(t, acc[:tm, :])
            zs_ref[:, cols] = acc[tm:, :]

    @pl.when(s >= n_tiles)
    def _():
        row_orders()
        for t in range(n_tiles):
            w = wb_ref[:, t * GROUP_WIDTH:(t + 1) * GROUP_WIDTH]
            prompt_tile(t, jnp.dot(xn_ref[src(t), :tm, :], w, preferred_element_type=F32))


def _in_proj_b(x, xs, g, w, layer, *, tm):
    m, k = x.shape
    ms = xs.shape[0]
    n = w.shape[2]
    n_tiles = n // GROUP_WIDTH
    row = lambda s: jnp.maximum(s - (n_tiles - 1), 0)
    return pl.pallas_call(
        _in_proj_b_kernel,
        grid=(n_tiles + m // tm - 1,),
        in_specs=[
            pl.BlockSpec((tm, k), lambda s: (row(s), 0)),
            pl.BlockSpec((ms, k), lambda s: (0, 0)),
            pl.BlockSpec((1, k), lambda s: (0, 0)),
            pl.BlockSpec((None, k, GROUP_WIDTH), lambda s: (layer, 0, jnp.minimum(s, n_tiles - 1))),
        ],
        out_specs=[
            pl.BlockSpec((n_tiles, tm, GROUP_WIDTH), lambda s: (0, row(s), 0)),
            pl.BlockSpec((ms, n), lambda s: (0, 0)),
        ],
        out_shape=[
            jax.ShapeDtypeStruct((n_tiles, m, GROUP_WIDTH), BF16),
            jax.ShapeDtypeStruct((ms, n), F32),
        ],
        scratch_shapes=[pltpu.VMEM((k, n), BF16), pltpu.VMEM((N_SWA_GROUPS, tm + ms, k), BF16)],
        compiler_params=_params("arbitrary"),
        name="in_proj_b",
    )(x, xs, g.reshape(1, k), w)


MEM_KV_SCRATCH = [pltpu.VMEM((N_MEM_HEADS, N_MEM, HEAD_DIM), BF16),
                  pltpu.VMEM((N_MEM_HEADS, N_MEM, 2 * HEAD_DIM), BF16)]


def _prepare_mem_kv(kv_ref, kb_ref, vb_ref):
    for h in range(N_MEM_HEADS):
        kb_ref[h] = kv_ref[:, 0, h, :].astype(BF16)
        vb_ref[h, :, :HEAD_DIM] = kv_ref[:, 1, h, :].astype(BF16)
        vb_ref[h, :, HEAD_DIM:] = jnp.ones((N_MEM, HEAD_DIM), BF16)


def _mem_attention_parts(q_ref, kb_ref, vb_ref, o_ref, col0):
    def head(h):
        lo, hi = h * HEAD_DIM, (h + 1) * HEAD_DIM
        s = lax.dot_general(q_ref[:, lo:hi], kb_ref[h], (((1,), (1,)), ((), ())), preferred_element_type=F32)
        m = jnp.max(s, axis=1, keepdims=True)
        p = jnp.exp(s - m).astype(BF16)
        ov = jnp.dot(p, vb_ref[h], preferred_element_type=F32)
        o_ref[:, col0 + lo:col0 + hi] = (ov[:, :HEAD_DIM] / ov[:, HEAD_DIM:]).astype(o_ref.dtype)

    return [functools.partial(head, h) for h in range(N_MEM_HEADS)]


def _gmlp_stage(u_ref, v_ref, q_ref, gv_ref, ws_ref, bs_ref, kb_ref, vb_ref, vn_ref, o_ref):
    tm = u_ref.shape[0]

    def norm_v():
        vn_ref[...] = _rms(v_ref[...].astype(F32), gv_ref[...]).astype(BF16)

    def group(g):
        row = lax.broadcasted_iota(jnp.int32, (CHUNK, CHUNK), 0)
        col = lax.broadcasted_iota(jnp.int32, (CHUNK, CHUNK), 1)
        w = jnp.where(row >= col, ws_ref[g], 0.0).astype(BF16)
        b = bs_ref[:, g:g + 1]
        c0, c1 = g * GROUP_DIM_A, (g + 1) * GROUP_DIM_A
        for c in range(tm // CHUNK):
            r0, r1 = c * CHUNK, (c + 1) * CHUNK
            s = jnp.dot(w, vn_ref[r0:r1, c0:c1], preferred_element_type=F32) + b
            o_ref[r0:r1, c0:c1] = (u_ref[r0:r1, c0:c1].astype(F32) * s).astype(o_ref.dtype)

    return ([norm_v] + [functools.partial(group, g) for g in range(N_GROUPS_A)]
            + _mem_attention_parts(q_ref, kb_ref, vb_ref, o_ref, MIXER_WIDTH))


def _mem_kv_spec(layer, batch_of):
    return pl.BlockSpec((None, None, N_MEM, 2, N_MEM_HEADS, HEAD_DIM),
                        lambda i: (layer, batch_of(i), 0, 0, 0, 0))


def _mix_out_kernel(*refs, stage, n_in, n_tiles, tiles_per_batch):
    mix_in = refs[:n_in]
    kv_ref, w_ref, x_ref, g_ref, o_ref, buf0_ref, buf1_ref, acc_ref, kb_ref, vb_ref = refs[n_in:n_in + 10]
    extra = refs[n_in + 10:]
    s = pl.program_id(0)

    @pl.when(s == 0)
    def _():
        buf1_ref[...] = jnp.zeros_like(buf1_ref)

    @pl.when(jnp.minimum(s, n_tiles - 1) % tiles_per_batch == 0)
    def _():
        _prepare_mem_kv(kv_ref, kb_ref, vb_ref)

    def step(dst_ref, src_ref):
        parts = stage(*mix_in, kb_ref, vb_ref, *extra, dst_ref)
        n_chunks = D_MODEL // OUT_CHUNK
        for c in range(n_chunks):
            cols = slice(c * OUT_CHUNK, (c + 1) * OUT_CHUNK)
            acc_ref[:, cols] = jnp.dot(src_ref[...], w_ref[:, cols], preferred_element_type=F32)
            for part in parts[c * len(parts) // n_chunks:(c + 1) * len(parts) // n_chunks]:
                part()
        o_ref[...] = x_ref[...] + _rms(acc_ref[...], g_ref[...])

    @pl.when(s % 2 == 0)
    def _():
        step(buf0_ref, buf1_ref)

    @pl.when(s % 2 == 1)
    def _():
        step(buf1_ref, buf0_ref)


def _mix_out(stage, mix_inputs, mix_specs, extra_scratch, mem_kv, layer, w, x, g, *, tm, rows_per_batch, name):
    m = x.shape[0]
    n_tiles = m // tm
    tiles_per_batch = rows_per_batch // tm
    mix_tile = lambda s: jnp.minimum(s, n_tiles - 1)
    out_tile = lambda s: jnp.maximum(s - 1, 0)
    row_spec = pl.BlockSpec((tm, D_MODEL), lambda s: (out_tile(s), 0))
    return pl.pallas_call(
        functools.partial(_mix_out_kernel, stage=stage, n_in=len(mix_inputs), n_tiles=n_tiles,
                          tiles_per_batch=tiles_per_batch),
        grid=(n_tiles + 1,),
        in_specs=[spec(mix_tile) for spec in mix_specs] + [
            _mem_kv_spec(layer, lambda s: mix_tile(s) // tiles_per_batch),
            pl.BlockSpec((D_MODEL, D_MODEL), lambda s: (0, 0), pipeline_mode=pl.Buffered(1)),
            row_spec,
            pl.BlockSpec((1, D_MODEL), lambda s: (0, 0)),
        ],
        out_specs=row_spec,
        out_shape=jax.ShapeDtypeStruct((m, D_MODEL), F32),
        scratch_shapes=[pltpu.VMEM((tm, D_MODEL), BF16), pltpu.VMEM((tm, D_MODEL), BF16),
                        pltpu.VMEM((tm, D_MODEL), F32)] + MEM_KV_SCRATCH + list(extra_scratch),
        compiler_params=_params("arbitrary"),
        name=name,
    )(*mix_inputs, mem_kv, w, x, g.reshape(1, D_MODEL))


def _gmlp_mix_out(zact, g_v, w_s, b_s, mem_kv, layer, w, x, g, *, tm, rows_per_batch):
    const = lambda shape: (lambda tile: pl.BlockSpec(shape, lambda s: (0,) * len(shape)))
    specs = [
        lambda tile: pl.BlockSpec((tm, MIXER_WIDTH), lambda s: (tile(s), 0)),
        lambda tile: pl.BlockSpec((tm, MIXER_WIDTH), lambda s: (tile(s), 1)),
        lambda tile: pl.BlockSpec((tm, MEM_WIDTH), lambda s: (tile(s), 2 * MIXER_WIDTH // MEM_WIDTH)),
        const((1, MIXER_WIDTH)),
        const((N_GROUPS_A, CHUNK, CHUNK)),
        const((CHUNK, N_GROUPS_A)),
    ]
    return _mix_out(_gmlp_stage, (zact, zact, zact, g_v.reshape(1, MIXER_WIDTH), w_s, b_s.T), specs,
                    [pltpu.VMEM((tm, MIXER_WIDTH), BF16)], mem_kv, layer, w, x, g,
                    tm=tm, rows_per_batch=rows_per_batch, name="gmlp_mix_out")


def _out_proj_cast_kernel(mix_ref, w_ref, x_ref, g_ref, o_ref, wb_ref, acc_ref):
    j = pl.program_id(0)
    w = w_ref[...].astype(BF16)
    wb_ref[...] = w
    acc_ref[j] = jnp.dot(mix_ref[...], w, preferred_element_type=F32)

    @pl.when(j == pl.num_programs(0) - 1)
    def _():
        o = jnp.concatenate([acc_ref[t] for t in range(acc_ref.shape[0])], axis=1)
        o_ref[...] = x_ref[...] + _rms(o, g_ref[...])


def _sample_out_proj(mix, w, layer, x, g):
    bd = mix.shape[0]
    mix = jnp.pad(mix.reshape(bd, D_MODEL), ((0, x.shape[0] - bd), (0, 0))).astype(BF16)
    return _out_proj_cast(mix, w, layer, x, g, tn=CAST_TILE)


def _out_proj_cast(mix, w, layer, x, g, *, tn):
    m = x.shape[0]
    n_tiles = D_MODEL // tn
    return pl.pallas_call(
        _out_proj_cast_kernel,
        grid=(n_tiles,),
        in_specs=[
            pl.BlockSpec((m, D_MODEL), lambda j: (0, 0)),
            pl.BlockSpec((None, D_MODEL, tn), lambda j: (layer, 0, j)),
            pl.BlockSpec((m, D_MODEL), lambda j: (0, 0)),
            pl.BlockSpec((1, D_MODEL), lambda j: (0, 0)),
        ],
        out_specs=[
            pl.BlockSpec((m, D_MODEL), lambda j: (0, 0)),
            pl.BlockSpec((D_MODEL, tn), lambda j: (0, j)),
        ],
        out_shape=[
            jax.ShapeDtypeStruct((m, D_MODEL), F32),
            jax.ShapeDtypeStruct((D_MODEL, D_MODEL), BF16),
        ],
        scratch_shapes=[pltpu.VMEM((n_tiles, m, tn), F32)],
        compiler_params=_params("arbitrary"),
        name="out_proj_cast",
    )(mix, w, x, g.reshape(1, D_MODEL))


def _accumulate(o_ref, ssq_ref, part):
    new = o_ref[...] + part
    o_ref[...] = new
    sq = new * new
    ssq_ref[...] = functools.reduce(
        jnp.add, [sq[:, c:c + HEAD_DIM] for c in range(0, sq.shape[1], HEAD_DIM)])


def _residual_norm(x_ref, o_ref, ssq_ref, g_ref):
    ms = jnp.sum(ssq_ref[...], axis=-1, keepdims=True) * (1.0 / o_ref.shape[1])
    o_ref[...] = x_ref[...] + o_ref[...] * lax.rsqrt(ms + EPS) * g_ref[...]


def _ffn_kernel(x_ref, gpre_ref, gpost_ref, wg_ref, wl_ref, wd_ref, o_ref, xn_ref, ssq_ref):
    f = pl.program_id(1)

    @pl.when(f == 0)
    def _():
        xn_ref[...] = _rms(x_ref[...], gpre_ref[...]).astype(BF16)
        o_ref[...] = jnp.zeros_like(o_ref)

    xn = xn_ref[...]
    hg = jnp.dot(xn, wg_ref[...], preferred_element_type=F32)
    hl = jnp.dot(xn, wl_ref[...], preferred_element_type=F32)
    a = (hg * jax.nn.sigmoid(hg) * hl).astype(BF16)
    _accumulate(o_ref, ssq_ref, jnp.dot(a, wd_ref[...], preferred_element_type=F32))

    @pl.when(f == pl.num_programs(1) - 1)
    def _():
        _residual_norm(x_ref, o_ref, ssq_ref, gpost_ref)


def _ffn(x, g_pre, g_post, weights, *, tm, tf, first_tile):
    m = x.shape[0]
    nf = D_FF // tf
    rows = pl.BlockSpec((tm, D_MODEL), lambda i, f: (i + first_tile, 0))
    return pl.pallas_call(
        _ffn_kernel,
        grid=(m // tm - first_tile, nf),
        in_specs=[
            rows,
            pl.BlockSpec((1, D_MODEL), lambda i, f: (0, 0)),
            pl.BlockSpec((1, D_MODEL), lambda i, f: (0, 0)),
            pl.BlockSpec((D_MODEL, tf), lambda i, f: (0, f)),
            pl.BlockSpec((D_MODEL, tf), lambda i, f: (0, f)),
            pl.BlockSpec((tf, D_MODEL), lambda i, f: (f, 0)),
        ],
        out_specs=rows,
        out_shape=jax.ShapeDtypeStruct((m, D_MODEL), F32),
        input_output_aliases={0: 0},
        scratch_shapes=[pltpu.VMEM((tm, D_MODEL), BF16), pltpu.VMEM((tm, HEAD_DIM), F32)],
        compiler_params=_params("parallel", "arbitrary", vmem_limit=FFN_VMEM_LIMIT),
        name="ffn",
    )(x, g_pre.reshape(1, D_MODEL), g_post.reshape(1, D_MODEL), *weights)


def _ffn_head_kernel(x_ref, xs_ref, gpre_ref, gpost_ref, wg_ref, wl_ref, wd_ref,
                     o_ref, os_ref, wgb_ref, wlb_ref, wdb_ref, xn_ref, ssq_ref, ssqs_ref):
    tm = x_ref.shape[0]
    f = pl.program_id(0)

    @pl.when(f == 0)
    def _():
        xn_ref[:tm, :] = _rms(x_ref[...], gpre_ref[...]).astype(BF16)
        xn_ref[tm:, :] = _rms(xs_ref[...], gpre_ref[...]).astype(BF16)
        o_ref[...] = jnp.zeros_like(o_ref)
        os_ref[...] = jnp.zeros_like(os_ref)

    wg, wl, wd = (r[...].astype(BF16) for r in (wg_ref, wl_ref, wd_ref))
    wgb_ref[...] = wg
    wlb_ref[...] = wl
    wdb_ref[...] = wd
    xn = xn_ref[...]
    hg = jnp.dot(xn, wg, preferred_element_type=F32)
    hl = jnp.dot(xn, wl, preferred_element_type=F32)
    a = (hg * jax.nn.sigmoid(hg) * hl).astype(BF16)
    part = jnp.dot(a, wd, preferred_element_type=F32)
    _accumulate(o_ref, ssq_ref, part[:tm, :])
    _accumulate(os_ref, ssqs_ref, part[tm:, :])

    @pl.when(f == pl.num_programs(0) - 1)
    def _():
        _residual_norm(x_ref, o_ref, ssq_ref, gpost_ref)
        _residual_norm(xs_ref, os_ref, ssqs_ref, gpost_ref)


def _ffn_head(x, xs, g_pre, g_post, w_up, w_down, layer, *, tm, tf):
    m = x.shape[0]
    ms = xs.shape[0]
    nf = D_FF // tf
    once = dict(pipeline_mode=pl.Buffered(1))
    head = pl.BlockSpec((tm, D_MODEL), lambda f: (0, 0), **once)
    sample = pl.BlockSpec((ms, D_MODEL), lambda f: (0, 0))
    vec = pl.BlockSpec((1, D_MODEL), lambda f: (0, 0))
    outs = pl.pallas_call(
        _ffn_head_kernel,
        grid=(nf,),
        in_specs=[
            head, sample, vec, vec,
            pl.BlockSpec((None, D_MODEL, tf), lambda f: (layer, 0, f)),
            pl.BlockSpec((None, D_MODEL, tf), lambda f: (layer, 0, nf + f)),
            pl.BlockSpec((None, tf, D_MODEL), lambda f: (layer, f, 0)),
        ],
        out_specs=[
            head, sample,
            pl.BlockSpec((D_MODEL, tf), lambda f: (0, f)),
            pl.BlockSpec((D_MODEL, tf), lambda f: (0, f)),
            pl.BlockSpec((tf, D_MODEL), lambda f: (f, 0)),
        ],
        out_shape=[
            jax.ShapeDtypeStruct((m, D_MODEL), F32),
            jax.ShapeDtypeStruct((ms, D_MODEL), F32),
            jax.ShapeDtypeStruct((D_MODEL, D_FF), BF16),
            jax.ShapeDtypeStruct((D_MODEL, D_FF), BF16),
            jax.ShapeDtypeStruct((D_FF, D_MODEL), BF16),
        ],
        input_output_aliases={0: 0},
        scratch_shapes=[pltpu.VMEM((tm + ms, D_MODEL), BF16), pltpu.VMEM((tm, HEAD_DIM), F32),
                        pltpu.VMEM((ms, HEAD_DIM), F32)],
        compiler_params=_params("arbitrary"),
        name="ffn_head",
    )(x, xs, g_pre.reshape(1, D_MODEL), g_post.reshape(1, D_MODEL), w_up, w_up, w_down)
    return outs[0], outs[1], tuple(outs[2:])


def _swa_kernel(q_ref, k_ref, v_ref, tb_ref, o_ref, lse_ref):
    n_units, n_res, u, _ = q_ref.shape
    per_blk = N_BACK // u
    n_blk = n_units // per_blk
    lane = lax.broadcasted_iota(jnp.int32, (N_BACK, HEAD_DIM), 1)
    ones = jnp.ones((2 * N_BACK, HEAD_DIM), BF16)

    def rows(ref, res, unit0, n_rows, lo, hi):
        return ref[pl.ds(unit0, n_rows // u), res, :, lo:hi].reshape(n_rows, hi - lo)

    def block(res, qu, ku, table):
        n_keys = N_BACK if table == 0 else 2 * N_BACK
        lse_tile = jnp.zeros((N_BACK, HEAD_DIM), F32)
        for h in range(HEADS_PER_GROUP):
            lo, hi = h * HEAD_DIM, (h + 1) * HEAD_DIM
            q = rows(q_ref, res, qu, N_BACK, lo, hi)
            kw = rows(k_ref, res, ku, n_keys, lo, hi)
            vw = rows(v_ref, res, ku, n_keys, lo, hi)
            s = lax.dot_general(q, kw, (((1,), (1,)), ((), ())), preferred_element_type=F32)
            s = s + tb_ref[table, h][:, :n_keys]
            m = jnp.max(s, axis=1, keepdims=True)
            p = jnp.exp(s - m).astype(BF16)
            ov = jnp.dot(p, jnp.concatenate([vw, ones[:n_keys]], axis=1), preferred_element_type=F32)
            den = ov[:, HEAD_DIM:]
            o = ov[:, :HEAD_DIM] / den
            o_ref[pl.ds(qu, per_blk), res, :, lo:hi] = o.reshape(per_blk, u, HEAD_DIM).astype(o_ref.dtype)
            lse_tile = jnp.where(lane == h, m + jnp.log(den), lse_tile)
        lse_ref[pl.ds(qu, per_blk), res, :, :] = lse_tile.reshape(per_blk, u, HEAD_DIM)

    def block_at(res, n):
        block(res, n * per_blk, (n - 1) * per_blk, 1)

    n_trips = (n_blk - 1) // SWA_BLOCKS_PER_TRIP
    for res in range(n_res):
        block(res, 0, 0, 0)

        def body(i, carry, res=res):
            for b in range(SWA_BLOCKS_PER_TRIP):
                block_at(res, 1 + SWA_BLOCKS_PER_TRIP * i + b)
            return carry

        if n_trips > 0:
            lax.fori_loop(0, n_trips, body, 0)
        for n in range(1 + n_trips * SWA_BLOCKS_PER_TRIP, n_blk):
            block_at(res, n)


def _sub_block(dil):
    return N_BACK if dil == 1 else PERM_BLOCK


def _swa_group(zb, tables, g, *, batch, seq):
    dil = SWA_PATTERN[g][1]
    sub = _sub_block(dil)
    n_units, u = seq // sub, sub // dil
    n_blk = n_units * u // N_BACK
    n_res = min(dil, max(1, SWA_BLOCKS_PER_TRIP // n_blk))
    view = zb.reshape(zb.shape[0], batch, n_units, dil, u, GROUP_WIDTH)

    def rows_in(tile):
        return pl.BlockSpec((None, None, n_units, n_res, u, GROUP_WIDTH), lambda i, r: (tile, i, 0, r, 0, 0))

    def rows_out(width):
        return pl.BlockSpec((None, n_units, n_res, u, width), lambda i, r: (i, 0, r, 0, 0))

    return pl.pallas_call(
        _swa_kernel,
        grid=(batch, dil // n_res),
        in_specs=[rows_in(g), rows_in(N_SWA_GROUPS + g), rows_in(2 * N_SWA_GROUPS + g),
                  pl.BlockSpec((None, 2, HEADS_PER_GROUP, N_BACK, 2 * N_BACK), lambda i, r: (g, 0, 0, 0, 0))],
        out_specs=[rows_out(GROUP_WIDTH), rows_out(HEAD_DIM)],
        out_shape=[
            jax.ShapeDtypeStruct((batch, n_units, dil, u, GROUP_WIDTH), BF16),
            jax.ShapeDtypeStruct((batch, n_units, dil, u, HEAD_DIM), F32),
        ],
        compiler_params=_params("parallel", "parallel"),
        name=f"swa_group{g}",
    )(view, view, view, tables)


def _split3(x):
    hi = x.astype(BF16)
    rest = x - hi.astype(F32)
    mid = rest.astype(BF16)
    lo = (rest - mid.astype(F32)).astype(BF16)
    return hi, mid, lo


def _merge_stage(o0_ref, o1_ref, o2_ref, l0_ref, l1_ref, l2_ref, q_ref, kb_ref, vb_ref, o_ref):
    tm = o_ref.shape[0]
    group_refs = ((o0_ref, l0_ref), (o1_ref, l1_ref), (o2_ref, l2_ref))
    outs, lses = [None] * N_SWA_GROUPS, [None] * N_SWA_GROUPS

    def token_order(g):
        o_g, l_g = group_refs[g]
        dil = SWA_PATTERN[g][1]
        o = o_g[...].reshape(tm, GROUP_WIDTH)
        l = l_g[...].reshape(tm, HEAD_DIM)
        if dil > 1:
            inv = _residue_major_perm(PERM_BLOCK, dil, transpose=True)
            l3 = _split3(l)
            o_nat, l_nat = [], []
            for s in range(0, tm, PERM_BLOCK):
                o_nat.append(jnp.dot(inv, o[s:s + PERM_BLOCK, :], preferred_element_type=F32))
                l_nat.append(sum(jnp.dot(inv, t[s:s + PERM_BLOCK, :], preferred_element_type=F32) for t in l3))
            o = jnp.concatenate(o_nat, axis=0)
            l = jnp.concatenate(l_nat, axis=0)
        outs[g] = o.astype(F32)
        lses[g] = l

    def merge_head(h):
        lo, hi = h * HEAD_DIM, (h + 1) * HEAD_DIM
        ls = [l[:, h:h + 1] for l in lses]
        mx = jnp.maximum(jnp.maximum(ls[0], ls[1]), ls[2])
        es = [jnp.exp(l - mx) for l in ls]
        tot = es[0] + es[1] + es[2]
        for g in range(N_SWA_GROUPS):
            alpha = es[g] / tot
            o_ref[:, g * GROUP_WIDTH + lo:g * GROUP_WIDTH + hi] = (outs[g][:, lo:hi] * alpha).astype(o_ref.dtype)

    return ([functools.partial(token_order, g) for g in range(N_SWA_GROUPS)]
            + [functools.partial(merge_head, h) for h in range(HEADS_PER_GROUP)]
            + _mem_attention_parts(q_ref, kb_ref, vb_ref, o_ref, MIXER_WIDTH))


def _swa_merge_out(outs, lses, zb, mem_kv, layer, w, x, g, *, rows_per_batch):
    tm = ROW_TILE
    tiles_per_batch = rows_per_batch // tm

    def group_tile(width, dil):
        sub = _sub_block(dil)
        return lambda tile: pl.BlockSpec(
            (None, tm // sub, dil, sub // dil, width),
            lambda s: (tile(s) // tiles_per_batch, tile(s) % tiles_per_batch, 0, 0, 0))

    specs = ([group_tile(GROUP_WIDTH, dil) for _, dil in SWA_PATTERN]
             + [group_tile(HEAD_DIM, dil) for _, dil in SWA_PATTERN]
             + [lambda tile: pl.BlockSpec((None, tm, MEM_WIDTH), lambda s: (3 * N_SWA_GROUPS, tile(s), 0))])
    return _mix_out(_merge_stage, (*outs, *lses, zb), specs, [], mem_kv, layer, w, x, g,
                    tm=tm, rows_per_batch=rows_per_batch, name="swa_merge_out")


def _sample_mem_attention(q_row, kv_ref, o_ref, col0):
    for h in range(N_MEM_HEADS):
        lo, hi = h * HEAD_DIM, (h + 1) * HEAD_DIM
        q = q_row[:, lo:hi]
        k = kv_ref[:, 0, h, :]
        v = kv_ref[:, 1, h, :]
        s = jnp.sum(k * q, axis=1, keepdims=True) * ATTN_SCALE
        m = jnp.max(s, axis=0, keepdims=True)
        p = jnp.exp(s - m)
        den = jnp.sum(p, axis=0, keepdims=True)
        o_ref[:, col0 + lo:col0 + hi] = jnp.sum(p * v, axis=0, keepdims=True) / den


def _sample_mix_a_kernel(z_ref, gv_ref, w0_ref, b0_ref, kv_ref, o_ref, vrow_ref):
    u = z_ref[:, 0:MIXER_WIDTH]
    v = _rms(z_ref[:, MIXER_WIDTH:2 * MIXER_WIDTH], gv_ref[...])
    vrow_ref[...] = v
    o_ref[:, 0:MIXER_WIDTH] = u * (w0_ref[...] * v + b0_ref[...])
    _sample_mem_attention(z_ref[:, 2 * MIXER_WIDTH:2 * MIXER_WIDTH + MEM_WIDTH], kv_ref, o_ref, MIXER_WIDTH)


def _sample_mix_a(z, g_v, w_s, b_s, mem_kv, layer):
    bd = mem_kv.shape[1]
    w0 = jnp.repeat(w_s[:, 0, 0], GROUP_DIM_A).reshape(1, MIXER_WIDTH)
    b0 = jnp.repeat(b_s[:, 0], GROUP_DIM_A).reshape(1, MIXER_WIDTH)
    width = z.shape[1]
    vec = lambda i: (0, 0)
    return pl.pallas_call(
        _sample_mix_a_kernel,
        grid=(bd,),
        in_specs=[
            pl.BlockSpec((None, 1, width), lambda i: (i, 0, 0)),
            pl.BlockSpec((1, MIXER_WIDTH), vec),
            pl.BlockSpec((1, MIXER_WIDTH), vec),
            pl.BlockSpec((1, MIXER_WIDTH), vec),
            _mem_kv_spec(layer, lambda i: i),
        ],
        out_specs=[
            pl.BlockSpec((None, 1, D_MODEL), lambda i: (i, 0, 0)),
            pl.BlockSpec((None, 1, MIXER_WIDTH), lambda i: (i, 0, 0)),
        ],
        out_shape=[
            jax.ShapeDtypeStruct((bd, 1, D_MODEL), F32),
            jax.ShapeDtypeStruct((bd, 1, MIXER_WIDTH), F32),
        ],
        compiler_params=_params("parallel"),
        name="sample_mix_a",
    )(z[:bd].reshape(bd, 1, width), g_v.reshape(1, MIXER_WIDTH), w0, b0, mem_kv)


def _sample_mix_b_kernel(z_ref, c0_ref, c1_ref, c2_ref, bcol_ref, bnew_ref, kv_ref, o_ref):
    caches = (c0_ref, c1_ref, c2_ref)
    outs = [[None] * HEADS_PER_GROUP for _ in range(N_SWA_GROUPS)]
    lses = [[None] * HEADS_PER_GROUP for _ in range(N_SWA_GROUPS)]
    for g in range(N_SWA_GROUPS):
        for h in range(HEADS_PER_GROUP):
            hd = g * HEADS_PER_GROUP + h
            lo, hi = h * HEAD_DIM, (h + 1) * HEAD_DIM
            q = z_ref[:, hd * HEAD_DIM:(hd + 1) * HEAD_DIM]
            k_new = z_ref[:, MIXER_WIDTH + hd * HEAD_DIM:MIXER_WIDTH + (hd + 1) * HEAD_DIM]
            v_new = z_ref[:, 2 * MIXER_WIDTH + hd * HEAD_DIM:2 * MIXER_WIDTH + (hd + 1) * HEAD_DIM]
            kc = caches[g][:, 0, h, :]
            vc = caches[g][:, 1, h, :]
            s_c = jnp.sum(kc * q, axis=1, keepdims=True) * ATTN_SCALE + bcol_ref[g][:, h:h + 1]
            s_n = jnp.sum(k_new * q, axis=1, keepdims=True) * ATTN_SCALE + bnew_ref[:, hd:hd + 1]
            m = jnp.maximum(jnp.max(s_c, axis=0, keepdims=True), s_n)
            p_c = jnp.exp(s_c - m)
            p_n = jnp.exp(s_n - m)
            den = jnp.sum(p_c, axis=0, keepdims=True) + p_n
            outs[g][h] = (jnp.sum(p_c * vc, axis=0, keepdims=True) + p_n * v_new) / den
            lses[g][h] = m + jnp.log(den)
    for h in range(HEADS_PER_GROUP):
        ls = [lses[g][h] for g in range(N_SWA_GROUPS)]
        mx = jnp.maximum(jnp.maximum(ls[0], ls[1]), ls[2])
        es = [jnp.exp(l - mx) for l in ls]
        tot = es[0] + es[1] + es[2]
        for g in range(N_SWA_GROUPS):
            c0 = g * GROUP_WIDTH + h * HEAD_DIM
            o_ref[:, c0:c0 + HEAD_DIM] = outs[g][h] * (es[g] / tot)
    _sample_mem_attention(z_ref[:, 3 * MIXER_WIDTH:3 * MIXER_WIDTH + MEM_WIDTH], kv_ref, o_ref, MIXER_WIDTH)


def _sample_mix_b(z, win_caches, swa_layer, bias_groups, mem_kv, layer):
    bd = mem_kv.shape[1]
    width = z.shape[1]
    cache_views, cache_specs = [], []
    for g, (win, dil) in enumerate(SWA_PATTERN):
        c = win_caches[g]
        cache_views.append(c.reshape(c.shape[0], bd, win // dil, dil, 2, HEADS_PER_GROUP, HEAD_DIM))
        cache_specs.append(pl.BlockSpec((None, None, N_BACK, None, 2, HEADS_PER_GROUP, HEAD_DIM),
                                        lambda i: (swa_layer, i, 0, 0, 0, 0, 0)))
    bcol = jnp.stack([bg[:, N_BACK:0:-1].T for bg in bias_groups], axis=0)
    bnew = jnp.concatenate([bg[:, 0] for bg in bias_groups])
    bnew = jnp.pad(bnew, (0, HEAD_DIM - bnew.shape[0])).reshape(1, HEAD_DIM)
    return pl.pallas_call(
        _sample_mix_b_kernel,
        grid=(bd,),
        in_specs=[pl.BlockSpec((None, 1, width), lambda i: (i, 0, 0))] + cache_specs + [
            pl.BlockSpec((N_SWA_GROUPS, N_BACK, HEADS_PER_GROUP), lambda i: (0, 0, 0)),
            pl.BlockSpec((1, HEAD_DIM), lambda i: (0, 0)),
            _mem_kv_spec(layer, lambda i: i),
        ],
        out_specs=pl.BlockSpec((None, 1, D_MODEL), lambda i: (i, 0, 0)),
        out_shape=jax.ShapeDtypeStruct((bd, 1, D_MODEL), F32),
        compiler_params=_params("parallel"),
        name="sample_mix_b",
    )(z[:bd].reshape(bd, 1, width), *cache_views, bcol, bnew, mem_kv)


def _t5_bucket(dist):
    nf = jnp.maximum(dist, MAX_EXACT).astype(F32)
    large = MAX_EXACT + (jnp.log(nf / MAX_EXACT) / math.log(MAX_DISTANCE / MAX_EXACT)
                         * (N_BUCKETS - MAX_EXACT)).astype(jnp.int32)
    large = jnp.minimum(large, N_BUCKETS - 1)
    return jnp.where(dist < MAX_EXACT, dist, large)


def _group_bias(rel_bias, g, dil):
    dist = jnp.arange(N_BACK + 1, dtype=jnp.int32) * dil
    b = rel_bias[_t5_bucket(dist)][:, g * HEADS_PER_GROUP:(g + 1) * HEADS_PER_GROUP]
    return b.T.astype(F32)


def _band_tables_kernel(b_ref, o_ref):
    n, rows, width = o_ref.shape
    for x in range(n):
        row = jnp.broadcast_to(b_ref[x:x + 1, :], (rows, width))
        o_ref[x] = pltpu.roll(row, 0, 1, stride=1, stride_axis=0)


def _band_tables(bias_groups):
    width = 2 * N_BACK
    rows = []
    for bias_j in bias_groups:
        masked = jnp.full((HEADS_PER_GROUP, N_BACK - 1), NEG_INF, F32)
        rows.append(jnp.concatenate([bias_j[:, :1], masked, bias_j[:, N_BACK:0:-1]], axis=1))
        rows.append(jnp.concatenate([bias_j[:, ::-1], masked], axis=1))
    base = jnp.stack(rows, axis=0).reshape(-1, width)
    tabs = pl.pallas_call(
        _band_tables_kernel,
        out_shape=jax.ShapeDtypeStruct((base.shape[0], N_BACK, width), F32),
        name="band_tables",
    )(base)
    return tabs.reshape(N_SWA_GROUPS, 2, HEADS_PER_GROUP, N_BACK, width)


def _kv_tail_kernel(k_ref, v_ref, o_ref, *, dil):
    rows = k_ref.shape[0]
    for kv, ref in enumerate((k_ref, v_ref)):
        x = ref[...]
        if dil > 1:
            inv = _residue_major_perm(PERM_BLOCK, dil, transpose=True)
            x = jnp.concatenate([jnp.dot(inv, x[s:s + PERM_BLOCK, :], preferred_element_type=F32)
                                 for s in range(0, rows, PERM_BLOCK)], axis=0)
        x = x.astype(F32)
        for h in range(HEADS_PER_GROUP):
            o_ref[:, kv, h, :] = x[:, h * HEAD_DIM:(h + 1) * HEAD_DIM]


def _kv_tail(zb, g, *, batch, seq):
    win, dil = SWA_PATTERN[g]
    rows = min(win, FFN_ROW_TILE)
    first = (seq - win) // rows
    per_batch = seq // rows

    def tile(t):
        return pl.BlockSpec((None, rows, GROUP_WIDTH), lambda b, s: (t, b * per_batch + first + s, 0))

    return pl.pallas_call(
        functools.partial(_kv_tail_kernel, dil=dil),
        grid=(batch, win // rows),
        in_specs=[tile(N_SWA_GROUPS + g), tile(2 * N_SWA_GROUPS + g)],
        out_specs=pl.BlockSpec((None, rows, 2, HEADS_PER_GROUP, HEAD_DIM), lambda b, s: (b, s, 0, 0, 0)),
        out_shape=jax.ShapeDtypeStruct((batch, win, 2, HEADS_PER_GROUP, HEAD_DIM), F32),
        compiler_params=_params("parallel", "parallel"),
        name=f"kv_tail{g}",
    )(zb, zb)


def kernel(x_prompt, x_sample, mem_prompt, cache_mem_kv, cache_win128_kv, cache_win512_kv, cache_win2048_kv, rel_bias, norm_mix_pre, norm_mix_post, norm_ffn_pre, norm_ffn_post, norm_mem, w_mem_kv, w_in_a, norm_v_a, w_spatial_a, b_spatial_a, w_in_b, w_out, w_ffn_up, w_ffn_down):
    batch, seq, _ = x_prompt.shape
    bd = x_sample.shape[0]
    depth = w_out.shape[0]
    m_p = batch * seq
    win_caches = (cache_win128_kv, cache_win512_kv, cache_win2048_kv)

    bias_groups = [_group_bias(rel_bias, g, dil) for g, (_, dil) in enumerate(SWA_PATTERN)]
    band_tables = _band_tables(bias_groups)

    yp = x_prompt.reshape(m_p, D_MODEL)
    ys = jnp.pad(x_sample.reshape(bd, D_MODEL), ((0, SAMPLE_PAD - bd), (0, 0)))
    mem_rows = mem_prompt.reshape(batch * N_MEM, D_MODEL)

    mem_kv_p = _mem_kv(mem_rows, norm_mem, w_mem_kv, batch=batch)
    chunk_v_s = []
    win_p = [[] for _ in SWA_PATTERN]
    win_s = [[] for _ in SWA_PATTERN]
    for i in range(depth):
        li = i // 2
        if i % 2 == 0:
            zp, zs = _in_proj_a(yp, ys, norm_mix_pre[i], w_in_a, li, tm=ROW_TILE, tn=CAST_TILE,
                                gelu_cols=2 * MIXER_WIDTH)
            mix_s, v_rows = _sample_mix_a(zs, norm_v_a[li], w_spatial_a[li], b_spatial_a[li], cache_mem_kv, i)
            chunk_v_s.append(v_rows)
            ys, w_o = _sample_out_proj(mix_s, w_out, i, ys, norm_mix_post[i])
            yp = _gmlp_mix_out(zp, norm_v_a[li], w_spatial_a[li], b_spatial_a[li], mem_kv_p, i,
                               w_o, yp, norm_mix_post[i], tm=ROW_TILE, rows_per_batch=seq)
        else:
            zb, zs = _in_proj_b(yp, ys, norm_mix_pre[i], w_in_b, li, tm=ROW_TILE)
            outs, lses = [], []
            for g, (win, dil) in enumerate(SWA_PATTERN):
                o, lse = _swa_group(zb, band_tables, g, batch=batch, seq=seq)
                outs.append(o)
                lses.append(lse)
                win_p[g].append(_kv_tail(zb, g, batch=batch, seq=seq))
                kv_new = zs[:bd, MIXER_WIDTH:3 * MIXER_WIDTH]
                kv_new = kv_new.reshape(bd, 1, 2, N_SWA_GROUPS, HEADS_PER_GROUP, HEAD_DIM)[:, :, :, g]
                win_s[g].append(kv_new)
            mix_s = _sample_mix_b(zs, win_caches, li, bias_groups, cache_mem_kv, i)
            ys, w_o = _sample_out_proj(mix_s, w_out, i, ys, norm_mix_post[i])
            yp = _swa_merge_out(outs, lses, zb, mem_kv_p, i, w_o, yp, norm_mix_post[i], rows_per_batch=seq)
        yp, ys, w_ffn = _ffn_head(yp, ys, norm_ffn_pre[i], norm_ffn_post[i], w_ffn_up, w_ffn_down, i,
                                  tm=FFN_ROW_TILE, tf=HEAD_FF_TILE)
        yp = _ffn(yp, norm_ffn_pre[i], norm_ffn_post[i], w_ffn, tm=FFN_ROW_TILE, tf=FF_TILE, first_tile=1)

    return (
        yp.reshape(batch, seq, D_MODEL),
        ys[:bd].reshape(bd, 1, D_MODEL),
        mem_kv_p,
        jnp.stack(chunk_v_s, axis=0),
        jnp.stack(win_p[0], axis=0),
        jnp.stack(win_p[1], axis=0),
        jnp.stack(win_p[2], axis=0),
        jnp.stack(win_s[0], axis=0),
        jnp.stack(win_s[1], axis=0),
        jnp.stack(win_s[2], axis=0),
    )
```

```python
import functools
import math

import jax
import jax.numpy as jnp
from jax import lax
from jax.experimental import pallas as pl
from jax.experimental.pallas import tpu as pltpu

F32 = jnp.float32
BF16 = jnp.bfloat16

D_MODEL = 2048
HEAD_DIM = 128
N_MEM = 256
N_MEM_HEADS = 4
MEM_WIDTH = N_MEM_HEADS * HEAD_DIM
MIXER_WIDTH = D_MODEL - MEM_WIDTH
CHUNK = 128
N_GROUPS_A = 4
GROUP_DIM_A = MIXER_WIDTH // N_GROUPS_A
SWA_PATTERN = ((128, 1), (512, 4), (2048, 16))
N_SWA_GROUPS = len(SWA_PATTERN)
HEADS_PER_GROUP = 4
GROUP_WIDTH = HEADS_PER_GROUP * HEAD_DIM
N_BACK = 128
N_BUCKETS = 32
MAX_EXACT = N_BUCKETS // 2
MAX_DISTANCE = 2048
D_FF = 5632
EPS = 1e-6
NEG_INF = -1e30
ATTN_SCALE = HEAD_DIM ** -0.5
SAMPLE_PAD = 16
PERM_BLOCK = 256
SWA_BLOCKS_PER_TRIP = 8
OUT_CHUNK = 256

ROW_TILE = 512
FFN_ROW_TILE = 1024
FF_TILE = 512
HEAD_FF_TILE = 256
CAST_TILE = 512

V7X_VMEM_BYTES = 64 * 1024 * 1024
VMEM_LIMIT = V7X_VMEM_BYTES - 8 * 1024 * 1024
FFN_VMEM_LIMIT = V7X_VMEM_BYTES - 2 * 1024 * 1024


def _params(*sem, vmem_limit=VMEM_LIMIT):
    return pltpu.CompilerParams(dimension_semantics=sem, vmem_limit_bytes=vmem_limit)


def _gelu(x):
    return 0.5 * x * (1.0 + jnp.tanh(0.7978845608028654 * (x + 0.044715 * (x * x * x))))


def _rms(x, g):
    return x * lax.rsqrt(jnp.mean(x * x, axis=-1, keepdims=True) + EPS) * g


def _log2(n):
    assert n & (n - 1) == 0
    return n.bit_length() - 1


def _residue_major_perm(tm, dil, transpose=False):
    n = tm // dil
    row = lax.broadcasted_iota(jnp.int32, (tm, tm), 0)
    col = lax.broadcasted_iota(jnp.int32, (tm, tm), 1)
    dst, src = (col, row) if transpose else (row, col)
    want = lax.shift_left(jnp.bitwise_and(dst, n - 1), _log2(dil)) + lax.shift_right_logical(dst, _log2(n))
    return (src == want).astype(BF16)


def _in_proj_a_kernel(x_ref, xs_ref, g_ref, w_ref, zp_ref, zs_ref, wb_ref, xn_ref, *, n_col, gelu_cols):
    tm = x_ref.shape[0]
    tn = w_ref.shape[1]
    s = pl.program_id(0)

    def prompt_cols(acc, c0):
        col = lax.broadcasted_iota(jnp.int32, acc.shape, 1) + c0
        return jnp.where(col < gelu_cols, _gelu(acc), acc * ATTN_SCALE).astype(zp_ref.dtype)

    @pl.when(s == 0)
    def _():
        xn_ref[:tm, :] = _rms(x_ref[...], g_ref[...]).astype(BF16)
        xn_ref[tm:, :] = _rms(xs_ref[...], g_ref[...]).astype(BF16)

    for j in range(n_col):
        @pl.when(s == j)
        def _(j=j):
            cols = slice(j * tn, (j + 1) * tn)
            w = w_ref[...].astype(BF16)
            wb_ref[:, cols] = w
            acc = jnp.dot(xn_ref[...], w, preferred_element_type=F32)
            zp_ref[:, cols] = prompt_cols(acc[:tm, :], j * tn)
            sample = acc[tm:, :]
            zs_ref[:, cols] = _gelu(sample) if (j + 1) * tn <= gelu_cols else sample

    @pl.when(s >= n_col)
    def _():
        xn = _rms(x_ref[...], g_ref[...]).astype(BF16)
        acc = jnp.dot(xn, wb_ref[...], preferred_element_type=F32)
        zp_ref[...] = prompt_cols(acc, 0)


def _in_proj_a(x, xs, g, w, layer, *, tm, tn, gelu_cols):
    m, k = x.shape
    ms = xs.shape[0]
    n = w.shape[2]
    n_col, n_row = n // tn, m // tm
    assert gelu_cols % tn == 0
    row = lambda s: jnp.maximum(s - (n_col - 1), 0)
    return pl.pallas_call(
        functools.partial(_in_proj_a_kernel, n_col=n_col, gelu_cols=gelu_cols),
        grid=(n_col + n_row - 1,),
        in_specs=[
            pl.BlockSpec((tm, k), lambda s: (row(s), 0)),
            pl.BlockSpec((ms, k), lambda s: (0, 0)),
            pl.BlockSpec((1, k), lambda s: (0, 0)),
            pl.BlockSpec((None, k, tn), lambda s: (layer, 0, jnp.minimum(s, n_col - 1))),
        ],
        out_specs=[
            pl.BlockSpec((tm, n), lambda s: (row(s), 0)),
            pl.BlockSpec((ms, n), lambda s: (0, 0)),
        ],
        out_shape=[
            jax.ShapeDtypeStruct((m, n), BF16),
            jax.ShapeDtypeStruct((ms, n), F32),
        ],
        scratch_shapes=[pltpu.VMEM((k, n), BF16), pltpu.VMEM((tm + ms, k), BF16)],
        compiler_params=_params("arbitrary"),
        name="in_proj_a",
    )(x, xs, g.reshape(1, k), w)


def _mem_kv_kernel(x_ref, g_ref, w_ref, o_ref):
    xn = _rms(x_ref[...], g_ref[...]).astype(BF16)
    acc = jnp.dot(xn, w_ref[...].astype(BF16), preferred_element_type=F32)
    for kv in range(2):
        for h in range(N_MEM_HEADS):
            c0 = (kv * N_MEM_HEADS + h) * HEAD_DIM
            o_ref[:, kv, h, :] = acc[:, c0:c0 + HEAD_DIM]


def _mem_kv(mem_rows, g, w, *, batch):
    m, k = mem_rows.shape
    layers = w.shape[0]
    out = pl.pallas_call(
        _mem_kv_kernel,
        grid=(layers,),
        in_specs=[
            pl.BlockSpec((m, k), lambda l: (0, 0)),
            pl.BlockSpec((None, 1, k), lambda l: (l, 0, 0)),
            pl.BlockSpec((None, k, 2 * MEM_WIDTH), lambda l: (l, 0, 0)),
        ],
        out_specs=pl.BlockSpec((None, m, 2, N_MEM_HEADS, HEAD_DIM), lambda l: (l, 0, 0, 0, 0)),
        out_shape=jax.ShapeDtypeStruct((layers, m, 2, N_MEM_HEADS, HEAD_DIM), F32),
        compiler_params=_params("parallel"),
        name="mem_kv",
    )(mem_rows, g.reshape(layers, 1, k), w)
    return out.reshape(layers, batch, m // batch, 2, N_MEM_HEADS, HEAD_DIM)


def _in_proj_b_kernel(x_ref, xs_ref, g_ref, w_ref, o_ref, zs_ref, wb_ref, xn_ref):
    tm = x_ref.shape[0]
    n_tiles = o_ref.shape[0]
    n_qkv = 3 * N_SWA_GROUPS
    s = pl.program_id(0)

    def row_orders():
        xn = _rms(x_ref[...], g_ref[...]).astype(BF16)
        xn_ref[0, :tm, :] = xn
        for g in range(1, N_SWA_GROUPS):
            perm = _residue_major_perm(PERM_BLOCK, SWA_PATTERN[g][1])
            for r in range(0, tm, PERM_BLOCK):
                xn_ref[g, r:r + PERM_BLOCK, :] = jnp.dot(
                    perm, xn[r:r + PERM_BLOCK, :], preferred_element_type=F32).astype(BF16)

    def prompt_tile(t, acc):
        if t < N_SWA_GROUPS or t == n_qkv:
            acc = acc * ATTN_SCALE
        o_ref[t] = acc.astype(o_ref.dtype)

    src = lambda t: t % N_SWA_GROUPS if t < n_qkv else 0

    @pl.when(s == 0)
    def _():
        row_orders()
        xs = _rms(xs_ref[...], g_ref[...]).astype(BF16)
        for g in range(N_SWA_GROUPS):
            xn_ref[g, tm:, :] = xs

    for t in range(n_tiles):
        @pl.when(s == t)
        def _(t=t):
            cols = slice(t * GROUP_WIDTH, (t + 1) * GROUP_WIDTH)
            w = w_ref[...].astype(BF16)
            wb_ref[:, cols] = w
            acc = jnp.dot(xn_ref[src(t)], w, preferred_element_type=F32)
            prompt_tile(t, acc[:tm, :])
            zs_ref[:, cols] = acc[tm:, :]

    @pl.when(s >= n_tiles)
    def _():
        row_orders()
        for t in range(n_tiles):
            w = wb_ref[:, t * GROUP_WIDTH:(t + 1) * GROUP_WIDTH]
            prompt_tile(t, jnp.dot(xn_ref[src(t), :tm, :], w, preferred_element_type=F32))


def _in_proj_b(x, xs, g, w, layer, *, tm):
    m, k = x.shape
    ms = xs.shape[0]
    n = w.shape[2]
    n_tiles = n // GROUP_WIDTH
    row = lambda s: jnp.maximum(s - (n_tiles - 1), 0)
    return pl.pallas_call(
        _in_proj_b_kernel,
        grid=(n_tiles + m // tm - 1,),
        in_specs=[
            pl.BlockSpec((tm, k), lambda s: (row(s), 0)),
            pl.BlockSpec((ms, k), lambda s: (0, 0)),
            pl.BlockSpec((1, k), lambda s: (0, 0)),
            pl.BlockSpec((None, k, GROUP_WIDTH), lambda s: (layer, 0, jnp.minimum(s, n_tiles - 1))),
        ],
        out_specs=[
            pl.BlockSpec((n_tiles, tm, GROUP_WIDTH), lambda s: (0, row(s), 0)),
            pl.BlockSpec((ms, n), lambda s: (0, 0)),
        ],
        out_shape=[
            jax.ShapeDtypeStruct((n_tiles, m, GROUP_WIDTH), BF16),
            jax.ShapeDtypeStruct((ms, n), F32),
        ],
        scratch_shapes=[pltpu.VMEM((k, n), BF16), pltpu.VMEM((N_SWA_GROUPS, tm + ms, k), BF16)],
        compiler_params=_params("arbitrary"),
        name="in_proj_b",
    )(x, xs, g.reshape(1, k), w)


MEM_KV_SCRATCH = [pltpu.VMEM((N_MEM_HEADS, N_MEM, HEAD_DIM), BF16),
                  pltpu.VMEM((N_MEM_HEADS, N_MEM, 2 * HEAD_DIM), BF16)]


def _prepare_mem_kv(kv_ref, kb_ref, vb_ref):
    for h in range(N_MEM_HEADS):
        kb_ref[h] = kv_ref[:, 0, h, :].astype(BF16)
        vb_ref[h, :, :HEAD_DIM] = kv_ref[:, 1, h, :].astype(BF16)
        vb_ref[h, :, HEAD_DIM:] = jnp.ones((N_MEM, HEAD_DIM), BF16)


def _mem_attention_parts(q_ref, kb_ref, vb_ref, o_ref, col0):
    def head(h):
        lo, hi = h * HEAD_DIM, (h + 1) * HEAD_DIM
        s = lax.dot_general(q_ref[:, lo:hi], kb_ref[h], (((1,), (1,)), ((), ())), preferred_element_type=F32)
        m = jnp.max(s, axis=1, keepdims=True)
        p = jnp.exp(s - m).astype(BF16)
        ov = jnp.dot(p, vb_ref[h], preferred_element_type=F32)
        o_ref[:, col0 + lo:col0 + hi] = (ov[:, :HEAD_DIM] / ov[:, HEAD_DIM:]).astype(o_ref.dtype)

    return [functools.partial(head, h) for h in range(N_MEM_HEADS)]


def _gmlp_stage(u_ref, v_ref, q_ref, gv_ref, ws_ref, bs_ref, kb_ref, vb_ref, vn_ref, o_ref):
    tm = u_ref.shape[0]

    def norm_v():
        vn_ref[...] = _rms(v_ref[...].astype(F32), gv_ref[...]).astype(BF16)

    def group(g):
        row = lax.broadcasted_iota(jnp.int32, (CHUNK, CHUNK), 0)
        col = lax.broadcasted_iota(jnp.int32, (CHUNK, CHUNK), 1)
        w = jnp.where(row >= col, ws_ref[g], 0.0).astype(BF16)
        b = bs_ref[:, g:g + 1]
        c0, c1 = g * GROUP_DIM_A, (g + 1) * GROUP_DIM_A
        for c in range(tm // CHUNK):
            r0, r1 = c * CHUNK, (c + 1) * CHUNK
            s = jnp.dot(w, vn_ref[r0:r1, c0:c1], preferred_element_type=F32) + b
            o_ref[r0:r1, c0:c1] = (u_ref[r0:r1, c0:c1].astype(F32) * s).astype(o_ref.dtype)

    return ([norm_v] + [functools.partial(group, g) for g in range(N_GROUPS_A)]
            + _mem_attention_parts(q_ref, kb_ref, vb_ref, o_ref, MIXER_WIDTH))


def _mem_kv_spec(layer, batch_of):
    return pl.BlockSpec((None, None, N_MEM, 2, N_MEM_HEADS, HEAD_DIM),
                        lambda i: (layer, batch_of(i), 0, 0, 0, 0))


def _mix_out_kernel(*refs, stage, n_in, n_tiles, tiles_per_batch):
    mix_in = refs[:n_in]
    kv_ref, w_ref, x_ref, g_ref, o_ref, buf0_ref, buf1_ref, acc_ref, kb_ref, vb_ref = refs[n_in:n_in + 10]
    extra = refs[n_in + 10:]
    s = pl.program_id(0)

    @pl.when(s == 0)
    def _():
        buf1_ref[...] = jnp.zeros_like(buf1_ref)

    @pl.when(jnp.minimum(s, n_tiles - 1) % tiles_per_batch == 0)
    def _():
        _prepare_mem_kv(kv_ref, kb_ref, vb_ref)

    def step(dst_ref, src_ref):
        parts = stage(*mix_in, kb_ref, vb_ref, *extra, dst_ref)
        n_chunks = D_MODEL // OUT_CHUNK
        for c in range(n_chunks):
            cols = slice(c * OUT_CHUNK, (c + 1) * OUT_CHUNK)
            acc_ref[:, cols] = jnp.dot(src_ref[...], w_ref[:, cols], preferred_element_type=F32)
            for part in parts[c * len(parts) // n_chunks:(c + 1) * len(parts) // n_chunks]:
                part()
        o_ref[...] = x_ref[...] + _rms(acc_ref[...], g_ref[...])

    @pl.when(s % 2 == 0)
    def _():
        step(buf0_ref, buf1_ref)

    @pl.when(s % 2 == 1)
    def _():
        step(buf1_ref, buf0_ref)


def _mix_out(stage, mix_inputs, mix_specs, extra_scratch, mem_kv, layer, w, x, g, *, tm, rows_per_batch, name):
    m = x.shape[0]
    n_tiles = m // tm
    tiles_per_batch = rows_per_batch // tm
    mix_tile = lambda s: jnp.minimum(s, n_tiles - 1)
    out_tile = lambda s: jnp.maximum(s - 1, 0)
    row_spec = pl.BlockSpec((tm, D_MODEL), lambda s: (out_tile(s), 0))
    return pl.pallas_call(
        functools.partial(_mix_out_kernel, stage=stage, n_in=len(mix_inputs), n_tiles=n_tiles,
                          tiles_per_batch=tiles_per_batch),
        grid=(n_tiles + 1,),
        in_specs=[spec(mix_tile) for spec in mix_specs] + [
            _mem_kv_spec(layer, lambda s: mix_tile(s) // tiles_per_batch),
            pl.BlockSpec((D_MODEL, D_MODEL), lambda s: (0, 0), pipeline_mode=pl.Buffered(1)),
            row_spec,
            pl.BlockSpec((1, D_MODEL), lambda s: (0, 0)),
        ],
        out_specs=row_spec,
        out_shape=jax.ShapeDtypeStruct((m, D_MODEL), F32),
        scratch_shapes=[pltpu.VMEM((tm, D_MODEL), BF16), pltpu.VMEM((tm, D_MODEL), BF16),
                        pltpu.VMEM((tm, D_MODEL), F32)] + MEM_KV_SCRATCH + list(extra_scratch),
        compiler_params=_params("arbitrary"),
        name=name,
    )(*mix_inputs, mem_kv, w, x, g.reshape(1, D_MODEL))


def _gmlp_mix_out(zact, g_v, w_s, b_s, mem_kv, layer, w, x, g, *, tm, rows_per_batch):
    const = lambda shape: (lambda tile: pl.BlockSpec(shape, lambda s: (0,) * len(shape)))
    specs = [
        lambda tile: pl.BlockSpec((tm, MIXER_WIDTH), lambda s: (tile(s), 0)),
        lambda tile: pl.BlockSpec((tm, MIXER_WIDTH), lambda s: (tile(s), 1)),
        lambda tile: pl.BlockSpec((tm, MEM_WIDTH), lambda s: (tile(s), 2 * MIXER_WIDTH // MEM_WIDTH)),
        const((1, MIXER_WIDTH)),
        const((N_GROUPS_A, CHUNK, CHUNK)),
        const((CHUNK, N_GROUPS_A)),
    ]
    return _mix_out(_gmlp_stage, (zact, zact, zact, g_v.reshape(1, MIXER_WIDTH), w_s, b_s.T), specs,
                    [pltpu.VMEM((tm, MIXER_WIDTH), BF16)], mem_kv, layer, w, x, g,
                    tm=tm, rows_per_batch=rows_per_batch, name="gmlp_mix_out")


def _out_proj_cast_kernel(mix_ref, w_ref, x_ref, g_ref, o_ref, wb_ref, acc_ref):
    j = pl.program_id(0)
    w = w_ref[...].astype(BF16)
    wb_ref[...] = w
    acc_ref[j] = jnp.dot(mix_ref[...], w, preferred_element_type=F32)

    @pl.when(j == pl.num_programs(0) - 1)
    def _():
        o = jnp.concatenate([acc_ref[t] for t in range(acc_ref.shape[0])], axis=1)
        o_ref[...] = x_ref[...] + _rms(o, g_ref[...])


def _sample_out_proj(mix, w, layer, x, g):
    bd = mix.shape[0]
    mix = jnp.pad(mix.reshape(bd, D_MODEL), ((0, x.shape[0] - bd), (0, 0))).astype(BF16)
    return _out_proj_cast(mix, w, layer, x, g, tn=CAST_TILE)


def _out_proj_cast(mix, w, layer, x, g, *, tn):
    m = x.shape[0]
    n_tiles = D_MODEL // tn
    return pl.pallas_call(
        _out_proj_cast_kernel,
        grid=(n_tiles,),
        in_specs=[
            pl.BlockSpec((m, D_MODEL), lambda j: (0, 0)),
            pl.BlockSpec((None, D_MODEL, tn), lambda j: (layer, 0, j)),
            pl.BlockSpec((m, D_MODEL), lambda j: (0, 0)),
            pl.BlockSpec((1, D_MODEL), lambda j: (0, 0)),
        ],
        out_specs=[
            pl.BlockSpec((m, D_MODEL), lambda j: (0, 0)),
            pl.BlockSpec((D_MODEL, tn), lambda j: (0, j)),
        ],
        out_shape=[
            jax.ShapeDtypeStruct((m, D_MODEL), F32),
            jax.ShapeDtypeStruct((D_MODEL, D_MODEL), BF16),
        ],
        scratch_shapes=[pltpu.VMEM((n_tiles, m, tn), F32)],
        compiler_params=_params("arbitrary"),
        name="out_proj_cast",
    )(mix, w, x, g.reshape(1, D_MODEL))


def _accumulate(o_ref, ssq_ref, part):
    new = o_ref[...] + part
    o_ref[...] = new
    sq = new * new
    ssq_ref[...] = functools.reduce(
        jnp.add, [sq[:, c:c + HEAD_DIM] for c in range(0, sq.shape[1], HEAD_DIM)])


def _residual_norm(x_ref, o_ref, ssq_ref, g_ref):
    ms = jnp.sum(ssq_ref[...], axis=-1, keepdims=True) * (1.0 / o_ref.shape[1])
    o_ref[...] = x_ref[...] + o_ref[...] * lax.rsqrt(ms + EPS) * g_ref[...]


def _ffn_kernel(x_ref, gpre_ref, gpost_ref, wg_ref, wl_ref, wd_ref, o_ref, xn_ref, ssq_ref):
    f = pl.program_id(1)

    @pl.when(f == 0)
    def _():
        xn_ref[...] = _rms(x_ref[...], gpre_ref[...]).astype(BF16)
        o_ref[...] = jnp.zeros_like(o_ref)

    xn = xn_ref[...]
    hg = jnp.dot(xn, wg_ref[...], preferred_element_type=F32)
    hl = jnp.dot(xn, wl_ref[...], preferred_element_type=F32)
    a = (hg * jax.nn.sigmoid(hg) * hl).astype(BF16)
    _accumulate(o_ref, ssq_ref, jnp.dot(a, wd_ref[...], preferred_element_type=F32))

    @pl.when(f == pl.num_programs(1) - 1)
    def _():
        _residual_norm(x_ref, o_ref, ssq_ref, gpost_ref)


def _ffn(x, g_pre, g_post, weights, *, tm, tf, first_tile):
    m = x.shape[0]
    nf = D_FF // tf
    rows = pl.BlockSpec((tm, D_MODEL), lambda i, f: (i + first_tile, 0))
    return pl.pallas_call(
        _ffn_kernel,
        grid=(m // tm - first_tile, nf),
        in_specs=[
            rows,
            pl.BlockSpec((1, D_MODEL), lambda i, f: (0, 0)),
            pl.BlockSpec((1, D_MODEL), lambda i, f: (0, 0)),
            pl.BlockSpec((D_MODEL, tf), lambda i, f: (0, f)),
            pl.BlockSpec((D_MODEL, tf), lambda i, f: (0, f)),
            pl.BlockSpec((tf, D_MODEL), lambda i, f: (f, 0)),
        ],
        out_specs=rows,
        out_shape=jax.ShapeDtypeStruct((m, D_MODEL), F32),
        input_output_aliases={0: 0},
        scratch_shapes=[pltpu.VMEM((tm, D_MODEL), BF16), pltpu.VMEM((tm, HEAD_DIM), F32)],
        compiler_params=_params("parallel", "arbitrary", vmem_limit=FFN_VMEM_LIMIT),
        name="ffn",
    )(x, g_pre.reshape(1, D_MODEL), g_post.reshape(1, D_MODEL), *weights)


def _ffn_head_kernel(x_ref, xs_ref, gpre_ref, gpost_ref, wg_ref, wl_ref, wd_ref,
                     o_ref, os_ref, wgb_ref, wlb_ref, wdb_ref, xn_ref, ssq_ref, ssqs_ref):
    tm = x_ref.shape[0]
    f = pl.program_id(0)

    @pl.when(f == 0)
    def _():
        xn_ref[:tm, :] = _rms(x_ref[...], gpre_ref[...]).astype(BF16)
        xn_ref[tm:, :] = _rms(xs_ref[...], gpre_ref[...]).astype(BF16)
        o_ref[...] = jnp.zeros_like(o_ref)
        os_ref[...] = jnp.zeros_like(os_ref)

    wg, wl, wd = (r[...].astype(BF16) for r in (wg_ref, wl_ref, wd_ref))
    wgb_ref[...] = wg
    wlb_ref[...] = wl
    wdb_ref[...] = wd
    xn = xn_ref[...]
    hg = jnp.dot(xn, wg, preferred_element_type=F32)
    hl = jnp.dot(xn, wl, preferred_element_type=F32)
    a = (hg * jax.nn.sigmoid(hg) * hl).astype(BF16)
    part = jnp.dot(a, wd, preferred_element_type=F32)
    _accumulate(o_ref, ssq_ref, part[:tm, :])
    _accumulate(os_ref, ssqs_ref, part[tm:, :])

    @pl.when(f == pl.num_programs(0) - 1)
    def _():
        _residual_norm(x_ref, o_ref, ssq_ref, gpost_ref)
        _residual_norm(xs_ref, os_ref, ssqs_ref, gpost_ref)


def _ffn_head(x, xs, g_pre, g_post, w_up, w_down, layer, *, tm, tf):
    m = x.shape[0]
    ms = xs.shape[0]
    nf = D_FF // tf
    once = dict(pipeline_mode=pl.Buffered(1))
    head = pl.BlockSpec((tm, D_MODEL), lambda f: (0, 0), **once)
    sample = pl.BlockSpec((ms, D_MODEL), lambda f: (0, 0))
    vec = pl.BlockSpec((1, D_MODEL), lambda f: (0, 0))
    outs = pl.pallas_call(
        _ffn_head_kernel,
        grid=(nf,),
        in_specs=[
            head, sample, vec, vec,
            pl.BlockSpec((None, D_MODEL, tf), lambda f: (layer, 0, f)),
            pl.BlockSpec((None, D_MODEL, tf), lambda f: (layer, 0, nf + f)),
            pl.BlockSpec((None, tf, D_MODEL), lambda f: (layer, f, 0)),
        ],
        out_specs=[
            head, sample,
            pl.BlockSpec((D_MODEL, tf), lambda f: (0, f)),
            pl.BlockSpec((D_MODEL, tf), lambda f: (0, f)),
            pl.BlockSpec((tf, D_MODEL), lambda f: (f, 0)),
        ],
        out_shape=[
            jax.ShapeDtypeStruct((m, D_MODEL), F32),
            jax.ShapeDtypeStruct((ms, D_MODEL), F32),
            jax.ShapeDtypeStruct((D_MODEL, D_FF), BF16),
            jax.ShapeDtypeStruct((D_MODEL, D_FF), BF16),
            jax.ShapeDtypeStruct((D_FF, D_MODEL), BF16),
        ],
        input_output_aliases={0: 0},
        scratch_shapes=[pltpu.VMEM((tm + ms, D_MODEL), BF16), pltpu.VMEM((tm, HEAD_DIM), F32),
                        pltpu.VMEM((ms, HEAD_DIM), F32)],
        compiler_params=_params("arbitrary"),
        name="ffn_head",
    )(x, xs, g_pre.reshape(1, D_MODEL), g_post.reshape(1, D_MODEL), w_up, w_up, w_down)
    return outs[0], outs[1], tuple(outs[2:])


def _swa_kernel(q_ref, k_ref, v_ref, tb_ref, o_ref, lse_ref):
    n_units, n_res, u, _ = q_ref.shape
    per_blk = N_BACK // u
    n_blk = n_units // per_blk
    lane = lax.broadcasted_iota(jnp.int32, (N_BACK, HEAD_DIM), 1)
    ones = jnp.ones((2 * N_BACK, HEAD_DIM), BF16)

    def rows(ref, res, unit0, n_rows, lo, hi):
        return ref[pl.ds(unit0, n_rows // u), res, :, lo:hi].reshape(n_rows, hi - lo)

    def block(res, qu, ku, table):
        n_keys = N_BACK if table == 0 else 2 * N_BACK
        lse_tile = jnp.zeros((N_BACK, HEAD_DIM), F32)
        for h in range(HEADS_PER_GROUP):
            lo, hi = h * HEAD_DIM, (h + 1) * HEAD_DIM
            q = rows(q_ref, res, qu, N_BACK, lo, hi)
            kw = rows(k_ref, res, ku, n_keys, lo, hi)
            vw = rows(v_ref, res, ku, n_keys, lo, hi)
            s = lax.dot_general(q, kw, (((1,), (1,)), ((), ())), preferred_element_type=F32)
            s = s + tb_ref[table, h][:, :n_keys]
            m = jnp.max(s, axis=1, keepdims=True)
            p = jnp.exp(s - m).astype(BF16)
            ov = jnp.dot(p, jnp.concatenate([vw, ones[:n_keys]], axis=1), preferred_element_type=F32)
            den = ov[:, HEAD_DIM:]
            o = ov[:, :HEAD_DIM] / den
            o_ref[pl.ds(qu, per_blk), res, :, lo:hi] = o.reshape(per_blk, u, HEAD_DIM).astype(o_ref.dtype)
            lse_tile = jnp.where(lane == h, m + jnp.log(den), lse_tile)
        lse_ref[pl.ds(qu, per_blk), res, :, :] = lse_tile.reshape(per_blk, u, HEAD_DIM)

    def block_at(res, n):
        block(res, n * per_blk, (n - 1) * per_blk, 1)

    n_trips = (n_blk - 1) // SWA_BLOCKS_PER_TRIP
    for res in range(n_res):
        block(res, 0, 0, 0)

        def body(i, carry, res=res):
            for b in range(SWA_BLOCKS_PER_TRIP):
                block_at(res, 1 + SWA_BLOCKS_PER_TRIP * i + b)
            return carry

        if n_trips > 0:
            lax.fori_loop(0, n_trips, body, 0)
        for n in range(1 + n_trips * SWA_BLOCKS_PER_TRIP, n_blk):
            block_at(res, n)


def _sub_block(dil):
    return N_BACK if dil == 1 else PERM_BLOCK


def _swa_group(zb, tables, g, *, batch, seq):
    dil = SWA_PATTERN[g][1]
    sub = _sub_block(dil)
    n_units, u = seq // sub, sub // dil
    n_blk = n_units * u // N_BACK
    n_res = min(dil, max(1, SWA_BLOCKS_PER_TRIP // n_blk))
    view = zb.reshape(zb.shape[0], batch, n_units, dil, u, GROUP_WIDTH)

    def rows_in(tile):
        return pl.BlockSpec((None, None, n_units, n_res, u, GROUP_WIDTH), lambda i, r: (tile, i, 0, r, 0, 0))

    def rows_out(width):
        return pl.BlockSpec((None, n_units, n_res, u, width), lambda i, r: (i, 0, r, 0, 0))

    return pl.pallas_call(
        _swa_kernel,
        grid=(batch, dil // n_res),
        in_specs=[rows_in(g), rows_in(N_SWA_GROUPS + g), rows_in(2 * N_SWA_GROUPS + g),
                  pl.BlockSpec((None, 2, HEADS_PER_GROUP, N_BACK, 2 * N_BACK), lambda i, r: (g, 0, 0, 0, 0))],
        out_specs=[rows_out(GROUP_WIDTH), rows_out(HEAD_DIM)],
        out_shape=[
            jax.ShapeDtypeStruct((batch, n_units, dil, u, GROUP_WIDTH), BF16),
            jax.ShapeDtypeStruct((batch, n_units, dil, u, HEAD_DIM), F32),
        ],
        compiler_params=_params("parallel", "parallel"),
        name=f"swa_group{g}",
    )(view, view, view, tables)


def _split3(x):
    hi = x.astype(BF16)
    rest = x - hi.astype(F32)
    mid = rest.astype(BF16)
    lo = (rest - mid.astype(F32)).astype(BF16)
    return hi, mid, lo


def _merge_stage(o0_ref, o1_ref, o2_ref, l0_ref, l1_ref, l2_ref, q_ref, kb_ref, vb_ref, o_ref):
    tm = o_ref.shape[0]
    group_refs = ((o0_ref, l0_ref), (o1_ref, l1_ref), (o2_ref, l2_ref))
    outs, lses = [None] * N_SWA_GROUPS, [None] * N_SWA_GROUPS

    def token_order(g):
        o_g, l_g = group_refs[g]
        dil = SWA_PATTERN[g][1]
        o = o_g[...].reshape(tm, GROUP_WIDTH)
        l = l_g[...].reshape(tm, HEAD_DIM)
        if dil > 1:
            inv = _residue_major_perm(PERM_BLOCK, dil, transpose=True)
            l3 = _split3(l)
            o_nat, l_nat = [], []
            for s in range(0, tm, PERM_BLOCK):
                o_nat.append(jnp.dot(inv, o[s:s + PERM_BLOCK, :], preferred_element_type=F32))
                l_nat.append(sum(jnp.dot(inv, t[s:s + PERM_BLOCK, :], preferred_element_type=F32) for t in l3))
            o = jnp.concatenate(o_nat, axis=0)
            l = jnp.concatenate(l_nat, axis=0)
        outs[g] = o.astype(F32)
        lses[g] = l

    def merge_head(h):
        lo, hi = h * HEAD_DIM, (h + 1) * HEAD_DIM
        ls = [l[:, h:h + 1] for l in lses]
        mx = jnp.maximum(jnp.maximum(ls[0], ls[1]), ls[2])
        es = [jnp.exp(l - mx) for l in ls]
        tot = es[0] + es[1] + es[2]
        for g in range(N_SWA_GROUPS):
            alpha = es[g] / tot
            o_ref[:, g * GROUP_WIDTH + lo:g * GROUP_WIDTH + hi] = (outs[g][:, lo:hi] * alpha).astype(o_ref.dtype)

    return ([functools.partial(token_order, g) for g in range(N_SWA_GROUPS)]
            + [functools.partial(merge_head, h) for h in range(HEADS_PER_GROUP)]
            + _mem_attention_parts(q_ref, kb_ref, vb_ref, o_ref, MIXER_WIDTH))


def _swa_merge_out(outs, lses, zb, mem_kv, layer, w, x, g, *, rows_per_batch):
    tm = ROW_TILE
    tiles_per_batch = rows_per_batch // tm

    def group_tile(width, dil):
        sub = _sub_block(dil)
        return lambda tile: pl.BlockSpec(
            (None, tm // sub, dil, sub // dil, width),
            lambda s: (tile(s) // tiles_per_batch, tile(s) % tiles_per_batch, 0, 0, 0))

    specs = ([group_tile(GROUP_WIDTH, dil) for _, dil in SWA_PATTERN]
             + [group_tile(HEAD_DIM, dil) for _, dil in SWA_PATTERN]
             + [lambda tile: pl.BlockSpec((None, tm, MEM_WIDTH), lambda s: (3 * N_SWA_GROUPS, tile(s), 0))])
    return _mix_out(_merge_stage, (*outs, *lses, zb), specs, [], mem_kv, layer, w, x, g,
                    tm=tm, rows_per_batch=rows_per_batch, name="swa_merge_out")


def _head_rows(row, col0):
    return jnp.concatenate([row[:, col0 + h * HEAD_DIM:col0 + (h + 1) * HEAD_DIM]
                            for h in range(HEADS_PER_GROUP)], axis=0)


def _sample_attention(q4, kv_ref, bias=None, new=None):
    k3, v3 = kv_ref[:, 0], kv_ref[:, 1]
    s = jnp.sum(k3 * q4[None], axis=-1, keepdims=True) * ATTN_SCALE
    if bias is not None:
        s = s + bias
    m = jnp.max(s, axis=0)
    if new is not None:
        k_new, v_new, b_new = new
        s_new = jnp.sum(k_new * q4, axis=-1, keepdims=True) * ATTN_SCALE + b_new
        m = jnp.maximum(m, s_new)
    p = jnp.exp(s - m[None])
    den = jnp.sum(p, axis=0)
    o = jnp.sum(p * v3, axis=0)
    if new is not None:
        p_new = jnp.exp(s_new - m)
        den = den + p_new
        o = o + p_new * v_new
    return o / den, m + jnp.log(den)


def _store_head_rows(o_ref, col0, x4):
    for h in range(HEADS_PER_GROUP):
        o_ref[:, col0 + h * HEAD_DIM:col0 + (h + 1) * HEAD_DIM] = x4[h:h + 1, :]


def _sample_mem_attention(q_row, kv_ref, o_ref, col0):
    o, _ = _sample_attention(_head_rows(q_row, 0), kv_ref)
    _store_head_rows(o_ref, col0, o)


def _sample_mix_a_kernel(z_ref, gv_ref, w0_ref, b0_ref, kv_ref, o_ref, vrow_ref):
    u = z_ref[:, 0:MIXER_WIDTH]
    v = _rms(z_ref[:, MIXER_WIDTH:2 * MIXER_WIDTH], gv_ref[...])
    vrow_ref[...] = v
    o_ref[:, 0:MIXER_WIDTH] = u * (w0_ref[...] * v + b0_ref[...])
    _sample_mem_attention(z_ref[:, 2 * MIXER_WIDTH:2 * MIXER_WIDTH + MEM_WIDTH], kv_ref, o_ref, MIXER_WIDTH)


def _sample_mix_a(z, g_v, w_s, b_s, mem_kv, layer):
    bd = mem_kv.shape[1]
    w0 = jnp.repeat(w_s[:, 0, 0], GROUP_DIM_A).reshape(1, MIXER_WIDTH)
    b0 = jnp.repeat(b_s[:, 0], GROUP_DIM_A).reshape(1, MIXER_WIDTH)
    width = z.shape[1]
    vec = lambda i: (0, 0)
    return pl.pallas_call(
        _sample_mix_a_kernel,
        grid=(bd,),
        in_specs=[
            pl.BlockSpec((None, 1, width), lambda i: (i, 0, 0)),
            pl.BlockSpec((1, MIXER_WIDTH), vec),
            pl.BlockSpec((1, MIXER_WIDTH), vec),
            pl.BlockSpec((1, MIXER_WIDTH), vec),
            _mem_kv_spec(layer, lambda i: i),
        ],
        out_specs=[
            pl.BlockSpec((None, 1, D_MODEL), lambda i: (i, 0, 0)),
            pl.BlockSpec((None, 1, MIXER_WIDTH), lambda i: (i, 0, 0)),
        ],
        out_shape=[
            jax.ShapeDtypeStruct((bd, 1, D_MODEL), F32),
            jax.ShapeDtypeStruct((bd, 1, MIXER_WIDTH), F32),
        ],
        compiler_params=_params("parallel"),
        name="sample_mix_a",
    )(z[:bd].reshape(bd, 1, width), g_v.reshape(1, MIXER_WIDTH), w0, b0, mem_kv)


def _sample_mix_b_kernel(z_ref, c0_ref, c1_ref, c2_ref, bcol_ref, bnew_ref, kv_ref, o_ref):
    caches = (c0_ref, c1_ref, c2_ref)
    z = z_ref[...]
    outs, lses = [], []
    for g in range(N_SWA_GROUPS):
        c0 = g * GROUP_WIDTH
        new = (_head_rows(z, MIXER_WIDTH + c0), _head_rows(z, 2 * MIXER_WIDTH + c0), bnew_ref[g])
        o, lse = _sample_attention(_head_rows(z, c0), caches[g], bias=bcol_ref[g], new=new)
        outs.append(o)
        lses.append(lse)
    mx = jnp.maximum(jnp.maximum(lses[0], lses[1]), lses[2])
    es = [jnp.exp(l - mx) for l in lses]
    tot = es[0] + es[1] + es[2]
    for g in range(N_SWA_GROUPS):
        _store_head_rows(o_ref, g * GROUP_WIDTH, outs[g] * (es[g] / tot))
    _sample_mem_attention(z[:, 3 * MIXER_WIDTH:3 * MIXER_WIDTH + MEM_WIDTH], kv_ref, o_ref, MIXER_WIDTH)


def _sample_mix_b(z, win_caches, swa_layer, bias_groups, mem_kv, layer):
    bd = mem_kv.shape[1]
    width = z.shape[1]
    cache_views, cache_specs = [], []
    for g, (win, dil) in enumerate(SWA_PATTERN):
        c = win_caches[g]
        cache_views.append(c.reshape(c.shape[0], bd, win // dil, dil, 2, HEADS_PER_GROUP, HEAD_DIM))
        cache_specs.append(pl.BlockSpec((None, None, N_BACK, None, 2, HEADS_PER_GROUP, HEAD_DIM),
                                        lambda i: (swa_layer, i, 0, 0, 0, 0, 0)))
    bcol = jnp.stack([bg[:, N_BACK:0:-1].T for bg in bias_groups], axis=0)
    bcol = jnp.broadcast_to(bcol[..., None], bcol.shape + (HEAD_DIM,))
    bnew = jnp.stack([bg[:, 0] for bg in bias_groups], axis=0)
    bnew = jnp.broadcast_to(bnew[..., None], bnew.shape + (HEAD_DIM,))
    return pl.pallas_call(
        _sample_mix_b_kernel,
        grid=(bd,),
        in_specs=[pl.BlockSpec((None, 1, width), lambda i: (i, 0, 0))] + cache_specs + [
            pl.BlockSpec((N_SWA_GROUPS, N_BACK, HEADS_PER_GROUP, HEAD_DIM), lambda i: (0, 0, 0, 0)),
            pl.BlockSpec((N_SWA_GROUPS, HEADS_PER_GROUP, HEAD_DIM), lambda i: (0, 0, 0)),
            _mem_kv_spec(layer, lambda i: i),
        ],
        out_specs=pl.BlockSpec((None, 1, D_MODEL), lambda i: (i, 0, 0)),
        out_shape=jax.ShapeDtypeStruct((bd, 1, D_MODEL), F32),
        compiler_params=_params("parallel"),
        name="sample_mix_b",
    )(z[:bd].reshape(bd, 1, width), *cache_views, bcol, bnew, mem_kv)


def _t5_bucket(dist):
    nf = jnp.maximum(dist, MAX_EXACT).astype(F32)
    large = MAX_EXACT + (jnp.log(nf / MAX_EXACT) / math.log(MAX_DISTANCE / MAX_EXACT)
                         * (N_BUCKETS - MAX_EXACT)).astype(jnp.int32)
    large = jnp.minimum(large, N_BUCKETS - 1)
    return jnp.where(dist < MAX_EXACT, dist, large)


def _group_bias(rel_bias, g, dil):
    dist = jnp.arange(N_BACK + 1, dtype=jnp.int32) * dil
    b = rel_bias[_t5_bucket(dist)][:, g * HEADS_PER_GROUP:(g + 1) * HEADS_PER_GROUP]
    return b.T.astype(F32)


def _band_tables_kernel(b_ref, o_ref):
    n, rows, width = o_ref.shape
    for x in range(n):
        row = jnp.broadcast_to(b_ref[x:x + 1, :], (rows, width))
        o_ref[x] = pltpu.roll(row, 0, 1, stride=1, stride_axis=0)


def _band_tables(bias_groups):
    width = 2 * N_BACK
    rows = []
    for bias_j in bias_groups:
        masked = jnp.full((HEADS_PER_GROUP, N_BACK - 1), NEG_INF, F32)
        rows.append(jnp.concatenate([bias_j[:, :1], masked, bias_j[:, N_BACK:0:-1]], axis=1))
        rows.append(jnp.concatenate([bias_j[:, ::-1], masked], axis=1))
    base = jnp.stack(rows, axis=0).reshape(-1, width)
    tabs = pl.pallas_call(
        _band_tables_kernel,
        out_shape=jax.ShapeDtypeStruct((base.shape[0], N_BACK, width), F32),
        name="band_tables",
    )(base)
    return tabs.reshape(N_SWA_GROUPS, 2, HEADS_PER_GROUP, N_BACK, width)


def _kv_tail_kernel(k_ref, v_ref, o_ref, *, dil):
    rows = k_ref.shape[0]
    for kv, ref in enumerate((k_ref, v_ref)):
        x = ref[...]
        if dil > 1:
            inv = _residue_major_perm(PERM_BLOCK, dil, transpose=True)
            x = jnp.concatenate([jnp.dot(inv, x[s:s + PERM_BLOCK, :], preferred_element_type=F32)
                                 for s in range(0, rows, PERM_BLOCK)], axis=0)
        x = x.astype(F32)
        for h in range(HEADS_PER_GROUP):
            o_ref[:, kv, h, :] = x[:, h * HEAD_DIM:(h + 1) * HEAD_DIM]


def _kv_tail(zb, g, *, batch, seq):
    win, dil = SWA_PATTERN[g]
    rows = min(win, FFN_ROW_TILE)
    first = (seq - win) // rows
    per_batch = seq // rows

    def tile(t):
        return pl.BlockSpec((None, rows, GROUP_WIDTH), lambda b, s: (t, b * per_batch + first + s, 0))

    return pl.pallas_call(
        functools.partial(_kv_tail_kernel, dil=dil),
        grid=(batch, win // rows),
        in_specs=[tile(N_SWA_GROUPS + g), tile(2 * N_SWA_GROUPS + g)],
        out_specs=pl.BlockSpec((None, rows, 2, HEADS_PER_GROUP, HEAD_DIM), lambda b, s: (b, s, 0, 0, 0)),
        out_shape=jax.ShapeDtypeStruct((batch, win, 2, HEADS_PER_GROUP, HEAD_DIM), F32),
        compiler_params=_params("parallel", "parallel"),
        name=f"kv_tail{g}",
    )(zb, zb)


def kernel(x_prompt, x_sample, mem_prompt, cache_mem_kv, cache_win128_kv, cache_win512_kv, cache_win2048_kv, rel_bias, norm_mix_pre, norm_mix_post, norm_ffn_pre, norm_ffn_post, norm_mem, w_mem_kv, w_in_a, norm_v_a, w_spatial_a, b_spatial_a, w_in_b, w_out, w_ffn_up, w_ffn_down):
    batch, seq, _ = x_prompt.shape
    bd = x_sample.shape[0]
    depth = w_out.shape[0]
    m_p = batch * seq
    win_caches = (cache_win128_kv, cache_win512_kv, cache_win2048_kv)

    bias_groups = [_group_bias(rel_bias, g, dil) for g, (_, dil) in enumerate(SWA_PATTERN)]
    band_tables = _band_tables(bias_groups)

    yp = x_prompt.reshape(m_p, D_MODEL)
    ys = jnp.pad(x_sample.reshape(bd, D_MODEL), ((0, SAMPLE_PAD - bd), (0, 0)))
    mem_rows = mem_prompt.reshape(batch * N_MEM, D_MODEL)

    mem_kv_p = _mem_kv(mem_rows, norm_mem, w_mem_kv, batch=batch)
    chunk_v_s = []
    win_p = [[] for _ in SWA_PATTERN]
    win_s = [[] for _ in SWA_PATTERN]
    for i in range(depth):
        li = i // 2
        if i % 2 == 0:
            zp, zs = _in_proj_a(yp, ys, norm_mix_pre[i], w_in_a, li, tm=ROW_TILE, tn=CAST_TILE,
                                gelu_cols=2 * MIXER_WIDTH)
            mix_s, v_rows = _sample_mix_a(zs, norm_v_a[li], w_spatial_a[li], b_spatial_a[li], cache_mem_kv, i)
            chunk_v_s.append(v_rows)
            ys, w_o = _sample_out_proj(mix_s, w_out, i, ys, norm_mix_post[i])
            yp = _gmlp_mix_out(zp, norm_v_a[li], w_spatial_a[li], b_spatial_a[li], mem_kv_p, i,
                               w_o, yp, norm_mix_post[i], tm=ROW_TILE, rows_per_batch=seq)
        else:
            zb, zs = _in_proj_b(yp, ys, norm_mix_pre[i], w_in_b, li, tm=ROW_TILE)
            outs, lses = [], []
            for g, (win, dil) in enumerate(SWA_PATTERN):
                o, lse = _swa_group(zb, band_tables, g, batch=batch, seq=seq)
                outs.append(o)
                lses.append(lse)
                win_p[g].append(_kv_tail(zb, g, batch=batch, seq=seq))
                kv_new = zs[:bd, MIXER_WIDTH:3 * MIXER_WIDTH]
                kv_new = kv_new.reshape(bd, 1, 2, N_SWA_GROUPS, HEADS_PER_GROUP, HEAD_DIM)[:, :, :, g]
                win_s[g].append(kv_new)
            mix_s = _sample_mix_b(zs, win_caches, li, bias_groups, cache_mem_kv, i)
            ys, w_o = _sample_out_proj(mix_s, w_out, i, ys, norm_mix_post[i])
            yp = _swa_merge_out(outs, lses, zb, mem_kv_p, i, w_o, yp, norm_mix_post[i], rows_per_batch=seq)
        yp, ys, w_ffn = _ffn_head(yp, ys, norm_ffn_pre[i], norm_ffn_post[i], w_ffn_up, w_ffn_down, i,
                                  tm=FFN_ROW_TILE, tf=HEAD_FF_TILE)
        yp = _ffn(yp, norm_ffn_pre[i], norm_ffn_post[i], w_ffn, tm=FFN_ROW_TILE, tf=FF_TILE, first_tile=1)

    return (
        yp.reshape(batch, seq, D_MODEL),
        ys[:bd].reshape(bd, 1, D_MODEL),
        mem_kv_p,
        jnp.stack(chunk_v_s, axis=0),
        jnp.stack(win_p[0], axis=0),
        jnp.stack(win_p[1], axis=0),
        jnp.stack(win_p[2], axis=0),
        jnp.stack(win_s[0], axis=0),
        jnp.stack(win_s[1], axis=0),
        jnp.stack(win_s[2], axis=0),
    )
```

```python
import functools
import math

import jax
import jax.numpy as jnp
from jax import lax
from jax.experimental import pallas as pl
from jax.experimental.pallas import tpu as pltpu

F32 = jnp.float32
BF16 = jnp.bfloat16

D_MODEL = 2048
HEAD_DIM = 128
N_MEM = 256
N_MEM_HEADS = 4
MEM_WIDTH = N_MEM_HEADS * HEAD_DIM
MIXER_WIDTH = D_MODEL - MEM_WIDTH
CHUNK = 128
N_GROUPS_A = 4
GROUP_DIM_A = MIXER_WIDTH // N_GROUPS_A
SWA_PATTERN = ((128, 1), (512, 4), (2048, 16))
N_SWA_GROUPS = len(SWA_PATTERN)
HEADS_PER_GROUP = 4
GROUP_WIDTH = HEADS_PER_GROUP * HEAD_DIM
N_BACK = 128
N_BUCKETS = 32
MAX_EXACT = N_BUCKETS // 2
MAX_DISTANCE = 2048
D_FF = 5632
EPS = 1e-6
NEG_INF = -1e30
ATTN_SCALE = HEAD_DIM ** -0.5
SAMPLE_PAD = 16
PERM_BLOCK = 256
SWA_BLOCKS_PER_TRIP = 16
OUT_CHUNK = 256

ROW_TILE = 512
FFN_ROW_TILE = 1024
FF_TILE = 512
HEAD_FF_TILE = 256
CAST_TILE = 512

V7X_VMEM_BYTES = 64 * 1024 * 1024
VMEM_LIMIT = V7X_VMEM_BYTES - 8 * 1024 * 1024
FFN_VMEM_LIMIT = V7X_VMEM_BYTES - 2 * 1024 * 1024


def _params(*sem, vmem_limit=VMEM_LIMIT):
    return pltpu.CompilerParams(dimension_semantics=sem, vmem_limit_bytes=vmem_limit)


def _gelu(x):
    return 0.5 * x * (1.0 + jnp.tanh(0.7978845608028654 * (x + 0.044715 * (x * x * x))))


def _rms(x, g):
    return x * lax.rsqrt(jnp.mean(x * x, axis=-1, keepdims=True) + EPS) * g


def _log2(n):
    assert n & (n - 1) == 0
    return n.bit_length() - 1


def _residue_major_perm(tm, dil, transpose=False):
    n = tm // dil
    row = lax.broadcasted_iota(jnp.int32, (tm, tm), 0)
    col = lax.broadcasted_iota(jnp.int32, (tm, tm), 1)
    dst, src = (col, row) if transpose else (row, col)
    want = lax.shift_left(jnp.bitwise_and(dst, n - 1), _log2(dil)) + lax.shift_right_logical(dst, _log2(n))
    return (src == want).astype(BF16)


def _in_proj_a_kernel(x_ref, xs_ref, g_ref, w_ref, zp_ref, zs_ref, wb_ref, xn_ref, *, n_col, gelu_cols):
    tm = x_ref.shape[0]
    tn = w_ref.shape[1]
    s = pl.program_id(0)

    def prompt_cols(acc, c0):
        col = lax.broadcasted_iota(jnp.int32, acc.shape, 1) + c0
        return jnp.where(col < gelu_cols, _gelu(acc), acc * ATTN_SCALE).astype(zp_ref.dtype)

    @pl.when(s == 0)
    def _():
        xn_ref[:tm, :] = _rms(x_ref[...], g_ref[...]).astype(BF16)
        xn_ref[tm:, :] = _rms(xs_ref[...], g_ref[...]).astype(BF16)

    for j in range(n_col):
        @pl.when(s == j)
        def _(j=j):
            cols = slice(j * tn, (j + 1) * tn)
            w = w_ref[...].astype(BF16)
            wb_ref[:, cols] = w
            acc = jnp.dot(xn_ref[...], w, preferred_element_type=F32)
            zp_ref[:, cols] = prompt_cols(acc[:tm, :], j * tn)
            sample = acc[tm:, :]
            zs_ref[:, cols] = _gelu(sample) if (j + 1) * tn <= gelu_cols else sample

    @pl.when(s >= n_col)
    def _():
        xn = _rms(x_ref[...], g_ref[...]).astype(BF16)
        acc = jnp.dot(xn, wb_ref[...], preferred_element_type=F32)
        zp_ref[...] = prompt_cols(acc, 0)


def _in_proj_a(x, xs, g, w, layer, *, tm, tn, gelu_cols):
    m, k = x.shape
    ms = xs.shape[0]
    n = w.shape[2]
    n_col, n_row = n // tn, m // tm
    assert gelu_cols % tn == 0
    row = lambda s: jnp.maximum(s - (n_col - 1), 0)
    return pl.pallas_call(
        functools.partial(_in_proj_a_kernel, n_col=n_col, gelu_cols=gelu_cols),
        grid=(n_col + n_row - 1,),
        in_specs=[
            pl.BlockSpec((tm, k), lambda s: (row(s), 0)),
            pl.BlockSpec((ms, k), lambda s: (0, 0)),
            pl.BlockSpec((1, k), lambda s: (0, 0)),
            pl.BlockSpec((None, k, tn), lambda s: (layer, 0, jnp.minimum(s, n_col - 1))),
        ],
        out_specs=[
            pl.BlockSpec((tm, n), lambda s: (row(s), 0)),
            pl.BlockSpec((ms, n), lambda s: (0, 0)),
        ],
        out_shape=[
            jax.ShapeDtypeStruct((m, n), BF16),
            jax.ShapeDtypeStruct((ms, n), F32),
        ],
        scratch_shapes=[pltpu.VMEM((k, n), BF16), pltpu.VMEM((tm + ms, k), BF16)],
        compiler_params=_params("arbitrary"),
        name="in_proj_a",
    )(x, xs, g.reshape(1, k), w)


def _mem_kv_kernel(x_ref, g_ref, w_ref, o_ref):
    xn = _rms(x_ref[...], g_ref[...]).astype(BF16)
    acc = jnp.dot(xn, w_ref[...].astype(BF16), preferred_element_type=F32)
    for kv in range(2):
        for h in range(N_MEM_HEADS):
            c0 = (kv * N_MEM_HEADS + h) * HEAD_DIM
            o_ref[:, kv, h, :] = acc[:, c0:c0 + HEAD_DIM]


def _mem_kv(mem_rows, g, w, *, batch):
    m, k = mem_rows.shape
    layers = w.shape[0]
    out = pl.pallas_call(
        _mem_kv_kernel,
        grid=(layers,),
        in_specs=[
            pl.BlockSpec((m, k), lambda l: (0, 0)),
            pl.BlockSpec((None, 1, k), lambda l: (l, 0, 0)),
            pl.BlockSpec((None, k, 2 * MEM_WIDTH), lambda l: (l, 0, 0)),
        ],
        out_specs=pl.BlockSpec((None, m, 2, N_MEM_HEADS, HEAD_DIM), lambda l: (l, 0, 0, 0, 0)),
        out_shape=jax.ShapeDtypeStruct((layers, m, 2, N_MEM_HEADS, HEAD_DIM), F32),
        compiler_params=_params("parallel"),
        name="mem_kv",
    )(mem_rows, g.reshape(layers, 1, k), w)
    return out.reshape(layers, batch, m // batch, 2, N_MEM_HEADS, HEAD_DIM)


def _in_proj_b_kernel(x_ref, xs_ref, g_ref, w_ref, o_ref, zs_ref, wb_ref, xn_ref):
    tm = x_ref.shape[0]
    n_tiles = o_ref.shape[0]
    n_qkv = 3 * N_SWA_GROUPS
    s = pl.program_id(0)

    def row_orders():
        xn = _rms(x_ref[...], g_ref[...]).astype(BF16)
        xn_ref[0, :tm, :] = xn
        for g in range(1, N_SWA_GROUPS):
            perm = _residue_major_perm(PERM_BLOCK, SWA_PATTERN[g][1])
            for r in range(0, tm, PERM_BLOCK):
                xn_ref[g, r:r + PERM_BLOCK, :] = jnp.dot(
                    perm, xn[r:r + PERM_BLOCK, :], preferred_element_type=F32).astype(BF16)

    def prompt_tile(t, acc):
        if t < N_SWA_GROUPS or t == n_qkv:
            acc = acc * ATTN_SCALE
        o_ref[t] = acc.astype(o_ref.dtype)

    src = lambda t: t % N_SWA_GROUPS if t < n_qkv else 0

    @pl.when(s == 0)
    def _():
        row_orders()
        xs = _rms(xs_ref[...], g_ref[...]).astype(BF16)
        for g in range(N_SWA_GROUPS):
            xn_ref[g, tm:, :] = xs

    for t in range(n_tiles):
        @pl.when(s == t)
        def _(t=t):
            cols = slice(t * GROUP_WIDTH, (t + 1) * GROUP_WIDTH)
            w = w_ref[...].astype(BF16)
            wb_ref[:, cols] = w
            acc = jnp.dot(xn_ref[src(t)], w, preferred_element_type=F32)
            prompt_tile(t, acc[:tm, :])
            zs_ref[:, cols] = acc[tm:, :]

    @pl.when(s >= n_tiles)
    def _():
        row_orders()
        for t in range(n_tiles):
            w = wb_ref[:, t * GROUP_WIDTH:(t + 1) * GROUP_WIDTH]
            prompt_tile(t, jnp.dot(xn_ref[src(t), :tm, :], w, preferred_element_type=F32))


def _in_proj_b(x, xs, g, w, layer, *, tm):
    m, k = x.shape
    ms = xs.shape[0]
    n = w.shape[2]
    n_tiles = n // GROUP_WIDTH
    row = lambda s: jnp.maximum(s - (n_tiles - 1), 0)
    return pl.pallas_call(
        _in_proj_b_kernel,
        grid=(n_tiles + m // tm - 1,),
        in_specs=[
            pl.BlockSpec((tm, k), lambda s: (row(s), 0)),
            pl.BlockSpec((ms, k), lambda s: (0, 0)),
            pl.BlockSpec((1, k), lambda s: (0, 0)),
            pl.BlockSpec((None, k, GROUP_WIDTH), lambda s: (layer, 0, jnp.minimum(s, n_tiles - 1))),
        ],
        out_specs=[
            pl.BlockSpec((n_tiles, tm, GROUP_WIDTH), lambda s: (0, row(s), 0)),
            pl.BlockSpec((ms, n), lambda s: (0, 0)),
        ],
        out_shape=[
            jax.ShapeDtypeStruct((n_tiles, m, GROUP_WIDTH), BF16),
            jax.ShapeDtypeStruct((ms, n), F32),
        ],
        scratch_shapes=[pltpu.VMEM((k, n), BF16), pltpu.VMEM((N_SWA_GROUPS, tm + ms, k), BF16)],
        compiler_params=_params("arbitrary"),
        name="in_proj_b",
    )(x, xs, g.reshape(1, k), w)


MEM_KV_SCRATCH = [pltpu.VMEM((N_MEM_HEADS, N_MEM, HEAD_DIM), BF16),
                  pltpu.VMEM((N_MEM_HEADS, N_MEM, 2 * HEAD_DIM), BF16)]


def _prepare_mem_kv(kv_ref, kb_ref, vb_ref):
    for h in range(N_MEM_HEADS):
        kb_ref[h] = kv_ref[:, 0, h, :].astype(BF16)
        vb_ref[h, :, :HEAD_DIM] = kv_ref[:, 1, h, :].astype(BF16)
        vb_ref[h, :, HEAD_DIM:] = jnp.ones((N_MEM, HEAD_DIM), BF16)


def _mem_attention_parts(q_ref, kb_ref, vb_ref, o_ref, col0):
    def head(h):
        lo, hi = h * HEAD_DIM, (h + 1) * HEAD_DIM
        s = lax.dot_general(q_ref[:, lo:hi], kb_ref[h], (((1,), (1,)), ((), ())), preferred_element_type=F32)
        m = jnp.max(s, axis=1, keepdims=True)
        p = jnp.exp(s - m).astype(BF16)
        ov = jnp.dot(p, vb_ref[h], preferred_element_type=F32)
        o_ref[:, col0 + lo:col0 + hi] = (ov[:, :HEAD_DIM] / ov[:, HEAD_DIM:]).astype(o_ref.dtype)

    return [functools.partial(head, h) for h in range(N_MEM_HEADS)]


def _gmlp_stage(u_ref, v_ref, q_ref, gv_ref, ws_ref, bs_ref, kb_ref, vb_ref, vn_ref, o_ref):
    tm = u_ref.shape[0]

    def norm_v():
        vn_ref[...] = _rms(v_ref[...].astype(F32), gv_ref[...]).astype(BF16)

    def group(g):
        row = lax.broadcasted_iota(jnp.int32, (CHUNK, CHUNK), 0)
        col = lax.broadcasted_iota(jnp.int32, (CHUNK, CHUNK), 1)
        w = jnp.where(row >= col, ws_ref[g], 0.0).astype(BF16)
        b = bs_ref[:, g:g + 1]
        c0, c1 = g * GROUP_DIM_A, (g + 1) * GROUP_DIM_A
        for c in range(tm // CHUNK):
            r0, r1 = c * CHUNK, (c + 1) * CHUNK
            s = jnp.dot(w, vn_ref[r0:r1, c0:c1], preferred_element_type=F32) + b
            o_ref[r0:r1, c0:c1] = (u_ref[r0:r1, c0:c1].astype(F32) * s).astype(o_ref.dtype)

    return ([norm_v] + [functools.partial(group, g) for g in range(N_GROUPS_A)]
            + _mem_attention_parts(q_ref, kb_ref, vb_ref, o_ref, MIXER_WIDTH))


def _mem_kv_spec(layer, batch_of):
    return pl.BlockSpec((None, None, N_MEM, 2, N_MEM_HEADS, HEAD_DIM),
                        lambda i: (layer, batch_of(i), 0, 0, 0, 0))


def _mix_out_kernel(*refs, stage, n_in, n_tiles, tiles_per_batch):
    mix_in = refs[:n_in]
    kv_ref, w_ref, x_ref, g_ref, o_ref, buf0_ref, buf1_ref, acc_ref, kb_ref, vb_ref = refs[n_in:n_in + 10]
    extra = refs[n_in + 10:]
    s = pl.program_id(0)

    @pl.when(s == 0)
    def _():
        buf1_ref[...] = jnp.zeros_like(buf1_ref)

    @pl.when(jnp.minimum(s, n_tiles - 1) % tiles_per_batch == 0)
    def _():
        _prepare_mem_kv(kv_ref, kb_ref, vb_ref)

    def step(dst_ref, src_ref):
        parts = stage(*mix_in, kb_ref, vb_ref, *extra, dst_ref)
        n_chunks = D_MODEL // OUT_CHUNK
        for c in range(n_chunks):
            cols = slice(c * OUT_CHUNK, (c + 1) * OUT_CHUNK)
            acc_ref[:, cols] = jnp.dot(src_ref[...], w_ref[:, cols], preferred_element_type=F32)
            for part in parts[c * len(parts) // n_chunks:(c + 1) * len(parts) // n_chunks]:
                part()
        o_ref[...] = x_ref[...] + _rms(acc_ref[...], g_ref[...])

    @pl.when(s % 2 == 0)
    def _():
        step(buf0_ref, buf1_ref)

    @pl.when(s % 2 == 1)
    def _():
        step(buf1_ref, buf0_ref)


def _mix_out(stage, mix_inputs, mix_specs, extra_scratch, mem_kv, layer, w, x, g, *, tm, rows_per_batch, name):
    m = x.shape[0]
    n_tiles = m // tm
    tiles_per_batch = rows_per_batch // tm
    mix_tile = lambda s: jnp.minimum(s, n_tiles - 1)
    out_tile = lambda s: jnp.maximum(s - 1, 0)
    row_spec = pl.BlockSpec((tm, D_MODEL), lambda s: (out_tile(s), 0))
    return pl.pallas_call(
        functools.partial(_mix_out_kernel, stage=stage, n_in=len(mix_inputs), n_tiles=n_tiles,
                          tiles_per_batch=tiles_per_batch),
        grid=(n_tiles + 1,),
        in_specs=[spec(mix_tile) for spec in mix_specs] + [
            _mem_kv_spec(layer, lambda s: mix_tile(s) // tiles_per_batch),
            pl.BlockSpec((D_MODEL, D_MODEL), lambda s: (0, 0), pipeline_mode=pl.Buffered(1)),
            row_spec,
            pl.BlockSpec((1, D_MODEL), lambda s: (0, 0)),
        ],
        out_specs=row_spec,
        out_shape=jax.ShapeDtypeStruct((m, D_MODEL), F32),
        scratch_shapes=[pltpu.VMEM((tm, D_MODEL), BF16), pltpu.VMEM((tm, D_MODEL), BF16),
                        pltpu.VMEM((tm, D_MODEL), F32)] + MEM_KV_SCRATCH + list(extra_scratch),
        compiler_params=_params("arbitrary"),
        name=name,
    )(*mix_inputs, mem_kv, w, x, g.reshape(1, D_MODEL))


def _gmlp_mix_out(zact, g_v, w_s, b_s, mem_kv, layer, w, x, g, *, tm, rows_per_batch):
    const = lambda shape: (lambda tile: pl.BlockSpec(shape, lambda s: (0,) * len(shape)))
    specs = [
        lambda tile: pl.BlockSpec((tm, MIXER_WIDTH), lambda s: (tile(s), 0)),
        lambda tile: pl.BlockSpec((tm, MIXER_WIDTH), lambda s: (tile(s), 1)),
        lambda tile: pl.BlockSpec((tm, MEM_WIDTH), lambda s: (tile(s), 2 * MIXER_WIDTH // MEM_WIDTH)),
        const((1, MIXER_WIDTH)),
        const((N_GROUPS_A, CHUNK, CHUNK)),
        const((CHUNK, N_GROUPS_A)),
    ]
    return _mix_out(_gmlp_stage, (zact, zact, zact, g_v.reshape(1, MIXER_WIDTH), w_s, b_s.T), specs,
                    [pltpu.VMEM((tm, MIXER_WIDTH), BF16)], mem_kv, layer, w, x, g,
                    tm=tm, rows_per_batch=rows_per_batch, name="gmlp_mix_out")


def _out_proj_cast_kernel(mix_ref, w_ref, x_ref, g_ref, o_ref, wb_ref, acc_ref):
    j = pl.program_id(0)
    w = w_ref[...].astype(BF16)
    wb_ref[...] = w
    acc_ref[j] = jnp.dot(mix_ref[...], w, preferred_element_type=F32)

    @pl.when(j == pl.num_programs(0) - 1)
    def _():
        o = jnp.concatenate([acc_ref[t] for t in range(acc_ref.shape[0])], axis=1)
        o_ref[...] = x_ref[...] + _rms(o, g_ref[...])


def _sample_out_proj(mix, w, layer, x, g):
    bd = mix.shape[0]
    mix = jnp.pad(mix.reshape(bd, D_MODEL), ((0, x.shape[0] - bd), (0, 0))).astype(BF16)
    return _out_proj_cast(mix, w, layer, x, g, tn=CAST_TILE)


def _out_proj_cast(mix, w, layer, x, g, *, tn):
    m = x.shape[0]
    n_tiles = D_MODEL // tn
    return pl.pallas_call(
        _out_proj_cast_kernel,
        grid=(n_tiles,),
        in_specs=[
            pl.BlockSpec((m, D_MODEL), lambda j: (0, 0)),
            pl.BlockSpec((None, D_MODEL, tn), lambda j: (layer, 0, j)),
            pl.BlockSpec((m, D_MODEL), lambda j: (0, 0)),
            pl.BlockSpec((1, D_MODEL), lambda j: (0, 0)),
        ],
        out_specs=[
            pl.BlockSpec((m, D_MODEL), lambda j: (0, 0)),
            pl.BlockSpec((D_MODEL, tn), lambda j: (0, j)),
        ],
        out_shape=[
            jax.ShapeDtypeStruct((m, D_MODEL), F32),
            jax.ShapeDtypeStruct((D_MODEL, D_MODEL), BF16),
        ],
        scratch_shapes=[pltpu.VMEM((n_tiles, m, tn), F32)],
        compiler_params=_params("arbitrary"),
        name="out_proj_cast",
    )(mix, w, x, g.reshape(1, D_MODEL))


def _accumulate(o_ref, ssq_ref, part):
    new = o_ref[...] + part
    o_ref[...] = new
    sq = new * new
    ssq_ref[...] = functools.reduce(
        jnp.add, [sq[:, c:c + HEAD_DIM] for c in range(0, sq.shape[1], HEAD_DIM)])


def _residual_norm(x_ref, o_ref, ssq_ref, g_ref):
    ms = jnp.sum(ssq_ref[...], axis=-1, keepdims=True) * (1.0 / o_ref.shape[1])
    o_ref[...] = x_ref[...] + o_ref[...] * lax.rsqrt(ms + EPS) * g_ref[...]


def _ffn_kernel(x_ref, gpre_ref, gpost_ref, wg_ref, wl_ref, wd_ref, o_ref, xn_ref, ssq_ref):
    f = pl.program_id(1)

    @pl.when(f == 0)
    def _():
        xn_ref[...] = _rms(x_ref[...], gpre_ref[...]).astype(BF16)
        o_ref[...] = jnp.zeros_like(o_ref)

    xn = xn_ref[...]
    hg = jnp.dot(xn, wg_ref[...], preferred_element_type=F32)
    hl = jnp.dot(xn, wl_ref[...], preferred_element_type=F32)
    a = (hg * jax.nn.sigmoid(hg) * hl).astype(BF16)
    _accumulate(o_ref, ssq_ref, jnp.dot(a, wd_ref[...], preferred_element_type=F32))

    @pl.when(f == pl.num_programs(1) - 1)
    def _():
        _residual_norm(x_ref, o_ref, ssq_ref, gpost_ref)


def _ffn(x, g_pre, g_post, weights, *, tm, tf, first_tile):
    m = x.shape[0]
    nf = D_FF // tf
    rows = pl.BlockSpec((tm, D_MODEL), lambda i, f: (i + first_tile, 0))
    return pl.pallas_call(
        _ffn_kernel,
        grid=(m // tm - first_tile, nf),
        in_specs=[
            rows,
            pl.BlockSpec((1, D_MODEL), lambda i, f: (0, 0)),
            pl.BlockSpec((1, D_MODEL), lambda i, f: (0, 0)),
            pl.BlockSpec((D_MODEL, tf), lambda i, f: (0, f)),
            pl.BlockSpec((D_MODEL, tf), lambda i, f: (0, f)),
            pl.BlockSpec((tf, D_MODEL), lambda i, f: (f, 0)),
        ],
        out_specs=rows,
        out_shape=jax.ShapeDtypeStruct((m, D_MODEL), F32),
        input_output_aliases={0: 0},
        scratch_shapes=[pltpu.VMEM((tm, D_MODEL), BF16), pltpu.VMEM((tm, HEAD_DIM), F32)],
        compiler_params=_params("parallel", "arbitrary", vmem_limit=FFN_VMEM_LIMIT),
        name="ffn",
    )(x, g_pre.reshape(1, D_MODEL), g_post.reshape(1, D_MODEL), *weights)


def _ffn_head_kernel(x_ref, xs_ref, gpre_ref, gpost_ref, wg_ref, wl_ref, wd_ref,
                     o_ref, os_ref, wgb_ref, wlb_ref, wdb_ref, xn_ref, ssq_ref, ssqs_ref):
    tm = x_ref.shape[0]
    f = pl.program_id(0)

    @pl.when(f == 0)
    def _():
        xn_ref[:tm, :] = _rms(x_ref[...], gpre_ref[...]).astype(BF16)
        xn_ref[tm:, :] = _rms(xs_ref[...], gpre_ref[...]).astype(BF16)
        o_ref[...] = jnp.zeros_like(o_ref)
        os_ref[...] = jnp.zeros_like(os_ref)

    wg, wl, wd = (r[...].astype(BF16) for r in (wg_ref, wl_ref, wd_ref))
    wgb_ref[...] = wg
    wlb_ref[...] = wl
    wdb_ref[...] = wd
    xn = xn_ref[...]
    hg = jnp.dot(xn, wg, preferred_element_type=F32)
    hl = jnp.dot(xn, wl, preferred_element_type=F32)
    a = (hg * jax.nn.sigmoid(hg) * hl).astype(BF16)
    part = jnp.dot(a, wd, preferred_element_type=F32)
    _accumulate(o_ref, ssq_ref, part[:tm, :])
    _accumulate(os_ref, ssqs_ref, part[tm:, :])

    @pl.when(f == pl.num_programs(0) - 1)
    def _():
        _residual_norm(x_ref, o_ref, ssq_ref, gpost_ref)
        _residual_norm(xs_ref, os_ref, ssqs_ref, gpost_ref)


def _ffn_head(x, xs, g_pre, g_post, w_up, w_down, layer, *, tm, tf):
    m = x.shape[0]
    ms = xs.shape[0]
    nf = D_FF // tf
    once = dict(pipeline_mode=pl.Buffered(1))
    head = pl.BlockSpec((tm, D_MODEL), lambda f: (0, 0), **once)
    sample = pl.BlockSpec((ms, D_MODEL), lambda f: (0, 0))
    vec = pl.BlockSpec((1, D_MODEL), lambda f: (0, 0))
    outs = pl.pallas_call(
        _ffn_head_kernel,
        grid=(nf,),
        in_specs=[
            head, sample, vec, vec,
            pl.BlockSpec((None, D_MODEL, tf), lambda f: (layer, 0, f)),
            pl.BlockSpec((None, D_MODEL, tf), lambda f: (layer, 0, nf + f)),
            pl.BlockSpec((None, tf, D_MODEL), lambda f: (layer, f, 0)),
        ],
        out_specs=[
            head, sample,
            pl.BlockSpec((D_MODEL, tf), lambda f: (0, f)),
            pl.BlockSpec((D_MODEL, tf), lambda f: (0, f)),
            pl.BlockSpec((tf, D_MODEL), lambda f: (f, 0)),
        ],
        out_shape=[
            jax.ShapeDtypeStruct((m, D_MODEL), F32),
            jax.ShapeDtypeStruct((ms, D_MODEL), F32),
            jax.ShapeDtypeStruct((D_MODEL, D_FF), BF16),
            jax.ShapeDtypeStruct((D_MODEL, D_FF), BF16),
            jax.ShapeDtypeStruct((D_FF, D_MODEL), BF16),
        ],
        input_output_aliases={0: 0},
        scratch_shapes=[pltpu.VMEM((tm + ms, D_MODEL), BF16), pltpu.VMEM((tm, HEAD_DIM), F32),
                        pltpu.VMEM((ms, HEAD_DIM), F32)],
        compiler_params=_params("arbitrary"),
        name="ffn_head",
    )(x, xs, g_pre.reshape(1, D_MODEL), g_post.reshape(1, D_MODEL), w_up, w_up, w_down)
    return outs[0], outs[1], tuple(outs[2:])


def _swa_kernel(q_ref, k_ref, v_ref, tb_ref, o_ref, lse_ref):
    n_units, n_res, u, _ = q_ref.shape
    per_blk = N_BACK // u
    n_blk = n_units // per_blk
    lane = lax.broadcasted_iota(jnp.int32, (N_BACK, HEAD_DIM), 1)
    ones = jnp.ones((2 * N_BACK, HEAD_DIM), BF16)

    def rows(ref, res, unit0, n_rows, lo, hi):
        return ref[pl.ds(unit0, n_rows // u), res, :, lo:hi].reshape(n_rows, hi - lo)

    def block(res, qu, ku, table):
        n_keys = N_BACK if table == 0 else 2 * N_BACK
        lse_tile = jnp.zeros((N_BACK, HEAD_DIM), F32)
        for h in range(HEADS_PER_GROUP):
            lo, hi = h * HEAD_DIM, (h + 1) * HEAD_DIM
            q = rows(q_ref, res, qu, N_BACK, lo, hi)
            kw = rows(k_ref, res, ku, n_keys, lo, hi)
            vw = rows(v_ref, res, ku, n_keys, lo, hi)
            s = lax.dot_general(q, kw, (((1,), (1,)), ((), ())), preferred_element_type=F32)
            s = s + tb_ref[table, h][:, :n_keys]
            m = jnp.max(s, axis=1, keepdims=True)
            p = jnp.exp(s - m).astype(BF16)
            ov = jnp.dot(p, jnp.concatenate([vw, ones[:n_keys]], axis=1), preferred_element_type=F32)
            den = ov[:, HEAD_DIM:]
            o = ov[:, :HEAD_DIM] / den
            o_ref[pl.ds(qu, per_blk), res, :, lo:hi] = o.reshape(per_blk, u, HEAD_DIM).astype(o_ref.dtype)
            lse_tile = jnp.where(lane == h, m + jnp.log(den), lse_tile)
        lse_ref[pl.ds(qu, per_blk), res, :, :] = lse_tile.reshape(per_blk, u, HEAD_DIM)

    def block_at(res, n):
        block(res, n * per_blk, (n - 1) * per_blk, 1)

    n_trips = (n_blk - 1) // SWA_BLOCKS_PER_TRIP
    for res in range(n_res):
        block(res, 0, 0, 0)

        def body(i, carry, res=res):
            for b in range(SWA_BLOCKS_PER_TRIP):
                block_at(res, 1 + SWA_BLOCKS_PER_TRIP * i + b)
            return carry

        if n_trips > 0:
            lax.fori_loop(0, n_trips, body, 0)
        for n in range(1 + n_trips * SWA_BLOCKS_PER_TRIP, n_blk):
            block_at(res, n)


def _sub_block(dil):
    return N_BACK if dil == 1 else PERM_BLOCK


def _swa_group(zb, tables, g, *, batch, seq):
    dil = SWA_PATTERN[g][1]
    sub = _sub_block(dil)
    n_units, u = seq // sub, sub // dil
    n_blk = n_units * u // N_BACK
    n_res = min(dil, max(1, SWA_BLOCKS_PER_TRIP // n_blk))
    view = zb.reshape(zb.shape[0], batch, n_units, dil, u, GROUP_WIDTH)

    def rows_in(tile):
        return pl.BlockSpec((None, None, n_units, n_res, u, GROUP_WIDTH), lambda i, r: (tile, i, 0, r, 0, 0))

    def rows_out(width):
        return pl.BlockSpec((None, n_units, n_res, u, width), lambda i, r: (i, 0, r, 0, 0))

    return pl.pallas_call(
        _swa_kernel,
        grid=(batch, dil // n_res),
        in_specs=[rows_in(g), rows_in(N_SWA_GROUPS + g), rows_in(2 * N_SWA_GROUPS + g),
                  pl.BlockSpec((None, 2, HEADS_PER_GROUP, N_BACK, 2 * N_BACK), lambda i, r: (g, 0, 0, 0, 0))],
        out_specs=[rows_out(GROUP_WIDTH), rows_out(HEAD_DIM)],
        out_shape=[
            jax.ShapeDtypeStruct((batch, n_units, dil, u, GROUP_WIDTH), BF16),
            jax.ShapeDtypeStruct((batch, n_units, dil, u, HEAD_DIM), F32),
        ],
        compiler_params=_params("parallel", "parallel"),
        name=f"swa_group{g}",
    )(view, view, view, tables)


def _split3(x):
    hi = x.astype(BF16)
    rest = x - hi.astype(F32)
    mid = rest.astype(BF16)
    lo = (rest - mid.astype(F32)).astype(BF16)
    return hi, mid, lo


def _merge_stage(o0_ref, o1_ref, o2_ref, l0_ref, l1_ref, l2_ref, q_ref, kb_ref, vb_ref, o_ref):
    tm = o_ref.shape[0]
    group_refs = ((o0_ref, l0_ref), (o1_ref, l1_ref), (o2_ref, l2_ref))
    outs, lses = [None] * N_SWA_GROUPS, [None] * N_SWA_GROUPS

    def token_order(g):
        o_g, l_g = group_refs[g]
        dil = SWA_PATTERN[g][1]
        o = o_g[...].reshape(tm, GROUP_WIDTH)
        l = l_g[...].reshape(tm, HEAD_DIM)
        if dil > 1:
            inv = _residue_major_perm(PERM_BLOCK, dil, transpose=True)
            l3 = _split3(l)
            o_nat, l_nat = [], []
            for s in range(0, tm, PERM_BLOCK):
                o_nat.append(jnp.dot(inv, o[s:s + PERM_BLOCK, :], preferred_element_type=F32))
                l_nat.append(sum(jnp.dot(inv, t[s:s + PERM_BLOCK, :], preferred_element_type=F32) for t in l3))
            o = jnp.concatenate(o_nat, axis=0)
            l = jnp.concatenate(l_nat, axis=0)
        outs[g] = o.astype(F32)
        lses[g] = l

    def merge_head(h):
        lo, hi = h * HEAD_DIM, (h + 1) * HEAD_DIM
        ls = [l[:, h:h + 1] for l in lses]
        mx = jnp.maximum(jnp.maximum(ls[0], ls[1]), ls[2])
        es = [jnp.exp(l - mx) for l in ls]
        tot = es[0] + es[1] + es[2]
        for g in range(N_SWA_GROUPS):
            alpha = es[g] / tot
            o_ref[:, g * GROUP_WIDTH + lo:g * GROUP_WIDTH + hi] = (outs[g][:, lo:hi] * alpha).astype(o_ref.dtype)

    return ([functools.partial(token_order, g) for g in range(N_SWA_GROUPS)]
            + [functools.partial(merge_head, h) for h in range(HEADS_PER_GROUP)]
            + _mem_attention_parts(q_ref, kb_ref, vb_ref, o_ref, MIXER_WIDTH))


def _swa_merge_out(outs, lses, zb, mem_kv, layer, w, x, g, *, rows_per_batch):
    tm = ROW_TILE
    tiles_per_batch = rows_per_batch // tm

    def group_tile(width, dil):
        sub = _sub_block(dil)
        return lambda tile: pl.BlockSpec(
            (None, tm // sub, dil, sub // dil, width),
            lambda s: (tile(s) // tiles_per_batch, tile(s) % tiles_per_batch, 0, 0, 0))

    specs = ([group_tile(GROUP_WIDTH, dil) for _, dil in SWA_PATTERN]
             + [group_tile(HEAD_DIM, dil) for _, dil in SWA_PATTERN]
             + [lambda tile: pl.BlockSpec((None, tm, MEM_WIDTH), lambda s: (3 * N_SWA_GROUPS, tile(s), 0))])
    return _mix_out(_merge_stage, (*outs, *lses, zb), specs, [], mem_kv, layer, w, x, g,
                    tm=tm, rows_per_batch=rows_per_batch, name="swa_merge_out")


def _head_rows(row, col0):
    return jnp.concatenate([row[:, col0 + h * HEAD_DIM:col0 + (h + 1) * HEAD_DIM]
                            for h in range(HEADS_PER_GROUP)], axis=0)


def _sample_attention(q4, kv_ref, bias=None, new=None):
    k3, v3 = kv_ref[:, 0], kv_ref[:, 1]
    s = jnp.sum(k3 * q4[None], axis=-1, keepdims=True) * ATTN_SCALE
    if bias is not None:
        s = s + bias
    m = jnp.max(s, axis=0)
    if new is not None:
        k_new, v_new, b_new = new
        s_new = jnp.sum(k_new * q4, axis=-1, keepdims=True) * ATTN_SCALE + b_new
        m = jnp.maximum(m, s_new)
    p = jnp.exp(s - m[None])
    den = jnp.sum(p, axis=0)
    o = jnp.sum(p * v3, axis=0)
    if new is not None:
        p_new = jnp.exp(s_new - m)
        den = den + p_new
        o = o + p_new * v_new
    return o / den, m + jnp.log(den)


def _store_head_rows(o_ref, col0, x4):
    for h in range(HEADS_PER_GROUP):
        o_ref[:, col0 + h * HEAD_DIM:col0 + (h + 1) * HEAD_DIM] = x4[h:h + 1, :]


def _sample_mem_attention(q_row, kv_ref, o_ref, col0):
    o, _ = _sample_attention(_head_rows(q_row, 0), kv_ref)
    _store_head_rows(o_ref, col0, o)


def _sample_mix_a_kernel(z_ref, gv_ref, w0_ref, b0_ref, kv_ref, o_ref, vrow_ref):
    u = z_ref[:, 0:MIXER_WIDTH]
    v = _rms(z_ref[:, MIXER_WIDTH:2 * MIXER_WIDTH], gv_ref[...])
    vrow_ref[...] = v
    o_ref[:, 0:MIXER_WIDTH] = u * (w0_ref[...] * v + b0_ref[...])
    _sample_mem_attention(z_ref[:, 2 * MIXER_WIDTH:2 * MIXER_WIDTH + MEM_WIDTH], kv_ref, o_ref, MIXER_WIDTH)


def _sample_mix_a(z, g_v, w_s, b_s, mem_kv, layer):
    bd = mem_kv.shape[1]
    w0 = jnp.repeat(w_s[:, 0, 0], GROUP_DIM_A).reshape(1, MIXER_WIDTH)
    b0 = jnp.repeat(b_s[:, 0], GROUP_DIM_A).reshape(1, MIXER_WIDTH)
    width = z.shape[1]
    vec = lambda i: (0, 0)
    return pl.pallas_call(
        _sample_mix_a_kernel,
        grid=(bd,),
        in_specs=[
            pl.BlockSpec((None, 1, width), lambda i: (i, 0, 0)),
            pl.BlockSpec((1, MIXER_WIDTH), vec),
            pl.BlockSpec((1, MIXER_WIDTH), vec),
            pl.BlockSpec((1, MIXER_WIDTH), vec),
            _mem_kv_spec(layer, lambda i: i),
        ],
        out_specs=[
            pl.BlockSpec((None, 1, D_MODEL), lambda i: (i, 0, 0)),
            pl.BlockSpec((None, 1, MIXER_WIDTH), lambda i: (i, 0, 0)),
        ],
        out_shape=[
            jax.ShapeDtypeStruct((bd, 1, D_MODEL), F32),
            jax.ShapeDtypeStruct((bd, 1, MIXER_WIDTH), F32),
        ],
        compiler_params=_params("parallel"),
        name="sample_mix_a",
    )(z[:bd].reshape(bd, 1, width), g_v.reshape(1, MIXER_WIDTH), w0, b0, mem_kv)


def _sample_mix_b_kernel(z_ref, c0_ref, c1_ref, c2_ref, bcol_ref, bnew_ref, kv_ref, o_ref):
    caches = (c0_ref, c1_ref, c2_ref)
    z = z_ref[...]
    outs, lses = [], []
    for g in range(N_SWA_GROUPS):
        c0 = g * GROUP_WIDTH
        new = (_head_rows(z, MIXER_WIDTH + c0), _head_rows(z, 2 * MIXER_WIDTH + c0), bnew_ref[g])
        o, lse = _sample_attention(_head_rows(z, c0), caches[g], bias=bcol_ref[g], new=new)
        outs.append(o)
        lses.append(lse)
    mx = jnp.maximum(jnp.maximum(lses[0], lses[1]), lses[2])
    es = [jnp.exp(l - mx) for l in lses]
    tot = es[0] + es[1] + es[2]
    for g in range(N_SWA_GROUPS):
        _store_head_rows(o_ref, g * GROUP_WIDTH, outs[g] * (es[g] / tot))
    _sample_mem_attention(z[:, 3 * MIXER_WIDTH:3 * MIXER_WIDTH + MEM_WIDTH], kv_ref, o_ref, MIXER_WIDTH)


def _sample_mix_b(z, win_caches, swa_layer, bias_groups, mem_kv, layer):
    bd = mem_kv.shape[1]
    width = z.shape[1]
    cache_views, cache_specs = [], []
    for g, (win, dil) in enumerate(SWA_PATTERN):
        c = win_caches[g]
        cache_views.append(c.reshape(c.shape[0], bd, win // dil, dil, 2, HEADS_PER_GROUP, HEAD_DIM))
        cache_specs.append(pl.BlockSpec((None, None, N_BACK, None, 2, HEADS_PER_GROUP, HEAD_DIM),
                                        lambda i: (swa_layer, i, 0, 0, 0, 0, 0)))
    bcol = jnp.stack([bg[:, N_BACK:0:-1].T for bg in bias_groups], axis=0)
    bcol = jnp.broadcast_to(bcol[..., None], bcol.shape + (HEAD_DIM,))
    bnew = jnp.stack([bg[:, 0] for bg in bias_groups], axis=0)
    bnew = jnp.broadcast_to(bnew[..., None], bnew.shape + (HEAD_DIM,))
    return pl.pallas_call(
        _sample_mix_b_kernel,
        grid=(bd,),
        in_specs=[pl.BlockSpec((None, 1, width), lambda i: (i, 0, 0))] + cache_specs + [
            pl.BlockSpec((N_SWA_GROUPS, N_BACK, HEADS_PER_GROUP, HEAD_DIM), lambda i: (0, 0, 0, 0)),
            pl.BlockSpec((N_SWA_GROUPS, HEADS_PER_GROUP, HEAD_DIM), lambda i: (0, 0, 0)),
            _mem_kv_spec(layer, lambda i: i),
        ],
        out_specs=pl.BlockSpec((None, 1, D_MODEL), lambda i: (i, 0, 0)),
        out_shape=jax.ShapeDtypeStruct((bd, 1, D_MODEL), F32),
        compiler_params=_params("parallel"),
        name="sample_mix_b",
    )(z[:bd].reshape(bd, 1, width), *cache_views, bcol, bnew, mem_kv)


def _t5_bucket(dist):
    nf = jnp.maximum(dist, MAX_EXACT).astype(F32)
    large = MAX_EXACT + (jnp.log(nf / MAX_EXACT) / math.log(MAX_DISTANCE / MAX_EXACT)
                         * (N_BUCKETS - MAX_EXACT)).astype(jnp.int32)
    large = jnp.minimum(large, N_BUCKETS - 1)
    return jnp.where(dist < MAX_EXACT, dist, large)


def _group_bias(rel_bias, g, dil):
    dist = jnp.arange(N_BACK + 1, dtype=jnp.int32) * dil
    b = rel_bias[_t5_bucket(dist)][:, g * HEADS_PER_GROUP:(g + 1) * HEADS_PER_GROUP]
    return b.T.astype(F32)


def _band_tables_kernel(b_ref, o_ref):
    n, rows, width = o_ref.shape
    for x in range(n):
        row = jnp.broadcast_to(b_ref[x:x + 1, :], (rows, width))
        o_ref[x] = pltpu.roll(row, 0, 1, stride=1, stride_axis=0)


def _band_tables(bias_groups):
    width = 2 * N_BACK
    rows = []
    for bias_j in bias_groups:
        masked = jnp.full((HEADS_PER_GROUP, N_BACK - 1), NEG_INF, F32)
        rows.append(jnp.concatenate([bias_j[:, :1], masked, bias_j[:, N_BACK:0:-1]], axis=1))
        rows.append(jnp.concatenate([bias_j[:, ::-1], masked], axis=1))
    base = jnp.stack(rows, axis=0).reshape(-1, width)
    tabs = pl.pallas_call(
        _band_tables_kernel,
        out_shape=jax.ShapeDtypeStruct((base.shape[0], N_BACK, width), F32),
        name="band_tables",
    )(base)
    return tabs.reshape(N_SWA_GROUPS, 2, HEADS_PER_GROUP, N_BACK, width)


def _kv_tail_kernel(k_ref, v_ref, o_ref, *, dil):
    rows = k_ref.shape[0]
    for kv, ref in enumerate((k_ref, v_ref)):
        x = ref[...]
        if dil > 1:
            inv = _residue_major_perm(PERM_BLOCK, dil, transpose=True)
            x = jnp.concatenate([jnp.dot(inv, x[s:s + PERM_BLOCK, :], preferred_element_type=F32)
                                 for s in range(0, rows, PERM_BLOCK)], axis=0)
        x = x.astype(F32)
        for h in range(HEADS_PER_GROUP):
            o_ref[:, kv, h, :] = x[:, h * HEAD_DIM:(h + 1) * HEAD_DIM]


def _kv_tail(zb, g, *, batch, seq):
    win, dil = SWA_PATTERN[g]
    rows = min(win, FFN_ROW_TILE)
    first = (seq - win) // rows
    per_batch = seq // rows

    def tile(t):
        return pl.BlockSpec((None, rows, GROUP_WIDTH), lambda b, s: (t, b * per_batch + first + s, 0))

    return pl.pallas_call(
        functools.partial(_kv_tail_kernel, dil=dil),
        grid=(batch, win // rows),
        in_specs=[tile(N_SWA_GROUPS + g), tile(2 * N_SWA_GROUPS + g)],
        out_specs=pl.BlockSpec((None, rows, 2, HEADS_PER_GROUP, HEAD_DIM), lambda b, s: (b, s, 0, 0, 0)),
        out_shape=jax.ShapeDtypeStruct((batch, win, 2, HEADS_PER_GROUP, HEAD_DIM), F32),
        compiler_params=_params("parallel", "parallel"),
        name=f"kv_tail{g}",
    )(zb, zb)


def kernel(x_prompt, x_sample, mem_prompt, cache_mem_kv, cache_win128_kv, cache_win512_kv, cache_win2048_kv, rel_bias, norm_mix_pre, norm_mix_post, norm_ffn_pre, norm_ffn_post, norm_mem, w_mem_kv, w_in_a, norm_v_a, w_spatial_a, b_spatial_a, w_in_b, w_out, w_ffn_up, w_ffn_down):
    batch, seq, _ = x_prompt.shape
    bd = x_sample.shape[0]
    depth = w_out.shape[0]
    m_p = batch * seq
    win_caches = (cache_win128_kv, cache_win512_kv, cache_win2048_kv)

    bias_groups = [_group_bias(rel_bias, g, dil) for g, (_, dil) in enumerate(SWA_PATTERN)]
    band_tables = _band_tables(bias_groups)

    yp = x_prompt.reshape(m_p, D_MODEL)
    ys = jnp.pad(x_sample.reshape(bd, D_MODEL), ((0, SAMPLE_PAD - bd), (0, 0)))
    mem_rows = mem_prompt.reshape(batch * N_MEM, D_MODEL)

    mem_kv_p = _mem_kv(mem_rows, norm_mem, w_mem_kv, batch=batch)
    chunk_v_s = []
    win_p = [[] for _ in SWA_PATTERN]
    win_s = [[] for _ in SWA_PATTERN]
    for i in range(depth):
        li = i // 2
        if i % 2 == 0:
            zp, zs = _in_proj_a(yp, ys, norm_mix_pre[i], w_in_a, li, tm=ROW_TILE, tn=CAST_TILE,
                                gelu_cols=2 * MIXER_WIDTH)
            mix_s, v_rows = _sample_mix_a(zs, norm_v_a[li], w_spatial_a[li], b_spatial_a[li], cache_mem_kv, i)
            chunk_v_s.append(v_rows)
            ys, w_o = _sample_out_proj(mix_s, w_out, i, ys, norm_mix_post[i])
            yp = _gmlp_mix_out(zp, norm_v_a[li], w_spatial_a[li], b_spatial_a[li], mem_kv_p, i,
                               w_o, yp, norm_mix_post[i], tm=ROW_TILE, rows_per_batch=seq)
        else:
            zb, zs = _in_proj_b(yp, ys, norm_mix_pre[i], w_in_b, li, tm=ROW_TILE)
            outs, lses = [], []
            for g, (win, dil) in enumerate(SWA_PATTERN):
                o, lse = _swa_group(zb, band_tables, g, batch=batch, seq=seq)
                outs.append(o)
                lses.append(lse)
                win_p[g].append(_kv_tail(zb, g, batch=batch, seq=seq))
                kv_new = zs[:bd, MIXER_WIDTH:3 * MIXER_WIDTH]
                kv_new = kv_new.reshape(bd, 1, 2, N_SWA_GROUPS, HEADS_PER_GROUP, HEAD_DIM)[:, :, :, g]
                win_s[g].append(kv_new)
            mix_s = _sample_mix_b(zs, win_caches, li, bias_groups, cache_mem_kv, i)
            ys, w_o = _sample_out_proj(mix_s, w_out, i, ys, norm_mix_post[i])
            yp = _swa_merge_out(outs, lses, zb, mem_kv_p, i, w_o, yp, norm_mix_post[i], rows_per_batch=seq)
        yp, ys, w_ffn = _ffn_head(yp, ys, norm_ffn_pre[i], norm_ffn_post[i], w_ffn_up, w_ffn_down, i,
                                  tm=FFN_ROW_TILE, tf=HEAD_FF_TILE)
        yp = _ffn(yp, norm_ffn_pre[i], norm_ffn_post[i], w_ffn, tm=FFN_ROW_TILE, tf=FF_TILE, first_tile=1)

    return (
        yp.reshape(batch, seq, D_MODEL),
        ys[:bd].reshape(bd, 1, D_MODEL),
        mem_kv_p,
        jnp.stack(chunk_v_s, axis=0),
        jnp.stack(win_p[0], axis=0),
        jnp.stack(win_p[1], axis=0),
        jnp.stack(win_p[2], axis=0),
        jnp.stack(win_s[0], axis=0),
        jnp.stack(win_s[1], axis=0),
        jnp.stack(win_s[2], axis=0),
    )
```

```python
import functools
import math

import jax
import jax.numpy as jnp
from jax import lax
from jax.experimental import pallas as pl
from jax.experimental.pallas import tpu as pltpu

F32 = jnp.float32
BF16 = jnp.bfloat16

D_MODEL = 2048
HEAD_DIM = 128
N_MEM = 256
N_MEM_HEADS = 4
MEM_WIDTH = N_MEM_HEADS * HEAD_DIM
MIXER_WIDTH = D_MODEL - MEM_WIDTH
CHUNK = 128
N_GROUPS_A = 4
GROUP_DIM_A = MIXER_WIDTH // N_GROUPS_A
SWA_PATTERN = ((128, 1), (512, 4), (2048, 16))
N_SWA_GROUPS = len(SWA_PATTERN)
HEADS_PER_GROUP = 4
GROUP_WIDTH = HEADS_PER_GROUP * HEAD_DIM
N_BACK = 128
N_BUCKETS = 32
MAX_EXACT = N_BUCKETS // 2
MAX_DISTANCE = 2048
D_FF = 5632
EPS = 1e-6
NEG_INF = -1e30
ATTN_SCALE = HEAD_DIM ** -0.5
SAMPLE_PAD = 16
PERM_BLOCK = 256
SWA_BLOCKS_PER_TRIP = 32
OUT_CHUNK = 256

ROW_TILE = 512
FFN_ROW_TILE = 1024
FF_TILE = 512
HEAD_FF_TILE = 256
CAST_TILE = 512

V7X_VMEM_BYTES = 64 * 1024 * 1024
VMEM_LIMIT = V7X_VMEM_BYTES - 8 * 1024 * 1024
FFN_VMEM_LIMIT = V7X_VMEM_BYTES - 2 * 1024 * 1024


def _params(*sem, vmem_limit=VMEM_LIMIT):
    return pltpu.CompilerParams(dimension_semantics=sem, vmem_limit_bytes=vmem_limit)


def _gelu(x):
    return 0.5 * x * (1.0 + jnp.tanh(0.7978845608028654 * (x + 0.044715 * (x * x * x))))


def _rms(x, g):
    return x * lax.rsqrt(jnp.mean(x * x, axis=-1, keepdims=True) + EPS) * g


def _log2(n):
    assert n & (n - 1) == 0
    return n.bit_length() - 1


def _residue_major_perm(tm, dil, transpose=False):
    n = tm // dil
    row = lax.broadcasted_iota(jnp.int32, (tm, tm), 0)
    col = lax.broadcasted_iota(jnp.int32, (tm, tm), 1)
    dst, src = (col, row) if transpose else (row, col)
    want = lax.shift_left(jnp.bitwise_and(dst, n - 1), _log2(dil)) + lax.shift_right_logical(dst, _log2(n))
    return (src == want).astype(BF16)


def _in_proj_a_kernel(x_ref, xs_ref, g_ref, w_ref, zp_ref, zs_ref, wb_ref, xn_ref, *, n_col, gelu_cols):
    tm = x_ref.shape[0]
    tn = w_ref.shape[1]
    s = pl.program_id(0)

    def prompt_cols(acc, c0):
        col = lax.broadcasted_iota(jnp.int32, acc.shape, 1) + c0
        return jnp.where(col < gelu_cols, _gelu(acc), acc * ATTN_SCALE).astype(zp_ref.dtype)

    @pl.when(s == 0)
    def _():
        xn_ref[:tm, :] = _rms(x_ref[...], g_ref[...]).astype(BF16)
        xn_ref[tm:, :] = _rms(xs_ref[...], g_ref[...]).astype(BF16)

    for j in range(n_col):
        @pl.when(s == j)
        def _(j=j):
            cols = slice(j * tn, (j + 1) * tn)
            w = w_ref[...].astype(BF16)
            wb_ref[:, cols] = w
            acc = jnp.dot(xn_ref[...], w, preferred_element_type=F32)
            zp_ref[:, cols] = prompt_cols(acc[:tm, :], j * tn)
            sample = acc[tm:, :]
            zs_ref[:, cols] = _gelu(sample) if (j + 1) * tn <= gelu_cols else sample

    @pl.when(s >= n_col)
    def _():
        xn = _rms(x_ref[...], g_ref[...]).astype(BF16)
        acc = jnp.dot(xn, wb_ref[...], preferred_element_type=F32)
        zp_ref[...] = prompt_cols(acc, 0)


def _in_proj_a(x, xs, g, w, layer, *, tm, tn, gelu_cols):
    m, k = x.shape
    ms = xs.shape[0]
    n = w.shape[2]
    n_col, n_row = n // tn, m // tm
    assert gelu_cols % tn == 0
    row = lambda s: jnp.maximum(s - (n_col - 1), 0)
    return pl.pallas_call(
        functools.partial(_in_proj_a_kernel, n_col=n_col, gelu_cols=gelu_cols),
        grid=(n_col + n_row - 1,),
        in_specs=[
            pl.BlockSpec((tm, k), lambda s: (row(s), 0)),
            pl.BlockSpec((ms, k), lambda s: (0, 0)),
            pl.BlockSpec((1, k), lambda s: (0, 0)),
            pl.BlockSpec((None, k, tn), lambda s: (layer, 0, jnp.minimum(s, n_col - 1))),
        ],
        out_specs=[
            pl.BlockSpec((tm, n), lambda s: (row(s), 0)),
            pl.BlockSpec((ms, n), lambda s: (0, 0)),
        ],
        out_shape=[
            jax.ShapeDtypeStruct((m, n), BF16),
            jax.ShapeDtypeStruct((ms, n), F32),
        ],
        scratch_shapes=[pltpu.VMEM((k, n), BF16), pltpu.VMEM((tm + ms, k), BF16)],
        compiler_params=_params("arbitrary"),
        name="in_proj_a",
    )(x, xs, g.reshape(1, k), w)


def _mem_kv_kernel(x_ref, g_ref, w_ref, o_ref):
    xn = _rms(x_ref[...], g_ref[...]).astype(BF16)
    acc = jnp.dot(xn, w_ref[...].astype(BF16), preferred_element_type=F32)
    for kv in range(2):
        for h in range(N_MEM_HEADS):
            c0 = (kv * N_MEM_HEADS + h) * HEAD_DIM
            o_ref[:, kv, h, :] = acc[:, c0:c0 + HEAD_DIM]


def _mem_kv(mem_rows, g, w, *, batch):
    m, k = mem_rows.shape
    layers = w.shape[0]
    out = pl.pallas_call(
        _mem_kv_kernel,
        grid=(layers,),
        in_specs=[
            pl.BlockSpec((m, k), lambda l: (0, 0)),
            pl.BlockSpec((None, 1, k), lambda l: (l, 0, 0)),
            pl.BlockSpec((None, k, 2 * MEM_WIDTH), lambda l: (l, 0, 0)),
        ],
        out_specs=pl.BlockSpec((None, m, 2, N_MEM_HEADS, HEAD_DIM), lambda l: (l, 0, 0, 0, 0)),
        out_shape=jax.ShapeDtypeStruct((layers, m, 2, N_MEM_HEADS, HEAD_DIM), F32),
        compiler_params=_params("parallel"),
        name="mem_kv",
    )(mem_rows, g.reshape(layers, 1, k), w)
    return out.reshape(layers, batch, m // batch, 2, N_MEM_HEADS, HEAD_DIM)


def _in_proj_b_kernel(x_ref, xs_ref, g_ref, w_ref, o_ref, zs_ref, wb_ref, xn_ref):
    tm = x_ref.shape[0]
    n_tiles = o_ref.shape[0]
    n_qkv = 3 * N_SWA_GROUPS
    s = pl.program_id(0)

    def row_orders():
        xn = _rms(x_ref[...], g_ref[...]).astype(BF16)
        xn_ref[0, :tm, :] = xn
        for g in range(1, N_SWA_GROUPS):
            perm = _residue_major_perm(PERM_BLOCK, SWA_PATTERN[g][1])
            for r in range(0, tm, PERM_BLOCK):
                xn_ref[g, r:r + PERM_BLOCK, :] = jnp.dot(
                    perm, xn[r:r + PERM_BLOCK, :], preferred_element_type=F32).astype(BF16)

    def prompt_tile(t, acc):
        if t < N_SWA_GROUPS or t == n_qkv:
            acc = acc * ATTN_SCALE
        o_ref[t] = acc.astype(o_ref.dtype)

    src = lambda t: t % N_SWA_GROUPS if t < n_qkv else 0

    @pl.when(s == 0)
    def _():
        row_orders()
        xs = _rms(xs_ref[...], g_ref[...]).astype(BF16)
        for g in range(N_SWA_GROUPS):
            xn_ref[g, tm:, :] = xs

    for t in range(n_tiles):
        @pl.when(s == t)
        def _(t=t):
            cols = slice(t * GROUP_WIDTH, (t + 1) * GROUP_WIDTH)
            w = w_ref[...].astype(BF16)
            wb_ref[:, cols] = w
            acc = jnp.dot(xn_ref[src(t)], w, preferred_element_type=F32)
            prompt_tile(t, acc[:tm, :])
            zs_ref[:, cols] = acc[tm:, :]

    @pl.when(s >= n_tiles)
    def _():
        row_orders()
        for t in range(n_tiles):
            w = wb_ref[:, t * GROUP_WIDTH:(t + 1) * GROUP_WIDTH]
            prompt_tile(t, jnp.dot(xn_ref[src(t), :tm, :], w, preferred_element_type=F32))


def _in_proj_b(x, xs, g, w, layer, *, tm):
    m, k = x.shape
    ms = xs.shape[0]
    n = w.shape[2]
    n_tiles = n // GROUP_WIDTH
    row = lambda s: jnp.maximum(s - (n_tiles - 1), 0)
    return pl.pallas_call(
        _in_proj_b_kernel,
        grid=(n_tiles + m // tm - 1,),
        in_specs=[
            pl.BlockSpec((tm, k), lambda s: (row(s), 0)),
            pl.BlockSpec((ms, k), lambda s: (0, 0)),
            pl.BlockSpec((1, k), lambda s: (0, 0)),
            pl.BlockSpec((None, k, GROUP_WIDTH), lambda s: (layer, 0, jnp.minimum(s, n_tiles - 1))),
        ],
        out_specs=[
            pl.BlockSpec((n_tiles, tm, GROUP_WIDTH), lambda s: (0, row(s), 0)),
            pl.BlockSpec((ms, n), lambda s: (0, 0)),
        ],
        out_shape=[
            jax.ShapeDtypeStruct((n_tiles, m, GROUP_WIDTH), BF16),
            jax.ShapeDtypeStruct((ms, n), F32),
        ],
        scratch_shapes=[pltpu.VMEM((k, n), BF16), pltpu.VMEM((N_SWA_GROUPS, tm + ms, k), BF16)],
        compiler_params=_params("arbitrary"),
        name="in_proj_b",
    )(x, xs, g.reshape(1, k), w)


MEM_KV_SCRATCH = [pltpu.VMEM((N_MEM_HEADS, N_MEM, HEAD_DIM), BF16),
                  pltpu.VMEM((N_MEM_HEADS, N_MEM, 2 * HEAD_DIM), BF16)]


def _prepare_mem_kv(kv_ref, kb_ref, vb_ref):
    for h in range(N_MEM_HEADS):
        kb_ref[h] = kv_ref[:, 0, h, :].astype(BF16)
        vb_ref[h, :, :HEAD_DIM] = kv_ref[:, 1, h, :].astype(BF16)
        vb_ref[h, :, HEAD_DIM:] = jnp.ones((N_MEM, HEAD_DIM), BF16)


def _mem_attention_parts(q_ref, kb_ref, vb_ref, o_ref, col0):
    def head(h):
        lo, hi = h * HEAD_DIM, (h + 1) * HEAD_DIM
        s = lax.dot_general(q_ref[:, lo:hi], kb_ref[h], (((1,), (1,)), ((), ())), preferred_element_type=F32)
        m = jnp.max(s, axis=1, keepdims=True)
        p = jnp.exp(s - m).astype(BF16)
        ov = jnp.dot(p, vb_ref[h], preferred_element_type=F32)
        o_ref[:, col0 + lo:col0 + hi] = (ov[:, :HEAD_DIM] / ov[:, HEAD_DIM:]).astype(o_ref.dtype)

    return [functools.partial(head, h) for h in range(N_MEM_HEADS)]


def _gmlp_stage(u_ref, v_ref, q_ref, gv_ref, ws_ref, bs_ref, kb_ref, vb_ref, vn_ref, o_ref):
    tm = u_ref.shape[0]

    def norm_v():
        vn_ref[...] = _rms(v_ref[...].astype(F32), gv_ref[...]).astype(BF16)

    def group(g):
        row = lax.broadcasted_iota(jnp.int32, (CHUNK, CHUNK), 0)
        col = lax.broadcasted_iota(jnp.int32, (CHUNK, CHUNK), 1)
        w = jnp.where(row >= col, ws_ref[g], 0.0).astype(BF16)
        b = bs_ref[:, g:g + 1]
        c0, c1 = g * GROUP_DIM_A, (g + 1) * GROUP_DIM_A
        for c in range(tm // CHUNK):
            r0, r1 = c * CHUNK, (c + 1) * CHUNK
            s = jnp.dot(w, vn_ref[r0:r1, c0:c1], preferred_element_type=F32) + b
            o_ref[r0:r1, c0:c1] = (u_ref[r0:r1, c0:c1].astype(F32) * s).astype(o_ref.dtype)

    return ([norm_v] + [functools.partial(group, g) for g in range(N_GROUPS_A)]
            + _mem_attention_parts(q_ref, kb_ref, vb_ref, o_ref, MIXER_WIDTH))


def _mem_kv_spec(layer, batch_of):
    return pl.BlockSpec((None, None, N_MEM, 2, N_MEM_HEADS, HEAD_DIM),
                        lambda i: (layer, batch_of(i), 0, 0, 0, 0))


def _mix_out_kernel(*refs, stage, n_in, n_tiles, tiles_per_batch):
    mix_in = refs[:n_in]
    kv_ref, w_ref, x_ref, g_ref, o_ref, buf0_ref, buf1_ref, acc_ref, kb_ref, vb_ref = refs[n_in:n_in + 10]
    extra = refs[n_in + 10:]
    s = pl.program_id(0)

    @pl.when(s == 0)
    def _():
        buf1_ref[...] = jnp.zeros_like(buf1_ref)

    @pl.when(jnp.minimum(s, n_tiles - 1) % tiles_per_batch == 0)
    def _():
        _prepare_mem_kv(kv_ref, kb_ref, vb_ref)

    def step(dst_ref, src_ref):
        parts = stage(*mix_in, kb_ref, vb_ref, *extra, dst_ref)
        n_chunks = D_MODEL // OUT_CHUNK
        for c in range(n_chunks):
            cols = slice(c * OUT_CHUNK, (c + 1) * OUT_CHUNK)
            acc_ref[:, cols] = jnp.dot(src_ref[...], w_ref[:, cols], preferred_element_type=F32)
            for part in parts[c * len(parts) // n_chunks:(c + 1) * len(parts) // n_chunks]:
                part()
        o_ref[...] = x_ref[...] + _rms(acc_ref[...], g_ref[...])

    @pl.when(s % 2 == 0)
    def _():
        step(buf0_ref, buf1_ref)

    @pl.when(s % 2 == 1)
    def _():
        step(buf1_ref, buf0_ref)


def _mix_out(stage, mix_inputs, mix_specs, extra_scratch, mem_kv, layer, w, x, g, *, tm, rows_per_batch, name):
    m = x.shape[0]
    n_tiles = m // tm
    tiles_per_batch = rows_per_batch // tm
    mix_tile = lambda s: jnp.minimum(s, n_tiles - 1)
    out_tile = lambda s: jnp.maximum(s - 1, 0)
    row_spec = pl.BlockSpec((tm, D_MODEL), lambda s: (out_tile(s), 0))
    return pl.pallas_call(
        functools.partial(_mix_out_kernel, stage=stage, n_in=len(mix_inputs), n_tiles=n_tiles,
                          tiles_per_batch=tiles_per_batch),
        grid=(n_tiles + 1,),
        in_specs=[spec(mix_tile) for spec in mix_specs] + [
            _mem_kv_spec(layer, lambda s: mix_tile(s) // tiles_per_batch),
            pl.BlockSpec((D_MODEL, D_MODEL), lambda s: (0, 0), pipeline_mode=pl.Buffered(1)),
            row_spec,
            pl.BlockSpec((1, D_MODEL), lambda s: (0, 0)),
        ],
        out_specs=row_spec,
        out_shape=jax.ShapeDtypeStruct((m, D_MODEL), F32),
        scratch_shapes=[pltpu.VMEM((tm, D_MODEL), BF16), pltpu.VMEM((tm, D_MODEL), BF16),
                        pltpu.VMEM((tm, D_MODEL), F32)] + MEM_KV_SCRATCH + list(extra_scratch),
        compiler_params=_params("arbitrary"),
        name=name,
    )(*mix_inputs, mem_kv, w, x, g.reshape(1, D_MODEL))


def _gmlp_mix_out(zact, g_v, w_s, b_s, mem_kv, layer, w, x, g, *, tm, rows_per_batch):
    const = lambda shape: (lambda tile: pl.BlockSpec(shape, lambda s: (0,) * len(shape)))
    specs = [
        lambda tile: pl.BlockSpec((tm, MIXER_WIDTH), lambda s: (tile(s), 0)),
        lambda tile: pl.BlockSpec((tm, MIXER_WIDTH), lambda s: (tile(s), 1)),
        lambda tile: pl.BlockSpec((tm, MEM_WIDTH), lambda s: (tile(s), 2 * MIXER_WIDTH // MEM_WIDTH)),
        const((1, MIXER_WIDTH)),
        const((N_GROUPS_A, CHUNK, CHUNK)),
        const((CHUNK, N_GROUPS_A)),
    ]
    return _mix_out(_gmlp_stage, (zact, zact, zact, g_v.reshape(1, MIXER_WIDTH), w_s, b_s.T), specs,
                    [pltpu.VMEM((tm, MIXER_WIDTH), BF16)], mem_kv, layer, w, x, g,
                    tm=tm, rows_per_batch=rows_per_batch, name="gmlp_mix_out")


def _out_proj_cast_kernel(mix_ref, w_ref, x_ref, g_ref, o_ref, wb_ref, acc_ref):
    j = pl.program_id(0)
    w = w_ref[...].astype(BF16)
    wb_ref[...] = w
    acc_ref[j] = jnp.dot(mix_ref[...], w, preferred_element_type=F32)

    @pl.when(j == pl.num_programs(0) - 1)
    def _():
        o = jnp.concatenate([acc_ref[t] for t in range(acc_ref.shape[0])], axis=1)
        o_ref[...] = x_ref[...] + _rms(o, g_ref[...])


def _sample_out_proj(mix, w, layer, x, g):
    bd = mix.shape[0]
    mix = jnp.pad(mix.reshape(bd, D_MODEL), ((0, x.shape[0] - bd), (0, 0))).astype(BF16)
    return _out_proj_cast(mix, w, layer, x, g, tn=CAST_TILE)


def _out_proj_cast(mix, w, layer, x, g, *, tn):
    m = x.shape[0]
    n_tiles = D_MODEL // tn
    return pl.pallas_call(
        _out_proj_cast_kernel,
        grid=(n_tiles,),
        in_specs=[
            pl.BlockSpec((m, D_MODEL), lambda j: (0, 0)),
            pl.BlockSpec((None, D_MODEL, tn), lambda j: (layer, 0, j)),
            pl.BlockSpec((m, D_MODEL), lambda j: (0, 0)),
            pl.BlockSpec((1, D_MODEL), lambda j: (0, 0)),
        ],
        out_specs=[
            pl.BlockSpec((m, D_MODEL), lambda j: (0, 0)),
            pl.BlockSpec((D_MODEL, tn), lambda j: (0, j)),
        ],
        out_shape=[
            jax.ShapeDtypeStruct((m, D_MODEL), F32),
            jax.ShapeDtypeStruct((D_MODEL, D_MODEL), BF16),
        ],
        scratch_shapes=[pltpu.VMEM((n_tiles, m, tn), F32)],
        compiler_params=_params("arbitrary"),
        name="out_proj_cast",
    )(mix, w, x, g.reshape(1, D_MODEL))


def _accumulate(o_ref, ssq_ref, part):
    new = o_ref[...] + part
    o_ref[...] = new
    sq = new * new
    ssq_ref[...] = functools.reduce(
        jnp.add, [sq[:, c:c + HEAD_DIM] for c in range(0, sq.shape[1], HEAD_DIM)])


def _residual_norm(x_ref, o_ref, ssq_ref, g_ref):
    ms = jnp.sum(ssq_ref[...], axis=-1, keepdims=True) * (1.0 / o_ref.shape[1])
    o_ref[...] = x_ref[...] + o_ref[...] * lax.rsqrt(ms + EPS) * g_ref[...]


def _ffn_kernel(x_ref, gpre_ref, gpost_ref, wg_ref, wl_ref, wd_ref, o_ref, xn_ref, ssq_ref):
    f = pl.program_id(1)

    @pl.when(f == 0)
    def _():
        xn_ref[...] = _rms(x_ref[...], gpre_ref[...]).astype(BF16)
        o_ref[...] = jnp.zeros_like(o_ref)

    xn = xn_ref[...]
    hg = jnp.dot(xn, wg_ref[...], preferred_element_type=F32)
    hl = jnp.dot(xn, wl_ref[...], preferred_element_type=F32)
    a = (hg * jax.nn.sigmoid(hg) * hl).astype(BF16)
    _accumulate(o_ref, ssq_ref, jnp.dot(a, wd_ref[...], preferred_element_type=F32))

    @pl.when(f == pl.num_programs(1) - 1)
    def _():
        _residual_norm(x_ref, o_ref, ssq_ref, gpost_ref)


def _ffn(x, g_pre, g_post, weights, *, tm, tf, first_tile):
    m = x.shape[0]
    nf = D_FF // tf
    rows = pl.BlockSpec((tm, D_MODEL), lambda i, f: (i + first_tile, 0))
    return pl.pallas_call(
        _ffn_kernel,
        grid=(m // tm - first_tile, nf),
        in_specs=[
            rows,
            pl.BlockSpec((1, D_MODEL), lambda i, f: (0, 0)),
            pl.BlockSpec((1, D_MODEL), lambda i, f: (0, 0)),
            pl.BlockSpec((D_MODEL, tf), lambda i, f: (0, f)),
            pl.BlockSpec((D_MODEL, tf), lambda i, f: (0, f)),
            pl.BlockSpec((tf, D_MODEL), lambda i, f: (f, 0)),
        ],
        out_specs=rows,
        out_shape=jax.ShapeDtypeStruct((m, D_MODEL), F32),
        input_output_aliases={0: 0},
        scratch_shapes=[pltpu.VMEM((tm, D_MODEL), BF16), pltpu.VMEM((tm, HEAD_DIM), F32)],
        compiler_params=_params("parallel", "arbitrary", vmem_limit=FFN_VMEM_LIMIT),
        name="ffn",
    )(x, g_pre.reshape(1, D_MODEL), g_post.reshape(1, D_MODEL), *weights)


def _ffn_head_kernel(x_ref, xs_ref, gpre_ref, gpost_ref, wg_ref, wl_ref, wd_ref,
                     o_ref, os_ref, wgb_ref, wlb_ref, wdb_ref, xn_ref, ssq_ref, ssqs_ref):
    tm = x_ref.shape[0]
    f = pl.program_id(0)

    @pl.when(f == 0)
    def _():
        xn_ref[:tm, :] = _rms(x_ref[...], gpre_ref[...]).astype(BF16)
        xn_ref[tm:, :] = _rms(xs_ref[...], gpre_ref[...]).astype(BF16)
        o_ref[...] = jnp.zeros_like(o_ref)
        os_ref[...] = jnp.zeros_like(os_ref)

    wg, wl, wd = (r[...].astype(BF16) for r in (wg_ref, wl_ref, wd_ref))
    wgb_ref[...] = wg
    wlb_ref[...] = wl
    wdb_ref[...] = wd
    xn = xn_ref[...]
    hg = jnp.dot(xn, wg, preferred_element_type=F32)
    hl = jnp.dot(xn, wl, preferred_element_type=F32)
    a = (hg * jax.nn.sigmoid(hg) * hl).astype(BF16)
    part = jnp.dot(a, wd, preferred_element_type=F32)
    _accumulate(o_ref, ssq_ref, part[:tm, :])
    _accumulate(os_ref, ssqs_ref, part[tm:, :])

    @pl.when(f == pl.num_programs(0) - 1)
    def _():
        _residual_norm(x_ref, o_ref, ssq_ref, gpost_ref)
        _residual_norm(xs_ref, os_ref, ssqs_ref, gpost_ref)


def _ffn_head(x, xs, g_pre, g_post, w_up, w_down, layer, *, tm, tf):
    m = x.shape[0]
    ms = xs.shape[0]
    nf = D_FF // tf
    once = dict(pipeline_mode=pl.Buffered(1))
    head = pl.BlockSpec((tm, D_MODEL), lambda f: (0, 0), **once)
    sample = pl.BlockSpec((ms, D_MODEL), lambda f: (0, 0))
    vec = pl.BlockSpec((1, D_MODEL), lambda f: (0, 0))
    outs = pl.pallas_call(
        _ffn_head_kernel,
        grid=(nf,),
        in_specs=[
            head, sample, vec, vec,
            pl.BlockSpec((None, D_MODEL, tf), lambda f: (layer, 0, f)),
            pl.BlockSpec((None, D_MODEL, tf), lambda f: (layer, 0, nf + f)),
            pl.BlockSpec((None, tf, D_MODEL), lambda f: (layer, f, 0)),
        ],
        out_specs=[
            head, sample,
            pl.BlockSpec((D_MODEL, tf), lambda f: (0, f)),
            pl.BlockSpec((D_MODEL, tf), lambda f: (0, f)),
            pl.BlockSpec((tf, D_MODEL), lambda f: (f, 0)),
        ],
        out_shape=[
            jax.ShapeDtypeStruct((m, D_MODEL), F32),
            jax.ShapeDtypeStruct((ms, D_MODEL), F32),
            jax.ShapeDtypeStruct((D_MODEL, D_FF), BF16),
            jax.ShapeDtypeStruct((D_MODEL, D_FF), BF16),
            jax.ShapeDtypeStruct((D_FF, D_MODEL), BF16),
        ],
        input_output_aliases={0: 0},
        scratch_shapes=[pltpu.VMEM((tm + ms, D_MODEL), BF16), pltpu.VMEM((tm, HEAD_DIM), F32),
                        pltpu.VMEM((ms, HEAD_DIM), F32)],
        compiler_params=_params("arbitrary"),
        name="ffn_head",
    )(x, xs, g_pre.reshape(1, D_MODEL), g_post.reshape(1, D_MODEL), w_up, w_up, w_down)
    return outs[0], outs[1], tuple(outs[2:])


def _swa_kernel(q_ref, k_ref, v_ref, tb_ref, o_ref, lse_ref):
    n_units, n_res, u, _ = q_ref.shape
    per_blk = N_BACK // u
    n_blk = n_units // per_blk
    lane = lax.broadcasted_iota(jnp.int32, (N_BACK, HEAD_DIM), 1)
    ones = jnp.ones((2 * N_BACK, HEAD_DIM), BF16)

    def rows(ref, res, unit0, n_rows, lo, hi):
        return ref[pl.ds(unit0, n_rows // u), res, :, lo:hi].reshape(n_rows, hi - lo)

    def block(res, qu, ku, table):
        n_keys = N_BACK if table == 0 else 2 * N_BACK
        lse_tile = jnp.zeros((N_BACK, HEAD_DIM), F32)
        for h in range(HEADS_PER_GROUP):
            lo, hi = h * HEAD_DIM, (h + 1) * HEAD_DIM
            q = rows(q_ref, res, qu, N_BACK, lo, hi)
            kw = rows(k_ref, res, ku, n_keys, lo, hi)
            vw = rows(v_ref, res, ku, n_keys, lo, hi)
            s = lax.dot_general(q, kw, (((1,), (1,)), ((), ())), preferred_element_type=F32)
            s = s + tb_ref[table, h][:, :n_keys]
            m = jnp.max(s, axis=1, keepdims=True)
            p = jnp.exp(s - m).astype(BF16)
            ov = jnp.dot(p, jnp.concatenate([vw, ones[:n_keys]], axis=1), preferred_element_type=F32)
            den = ov[:, HEAD_DIM:]
            o = ov[:, :HEAD_DIM] / den
            o_ref[pl.ds(qu, per_blk), res, :, lo:hi] = o.reshape(per_blk, u, HEAD_DIM).astype(o_ref.dtype)
            lse_tile = jnp.where(lane == h, m + jnp.log(den), lse_tile)
        lse_ref[pl.ds(qu, per_blk), res, :, :] = lse_tile.reshape(per_blk, u, HEAD_DIM)

    def block_at(res, n):
        block(res, n * per_blk, (n - 1) * per_blk, 1)

    n_trips = (n_blk - 1) // SWA_BLOCKS_PER_TRIP
    for res in range(n_res):
        block(res, 0, 0, 0)

        def body(i, carry, res=res):
            for b in range(SWA_BLOCKS_PER_TRIP):
                block_at(res, 1 + SWA_BLOCKS_PER_TRIP * i + b)
            return carry

        if n_trips > 0:
            lax.fori_loop(0, n_trips, body, 0)
        for n in range(1 + n_trips * SWA_BLOCKS_PER_TRIP, n_blk):
            block_at(res, n)


def _sub_block(dil):
    return N_BACK if dil == 1 else PERM_BLOCK


def _swa_group(zb, tables, g, *, batch, seq):
    dil = SWA_PATTERN[g][1]
    sub = _sub_block(dil)
    n_units, u = seq // sub, sub // dil
    n_blk = n_units * u // N_BACK
    n_res = min(dil, max(1, SWA_BLOCKS_PER_TRIP // n_blk))
    view = zb.reshape(zb.shape[0], batch, n_units, dil, u, GROUP_WIDTH)

    def rows_in(tile):
        return pl.BlockSpec((None, None, n_units, n_res, u, GROUP_WIDTH), lambda i, r: (tile, i, 0, r, 0, 0))

    def rows_out(width):
        return pl.BlockSpec((None, n_units, n_res, u, width), lambda i, r: (i, 0, r, 0, 0))

    return pl.pallas_call(
        _swa_kernel,
        grid=(batch, dil // n_res),
        in_specs=[rows_in(g), rows_in(N_SWA_GROUPS + g), rows_in(2 * N_SWA_GROUPS + g),
                  pl.BlockSpec((None, 2, HEADS_PER_GROUP, N_BACK, 2 * N_BACK), lambda i, r: (g, 0, 0, 0, 0))],
        out_specs=[rows_out(GROUP_WIDTH), rows_out(HEAD_DIM)],
        out_shape=[
            jax.ShapeDtypeStruct((batch, n_units, dil, u, GROUP_WIDTH), BF16),
            jax.ShapeDtypeStruct((batch, n_units, dil, u, HEAD_DIM), F32),
        ],
        compiler_params=_params("parallel", "parallel"),
        name=f"swa_group{g}",
    )(view, view, view, tables)


def _split3(x):
    hi = x.astype(BF16)
    rest = x - hi.astype(F32)
    mid = rest.astype(BF16)
    lo = (rest - mid.astype(F32)).astype(BF16)
    return hi, mid, lo


def _merge_stage(o0_ref, o1_ref, o2_ref, l0_ref, l1_ref, l2_ref, q_ref, kb_ref, vb_ref, o_ref):
    tm = o_ref.shape[0]
    group_refs = ((o0_ref, l0_ref), (o1_ref, l1_ref), (o2_ref, l2_ref))
    outs, lses = [None] * N_SWA_GROUPS, [None] * N_SWA_GROUPS

    def token_order(g):
        o_g, l_g = group_refs[g]
        dil = SWA_PATTERN[g][1]
        o = o_g[...].reshape(tm, GROUP_WIDTH)
        l = l_g[...].reshape(tm, HEAD_DIM)
        if dil > 1:
            inv = _residue_major_perm(PERM_BLOCK, dil, transpose=True)
            l3 = _split3(l)
            o_nat, l_nat = [], []
            for s in range(0, tm, PERM_BLOCK):
                o_nat.append(jnp.dot(inv, o[s:s + PERM_BLOCK, :], preferred_element_type=F32))
                l_nat.append(sum(jnp.dot(inv, t[s:s + PERM_BLOCK, :], preferred_element_type=F32) for t in l3))
            o = jnp.concatenate(o_nat, axis=0)
            l = jnp.concatenate(l_nat, axis=0)
        outs[g] = o.astype(F32)
        lses[g] = l

    def merge_head(h):
        lo, hi = h * HEAD_DIM, (h + 1) * HEAD_DIM
        ls = [l[:, h:h + 1] for l in lses]
        mx = jnp.maximum(jnp.maximum(ls[0], ls[1]), ls[2])
        es = [jnp.exp(l - mx) for l in ls]
        tot = es[0] + es[1] + es[2]
        for g in range(N_SWA_GROUPS):
            alpha = es[g] / tot
            o_ref[:, g * GROUP_WIDTH + lo:g * GROUP_WIDTH + hi] = (outs[g][:, lo:hi] * alpha).astype(o_ref.dtype)

    return ([functools.partial(token_order, g) for g in range(N_SWA_GROUPS)]
            + [functools.partial(merge_head, h) for h in range(HEADS_PER_GROUP)]
            + _mem_attention_parts(q_ref, kb_ref, vb_ref, o_ref, MIXER_WIDTH))


def _swa_merge_out(outs, lses, zb, mem_kv, layer, w, x, g, *, rows_per_batch):
    tm = ROW_TILE
    tiles_per_batch = rows_per_batch // tm

    def group_tile(width, dil):
        sub = _sub_block(dil)
        return lambda tile: pl.BlockSpec(
            (None, tm // sub, dil, sub // dil, width),
            lambda s: (tile(s) // tiles_per_batch, tile(s) % tiles_per_batch, 0, 0, 0))

    specs = ([group_tile(GROUP_WIDTH, dil) for _, dil in SWA_PATTERN]
             + [group_tile(HEAD_DIM, dil) for _, dil in SWA_PATTERN]
             + [lambda tile: pl.BlockSpec((None, tm, MEM_WIDTH), lambda s: (3 * N_SWA_GROUPS, tile(s), 0))])
    return _mix_out(_merge_stage, (*outs, *lses, zb), specs, [], mem_kv, layer, w, x, g,
                    tm=tm, rows_per_batch=rows_per_batch, name="swa_merge_out")


def _head_rows(row, col0):
    return jnp.concatenate([row[:, col0 + h * HEAD_DIM:col0 + (h + 1) * HEAD_DIM]
                            for h in range(HEADS_PER_GROUP)], axis=0)


def _sample_attention(q4, kv_ref, bias=None, new=None):
    k3, v3 = kv_ref[:, 0], kv_ref[:, 1]
    s = jnp.sum(k3 * q4[None], axis=-1, keepdims=True) * ATTN_SCALE
    if bias is not None:
        s = s + bias
    m = jnp.max(s, axis=0)
    if new is not None:
        k_new, v_new, b_new = new
        s_new = jnp.sum(k_new * q4, axis=-1, keepdims=True) * ATTN_SCALE + b_new
        m = jnp.maximum(m, s_new)
    p = jnp.exp(s - m[None])
    den = jnp.sum(p, axis=0)
    o = jnp.sum(p * v3, axis=0)
    if new is not None:
        p_new = jnp.exp(s_new - m)
        den = den + p_new
        o = o + p_new * v_new
    return o / den, m + jnp.log(den)


def _store_head_rows(o_ref, col0, x4):
    for h in range(HEADS_PER_GROUP):
        o_ref[:, col0 + h * HEAD_DIM:col0 + (h + 1) * HEAD_DIM] = x4[h:h + 1, :]


def _sample_mem_attention(q_row, kv_ref, o_ref, col0):
    o, _ = _sample_attention(_head_rows(q_row, 0), kv_ref)
    _store_head_rows(o_ref, col0, o)


def _sample_mix_a_kernel(z_ref, gv_ref, w0_ref, b0_ref, kv_ref, o_ref, vrow_ref):
    u = z_ref[:, 0:MIXER_WIDTH]
    v = _rms(z_ref[:, MIXER_WIDTH:2 * MIXER_WIDTH], gv_ref[...])
    vrow_ref[...] = v
    o_ref[:, 0:MIXER_WIDTH] = u * (w0_ref[...] * v + b0_ref[...])
    _sample_mem_attention(z_ref[:, 2 * MIXER_WIDTH:2 * MIXER_WIDTH + MEM_WIDTH], kv_ref, o_ref, MIXER_WIDTH)


def _sample_mix_a(z, g_v, w_s, b_s, mem_kv, layer):
    bd = mem_kv.shape[1]
    w0 = jnp.repeat(w_s[:, 0, 0], GROUP_DIM_A).reshape(1, MIXER_WIDTH)
    b0 = jnp.repeat(b_s[:, 0], GROUP_DIM_A).reshape(1, MIXER_WIDTH)
    width = z.shape[1]
    vec = lambda i: (0, 0)
    return pl.pallas_call(
        _sample_mix_a_kernel,
        grid=(bd,),
        in_specs=[
            pl.BlockSpec((None, 1, width), lambda i: (i, 0, 0)),
            pl.BlockSpec((1, MIXER_WIDTH), vec),
            pl.BlockSpec((1, MIXER_WIDTH), vec),
            pl.BlockSpec((1, MIXER_WIDTH), vec),
            _mem_kv_spec(layer, lambda i: i),
        ],
        out_specs=[
            pl.BlockSpec((None, 1, D_MODEL), lambda i: (i, 0, 0)),
            pl.BlockSpec((None, 1, MIXER_WIDTH), lambda i: (i, 0, 0)),
        ],
        out_shape=[
            jax.ShapeDtypeStruct((bd, 1, D_MODEL), F32),
            jax.ShapeDtypeStruct((bd, 1, MIXER_WIDTH), F32),
        ],
        compiler_params=_params("parallel"),
        name="sample_mix_a",
    )(z[:bd].reshape(bd, 1, width), g_v.reshape(1, MIXER_WIDTH), w0, b0, mem_kv)


def _sample_mix_b_kernel(z_ref, c0_ref, c1_ref, c2_ref, bcol_ref, bnew_ref, kv_ref, o_ref):
    caches = (c0_ref, c1_ref, c2_ref)
    z = z_ref[...]
    outs, lses = [], []
    for g in range(N_SWA_GROUPS):
        c0 = g * GROUP_WIDTH
        new = (_head_rows(z, MIXER_WIDTH + c0), _head_rows(z, 2 * MIXER_WIDTH + c0), bnew_ref[g])
        o, lse = _sample_attention(_head_rows(z, c0), caches[g], bias=bcol_ref[g], new=new)
        outs.append(o)
        lses.append(lse)
    mx = jnp.maximum(jnp.maximum(lses[0], lses[1]), lses[2])
    es = [jnp.exp(l - mx) for l in lses]
    tot = es[0] + es[1] + es[2]
    for g in range(N_SWA_GROUPS):
        _store_head_rows(o_ref, g * GROUP_WIDTH, outs[g] * (es[g] / tot))
    _sample_mem_attention(z[:, 3 * MIXER_WIDTH:3 * MIXER_WIDTH + MEM_WIDTH], kv_ref, o_ref, MIXER_WIDTH)


def _sample_mix_b(z, win_caches, swa_layer, bias_groups, mem_kv, layer):
    bd = mem_kv.shape[1]
    width = z.shape[1]
    cache_views, cache_specs = [], []
    for g, (win, dil) in enumerate(SWA_PATTERN):
        c = win_caches[g]
        cache_views.append(c.reshape(c.shape[0], bd, win // dil, dil, 2, HEADS_PER_GROUP, HEAD_DIM))
        cache_specs.append(pl.BlockSpec((None, None, N_BACK, None, 2, HEADS_PER_GROUP, HEAD_DIM),
                                        lambda i: (swa_layer, i, 0, 0, 0, 0, 0)))
    bcol = jnp.stack([bg[:, N_BACK:0:-1].T for bg in bias_groups], axis=0)
    bcol = jnp.broadcast_to(bcol[..., None], bcol.shape + (HEAD_DIM,))
    bnew = jnp.stack([bg[:, 0] for bg in bias_groups], axis=0)
    bnew = jnp.broadcast_to(bnew[..., None], bnew.shape + (HEAD_DIM,))
    return pl.pallas_call(
        _sample_mix_b_kernel,
        grid=(bd,),
        in_specs=[pl.BlockSpec((None, 1, width), lambda i: (i, 0, 0))] + cache_specs + [
            pl.BlockSpec((N_SWA_GROUPS, N_BACK, HEADS_PER_GROUP, HEAD_DIM), lambda i: (0, 0, 0, 0)),
            pl.BlockSpec((N_SWA_GROUPS, HEADS_PER_GROUP, HEAD_DIM), lambda i: (0, 0, 0)),
            _mem_kv_spec(layer, lambda i: i),
        ],
        out_specs=pl.BlockSpec((None, 1, D_MODEL), lambda i: (i, 0, 0)),
        out_shape=jax.ShapeDtypeStruct((bd, 1, D_MODEL), F32),
        compiler_params=_params("parallel"),
        name="sample_mix_b",
    )(z[:bd].reshape(bd, 1, width), *cache_views, bcol, bnew, mem_kv)


def _t5_bucket(dist):
    nf = jnp.maximum(dist, MAX_EXACT).astype(F32)
    large = MAX_EXACT + (jnp.log(nf / MAX_EXACT) / math.log(MAX_DISTANCE / MAX_EXACT)
                         * (N_BUCKETS - MAX_EXACT)).astype(jnp.int32)
    large = jnp.minimum(large, N_BUCKETS - 1)
    return jnp.where(dist < MAX_EXACT, dist, large)


def _group_bias(rel_bias, g, dil):
    dist = jnp.arange(N_BACK + 1, dtype=jnp.int32) * dil
    b = rel_bias[_t5_bucket(dist)][:, g * HEADS_PER_GROUP:(g + 1) * HEADS_PER_GROUP]
    return b.T.astype(F32)


def _band_tables_kernel(b_ref, o_ref):
    n, rows, width = o_ref.shape
    for x in range(n):
        row = jnp.broadcast_to(b_ref[x:x + 1, :], (rows, width))
        o_ref[x] = pltpu.roll(row, 0, 1, stride=1, stride_axis=0)


def _band_tables(bias_groups):
    width = 2 * N_BACK
    rows = []
    for bias_j in bias_groups:
        masked = jnp.full((HEADS_PER_GROUP, N_BACK - 1), NEG_INF, F32)
        rows.append(jnp.concatenate([bias_j[:, :1], masked, bias_j[:, N_BACK:0:-1]], axis=1))
        rows.append(jnp.concatenate([bias_j[:, ::-1], masked], axis=1))
    base = jnp.stack(rows, axis=0).reshape(-1, width)
    tabs = pl.pallas_call(
        _band_tables_kernel,
        out_shape=jax.ShapeDtypeStruct((base.shape[0], N_BACK, width), F32),
        name="band_tables",
    )(base)
    return tabs.reshape(N_SWA_GROUPS, 2, HEADS_PER_GROUP, N_BACK, width)


def _kv_tail_kernel(k_ref, v_ref, o_ref, *, dil):
    rows = k_ref.shape[0]
    for kv, ref in enumerate((k_ref, v_ref)):
        x = ref[...]
        if dil > 1:
            inv = _residue_major_perm(PERM_BLOCK, dil, transpose=True)
            x = jnp.concatenate([jnp.dot(inv, x[s:s + PERM_BLOCK, :], preferred_element_type=F32)
                                 for s in range(0, rows, PERM_BLOCK)], axis=0)
        x = x.astype(F32)
        for h in range(HEADS_PER_GROUP):
            o_ref[:, kv, h, :] = x[:, h * HEAD_DIM:(h + 1) * HEAD_DIM]


def _kv_tail(zb, g, *, batch, seq):
    win, dil = SWA_PATTERN[g]
    rows = min(win, FFN_ROW_TILE)
    first = (seq - win) // rows
    per_batch = seq // rows

    def tile(t):
        return pl.BlockSpec((None, rows, GROUP_WIDTH), lambda b, s: (t, b * per_batch + first + s, 0))

    return pl.pallas_call(
        functools.partial(_kv_tail_kernel, dil=dil),
        grid=(batch, win // rows),
        in_specs=[tile(N_SWA_GROUPS + g), tile(2 * N_SWA_GROUPS + g)],
        out_specs=pl.BlockSpec((None, rows, 2, HEADS_PER_GROUP, HEAD_DIM), lambda b, s: (b, s, 0, 0, 0)),
        out_shape=jax.ShapeDtypeStruct((batch, win, 2, HEADS_PER_GROUP, HEAD_DIM), F32),
        compiler_params=_params("parallel", "parallel"),
        name=f"kv_tail{g}",
    )(zb, zb)


def kernel(x_prompt, x_sample, mem_prompt, cache_mem_kv, cache_win128_kv, cache_win512_kv, cache_win2048_kv, rel_bias, norm_mix_pre, norm_mix_post, norm_ffn_pre, norm_ffn_post, norm_mem, w_mem_kv, w_in_a, norm_v_a, w_spatial_a, b_spatial_a, w_in_b, w_out, w_ffn_up, w_ffn_down):
    batch, seq, _ = x_prompt.shape
    bd = x_sample.shape[0]
    depth = w_out.shape[0]
    m_p = batch * seq
    win_caches = (cache_win128_kv, cache_win512_kv, cache_win2048_kv)

    bias_groups = [_group_bias(rel_bias, g, dil) for g, (_, dil) in enumerate(SWA_PATTERN)]
    band_tables = _band_tables(bias_groups)

    yp = x_prompt.reshape(m_p, D_MODEL)
    ys = jnp.pad(x_sample.reshape(bd, D_MODEL), ((0, SAMPLE_PAD - bd), (0, 0)))
    mem_rows = mem_prompt.reshape(batch * N_MEM, D_MODEL)

    mem_kv_p = _mem_kv(mem_rows, norm_mem, w_mem_kv, batch=batch)
    chunk_v_s = []
    win_p = [[] for _ in SWA_PATTERN]
    win_s = [[] for _ in SWA_PATTERN]
    for i in range(depth):
        li = i // 2
        if i % 2 == 0:
            zp, zs = _in_proj_a(yp, ys, norm_mix_pre[i], w_in_a, li, tm=ROW_TILE, tn=CAST_TILE,
                                gelu_cols=2 * MIXER_WIDTH)
            mix_s, v_rows = _sample_mix_a(zs, norm_v_a[li], w_spatial_a[li], b_spatial_a[li], cache_mem_kv, i)
            chunk_v_s.append(v_rows)
            ys, w_o = _sample_out_proj(mix_s, w_out, i, ys, norm_mix_post[i])
            yp = _gmlp_mix_out(zp, norm_v_a[li], w_spatial_a[li], b_spatial_a[li], mem_kv_p, i,
                               w_o, yp, norm_mix_post[i], tm=ROW_TILE, rows_per_batch=seq)
        else:
            zb, zs = _in_proj_b(yp, ys, norm_mix_pre[i], w_in_b, li, tm=ROW_TILE)
            outs, lses = [], []
            for g, (win, dil) in enumerate(SWA_PATTERN):
                o, lse = _swa_group(zb, band_tables, g, batch=batch, seq=seq)
                outs.append(o)
                lses.append(lse)
                win_p[g].append(_kv_tail(zb, g, batch=batch, seq=seq))
                kv_new = zs[:bd, MIXER_WIDTH:3 * MIXER_WIDTH]
                kv_new = kv_new.reshape(bd, 1, 2, N_SWA_GROUPS, HEADS_PER_GROUP, HEAD_DIM)[:, :, :, g]
                win_s[g].append(kv_new)
            mix_s = _sample_mix_b(zs, win_caches, li, bias_groups, cache_mem_kv, i)
            ys, w_o = _sample_out_proj(mix_s, w_out, i, ys, norm_mix_post[i])
            yp = _swa_merge_out(outs, lses, zb, mem_kv_p, i, w_o, yp, norm_mix_post[i], rows_per_batch=seq)
        yp, ys, w_ffn = _ffn_head(yp, ys, norm_ffn_pre[i], norm_ffn_post[i], w_ffn_up, w_ffn_down, i,
                                  tm=FFN_ROW_TILE, tf=HEAD_FF_TILE)
        yp = _ffn(yp, norm_ffn_pre[i], norm_ffn_post[i], w_ffn, tm=FFN_ROW_TILE, tf=FF_TILE, first_tile=1)

    return (
        yp.reshape(batch, seq, D_MODEL),
        ys[:bd].reshape(bd, 1, D_MODEL),
        mem_kv_p,
        jnp.stack(chunk_v_s, axis=0),
        jnp.stack(win_p[0], axis=0),
        jnp.stack(win_p[1], axis=0),
        jnp.stack(win_p[2], axis=0),
        jnp.stack(win_s[0], axis=0),
        jnp.stack(win_s[1], axis=0),
        jnp.stack(win_s[2], axis=0),
    )
```

```python
import functools
import math

import jax
import jax.numpy as jnp
from jax import lax
from jax.experimental import pallas as pl
from jax.experimental.pallas import tpu as pltpu

F32 = jnp.float32
BF16 = jnp.bfloat16

D_MODEL = 2048
HEAD_DIM = 128
N_MEM = 256
N_MEM_HEADS = 4
MEM_WIDTH = N_MEM_HEADS * HEAD_DIM
MIXER_WIDTH = D_MODEL - MEM_WIDTH
CHUNK = 128
N_GROUPS_A = 4
GROUP_DIM_A = MIXER_WIDTH // N_GROUPS_A
SWA_PATTERN = ((128, 1), (512, 4), (2048, 16))
N_SWA_GROUPS = len(SWA_PATTERN)
HEADS_PER_GROUP = 4
GROUP_WIDTH = HEADS_PER_GROUP * HEAD_DIM
N_BACK = 128
N_BUCKETS = 32
MAX_EXACT = N_BUCKETS // 2
MAX_DISTANCE = 2048
D_FF = 5632
EPS = 1e-6
NEG_INF = -1e30
ATTN_SCALE = HEAD_DIM ** -0.5
SAMPLE_PAD = 16
PERM_BLOCK = 256
SWA_BLOCKS_PER_TRIP = 16
OUT_CHUNK = 256

ROW_TILE = 512
FFN_ROW_TILE = 1024
FF_TILE = 512
HEAD_FF_TILE = 256
CAST_TILE = 512

V7X_VMEM_BYTES = 64 * 1024 * 1024
VMEM_LIMIT = V7X_VMEM_BYTES - 8 * 1024 * 1024
FFN_VMEM_LIMIT = V7X_VMEM_BYTES - 2 * 1024 * 1024


def _params(*sem, vmem_limit=VMEM_LIMIT):
    return pltpu.CompilerParams(dimension_semantics=sem, vmem_limit_bytes=vmem_limit)


def _gelu(x):
    return 0.5 * x * (1.0 + jnp.tanh(0.7978845608028654 * (x + 0.044715 * (x * x * x))))


def _rms(x, g):
    return x * lax.rsqrt(jnp.mean(x * x, axis=-1, keepdims=True) + EPS) * g


def _log2(n):
    assert n & (n - 1) == 0
    return n.bit_length() - 1


def _residue_major_perm(tm, dil, transpose=False):
    n = tm // dil
    row = lax.broadcasted_iota(jnp.int32, (tm, tm), 0)
    col = lax.broadcasted_iota(jnp.int32, (tm, tm), 1)
    dst, src = (col, row) if transpose else (row, col)
    want = lax.shift_left(jnp.bitwise_and(dst, n - 1), _log2(dil)) + lax.shift_right_logical(dst, _log2(n))
    return (src == want).astype(BF16)


def _in_proj_a_kernel(x_ref, xs_ref, g_ref, w_ref, zp_ref, zs_ref, wb_ref, xn_ref, *, n_col, gelu_cols):
    tm = x_ref.shape[0]
    tn = w_ref.shape[1]
    s = pl.program_id(0)

    def prompt_cols(acc, c0):
        col = lax.broadcasted_iota(jnp.int32, acc.shape, 1) + c0
        return jnp.where(col < gelu_cols, _gelu(acc), acc * ATTN_SCALE).astype(zp_ref.dtype)

    @pl.when(s == 0)
    def _():
        xn_ref[:tm, :] = _rms(x_ref[...], g_ref[...]).astype(BF16)
        xn_ref[tm:, :] = _rms(xs_ref[...], g_ref[...]).astype(BF16)

    for j in range(n_col):
        @pl.when(s == j)
        def _(j=j):
            cols = slice(j * tn, (j + 1) * tn)
            w = w_ref[...].astype(BF16)
            wb_ref[:, cols] = w
            acc = jnp.dot(xn_ref[...], w, preferred_element_type=F32)
            zp_ref[:, cols] = prompt_cols(acc[:tm, :], j * tn)
            sample = acc[tm:, :]
            zs_ref[:, cols] = _gelu(sample) if (j + 1) * tn <= gelu_cols else sample

    @pl.when(s >= n_col)
    def _():
        xn = _rms(x_ref[...], g_ref[...]).astype(BF16)
        acc = jnp.dot(xn, wb_ref[...], preferred_element_type=F32)
        zp_ref[...] = prompt_cols(acc, 0)


def _in_proj_a(x, xs, g, w, layer, *, tm, tn, gelu_cols):
    m, k = x.shape
    ms = xs.shape[0]
    n = w.shape[2]
    n_col, n_row = n // tn, m // tm
    assert gelu_cols % tn == 0
    row = lambda s: jnp.maximum(s - (n_col - 1), 0)
    return pl.pallas_call(
        functools.partial(_in_proj_a_kernel, n_col=n_col, gelu_cols=gelu_cols),
        grid=(n_col + n_row - 1,),
        in_specs=[
            pl.BlockSpec((tm, k), lambda s: (row(s), 0)),
            pl.BlockSpec((ms, k), lambda s: (0, 0)),
            pl.BlockSpec((1, k), lambda s: (0, 0)),
            pl.BlockSpec((None, k, tn), lambda s: (layer, 0, jnp.minimum(s, n_col - 1))),
        ],
        out_specs=[
            pl.BlockSpec((tm, n), lambda s: (row(s), 0)),
            pl.BlockSpec((ms, n), lambda s: (0, 0)),
        ],
        out_shape=[
            jax.ShapeDtypeStruct((m, n), BF16),
            jax.ShapeDtypeStruct((ms, n), F32),
        ],
        scratch_shapes=[pltpu.VMEM((k, n), BF16), pltpu.VMEM((tm + ms, k), BF16)],
        compiler_params=_params("arbitrary"),
        name="in_proj_a",
    )(x, xs, g.reshape(1, k), w)


def _mem_kv_kernel(x_ref, g_ref, w_ref, o_ref):
    xn = _rms(x_ref[...], g_ref[...]).astype(BF16)
    acc = jnp.dot(xn, w_ref[...].astype(BF16), preferred_element_type=F32)
    for kv in range(2):
        for h in range(N_MEM_HEADS):
            c0 = (kv * N_MEM_HEADS + h) * HEAD_DIM
            o_ref[:, kv, h, :] = acc[:, c0:c0 + HEAD_DIM]


def _mem_kv(mem_rows, g, w, *, batch):
    m, k = mem_rows.shape
    layers = w.shape[0]
    out = pl.pallas_call(
        _mem_kv_kernel,
        grid=(layers,),
        in_specs=[
            pl.BlockSpec((m, k), lambda l: (0, 0)),
            pl.BlockSpec((None, 1, k), lambda l: (l, 0, 0)),
            pl.BlockSpec((None, k, 2 * MEM_WIDTH), lambda l: (l, 0, 0)),
        ],
        out_specs=pl.BlockSpec((None, m, 2, N_MEM_HEADS, HEAD_DIM), lambda l: (l, 0, 0, 0, 0)),
        out_shape=jax.ShapeDtypeStruct((layers, m, 2, N_MEM_HEADS, HEAD_DIM), F32),
        compiler_params=_params("parallel"),
        name="mem_kv",
    )(mem_rows, g.reshape(layers, 1, k), w)
    return out.reshape(layers, batch, m // batch, 2, N_MEM_HEADS, HEAD_DIM)


def _in_proj_b_kernel(x_ref, xs_ref, g_ref, w_ref, o_ref, zs_ref, wb_ref, xn_ref):
    tm = x_ref.shape[0]
    n_tiles = o_ref.shape[0]
    n_qkv = 3 * N_SWA_GROUPS
    s = pl.program_id(0)

    def row_orders():
        xn = _rms(x_ref[...], g_ref[...]).astype(BF16)
        xn_ref[0, :tm, :] = xn
        for g in range(1, N_SWA_GROUPS):
            perm = _residue_major_perm(PERM_BLOCK, SWA_PATTERN[g][1])
            for r in range(0, tm, PERM_BLOCK):
                xn_ref[g, r:r + PERM_BLOCK, :] = jnp.dot(
                    perm, xn[r:r + PERM_BLOCK, :], preferred_element_type=F32).astype(BF16)

    def prompt_tile(t, acc):
        if t < N_SWA_GROUPS or t == n_qkv:
            acc = acc * ATTN_SCALE
        o_ref[t] = acc.astype(o_ref.dtype)

    src = lambda t: t % N_SWA_GROUPS if t < n_qkv else 0

    @pl.when(s == 0)
    def _():
        row_orders()
        xs = _rms(xs_ref[...], g_ref[...]).astype(BF16)
        for g in range(N_SWA_GROUPS):
            xn_ref[g, tm:, :] = xs

    for t in range(n_tiles):
        @pl.when(s == t)
        def _(t=t):
            cols = slice(t * GROUP_WIDTH, (t + 1) * GROUP_WIDTH)
            w = w_ref[...].astype(BF16)
            wb_ref[:, cols] = w
            acc = jnp.dot(xn_ref[src(t)], w, preferred_element_type=F32)
            prompt_tile(t, acc[:tm, :])
            zs_ref[:, cols] = acc[tm:, :]

    @pl.when(s >= n_tiles)
    def _():
        row_orders()
        for t in range(n_tiles):
            w = wb_ref[:, t * GROUP_WIDTH:(t + 1) * GROUP_WIDTH]
            prompt_tile(t, jnp.dot(xn_ref[src(t), :tm, :], w, preferred_element_type=F32))


def _in_proj_b(x, xs, g, w, layer, *, tm):
    m, k = x.shape
    ms = xs.shape[0]
    n = w.shape[2]
    n_tiles = n // GROUP_WIDTH
    row = lambda s: jnp.maximum(s - (n_tiles - 1), 0)
    return pl.pallas_call(
        _in_proj_b_kernel,
        grid=(n_tiles + m // tm - 1,),
        in_specs=[
            pl.BlockSpec((tm, k), lambda s: (row(s), 0)),
            pl.BlockSpec((ms, k), lambda s: (0, 0)),
            pl.BlockSpec((1, k), lambda s: (0, 0)),
            pl.BlockSpec((None, k, GROUP_WIDTH), lambda s: (layer, 0, jnp.minimum(s, n_tiles - 1))),
        ],
        out_specs=[
            pl.BlockSpec((n_tiles, tm, GROUP_WIDTH), lambda s: (0, row(s), 0)),
            pl.BlockSpec((ms, n), lambda s: (0, 0)),
        ],
        out_shape=[
            jax.ShapeDtypeStruct((n_tiles, m, GROUP_WIDTH), BF16),
            jax.ShapeDtypeStruct((ms, n), F32),
        ],
        scratch_shapes=[pltpu.VMEM((k, n), BF16), pltpu.VMEM((N_SWA_GROUPS, tm + ms, k), BF16)],
        compiler_params=_params("arbitrary"),
        name="in_proj_b",
    )(x, xs, g.reshape(1, k), w)


MEM_KV_SCRATCH = [pltpu.VMEM((N_MEM_HEADS, N_MEM, HEAD_DIM), BF16),
                  pltpu.VMEM((N_MEM_HEADS, N_MEM, 2 * HEAD_DIM), BF16)]


def _prepare_mem_kv(kv_ref, kb_ref, vb_ref):
    for h in range(N_MEM_HEADS):
        kb_ref[h] = kv_ref[:, 0, h, :].astype(BF16)
        vb_ref[h, :, :HEAD_DIM] = kv_ref[:, 1, h, :].astype(BF16)
        vb_ref[h, :, HEAD_DIM:] = jnp.ones((N_MEM, HEAD_DIM), BF16)


def _mem_attention_parts(q_ref, kb_ref, vb_ref, o_ref, col0):
    def head(h):
        lo, hi = h * HEAD_DIM, (h + 1) * HEAD_DIM
        s = lax.dot_general(q_ref[:, lo:hi], kb_ref[h], (((1,), (1,)), ((), ())), preferred_element_type=F32)
        m = jnp.max(s, axis=1, keepdims=True)
        p = jnp.exp(s - m).astype(BF16)
        ov = jnp.dot(p, vb_ref[h], preferred_element_type=F32)
        o_ref[:, col0 + lo:col0 + hi] = (ov[:, :HEAD_DIM] / ov[:, HEAD_DIM:]).astype(o_ref.dtype)

    return [functools.partial(head, h) for h in range(N_MEM_HEADS)]


def _gmlp_stage(u_ref, v_ref, q_ref, gv_ref, ws_ref, bs_ref, kb_ref, vb_ref, vn_ref, o_ref):
    tm = u_ref.shape[0]

    def norm_v():
        vn_ref[...] = _rms(v_ref[...].astype(F32), gv_ref[...]).astype(BF16)

    def group(g):
        row = lax.broadcasted_iota(jnp.int32, (CHUNK, CHUNK), 0)
        col = lax.broadcasted_iota(jnp.int32, (CHUNK, CHUNK), 1)
        w = jnp.where(row >= col, ws_ref[g], 0.0).astype(BF16)
        b = bs_ref[:, g:g + 1]
        c0, c1 = g * GROUP_DIM_A, (g + 1) * GROUP_DIM_A
        for c in range(tm // CHUNK):
            r0, r1 = c * CHUNK, (c + 1) * CHUNK
            s = jnp.dot(w, vn_ref[r0:r1, c0:c1], preferred_element_type=F32) + b
            o_ref[r0:r1, c0:c1] = (u_ref[r0:r1, c0:c1].astype(F32) * s).astype(o_ref.dtype)

    return ([norm_v] + [functools.partial(group, g) for g in range(N_GROUPS_A)]
            + _mem_attention_parts(q_ref, kb_ref, vb_ref, o_ref, MIXER_WIDTH))


def _mem_kv_spec(layer, batch_of):
    return pl.BlockSpec((None, None, N_MEM, 2, N_MEM_HEADS, HEAD_DIM),
                        lambda i: (layer, batch_of(i), 0, 0, 0, 0))


def _mix_out_kernel(*refs, stage, n_in, n_tiles, tiles_per_batch):
    mix_in = refs[:n_in]
    kv_ref, w_ref, x_ref, g_ref, o_ref, buf0_ref, buf1_ref, acc_ref, kb_ref, vb_ref = refs[n_in:n_in + 10]
    extra = refs[n_in + 10:]
    s = pl.program_id(0)

    @pl.when(s == 0)
    def _():
        buf1_ref[...] = jnp.zeros_like(buf1_ref)

    @pl.when(jnp.minimum(s, n_tiles - 1) % tiles_per_batch == 0)
    def _():
        _prepare_mem_kv(kv_ref, kb_ref, vb_ref)

    def step(dst_ref, src_ref):
        parts = stage(*mix_in, kb_ref, vb_ref, *extra, dst_ref)
        n_chunks = D_MODEL // OUT_CHUNK
        for c in range(n_chunks):
            cols = slice(c * OUT_CHUNK, (c + 1) * OUT_CHUNK)
            acc_ref[:, cols] = jnp.dot(src_ref[...], w_ref[:, cols], preferred_element_type=F32)
            for part in parts[c * len(parts) // n_chunks:(c + 1) * len(parts) // n_chunks]:
                part()
        o_ref[...] = x_ref[...] + _rms(acc_ref[...], g_ref[...])

    @pl.when(s % 2 == 0)
    def _():
        step(buf0_ref, buf1_ref)

    @pl.when(s % 2 == 1)
    def _():
        step(buf1_ref, buf0_ref)


def _mix_out(stage, mix_inputs, mix_specs, extra_scratch, mem_kv, layer, w, x, g, *, tm, rows_per_batch, name):
    m = x.shape[0]
    n_tiles = m // tm
    tiles_per_batch = rows_per_batch // tm
    mix_tile = lambda s: jnp.minimum(s, n_tiles - 1)
    out_tile = lambda s: jnp.maximum(s - 1, 0)
    row_spec = pl.BlockSpec((tm, D_MODEL), lambda s: (out_tile(s), 0))
    return pl.pallas_call(
        functools.partial(_mix_out_kernel, stage=stage, n_in=len(mix_inputs), n_tiles=n_tiles,
                          tiles_per_batch=tiles_per_batch),
        grid=(n_tiles + 1,),
        in_specs=[spec(mix_tile) for spec in mix_specs] + [
            _mem_kv_spec(layer, lambda s: mix_tile(s) // tiles_per_batch),
            pl.BlockSpec((D_MODEL, D_MODEL), lambda s: (0, 0), pipeline_mode=pl.Buffered(1)),
            row_spec,
            pl.BlockSpec((1, D_MODEL), lambda s: (0, 0)),
        ],
        out_specs=row_spec,
        out_shape=jax.ShapeDtypeStruct((m, D_MODEL), F32),
        scratch_shapes=[pltpu.VMEM((tm, D_MODEL), BF16), pltpu.VMEM((tm, D_MODEL), BF16),
                        pltpu.VMEM((tm, D_MODEL), F32)] + MEM_KV_SCRATCH + list(extra_scratch),
        compiler_params=_params("arbitrary"),
        name=name,
    )(*mix_inputs, mem_kv, w, x, g.reshape(1, D_MODEL))


def _gmlp_mix_out(zact, g_v, w_s, b_s, mem_kv, layer, w, x, g, *, tm, rows_per_batch):
    const = lambda shape: (lambda tile: pl.BlockSpec(shape, lambda s: (0,) * len(shape)))
    specs = [
        lambda tile: pl.BlockSpec((tm, MIXER_WIDTH), lambda s: (tile(s), 0)),
        lambda tile: pl.BlockSpec((tm, MIXER_WIDTH), lambda s: (tile(s), 1)),
        lambda tile: pl.BlockSpec((tm, MEM_WIDTH), lambda s: (tile(s), 2 * MIXER_WIDTH // MEM_WIDTH)),
        const((1, MIXER_WIDTH)),
        const((N_GROUPS_A, CHUNK, CHUNK)),
        const((CHUNK, N_GROUPS_A)),
    ]
    return _mix_out(_gmlp_stage, (zact, zact, zact, g_v.reshape(1, MIXER_WIDTH), w_s, b_s.T), specs,
                    [pltpu.VMEM((tm, MIXER_WIDTH), BF16)], mem_kv, layer, w, x, g,
                    tm=tm, rows_per_batch=rows_per_batch, name="gmlp_mix_out")


def _out_proj_cast_kernel(mix_ref, w_ref, x_ref, g_ref, o_ref, wb_ref, acc_ref):
    j = pl.program_id(0)
    w = w_ref[...].astype(BF16)
    wb_ref[...] = w
    acc_ref[j] = jnp.dot(mix_ref[...], w, preferred_element_type=F32)

    @pl.when(j == pl.num_programs(0) - 1)
    def _():
        o = jnp.concatenate([acc_ref[t] for t in range(acc_ref.shape[0])], axis=1)
        o_ref[...] = x_ref[...] + _rms(o, g_ref[...])


def _sample_out_proj(mix, w, layer, x, g):
    bd = mix.shape[0]
    mix = jnp.pad(mix.reshape(bd, D_MODEL), ((0, x.shape[0] - bd), (0, 0))).astype(BF16)
    return _out_proj_cast(mix, w, layer, x, g, tn=CAST_TILE)


def _out_proj_cast(mix, w, layer, x, g, *, tn):
    m = x.shape[0]
    n_tiles = D_MODEL // tn
    return pl.pallas_call(
        _out_proj_cast_kernel,
        grid=(n_tiles,),
        in_specs=[
            pl.BlockSpec((m, D_MODEL), lambda j: (0, 0)),
            pl.BlockSpec((None, D_MODEL, tn), lambda j: (layer, 0, j)),
            pl.BlockSpec((m, D_MODEL), lambda j: (0, 0)),
            pl.BlockSpec((1, D_MODEL), lambda j: (0, 0)),
        ],
        out_specs=[
            pl.BlockSpec((m, D_MODEL), lambda j: (0, 0)),
            pl.BlockSpec((D_MODEL, tn), lambda j: (0, j)),
        ],
        out_shape=[
            jax.ShapeDtypeStruct((m, D_MODEL), F32),
            jax.ShapeDtypeStruct((D_MODEL, D_MODEL), BF16),
        ],
        scratch_shapes=[pltpu.VMEM((n_tiles, m, tn), F32)],
        compiler_params=_params("arbitrary"),
        name="out_proj_cast",
    )(mix, w, x, g.reshape(1, D_MODEL))


def _accumulate(o_ref, ssq_ref, part):
    new = o_ref[...] + part
    o_ref[...] = new
    sq = new * new
    ssq_ref[...] = functools.reduce(
        jnp.add, [sq[:, c:c + HEAD_DIM] for c in range(0, sq.shape[1], HEAD_DIM)])


def _residual_norm(x_ref, o_ref, ssq_ref, g_ref):
    ms = jnp.sum(ssq_ref[...], axis=-1, keepdims=True) * (1.0 / o_ref.shape[1])
    o_ref[...] = x_ref[...] + o_ref[...] * lax.rsqrt(ms + EPS) * g_ref[...]


def _ffn_kernel(x_ref, gpre_ref, gpost_ref, wg_ref, wl_ref, wd_ref, o_ref, xn_ref, ssq_ref):
    f = pl.program_id(1)

    @pl.when(f == 0)
    def _():
        xn_ref[...] = _rms(x_ref[...], gpre_ref[...]).astype(BF16)
        o_ref[...] = jnp.zeros_like(o_ref)

    xn = xn_ref[...]
    hg = jnp.dot(xn, wg_ref[...], preferred_element_type=F32)
    hl = jnp.dot(xn, wl_ref[...], preferred_element_type=F32)
    a = (hg * jax.nn.sigmoid(hg) * hl).astype(BF16)
    _accumulate(o_ref, ssq_ref, jnp.dot(a, wd_ref[...], preferred_element_type=F32))

    @pl.when(f == pl.num_programs(1) - 1)
    def _():
        _residual_norm(x_ref, o_ref, ssq_ref, gpost_ref)


def _ffn(x, g_pre, g_post, weights, *, tm, tf, first_tile):
    m = x.shape[0]
    nf = D_FF // tf
    rows = pl.BlockSpec((tm, D_MODEL), lambda i, f: (i + first_tile, 0))
    return pl.pallas_call(
        _ffn_kernel,
        grid=(m // tm - first_tile, nf),
        in_specs=[
            rows,
            pl.BlockSpec((1, D_MODEL), lambda i, f: (0, 0)),
            pl.BlockSpec((1, D_MODEL), lambda i, f: (0, 0)),
            pl.BlockSpec((D_MODEL, tf), lambda i, f: (0, f)),
            pl.BlockSpec((D_MODEL, tf), lambda i, f: (0, f)),
            pl.BlockSpec((tf, D_MODEL), lambda i, f: (f, 0)),
        ],
        out_specs=rows,
        out_shape=jax.ShapeDtypeStruct((m, D_MODEL), F32),
        input_output_aliases={0: 0},
        scratch_shapes=[pltpu.VMEM((tm, D_MODEL), BF16), pltpu.VMEM((tm, HEAD_DIM), F32)],
        compiler_params=_params("parallel", "arbitrary", vmem_limit=FFN_VMEM_LIMIT),
        name="ffn",
    )(x, g_pre.reshape(1, D_MODEL), g_post.reshape(1, D_MODEL), *weights)


def _ffn_head_kernel(x_ref, xs_ref, gpre_ref, gpost_ref, wg_ref, wl_ref, wd_ref,
                     o_ref, os_ref, wgb_ref, wlb_ref, wdb_ref, xn_ref, ssq_ref, ssqs_ref):
    tm = x_ref.shape[0]
    f = pl.program_id(0)

    @pl.when(f == 0)
    def _():
        xn_ref[:tm, :] = _rms(x_ref[...], gpre_ref[...]).astype(BF16)
        xn_ref[tm:, :] = _rms(xs_ref[...], gpre_ref[...]).astype(BF16)
        o_ref[...] = jnp.zeros_like(o_ref)
        os_ref[...] = jnp.zeros_like(os_ref)

    wg, wl, wd = (r[...].astype(BF16) for r in (wg_ref, wl_ref, wd_ref))
    wgb_ref[...] = wg
    wlb_ref[...] = wl
    wdb_ref[...] = wd
    xn = xn_ref[...]
    hg = jnp.dot(xn, wg, preferred_element_type=F32)
    hl = jnp.dot(xn, wl, preferred_element_type=F32)
    a = (hg * jax.nn.sigmoid(hg) * hl).astype(BF16)
    part = jnp.dot(a, wd, preferred_element_type=F32)
    _accumulate(o_ref, ssq_ref, part[:tm, :])
    _accumulate(os_ref, ssqs_ref, part[tm:, :])

    @pl.when(f == pl.num_programs(0) - 1)
    def _():
        _residual_norm(x_ref, o_ref, ssq_ref, gpost_ref)
        _residual_norm(xs_ref, os_ref, ssqs_ref, gpost_ref)


def _ffn_head(x, xs, g_pre, g_post, w_up, w_down, layer, *, tm, tf):
    m = x.shape[0]
    ms = xs.shape[0]
    nf = D_FF // tf
    once = dict(pipeline_mode=pl.Buffered(1))
    head = pl.BlockSpec((tm, D_MODEL), lambda f: (0, 0), **once)
    sample = pl.BlockSpec((ms, D_MODEL), lambda f: (0, 0))
    vec = pl.BlockSpec((1, D_MODEL), lambda f: (0, 0))
    outs = pl.pallas_call(
        _ffn_head_kernel,
        grid=(nf,),
        in_specs=[
            head, sample, vec, vec,
            pl.BlockSpec((None, D_MODEL, tf), lambda f: (layer, 0, f)),
            pl.BlockSpec((None, D_MODEL, tf), lambda f: (layer, 0, nf + f)),
            pl.BlockSpec((None, tf, D_MODEL), lambda f: (layer, f, 0)),
        ],
        out_specs=[
            head, sample,
            pl.BlockSpec((D_MODEL, tf), lambda f: (0, f)),
            pl.BlockSpec((D_MODEL, tf), lambda f: (0, f)),
            pl.BlockSpec((tf, D_MODEL), lambda f: (f, 0)),
        ],
        out_shape=[
            jax.ShapeDtypeStruct((m, D_MODEL), F32),
            jax.ShapeDtypeStruct((ms, D_MODEL), F32),
            jax.ShapeDtypeStruct((D_MODEL, D_FF), BF16),
            jax.ShapeDtypeStruct((D_MODEL, D_FF), BF16),
            jax.ShapeDtypeStruct((D_FF, D_MODEL), BF16),
        ],
        input_output_aliases={0: 0},
        scratch_shapes=[pltpu.VMEM((tm + ms, D_MODEL), BF16), pltpu.VMEM((tm, HEAD_DIM), F32),
                        pltpu.VMEM((ms, HEAD_DIM), F32)],
        compiler_params=_params("arbitrary"),
        name="ffn_head",
    )(x, xs, g_pre.reshape(1, D_MODEL), g_post.reshape(1, D_MODEL), w_up, w_up, w_down)
    return outs[0], outs[1], tuple(outs[2:])


def _swa_kernel(q_ref, k_ref, v_ref, tb_ref, o_ref, lse_ref):
    n_units, n_res, u, _ = q_ref.shape
    per_blk = N_BACK // u
    n_blk = n_units // per_blk
    lane = lax.broadcasted_iota(jnp.int32, (N_BACK, HEAD_DIM), 1)
    ones = jnp.ones((2 * N_BACK, HEAD_DIM), BF16)

    def rows(ref, res, unit0, n_rows, lo, hi):
        return ref[pl.ds(unit0, n_rows // u), res, :, lo:hi].reshape(n_rows, hi - lo)

    def block(res, qu, ku, table):
        n_keys = N_BACK if table == 0 else 2 * N_BACK
        lse_tile = jnp.zeros((N_BACK, HEAD_DIM), F32)
        for h in range(HEADS_PER_GROUP):
            lo, hi = h * HEAD_DIM, (h + 1) * HEAD_DIM
            q = rows(q_ref, res, qu, N_BACK, lo, hi)
            kw = rows(k_ref, res, ku, n_keys, lo, hi)
            vw = rows(v_ref, res, ku, n_keys, lo, hi)
            s = lax.dot_general(q, kw, (((1,), (1,)), ((), ())), preferred_element_type=F32)
            s = s + tb_ref[table, h][:, :n_keys]
            m = jnp.max(s, axis=1, keepdims=True)
            p = jnp.exp(s - m).astype(BF16)
            ov = jnp.dot(p, jnp.concatenate([vw, ones[:n_keys]], axis=1), preferred_element_type=F32)
            den = ov[:, HEAD_DIM:]
            o = ov[:, :HEAD_DIM] / den
            o_ref[pl.ds(qu, per_blk), res, :, lo:hi] = o.reshape(per_blk, u, HEAD_DIM).astype(o_ref.dtype)
            lse_tile = jnp.where(lane == h, m, jnp.where(lane == HEADS_PER_GROUP + h, den, lse_tile))
        lse_ref[pl.ds(qu, per_blk), res, :, :] = lse_tile.reshape(per_blk, u, HEAD_DIM)

    def block_at(res, n):
        block(res, n * per_blk, (n - 1) * per_blk, 1)

    n_trips = (n_blk - 1) // SWA_BLOCKS_PER_TRIP
    for res in range(n_res):
        block(res, 0, 0, 0)

        def body(i, carry, res=res):
            for b in range(SWA_BLOCKS_PER_TRIP):
                block_at(res, 1 + SWA_BLOCKS_PER_TRIP * i + b)
            return carry

        if n_trips > 0:
            lax.fori_loop(0, n_trips, body, 0)
        for n in range(1 + n_trips * SWA_BLOCKS_PER_TRIP, n_blk):
            block_at(res, n)


def _sub_block(dil):
    return N_BACK if dil == 1 else PERM_BLOCK


def _swa_group(zb, tables, g, *, batch, seq):
    dil = SWA_PATTERN[g][1]
    sub = _sub_block(dil)
    n_units, u = seq // sub, sub // dil
    n_blk = n_units * u // N_BACK
    n_res = min(dil, max(1, SWA_BLOCKS_PER_TRIP // n_blk))
    view = zb.reshape(zb.shape[0], batch, n_units, dil, u, GROUP_WIDTH)

    def rows_in(tile):
        return pl.BlockSpec((None, None, n_units, n_res, u, GROUP_WIDTH), lambda i, r: (tile, i, 0, r, 0, 0))

    def rows_out(width):
        return pl.BlockSpec((None, n_units, n_res, u, width), lambda i, r: (i, 0, r, 0, 0))

    return pl.pallas_call(
        _swa_kernel,
        grid=(batch, dil // n_res),
        in_specs=[rows_in(g), rows_in(N_SWA_GROUPS + g), rows_in(2 * N_SWA_GROUPS + g),
                  pl.BlockSpec((None, 2, HEADS_PER_GROUP, N_BACK, 2 * N_BACK), lambda i, r: (g, 0, 0, 0, 0))],
        out_specs=[rows_out(GROUP_WIDTH), rows_out(HEAD_DIM)],
        out_shape=[
            jax.ShapeDtypeStruct((batch, n_units, dil, u, GROUP_WIDTH), BF16),
            jax.ShapeDtypeStruct((batch, n_units, dil, u, HEAD_DIM), F32),
        ],
        compiler_params=_params("parallel", "parallel"),
        name=f"swa_group{g}",
    )(view, view, view, tables)


def _split3(x):
    hi = x.astype(BF16)
    rest = x - hi.astype(F32)
    mid = rest.astype(BF16)
    lo = (rest - mid.astype(F32)).astype(BF16)
    return hi, mid, lo


def _merge_stage(o0_ref, o1_ref, o2_ref, l0_ref, l1_ref, l2_ref, q_ref, kb_ref, vb_ref, o_ref):
    tm = o_ref.shape[0]
    group_refs = ((o0_ref, l0_ref), (o1_ref, l1_ref), (o2_ref, l2_ref))
    outs, lses = [None] * N_SWA_GROUPS, [None] * N_SWA_GROUPS

    def token_order(g):
        o_g, l_g = group_refs[g]
        dil = SWA_PATTERN[g][1]
        o = o_g[...].reshape(tm, GROUP_WIDTH)
        l = l_g[...].reshape(tm, HEAD_DIM)
        if dil > 1:
            inv = _residue_major_perm(PERM_BLOCK, dil, transpose=True)
            l3 = _split3(l)
            o_nat, l_nat = [], []
            for s in range(0, tm, PERM_BLOCK):
                o_nat.append(jnp.dot(inv, o[s:s + PERM_BLOCK, :], preferred_element_type=F32))
                l_nat.append(sum(jnp.dot(inv, t[s:s + PERM_BLOCK, :], preferred_element_type=F32) for t in l3))
            o = jnp.concatenate(o_nat, axis=0)
            l = jnp.concatenate(l_nat, axis=0)
        outs[g] = o.astype(F32)
        lses[g] = l

    def merge_head(h):
        lo, hi = h * HEAD_DIM, (h + 1) * HEAD_DIM
        ls = [l[:, h:h + 1] for l in lses]
        ds = [l[:, HEADS_PER_GROUP + h:HEADS_PER_GROUP + h + 1] for l in lses]
        mx = jnp.maximum(jnp.maximum(ls[0], ls[1]), ls[2])
        es = [d * jnp.exp(l - mx) for l, d in zip(ls, ds)]
        tot = es[0] + es[1] + es[2]
        for g in range(N_SWA_GROUPS):
            alpha = es[g] / tot
            o_ref[:, g * GROUP_WIDTH + lo:g * GROUP_WIDTH + hi] = (outs[g][:, lo:hi] * alpha).astype(o_ref.dtype)

    return ([functools.partial(token_order, g) for g in range(N_SWA_GROUPS)]
            + [functools.partial(merge_head, h) for h in range(HEADS_PER_GROUP)]
            + _mem_attention_parts(q_ref, kb_ref, vb_ref, o_ref, MIXER_WIDTH))


def _swa_merge_out(outs, lses, zb, mem_kv, layer, w, x, g, *, rows_per_batch):
    tm = ROW_TILE
    tiles_per_batch = rows_per_batch // tm

    def group_tile(width, dil):
        sub = _sub_block(dil)
        return lambda tile: pl.BlockSpec(
            (None, tm // sub, dil, sub // dil, width),
            lambda s: (tile(s) // tiles_per_batch, tile(s) % tiles_per_batch, 0, 0, 0))

    specs = ([group_tile(GROUP_WIDTH, dil) for _, dil in SWA_PATTERN]
             + [group_tile(HEAD_DIM, dil) for _, dil in SWA_PATTERN]
             + [lambda tile: pl.BlockSpec((None, tm, MEM_WIDTH), lambda s: (3 * N_SWA_GROUPS, tile(s), 0))])
    return _mix_out(_merge_stage, (*outs, *lses, zb), specs, [], mem_kv, layer, w, x, g,
                    tm=tm, rows_per_batch=rows_per_batch, name="swa_merge_out")


def _head_rows(row, col0):
    return jnp.concatenate([row[:, col0 + h * HEAD_DIM:col0 + (h + 1) * HEAD_DIM]
                            for h in range(HEADS_PER_GROUP)], axis=0)


def _sample_attention(q4, kv_ref, bias=None, new=None):
    k3, v3 = kv_ref[:, 0], kv_ref[:, 1]
    s = jnp.sum(k3 * q4[None], axis=-1, keepdims=True) * ATTN_SCALE
    if bias is not None:
        s = s + bias
    m = jnp.max(s, axis=0)
    if new is not None:
        k_new, v_new, b_new = new
        s_new = jnp.sum(k_new * q4, axis=-1, keepdims=True) * ATTN_SCALE + b_new
        m = jnp.maximum(m, s_new)
    p = jnp.exp(s - m[None])
    den = jnp.sum(p, axis=0)
    o = jnp.sum(p * v3, axis=0)
    if new is not None:
        p_new = jnp.exp(s_new - m)
        den = den + p_new
        o = o + p_new * v_new
    return o / den, m + jnp.log(den)


def _store_head_rows(o_ref, col0, x4):
    for h in range(HEADS_PER_GROUP):
        o_ref[:, col0 + h * HEAD_DIM:col0 + (h + 1) * HEAD_DIM] = x4[h:h + 1, :]


def _sample_mem_attention(q_row, kv_ref, o_ref, col0):
    o, _ = _sample_attention(_head_rows(q_row, 0), kv_ref)
    _store_head_rows(o_ref, col0, o)


def _sample_mix_a_kernel(z_ref, gv_ref, w0_ref, b0_ref, kv_ref, o_ref, vrow_ref):
    u = z_ref[:, 0:MIXER_WIDTH]
    v = _rms(z_ref[:, MIXER_WIDTH:2 * MIXER_WIDTH], gv_ref[...])
    vrow_ref[...] = v
    o_ref[:, 0:MIXER_WIDTH] = u * (w0_ref[...] * v + b0_ref[...])
    _sample_mem_attention(z_ref[:, 2 * MIXER_WIDTH:2 * MIXER_WIDTH + MEM_WIDTH], kv_ref, o_ref, MIXER_WIDTH)


def _sample_mix_a(z, g_v, w_s, b_s, mem_kv, layer):
    bd = mem_kv.shape[1]
    w0 = jnp.repeat(w_s[:, 0, 0], GROUP_DIM_A).reshape(1, MIXER_WIDTH)
    b0 = jnp.repeat(b_s[:, 0], GROUP_DIM_A).reshape(1, MIXER_WIDTH)
    width = z.shape[1]
    vec = lambda i: (0, 0)
    return pl.pallas_call(
        _sample_mix_a_kernel,
        grid=(bd,),
        in_specs=[
            pl.BlockSpec((None, 1, width), lambda i: (i, 0, 0)),
            pl.BlockSpec((1, MIXER_WIDTH), vec),
            pl.BlockSpec((1, MIXER_WIDTH), vec),
            pl.BlockSpec((1, MIXER_WIDTH), vec),
            _mem_kv_spec(layer, lambda i: i),
        ],
        out_specs=[
            pl.BlockSpec((None, 1, D_MODEL), lambda i: (i, 0, 0)),
            pl.BlockSpec((None, 1, MIXER_WIDTH), lambda i: (i, 0, 0)),
        ],
        out_shape=[
            jax.ShapeDtypeStruct((bd, 1, D_MODEL), F32),
            jax.ShapeDtypeStruct((bd, 1, MIXER_WIDTH), F32),
        ],
        compiler_params=_params("parallel"),
        name="sample_mix_a",
    )(z[:bd].reshape(bd, 1, width), g_v.reshape(1, MIXER_WIDTH), w0, b0, mem_kv)


def _sample_mix_b_kernel(z_ref, c0_ref, c1_ref, c2_ref, bcol_ref, bnew_ref, kv_ref, o_ref):
    caches = (c0_ref, c1_ref, c2_ref)
    z = z_ref[...]
    outs, lses = [], []
    for g in range(N_SWA_GROUPS):
        c0 = g * GROUP_WIDTH
        new = (_head_rows(z, MIXER_WIDTH + c0), _head_rows(z, 2 * MIXER_WIDTH + c0), bnew_ref[g])
        o, lse = _sample_attention(_head_rows(z, c0), caches[g], bias=bcol_ref[g], new=new)
        outs.append(o)
        lses.append(lse)
    mx = jnp.maximum(jnp.maximum(lses[0], lses[1]), lses[2])
    es = [jnp.exp(l - mx) for l in lses]
    tot = es[0] + es[1] + es[2]
    for g in range(N_SWA_GROUPS):
        _store_head_rows(o_ref, g * GROUP_WIDTH, outs[g] * (es[g] / tot))
    _sample_mem_attention(z[:, 3 * MIXER_WIDTH:3 * MIXER_WIDTH + MEM_WIDTH], kv_ref, o_ref, MIXER_WIDTH)


def _sample_mix_b(z, win_caches, swa_layer, bias_groups, mem_kv, layer):
    bd = mem_kv.shape[1]
    width = z.shape[1]
    cache_views, cache_specs = [], []
    for g, (win, dil) in enumerate(SWA_PATTERN):
        c = win_caches[g]
        cache_views.append(c.reshape(c.shape[0], bd, win // dil, dil, 2, HEADS_PER_GROUP, HEAD_DIM))
        cache_specs.append(pl.BlockSpec((None, None, N_BACK, None, 2, HEADS_PER_GROUP, HEAD_DIM),
                                        lambda i: (swa_layer, i, 0, 0, 0, 0, 0)))
    bcol = jnp.stack([bg[:, N_BACK:0:-1].T for bg in bias_groups], axis=0)
    bcol = jnp.broadcast_to(bcol[..., None], bcol.shape + (HEAD_DIM,))
    bnew = jnp.stack([bg[:, 0] for bg in bias_groups], axis=0)
    bnew = jnp.broadcast_to(bnew[..., None], bnew.shape + (HEAD_DIM,))
    return pl.pallas_call(
        _sample_mix_b_kernel,
        grid=(bd,),
        in_specs=[pl.BlockSpec((None, 1, width), lambda i: (i, 0, 0))] + cache_specs + [
            pl.BlockSpec((N_SWA_GROUPS, N_BACK, HEADS_PER_GROUP, HEAD_DIM), lambda i: (0, 0, 0, 0)),
            pl.BlockSpec((N_SWA_GROUPS, HEADS_PER_GROUP, HEAD_DIM), lambda i: (0, 0, 0)),
            _mem_kv_spec(layer, lambda i: i),
        ],
        out_specs=pl.BlockSpec((None, 1, D_MODEL), lambda i: (i, 0, 0)),
        out_shape=jax.ShapeDtypeStruct((bd, 1, D_MODEL), F32),
        compiler_params=_params("parallel"),
        name="sample_mix_b",
    )(z[:bd].reshape(bd, 1, width), *cache_views, bcol, bnew, mem_kv)


def _t5_bucket(dist):
    nf = jnp.maximum(dist, MAX_EXACT).astype(F32)
    large = MAX_EXACT + (jnp.log(nf / MAX_EXACT) / math.log(MAX_DISTANCE / MAX_EXACT)
                         * (N_BUCKETS - MAX_EXACT)).astype(jnp.int32)
    large = jnp.minimum(large, N_BUCKETS - 1)
    return jnp.where(dist < MAX_EXACT, dist, large)


def _group_bias(rel_bias, g, dil):
    dist = jnp.arange(N_BACK + 1, dtype=jnp.int32) * dil
    b = rel_bias[_t5_bucket(dist)][:, g * HEADS_PER_GROUP:(g + 1) * HEADS_PER_GROUP]
    return b.T.astype(F32)


def _band_tables_kernel(b_ref, o_ref):
    n, rows, width = o_ref.shape
    for x in range(n):
        row = jnp.broadcast_to(b_ref[x:x + 1, :], (rows, width))
        o_ref[x] = pltpu.roll(row, 0, 1, stride=1, stride_axis=0)


def _band_tables(bias_groups):
    width = 2 * N_BACK
    rows = []
    for bias_j in bias_groups:
        masked = jnp.full((HEADS_PER_GROUP, N_BACK - 1), NEG_INF, F32)
        rows.append(jnp.concatenate([bias_j[:, :1], masked, bias_j[:, N_BACK:0:-1]], axis=1))
        rows.append(jnp.concatenate([bias_j[:, ::-1], masked], axis=1))
    base = jnp.stack(rows, axis=0).reshape(-1, width)
    tabs = pl.pallas_call(
        _band_tables_kernel,
        out_shape=jax.ShapeDtypeStruct((base.shape[0], N_BACK, width), F32),
        name="band_tables",
    )(base)
    return tabs.reshape(N_SWA_GROUPS, 2, HEADS_PER_GROUP, N_BACK, width)


def _kv_tail_kernel(k_ref, v_ref, o_ref, *, dil):
    rows = k_ref.shape[0]
    for kv, ref in enumerate((k_ref, v_ref)):
        x = ref[...]
        if dil > 1:
            inv = _residue_major_perm(PERM_BLOCK, dil, transpose=True)
            x = jnp.concatenate([jnp.dot(inv, x[s:s + PERM_BLOCK, :], preferred_element_type=F32)
                                 for s in range(0, rows, PERM_BLOCK)], axis=0)
        x = x.astype(F32)
        for h in range(HEADS_PER_GROUP):
            o_ref[:, kv, h, :] = x[:, h * HEAD_DIM:(h + 1) * HEAD_DIM]


def _kv_tail(zb, g, *, batch, seq):
    win, dil = SWA_PATTERN[g]
    rows = min(win, FFN_ROW_TILE)
    first = (seq - win) // rows
    per_batch = seq // rows

    def tile(t):
        return pl.BlockSpec((None, rows, GROUP_WIDTH), lambda b, s: (t, b * per_batch + first + s, 0))

    return pl.pallas_call(
        functools.partial(_kv_tail_kernel, dil=dil),
        grid=(batch, win // rows),
        in_specs=[tile(N_SWA_GROUPS + g), tile(2 * N_SWA_GROUPS + g)],
        out_specs=pl.BlockSpec((None, rows, 2, HEADS_PER_GROUP, HEAD_DIM), lambda b, s: (b, s, 0, 0, 0)),
        out_shape=jax.ShapeDtypeStruct((batch, win, 2, HEADS_PER_GROUP, HEAD_DIM), F32),
        compiler_params=_params("parallel", "parallel"),
        name=f"kv_tail{g}",
    )(zb, zb)


def kernel(x_prompt, x_sample, mem_prompt, cache_mem_kv, cache_win128_kv, cache_win512_kv, cache_win2048_kv, rel_bias, norm_mix_pre, norm_mix_post, norm_ffn_pre, norm_ffn_post, norm_mem, w_mem_kv, w_in_a, norm_v_a, w_spatial_a, b_spatial_a, w_in_b, w_out, w_ffn_up, w_ffn_down):
    batch, seq, _ = x_prompt.shape
    bd = x_sample.shape[0]
    depth = w_out.shape[0]
    m_p = batch * seq
    win_caches = (cache_win128_kv, cache_win512_kv, cache_win2048_kv)

    bias_groups = [_group_bias(rel_bias, g, dil) for g, (_, dil) in enumerate(SWA_PATTERN)]
    band_tables = _band_tables(bias_groups)

    yp = x_prompt.reshape(m_p, D_MODEL)
    ys = jnp.pad(x_sample.reshape(bd, D_MODEL), ((0, SAMPLE_PAD - bd), (0, 0)))
    mem_rows = mem_prompt.reshape(batch * N_MEM, D_MODEL)

    mem_kv_p = _mem_kv(mem_rows, norm_mem, w_mem_kv, batch=batch)
    chunk_v_s = []
    win_p = [[] for _ in SWA_PATTERN]
    win_s = [[] for _ in SWA_PATTERN]
    for i in range(depth):
        li = i // 2
        if i % 2 == 0:
            zp, zs = _in_proj_a(yp, ys, norm_mix_pre[i], w_in_a, li, tm=ROW_TILE, tn=CAST_TILE,
                                gelu_cols=2 * MIXER_WIDTH)
            mix_s, v_rows = _sample_mix_a(zs, norm_v_a[li], w_spatial_a[li], b_spatial_a[li], cache_mem_kv, i)
            chunk_v_s.append(v_rows)
            ys, w_o = _sample_out_proj(mix_s, w_out, i, ys, norm_mix_post[i])
            yp = _gmlp_mix_out(zp, norm_v_a[li], w_spatial_a[li], b_spatial_a[li], mem_kv_p, i,
                               w_o, yp, norm_mix_post[i], tm=ROW_TILE, rows_per_batch=seq)
        else:
            zb, zs = _in_proj_b(yp, ys, norm_mix_pre[i], w_in_b, li, tm=ROW_TILE)
            outs, lses = [], []
            for g, (win, dil) in enumerate(SWA_PATTERN):
                o, lse = _swa_group(zb, band_tables, g, batch=batch, seq=seq)
                outs.append(o)
                lses.append(lse)
                win_p[g].append(_kv_tail(zb, g, batch=batch, seq=seq))
                kv_new = zs[:bd, MIXER_WIDTH:3 * MIXER_WIDTH]
                kv_new = kv_new.reshape(bd, 1, 2, N_SWA_GROUPS, HEADS_PER_GROUP, HEAD_DIM)[:, :, :, g]
                win_s[g].append(kv_new)
            mix_s = _sample_mix_b(zs, win_caches, li, bias_groups, cache_mem_kv, i)
            ys, w_o = _sample_out_proj(mix_s, w_out, i, ys, norm_mix_post[i])
            yp = _swa_merge_out(outs, lses, zb, mem_kv_p, i, w_o, yp, norm_mix_post[i], rows_per_batch=seq)
        yp, ys, w_ffn = _ffn_head(yp, ys, norm_ffn_pre[i], norm_ffn_post[i], w_ffn_up, w_ffn_down, i,
                                  tm=FFN_ROW_TILE, tf=HEAD_FF_TILE)
        yp = _ffn(yp, norm_ffn_pre[i], norm_ffn_post[i], w_ffn, tm=FFN_ROW_TILE, tf=FF_TILE, first_tile=1)

    return (
        yp.reshape(batch, seq, D_MODEL),
        ys[:bd].reshape(bd, 1, D_MODEL),
        mem_kv_p,
        jnp.stack(chunk_v_s, axis=0),
        jnp.stack(win_p[0], axis=0),
        jnp.stack(win_p[1], axis=0),
        jnp.stack(win_p[2], axis=0),
        jnp.stack(win_s[0], axis=0),
        jnp.stack(win_s[1], axis=0),
        jnp.stack(win_s[2], axis=0),
    )
```

```python
import functools
import math

import jax
import jax.numpy as jnp
from jax import lax
from jax.experimental import pallas as pl
from jax.experimental.pallas import tpu as pltpu

F32 = jnp.float32
BF16 = jnp.bfloat16

D_MODEL = 2048
HEAD_DIM = 128
N_MEM = 256
N_MEM_HEADS = 4
MEM_WIDTH = N_MEM_HEADS * HEAD_DIM
MIXER_WIDTH = D_MODEL - MEM_WIDTH
CHUNK = 128
N_GROUPS_A = 4
GROUP_DIM_A = MIXER_WIDTH // N_GROUPS_A
SWA_PATTERN = ((128, 1), (512, 4), (2048, 16))
N_SWA_GROUPS = len(SWA_PATTERN)
HEADS_PER_GROUP = 4
GROUP_WIDTH = HEADS_PER_GROUP * HEAD_DIM
N_BACK = 128
N_BUCKETS = 32
MAX_EXACT = N_BUCKETS // 2
MAX_DISTANCE = 2048
D_FF = 5632
EPS = 1e-6
NEG_INF = -1e30
ATTN_SCALE = HEAD_DIM ** -0.5
SAMPLE_PAD = 16
PERM_BLOCK = 256
SWA_BLOCKS_PER_TRIP = 16
OUT_CHUNK = 256

ROW_TILE = 512
FFN_ROW_TILE = 1024
FF_TILE = 512
HEAD_FF_TILE = 256
CAST_TILE = 512

V7X_VMEM_BYTES = 64 * 1024 * 1024
VMEM_LIMIT = V7X_VMEM_BYTES - 8 * 1024 * 1024
FFN_VMEM_LIMIT = V7X_VMEM_BYTES - 2 * 1024 * 1024


def _params(*sem, vmem_limit=VMEM_LIMIT):
    return pltpu.CompilerParams(dimension_semantics=sem, vmem_limit_bytes=vmem_limit)


def _gelu(x):
    return 0.5 * x * (1.0 + jnp.tanh(0.7978845608028654 * (x + 0.044715 * (x * x * x))))


def _rms(x, g):
    return x * lax.rsqrt(jnp.mean(x * x, axis=-1, keepdims=True) + EPS) * g


def _log2(n):
    assert n & (n - 1) == 0
    return n.bit_length() - 1


def _residue_major_perm(tm, dil, transpose=False):
    n = tm // dil
    row = lax.broadcasted_iota(jnp.int32, (tm, tm), 0)
    col = lax.broadcasted_iota(jnp.int32, (tm, tm), 1)
    dst, src = (col, row) if transpose else (row, col)
    want = lax.shift_left(jnp.bitwise_and(dst, n - 1), _log2(dil)) + lax.shift_right_logical(dst, _log2(n))
    return (src == want).astype(BF16)


def _in_proj_a_kernel(x_ref, xs_ref, g_ref, w_ref, zp_ref, zs_ref, wb_ref, xn_ref, *, n_col, gelu_cols):
    tm = x_ref.shape[0]
    tn = w_ref.shape[1]
    s = pl.program_id(0)

    def prompt_cols(acc, c0):
        col = lax.broadcasted_iota(jnp.int32, acc.shape, 1) + c0
        return jnp.where(col < gelu_cols, _gelu(acc), acc * ATTN_SCALE).astype(zp_ref.dtype)

    @pl.when(s == 0)
    def _():
        xn_ref[:tm, :] = _rms(x_ref[...], g_ref[...]).astype(BF16)
        xn_ref[tm:, :] = _rms(xs_ref[...], g_ref[...]).astype(BF16)

    for j in range(n_col):
        @pl.when(s == j)
        def _(j=j):
            cols = slice(j * tn, (j + 1) * tn)
            w = w_ref[...].astype(BF16)
            wb_ref[:, cols] = w
            acc = jnp.dot(xn_ref[...], w, preferred_element_type=F32)
            zp_ref[:, cols] = prompt_cols(acc[:tm, :], j * tn)
            sample = acc[tm:, :]
            zs_ref[:, cols] = _gelu(sample) if (j + 1) * tn <= gelu_cols else sample

    @pl.when(s >= n_col)
    def _():
        xn = _rms(x_ref[...], g_ref[...]).astype(BF16)
        acc = jnp.dot(xn, wb_ref[...], preferred_element_type=F32)
        zp_ref[...] = prompt_cols(acc, 0)


def _in_proj_a(x, xs, g, w, layer, *, tm, tn, gelu_cols):
    m, k = x.shape
    ms = xs.shape[0]
    n = w.shape[2]
    n_col, n_row = n // tn, m // tm
    assert gelu_cols % tn == 0
    row = lambda s: jnp.maximum(s - (n_col - 1), 0)
    return pl.pallas_call(
        functools.partial(_in_proj_a_kernel, n_col=n_col, gelu_cols=gelu_cols),
        grid=(n_col + n_row - 1,),
        in_specs=[
            pl.BlockSpec((tm, k), lambda s: (row(s), 0)),
            pl.BlockSpec((ms, k), lambda s: (0, 0)),
            pl.BlockSpec((1, k), lambda s: (0, 0)),
            pl.BlockSpec((None, k, tn), lambda s: (layer, 0, jnp.minimum(s, n_col - 1))),
        ],
        out_specs=[
            pl.BlockSpec((tm, n), lambda s: (row(s), 0)),
            pl.BlockSpec((ms, n), lambda s: (0, 0)),
        ],
        out_shape=[
            jax.ShapeDtypeStruct((m, n), BF16),
            jax.ShapeDtypeStruct((ms, n), F32),
        ],
        scratch_shapes=[pltpu.VMEM((k, n), BF16), pltpu.VMEM((tm + ms, k), BF16)],
        compiler_params=_params("arbitrary"),
        name="in_proj_a",
    )(x, xs, g.reshape(1, k), w)


def _mem_kv_kernel(x_ref, g_ref, w_ref, o_ref):
    xn = _rms(x_ref[...], g_ref[...]).astype(BF16)
    acc = jnp.dot(xn, w_ref[...].astype(BF16), preferred_element_type=F32)
    for kv in range(2):
        for h in range(N_MEM_HEADS):
            c0 = (kv * N_MEM_HEADS + h) * HEAD_DIM
            o_ref[:, kv, h, :] = acc[:, c0:c0 + HEAD_DIM]


def _mem_kv(mem_rows, g, w, *, batch):
    m, k = mem_rows.shape
    layers = w.shape[0]
    out = pl.pallas_call(
        _mem_kv_kernel,
        grid=(layers,),
        in_specs=[
            pl.BlockSpec((m, k), lambda l: (0, 0)),
            pl.BlockSpec((None, 1, k), lambda l: (l, 0, 0)),
            pl.BlockSpec((None, k, 2 * MEM_WIDTH), lambda l: (l, 0, 0)),
        ],
        out_specs=pl.BlockSpec((None, m, 2, N_MEM_HEADS, HEAD_DIM), lambda l: (l, 0, 0, 0, 0)),
        out_shape=jax.ShapeDtypeStruct((layers, m, 2, N_MEM_HEADS, HEAD_DIM), F32),
        compiler_params=_params("parallel"),
        name="mem_kv",
    )(mem_rows, g.reshape(layers, 1, k), w)
    return out.reshape(layers, batch, m // batch, 2, N_MEM_HEADS, HEAD_DIM)


def _in_proj_b_kernel(x_ref, xs_ref, g_ref, w_ref, o_ref, zs_ref, wb_ref, xn_ref):
    tm = x_ref.shape[0]
    n_tiles = o_ref.shape[0]
    n_qkv = 3 * N_SWA_GROUPS
    s = pl.program_id(0)

    def row_orders():
        xn = _rms(x_ref[...], g_ref[...]).astype(BF16)
        xn_ref[0, :tm, :] = xn
        for g in range(1, N_SWA_GROUPS):
            perm = _residue_major_perm(PERM_BLOCK, SWA_PATTERN[g][1])
            for r in range(0, tm, PERM_BLOCK):
                xn_ref[g, r:r + PERM_BLOCK, :] = jnp.dot(
                    perm, xn[r:r + PERM_BLOCK, :], preferred_element_type=F32).astype(BF16)

    def prompt_tile(t, acc):
        if t < N_SWA_GROUPS or t == n_qkv:
            acc = acc * ATTN_SCALE
        o_ref[t] = acc.astype(o_ref.dtype)

    src = lambda t: t % N_SWA_GROUPS if t < n_qkv else 0

    @pl.when(s == 0)
    def _():
        row_orders()
        xs = _rms(xs_ref[...], g_ref[...]).astype(BF16)
        for g in range(N_SWA_GROUPS):
            xn_ref[g, tm:, :] = xs

    for t in range(n_tiles):
        @pl.when(s == t)
        def _(t=t):
            cols = slice(t * GROUP_WIDTH, (t + 1) * GROUP_WIDTH)
            w = w_ref[...].astype(BF16)
            wb_ref[:, cols] = w
            acc = jnp.dot(xn_ref[src(t)], w, preferred_element_type=F32)
            prompt_tile(t, acc[:tm, :])
            zs_ref[:, cols] = acc[tm:, :]

    @pl.when(s >= n_tiles)
    def _():
        row_orders()
        for t in range(n_tiles):
            w = wb_ref[:, t * GROUP_WIDTH:(t + 1) * GROUP_WIDTH]
            prompt_tile(t, jnp.dot(xn_ref[src(t), :tm, :], w, preferred_element_type=F32))


def _in_proj_b(x, xs, g, w, layer, *, tm):
    m, k = x.shape
    ms = xs.shape[0]
    n = w.shape[2]
    n_tiles = n // GROUP_WIDTH
    row = lambda s: jnp.maximum(s - (n_tiles - 1), 0)
    return pl.pallas_call(
        _in_proj_b_kernel,
        grid=(n_tiles + m // tm - 1,),
        in_specs=[
            pl.BlockSpec((tm, k), lambda s: (row(s), 0)),
            pl.BlockSpec((ms, k), lambda s: (0, 0)),
            pl.BlockSpec((1, k), lambda s: (0, 0)),
            pl.BlockSpec((None, k, GROUP_WIDTH), lambda s: (layer, 0, jnp.minimum(s, n_tiles - 1))),
        ],
        out_specs=[
            pl.BlockSpec((n_tiles, tm, GROUP_WIDTH), lambda s: (0, row(s), 0)),
            pl.BlockSpec((ms, n), lambda s: (0, 0)),
        ],
        out_shape=[
            jax.ShapeDtypeStruct((n_tiles, m, GROUP_WIDTH), BF16),
            jax.ShapeDtypeStruct((ms, n), F32),
        ],
        scratch_shapes=[pltpu.VMEM((k, n), BF16), pltpu.VMEM((N_SWA_GROUPS, tm + ms, k), BF16)],
        compiler_params=_params("arbitrary"),
        name="in_proj_b",
    )(x, xs, g.reshape(1, k), w)


MEM_KV_SCRATCH = [pltpu.VMEM((N_MEM_HEADS, N_MEM, HEAD_DIM), BF16),
                  pltpu.VMEM((N_MEM_HEADS, N_MEM, 2 * HEAD_DIM), BF16)]


def _prepare_mem_kv(kv_ref, kb_ref, vb_ref):
    for h in range(N_MEM_HEADS):
        kb_ref[h] = kv_ref[:, 0, h, :].astype(BF16)
        vb_ref[h, :, :HEAD_DIM] = kv_ref[:, 1, h, :].astype(BF16)
        vb_ref[h, :, HEAD_DIM:] = jnp.ones((N_MEM, HEAD_DIM), BF16)


def _mem_attention_parts(q_ref, kb_ref, vb_ref, o_ref, col0):
    def head(h):
        lo, hi = h * HEAD_DIM, (h + 1) * HEAD_DIM
        s = lax.dot_general(q_ref[:, lo:hi], kb_ref[h], (((1,), (1,)), ((), ())), preferred_element_type=F32)
        m = jnp.max(s, axis=1, keepdims=True)
        p = jnp.exp(s - m).astype(BF16)
        ov = jnp.dot(p, vb_ref[h], preferred_element_type=F32)
        o_ref[:, col0 + lo:col0 + hi] = (ov[:, :HEAD_DIM] / ov[:, HEAD_DIM:]).astype(o_ref.dtype)

    return [functools.partial(head, h) for h in range(N_MEM_HEADS)]


def _gmlp_stage(u_ref, v_ref, q_ref, gv_ref, ws_ref, bs_ref, kb_ref, vb_ref, vn_ref, o_ref):
    tm = u_ref.shape[0]

    def norm_v():
        vn_ref[...] = _rms(v_ref[...].astype(F32), gv_ref[...]).astype(BF16)

    def group(g):
        row = lax.broadcasted_iota(jnp.int32, (CHUNK, CHUNK), 0)
        col = lax.broadcasted_iota(jnp.int32, (CHUNK, CHUNK), 1)
        w = jnp.where(row >= col, ws_ref[g], 0.0).astype(BF16)
        b = bs_ref[:, g:g + 1]
        c0, c1 = g * GROUP_DIM_A, (g + 1) * GROUP_DIM_A
        for c in range(tm // CHUNK):
            r0, r1 = c * CHUNK, (c + 1) * CHUNK
            s = jnp.dot(w, vn_ref[r0:r1, c0:c1], preferred_element_type=F32) + b
            o_ref[r0:r1, c0:c1] = (u_ref[r0:r1, c0:c1].astype(F32) * s).astype(o_ref.dtype)

    return ([norm_v] + [functools.partial(group, g) for g in range(N_GROUPS_A)]
            + _mem_attention_parts(q_ref, kb_ref, vb_ref, o_ref, MIXER_WIDTH))


def _mem_kv_spec(layer, batch_of):
    return pl.BlockSpec((None, None, N_MEM, 2, N_MEM_HEADS, HEAD_DIM),
                        lambda i: (layer, batch_of(i), 0, 0, 0, 0))


def _mix_out_kernel(*refs, stage, n_in, n_tiles, tiles_per_batch):
    mix_in = refs[:n_in]
    kv_ref, w_ref, x_ref, g_ref, o_ref, buf0_ref, buf1_ref, acc_ref, kb_ref, vb_ref = refs[n_in:n_in + 10]
    extra = refs[n_in + 10:]
    s = pl.program_id(0)

    @pl.when(s == 0)
    def _():
        buf1_ref[...] = jnp.zeros_like(buf1_ref)

    @pl.when(jnp.minimum(s, n_tiles - 1) % tiles_per_batch == 0)
    def _():
        _prepare_mem_kv(kv_ref, kb_ref, vb_ref)

    def step(dst_ref, src_ref):
        parts = stage(*mix_in, kb_ref, vb_ref, *extra, dst_ref)
        n_chunks = D_MODEL // OUT_CHUNK
        for c in range(n_chunks):
            cols = slice(c * OUT_CHUNK, (c + 1) * OUT_CHUNK)
            acc_ref[:, cols] = jnp.dot(src_ref[...], w_ref[:, cols], preferred_element_type=F32)
            for part in parts[c * len(parts) // n_chunks:(c + 1) * len(parts) // n_chunks]:
                part()
        o_ref[...] = x_ref[...] + _rms(acc_ref[...], g_ref[...])

    @pl.when(s % 2 == 0)
    def _():
        step(buf0_ref, buf1_ref)

    @pl.when(s % 2 == 1)
    def _():
        step(buf1_ref, buf0_ref)


def _mix_out(stage, mix_inputs, mix_specs, extra_scratch, mem_kv, layer, w, x, g, *, tm, rows_per_batch, name):
    m = x.shape[0]
    n_tiles = m // tm
    tiles_per_batch = rows_per_batch // tm
    mix_tile = lambda s: jnp.minimum(s, n_tiles - 1)
    out_tile = lambda s: jnp.maximum(s - 1, 0)
    row_spec = pl.BlockSpec((tm, D_MODEL), lambda s: (out_tile(s), 0))
    return pl.pallas_call(
        functools.partial(_mix_out_kernel, stage=stage, n_in=len(mix_inputs), n_tiles=n_tiles,
                          tiles_per_batch=tiles_per_batch),
        grid=(n_tiles + 1,),
        in_specs=[spec(mix_tile) for spec in mix_specs] + [
            _mem_kv_spec(layer, lambda s: mix_tile(s) // tiles_per_batch),
            pl.BlockSpec((D_MODEL, D_MODEL), lambda s: (0, 0), pipeline_mode=pl.Buffered(1)),
            row_spec,
            pl.BlockSpec((1, D_MODEL), lambda s: (0, 0)),
        ],
        out_specs=row_spec,
        out_shape=jax.ShapeDtypeStruct((m, D_MODEL), F32),
        scratch_shapes=[pltpu.VMEM((tm, D_MODEL), BF16), pltpu.VMEM((tm, D_MODEL), BF16),
                        pltpu.VMEM((tm, D_MODEL), F32)] + MEM_KV_SCRATCH + list(extra_scratch),
        compiler_params=_params("arbitrary"),
        name=name,
    )(*mix_inputs, mem_kv, w, x, g.reshape(1, D_MODEL))


def _gmlp_mix_out(zact, g_v, w_s, b_s, mem_kv, layer, w, x, g, *, tm, rows_per_batch):
    const = lambda shape: (lambda tile: pl.BlockSpec(shape, lambda s: (0,) * len(shape)))
    specs = [
        lambda tile: pl.BlockSpec((tm, MIXER_WIDTH), lambda s: (tile(s), 0)),
        lambda tile: pl.BlockSpec((tm, MIXER_WIDTH), lambda s: (tile(s), 1)),
        lambda tile: pl.BlockSpec((tm, MEM_WIDTH), lambda s: (tile(s), 2 * MIXER_WIDTH // MEM_WIDTH)),
        const((1, MIXER_WIDTH)),
        const((N_GROUPS_A, CHUNK, CHUNK)),
        const((CHUNK, N_GROUPS_A)),
    ]
    return _mix_out(_gmlp_stage, (zact, zact, zact, g_v.reshape(1, MIXER_WIDTH), w_s, b_s.T), specs,
                    [pltpu.VMEM((tm, MIXER_WIDTH), BF16)], mem_kv, layer, w, x, g,
                    tm=tm, rows_per_batch=rows_per_batch, name="gmlp_mix_out")


def _out_proj_cast_kernel(mix_ref, w_ref, x_ref, g_ref, o_ref, wb_ref, acc_ref):
    j = pl.program_id(0)
    w = w_ref[...].astype(BF16)
    wb_ref[...] = w
    acc_ref[j] = jnp.dot(mix_ref[...], w, preferred_element_type=F32)

    @pl.when(j == pl.num_programs(0) - 1)
    def _():
        o = jnp.concatenate([acc_ref[t] for t in range(acc_ref.shape[0])], axis=1)
        o_ref[...] = x_ref[...] + _rms(o, g_ref[...])


def _sample_out_proj(mix, w, layer, x, g):
    bd = mix.shape[0]
    mix = jnp.pad(mix.reshape(bd, D_MODEL), ((0, x.shape[0] - bd), (0, 0))).astype(BF16)
    return _out_proj_cast(mix, w, layer, x, g, tn=CAST_TILE)


def _out_proj_cast(mix, w, layer, x, g, *, tn):
    m = x.shape[0]
    n_tiles = D_MODEL // tn
    return pl.pallas_call(
        _out_proj_cast_kernel,
        grid=(n_tiles,),
        in_specs=[
            pl.BlockSpec((m, D_MODEL), lambda j: (0, 0)),
            pl.BlockSpec((None, D_MODEL, tn), lambda j: (layer, 0, j)),
            pl.BlockSpec((m, D_MODEL), lambda j: (0, 0)),
            pl.BlockSpec((1, D_MODEL), lambda j: (0, 0)),
        ],
        out_specs=[
            pl.BlockSpec((m, D_MODEL), lambda j: (0, 0)),
            pl.BlockSpec((D_MODEL, tn), lambda j: (0, j)),
        ],
        out_shape=[
            jax.ShapeDtypeStruct((m, D_MODEL), F32),
            jax.ShapeDtypeStruct((D_MODEL, D_MODEL), BF16),
        ],
        scratch_shapes=[pltpu.VMEM((n_tiles, m, tn), F32)],
        compiler_params=_params("arbitrary"),
        name="out_proj_cast",
    )(mix, w, x, g.reshape(1, D_MODEL))


def _accumulate(o_ref, ssq_ref, part):
    new = o_ref[...] + part
    o_ref[...] = new
    sq = new * new
    ssq_ref[...] = functools.reduce(
        jnp.add, [sq[:, c:c + HEAD_DIM] for c in range(0, sq.shape[1], HEAD_DIM)])


def _residual_norm(x_ref, o_ref, ssq_ref, g_ref):
    ms = jnp.sum(ssq_ref[...], axis=-1, keepdims=True) * (1.0 / o_ref.shape[1])
    o_ref[...] = x_ref[...] + o_ref[...] * lax.rsqrt(ms + EPS) * g_ref[...]


def _ffn_kernel(x_ref, gpre_ref, gpost_ref, wg_ref, wl_ref, wd_ref, o_ref, xn_ref, ssq_ref):
    f = pl.program_id(1)

    @pl.when(f == 0)
    def _():
        xn_ref[...] = _rms(x_ref[...], gpre_ref[...]).astype(BF16)
        o_ref[...] = jnp.zeros_like(o_ref)

    xn = xn_ref[...]
    hg = jnp.dot(xn, wg_ref[...], preferred_element_type=F32)
    hl = jnp.dot(xn, wl_ref[...], preferred_element_type=F32)
    a = (hg * jax.nn.sigmoid(hg) * hl).astype(BF16)
    _accumulate(o_ref, ssq_ref, jnp.dot(a, wd_ref[...], preferred_element_type=F32))

    @pl.when(f == pl.num_programs(1) - 1)
    def _():
        _residual_norm(x_ref, o_ref, ssq_ref, gpost_ref)


def _ffn(x, g_pre, g_post, weights, *, tm, tf, first_tile):
    m = x.shape[0]
    nf = D_FF // tf
    rows = pl.BlockSpec((tm, D_MODEL), lambda i, f: (i + first_tile, 0))
    return pl.pallas_call(
        _ffn_kernel,
        grid=(m // tm - first_tile, nf),
        in_specs=[
            rows,
            pl.BlockSpec((1, D_MODEL), lambda i, f: (0, 0)),
            pl.BlockSpec((1, D_MODEL), lambda i, f: (0, 0)),
            pl.BlockSpec((D_MODEL, tf), lambda i, f: (0, f)),
            pl.BlockSpec((D_MODEL, tf), lambda i, f: (0, f)),
            pl.BlockSpec((tf, D_MODEL), lambda i, f: (f, 0)),
        ],
        out_specs=rows,
        out_shape=jax.ShapeDtypeStruct((m, D_MODEL), F32),
        input_output_aliases={0: 0},
        scratch_shapes=[pltpu.VMEM((tm, D_MODEL), BF16), pltpu.VMEM((tm, HEAD_DIM), F32)],
        compiler_params=_params("parallel", "arbitrary", vmem_limit=FFN_VMEM_LIMIT),
        name="ffn",
    )(x, g_pre.reshape(1, D_MODEL), g_post.reshape(1, D_MODEL), *weights)


def _ffn_head_kernel(x_ref, xs_ref, gpre_ref, gpost_ref, wg_ref, wl_ref, wd_ref,
                     o_ref, os_ref, wgb_ref, wlb_ref, wdb_ref, xn_ref, ssq_ref, ssqs_ref):
    tm = x_ref.shape[0]
    f = pl.program_id(0)

    @pl.when(f == 0)
    def _():
        xn_ref[:tm, :] = _rms(x_ref[...], gpre_ref[...]).astype(BF16)
        xn_ref[tm:, :] = _rms(xs_ref[...], gpre_ref[...]).astype(BF16)
        o_ref[...] = jnp.zeros_like(o_ref)
        os_ref[...] = jnp.zeros_like(os_ref)

    wg, wl, wd = (r[...].astype(BF16) for r in (wg_ref, wl_ref, wd_ref))
    wgb_ref[...] = wg
    wlb_ref[...] = wl
    wdb_ref[...] = wd
    xn = xn_ref[...]
    hg = jnp.dot(xn, wg, preferred_element_type=F32)
    hl = jnp.dot(xn, wl, preferred_element_type=F32)
    a = (hg * jax.nn.sigmoid(hg) * hl).astype(BF16)
    part = jnp.dot(a, wd, preferred_element_type=F32)
    _accumulate(o_ref, ssq_ref, part[:tm, :])
    _accumulate(os_ref, ssqs_ref, part[tm:, :])

    @pl.when(f == pl.num_programs(0) - 1)
    def _():
        _residual_norm(x_ref, o_ref, ssq_ref, gpost_ref)
        _residual_norm(xs_ref, os_ref, ssqs_ref, gpost_ref)


def _ffn_head(x, xs, g_pre, g_post, w_up, w_down, layer, *, tm, tf):
    m = x.shape[0]
    ms = xs.shape[0]
    nf = D_FF // tf
    once = dict(pipeline_mode=pl.Buffered(1))
    head = pl.BlockSpec((tm, D_MODEL), lambda f: (0, 0), **once)
    sample = pl.BlockSpec((ms, D_MODEL), lambda f: (0, 0))
    vec = pl.BlockSpec((1, D_MODEL), lambda f: (0, 0))
    outs = pl.pallas_call(
        _ffn_head_kernel,
        grid=(nf,),
        in_specs=[
            head, sample, vec, vec,
            pl.BlockSpec((None, D_MODEL, tf), lambda f: (layer, 0, f)),
            pl.BlockSpec((None, D_MODEL, tf), lambda f: (layer, 0, nf + f)),
            pl.BlockSpec((None, tf, D_MODEL), lambda f: (layer, f, 0)),
        ],
        out_specs=[
            head, sample,
            pl.BlockSpec((D_MODEL, tf), lambda f: (0, f)),
            pl.BlockSpec((D_MODEL, tf), lambda f: (0, f)),
            pl.BlockSpec((tf, D_MODEL), lambda f: (f, 0)),
        ],
        out_shape=[
            jax.ShapeDtypeStruct((m, D_MODEL), F32),
            jax.ShapeDtypeStruct((ms, D_MODEL), F32),
            jax.ShapeDtypeStruct((D_MODEL, D_FF), BF16),
            jax.ShapeDtypeStruct((D_MODEL, D_FF), BF16),
            jax.ShapeDtypeStruct((D_FF, D_MODEL), BF16),
        ],
        input_output_aliases={0: 0},
        scratch_shapes=[pltpu.VMEM((tm + ms, D_MODEL), BF16), pltpu.VMEM((tm, HEAD_DIM), F32),
                        pltpu.VMEM((ms, HEAD_DIM), F32)],
        compiler_params=_params("arbitrary"),
        name="ffn_head",
    )(x, xs, g_pre.reshape(1, D_MODEL), g_post.reshape(1, D_MODEL), w_up, w_up, w_down)
    return outs[0], outs[1], tuple(outs[2:])


def _swa_kernel(q_ref, k_ref, v_ref, tb_ref, o_ref, lse_ref):
    n_units, n_res, u, _ = q_ref.shape
    per_blk = N_BACK // u
    n_blk = n_units // per_blk
    lane = lax.broadcasted_iota(jnp.int32, (N_BACK, HEAD_DIM), 1)
    ones = jnp.ones((2 * N_BACK, HEAD_DIM), BF16)

    def rows(ref, res, unit0, n_rows, lo, hi):
        return ref[pl.ds(unit0, n_rows // u), res, :, lo:hi].reshape(n_rows, hi - lo)

    def block(res, qu, ku, table):
        n_keys = N_BACK if table == 0 else 2 * N_BACK
        lse_tile = jnp.zeros((N_BACK, HEAD_DIM), F32)
        for h in range(HEADS_PER_GROUP):
            lo, hi = h * HEAD_DIM, (h + 1) * HEAD_DIM
            q = rows(q_ref, res, qu, N_BACK, lo, hi)
            kw = rows(k_ref, res, ku, n_keys, lo, hi)
            vw = rows(v_ref, res, ku, n_keys, lo, hi)
            s = lax.dot_general(q, kw, (((1,), (1,)), ((), ())), preferred_element_type=F32)
            s = s + tb_ref[table, h][:, :n_keys]
            m = jnp.max(s, axis=1, keepdims=True)
            p = jnp.exp(s - m).astype(BF16)
            ov = jnp.dot(p, jnp.concatenate([vw, ones[:n_keys]], axis=1), preferred_element_type=F32)
            o = ov[:, :HEAD_DIM]
            o_ref[pl.ds(qu, per_blk), res, :, lo:hi] = o.reshape(per_blk, u, HEAD_DIM).astype(o_ref.dtype)
            den = ov[:, HEAD_DIM:]
            lse_tile = jnp.where(lane == h, m, jnp.where(lane == HEADS_PER_GROUP + h, den, lse_tile))
        lse_ref[pl.ds(qu, per_blk), res, :, :] = lse_tile.reshape(per_blk, u, HEAD_DIM)

    def block_at(res, n):
        block(res, n * per_blk, (n - 1) * per_blk, 1)

    n_trips = (n_blk - 1) // SWA_BLOCKS_PER_TRIP
    for res in range(n_res):
        block(res, 0, 0, 0)

        def body(i, carry, res=res):
            for b in range(SWA_BLOCKS_PER_TRIP):
                block_at(res, 1 + SWA_BLOCKS_PER_TRIP * i + b)
            return carry

        if n_trips > 0:
            lax.fori_loop(0, n_trips, body, 0)
        for n in range(1 + n_trips * SWA_BLOCKS_PER_TRIP, n_blk):
            block_at(res, n)


def _sub_block(dil):
    return N_BACK if dil == 1 else PERM_BLOCK


def _swa_group(zb, tables, g, *, batch, seq):
    dil = SWA_PATTERN[g][1]
    sub = _sub_block(dil)
    n_units, u = seq // sub, sub // dil
    n_blk = n_units * u // N_BACK
    n_res = min(dil, max(1, SWA_BLOCKS_PER_TRIP // n_blk))
    view = zb.reshape(zb.shape[0], batch, n_units, dil, u, GROUP_WIDTH)

    def rows_in(tile):
        return pl.BlockSpec((None, None, n_units, n_res, u, GROUP_WIDTH), lambda i, r: (tile, i, 0, r, 0, 0))

    def rows_out(width):
        return pl.BlockSpec((None, n_units, n_res, u, width), lambda i, r: (i, 0, r, 0, 0))

    return pl.pallas_call(
        _swa_kernel,
        grid=(batch, dil // n_res),
        in_specs=[rows_in(g), rows_in(N_SWA_GROUPS + g), rows_in(2 * N_SWA_GROUPS + g),
                  pl.BlockSpec((None, 2, HEADS_PER_GROUP, N_BACK, 2 * N_BACK), lambda i, r: (g, 0, 0, 0, 0))],
        out_specs=[rows_out(GROUP_WIDTH), rows_out(HEAD_DIM)],
        out_shape=[
            jax.ShapeDtypeStruct((batch, n_units, dil, u, GROUP_WIDTH), BF16),
            jax.ShapeDtypeStruct((batch, n_units, dil, u, HEAD_DIM), F32),
        ],
        compiler_params=_params("parallel", "parallel"),
        name=f"swa_group{g}",
    )(view, view, view, tables)


def _split3(x):
    hi = x.astype(BF16)
    rest = x - hi.astype(F32)
    mid = rest.astype(BF16)
    lo = (rest - mid.astype(F32)).astype(BF16)
    return hi, mid, lo


def _merge_stage(o0_ref, o1_ref, o2_ref, l0_ref, l1_ref, l2_ref, q_ref, kb_ref, vb_ref, o_ref):
    tm = o_ref.shape[0]
    group_refs = ((o0_ref, l0_ref), (o1_ref, l1_ref), (o2_ref, l2_ref))
    outs, lses = [None] * N_SWA_GROUPS, [None] * N_SWA_GROUPS

    def token_order(g):
        o_g, l_g = group_refs[g]
        dil = SWA_PATTERN[g][1]
        o = o_g[...].reshape(tm, GROUP_WIDTH)
        l = l_g[...].reshape(tm, HEAD_DIM)
        if dil > 1:
            inv = _residue_major_perm(PERM_BLOCK, dil, transpose=True)
            l3 = _split3(l)
            o_nat, l_nat = [], []
            for s in range(0, tm, PERM_BLOCK):
                o_nat.append(jnp.dot(inv, o[s:s + PERM_BLOCK, :], preferred_element_type=F32))
                l_nat.append(sum(jnp.dot(inv, t[s:s + PERM_BLOCK, :], preferred_element_type=F32) for t in l3))
            o = jnp.concatenate(o_nat, axis=0)
            l = jnp.concatenate(l_nat, axis=0)
        outs[g] = o.astype(F32)
        lses[g] = l

    def merge_head(h):
        lo, hi = h * HEAD_DIM, (h + 1) * HEAD_DIM
        ls = [l[:, h:h + 1] for l in lses]
        ds = [l[:, HEADS_PER_GROUP + h:HEADS_PER_GROUP + h + 1] for l in lses]
        mx = jnp.maximum(jnp.maximum(ls[0], ls[1]), ls[2])
        es = [jnp.exp(l - mx) for l in ls]
        tot = ds[0] * es[0] + ds[1] * es[1] + ds[2] * es[2]
        for g in range(N_SWA_GROUPS):
            alpha = es[g] / tot
            o_ref[:, g * GROUP_WIDTH + lo:g * GROUP_WIDTH + hi] = (outs[g][:, lo:hi] * alpha).astype(o_ref.dtype)

    return ([functools.partial(token_order, g) for g in range(N_SWA_GROUPS)]
            + [functools.partial(merge_head, h) for h in range(HEADS_PER_GROUP)]
            + _mem_attention_parts(q_ref, kb_ref, vb_ref, o_ref, MIXER_WIDTH))


def _swa_merge_out(outs, lses, zb, mem_kv, layer, w, x, g, *, rows_per_batch):
    tm = ROW_TILE
    tiles_per_batch = rows_per_batch // tm

    def group_tile(width, dil):
        sub = _sub_block(dil)
        return lambda tile: pl.BlockSpec(
            (None, tm // sub, dil, sub // dil, width),
            lambda s: (tile(s) // tiles_per_batch, tile(s) % tiles_per_batch, 0, 0, 0))

    specs = ([group_tile(GROUP_WIDTH, dil) for _, dil in SWA_PATTERN]
             + [group_tile(HEAD_DIM, dil) for _, dil in SWA_PATTERN]
             + [lambda tile: pl.BlockSpec((None, tm, MEM_WIDTH), lambda s: (3 * N_SWA_GROUPS, tile(s), 0))])
    return _mix_out(_merge_stage, (*outs, *lses, zb), specs, [], mem_kv, layer, w, x, g,
                    tm=tm, rows_per_batch=rows_per_batch, name="swa_merge_out")


def _head_rows(row, col0):
    return jnp.concatenate([row[:, col0 + h * HEAD_DIM:col0 + (h + 1) * HEAD_DIM]
                            for h in range(HEADS_PER_GROUP)], axis=0)


def _sample_attention(q4, kv_ref, bias=None, new=None):
    k3, v3 = kv_ref[:, 0], kv_ref[:, 1]
    s = jnp.sum(k3 * q4[None], axis=-1, keepdims=True) * ATTN_SCALE
    if bias is not None:
        s = s + bias
    m = jnp.max(s, axis=0)
    if new is not None:
        k_new, v_new, b_new = new
        s_new = jnp.sum(k_new * q4, axis=-1, keepdims=True) * ATTN_SCALE + b_new
        m = jnp.maximum(m, s_new)
    p = jnp.exp(s - m[None])
    den = jnp.sum(p, axis=0)
    o = jnp.sum(p * v3, axis=0)
    if new is not None:
        p_new = jnp.exp(s_new - m)
        den = den + p_new
        o = o + p_new * v_new
    return o / den, m + jnp.log(den)


def _store_head_rows(o_ref, col0, x4):
    for h in range(HEADS_PER_GROUP):
        o_ref[:, col0 + h * HEAD_DIM:col0 + (h + 1) * HEAD_DIM] = x4[h:h + 1, :]


def _sample_mem_attention(q_row, kv_ref, o_ref, col0):
    o, _ = _sample_attention(_head_rows(q_row, 0), kv_ref)
    _store_head_rows(o_ref, col0, o)


def _sample_mix_a_kernel(z_ref, gv_ref, w0_ref, b0_ref, kv_ref, o_ref, vrow_ref):
    u = z_ref[:, 0:MIXER_WIDTH]
    v = _rms(z_ref[:, MIXER_WIDTH:2 * MIXER_WIDTH], gv_ref[...])
    vrow_ref[...] = v
    o_ref[:, 0:MIXER_WIDTH] = u * (w0_ref[...] * v + b0_ref[...])
    _sample_mem_attention(z_ref[:, 2 * MIXER_WIDTH:2 * MIXER_WIDTH + MEM_WIDTH], kv_ref, o_ref, MIXER_WIDTH)


def _sample_mix_a(z, g_v, w_s, b_s, mem_kv, layer):
    bd = mem_kv.shape[1]
    w0 = jnp.repeat(w_s[:, 0, 0], GROUP_DIM_A).reshape(1, MIXER_WIDTH)
    b0 = jnp.repeat(b_s[:, 0], GROUP_DIM_A).reshape(1, MIXER_WIDTH)
    width = z.shape[1]
    vec = lambda i: (0, 0)
    return pl.pallas_call(
        _sample_mix_a_kernel,
        grid=(bd,),
        in_specs=[
            pl.BlockSpec((None, 1, width), lambda i: (i, 0, 0)),
            pl.BlockSpec((1, MIXER_WIDTH), vec),
            pl.BlockSpec((1, MIXER_WIDTH), vec),
            pl.BlockSpec((1, MIXER_WIDTH), vec),
            _mem_kv_spec(layer, lambda i: i),
        ],
        out_specs=[
            pl.BlockSpec((None, 1, D_MODEL), lambda i: (i, 0, 0)),
            pl.BlockSpec((None, 1, MIXER_WIDTH), lambda i: (i, 0, 0)),
        ],
        out_shape=[
            jax.ShapeDtypeStruct((bd, 1, D_MODEL), F32),
            jax.ShapeDtypeStruct((bd, 1, MIXER_WIDTH), F32),
        ],
        compiler_params=_params("parallel"),
        name="sample_mix_a",
    )(z[:bd].reshape(bd, 1, width), g_v.reshape(1, MIXER_WIDTH), w0, b0, mem_kv)


def _sample_mix_b_kernel(z_ref, c0_ref, c1_ref, c2_ref, bcol_ref, bnew_ref, kv_ref, o_ref):
    caches = (c0_ref, c1_ref, c2_ref)
    z = z_ref[...]
    outs, lses = [], []
    for g in range(N_SWA_GROUPS):
        c0 = g * GROUP_WIDTH
        new = (_head_rows(z, MIXER_WIDTH + c0), _head_rows(z, 2 * MIXER_WIDTH + c0), bnew_ref[g])
        o, lse = _sample_attention(_head_rows(z, c0), caches[g], bias=bcol_ref[g], new=new)
        outs.append(o)
        lses.append(lse)
    mx = jnp.maximum(jnp.maximum(lses[0], lses[1]), lses[2])
    es = [jnp.exp(l - mx) for l in lses]
    tot = es[0] + es[1] + es[2]
    for g in range(N_SWA_GROUPS):
        _store_head_rows(o_ref, g * GROUP_WIDTH, outs[g] * (es[g] / tot))
    _sample_mem_attention(z[:, 3 * MIXER_WIDTH:3 * MIXER_WIDTH + MEM_WIDTH], kv_ref, o_ref, MIXER_WIDTH)


def _sample_mix_b(z, win_caches, swa_layer, bias_groups, mem_kv, layer):
    bd = mem_kv.shape[1]
    width = z.shape[1]
    cache_views, cache_specs = [], []
    for g, (win, dil) in enumerate(SWA_PATTERN):
        c = win_caches[g]
        cache_views.append(c.reshape(c.shape[0], bd, win // dil, dil, 2, HEADS_PER_GROUP, HEAD_DIM))
        cache_specs.append(pl.BlockSpec((None, None, N_BACK, None, 2, HEADS_PER_GROUP, HEAD_DIM),
                                        lambda i: (swa_layer, i, 0, 0, 0, 0, 0)))
    bcol = jnp.stack([bg[:, N_BACK:0:-1].T for bg in bias_groups], axis=0)
    bcol = jnp.broadcast_to(bcol[..., None], bcol.shape + (HEAD_DIM,))
    bnew = jnp.stack([bg[:, 0] for bg in bias_groups], axis=0)
    bnew = jnp.broadcast_to(bnew[..., None], bnew.shape + (HEAD_DIM,))
    return pl.pallas_call(
        _sample_mix_b_kernel,
        grid=(bd,),
        in_specs=[pl.BlockSpec((None, 1, width), lambda i: (i, 0, 0))] + cache_specs + [
            pl.BlockSpec((N_SWA_GROUPS, N_BACK, HEADS_PER_GROUP, HEAD_DIM), lambda i: (0, 0, 0, 0)),
            pl.BlockSpec((N_SWA_GROUPS, HEADS_PER_GROUP, HEAD_DIM), lambda i: (0, 0, 0)),
            _mem_kv_spec(layer, lambda i: i),
        ],
        out_specs=pl.BlockSpec((None, 1, D_MODEL), lambda i: (i, 0, 0)),
        out_shape=jax.ShapeDtypeStruct((bd, 1, D_MODEL), F32),
        compiler_params=_params("parallel"),
        name="sample_mix_b",
    )(z[:bd].reshape(bd, 1, width), *cache_views, bcol, bnew, mem_kv)


def _t5_bucket(dist):
    nf = jnp.maximum(dist, MAX_EXACT).astype(F32)
    large = MAX_EXACT + (jnp.log(nf / MAX_EXACT) / math.log(MAX_DISTANCE / MAX_EXACT)
                         * (N_BUCKETS - MAX_EXACT)).astype(jnp.int32)
    large = jnp.minimum(large, N_BUCKETS - 1)
    return jnp.where(dist < MAX_EXACT, dist, large)


def _group_bias(rel_bias, g, dil):
    dist = jnp.arange(N_BACK + 1, dtype=jnp.int32) * dil
    b = rel_bias[_t5_bucket(dist)][:, g * HEADS_PER_GROUP:(g + 1) * HEADS_PER_GROUP]
    return b.T.astype(F32)


def _band_tables_kernel(b_ref, o_ref):
    n, rows, width = o_ref.shape
    for x in range(n):
        row = jnp.broadcast_to(b_ref[x:x + 1, :], (rows, width))
        o_ref[x] = pltpu.roll(row, 0, 1, stride=1, stride_axis=0)


def _band_tables(bias_groups):
    width = 2 * N_BACK
    rows = []
    for bias_j in bias_groups:
        masked = jnp.full((HEADS_PER_GROUP, N_BACK - 1), NEG_INF, F32)
        rows.append(jnp.concatenate([bias_j[:, :1], masked, bias_j[:, N_BACK:0:-1]], axis=1))
        rows.append(jnp.concatenate([bias_j[:, ::-1], masked], axis=1))
    base = jnp.stack(rows, axis=0).reshape(-1, width)
    tabs = pl.pallas_call(
        _band_tables_kernel,
        out_shape=jax.ShapeDtypeStruct((base.shape[0], N_BACK, width), F32),
        name="band_tables",
    )(base)
    return tabs.reshape(N_SWA_GROUPS, 2, HEADS_PER_GROUP, N_BACK, width)


def _kv_tail_kernel(k_ref, v_ref, o_ref, *, dil):
    rows = k_ref.shape[0]
    for kv, ref in enumerate((k_ref, v_ref)):
        x = ref[...]
        if dil > 1:
            inv = _residue_major_perm(PERM_BLOCK, dil, transpose=True)
            x = jnp.concatenate([jnp.dot(inv, x[s:s + PERM_BLOCK, :], preferred_element_type=F32)
                                 for s in range(0, rows, PERM_BLOCK)], axis=0)
        x = x.astype(F32)
        for h in range(HEADS_PER_GROUP):
            o_ref[:, kv, h, :] = x[:, h * HEAD_DIM:(h + 1) * HEAD_DIM]


def _kv_tail(zb, g, *, batch, seq):
    win, dil = SWA_PATTERN[g]
    rows = min(win, FFN_ROW_TILE)
    first = (seq - win) // rows
    per_batch = seq // rows

    def tile(t):
        return pl.BlockSpec((None, rows, GROUP_WIDTH), lambda b, s: (t, b * per_batch + first + s, 0))

    return pl.pallas_call(
        functools.partial(_kv_tail_kernel, dil=dil),
        grid=(batch, win // rows),
        in_specs=[tile(N_SWA_GROUPS + g), tile(2 * N_SWA_GROUPS + g)],
        out_specs=pl.BlockSpec((None, rows, 2, HEADS_PER_GROUP, HEAD_DIM), lambda b, s: (b, s, 0, 0, 0)),
        out_shape=jax.ShapeDtypeStruct((batch, win, 2, HEADS_PER_GROUP, HEAD_DIM), F32),
        compiler_params=_params("parallel", "parallel"),
        name=f"kv_tail{g}",
    )(zb, zb)


def kernel(x_prompt, x_sample, mem_prompt, cache_mem_kv, cache_win128_kv, cache_win512_kv, cache_win2048_kv, rel_bias, norm_mix_pre, norm_mix_post, norm_ffn_pre, norm_ffn_post, norm_mem, w_mem_kv, w_in_a, norm_v_a, w_spatial_a, b_spatial_a, w_in_b, w_out, w_ffn_up, w_ffn_down):
    batch, seq, _ = x_prompt.shape
    bd = x_sample.shape[0]
    depth = w_out.shape[0]
    m_p = batch * seq
    win_caches = (cache_win128_kv, cache_win512_kv, cache_win2048_kv)

    bias_groups = [_group_bias(rel_bias, g, dil) for g, (_, dil) in enumerate(SWA_PATTERN)]
    band_tables = _band_tables(bias_groups)

    yp = x_prompt.reshape(m_p, D_MODEL)
    ys = jnp.pad(x_sample.reshape(bd, D_MODEL), ((0, SAMPLE_PAD - bd), (0, 0)))
    mem_rows = mem_prompt.reshape(batch * N_MEM, D_MODEL)

    mem_kv_p = _mem_kv(mem_rows, norm_mem, w_mem_kv, batch=batch)
    chunk_v_s = []
    win_p = [[] for _ in SWA_PATTERN]
    win_s = [[] for _ in SWA_PATTERN]
    for i in range(depth):
        li = i // 2
        if i % 2 == 0:
            zp, zs = _in_proj_a(yp, ys, norm_mix_pre[i], w_in_a, li, tm=ROW_TILE, tn=CAST_TILE,
                                gelu_cols=2 * MIXER_WIDTH)
            mix_s, v_rows = _sample_mix_a(zs, norm_v_a[li], w_spatial_a[li], b_spatial_a[li], cache_mem_kv, i)
            chunk_v_s.append(v_rows)
            ys, w_o = _sample_out_proj(mix_s, w_out, i, ys, norm_mix_post[i])
            yp = _gmlp_mix_out(zp, norm_v_a[li], w_spatial_a[li], b_spatial_a[li], mem_kv_p, i,
                               w_o, yp, norm_mix_post[i], tm=ROW_TILE, rows_per_batch=seq)
        else:
            zb, zs = _in_proj_b(yp, ys, norm_mix_pre[i], w_in_b, li, tm=ROW_TILE)
            outs, lses = [], []
            for g, (win, dil) in enumerate(SWA_PATTERN):
                o, lse = _swa_group(zb, band_tables, g, batch=batch, seq=seq)
                outs.append(o)
                lses.append(lse)
                win_p[g].append(_kv_tail(zb, g, batch=batch, seq=seq))
                kv_new = zs[:bd, MIXER_WIDTH:3 * MIXER_WIDTH]
                kv_new = kv_new.reshape(bd, 1, 2, N_SWA_GROUPS, HEADS_PER_GROUP, HEAD_DIM)[:, :, :, g]
                win_s[g].append(kv_new)
            mix_s = _sample_mix_b(zs, win_caches, li, bias_groups, cache_mem_kv, i)
            ys, w_o = _sample_out_proj(mix_s, w_out, i, ys, norm_mix_post[i])
            yp = _swa_merge_out(outs, lses, zb, mem_kv_p, i, w_o, yp, norm_mix_post[i], rows_per_batch=seq)
        yp, ys, w_ffn = _ffn_head(yp, ys, norm_ffn_pre[i], norm_ffn_post[i], w_ffn_up, w_ffn_down, i,
                                  tm=FFN_ROW_TILE, tf=HEAD_FF_TILE)
        yp = _ffn(yp, norm_ffn_pre[i], norm_ffn_post[i], w_ffn, tm=FFN_ROW_TILE, tf=FF_TILE, first_tile=1)

    return (
        yp.reshape(batch, seq, D_MODEL),
        ys[:bd].reshape(bd, 1, D_MODEL),
        mem_kv_p,
        jnp.stack(chunk_v_s, axis=0),
        jnp.stack(win_p[0], axis=0),
        jnp.stack(win_p[1], axis=0),
        jnp.stack(win_p[2], axis=0),
        jnp.stack(win_s[0], axis=0),
        jnp.stack(win_s[1], axis=0),
        jnp.stack(win_s[2], axis=0),
    )
```

```python
import functools
import math

import jax
import jax.numpy as jnp
from jax import lax
from jax.experimental import pallas as pl
from jax.experimental.pallas import tpu as pltpu

F32 = jnp.float32
BF16 = jnp.bfloat16

D_MODEL = 2048
HEAD_DIM = 128
N_MEM = 256
N_MEM_HEADS = 4
MEM_WIDTH = N_MEM_HEADS * HEAD_DIM
MIXER_WIDTH = D_MODEL - MEM_WIDTH
CHUNK = 128
N_GROUPS_A = 4
GROUP_DIM_A = MIXER_WIDTH // N_GROUPS_A
SWA_PATTERN = ((128, 1), (512, 4), (2048, 16))
N_SWA_GROUPS = len(SWA_PATTERN)
HEADS_PER_GROUP = 4
GROUP_WIDTH = HEADS_PER_GROUP * HEAD_DIM
N_BACK = 128
N_BUCKETS = 32
MAX_EXACT = N_BUCKETS // 2
MAX_DISTANCE = 2048
D_FF = 5632
EPS = 1e-6
NEG_INF = -1e30
ATTN_SCALE = HEAD_DIM ** -0.5
SAMPLE_PAD = 16
PERM_BLOCK = 256
SWA_BLOCKS_PER_TRIP = 16
OUT_CHUNK = 256

ROW_TILE = 512
FFN_ROW_TILE = 1024
FF_TILE = 512
HEAD_FF_TILE = 256
CAST_TILE = 512

V7X_VMEM_BYTES = 64 * 1024 * 1024
VMEM_LIMIT = V7X_VMEM_BYTES - 8 * 1024 * 1024
FFN_VMEM_LIMIT = V7X_VMEM_BYTES - 2 * 1024 * 1024


def _params(*sem, vmem_limit=VMEM_LIMIT):
    return pltpu.CompilerParams(dimension_semantics=sem, vmem_limit_bytes=vmem_limit)


def _gelu(x):
    return 0.5 * x * (1.0 + jnp.tanh(0.7978845608028654 * (x + 0.044715 * (x * x * x))))


def _rms(x, g):
    return x * lax.rsqrt(jnp.mean(x * x, axis=-1, keepdims=True) + EPS) * g


def _log2(n):
    assert n & (n - 1) == 0
    return n.bit_length() - 1


def _residue_major_perm(tm, dil, transpose=False):
    n = tm // dil
    row = lax.broadcasted_iota(jnp.int32, (tm, tm), 0)
    col = lax.broadcasted_iota(jnp.int32, (tm, tm), 1)
    dst, src = (col, row) if transpose else (row, col)
    want = lax.shift_left(jnp.bitwise_and(dst, n - 1), _log2(dil)) + lax.shift_right_logical(dst, _log2(n))
    return (src == want).astype(BF16)


def _in_proj_a_kernel(x_ref, xs_ref, g_ref, w_ref, zp_ref, zs_ref, wb_ref, xn_ref, *, n_col, gelu_cols):
    tm = x_ref.shape[0]
    tn = w_ref.shape[1]
    s = pl.program_id(0)

    def prompt_cols(acc, c0):
        col = lax.broadcasted_iota(jnp.int32, acc.shape, 1) + c0
        return jnp.where(col < gelu_cols, _gelu(acc), acc * ATTN_SCALE).astype(zp_ref.dtype)

    @pl.when(s == 0)
    def _():
        xn_ref[:tm, :] = _rms(x_ref[...], g_ref[...]).astype(BF16)
        xn_ref[tm:, :] = _rms(xs_ref[...], g_ref[...]).astype(BF16)

    for j in range(n_col):
        @pl.when(s == j)
        def _(j=j):
            cols = slice(j * tn, (j + 1) * tn)
            w = w_ref[...].astype(BF16)
            wb_ref[:, cols] = w
            acc = jnp.dot(xn_ref[...], w, preferred_element_type=F32)
            zp_ref[:, cols] = prompt_cols(acc[:tm, :], j * tn)
            sample = acc[tm:, :]
            zs_ref[:, cols] = _gelu(sample) if (j + 1) * tn <= gelu_cols else sample

    @pl.when(s >= n_col)
    def _():
        xn = _rms(x_ref[...], g_ref[...]).astype(BF16)
        acc = jnp.dot(xn, wb_ref[...], preferred_element_type=F32)
        zp_ref[...] = prompt_cols(acc, 0)


def _in_proj_a(x, xs, g, w, layer, *, tm, tn, gelu_cols):
    m, k = x.shape
    ms = xs.shape[0]
    n = w.shape[2]
    n_col, n_row = n // tn, m // tm
    assert gelu_cols % tn == 0
    row = lambda s: jnp.maximum(s - (n_col - 1), 0)
    return pl.pallas_call(
        functools.partial(_in_proj_a_kernel, n_col=n_col, gelu_cols=gelu_cols),
        grid=(n_col + n_row - 1,),
        in_specs=[
            pl.BlockSpec((tm, k), lambda s: (row(s), 0)),
            pl.BlockSpec((ms, k), lambda s: (0, 0)),
            pl.BlockSpec((1, k), lambda s: (0, 0)),
            pl.BlockSpec((None, k, tn), lambda s: (layer, 0, jnp.minimum(s, n_col - 1))),
        ],
        out_specs=[
            pl.BlockSpec((tm, n), lambda s: (row(s), 0)),
            pl.BlockSpec((ms, n), lambda s: (0, 0)),
        ],
        out_shape=[
            jax.ShapeDtypeStruct((m, n), BF16),
            jax.ShapeDtypeStruct((ms, n), F32),
        ],
        scratch_shapes=[pltpu.VMEM((k, n), BF16), pltpu.VMEM((tm + ms, k), BF16)],
        compiler_params=_params("arbitrary"),
        name="in_proj_a",
    )(x, xs, g.reshape(1, k), w)


def _mem_kv_kernel(x_ref, g_ref, w_ref, o_ref):
    xn = _rms(x_ref[...], g_ref[...]).astype(BF16)
    acc = jnp.dot(xn, w_ref[...].astype(BF16), preferred_element_type=F32)
    for kv in range(2):
        for h in range(N_MEM_HEADS):
            c0 = (kv * N_MEM_HEADS + h) * HEAD_DIM
            o_ref[:, kv, h, :] = acc[:, c0:c0 + HEAD_DIM]


def _mem_kv(mem_rows, g, w, *, batch):
    m, k = mem_rows.shape
    layers = w.shape[0]
    out = pl.pallas_call(
        _mem_kv_kernel,
        grid=(layers,),
        in_specs=[
            pl.BlockSpec((m, k), lambda l: (0, 0)),
            pl.BlockSpec((None, 1, k), lambda l: (l, 0, 0)),
            pl.BlockSpec((None, k, 2 * MEM_WIDTH), lambda l: (l, 0, 0)),
        ],
        out_specs=pl.BlockSpec((None, m, 2, N_MEM_HEADS, HEAD_DIM), lambda l: (l, 0, 0, 0, 0)),
        out_shape=jax.ShapeDtypeStruct((layers, m, 2, N_MEM_HEADS, HEAD_DIM), F32),
        compiler_params=_params("parallel"),
        name="mem_kv",
    )(mem_rows, g.reshape(layers, 1, k), w)
    return out.reshape(layers, batch, m // batch, 2, N_MEM_HEADS, HEAD_DIM)


def _in_proj_b_kernel(x_ref, xs_ref, g_ref, w_ref, o_ref, zs_ref, wb_ref, xn_ref):
    tm = x_ref.shape[0]
    n_tiles = o_ref.shape[0]
    n_qkv = 3 * N_SWA_GROUPS
    s = pl.program_id(0)

    def row_orders():
        xn = _rms(x_ref[...], g_ref[...]).astype(BF16)
        xn_ref[0, :tm, :] = xn
        for g in range(1, N_SWA_GROUPS):
            perm = _residue_major_perm(PERM_BLOCK, SWA_PATTERN[g][1])
            for r in range(0, tm, PERM_BLOCK):
                xn_ref[g, r:r + PERM_BLOCK, :] = jnp.dot(
                    perm, xn[r:r + PERM_BLOCK, :], preferred_element_type=F32).astype(BF16)

    def prompt_tile(t, acc):
        if t < N_SWA_GROUPS or t == n_qkv:
            acc = acc * ATTN_SCALE
        o_ref[t] = acc.astype(o_ref.dtype)

    src = lambda t: t % N_SWA_GROUPS if t < n_qkv else 0

    @pl.when(s == 0)
    def _():
        row_orders()
        xs = _rms(xs_ref[...], g_ref[...]).astype(BF16)
        for g in range(N_SWA_GROUPS):
            xn_ref[g, tm:, :] = xs

    for t in range(n_tiles):
        @pl.when(s == t)
        def _(t=t):
            cols = slice(t * GROUP_WIDTH, (t + 1) * GROUP_WIDTH)
            w = w_ref[...].astype(BF16)
            wb_ref[:, cols] = w
            acc = jnp.dot(xn_ref[src(t)], w, preferred_element_type=F32)
            prompt_tile(t, acc[:tm, :])
            zs_ref[:, cols] = acc[tm:, :]

    @pl.when(s >= n_tiles)
    def _():
        row_orders()
        for t in range(n_tiles):
            w = wb_ref[:, t * GROUP_WIDTH:(t + 1) * GROUP_WIDTH]
            prompt_tile(t, jnp.dot(xn_ref[src(t), :tm, :], w, preferred_element_type=F32))


def _in_proj_b(x, xs, g, w, layer, *, tm):
    m, k = x.shape
    ms = xs.shape[0]
    n = w.shape[2]
    n_tiles = n // GROUP_WIDTH
    row = lambda s: jnp.maximum(s - (n_tiles - 1), 0)
    return pl.pallas_call(
        _in_proj_b_kernel,
        grid=(n_tiles + m // tm - 1,),
        in_specs=[
            pl.BlockSpec((tm, k), lambda s: (row(s), 0)),
            pl.BlockSpec((ms, k), lambda s: (0, 0)),
            pl.BlockSpec((1, k), lambda s: (0, 0)),
            pl.BlockSpec((None, k, GROUP_WIDTH), lambda s: (layer, 0, jnp.minimum(s, n_tiles - 1))),
        ],
        out_specs=[
            pl.BlockSpec((n_tiles, tm, GROUP_WIDTH), lambda s: (0, row(s), 0)),
            pl.BlockSpec((ms, n), lambda s: (0, 0)),
        ],
        out_shape=[
            jax.ShapeDtypeStruct((n_tiles, m, GROUP_WIDTH), BF16),
            jax.ShapeDtypeStruct((ms, n), F32),
        ],
        scratch_shapes=[pltpu.VMEM((k, n), BF16), pltpu.VMEM((N_SWA_GROUPS, tm + ms, k), BF16)],
        compiler_params=_params("arbitrary"),
        name="in_proj_b",
    )(x, xs, g.reshape(1, k), w)


MEM_KV_SCRATCH = [pltpu.VMEM((N_MEM_HEADS, N_MEM, HEAD_DIM), BF16),
                  pltpu.VMEM((N_MEM_HEADS, N_MEM, 2 * HEAD_DIM), BF16)]


def _prepare_mem_kv(kv_ref, kb_ref, vb_ref):
    for h in range(N_MEM_HEADS):
        kb_ref[h] = kv_ref[:, 0, h, :].astype(BF16)
        vb_ref[h, :, :HEAD_DIM] = kv_ref[:, 1, h, :].astype(BF16)
        vb_ref[h, :, HEAD_DIM:] = jnp.ones((N_MEM, HEAD_DIM), BF16)


def _mem_attention_parts(q_ref, kb_ref, vb_ref, o_ref, col0):
    def head(h):
        lo, hi = h * HEAD_DIM, (h + 1) * HEAD_DIM
        s = lax.dot_general(q_ref[:, lo:hi], kb_ref[h], (((1,), (1,)), ((), ())), preferred_element_type=F32)
        m = jnp.max(s, axis=1, keepdims=True)
        p = jnp.exp(s - m).astype(BF16)
        ov = jnp.dot(p, vb_ref[h], preferred_element_type=F32)
        o_ref[:, col0 + lo:col0 + hi] = (ov[:, :HEAD_DIM] / ov[:, HEAD_DIM:]).astype(o_ref.dtype)

    return [functools.partial(head, h) for h in range(N_MEM_HEADS)]


def _gmlp_stage(u_ref, v_ref, q_ref, gv_ref, ws_ref, bs_ref, kb_ref, vb_ref, vn_ref, o_ref):
    tm = u_ref.shape[0]

    def norm_v():
        vn_ref[...] = _rms(v_ref[...].astype(F32), gv_ref[...]).astype(BF16)

    def group(g):
        row = lax.broadcasted_iota(jnp.int32, (CHUNK, CHUNK), 0)
        col = lax.broadcasted_iota(jnp.int32, (CHUNK, CHUNK), 1)
        w = jnp.where(row >= col, ws_ref[g], 0.0).astype(BF16)
        b = bs_ref[:, g:g + 1]
        c0, c1 = g * GROUP_DIM_A, (g + 1) * GROUP_DIM_A
        for c in range(tm // CHUNK):
            r0, r1 = c * CHUNK, (c + 1) * CHUNK
            s = jnp.dot(w, vn_ref[r0:r1, c0:c1], preferred_element_type=F32) + b
            o_ref[r0:r1, c0:c1] = (u_ref[r0:r1, c0:c1].astype(F32) * s).astype(o_ref.dtype)

    return ([norm_v] + [functools.partial(group, g) for g in range(N_GROUPS_A)]
            + _mem_attention_parts(q_ref, kb_ref, vb_ref, o_ref, MIXER_WIDTH))


def _mem_kv_spec(layer, batch_of):
    return pl.BlockSpec((None, None, N_MEM, 2, N_MEM_HEADS, HEAD_DIM),
                        lambda i: (layer, batch_of(i), 0, 0, 0, 0))


def _mix_out_kernel(*refs, stage, n_in, n_tiles, tiles_per_batch):
    mix_in = refs[:n_in]
    kv_ref, w_ref, x_ref, g_ref, o_ref, buf0_ref, buf1_ref, acc_ref, kb_ref, vb_ref = refs[n_in:n_in + 10]
    extra = refs[n_in + 10:]
    s = pl.program_id(0)

    @pl.when(s == 0)
    def _():
        buf1_ref[...] = jnp.zeros_like(buf1_ref)

    @pl.when(jnp.minimum(s, n_tiles - 1) % tiles_per_batch == 0)
    def _():
        _prepare_mem_kv(kv_ref, kb_ref, vb_ref)

    def step(dst_ref, src_ref):
        parts = stage(*mix_in, kb_ref, vb_ref, *extra, dst_ref)
        n_chunks = D_MODEL // OUT_CHUNK
        for c in range(n_chunks):
            cols = slice(c * OUT_CHUNK, (c + 1) * OUT_CHUNK)
            acc_ref[:, cols] = jnp.dot(src_ref[...], w_ref[:, cols], preferred_element_type=F32)
            for part in parts[c * len(parts) // n_chunks:(c + 1) * len(parts) // n_chunks]:
                part()
        o_ref[...] = x_ref[...] + _rms(acc_ref[...], g_ref[...])

    @pl.when(s % 2 == 0)
    def _():
        step(buf0_ref, buf1_ref)

    @pl.when(s % 2 == 1)
    def _():
        step(buf1_ref, buf0_ref)


def _mix_out(stage, mix_inputs, mix_specs, extra_scratch, mem_kv, layer, w, x, g, *, tm, rows_per_batch, name):
    m = x.shape[0]
    n_tiles = m // tm
    tiles_per_batch = rows_per_batch // tm
    mix_tile = lambda s: jnp.minimum(s, n_tiles - 1)
    out_tile = lambda s: jnp.maximum(s - 1, 0)
    row_spec = pl.BlockSpec((tm, D_MODEL), lambda s: (out_tile(s), 0))
    return pl.pallas_call(
        functools.partial(_mix_out_kernel, stage=stage, n_in=len(mix_inputs), n_tiles=n_tiles,
                          tiles_per_batch=tiles_per_batch),
        grid=(n_tiles + 1,),
        in_specs=[spec(mix_tile) for spec in mix_specs] + [
            _mem_kv_spec(layer, lambda s: mix_tile(s) // tiles_per_batch),
            pl.BlockSpec((D_MODEL, D_MODEL), lambda s: (0, 0), pipeline_mode=pl.Buffered(1)),
            row_spec,
            pl.BlockSpec((1, D_MODEL), lambda s: (0, 0)),
        ],
        out_specs=row_spec,
        out_shape=jax.ShapeDtypeStruct((m, D_MODEL), F32),
        scratch_shapes=[pltpu.VMEM((tm, D_MODEL), BF16), pltpu.VMEM((tm, D_MODEL), BF16),
                        pltpu.VMEM((tm, D_MODEL), F32)] + MEM_KV_SCRATCH + list(extra_scratch),
        compiler_params=_params("arbitrary"),
        name=name,
    )(*mix_inputs, mem_kv, w, x, g.reshape(1, D_MODEL))


def _gmlp_mix_out(zact, g_v, w_s, b_s, mem_kv, layer, w, x, g, *, tm, rows_per_batch):
    const = lambda shape: (lambda tile: pl.BlockSpec(shape, lambda s: (0,) * len(shape)))
    specs = [
        lambda tile: pl.BlockSpec((tm, MIXER_WIDTH), lambda s: (tile(s), 0)),
        lambda tile: pl.BlockSpec((tm, MIXER_WIDTH), lambda s: (tile(s), 1)),
        lambda tile: pl.BlockSpec((tm, MEM_WIDTH), lambda s: (tile(s), 2 * MIXER_WIDTH // MEM_WIDTH)),
        const((1, MIXER_WIDTH)),
        const((N_GROUPS_A, CHUNK, CHUNK)),
        const((CHUNK, N_GROUPS_A)),
    ]
    return _mix_out(_gmlp_stage, (zact, zact, zact, g_v.reshape(1, MIXER_WIDTH), w_s, b_s.T), specs,
                    [pltpu.VMEM((tm, MIXER_WIDTH), BF16)], mem_kv, layer, w, x, g,
                    tm=tm, rows_per_batch=rows_per_batch, name="gmlp_mix_out")


def _out_proj_cast_kernel(mix_ref, w_ref, x_ref, g_ref, o_ref, wb_ref, acc_ref):
    j = pl.program_id(0)
    w = w_ref[...].astype(BF16)
    wb_ref[...] = w
    acc_ref[j] = jnp.dot(mix_ref[...], w, preferred_element_type=F32)

    @pl.when(j == pl.num_programs(0) - 1)
    def _():
        o = jnp.concatenate([acc_ref[t] for t in range(acc_ref.shape[0])], axis=1)
        o_ref[...] = x_ref[...] + _rms(o, g_ref[...])


def _sample_out_proj(mix, w, layer, x, g):
    bd = mix.shape[0]
    mix = jnp.pad(mix.reshape(bd, D_MODEL), ((0, x.shape[0] - bd), (0, 0))).astype(BF16)
    return _out_proj_cast(mix, w, layer, x, g, tn=CAST_TILE)


def _out_proj_cast(mix, w, layer, x, g, *, tn):
    m = x.shape[0]
    n_tiles = D_MODEL // tn
    return pl.pallas_call(
        _out_proj_cast_kernel,
        grid=(n_tiles,),
        in_specs=[
            pl.BlockSpec((m, D_MODEL), lambda j: (0, 0)),
            pl.BlockSpec((None, D_MODEL, tn), lambda j: (layer, 0, j)),
            pl.BlockSpec((m, D_MODEL), lambda j: (0, 0)),
            pl.BlockSpec((1, D_MODEL), lambda j: (0, 0)),
        ],
        out_specs=[
            pl.BlockSpec((m, D_MODEL), lambda j: (0, 0)),
            pl.BlockSpec((D_MODEL, tn), lambda j: (0, j)),
        ],
        out_shape=[
            jax.ShapeDtypeStruct((m, D_MODEL), F32),
            jax.ShapeDtypeStruct((D_MODEL, D_MODEL), BF16),
        ],
        scratch_shapes=[pltpu.VMEM((n_tiles, m, tn), F32)],
        compiler_params=_params("arbitrary"),
        name="out_proj_cast",
    )(mix, w, x, g.reshape(1, D_MODEL))


def _accumulate(o_ref, ssq_ref, part, first):
    new = part if first else o_ref[...] + part
    o_ref[...] = new
    sq = new * new
    ssq_ref[...] = functools.reduce(
        jnp.add, [sq[:, c:c + HEAD_DIM] for c in range(0, sq.shape[1], HEAD_DIM)])


def _residual_norm(x_ref, o_ref, ssq_ref, g_ref):
    ms = jnp.sum(ssq_ref[...], axis=-1, keepdims=True) * (1.0 / o_ref.shape[1])
    o_ref[...] = x_ref[...] + o_ref[...] * lax.rsqrt(ms + EPS) * g_ref[...]


def _ffn_kernel(x_ref, gpre_ref, gpost_ref, wg_ref, wl_ref, wd_ref, o_ref, xn_ref, ssq_ref):
    f = pl.program_id(1)

    def step(first):
        xn = xn_ref[...]
        hg = jnp.dot(xn, wg_ref[...], preferred_element_type=F32)
        hl = jnp.dot(xn, wl_ref[...], preferred_element_type=F32)
        a = (hg * jax.nn.sigmoid(hg) * hl).astype(BF16)
        _accumulate(o_ref, ssq_ref, jnp.dot(a, wd_ref[...], preferred_element_type=F32), first)

    @pl.when(f == 0)
    def _():
        xn_ref[...] = _rms(x_ref[...], gpre_ref[...]).astype(BF16)
        step(True)

    @pl.when(f > 0)
    def _():
        step(False)

    @pl.when(f == pl.num_programs(1) - 1)
    def _():
        _residual_norm(x_ref, o_ref, ssq_ref, gpost_ref)


def _ffn(x, g_pre, g_post, weights, *, tm, tf, first_tile):
    m = x.shape[0]
    nf = D_FF // tf
    rows = pl.BlockSpec((tm, D_MODEL), lambda i, f: (i + first_tile, 0))
    return pl.pallas_call(
        _ffn_kernel,
        grid=(m // tm - first_tile, nf),
        in_specs=[
            rows,
            pl.BlockSpec((1, D_MODEL), lambda i, f: (0, 0)),
            pl.BlockSpec((1, D_MODEL), lambda i, f: (0, 0)),
            pl.BlockSpec((D_MODEL, tf), lambda i, f: (0, f)),
            pl.BlockSpec((D_MODEL, tf), lambda i, f: (0, f)),
            pl.BlockSpec((tf, D_MODEL), lambda i, f: (f, 0)),
        ],
        out_specs=rows,
        out_shape=jax.ShapeDtypeStruct((m, D_MODEL), F32),
        input_output_aliases={0: 0},
        scratch_shapes=[pltpu.VMEM((tm, D_MODEL), BF16), pltpu.VMEM((tm, HEAD_DIM), F32)],
        compiler_params=_params("parallel", "arbitrary", vmem_limit=FFN_VMEM_LIMIT),
        name="ffn",
    )(x, g_pre.reshape(1, D_MODEL), g_post.reshape(1, D_MODEL), *weights)


def _ffn_head_kernel(x_ref, xs_ref, gpre_ref, gpost_ref, wg_ref, wl_ref, wd_ref,
                     o_ref, os_ref, wgb_ref, wlb_ref, wdb_ref, xn_ref, ssq_ref, ssqs_ref):
    tm = x_ref.shape[0]
    f = pl.program_id(0)

    def step(first):
        wg, wl, wd = (r[...].astype(BF16) for r in (wg_ref, wl_ref, wd_ref))
        wgb_ref[...] = wg
        wlb_ref[...] = wl
        wdb_ref[...] = wd
        xn = xn_ref[...]
        hg = jnp.dot(xn, wg, preferred_element_type=F32)
        hl = jnp.dot(xn, wl, preferred_element_type=F32)
        a = (hg * jax.nn.sigmoid(hg) * hl).astype(BF16)
        part = jnp.dot(a, wd, preferred_element_type=F32)
        _accumulate(o_ref, ssq_ref, part[:tm, :], first)
        _accumulate(os_ref, ssqs_ref, part[tm:, :], first)

    @pl.when(f == 0)
    def _():
        xn_ref[:tm, :] = _rms(x_ref[...], gpre_ref[...]).astype(BF16)
        xn_ref[tm:, :] = _rms(xs_ref[...], gpre_ref[...]).astype(BF16)
        step(True)

    @pl.when(f > 0)
    def _():
        step(False)

    @pl.when(f == pl.num_programs(0) - 1)
    def _():
        _residual_norm(x_ref, o_ref, ssq_ref, gpost_ref)
        _residual_norm(xs_ref, os_ref, ssqs_ref, gpost_ref)


def _ffn_head(x, xs, g_pre, g_post, w_up, w_down, layer, *, tm, tf):
    m = x.shape[0]
    ms = xs.shape[0]
    nf = D_FF // tf
    once = dict(pipeline_mode=pl.Buffered(1))
    head = pl.BlockSpec((tm, D_MODEL), lambda f: (0, 0), **once)
    sample = pl.BlockSpec((ms, D_MODEL), lambda f: (0, 0))
    vec = pl.BlockSpec((1, D_MODEL), lambda f: (0, 0))
    outs = pl.pallas_call(
        _ffn_head_kernel,
        grid=(nf,),
        in_specs=[
            head, sample, vec, vec,
            pl.BlockSpec((None, D_MODEL, tf), lambda f: (layer, 0, f)),
            pl.BlockSpec((None, D_MODEL, tf), lambda f: (layer, 0, nf + f)),
            pl.BlockSpec((None, tf, D_MODEL), lambda f: (layer, f, 0)),
        ],
        out_specs=[
            head, sample,
            pl.BlockSpec((D_MODEL, tf), lambda f: (0, f)),
            pl.BlockSpec((D_MODEL, tf), lambda f: (0, f)),
            pl.BlockSpec((tf, D_MODEL), lambda f: (f, 0)),
        ],
        out_shape=[
            jax.ShapeDtypeStruct((m, D_MODEL), F32),
            jax.ShapeDtypeStruct((ms, D_MODEL), F32),
            jax.ShapeDtypeStruct((D_MODEL, D_FF), BF16),
            jax.ShapeDtypeStruct((D_MODEL, D_FF), BF16),
            jax.ShapeDtypeStruct((D_FF, D_MODEL), BF16),
        ],
        input_output_aliases={0: 0},
        scratch_shapes=[pltpu.VMEM((tm + ms, D_MODEL), BF16), pltpu.VMEM((tm, HEAD_DIM), F32),
                        pltpu.VMEM((ms, HEAD_DIM), F32)],
        compiler_params=_params("arbitrary"),
        name="ffn_head",
    )(x, xs, g_pre.reshape(1, D_MODEL), g_post.reshape(1, D_MODEL), w_up, w_up, w_down)
    return outs[0], outs[1], tuple(outs[2:])


def _swa_kernel(q_ref, k_ref, v_ref, tb_ref, o_ref, lse_ref):
    n_units, n_res, u, _ = q_ref.shape
    per_blk = N_BACK // u
    n_blk = n_units // per_blk
    lane = lax.broadcasted_iota(jnp.int32, (N_BACK, HEAD_DIM), 1)
    ones = jnp.ones((2 * N_BACK, HEAD_DIM), BF16)

    def rows(ref, res, unit0, n_rows, lo, hi):
        return ref[pl.ds(unit0, n_rows // u), res, :, lo:hi].reshape(n_rows, hi - lo)

    def block(res, qu, ku, table):
        n_keys = N_BACK if table == 0 else 2 * N_BACK
        lse_tile = jnp.zeros((N_BACK, HEAD_DIM), F32)
        for h in range(HEADS_PER_GROUP):
            lo, hi = h * HEAD_DIM, (h + 1) * HEAD_DIM
            q = rows(q_ref, res, qu, N_BACK, lo, hi)
            kw = rows(k_ref, res, ku, n_keys, lo, hi)
            vw = rows(v_ref, res, ku, n_keys, lo, hi)
            s = lax.dot_general(q, kw, (((1,), (1,)), ((), ())), preferred_element_type=F32)
            s = s + tb_ref[table, h][:, :n_keys]
            m = jnp.max(s, axis=1, keepdims=True)
            p = jnp.exp(s - m).astype(BF16)
            ov = jnp.dot(p, jnp.concatenate([vw, ones[:n_keys]], axis=1), preferred_element_type=F32)
            den = ov[:, HEAD_DIM:]
            o = ov[:, :HEAD_DIM] / den
            o_ref[pl.ds(qu, per_blk), res, :, lo:hi] = o.reshape(per_blk, u, HEAD_DIM).astype(o_ref.dtype)
            lse_tile = jnp.where(lane == h, m, jnp.where(lane == HEADS_PER_GROUP + h, den, lse_tile))
        lse_ref[pl.ds(qu, per_blk), res, :, :] = lse_tile.reshape(per_blk, u, HEAD_DIM)

    def block_at(res, n):
        block(res, n * per_blk, (n - 1) * per_blk, 1)

    n_trips = (n_blk - 1) // SWA_BLOCKS_PER_TRIP
    for res in range(n_res):
        block(res, 0, 0, 0)

        def body(i, carry, res=res):
            for b in range(SWA_BLOCKS_PER_TRIP):
                block_at(res, 1 + SWA_BLOCKS_PER_TRIP * i + b)
            return carry

        if n_trips > 0:
            lax.fori_loop(0, n_trips, body, 0)
        for n in range(1 + n_trips * SWA_BLOCKS_PER_TRIP, n_blk):
            block_at(res, n)


def _sub_block(dil):
    return N_BACK if dil == 1 else PERM_BLOCK


def _swa_group(zb, tables, g, *, batch, seq):
    dil = SWA_PATTERN[g][1]
    sub = _sub_block(dil)
    n_units, u = seq // sub, sub // dil
    n_blk = n_units * u // N_BACK
    n_res = min(dil, max(1, SWA_BLOCKS_PER_TRIP // n_blk))
    view = zb.reshape(zb.shape[0], batch, n_units, dil, u, GROUP_WIDTH)

    def rows_in(tile):
        return pl.BlockSpec((None, None, n_units, n_res, u, GROUP_WIDTH), lambda i, r: (tile, i, 0, r, 0, 0))

    def rows_out(width):
        return pl.BlockSpec((None, n_units, n_res, u, width), lambda i, r: (i, 0, r, 0, 0))

    return pl.pallas_call(
        _swa_kernel,
        grid=(batch, dil // n_res),
        in_specs=[rows_in(g), rows_in(N_SWA_GROUPS + g), rows_in(2 * N_SWA_GROUPS + g),
                  pl.BlockSpec((None, 2, HEADS_PER_GROUP, N_BACK, 2 * N_BACK), lambda i, r: (g, 0, 0, 0, 0))],
        out_specs=[rows_out(GROUP_WIDTH), rows_out(HEAD_DIM)],
        out_shape=[
            jax.ShapeDtypeStruct((batch, n_units, dil, u, GROUP_WIDTH), BF16),
            jax.ShapeDtypeStruct((batch, n_units, dil, u, HEAD_DIM), F32),
        ],
        compiler_params=_params("parallel", "parallel"),
        name=f"swa_group{g}",
    )(view, view, view, tables)


def _split3(x):
    hi = x.astype(BF16)
    rest = x - hi.astype(F32)
    mid = rest.astype(BF16)
    lo = (rest - mid.astype(F32)).astype(BF16)
    return hi, mid, lo


def _merge_stage(o0_ref, o1_ref, o2_ref, l0_ref, l1_ref, l2_ref, q_ref, kb_ref, vb_ref, o_ref):
    tm = o_ref.shape[0]
    group_refs = ((o0_ref, l0_ref), (o1_ref, l1_ref), (o2_ref, l2_ref))
    outs, lses = [None] * N_SWA_GROUPS, [None] * N_SWA_GROUPS

    def token_order(g):
        o_g, l_g = group_refs[g]
        dil = SWA_PATTERN[g][1]
        o = o_g[...].reshape(tm, GROUP_WIDTH)
        l = l_g[...].reshape(tm, HEAD_DIM)
        if dil > 1:
            inv = _residue_major_perm(PERM_BLOCK, dil, transpose=True)
            l3 = _split3(l)
            o_nat, l_nat = [], []
            for s in range(0, tm, PERM_BLOCK):
                o_nat.append(jnp.dot(inv, o[s:s + PERM_BLOCK, :], preferred_element_type=F32))
                l_nat.append(sum(jnp.dot(inv, t[s:s + PERM_BLOCK, :], preferred_element_type=F32) for t in l3))
            o = jnp.concatenate(o_nat, axis=0)
            l = jnp.concatenate(l_nat, axis=0)
        outs[g] = o.astype(F32)
        lses[g] = l

    def merge_head(h):
        lo, hi = h * HEAD_DIM, (h + 1) * HEAD_DIM
        ls = [l[:, h:h + 1] for l in lses]
        ds = [l[:, HEADS_PER_GROUP + h:HEADS_PER_GROUP + h + 1] for l in lses]
        mx = jnp.maximum(jnp.maximum(ls[0], ls[1]), ls[2])
        es = [d * jnp.exp(l - mx) for l, d in zip(ls, ds)]
        tot = es[0] + es[1] + es[2]
        for g in range(N_SWA_GROUPS):
            alpha = es[g] / tot
            o_ref[:, g * GROUP_WIDTH + lo:g * GROUP_WIDTH + hi] = (outs[g][:, lo:hi] * alpha).astype(o_ref.dtype)

    return ([functools.partial(token_order, g) for g in range(N_SWA_GROUPS)]
            + [functools.partial(merge_head, h) for h in range(HEADS_PER_GROUP)]
            + _mem_attention_parts(q_ref, kb_ref, vb_ref, o_ref, MIXER_WIDTH))


def _swa_merge_out(outs, lses, zb, mem_kv, layer, w, x, g, *, rows_per_batch):
    tm = ROW_TILE
    tiles_per_batch = rows_per_batch // tm

    def group_tile(width, dil):
        sub = _sub_block(dil)
        return lambda tile: pl.BlockSpec(
            (None, tm // sub, dil, sub // dil, width),
            lambda s: (tile(s) // tiles_per_batch, tile(s) % tiles_per_batch, 0, 0, 0))

    specs = ([group_tile(GROUP_WIDTH, dil) for _, dil in SWA_PATTERN]
             + [group_tile(HEAD_DIM, dil) for _, dil in SWA_PATTERN]
             + [lambda tile: pl.BlockSpec((None, tm, MEM_WIDTH), lambda s: (3 * N_SWA_GROUPS, tile(s), 0))])
    return _mix_out(_merge_stage, (*outs, *lses, zb), specs, [], mem_kv, layer, w, x, g,
                    tm=tm, rows_per_batch=rows_per_batch, name="swa_merge_out")


def _head_rows(row, col0):
    return jnp.concatenate([row[:, col0 + h * HEAD_DIM:col0 + (h + 1) * HEAD_DIM]
                            for h in range(HEADS_PER_GROUP)], axis=0)


def _sample_attention(q4, kv_ref, bias=None, new=None):
    k3, v3 = kv_ref[:, 0], kv_ref[:, 1]
    s = jnp.sum(k3 * q4[None], axis=-1, keepdims=True) * ATTN_SCALE
    if bias is not None:
        s = s + bias
    m = jnp.max(s, axis=0)
    if new is not None:
        k_new, v_new, b_new = new
        s_new = jnp.sum(k_new * q4, axis=-1, keepdims=True) * ATTN_SCALE + b_new
        m = jnp.maximum(m, s_new)
    p = jnp.exp(s - m[None])
    den = jnp.sum(p, axis=0)
    o = jnp.sum(p * v3, axis=0)
    if new is not None:
        p_new = jnp.exp(s_new - m)
        den = den + p_new
        o = o + p_new * v_new
    return o / den, m + jnp.log(den)


def _store_head_rows(o_ref, col0, x4):
    for h in range(HEADS_PER_GROUP):
        o_ref[:, col0 + h * HEAD_DIM:col0 + (h + 1) * HEAD_DIM] = x4[h:h + 1, :]


def _sample_mem_attention(q_row, kv_ref, o_ref, col0):
    o, _ = _sample_attention(_head_rows(q_row, 0), kv_ref)
    _store_head_rows(o_ref, col0, o)


def _sample_mix_a_kernel(z_ref, gv_ref, w0_ref, b0_ref, kv_ref, o_ref, vrow_ref):
    u = z_ref[:, 0:MIXER_WIDTH]
    v = _rms(z_ref[:, MIXER_WIDTH:2 * MIXER_WIDTH], gv_ref[...])
    vrow_ref[...] = v
    o_ref[:, 0:MIXER_WIDTH] = u * (w0_ref[...] * v + b0_ref[...])
    _sample_mem_attention(z_ref[:, 2 * MIXER_WIDTH:2 * MIXER_WIDTH + MEM_WIDTH], kv_ref, o_ref, MIXER_WIDTH)


def _sample_mix_a(z, g_v, w_s, b_s, mem_kv, layer):
    bd = mem_kv.shape[1]
    w0 = jnp.repeat(w_s[:, 0, 0], GROUP_DIM_A).reshape(1, MIXER_WIDTH)
    b0 = jnp.repeat(b_s[:, 0], GROUP_DIM_A).reshape(1, MIXER_WIDTH)
    width = z.shape[1]
    vec = lambda i: (0, 0)
    return pl.pallas_call(
        _sample_mix_a_kernel,
        grid=(bd,),
        in_specs=[
            pl.BlockSpec((None, 1, width), lambda i: (i, 0, 0)),
            pl.BlockSpec((1, MIXER_WIDTH), vec),
            pl.BlockSpec((1, MIXER_WIDTH), vec),
            pl.BlockSpec((1, MIXER_WIDTH), vec),
            _mem_kv_spec(layer, lambda i: i),
        ],
        out_specs=[
            pl.BlockSpec((None, 1, D_MODEL), lambda i: (i, 0, 0)),
            pl.BlockSpec((None, 1, MIXER_WIDTH), lambda i: (i, 0, 0)),
        ],
        out_shape=[
            jax.ShapeDtypeStruct((bd, 1, D_MODEL), F32),
            jax.ShapeDtypeStruct((bd, 1, MIXER_WIDTH), F32),
        ],
        compiler_params=_params("parallel"),
        name="sample_mix_a",
    )(z[:bd].reshape(bd, 1, width), g_v.reshape(1, MIXER_WIDTH), w0, b0, mem_kv)


def _sample_mix_b_kernel(z_ref, c0_ref, c1_ref, c2_ref, bcol_ref, bnew_ref, kv_ref, o_ref):
    caches = (c0_ref, c1_ref, c2_ref)
    z = z_ref[...]
    outs, lses = [], []
    for g in range(N_SWA_GROUPS):
        c0 = g * GROUP_WIDTH
        new = (_head_rows(z, MIXER_WIDTH + c0), _head_rows(z, 2 * MIXER_WIDTH + c0), bnew_ref[g])
        o, lse = _sample_attention(_head_rows(z, c0), caches[g], bias=bcol_ref[g], new=new)
        outs.append(o)
        lses.append(lse)
    mx = jnp.maximum(jnp.maximum(lses[0], lses[1]), lses[2])
    es = [jnp.exp(l - mx) for l in lses]
    tot = es[0] + es[1] + es[2]
    for g in range(N_SWA_GROUPS):
        _store_head_rows(o_ref, g * GROUP_WIDTH, outs[g] * (es[g] / tot))
    _sample_mem_attention(z[:, 3 * MIXER_WIDTH:3 * MIXER_WIDTH + MEM_WIDTH], kv_ref, o_ref, MIXER_WIDTH)


def _sample_mix_b(z, win_caches, swa_layer, bias_groups, mem_kv, layer):
    bd = mem_kv.shape[1]
    width = z.shape[1]
    cache_views, cache_specs = [], []
    for g, (win, dil) in enumerate(SWA_PATTERN):
        c = win_caches[g]
        cache_views.append(c.reshape(c.shape[0], bd, win // dil, dil, 2, HEADS_PER_GROUP, HEAD_DIM))
        cache_specs.append(pl.BlockSpec((None, None, N_BACK, None, 2, HEADS_PER_GROUP, HEAD_DIM),
                                        lambda i: (swa_layer, i, 0, 0, 0, 0, 0)))
    bcol = jnp.stack([bg[:, N_BACK:0:-1].T for bg in bias_groups], axis=0)
    bcol = jnp.broadcast_to(bcol[..., None], bcol.shape + (HEAD_DIM,))
    bnew = jnp.stack([bg[:, 0] for bg in bias_groups], axis=0)
    bnew = jnp.broadcast_to(bnew[..., None], bnew.shape + (HEAD_DIM,))
    return pl.pallas_call(
        _sample_mix_b_kernel,
        grid=(bd,),
        in_specs=[pl.BlockSpec((None, 1, width), lambda i: (i, 0, 0))] + cache_specs + [
            pl.BlockSpec((N_SWA_GROUPS, N_BACK, HEADS_PER_GROUP, HEAD_DIM), lambda i: (0, 0, 0, 0)),
            pl.BlockSpec((N_SWA_GROUPS, HEADS_PER_GROUP, HEAD_DIM), lambda i: (0, 0, 0)),
            _mem_kv_spec(layer, lambda i: i),
        ],
        out_specs=pl.BlockSpec((None, 1, D_MODEL), lambda i: (i, 0, 0)),
        out_shape=jax.ShapeDtypeStruct((bd, 1, D_MODEL), F32),
        compiler_params=_params("parallel"),
        name="sample_mix_b",
    )(z[:bd].reshape(bd, 1, width), *cache_views, bcol, bnew, mem_kv)


def _t5_bucket(dist):
    nf = jnp.maximum(dist, MAX_EXACT).astype(F32)
    large = MAX_EXACT + (jnp.log(nf / MAX_EXACT) / math.log(MAX_DISTANCE / MAX_EXACT)
                         * (N_BUCKETS - MAX_EXACT)).astype(jnp.int32)
    large = jnp.minimum(large, N_BUCKETS - 1)
    return jnp.where(dist < MAX_EXACT, dist, large)


def _group_bias(rel_bias, g, dil):
    dist = jnp.arange(N_BACK + 1, dtype=jnp.int32) * dil
    b = rel_bias[_t5_bucket(dist)][:, g * HEADS_PER_GROUP:(g + 1) * HEADS_PER_GROUP]
    return b.T.astype(F32)


def _band_tables_kernel(b_ref, o_ref):
    n, rows, width = o_ref.shape
    for x in range(n):
        row = jnp.broadcast_to(b_ref[x:x + 1, :], (rows, width))
        o_ref[x] = pltpu.roll(row, 0, 1, stride=1, stride_axis=0)


def _band_tables(bias_groups):
    width = 2 * N_BACK
    rows = []
    for bias_j in bias_groups:
        masked = jnp.full((HEADS_PER_GROUP, N_BACK - 1), NEG_INF, F32)
        rows.append(jnp.concatenate([bias_j[:, :1], masked, bias_j[:, N_BACK:0:-1]], axis=1))
        rows.append(jnp.concatenate([bias_j[:, ::-1], masked], axis=1))
    base = jnp.stack(rows, axis=0).reshape(-1, width)
    tabs = pl.pallas_call(
        _band_tables_kernel,
        out_shape=jax.ShapeDtypeStruct((base.shape[0], N_BACK, width), F32),
        name="band_tables",
    )(base)
    return tabs.reshape(N_SWA_GROUPS, 2, HEADS_PER_GROUP, N_BACK, width)


def _kv_tail_kernel(k_ref, v_ref, o_ref, *, dil):
    rows = k_ref.shape[0]
    for kv, ref in enumerate((k_ref, v_ref)):
        x = ref[...]
        if dil > 1:
            inv = _residue_major_perm(PERM_BLOCK, dil, transpose=True)
            x = jnp.concatenate([jnp.dot(inv, x[s:s + PERM_BLOCK, :], preferred_element_type=F32)
                                 for s in range(0, rows, PERM_BLOCK)], axis=0)
        x = x.astype(F32)
        for h in range(HEADS_PER_GROUP):
            o_ref[:, kv, h, :] = x[:, h * HEAD_DIM:(h + 1) * HEAD_DIM]


def _kv_tail(zb, g, *, batch, seq):
    win, dil = SWA_PATTERN[g]
    rows = min(win, FFN_ROW_TILE)
    first = (seq - win) // rows
    per_batch = seq // rows

    def tile(t):
        return pl.BlockSpec((None, rows, GROUP_WIDTH), lambda b, s: (t, b * per_batch + first + s, 0))

    return pl.pallas_call(
        functools.partial(_kv_tail_kernel, dil=dil),
        grid=(batch, win // rows),
        in_specs=[tile(N_SWA_GROUPS + g), tile(2 * N_SWA_GROUPS + g)],
        out_specs=pl.BlockSpec((None, rows, 2, HEADS_PER_GROUP, HEAD_DIM), lambda b, s: (b, s, 0, 0, 0)),
        out_shape=jax.ShapeDtypeStruct((batch, win, 2, HEADS_PER_GROUP, HEAD_DIM), F32),
        compiler_params=_params("parallel", "parallel"),
        name=f"kv_tail{g}",
    )(zb, zb)


def kernel(x_prompt, x_sample, mem_prompt, cache_mem_kv, cache_win128_kv, cache_win512_kv, cache_win2048_kv, rel_bias, norm_mix_pre, norm_mix_post, norm_ffn_pre, norm_ffn_post, norm_mem, w_mem_kv, w_in_a, norm_v_a, w_spatial_a, b_spatial_a, w_in_b, w_out, w_ffn_up, w_ffn_down):
    batch, seq, _ = x_prompt.shape
    bd = x_sample.shape[0]
    depth = w_out.shape[0]
    m_p = batch * seq
    win_caches = (cache_win128_kv, cache_win512_kv, cache_win2048_kv)

    bias_groups = [_group_bias(rel_bias, g, dil) for g, (_, dil) in enumerate(SWA_PATTERN)]
    band_tables = _band_tables(bias_groups)

    yp = x_prompt.reshape(m_p, D_MODEL)
    ys = jnp.pad(x_sample.reshape(bd, D_MODEL), ((0, SAMPLE_PAD - bd), (0, 0)))
    mem_rows = mem_prompt.reshape(batch * N_MEM, D_MODEL)

    mem_kv_p = _mem_kv(mem_rows, norm_mem, w_mem_kv, batch=batch)
    chunk_v_s = []
    win_p = [[] for _ in SWA_PATTERN]
    win_s = [[] for _ in SWA_PATTERN]
    for i in range(depth):
        li = i // 2
        if i % 2 == 0:
            zp, zs = _in_proj_a(yp, ys, norm_mix_pre[i], w_in_a, li, tm=ROW_TILE, tn=CAST_TILE,
                                gelu_cols=2 * MIXER_WIDTH)
            mix_s, v_rows = _sample_mix_a(zs, norm_v_a[li], w_spatial_a[li], b_spatial_a[li], cache_mem_kv, i)
            chunk_v_s.append(v_rows)
            ys, w_o = _sample_out_proj(mix_s, w_out, i, ys, norm_mix_post[i])
            yp = _gmlp_mix_out(zp, norm_v_a[li], w_spatial_a[li], b_spatial_a[li], mem_kv_p, i,
                               w_o, yp, norm_mix_post[i], tm=ROW_TILE, rows_per_batch=seq)
        else:
            zb, zs = _in_proj_b(yp, ys, norm_mix_pre[i], w_in_b, li, tm=ROW_TILE)
            outs, lses = [], []
            for g, (win, dil) in enumerate(SWA_PATTERN):
                o, lse = _swa_group(zb, band_tables, g, batch=batch, seq=seq)
                outs.append(o)
                lses.append(lse)
                win_p[g].append(_kv_tail(zb, g, batch=batch, seq=seq))
                kv_new = zs[:bd, MIXER_WIDTH:3 * MIXER_WIDTH]
                kv_new = kv_new.reshape(bd, 1, 2, N_SWA_GROUPS, HEADS_PER_GROUP, HEAD_DIM)[:, :, :, g]
                win_s[g].append(kv_new)
            mix_s = _sample_mix_b(zs, win_caches, li, bias_groups, cache_mem_kv, i)
            ys, w_o = _sample_out_proj(mix_s, w_out, i, ys, norm_mix_post[i])
            yp = _swa_merge_out(outs, lses, zb, mem_kv_p, i, w_o, yp, norm_mix_post[i], rows_per_batch=seq)
        yp, ys, w_ffn = _ffn_head(yp, ys, norm_ffn_pre[i], norm_ffn_post[i], w_ffn_up, w_ffn_down, i,
                                  tm=FFN_ROW_TILE, tf=HEAD_FF_TILE)
        yp = _ffn(yp, norm_ffn_pre[i], norm_ffn_post[i], w_ffn, tm=FFN_ROW_TILE, tf=FF_TILE, first_tile=1)

    return (
        yp.reshape(batch, seq, D_MODEL),
        ys[:bd].reshape(bd, 1, D_MODEL),
        mem_kv_p,
        jnp.stack(chunk_v_s, axis=0),
        jnp.stack(win_p[0], axis=0),
        jnp.stack(win_p[1], axis=0),
        jnp.stack(win_p[2], axis=0),
        jnp.stack(win_s[0], axis=0),
        jnp.stack(win_s[1], axis=0),
        jnp.stack(win_s[2], axis=0),
    )
```

```python
import functools
import math

import jax
import jax.numpy as jnp
from jax import lax
from jax.experimental import pallas as pl
from jax.experimental.pallas import tpu as pltpu

F32 = jnp.float32
BF16 = jnp.bfloat16

D_MODEL = 2048
HEAD_DIM = 128
N_MEM = 256
N_MEM_HEADS = 4
MEM_WIDTH = N_MEM_HEADS * HEAD_DIM
MIXER_WIDTH = D_MODEL - MEM_WIDTH
CHUNK = 128
N_GROUPS_A = 4
GROUP_DIM_A = MIXER_WIDTH // N_GROUPS_A
SWA_PATTERN = ((128, 1), (512, 4), (2048, 16))
N_SWA_GROUPS = len(SWA_PATTERN)
HEADS_PER_GROUP = 4
GROUP_WIDTH = HEADS_PER_GROUP * HEAD_DIM
N_BACK = 128
N_BUCKETS = 32
MAX_EXACT = N_BUCKETS // 2
MAX_DISTANCE = 2048
D_FF = 5632
EPS = 1e-6
NEG_INF = -1e30
ATTN_SCALE = HEAD_DIM ** -0.5
SAMPLE_PAD = 16
PERM_BLOCK = 256
SWA_BLOCKS_PER_TRIP = 16
OUT_CHUNK = 256

ROW_TILE = 512
FFN_ROW_TILE = 1024
FF_TILE = 512
HEAD_FF_TILE = 256
CAST_TILE = 512

V7X_VMEM_BYTES = 64 * 1024 * 1024
VMEM_LIMIT = V7X_VMEM_BYTES - 8 * 1024 * 1024
FFN_VMEM_LIMIT = V7X_VMEM_BYTES - 2 * 1024 * 1024


def _params(*sem, vmem_limit=VMEM_LIMIT):
    return pltpu.CompilerParams(dimension_semantics=sem, vmem_limit_bytes=vmem_limit)


def _gelu(x):
    return 0.5 * x * (1.0 + jnp.tanh(0.7978845608028654 * (x + 0.044715 * (x * x * x))))


def _rms(x, g):
    return x * lax.rsqrt(jnp.mean(x * x, axis=-1, keepdims=True) + EPS) * g


def _log2(n):
    assert n & (n - 1) == 0
    return n.bit_length() - 1


def _residue_major_perm(tm, dil, transpose=False):
    n = tm // dil
    row = lax.broadcasted_iota(jnp.int32, (tm, tm), 0)
    col = lax.broadcasted_iota(jnp.int32, (tm, tm), 1)
    dst, src = (col, row) if transpose else (row, col)
    want = lax.shift_left(jnp.bitwise_and(dst, n - 1), _log2(dil)) + lax.shift_right_logical(dst, _log2(n))
    return (src == want).astype(BF16)


def _in_proj_a_kernel(x_ref, xs_ref, g_ref, w_ref, zp_ref, zs_ref, wb_ref, xn_ref, *, n_col, gelu_cols):
    tm = x_ref.shape[0]
    tn = w_ref.shape[1]
    s = pl.program_id(0)

    def prompt_cols(acc, c0):
        col = lax.broadcasted_iota(jnp.int32, acc.shape, 1) + c0
        return jnp.where(col < gelu_cols, _gelu(acc), acc * ATTN_SCALE).astype(zp_ref.dtype)

    @pl.when(s == 0)
    def _():
        xn_ref[:tm, :] = _rms(x_ref[...], g_ref[...]).astype(BF16)
        xn_ref[tm:, :] = _rms(xs_ref[...], g_ref[...]).astype(BF16)

    for j in range(n_col):
        @pl.when(s == j)
        def _(j=j):
            cols = slice(j * tn, (j + 1) * tn)
            w = w_ref[...].astype(BF16)
            wb_ref[:, cols] = w
            acc = jnp.dot(xn_ref[...], w, preferred_element_type=F32)
            zp_ref[:, cols] = prompt_cols(acc[:tm, :], j * tn)
            sample = acc[tm:, :]
            zs_ref[:, cols] = _gelu(sample) if (j + 1) * tn <= gelu_cols else sample

    @pl.when(s >= n_col)
    def _():
        xn = _rms(x_ref[...], g_ref[...]).astype(BF16)
        acc = jnp.dot(xn, wb_ref[...], preferred_element_type=F32)
        zp_ref[...] = prompt_cols(acc, 0)


def _in_proj_a(x, xs, g, w, layer, *, tm, tn, gelu_cols):
    m, k = x.shape
    ms = xs.shape[0]
    n = w.shape[2]
    n_col, n_row = n // tn, m // tm
    assert gelu_cols % tn == 0
    row = lambda s: jnp.maximum(s - (n_col - 1), 0)
    return pl.pallas_call(
        functools.partial(_in_proj_a_kernel, n_col=n_col, gelu_cols=gelu_cols),
        grid=(n_col + n_row - 1,),
        in_specs=[
            pl.BlockSpec((tm, k), lambda s: (row(s), 0)),
            pl.BlockSpec((ms, k), lambda s: (0, 0)),
            pl.BlockSpec((1, k), lambda s: (0, 0)),
            pl.BlockSpec((None, k, tn), lambda s: (layer, 0, jnp.minimum(s, n_col - 1))),
        ],
        out_specs=[
            pl.BlockSpec((tm, n), lambda s: (row(s), 0)),
            pl.BlockSpec((ms, n), lambda s: (0, 0)),
        ],
        out_shape=[
            jax.ShapeDtypeStruct((m, n), BF16),
            jax.ShapeDtypeStruct((ms, n), F32),
        ],
        scratch_shapes=[pltpu.VMEM((k, n), BF16), pltpu.VMEM((tm + ms, k), BF16)],
        compiler_params=_params("arbitrary"),
        name="in_proj_a",
    )(x, xs, g.reshape(1, k), w)


def _mem_kv_kernel(x_ref, g_ref, w_ref, o_ref):
    xn = _rms(x_ref[...], g_ref[...]).astype(BF16)
    acc = jnp.dot(xn, w_ref[...].astype(BF16), preferred_element_type=F32)
    for kv in range(2):
        for h in range(N_MEM_HEADS):
            c0 = (kv * N_MEM_HEADS + h) * HEAD_DIM
            o_ref[:, kv, h, :] = acc[:, c0:c0 + HEAD_DIM]


def _mem_kv(mem_rows, g, w, *, batch):
    m, k = mem_rows.shape
    layers = w.shape[0]
    out = pl.pallas_call(
        _mem_kv_kernel,
        grid=(layers,),
        in_specs=[
            pl.BlockSpec((m, k), lambda l: (0, 0)),
            pl.BlockSpec((None, 1, k), lambda l: (l, 0, 0)),
            pl.BlockSpec((None, k, 2 * MEM_WIDTH), lambda l: (l, 0, 0)),
        ],
        out_specs=pl.BlockSpec((None, m, 2, N_MEM_HEADS, HEAD_DIM), lambda l: (l, 0, 0, 0, 0)),
        out_shape=jax.ShapeDtypeStruct((layers, m, 2, N_MEM_HEADS, HEAD_DIM), F32),
        compiler_params=_params("parallel"),
        name="mem_kv",
    )(mem_rows, g.reshape(layers, 1, k), w)
    return out.reshape(layers, batch, m // batch, 2, N_MEM_HEADS, HEAD_DIM)


def _in_proj_b_kernel(x_ref, xs_ref, g_ref, w_ref, o_ref, zs_ref, wb_ref, xn_ref):
    tm = x_ref.shape[0]
    n_tiles = o_ref.shape[0]
    n_qkv = 3 * N_SWA_GROUPS
    s = pl.program_id(0)

    def row_orders():
        xn = _rms(x_ref[...], g_ref[...]).astype(BF16)
        xn_ref[0, :tm, :] = xn
        for g in range(1, N_SWA_GROUPS):
            perm = _residue_major_perm(PERM_BLOCK, SWA_PATTERN[g][1])
            for r in range(0, tm, PERM_BLOCK):
                xn_ref[g, r:r + PERM_BLOCK, :] = jnp.dot(
                    perm, xn[r:r + PERM_BLOCK, :], preferred_element_type=F32).astype(BF16)

    def prompt_tile(t, acc):
        if t < N_SWA_GROUPS or t == n_qkv:
            acc = acc * ATTN_SCALE
        o_ref[t] = acc.astype(o_ref.dtype)

    src = lambda t: t % N_SWA_GROUPS if t < n_qkv else 0

    @pl.when(s == 0)
    def _():
        row_orders()
        xs = _rms(xs_ref[...], g_ref[...]).astype(BF16)
        for g in range(N_SWA_GROUPS):
            xn_ref[g, tm:, :] = xs

    for t in range(n_tiles):
        @pl.when(s == t)
        def _(t=t):
            cols = slice(t * GROUP_WIDTH, (t + 1) * GROUP_WIDTH)
            w = w_ref[...].astype(BF16)
            wb_ref[:, cols] = w
            acc = jnp.dot(xn_ref[src(t)], w, preferred_element_type=F32)
            prompt_tile(t, acc[:tm, :])
            zs_ref[:, cols] = acc[tm:, :]

    @pl.when(s >= n_tiles)
    def _():
        row_orders()
        for t in range(n_tiles):
            w = wb_ref[:, t * GROUP_WIDTH:(t + 1) * GROUP_WIDTH]
            prompt_tile(t, jnp.dot(xn_ref[src(t), :tm, :], w, preferred_element_type=F32))


def _in_proj_b(x, xs, g, w, layer, *, tm):
    m, k = x.shape
    ms = xs.shape[0]
    n = w.shape[2]
    n_tiles = n // GROUP_WIDTH
    row = lambda s: jnp.maximum(s - (n_tiles - 1), 0)
    return pl.pallas_call(
        _in_proj_b_kernel,
        grid=(n_tiles + m // tm - 1,),
        in_specs=[
            pl.BlockSpec((tm, k), lambda s: (row(s), 0)),
            pl.BlockSpec((ms, k), lambda s: (0, 0)),
            pl.BlockSpec((1, k), lambda s: (0, 0)),
            pl.BlockSpec((None, k, GROUP_WIDTH), lambda s: (layer, 0, jnp.minimum(s, n_tiles - 1))),
        ],
        out_specs=[
            pl.BlockSpec((n_tiles, tm, GROUP_WIDTH), lambda s: (0, row(s), 0)),
            pl.BlockSpec((ms, n), lambda s: (0, 0)),
        ],
        out_shape=[
            jax.ShapeDtypeStruct((n_tiles, m, GROUP_WIDTH), BF16),
            jax.ShapeDtypeStruct((ms, n), F32),
        ],
        scratch_shapes=[pltpu.VMEM((k, n), BF16), pltpu.VMEM((N_SWA_GROUPS, tm + ms, k), BF16)],
        compiler_params=_params("arbitrary"),
        name="in_proj_b",
    )(x, xs, g.reshape(1, k), w)


MEM_KV_SCRATCH = [pltpu.VMEM((N_MEM_HEADS, N_MEM, HEAD_DIM), BF16),
                  pltpu.VMEM((N_MEM_HEADS, N_MEM, 2 * HEAD_DIM), BF16)]


def _prepare_mem_kv(kv_ref, kb_ref, vb_ref):
    for h in range(N_MEM_HEADS):
        kb_ref[h] = kv_ref[:, 0, h, :].astype(BF16)
        vb_ref[h, :, :HEAD_DIM] = kv_ref[:, 1, h, :].astype(BF16)
        vb_ref[h, :, HEAD_DIM:] = jnp.ones((N_MEM, HEAD_DIM), BF16)


def _mem_attention_parts(q_ref, kb_ref, vb_ref, o_ref, col0):
    def head(h):
        lo, hi = h * HEAD_DIM, (h + 1) * HEAD_DIM
        s = lax.dot_general(q_ref[:, lo:hi], kb_ref[h], (((1,), (1,)), ((), ())), preferred_element_type=F32)
        m = jnp.max(s, axis=1, keepdims=True)
        p = jnp.exp(s - m).astype(BF16)
        ov = jnp.dot(p, vb_ref[h], preferred_element_type=F32)
        o_ref[:, col0 + lo:col0 + hi] = (ov[:, :HEAD_DIM] / ov[:, HEAD_DIM:]).astype(o_ref.dtype)

    return [functools.partial(head, h) for h in range(N_MEM_HEADS)]


def _gmlp_stage(u_ref, v_ref, q_ref, gv_ref, ws_ref, bs_ref, kb_ref, vb_ref, vn_ref, o_ref):
    tm = u_ref.shape[0]

    def norm_v():
        vn_ref[...] = _rms(v_ref[...].astype(F32), gv_ref[...]).astype(BF16)

    def group(g):
        row = lax.broadcasted_iota(jnp.int32, (CHUNK, CHUNK), 0)
        col = lax.broadcasted_iota(jnp.int32, (CHUNK, CHUNK), 1)
        w = jnp.where(row >= col, ws_ref[g], 0.0).astype(BF16)
        b = bs_ref[:, g:g + 1]
        c0, c1 = g * GROUP_DIM_A, (g + 1) * GROUP_DIM_A
        for c in range(tm // CHUNK):
            r0, r1 = c * CHUNK, (c + 1) * CHUNK
            s = jnp.dot(w, vn_ref[r0:r1, c0:c1], preferred_element_type=F32) + b
            o_ref[r0:r1, c0:c1] = (u_ref[r0:r1, c0:c1].astype(F32) * s).astype(o_ref.dtype)

    return ([norm_v] + [functools.partial(group, g) for g in range(N_GROUPS_A)]
            + _mem_attention_parts(q_ref, kb_ref, vb_ref, o_ref, MIXER_WIDTH))


def _mem_kv_spec(layer, batch_of):
    return pl.BlockSpec((None, None, N_MEM, 2, N_MEM_HEADS, HEAD_DIM),
                        lambda i: (layer, batch_of(i), 0, 0, 0, 0))


def _mix_out_kernel(*refs, stage, n_in, n_tiles, tiles_per_batch):
    mix_in = refs[:n_in]
    kv_ref, w_ref, x_ref, g_ref, o_ref, buf0_ref, buf1_ref, acc_ref, kb_ref, vb_ref = refs[n_in:n_in + 10]
    extra = refs[n_in + 10:]
    s = pl.program_id(0)

    @pl.when(s == 0)
    def _():
        buf1_ref[...] = jnp.zeros_like(buf1_ref)

    @pl.when(jnp.minimum(s, n_tiles - 1) % tiles_per_batch == 0)
    def _():
        _prepare_mem_kv(kv_ref, kb_ref, vb_ref)

    def step(dst_ref, src_ref):
        parts = stage(*mix_in, kb_ref, vb_ref, *extra, dst_ref)
        n_chunks = D_MODEL // OUT_CHUNK
        for c in range(n_chunks):
            cols = slice(c * OUT_CHUNK, (c + 1) * OUT_CHUNK)
            acc_ref[:, cols] = jnp.dot(src_ref[...], w_ref[:, cols], preferred_element_type=F32)
            for part in parts[c * len(parts) // n_chunks:(c + 1) * len(parts) // n_chunks]:
                part()
        o_ref[...] = x_ref[...] + _rms(acc_ref[...], g_ref[...])

    @pl.when(s % 2 == 0)
    def _():
        step(buf0_ref, buf1_ref)

    @pl.when(s % 2 == 1)
    def _():
        step(buf1_ref, buf0_ref)


def _mix_out(stage, mix_inputs, mix_specs, extra_scratch, mem_kv, layer, w, x, g, *, tm, rows_per_batch, name):
    m = x.shape[0]
    n_tiles = m // tm
    tiles_per_batch = rows_per_batch // tm
    mix_tile = lambda s: jnp.minimum(s, n_tiles - 1)
    out_tile = lambda s: jnp.maximum(s - 1, 0)
    row_spec = pl.BlockSpec((tm, D_MODEL), lambda s: (out_tile(s), 0))
    return pl.pallas_call(
        functools.partial(_mix_out_kernel, stage=stage, n_in=len(mix_inputs), n_tiles=n_tiles,
                          tiles_per_batch=tiles_per_batch),
        grid=(n_tiles + 1,),
        in_specs=[spec(mix_tile) for spec in mix_specs] + [
            _mem_kv_spec(layer, lambda s: mix_tile(s) // tiles_per_batch),
            pl.BlockSpec((D_MODEL, D_MODEL), lambda s: (0, 0), pipeline_mode=pl.Buffered(1)),
            row_spec,
            pl.BlockSpec((1, D_MODEL), lambda s: (0, 0)),
        ],
        out_specs=row_spec,
        out_shape=jax.ShapeDtypeStruct((m, D_MODEL), F32),
        scratch_shapes=[pltpu.VMEM((tm, D_MODEL), BF16), pltpu.VMEM((tm, D_MODEL), BF16),
                        pltpu.VMEM((tm, D_MODEL), F32)] + MEM_KV_SCRATCH + list(extra_scratch),
        compiler_params=_params("arbitrary"),
        name=name,
    )(*mix_inputs, mem_kv, w, x, g.reshape(1, D_MODEL))


def _gmlp_mix_out(zact, g_v, w_s, b_s, mem_kv, layer, w, x, g, *, tm, rows_per_batch):
    const = lambda shape: (lambda tile: pl.BlockSpec(shape, lambda s: (0,) * len(shape)))
    specs = [
        lambda tile: pl.BlockSpec((tm, MIXER_WIDTH), lambda s: (tile(s), 0)),
        lambda tile: pl.BlockSpec((tm, MIXER_WIDTH), lambda s: (tile(s), 1)),
        lambda tile: pl.BlockSpec((tm, MEM_WIDTH), lambda s: (tile(s), 2 * MIXER_WIDTH // MEM_WIDTH)),
        const((1, MIXER_WIDTH)),
        const((N_GROUPS_A, CHUNK, CHUNK)),
        const((CHUNK, N_GROUPS_A)),
    ]
    return _mix_out(_gmlp_stage, (zact, zact, zact, g_v.reshape(1, MIXER_WIDTH), w_s, b_s.T), specs,
                    [pltpu.VMEM((tm, MIXER_WIDTH), BF16)], mem_kv, layer, w, x, g,
                    tm=tm, rows_per_batch=rows_per_batch, name="gmlp_mix_out")


def _out_proj_cast_kernel(mix_ref, w_ref, x_ref, g_ref, o_ref, wb_ref, acc_ref):
    j = pl.program_id(0)
    w = w_ref[...].astype(BF16)
    wb_ref[...] = w
    acc_ref[j] = jnp.dot(mix_ref[...], w, preferred_element_type=F32)

    @pl.when(j == pl.num_programs(0) - 1)
    def _():
        o = jnp.concatenate([acc_ref[t] for t in range(acc_ref.shape[0])], axis=1)
        o_ref[...] = x_ref[...] + _rms(o, g_ref[...])


def _sample_out_proj(mix, w, layer, x, g):
    bd = mix.shape[0]
    mix = jnp.pad(mix.reshape(bd, D_MODEL), ((0, x.shape[0] - bd), (0, 0))).astype(BF16)
    return _out_proj_cast(mix, w, layer, x, g, tn=CAST_TILE)


def _out_proj_cast(mix, w, layer, x, g, *, tn):
    m = x.shape[0]
    n_tiles = D_MODEL // tn
    return pl.pallas_call(
        _out_proj_cast_kernel,
        grid=(n_tiles,),
        in_specs=[
            pl.BlockSpec((m, D_MODEL), lambda j: (0, 0)),
            pl.BlockSpec((None, D_MODEL, tn), lambda j: (layer, 0, j)),
            pl.BlockSpec((m, D_MODEL), lambda j: (0, 0)),
            pl.BlockSpec((1, D_MODEL), lambda j: (0, 0)),
        ],
        out_specs=[
            pl.BlockSpec((m, D_MODEL), lambda j: (0, 0)),
            pl.BlockSpec((D_MODEL, tn), lambda j: (0, j)),
        ],
        out_shape=[
            jax.ShapeDtypeStruct((m, D_MODEL), F32),
            jax.ShapeDtypeStruct((D_MODEL, D_MODEL), BF16),
        ],
        scratch_shapes=[pltpu.VMEM((n_tiles, m, tn), F32)],
        compiler_params=_params("arbitrary"),
        name="out_proj_cast",
    )(mix, w, x, g.reshape(1, D_MODEL))


def _accumulate(o_ref, ssq_ref, part, first):
    new = part if first else o_ref[...] + part
    o_ref[...] = new
    sq = new * new
    ssq_ref[...] = functools.reduce(
        jnp.add, [sq[:, c:c + HEAD_DIM] for c in range(0, sq.shape[1], HEAD_DIM)])


def _residual_norm(x_ref, o_ref, ssq_ref, g_ref):
    ms = jnp.sum(ssq_ref[...], axis=-1, keepdims=True) * (1.0 / o_ref.shape[1])
    o_ref[...] = x_ref[...] + o_ref[...] * lax.rsqrt(ms + EPS) * g_ref[...]


def _ffn_kernel(x_ref, gpre_ref, gpost_ref, wg_ref, wl_ref, wd_ref, o_ref, xn_ref, ssq_ref):
    f = pl.program_id(1)

    def step(first):
        xn = xn_ref[...]
        parts = []
        for j in range(wg_ref.shape[0]):
            hg = jnp.dot(xn, wg_ref[j], preferred_element_type=F32)
            hl = jnp.dot(xn, wl_ref[j], preferred_element_type=F32)
            parts.append((hg * jax.nn.sigmoid(hg) * hl).astype(BF16))
        a = jnp.concatenate(parts, axis=1)
        _accumulate(o_ref, ssq_ref, jnp.dot(a, wd_ref[...], preferred_element_type=F32), first)

    @pl.when(f == 0)
    def _():
        xn_ref[...] = _rms(x_ref[...], gpre_ref[...]).astype(BF16)
        step(True)

    @pl.when(f > 0)
    def _():
        step(False)

    @pl.when(f == pl.num_programs(1) - 1)
    def _():
        _residual_norm(x_ref, o_ref, ssq_ref, gpost_ref)


def _ffn(x, g_pre, g_post, weights, *, tm, tf, first_tile):
    m = x.shape[0]
    nf = D_FF // tf
    up = pl.BlockSpec((tf // weights[0].shape[2], D_MODEL, weights[0].shape[2]), lambda i, f: (f, 0, 0))
    rows = pl.BlockSpec((tm, D_MODEL), lambda i, f: (i + first_tile, 0))
    return pl.pallas_call(
        _ffn_kernel,
        grid=(m // tm - first_tile, nf),
        in_specs=[
            rows,
            pl.BlockSpec((1, D_MODEL), lambda i, f: (0, 0)),
            pl.BlockSpec((1, D_MODEL), lambda i, f: (0, 0)),
            up, up,
            pl.BlockSpec((tf, D_MODEL), lambda i, f: (f, 0)),
        ],
        out_specs=rows,
        out_shape=jax.ShapeDtypeStruct((m, D_MODEL), F32),
        input_output_aliases={0: 0},
        scratch_shapes=[pltpu.VMEM((tm, D_MODEL), BF16), pltpu.VMEM((tm, HEAD_DIM), F32)],
        compiler_params=_params("parallel", "arbitrary", vmem_limit=FFN_VMEM_LIMIT),
        name="ffn",
    )(x, g_pre.reshape(1, D_MODEL), g_post.reshape(1, D_MODEL), *weights)


def _ffn_head_kernel(x_ref, xs_ref, gpre_ref, gpost_ref, wg_ref, wl_ref, wd_ref,
                     o_ref, os_ref, wgb_ref, wlb_ref, wdb_ref, xn_ref, ssq_ref, ssqs_ref):
    tm = x_ref.shape[0]
    f = pl.program_id(0)

    def step(first):
        wg, wl, wd = (r[...].astype(BF16) for r in (wg_ref, wl_ref, wd_ref))
        wgb_ref[...] = wg
        wlb_ref[...] = wl
        wdb_ref[...] = wd
        xn = xn_ref[...]
        hg = jnp.dot(xn, wg, preferred_element_type=F32)
        hl = jnp.dot(xn, wl, preferred_element_type=F32)
        a = (hg * jax.nn.sigmoid(hg) * hl).astype(BF16)
        part = jnp.dot(a, wd, preferred_element_type=F32)
        _accumulate(o_ref, ssq_ref, part[:tm, :], first)
        _accumulate(os_ref, ssqs_ref, part[tm:, :], first)

    @pl.when(f == 0)
    def _():
        xn_ref[:tm, :] = _rms(x_ref[...], gpre_ref[...]).astype(BF16)
        xn_ref[tm:, :] = _rms(xs_ref[...], gpre_ref[...]).astype(BF16)
        step(True)

    @pl.when(f > 0)
    def _():
        step(False)

    @pl.when(f == pl.num_programs(0) - 1)
    def _():
        _residual_norm(x_ref, o_ref, ssq_ref, gpost_ref)
        _residual_norm(xs_ref, os_ref, ssqs_ref, gpost_ref)


def _ffn_head(x, xs, g_pre, g_post, w_up, w_down, layer, *, tm, tf):
    m = x.shape[0]
    ms = xs.shape[0]
    nf = D_FF // tf
    once = dict(pipeline_mode=pl.Buffered(1))
    head = pl.BlockSpec((tm, D_MODEL), lambda f: (0, 0), **once)
    sample = pl.BlockSpec((ms, D_MODEL), lambda f: (0, 0))
    vec = pl.BlockSpec((1, D_MODEL), lambda f: (0, 0))
    outs = pl.pallas_call(
        _ffn_head_kernel,
        grid=(nf,),
        in_specs=[
            head, sample, vec, vec,
            pl.BlockSpec((None, D_MODEL, tf), lambda f: (layer, 0, f)),
            pl.BlockSpec((None, D_MODEL, tf), lambda f: (layer, 0, nf + f)),
            pl.BlockSpec((None, tf, D_MODEL), lambda f: (layer, f, 0)),
        ],
        out_specs=[
            head, sample,
            pl.BlockSpec((None, D_MODEL, tf), lambda f: (f, 0, 0)),
            pl.BlockSpec((None, D_MODEL, tf), lambda f: (f, 0, 0)),
            pl.BlockSpec((tf, D_MODEL), lambda f: (f, 0)),
        ],
        out_shape=[
            jax.ShapeDtypeStruct((m, D_MODEL), F32),
            jax.ShapeDtypeStruct((ms, D_MODEL), F32),
            jax.ShapeDtypeStruct((nf, D_MODEL, tf), BF16),
            jax.ShapeDtypeStruct((nf, D_MODEL, tf), BF16),
            jax.ShapeDtypeStruct((D_FF, D_MODEL), BF16),
        ],
        input_output_aliases={0: 0},
        scratch_shapes=[pltpu.VMEM((tm + ms, D_MODEL), BF16), pltpu.VMEM((tm, HEAD_DIM), F32),
                        pltpu.VMEM((ms, HEAD_DIM), F32)],
        compiler_params=_params("arbitrary"),
        name="ffn_head",
    )(x, xs, g_pre.reshape(1, D_MODEL), g_post.reshape(1, D_MODEL), w_up, w_up, w_down)
    return outs[0], outs[1], tuple(outs[2:])


def _swa_kernel(q_ref, k_ref, v_ref, tb_ref, o_ref, lse_ref):
    n_units, n_res, u, _ = q_ref.shape
    per_blk = N_BACK // u
    n_blk = n_units // per_blk
    lane = lax.broadcasted_iota(jnp.int32, (N_BACK, HEAD_DIM), 1)
    ones = jnp.ones((2 * N_BACK, HEAD_DIM), BF16)

    def rows(ref, res, unit0, n_rows, lo, hi):
        return ref[pl.ds(unit0, n_rows // u), res, :, lo:hi].reshape(n_rows, hi - lo)

    def block(res, qu, ku, table):
        n_keys = N_BACK if table == 0 else 2 * N_BACK
        lse_tile = jnp.zeros((N_BACK, HEAD_DIM), F32)
        for h in range(HEADS_PER_GROUP):
            lo, hi = h * HEAD_DIM, (h + 1) * HEAD_DIM
            q = rows(q_ref, res, qu, N_BACK, lo, hi)
            kw = rows(k_ref, res, ku, n_keys, lo, hi)
            vw = rows(v_ref, res, ku, n_keys, lo, hi)
            s = lax.dot_general(q, kw, (((1,), (1,)), ((), ())), preferred_element_type=F32)
            s = s + tb_ref[table, h][:, :n_keys]
            m = jnp.max(s, axis=1, keepdims=True)
            p = jnp.exp(s - m).astype(BF16)
            ov = jnp.dot(p, jnp.concatenate([vw, ones[:n_keys]], axis=1), preferred_element_type=F32)
            den = ov[:, HEAD_DIM:]
            o = ov[:, :HEAD_DIM] / den
            o_ref[pl.ds(qu, per_blk), res, :, lo:hi] = o.reshape(per_blk, u, HEAD_DIM).astype(o_ref.dtype)
            lse_tile = jnp.where(lane == h, m, jnp.where(lane == HEADS_PER_GROUP + h, den, lse_tile))
        lse_ref[pl.ds(qu, per_blk), res, :, :] = lse_tile.reshape(per_blk, u, HEAD_DIM)

    def block_at(res, n):
        block(res, n * per_blk, (n - 1) * per_blk, 1)

    n_trips = (n_blk - 1) // SWA_BLOCKS_PER_TRIP
    for res in range(n_res):
        block(res, 0, 0, 0)

        def body(i, carry, res=res):
            for b in range(SWA_BLOCKS_PER_TRIP):
                block_at(res, 1 + SWA_BLOCKS_PER_TRIP * i + b)
            return carry

        if n_trips > 0:
            lax.fori_loop(0, n_trips, body, 0)
        for n in range(1 + n_trips * SWA_BLOCKS_PER_TRIP, n_blk):
            block_at(res, n)


def _sub_block(dil):
    return N_BACK if dil == 1 else PERM_BLOCK


def _swa_group(zb, tables, g, *, batch, seq):
    dil = SWA_PATTERN[g][1]
    sub = _sub_block(dil)
    n_units, u = seq // sub, sub // dil
    n_blk = n_units * u // N_BACK
    n_res = min(dil, max(1, SWA_BLOCKS_PER_TRIP // n_blk))
    view = zb.reshape(zb.shape[0], batch, n_units, dil, u, GROUP_WIDTH)

    def rows_in(tile):
        return pl.BlockSpec((None, None, n_units, n_res, u, GROUP_WIDTH), lambda i, r: (tile, i, 0, r, 0, 0))

    def rows_out(width):
        return pl.BlockSpec((None, n_units, n_res, u, width), lambda i, r: (i, 0, r, 0, 0))

    return pl.pallas_call(
        _swa_kernel,
        grid=(batch, dil // n_res),
        in_specs=[rows_in(g), rows_in(N_SWA_GROUPS + g), rows_in(2 * N_SWA_GROUPS + g),
                  pl.BlockSpec((None, 2, HEADS_PER_GROUP, N_BACK, 2 * N_BACK), lambda i, r: (g, 0, 0, 0, 0))],
        out_specs=[rows_out(GROUP_WIDTH), rows_out(HEAD_DIM)],
        out_shape=[
            jax.ShapeDtypeStruct((batch, n_units, dil, u, GROUP_WIDTH), BF16),
            jax.ShapeDtypeStruct((batch, n_units, dil, u, HEAD_DIM), F32),
        ],
        compiler_params=_params("parallel", "parallel"),
        name=f"swa_group{g}",
    )(view, view, view, tables)


def _split3(x):
    hi = x.astype(BF16)
    rest = x - hi.astype(F32)
    mid = rest.astype(BF16)
    lo = (rest - mid.astype(F32)).astype(BF16)
    return hi, mid, lo


def _merge_stage(o0_ref, o1_ref, o2_ref, l0_ref, l1_ref, l2_ref, q_ref, kb_ref, vb_ref, o_ref):
    tm = o_ref.shape[0]
    group_refs = ((o0_ref, l0_ref), (o1_ref, l1_ref), (o2_ref, l2_ref))
    outs, lses = [None] * N_SWA_GROUPS, [None] * N_SWA_GROUPS

    def token_order(g):
        o_g, l_g = group_refs[g]
        dil = SWA_PATTERN[g][1]
        o = o_g[...].reshape(tm, GROUP_WIDTH)
        l = l_g[...].reshape(tm, HEAD_DIM)
        if dil > 1:
            inv = _residue_major_perm(PERM_BLOCK, dil, transpose=True)
            l3 = _split3(l)
            o_nat, l_nat = [], []
            for s in range(0, tm, PERM_BLOCK):
                o_nat.append(jnp.dot(inv, o[s:s + PERM_BLOCK, :], preferred_element_type=F32))
                l_nat.append(sum(jnp.dot(inv, t[s:s + PERM_BLOCK, :], preferred_element_type=F32) for t in l3))
            o = jnp.concatenate(o_nat, axis=0)
            l = jnp.concatenate(l_nat, axis=0)
        outs[g] = o.astype(F32)
        lses[g] = l

    def merge_head(h):
        lo, hi = h * HEAD_DIM, (h + 1) * HEAD_DIM
        ls = [l[:, h:h + 1] for l in lses]
        ds = [l[:, HEADS_PER_GROUP + h:HEADS_PER_GROUP + h + 1] for l in lses]
        mx = jnp.maximum(jnp.maximum(ls[0], ls[1]), ls[2])
        es = [d * jnp.exp(l - mx) for l, d in zip(ls, ds)]
        tot = es[0] + es[1] + es[2]
        for g in range(N_SWA_GROUPS):
            alpha = es[g] / tot
            o_ref[:, g * GROUP_WIDTH + lo:g * GROUP_WIDTH + hi] = (outs[g][:, lo:hi] * alpha).astype(o_ref.dtype)

    return ([functools.partial(token_order, g) for g in range(N_SWA_GROUPS)]
            + [functools.partial(merge_head, h) for h in range(HEADS_PER_GROUP)]
            + _mem_attention_parts(q_ref, kb_ref, vb_ref, o_ref, MIXER_WIDTH))


def _swa_merge_out(outs, lses, zb, mem_kv, layer, w, x, g, *, rows_per_batch):
    tm = ROW_TILE
    tiles_per_batch = rows_per_batch // tm

    def group_tile(width, dil):
        sub = _sub_block(dil)
        return lambda tile: pl.BlockSpec(
            (None, tm // sub, dil, sub // dil, width),
            lambda s: (tile(s) // tiles_per_batch, tile(s) % tiles_per_batch, 0, 0, 0))

    specs = ([group_tile(GROUP_WIDTH, dil) for _, dil in SWA_PATTERN]
             + [group_tile(HEAD_DIM, dil) for _, dil in SWA_PATTERN]
             + [lambda tile: pl.BlockSpec((None, tm, MEM_WIDTH), lambda s: (3 * N_SWA_GROUPS, tile(s), 0))])
    return _mix_out(_merge_stage, (*outs, *lses, zb), specs, [], mem_kv, layer, w, x, g,
                    tm=tm, rows_per_batch=rows_per_batch, name="swa_merge_out")


def _head_rows(row, col0):
    return jnp.concatenate([row[:, col0 + h * HEAD_DIM:col0 + (h + 1) * HEAD_DIM]
                            for h in range(HEADS_PER_GROUP)], axis=0)


def _sample_attention(q4, kv_ref, bias=None, new=None):
    k3, v3 = kv_ref[:, 0], kv_ref[:, 1]
    s = jnp.sum(k3 * q4[None], axis=-1, keepdims=True) * ATTN_SCALE
    if bias is not None:
        s = s + bias
    m = jnp.max(s, axis=0)
    if new is not None:
        k_new, v_new, b_new = new
        s_new = jnp.sum(k_new * q4, axis=-1, keepdims=True) * ATTN_SCALE + b_new
        m = jnp.maximum(m, s_new)
    p = jnp.exp(s - m[None])
    den = jnp.sum(p, axis=0)
    o = jnp.sum(p * v3, axis=0)
    if new is not None:
        p_new = jnp.exp(s_new - m)
        den = den + p_new
        o = o + p_new * v_new
    return o / den, m + jnp.log(den)


def _store_head_rows(o_ref, col0, x4):
    for h in range(HEADS_PER_GROUP):
        o_ref[:, col0 + h * HEAD_DIM:col0 + (h + 1) * HEAD_DIM] = x4[h:h + 1, :]


def _sample_mem_attention(q_row, kv_ref, o_ref, col0):
    o, _ = _sample_attention(_head_rows(q_row, 0), kv_ref)
    _store_head_rows(o_ref, col0, o)


def _sample_mix_a_kernel(z_ref, gv_ref, w0_ref, b0_ref, kv_ref, o_ref, vrow_ref):
    u = z_ref[:, 0:MIXER_WIDTH]
    v = _rms(z_ref[:, MIXER_WIDTH:2 * MIXER_WIDTH], gv_ref[...])
    vrow_ref[...] = v
    o_ref[:, 0:MIXER_WIDTH] = u * (w0_ref[...] * v + b0_ref[...])
    _sample_mem_attention(z_ref[:, 2 * MIXER_WIDTH:2 * MIXER_WIDTH + MEM_WIDTH], kv_ref, o_ref, MIXER_WIDTH)


def _sample_mix_a(z, g_v, w_s, b_s, mem_kv, layer):
    bd = mem_kv.shape[1]
    w0 = jnp.repeat(w_s[:, 0, 0], GROUP_DIM_A).reshape(1, MIXER_WIDTH)
    b0 = jnp.repeat(b_s[:, 0], GROUP_DIM_A).reshape(1, MIXER_WIDTH)
    width = z.shape[1]
    vec = lambda i: (0, 0)
    return pl.pallas_call(
        _sample_mix_a_kernel,
        grid=(bd,),
        in_specs=[
            pl.BlockSpec((None, 1, width), lambda i: (i, 0, 0)),
            pl.BlockSpec((1, MIXER_WIDTH), vec),
            pl.BlockSpec((1, MIXER_WIDTH), vec),
            pl.BlockSpec((1, MIXER_WIDTH), vec),
            _mem_kv_spec(layer, lambda i: i),
        ],
        out_specs=[
            pl.BlockSpec((None, 1, D_MODEL), lambda i: (i, 0, 0)),
            pl.BlockSpec((None, 1, MIXER_WIDTH), lambda i: (i, 0, 0)),
        ],
        out_shape=[
            jax.ShapeDtypeStruct((bd, 1, D_MODEL), F32),
            jax.ShapeDtypeStruct((bd, 1, MIXER_WIDTH), F32),
        ],
        compiler_params=_params("parallel"),
        name="sample_mix_a",
    )(z[:bd].reshape(bd, 1, width), g_v.reshape(1, MIXER_WIDTH), w0, b0, mem_kv)


def _sample_mix_b_kernel(z_ref, c0_ref, c1_ref, c2_ref, bcol_ref, bnew_ref, kv_ref, o_ref):
    caches = (c0_ref, c1_ref, c2_ref)
    z = z_ref[...]
    outs, lses = [], []
    for g in range(N_SWA_GROUPS):
        c0 = g * GROUP_WIDTH
        new = (_head_rows(z, MIXER_WIDTH + c0), _head_rows(z, 2 * MIXER_WIDTH + c0), bnew_ref[g])
        o, lse = _sample_attention(_head_rows(z, c0), caches[g], bias=bcol_ref[g], new=new)
        outs.append(o)
        lses.append(lse)
    mx = jnp.maximum(jnp.maximum(lses[0], lses[1]), lses[2])
    es = [jnp.exp(l - mx) for l in lses]
    tot = es[0] + es[1] + es[2]
    for g in range(N_SWA_GROUPS):
        _store_head_rows(o_ref, g * GROUP_WIDTH, outs[g] * (es[g] / tot))
    _sample_mem_attention(z[:, 3 * MIXER_WIDTH:3 * MIXER_WIDTH + MEM_WIDTH], kv_ref, o_ref, MIXER_WIDTH)


def _sample_mix_b(z, win_caches, swa_layer, bias_groups, mem_kv, layer):
    bd = mem_kv.shape[1]
    width = z.shape[1]
    cache_views, cache_specs = [], []
    for g, (win, dil) in enumerate(SWA_PATTERN):
        c = win_caches[g]
        cache_views.append(c.reshape(c.shape[0], bd, win // dil, dil, 2, HEADS_PER_GROUP, HEAD_DIM))
        cache_specs.append(pl.BlockSpec((None, None, N_BACK, None, 2, HEADS_PER_GROUP, HEAD_DIM),
                                        lambda i: (swa_layer, i, 0, 0, 0, 0, 0)))
    bcol = jnp.stack([bg[:, N_BACK:0:-1].T for bg in bias_groups], axis=0)
    bcol = jnp.broadcast_to(bcol[..., None], bcol.shape + (HEAD_DIM,))
    bnew = jnp.stack([bg[:, 0] for bg in bias_groups], axis=0)
    bnew = jnp.broadcast_to(bnew[..., None], bnew.shape + (HEAD_DIM,))
    return pl.pallas_call(
        _sample_mix_b_kernel,
        grid=(bd,),
        in_specs=[pl.BlockSpec((None, 1, width), lambda i: (i, 0, 0))] + cache_specs + [
            pl.BlockSpec((N_SWA_GROUPS, N_BACK, HEADS_PER_GROUP, HEAD_DIM), lambda i: (0, 0, 0, 0)),
            pl.BlockSpec((N_SWA_GROUPS, HEADS_PER_GROUP, HEAD_DIM), lambda i: (0, 0, 0)),
            _mem_kv_spec(layer, lambda i: i),
        ],
        out_specs=pl.BlockSpec((None, 1, D_MODEL), lambda i: (i, 0, 0)),
        out_shape=jax.ShapeDtypeStruct((bd, 1, D_MODEL), F32),
        compiler_params=_params("parallel"),
        name="sample_mix_b",
    )(z[:bd].reshape(bd, 1, width), *cache_views, bcol, bnew, mem_kv)


def _t5_bucket(dist):
    nf = jnp.maximum(dist, MAX_EXACT).astype(F32)
    large = MAX_EXACT + (jnp.log(nf / MAX_EXACT) / math.log(MAX_DISTANCE / MAX_EXACT)
                         * (N_BUCKETS - MAX_EXACT)).astype(jnp.int32)
    large = jnp.minimum(large, N_BUCKETS - 1)
    return jnp.where(dist < MAX_EXACT, dist, large)


def _group_bias(rel_bias, g, dil):
    dist = jnp.arange(N_BACK + 1, dtype=jnp.int32) * dil
    b = rel_bias[_t5_bucket(dist)][:, g * HEADS_PER_GROUP:(g + 1) * HEADS_PER_GROUP]
    return b.T.astype(F32)


def _band_tables_kernel(b_ref, o_ref):
    n, rows, width = o_ref.shape
    for x in range(n):
        row = jnp.broadcast_to(b_ref[x:x + 1, :], (rows, width))
        o_ref[x] = pltpu.roll(row, 0, 1, stride=1, stride_axis=0)


def _band_tables(bias_groups):
    width = 2 * N_BACK
    rows = []
    for bias_j in bias_groups:
        masked = jnp.full((HEADS_PER_GROUP, N_BACK - 1), NEG_INF, F32)
        rows.append(jnp.concatenate([bias_j[:, :1], masked, bias_j[:, N_BACK:0:-1]], axis=1))
        rows.append(jnp.concatenate([bias_j[:, ::-1], masked], axis=1))
    base = jnp.stack(rows, axis=0).reshape(-1, width)
    tabs = pl.pallas_call(
        _band_tables_kernel,
        out_shape=jax.ShapeDtypeStruct((base.shape[0], N_BACK, width), F32),
        name="band_tables",
    )(base)
    return tabs.reshape(N_SWA_GROUPS, 2, HEADS_PER_GROUP, N_BACK, width)


def _kv_tail_kernel(k_ref, v_ref, o_ref, *, dil):
    rows = k_ref.shape[0]
    for kv, ref in enumerate((k_ref, v_ref)):
        x = ref[...]
        if dil > 1:
            inv = _residue_major_perm(PERM_BLOCK, dil, transpose=True)
            x = jnp.concatenate([jnp.dot(inv, x[s:s + PERM_BLOCK, :], preferred_element_type=F32)
                                 for s in range(0, rows, PERM_BLOCK)], axis=0)
        x = x.astype(F32)
        for h in range(HEADS_PER_GROUP):
            o_ref[:, kv, h, :] = x[:, h * HEAD_DIM:(h + 1) * HEAD_DIM]


def _kv_tail(zb, g, *, batch, seq):
    win, dil = SWA_PATTERN[g]
    rows = min(win, FFN_ROW_TILE)
    first = (seq - win) // rows
    per_batch = seq // rows

    def tile(t):
        return pl.BlockSpec((None, rows, GROUP_WIDTH), lambda b, s: (t, b * per_batch + first + s, 0))

    return pl.pallas_call(
        functools.partial(_kv_tail_kernel, dil=dil),
        grid=(batch, win // rows),
        in_specs=[tile(N_SWA_GROUPS + g), tile(2 * N_SWA_GROUPS + g)],
        out_specs=pl.BlockSpec((None, rows, 2, HEADS_PER_GROUP, HEAD_DIM), lambda b, s: (b, s, 0, 0, 0)),
        out_shape=jax.ShapeDtypeStruct((batch, win, 2, HEADS_PER_GROUP, HEAD_DIM), F32),
        compiler_params=_params("parallel", "parallel"),
        name=f"kv_tail{g}",
    )(zb, zb)


def kernel(x_prompt, x_sample, mem_prompt, cache_mem_kv, cache_win128_kv, cache_win512_kv, cache_win2048_kv, rel_bias, norm_mix_pre, norm_mix_post, norm_ffn_pre, norm_ffn_post, norm_mem, w_mem_kv, w_in_a, norm_v_a, w_spatial_a, b_spatial_a, w_in_b, w_out, w_ffn_up, w_ffn_down):
    batch, seq, _ = x_prompt.shape
    bd = x_sample.shape[0]
    depth = w_out.shape[0]
    m_p = batch * seq
    win_caches = (cache_win128_kv, cache_win512_kv, cache_win2048_kv)

    bias_groups = [_group_bias(rel_bias, g, dil) for g, (_, dil) in enumerate(SWA_PATTERN)]
    band_tables = _band_tables(bias_groups)

    yp = x_prompt.reshape(m_p, D_MODEL)
    ys = jnp.pad(x_sample.reshape(bd, D_MODEL), ((0, SAMPLE_PAD - bd), (0, 0)))
    mem_rows = mem_prompt.reshape(batch * N_MEM, D_MODEL)

    mem_kv_p = _mem_kv(mem_rows, norm_mem, w_mem_kv, batch=batch)
    chunk_v_s = []
    win_p = [[] for _ in SWA_PATTERN]
    win_s = [[] for _ in SWA_PATTERN]
    for i in range(depth):
        li = i // 2
        if i % 2 == 0:
            zp, zs = _in_proj_a(yp, ys, norm_mix_pre[i], w_in_a, li, tm=ROW_TILE, tn=CAST_TILE,
                                gelu_cols=2 * MIXER_WIDTH)
            mix_s, v_rows = _sample_mix_a(zs, norm_v_a[li], w_spatial_a[li], b_spatial_a[li], cache_mem_kv, i)
            chunk_v_s.append(v_rows)
            ys, w_o = _sample_out_proj(mix_s, w_out, i, ys, norm_mix_post[i])
            yp = _gmlp_mix_out(zp, norm_v_a[li], w_spatial_a[li], b_spatial_a[li], mem_kv_p, i,
                               w_o, yp, norm_mix_post[i], tm=ROW_TILE, rows_per_batch=seq)
        else:
            zb, zs = _in_proj_b(yp, ys, norm_mix_pre[i], w_in_b, li, tm=ROW_TILE)
            outs, lses = [], []
            for g, (win, dil) in enumerate(SWA_PATTERN):
                o, lse = _swa_group(zb, band_tables, g, batch=batch, seq=seq)
                outs.append(o)
                lses.append(lse)
                win_p[g].append(_kv_tail(zb, g, batch=batch, seq=seq))
                kv_new = zs[:bd, MIXER_WIDTH:3 * MIXER_WIDTH]
                kv_new = kv_new.reshape(bd, 1, 2, N_SWA_GROUPS, HEADS_PER_GROUP, HEAD_DIM)[:, :, :, g]
                win_s[g].append(kv_new)
            mix_s = _sample_mix_b(zs, win_caches, li, bias_groups, cache_mem_kv, i)
            ys, w_o = _sample_out_proj(mix_s, w_out, i, ys, norm_mix_post[i])
            yp = _swa_merge_out(outs, lses, zb, mem_kv_p, i, w_o, yp, norm_mix_post[i], rows_per_batch=seq)
        yp, ys, w_ffn = _ffn_head(yp, ys, norm_ffn_pre[i], norm_ffn_post[i], w_ffn_up, w_ffn_down, i,
                                  tm=FFN_ROW_TILE, tf=HEAD_FF_TILE)
        yp = _ffn(yp, norm_ffn_pre[i], norm_ffn_post[i], w_ffn, tm=FFN_ROW_TILE, tf=FF_TILE, first_tile=1)

    return (
        yp.reshape(batch, seq, D_MODEL),
        ys[:bd].reshape(bd, 1, D_MODEL),
        mem_kv_p,
        jnp.stack(chunk_v_s, axis=0),
        jnp.stack(win_p[0], axis=0),
        jnp.stack(win_p[1], axis=0),
        jnp.stack(win_p[2], axis=0),
        jnp.stack(win_s[0], axis=0),
        jnp.stack(win_s[1], axis=0),
        jnp.stack(win_s[2], axis=0),
    )
```

```python
import functools
import math

import jax
import jax.numpy as jnp
from jax import lax
from jax.experimental import pallas as pl
from jax.experimental.pallas import tpu as pltpu

F32 = jnp.float32
BF16 = jnp.bfloat16

D_MODEL = 2048
HEAD_DIM = 128
N_MEM = 256
N_MEM_HEADS = 4
MEM_WIDTH = N_MEM_HEADS * HEAD_DIM
MIXER_WIDTH = D_MODEL - MEM_WIDTH
CHUNK = 128
N_GROUPS_A = 4
GROUP_DIM_A = MIXER_WIDTH // N_GROUPS_A
SWA_PATTERN = ((128, 1), (512, 4), (2048, 16))
N_SWA_GROUPS = len(SWA_PATTERN)
HEADS_PER_GROUP = 4
GROUP_WIDTH = HEADS_PER_GROUP * HEAD_DIM
N_BACK = 128
N_BUCKETS = 32
MAX_EXACT = N_BUCKETS // 2
MAX_DISTANCE = 2048
D_FF = 5632
EPS = 1e-6
NEG_INF = -1e30
ATTN_SCALE = HEAD_DIM ** -0.5
SAMPLE_PAD = 16
PERM_BLOCK = 256
SWA_BLOCKS_PER_TRIP = 16
OUT_CHUNK = 256

ROW_TILE = 512
FFN_ROW_TILE = 1024
FF_TILE = 512
HEAD_FF_TILE = 256
CAST_TILE = 512

V7X_VMEM_BYTES = 64 * 1024 * 1024
VMEM_LIMIT = V7X_VMEM_BYTES - 8 * 1024 * 1024
FFN_VMEM_LIMIT = V7X_VMEM_BYTES - 2 * 1024 * 1024


def _params(*sem, vmem_limit=VMEM_LIMIT):
    return pltpu.CompilerParams(dimension_semantics=sem, vmem_limit_bytes=vmem_limit)


def _gelu(x):
    return 0.5 * x * (1.0 + jnp.tanh(0.7978845608028654 * (x + 0.044715 * (x * x * x))))


def _rms(x, g):
    return x * lax.rsqrt(jnp.mean(x * x, axis=-1, keepdims=True) + EPS) * g


def _log2(n):
    assert n & (n - 1) == 0
    return n.bit_length() - 1


def _residue_major_perm(tm, dil, transpose=False):
    n = tm // dil
    row = lax.broadcasted_iota(jnp.int32, (tm, tm), 0)
    col = lax.broadcasted_iota(jnp.int32, (tm, tm), 1)
    dst, src = (col, row) if transpose else (row, col)
    want = lax.shift_left(jnp.bitwise_and(dst, n - 1), _log2(dil)) + lax.shift_right_logical(dst, _log2(n))
    return (src == want).astype(BF16)


def _in_proj_a_kernel(x_ref, xs_ref, g_ref, w_ref, zp_ref, zs_ref, wb_ref, xn_ref, *, n_col, gelu_cols):
    tm = x_ref.shape[0]
    tn = w_ref.shape[1]
    s = pl.program_id(0)

    def prompt_cols(acc, c0):
        col = lax.broadcasted_iota(jnp.int32, acc.shape, 1) + c0
        return jnp.where(col < gelu_cols, _gelu(acc), acc * ATTN_SCALE).astype(zp_ref.dtype)

    @pl.when(s == 0)
    def _():
        xn_ref[:tm, :] = _rms(x_ref[...], g_ref[...]).astype(BF16)
        xn_ref[tm:, :] = _rms(xs_ref[...], g_ref[...]).astype(BF16)

    for j in range(n_col):
        @pl.when(s == j)
        def _(j=j):
            cols = slice(j * tn, (j + 1) * tn)
            w = w_ref[...].astype(BF16)
            wb_ref[:, cols] = w
            acc = jnp.dot(xn_ref[...], w, preferred_element_type=F32)
            zp_ref[:, cols] = prompt_cols(acc[:tm, :], j * tn)
            sample = acc[tm:, :]
            zs_ref[:, cols] = _gelu(sample) if (j + 1) * tn <= gelu_cols else sample

    @pl.when(s >= n_col)
    def _():
        xn = _rms(x_ref[...], g_ref[...]).astype(BF16)
        acc = jnp.dot(xn, wb_ref[...], preferred_element_type=F32)
        zp_ref[...] = prompt_cols(acc, 0)


def _in_proj_a(x, xs, g, w, layer, *, tm, tn, gelu_cols):
    m, k = x.shape
    ms = xs.shape[0]
    n = w.shape[2]
    n_col, n_row = n // tn, m // tm
    assert gelu_cols % tn == 0
    row = lambda s: jnp.maximum(s - (n_col - 1), 0)
    return pl.pallas_call(
        functools.partial(_in_proj_a_kernel, n_col=n_col, gelu_cols=gelu_cols),
        grid=(n_col + n_row - 1,),
        in_specs=[
            pl.BlockSpec((tm, k), lambda s: (row(s), 0)),
            pl.BlockSpec((ms, k), lambda s: (0, 0)),
            pl.BlockSpec((1, k), lambda s: (0, 0)),
            pl.BlockSpec((None, k, tn), lambda s: (layer, 0, jnp.minimum(s, n_col - 1))),
        ],
        out_specs=[
            pl.BlockSpec((tm, n), lambda s: (row(s), 0)),
            pl.BlockSpec((ms, n), lambda s: (0, 0)),
        ],
        out_shape=[
            jax.ShapeDtypeStruct((m, n), BF16),
            jax.ShapeDtypeStruct((ms, n), F32),
        ],
        scratch_shapes=[pltpu.VMEM((k, n), BF16), pltpu.VMEM((tm + ms, k), BF16)],
        compiler_params=_params("arbitrary"),
        name="in_proj_a",
    )(x, xs, g.reshape(1, k), w)


def _mem_kv_kernel(x_ref, g_ref, w_ref, o_ref):
    xn = _rms(x_ref[...], g_ref[...]).astype(BF16)
    acc = jnp.dot(xn, w_ref[...].astype(BF16), preferred_element_type=F32)
    for kv in range(2):
        for h in range(N_MEM_HEADS):
            c0 = (kv * N_MEM_HEADS + h) * HEAD_DIM
            o_ref[:, kv, h, :] = acc[:, c0:c0 + HEAD_DIM]


def _mem_kv(mem_rows, g, w, *, batch):
    m, k = mem_rows.shape
    layers = w.shape[0]
    out = pl.pallas_call(
        _mem_kv_kernel,
        grid=(layers,),
        in_specs=[
            pl.BlockSpec((m, k), lambda l: (0, 0)),
            pl.BlockSpec((None, 1, k), lambda l: (l, 0, 0)),
            pl.BlockSpec((None, k, 2 * MEM_WIDTH), lambda l: (l, 0, 0)),
        ],
        out_specs=pl.BlockSpec((None, m, 2, N_MEM_HEADS, HEAD_DIM), lambda l: (l, 0, 0, 0, 0)),
        out_shape=jax.ShapeDtypeStruct((layers, m, 2, N_MEM_HEADS, HEAD_DIM), F32),
        compiler_params=_params("parallel"),
        name="mem_kv",
    )(mem_rows, g.reshape(layers, 1, k), w)
    return out.reshape(layers, batch, m // batch, 2, N_MEM_HEADS, HEAD_DIM)


def _in_proj_b_kernel(x_ref, xs_ref, g_ref, w_ref, o_ref, zs_ref, wb_ref, xn_ref):
    tm = x_ref.shape[0]
    n_tiles = o_ref.shape[0]
    n_qkv = 3 * N_SWA_GROUPS
    s = pl.program_id(0)

    def row_orders():
        xn = _rms(x_ref[...], g_ref[...]).astype(BF16)
        xn_ref[0, :tm, :] = xn
        for g in range(1, N_SWA_GROUPS):
            perm = _residue_major_perm(PERM_BLOCK, SWA_PATTERN[g][1])
            for r in range(0, tm, PERM_BLOCK):
                xn_ref[g, r:r + PERM_BLOCK, :] = jnp.dot(
                    perm, xn[r:r + PERM_BLOCK, :], preferred_element_type=F32).astype(BF16)

    def prompt_tile(t, acc):
        if t < N_SWA_GROUPS or t == n_qkv:
            acc = acc * ATTN_SCALE
        o_ref[t] = acc.astype(o_ref.dtype)

    src = lambda t: t % N_SWA_GROUPS if t < n_qkv else 0

    @pl.when(s == 0)
    def _():
        row_orders()
        xs = _rms(xs_ref[...], g_ref[...]).astype(BF16)
        for g in range(N_SWA_GROUPS):
            xn_ref[g, tm:, :] = xs

    for t in range(n_tiles):
        @pl.when(s == t)
        def _(t=t):
            cols = slice(t * GROUP_WIDTH, (t + 1) * GROUP_WIDTH)
            w = w_ref[...].astype(BF16)
            wb_ref[:, cols] = w
            acc = jnp.dot(xn_ref[src(t)], w, preferred_element_type=F32)
            prompt_tile(t, acc[:tm, :])
            zs_ref[:, cols] = acc[tm:, :]

    @pl.when(s >= n_tiles)
    def _():
        row_orders()
        for t in range(n_tiles):
            w = wb_ref[:, t * GROUP_WIDTH:(t + 1) * GROUP_WIDTH]
            prompt_tile(t, jnp.dot(xn_ref[src(t), :tm, :], w, preferred_element_type=F32))


def _in_proj_b(x, xs, g, w, layer, *, tm):
    m, k = x.shape
    ms = xs.shape[0]
    n = w.shape[2]
    n_tiles = n // GROUP_WIDTH
    row = lambda s: jnp.maximum(s - (n_tiles - 1), 0)
    return pl.pallas_call(
        _in_proj_b_kernel,
        grid=(n_tiles + m // tm - 1,),
        in_specs=[
            pl.BlockSpec((tm, k), lambda s: (row(s), 0)),
            pl.BlockSpec((ms, k), lambda s: (0, 0)),
            pl.BlockSpec((1, k), lambda s: (0, 0)),
            pl.BlockSpec((None, k, GROUP_WIDTH), lambda s: (layer, 0, jnp.minimum(s, n_tiles - 1))),
        ],
        out_specs=[
            pl.BlockSpec((n_tiles, tm, GROUP_WIDTH), lambda s: (0, row(s), 0)),
            pl.BlockSpec((ms, n), lambda s: (0, 0)),
        ],
        out_shape=[
            jax.ShapeDtypeStruct((n_tiles, m, GROUP_WIDTH), BF16),
            jax.ShapeDtypeStruct((ms, n), F32),
        ],
        scratch_shapes=[pltpu.VMEM((k, n), BF16), pltpu.VMEM((N_SWA_GROUPS, tm + ms, k), BF16)],
        compiler_params=_params("arbitrary"),
        name="in_proj_b",
    )(x, xs, g.reshape(1, k), w)


MEM_KV_SCRATCH = [pltpu.VMEM((N_MEM_HEADS, N_MEM, HEAD_DIM), BF16),
                  pltpu.VMEM((N_MEM_HEADS, N_MEM, 2 * HEAD_DIM), BF16)]


def _prepare_mem_kv(kv_ref, kb_ref, vb_ref):
    for h in range(N_MEM_HEADS):
        kb_ref[h] = kv_ref[:, 0, h, :].astype(BF16)
        vb_ref[h, :, :HEAD_DIM] = kv_ref[:, 1, h, :].astype(BF16)
        vb_ref[h, :, HEAD_DIM:] = jnp.ones((N_MEM, HEAD_DIM), BF16)


def _mem_attention_parts(q_ref, kb_ref, vb_ref, o_ref, col0):
    def head(h):
        lo, hi = h * HEAD_DIM, (h + 1) * HEAD_DIM
        s = lax.dot_general(q_ref[:, lo:hi], kb_ref[h], (((1,), (1,)), ((), ())), preferred_element_type=F32)
        m = jnp.max(s, axis=1, keepdims=True)
        p = jnp.exp(s - m).astype(BF16)
        ov = jnp.dot(p, vb_ref[h], preferred_element_type=F32)
        o_ref[:, col0 + lo:col0 + hi] = (ov[:, :HEAD_DIM] / ov[:, HEAD_DIM:]).astype(o_ref.dtype)

    return [functools.partial(head, h) for h in range(N_MEM_HEADS)]


def _gmlp_stage(u_ref, v_ref, q_ref, gv_ref, ws_ref, bs_ref, kb_ref, vb_ref, vn_ref, o_ref):
    tm = u_ref.shape[0]

    def norm_v():
        vn_ref[...] = _rms(v_ref[...].astype(F32), gv_ref[...]).astype(BF16)

    def group(g):
        row = lax.broadcasted_iota(jnp.int32, (CHUNK, CHUNK), 0)
        col = lax.broadcasted_iota(jnp.int32, (CHUNK, CHUNK), 1)
        w = jnp.where(row >= col, ws_ref[g], 0.0).astype(BF16)
        b = bs_ref[:, g:g + 1]
        c0, c1 = g * GROUP_DIM_A, (g + 1) * GROUP_DIM_A
        for c in range(tm // CHUNK):
            r0, r1 = c * CHUNK, (c + 1) * CHUNK
            s = jnp.dot(w, vn_ref[r0:r1, c0:c1], preferred_element_type=F32) + b
            o_ref[r0:r1, c0:c1] = (u_ref[r0:r1, c0:c1].astype(F32) * s).astype(o_ref.dtype)

    return ([norm_v] + [functools.partial(group, g) for g in range(N_GROUPS_A)]
            + _mem_attention_parts(q_ref, kb_ref, vb_ref, o_ref, MIXER_WIDTH))


def _mem_kv_spec(layer, batch_of):
    return pl.BlockSpec((None, None, N_MEM, 2, N_MEM_HEADS, HEAD_DIM),
                        lambda i: (layer, batch_of(i), 0, 0, 0, 0))


def _mix_out_kernel(*refs, stage, n_in, n_tiles, tiles_per_batch):
    mix_in = refs[:n_in]
    kv_ref, w_ref, x_ref, g_ref, o_ref, buf0_ref, buf1_ref, acc_ref, kb_ref, vb_ref = refs[n_in:n_in + 10]
    extra = refs[n_in + 10:]
    s = pl.program_id(0)

    @pl.when(s == 0)
    def _():
        buf1_ref[...] = jnp.zeros_like(buf1_ref)

    @pl.when(jnp.minimum(s, n_tiles - 1) % tiles_per_batch == 0)
    def _():
        _prepare_mem_kv(kv_ref, kb_ref, vb_ref)

    def step(dst_ref, src_ref):
        parts = stage(*mix_in, kb_ref, vb_ref, *extra, dst_ref)
        n_chunks = D_MODEL // OUT_CHUNK
        for c in range(n_chunks):
            cols = slice(c * OUT_CHUNK, (c + 1) * OUT_CHUNK)
            acc_ref[:, cols] = jnp.dot(src_ref[...], w_ref[:, cols], preferred_element_type=F32)
            for part in parts[c * len(parts) // n_chunks:(c + 1) * len(parts) // n_chunks]:
                part()
        o_ref[...] = x_ref[...] + _rms(acc_ref[...], g_ref[...])

    @pl.when(s % 2 == 0)
    def _():
        step(buf0_ref, buf1_ref)

    @pl.when(s % 2 == 1)
    def _():
        step(buf1_ref, buf0_ref)


def _mix_out(stage, mix_inputs, mix_specs, extra_scratch, mem_kv, layer, w, x, g, *, tm, rows_per_batch, name):
    m = x.shape[0]
    n_tiles = m // tm
    tiles_per_batch = rows_per_batch // tm
    mix_tile = lambda s: jnp.minimum(s, n_tiles - 1)
    out_tile = lambda s: jnp.maximum(s - 1, 0)
    row_spec = pl.BlockSpec((tm, D_MODEL), lambda s: (out_tile(s), 0))
    return pl.pallas_call(
        functools.partial(_mix_out_kernel, stage=stage, n_in=len(mix_inputs), n_tiles=n_tiles,
                          tiles_per_batch=tiles_per_batch),
        grid=(n_tiles + 1,),
        in_specs=[spec(mix_tile) for spec in mix_specs] + [
            _mem_kv_spec(layer, lambda s: mix_tile(s) // tiles_per_batch),
            pl.BlockSpec((D_MODEL, D_MODEL), lambda s: (0, 0), pipeline_mode=pl.Buffered(1)),
            row_spec,
            pl.BlockSpec((1, D_MODEL), lambda s: (0, 0)),
        ],
        out_specs=row_spec,
        out_shape=jax.ShapeDtypeStruct((m, D_MODEL), F32),
        scratch_shapes=[pltpu.VMEM((tm, D_MODEL), BF16), pltpu.VMEM((tm, D_MODEL), BF16),
                        pltpu.VMEM((tm, D_MODEL), F32)] + MEM_KV_SCRATCH + list(extra_scratch),
        compiler_params=_params("arbitrary"),
        name=name,
    )(*mix_inputs, mem_kv, w, x, g.reshape(1, D_MODEL))


def _gmlp_mix_out(zact, g_v, w_s, b_s, mem_kv, layer, w, x, g, *, tm, rows_per_batch):
    const = lambda shape: (lambda tile: pl.BlockSpec(shape, lambda s: (0,) * len(shape)))
    specs = [
        lambda tile: pl.BlockSpec((tm, MIXER_WIDTH), lambda s: (tile(s), 0)),
        lambda tile: pl.BlockSpec((tm, MIXER_WIDTH), lambda s: (tile(s), 1)),
        lambda tile: pl.BlockSpec((tm, MEM_WIDTH), lambda s: (tile(s), 2 * MIXER_WIDTH // MEM_WIDTH)),
        const((1, MIXER_WIDTH)),
        const((N_GROUPS_A, CHUNK, CHUNK)),
        const((CHUNK, N_GROUPS_A)),
    ]
    return _mix_out(_gmlp_stage, (zact, zact, zact, g_v.reshape(1, MIXER_WIDTH), w_s, b_s.T), specs,
                    [pltpu.VMEM((tm, MIXER_WIDTH), BF16)], mem_kv, layer, w, x, g,
                    tm=tm, rows_per_batch=rows_per_batch, name="gmlp_mix_out")


def _out_proj_cast_kernel(mix_ref, w_ref, x_ref, g_ref, o_ref, wb_ref, acc_ref):
    j = pl.program_id(0)
    w = w_ref[...].astype(BF16)
    wb_ref[...] = w
    acc_ref[j] = jnp.dot(mix_ref[...], w, preferred_element_type=F32)

    @pl.when(j == pl.num_programs(0) - 1)
    def _():
        o = jnp.concatenate([acc_ref[t] for t in range(acc_ref.shape[0])], axis=1)
        o_ref[...] = x_ref[...] + _rms(o, g_ref[...])


def _sample_out_proj(mix, w, layer, x, g):
    bd = mix.shape[0]
    mix = jnp.pad(mix.reshape(bd, D_MODEL), ((0, x.shape[0] - bd), (0, 0))).astype(BF16)
    return _out_proj_cast(mix, w, layer, x, g, tn=CAST_TILE)


def _out_proj_cast(mix, w, layer, x, g, *, tn):
    m = x.shape[0]
    n_tiles = D_MODEL // tn
    return pl.pallas_call(
        _out_proj_cast_kernel,
        grid=(n_tiles,),
        in_specs=[
            pl.BlockSpec((m, D_MODEL), lambda j: (0, 0)),
            pl.BlockSpec((None, D_MODEL, tn), lambda j: (layer, 0, j)),
            pl.BlockSpec((m, D_MODEL), lambda j: (0, 0)),
            pl.BlockSpec((1, D_MODEL), lambda j: (0, 0)),
        ],
        out_specs=[
            pl.BlockSpec((m, D_MODEL), lambda j: (0, 0)),
            pl.BlockSpec((D_MODEL, tn), lambda j: (0, j)),
        ],
        out_shape=[
            jax.ShapeDtypeStruct((m, D_MODEL), F32),
            jax.ShapeDtypeStruct((D_MODEL, D_MODEL), BF16),
        ],
        scratch_shapes=[pltpu.VMEM((n_tiles, m, tn), F32)],
        compiler_params=_params("arbitrary"),
        name="out_proj_cast",
    )(mix, w, x, g.reshape(1, D_MODEL))


def _accumulate(o_ref, ssq_ref, part, first):
    new = part if first else o_ref[...] + part
    o_ref[...] = new
    sq = new * new
    ssq_ref[...] = functools.reduce(
        jnp.add, [sq[:, c:c + HEAD_DIM] for c in range(0, sq.shape[1], HEAD_DIM)])


def _residual_norm(x_ref, o_ref, ssq_ref, g_ref):
    ms = jnp.sum(ssq_ref[...], axis=-1, keepdims=True) * (1.0 / o_ref.shape[1])
    o_ref[...] = x_ref[...] + o_ref[...] * lax.rsqrt(ms + EPS) * g_ref[...]


def _ffn_kernel(x_ref, gpre_ref, gpost_ref, wg_hbm, wl_hbm, wd_hbm, o_ref,
                xn_ref, ssq_ref, wg_buf, wl_buf, wd_buf, sem):
    i = pl.program_id(0)
    n_sub, tf = wg_buf.shape[1], wd_buf.shape[1]
    nf = wd_hbm.shape[0] // tf

    def copies(f, slot):
        return (
            pltpu.make_async_copy(wg_hbm.at[pl.ds(f * n_sub, n_sub)], wg_buf.at[slot], sem.at[0, slot]),
            pltpu.make_async_copy(wl_hbm.at[pl.ds(f * n_sub, n_sub)], wl_buf.at[slot], sem.at[1, slot]),
            pltpu.make_async_copy(wd_hbm.at[pl.ds(f * tf, tf)], wd_buf.at[slot], sem.at[2, slot]),
        )

    def start(f, slot):
        for c in copies(f, slot):
            c.start()

    def step(f, first):
        slot = lax.rem(i * nf + f, 2)
        for c in copies(f, slot):
            c.wait()

        @pl.when(jnp.logical_or(f + 1 < nf, i + 1 < pl.num_programs(0)))
        def _():
            start(lax.rem(f + 1, nf), 1 - slot)

        xn = xn_ref[...]
        parts = []
        for j in range(n_sub):
            hg = jnp.dot(xn, wg_buf[slot, j], preferred_element_type=F32)
            hl = jnp.dot(xn, wl_buf[slot, j], preferred_element_type=F32)
            parts.append((hg * jax.nn.sigmoid(hg) * hl).astype(BF16))
        a = jnp.concatenate(parts, axis=1)
        _accumulate(o_ref, ssq_ref, jnp.dot(a, wd_buf[slot], preferred_element_type=F32), first)

    @pl.when(i == 0)
    def _():
        start(0, 0)

    xn_ref[...] = _rms(x_ref[...], gpre_ref[...]).astype(BF16)
    step(jnp.int32(0), True)

    def body(f, carry):
        step(f, False)
        return carry

    lax.fori_loop(1, nf, body, 0)
    _residual_norm(x_ref, o_ref, ssq_ref, gpost_ref)


def _ffn(x, g_pre, g_post, weights, *, tm, tf, first_tile):
    m = x.shape[0]
    head_tf = weights[0].shape[2]
    rows = pl.BlockSpec((tm, D_MODEL), lambda i: (i + first_tile, 0))
    hbm = pl.BlockSpec(memory_space=pl.ANY)
    return pl.pallas_call(
        _ffn_kernel,
        grid=(m // tm - first_tile,),
        in_specs=[
            rows,
            pl.BlockSpec((1, D_MODEL), lambda i: (0, 0)),
            pl.BlockSpec((1, D_MODEL), lambda i: (0, 0)),
            hbm, hbm, hbm,
        ],
        out_specs=rows,
        out_shape=jax.ShapeDtypeStruct((m, D_MODEL), F32),
        input_output_aliases={0: 0},
        scratch_shapes=[pltpu.VMEM((tm, D_MODEL), BF16), pltpu.VMEM((tm, HEAD_DIM), F32),
                        pltpu.VMEM((2, tf // head_tf, D_MODEL, head_tf), BF16),
                        pltpu.VMEM((2, tf // head_tf, D_MODEL, head_tf), BF16),
                        pltpu.VMEM((2, tf, D_MODEL), BF16),
                        pltpu.SemaphoreType.DMA((3, 2))],
        compiler_params=_params("arbitrary", vmem_limit=FFN_VMEM_LIMIT),
        name="ffn",
    )(x, g_pre.reshape(1, D_MODEL), g_post.reshape(1, D_MODEL), *weights)


def _ffn_head_kernel(x_ref, xs_ref, gpre_ref, gpost_ref, wg_ref, wl_ref, wd_ref,
                     o_ref, os_ref, wgb_ref, wlb_ref, wdb_ref, xn_ref, ssq_ref, ssqs_ref):
    tm = x_ref.shape[0]
    f = pl.program_id(0)

    def step(first):
        wg, wl, wd = (r[...].astype(BF16) for r in (wg_ref, wl_ref, wd_ref))
        wgb_ref[...] = wg
        wlb_ref[...] = wl
        wdb_ref[...] = wd
        xn = xn_ref[...]
        hg = jnp.dot(xn, wg, preferred_element_type=F32)
        hl = jnp.dot(xn, wl, preferred_element_type=F32)
        a = (hg * jax.nn.sigmoid(hg) * hl).astype(BF16)
        part = jnp.dot(a, wd, preferred_element_type=F32)
        _accumulate(o_ref, ssq_ref, part[:tm, :], first)
        _accumulate(os_ref, ssqs_ref, part[tm:, :], first)

    @pl.when(f == 0)
    def _():
        xn_ref[:tm, :] = _rms(x_ref[...], gpre_ref[...]).astype(BF16)
        xn_ref[tm:, :] = _rms(xs_ref[...], gpre_ref[...]).astype(BF16)
        step(True)

    @pl.when(f > 0)
    def _():
        step(False)

    @pl.when(f == pl.num_programs(0) - 1)
    def _():
        _residual_norm(x_ref, o_ref, ssq_ref, gpost_ref)
        _residual_norm(xs_ref, os_ref, ssqs_ref, gpost_ref)


def _ffn_head(x, xs, g_pre, g_post, w_up, w_down, layer, *, tm, tf):
    m = x.shape[0]
    ms = xs.shape[0]
    nf = D_FF // tf
    once = dict(pipeline_mode=pl.Buffered(1))
    head = pl.BlockSpec((tm, D_MODEL), lambda f: (0, 0), **once)
    sample = pl.BlockSpec((ms, D_MODEL), lambda f: (0, 0))
    vec = pl.BlockSpec((1, D_MODEL), lambda f: (0, 0))
    outs = pl.pallas_call(
        _ffn_head_kernel,
        grid=(nf,),
        in_specs=[
            head, sample, vec, vec,
            pl.BlockSpec((None, D_MODEL, tf), lambda f: (layer, 0, f)),
            pl.BlockSpec((None, D_MODEL, tf), lambda f: (layer, 0, nf + f)),
            pl.BlockSpec((None, tf, D_MODEL), lambda f: (layer, f, 0)),
        ],
        out_specs=[
            head, sample,
            pl.BlockSpec((None, D_MODEL, tf), lambda f: (f, 0, 0)),
            pl.BlockSpec((None, D_MODEL, tf), lambda f: (f, 0, 0)),
            pl.BlockSpec((tf, D_MODEL), lambda f: (f, 0)),
        ],
        out_shape=[
            jax.ShapeDtypeStruct((m, D_MODEL), F32),
            jax.ShapeDtypeStruct((ms, D_MODEL), F32),
            jax.ShapeDtypeStruct((nf, D_MODEL, tf), BF16),
            jax.ShapeDtypeStruct((nf, D_MODEL, tf), BF16),
            jax.ShapeDtypeStruct((D_FF, D_MODEL), BF16),
        ],
        input_output_aliases={0: 0},
        scratch_shapes=[pltpu.VMEM((tm + ms, D_MODEL), BF16), pltpu.VMEM((tm, HEAD_DIM), F32),
                        pltpu.VMEM((ms, HEAD_DIM), F32)],
        compiler_params=_params("arbitrary"),
        name="ffn_head",
    )(x, xs, g_pre.reshape(1, D_MODEL), g_post.reshape(1, D_MODEL), w_up, w_up, w_down)
    return outs[0], outs[1], tuple(outs[2:])


def _swa_kernel(q_ref, k_ref, v_ref, tb_ref, o_ref, lse_ref):
    n_units, n_res, u, _ = q_ref.shape
    per_blk = N_BACK // u
    n_blk = n_units // per_blk
    lane = lax.broadcasted_iota(jnp.int32, (N_BACK, HEAD_DIM), 1)
    ones = jnp.ones((2 * N_BACK, HEAD_DIM), BF16)

    def rows(ref, res, unit0, n_rows, lo, hi):
        return ref[pl.ds(unit0, n_rows // u), res, :, lo:hi].reshape(n_rows, hi - lo)

    def block(res, qu, ku, table):
        n_keys = N_BACK if table == 0 else 2 * N_BACK
        lse_tile = jnp.zeros((N_BACK, HEAD_DIM), F32)
        for h in range(HEADS_PER_GROUP):
            lo, hi = h * HEAD_DIM, (h + 1) * HEAD_DIM
            q = rows(q_ref, res, qu, N_BACK, lo, hi)
            kw = rows(k_ref, res, ku, n_keys, lo, hi)
            vw = rows(v_ref, res, ku, n_keys, lo, hi)
            s = lax.dot_general(q, kw, (((1,), (1,)), ((), ())), preferred_element_type=F32)
            s = s + tb_ref[table, h][:, :n_keys]
            m = jnp.max(s, axis=1, keepdims=True)
            p = jnp.exp(s - m).astype(BF16)
            ov = jnp.dot(p, jnp.concatenate([vw, ones[:n_keys]], axis=1), preferred_element_type=F32)
            den = ov[:, HEAD_DIM:]
            o = ov[:, :HEAD_DIM] / den
            o_ref[pl.ds(qu, per_blk), res, :, lo:hi] = o.reshape(per_blk, u, HEAD_DIM).astype(o_ref.dtype)
            lse_tile = jnp.where(lane == h, m, jnp.where(lane == HEADS_PER_GROUP + h, den, lse_tile))
        lse_ref[pl.ds(qu, per_blk), res, :, :] = lse_tile.reshape(per_blk, u, HEAD_DIM)

    def block_at(res, n):
        block(res, n * per_blk, (n - 1) * per_blk, 1)

    n_trips = (n_blk - 1) // SWA_BLOCKS_PER_TRIP
    for res in range(n_res):
        block(res, 0, 0, 0)

        def body(i, carry, res=res):
            for b in range(SWA_BLOCKS_PER_TRIP):
                block_at(res, 1 + SWA_BLOCKS_PER_TRIP * i + b)
            return carry

        if n_trips > 0:
            lax.fori_loop(0, n_trips, body, 0)
        for n in range(1 + n_trips * SWA_BLOCKS_PER_TRIP, n_blk):
            block_at(res, n)


def _sub_block(dil):
    return N_BACK if dil == 1 else PERM_BLOCK


def _swa_group(zb, tables, g, *, batch, seq):
    dil = SWA_PATTERN[g][1]
    sub = _sub_block(dil)
    n_units, u = seq // sub, sub // dil
    n_blk = n_units * u // N_BACK
    n_res = min(dil, max(1, SWA_BLOCKS_PER_TRIP // n_blk))
    view = zb.reshape(zb.shape[0], batch, n_units, dil, u, GROUP_WIDTH)

    def rows_in(tile):
        return pl.BlockSpec((None, None, n_units, n_res, u, GROUP_WIDTH), lambda i, r: (tile, i, 0, r, 0, 0))

    def rows_out(width):
        return pl.BlockSpec((None, n_units, n_res, u, width), lambda i, r: (i, 0, r, 0, 0))

    return pl.pallas_call(
        _swa_kernel,
        grid=(batch, dil // n_res),
        in_specs=[rows_in(g), rows_in(N_SWA_GROUPS + g), rows_in(2 * N_SWA_GROUPS + g),
                  pl.BlockSpec((None, 2, HEADS_PER_GROUP, N_BACK, 2 * N_BACK), lambda i, r: (g, 0, 0, 0, 0))],
        out_specs=[rows_out(GROUP_WIDTH), rows_out(HEAD_DIM)],
        out_shape=[
            jax.ShapeDtypeStruct((batch, n_units, dil, u, GROUP_WIDTH), BF16),
            jax.ShapeDtypeStruct((batch, n_units, dil, u, HEAD_DIM), F32),
        ],
        compiler_params=_params("parallel", "parallel"),
        name=f"swa_group{g}",
    )(view, view, view, tables)


def _split3(x):
    hi = x.astype(BF16)
    rest = x - hi.astype(F32)
    mid = rest.astype(BF16)
    lo = (rest - mid.astype(F32)).astype(BF16)
    return hi, mid, lo


def _merge_stage(o0_ref, o1_ref, o2_ref, l0_ref, l1_ref, l2_ref, q_ref, kb_ref, vb_ref, o_ref):
    tm = o_ref.shape[0]
    group_refs = ((o0_ref, l0_ref), (o1_ref, l1_ref), (o2_ref, l2_ref))
    outs, lses = [None] * N_SWA_GROUPS, [None] * N_SWA_GROUPS

    def token_order(g):
        o_g, l_g = group_refs[g]
        dil = SWA_PATTERN[g][1]
        o = o_g[...].reshape(tm, GROUP_WIDTH)
        l = l_g[...].reshape(tm, HEAD_DIM)
        if dil > 1:
            inv = _residue_major_perm(PERM_BLOCK, dil, transpose=True)
            l3 = _split3(l)
            o_nat, l_nat = [], []
            for s in range(0, tm, PERM_BLOCK):
                o_nat.append(jnp.dot(inv, o[s:s + PERM_BLOCK, :], preferred_element_type=F32))
                l_nat.append(sum(jnp.dot(inv, t[s:s + PERM_BLOCK, :], preferred_element_type=F32) for t in l3))
            o = jnp.concatenate(o_nat, axis=0)
            l = jnp.concatenate(l_nat, axis=0)
        outs[g] = o.astype(F32)
        lses[g] = l

    def merge_head(h):
        lo, hi = h * HEAD_DIM, (h + 1) * HEAD_DIM
        ls = [l[:, h:h + 1] for l in lses]
        ds = [l[:, HEADS_PER_GROUP + h:HEADS_PER_GROUP + h + 1] for l in lses]
        mx = jnp.maximum(jnp.maximum(ls[0], ls[1]), ls[2])
        es = [d * jnp.exp(l - mx) for l, d in zip(ls, ds)]
        tot = es[0] + es[1] + es[2]
        for g in range(N_SWA_GROUPS):
            alpha = es[g] / tot
            o_ref[:, g * GROUP_WIDTH + lo:g * GROUP_WIDTH + hi] = (outs[g][:, lo:hi] * alpha).astype(o_ref.dtype)

    return ([functools.partial(token_order, g) for g in range(N_SWA_GROUPS)]
            + [functools.partial(merge_head, h) for h in range(HEADS_PER_GROUP)]
            + _mem_attention_parts(q_ref, kb_ref, vb_ref, o_ref, MIXER_WIDTH))


def _swa_merge_out(outs, lses, zb, mem_kv, layer, w, x, g, *, rows_per_batch):
    tm = ROW_TILE
    tiles_per_batch = rows_per_batch // tm

    def group_tile(width, dil):
        sub = _sub_block(dil)
        return lambda tile: pl.BlockSpec(
            (None, tm // sub, dil, sub // dil, width),
            lambda s: (tile(s) // tiles_per_batch, tile(s) % tiles_per_batch, 0, 0, 0))

    specs = ([group_tile(GROUP_WIDTH, dil) for _, dil in SWA_PATTERN]
             + [group_tile(HEAD_DIM, dil) for _, dil in SWA_PATTERN]
             + [lambda tile: pl.BlockSpec((None, tm, MEM_WIDTH), lambda s: (3 * N_SWA_GROUPS, tile(s), 0))])
    return _mix_out(_merge_stage, (*outs, *lses, zb), specs, [], mem_kv, layer, w, x, g,
                    tm=tm, rows_per_batch=rows_per_batch, name="swa_merge_out")


def _head_rows(row, col0):
    return jnp.concatenate([row[:, col0 + h * HEAD_DIM:col0 + (h + 1) * HEAD_DIM]
                            for h in range(HEADS_PER_GROUP)], axis=0)


def _sample_attention(q4, kv_ref, bias=None, new=None):
    k3, v3 = kv_ref[:, 0], kv_ref[:, 1]
    s = jnp.sum(k3 * q4[None], axis=-1, keepdims=True) * ATTN_SCALE
    if bias is not None:
        s = s + bias
    m = jnp.max(s, axis=0)
    if new is not None:
        k_new, v_new, b_new = new
        s_new = jnp.sum(k_new * q4, axis=-1, keepdims=True) * ATTN_SCALE + b_new
        m = jnp.maximum(m, s_new)
    p = jnp.exp(s - m[None])
    den = jnp.sum(p, axis=0)
    o = jnp.sum(p * v3, axis=0)
    if new is not None:
        p_new = jnp.exp(s_new - m)
        den = den + p_new
        o = o + p_new * v_new
    return o / den, m + jnp.log(den)


def _store_head_rows(o_ref, col0, x4):
    for h in range(HEADS_PER_GROUP):
        o_ref[:, col0 + h * HEAD_DIM:col0 + (h + 1) * HEAD_DIM] = x4[h:h + 1, :]


def _sample_mem_attention(q_row, kv_ref, o_ref, col0):
    o, _ = _sample_attention(_head_rows(q_row, 0), kv_ref)
    _store_head_rows(o_ref, col0, o)


def _sample_mix_a_kernel(z_ref, gv_ref, w0_ref, b0_ref, kv_ref, o_ref, vrow_ref):
    u = z_ref[:, 0:MIXER_WIDTH]
    v = _rms(z_ref[:, MIXER_WIDTH:2 * MIXER_WIDTH], gv_ref[...])
    vrow_ref[...] = v
    o_ref[:, 0:MIXER_WIDTH] = u * (w0_ref[...] * v + b0_ref[...])
    _sample_mem_attention(z_ref[:, 2 * MIXER_WIDTH:2 * MIXER_WIDTH + MEM_WIDTH], kv_ref, o_ref, MIXER_WIDTH)


def _sample_mix_a(z, g_v, w_s, b_s, mem_kv, layer):
    bd = mem_kv.shape[1]
    w0 = jnp.repeat(w_s[:, 0, 0], GROUP_DIM_A).reshape(1, MIXER_WIDTH)
    b0 = jnp.repeat(b_s[:, 0], GROUP_DIM_A).reshape(1, MIXER_WIDTH)
    width = z.shape[1]
    vec = lambda i: (0, 0)
    return pl.pallas_call(
        _sample_mix_a_kernel,
        grid=(bd,),
        in_specs=[
            pl.BlockSpec((None, 1, width), lambda i: (i, 0, 0)),
            pl.BlockSpec((1, MIXER_WIDTH), vec),
            pl.BlockSpec((1, MIXER_WIDTH), vec),
            pl.BlockSpec((1, MIXER_WIDTH), vec),
            _mem_kv_spec(layer, lambda i: i),
        ],
        out_specs=[
            pl.BlockSpec((None, 1, D_MODEL), lambda i: (i, 0, 0)),
            pl.BlockSpec((None, 1, MIXER_WIDTH), lambda i: (i, 0, 0)),
        ],
        out_shape=[
            jax.ShapeDtypeStruct((bd, 1, D_MODEL), F32),
            jax.ShapeDtypeStruct((bd, 1, MIXER_WIDTH), F32),
        ],
        compiler_params=_params("parallel"),
        name="sample_mix_a",
    )(z[:bd].reshape(bd, 1, width), g_v.reshape(1, MIXER_WIDTH), w0, b0, mem_kv)


def _sample_mix_b_kernel(z_ref, c0_ref, c1_ref, c2_ref, bcol_ref, bnew_ref, kv_ref, o_ref):
    caches = (c0_ref, c1_ref, c2_ref)
    z = z_ref[...]
    outs, lses = [], []
    for g in range(N_SWA_GROUPS):
        c0 = g * GROUP_WIDTH
        new = (_head_rows(z, MIXER_WIDTH + c0), _head_rows(z, 2 * MIXER_WIDTH + c0), bnew_ref[g])
        o, lse = _sample_attention(_head_rows(z, c0), caches[g], bias=bcol_ref[g], new=new)
        outs.append(o)
        lses.append(lse)
    mx = jnp.maximum(jnp.maximum(lses[0], lses[1]), lses[2])
    es = [jnp.exp(l - mx) for l in lses]
    tot = es[0] + es[1] + es[2]
    for g in range(N_SWA_GROUPS):
        _store_head_rows(o_ref, g * GROUP_WIDTH, outs[g] * (es[g] / tot))
    _sample_mem_attention(z[:, 3 * MIXER_WIDTH:3 * MIXER_WIDTH + MEM_WIDTH], kv_ref, o_ref, MIXER_WIDTH)


def _sample_mix_b(z, win_caches, swa_layer, bias_groups, mem_kv, layer):
    bd = mem_kv.shape[1]
    width = z.shape[1]
    cache_views, cache_specs = [], []
    for g, (win, dil) in enumerate(SWA_PATTERN):
        c = win_caches[g]
        cache_views.append(c.reshape(c.shape[0], bd, win // dil, dil, 2, HEADS_PER_GROUP, HEAD_DIM))
        cache_specs.append(pl.BlockSpec((None, None, N_BACK, None, 2, HEADS_PER_GROUP, HEAD_DIM),
                                        lambda i: (swa_layer, i, 0, 0, 0, 0, 0)))
    bcol = jnp.stack([bg[:, N_BACK:0:-1].T for bg in bias_groups], axis=0)
    bcol = jnp.broadcast_to(bcol[..., None], bcol.shape + (HEAD_DIM,))
    bnew = jnp.stack([bg[:, 0] for bg in bias_groups], axis=0)
    bnew = jnp.broadcast_to(bnew[..., None], bnew.shape + (HEAD_DIM,))
    return pl.pallas_call(
        _sample_mix_b_kernel,
        grid=(bd,),
        in_specs=[pl.BlockSpec((None, 1, width), lambda i: (i, 0, 0))] + cache_specs + [
            pl.BlockSpec((N_SWA_GROUPS, N_BACK, HEADS_PER_GROUP, HEAD_DIM), lambda i: (0, 0, 0, 0)),
            pl.BlockSpec((N_SWA_GROUPS, HEADS_PER_GROUP, HEAD_DIM), lambda i: (0, 0, 0)),
            _mem_kv_spec(layer, lambda i: i),
        ],
        out_specs=pl.BlockSpec((None, 1, D_MODEL), lambda i: (i, 0, 0)),
        out_shape=jax.ShapeDtypeStruct((bd, 1, D_MODEL), F32),
        compiler_params=_params("parallel"),
        name="sample_mix_b",
    )(z[:bd].reshape(bd, 1, width), *cache_views, bcol, bnew, mem_kv)


def _t5_bucket(dist):
    nf = jnp.maximum(dist, MAX_EXACT).astype(F32)
    large = MAX_EXACT + (jnp.log(nf / MAX_EXACT) / math.log(MAX_DISTANCE / MAX_EXACT)
                         * (N_BUCKETS - MAX_EXACT)).astype(jnp.int32)
    large = jnp.minimum(large, N_BUCKETS - 1)
    return jnp.where(dist < MAX_EXACT, dist, large)


def _group_bias(rel_bias, g, dil):
    dist = jnp.arange(N_BACK + 1, dtype=jnp.int32) * dil
    b = rel_bias[_t5_bucket(dist)][:, g * HEADS_PER_GROUP:(g + 1) * HEADS_PER_GROUP]
    return b.T.astype(F32)


def _band_tables_kernel(b_ref, o_ref):
    n, rows, width = o_ref.shape
    for x in range(n):
        row = jnp.broadcast_to(b_ref[x:x + 1, :], (rows, width))
        o_ref[x] = pltpu.roll(row, 0, 1, stride=1, stride_axis=0)


def _band_tables(bias_groups):
    width = 2 * N_BACK
    rows = []
    for bias_j in bias_groups:
        masked = jnp.full((HEADS_PER_GROUP, N_BACK - 1), NEG_INF, F32)
        rows.append(jnp.concatenate([bias_j[:, :1], masked, bias_j[:, N_BACK:0:-1]], axis=1))
        rows.append(jnp.concatenate([bias_j[:, ::-1], masked], axis=1))
    base = jnp.stack(rows, axis=0).reshape(-1, width)
    tabs = pl.pallas_call(
        _band_tables_kernel,
        out_shape=jax.ShapeDtypeStruct((base.shape[0], N_BACK, width), F32),
        name="band_tables",
    )(base)
    return tabs.reshape(N_SWA_GROUPS, 2, HEADS_PER_GROUP, N_BACK, width)


def _kv_tail_kernel(k_ref, v_ref, o_ref, *, dil):
    rows = k_ref.shape[0]
    for kv, ref in enumerate((k_ref, v_ref)):
        x = ref[...]
        if dil > 1:
            inv = _residue_major_perm(PERM_BLOCK, dil, transpose=True)
            x = jnp.concatenate([jnp.dot(inv, x[s:s + PERM_BLOCK, :], preferred_element_type=F32)
                                 for s in range(0, rows, PERM_BLOCK)], axis=0)
        x = x.astype(F32)
        for h in range(HEADS_PER_GROUP):
            o_ref[:, kv, h, :] = x[:, h * HEAD_DIM:(h + 1) * HEAD_DIM]


def _kv_tail(zb, g, *, batch, seq):
    win, dil = SWA_PATTERN[g]
    rows = min(win, FFN_ROW_TILE)
    first = (seq - win) // rows
    per_batch = seq // rows

    def tile(t):
        return pl.BlockSpec((None, rows, GROUP_WIDTH), lambda b, s: (t, b * per_batch + first + s, 0))

    return pl.pallas_call(
        functools.partial(_kv_tail_kernel, dil=dil),
        grid=(batch, win // rows),
        in_specs=[tile(N_SWA_GROUPS + g), tile(2 * N_SWA_GROUPS + g)],
        out_specs=pl.BlockSpec((None, rows, 2, HEADS_PER_GROUP, HEAD_DIM), lambda b, s: (b, s, 0, 0, 0)),
        out_shape=jax.ShapeDtypeStruct((batch, win, 2, HEADS_PER_GROUP, HEAD_DIM), F32),
        compiler_params=_params("parallel", "parallel"),
        name=f"kv_tail{g}",
    )(zb, zb)


def kernel(x_prompt, x_sample, mem_prompt, cache_mem_kv, cache_win128_kv, cache_win512_kv, cache_win2048_kv, rel_bias, norm_mix_pre, norm_mix_post, norm_ffn_pre, norm_ffn_post, norm_mem, w_mem_kv, w_in_a, norm_v_a, w_spatial_a, b_spatial_a, w_in_b, w_out, w_ffn_up, w_ffn_down):
    batch, seq, _ = x_prompt.shape
    bd = x_sample.shape[0]
    depth = w_out.shape[0]
    m_p = batch * seq
    win_caches = (cache_win128_kv, cache_win512_kv, cache_win2048_kv)

    bias_groups = [_group_bias(rel_bias, g, dil) for g, (_, dil) in enumerate(SWA_PATTERN)]
    band_tables = _band_tables(bias_groups)

    yp = x_prompt.reshape(m_p, D_MODEL)
    ys = jnp.pad(x_sample.reshape(bd, D_MODEL), ((0, SAMPLE_PAD - bd), (0, 0)))
    mem_rows = mem_prompt.reshape(batch * N_MEM, D_MODEL)

    mem_kv_p = _mem_kv(mem_rows, norm_mem, w_mem_kv, batch=batch)
    chunk_v_s = []
    win_p = [[] for _ in SWA_PATTERN]
    win_s = [[] for _ in SWA_PATTERN]
    for i in range(depth):
        li = i // 2
        if i % 2 == 0:
            zp, zs = _in_proj_a(yp, ys, norm_mix_pre[i], w_in_a, li, tm=ROW_TILE, tn=CAST_TILE,
                                gelu_cols=2 * MIXER_WIDTH)
            mix_s, v_rows = _sample_mix_a(zs, norm_v_a[li], w_spatial_a[li], b_spatial_a[li], cache_mem_kv, i)
            chunk_v_s.append(v_rows)
            ys, w_o = _sample_out_proj(mix_s, w_out, i, ys, norm_mix_post[i])
            yp = _gmlp_mix_out(zp, norm_v_a[li], w_spatial_a[li], b_spatial_a[li], mem_kv_p, i,
                               w_o, yp, norm_mix_post[i], tm=ROW_TILE, rows_per_batch=seq)
        else:
            zb, zs = _in_proj_b(yp, ys, norm_mix_pre[i], w_in_b, li, tm=ROW_TILE)
            outs, lses = [], []
            for g, (win, dil) in enumerate(SWA_PATTERN):
                o, lse = _swa_group(zb, band_tables, g, batch=batch, seq=seq)
                outs.append(o)
                lses.append(lse)
                win_p[g].append(_kv_tail(zb, g, batch=batch, seq=seq))
                kv_new = zs[:bd, MIXER_WIDTH:3 * MIXER_WIDTH]
                kv_new = kv_new.reshape(bd, 1, 2, N_SWA_GROUPS, HEADS_PER_GROUP, HEAD_DIM)[:, :, :, g]
                win_s[g].append(kv_new)
            mix_s = _sample_mix_b(zs, win_caches, li, bias_groups, cache_mem_kv, i)
            ys, w_o = _sample_out_proj(mix_s, w_out, i, ys, norm_mix_post[i])
            yp = _swa_merge_out(outs, lses, zb, mem_kv_p, i, w_o, yp, norm_mix_post[i], rows_per_batch=seq)
        yp, ys, w_ffn = _ffn_head(yp, ys, norm_ffn_pre[i], norm_ffn_post[i], w_ffn_up, w_ffn_down, i,
                                  tm=FFN_ROW_TILE, tf=HEAD_FF_TILE)
        yp = _ffn(yp, norm_ffn_pre[i], norm_ffn_post[i], w_ffn, tm=FFN_ROW_TILE, tf=FF_TILE, first_tile=1)

    return (
        yp.reshape(batch, seq, D_MODEL),
        ys[:bd].reshape(bd, 1, D_MODEL),
        mem_kv_p,
        jnp.stack(chunk_v_s, axis=0),
        jnp.stack(win_p[0], axis=0),
        jnp.stack(win_p[1], axis=0),
        jnp.stack(win_p[2], axis=0),
        jnp.stack(win_s[0], axis=0),
        jnp.stack(win_s[1], axis=0),
        jnp.stack(win_s[2], axis=0),
    )
```

```python
import functools
import math

import jax
import jax.numpy as jnp
from jax import lax
from jax.experimental import pallas as pl
from jax.experimental.pallas import tpu as pltpu

F32 = jnp.float32
BF16 = jnp.bfloat16

D_MODEL = 2048
HEAD_DIM = 128
N_MEM = 256
N_MEM_HEADS = 4
MEM_WIDTH = N_MEM_HEADS * HEAD_DIM
MIXER_WIDTH = D_MODEL - MEM_WIDTH
CHUNK = 128
N_GROUPS_A = 4
GROUP_DIM_A = MIXER_WIDTH // N_GROUPS_A
SWA_PATTERN = ((128, 1), (512, 4), (2048, 16))
N_SWA_GROUPS = len(SWA_PATTERN)
HEADS_PER_GROUP = 4
GROUP_WIDTH = HEADS_PER_GROUP * HEAD_DIM
N_BACK = 128
N_BUCKETS = 32
MAX_EXACT = N_BUCKETS // 2
MAX_DISTANCE = 2048
D_FF = 5632
EPS = 1e-6
NEG_INF = -1e30
ATTN_SCALE = HEAD_DIM ** -0.5
SAMPLE_PAD = 16
PERM_BLOCK = 256
SWA_BLOCKS_PER_TRIP = 16
OUT_CHUNK = 256

ROW_TILE = 512
FFN_ROW_TILE = 1024
FF_TILE = 512
HEAD_FF_TILE = 256
N_HEAD_SLOTS = 3
CAST_TILE = 512

V7X_VMEM_BYTES = 64 * 1024 * 1024
VMEM_LIMIT = V7X_VMEM_BYTES - 8 * 1024 * 1024
FFN_VMEM_LIMIT = V7X_VMEM_BYTES - 2 * 1024 * 1024


def _params(*sem, vmem_limit=VMEM_LIMIT):
    return pltpu.CompilerParams(dimension_semantics=sem, vmem_limit_bytes=vmem_limit)


def _gelu(x):
    return 0.5 * x * (1.0 + jnp.tanh(0.7978845608028654 * (x + 0.044715 * (x * x * x))))


def _rms(x, g):
    return x * lax.rsqrt(jnp.mean(x * x, axis=-1, keepdims=True) + EPS) * g


def _log2(n):
    assert n & (n - 1) == 0
    return n.bit_length() - 1


def _residue_major_perm(tm, dil, transpose=False):
    n = tm // dil
    row = lax.broadcasted_iota(jnp.int32, (tm, tm), 0)
    col = lax.broadcasted_iota(jnp.int32, (tm, tm), 1)
    dst, src = (col, row) if transpose else (row, col)
    want = lax.shift_left(jnp.bitwise_and(dst, n - 1), _log2(dil)) + lax.shift_right_logical(dst, _log2(n))
    return (src == want).astype(BF16)


def _in_proj_a_kernel(x_ref, xs_ref, g_ref, w_ref, zp_ref, zs_ref, wb_ref, xn_ref, *, n_col, gelu_cols):
    tm = x_ref.shape[0]
    tn = w_ref.shape[1]
    s = pl.program_id(0)

    def prompt_cols(acc, c0):
        col = lax.broadcasted_iota(jnp.int32, acc.shape, 1) + c0
        return jnp.where(col < gelu_cols, _gelu(acc), acc * ATTN_SCALE).astype(zp_ref.dtype)

    @pl.when(s == 0)
    def _():
        xn_ref[:tm, :] = _rms(x_ref[...], g_ref[...]).astype(BF16)
        xn_ref[tm:, :] = _rms(xs_ref[...], g_ref[...]).astype(BF16)

    for j in range(n_col):
        @pl.when(s == j)
        def _(j=j):
            cols = slice(j * tn, (j + 1) * tn)
            w = w_ref[...].astype(BF16)
            wb_ref[:, cols] = w
            acc = jnp.dot(xn_ref[...], w, preferred_element_type=F32)
            zp_ref[:, cols] = prompt_cols(acc[:tm, :], j * tn)
            sample = acc[tm:, :]
            zs_ref[:, cols] = _gelu(sample) if (j + 1) * tn <= gelu_cols else sample

    @pl.when(s >= n_col)
    def _():
        xn = _rms(x_ref[...], g_ref[...]).astype(BF16)
        acc = jnp.dot(xn, wb_ref[...], preferred_element_type=F32)
        zp_ref[...] = prompt_cols(acc, 0)


def _in_proj_a(x, xs, g, w, layer, *, tm, tn, gelu_cols):
    m, k = x.shape
    ms = xs.shape[0]
    n = w.shape[2]
    n_col, n_row = n // tn, m // tm
    assert gelu_cols % tn == 0
    row = lambda s: jnp.maximum(s - (n_col - 1), 0)
    return pl.pallas_call(
        functools.partial(_in_proj_a_kernel, n_col=n_col, gelu_cols=gelu_cols),
        grid=(n_col + n_row - 1,),
        in_specs=[
            pl.BlockSpec((tm, k), lambda s: (row(s), 0)),
            pl.BlockSpec((ms, k), lambda s: (0, 0)),
            pl.BlockSpec((1, k), lambda s: (0, 0)),
            pl.BlockSpec((None, k, tn), lambda s: (layer, 0, jnp.minimum(s, n_col - 1))),
        ],
        out_specs=[
            pl.BlockSpec((tm, n), lambda s: (row(s), 0)),
            pl.BlockSpec((ms, n), lambda s: (0, 0)),
        ],
        out_shape=[
            jax.ShapeDtypeStruct((m, n), BF16),
            jax.ShapeDtypeStruct((ms, n), F32),
        ],
        scratch_shapes=[pltpu.VMEM((k, n), BF16), pltpu.VMEM((tm + ms, k), BF16)],
        compiler_params=_params("arbitrary"),
        name="in_proj_a",
    )(x, xs, g.reshape(1, k), w)


def _mem_kv_kernel(x_ref, g_ref, w_ref, o_ref):
    xn = _rms(x_ref[...], g_ref[...]).astype(BF16)
    acc = jnp.dot(xn, w_ref[...].astype(BF16), preferred_element_type=F32)
    for kv in range(2):
        for h in range(N_MEM_HEADS):
            c0 = (kv * N_MEM_HEADS + h) * HEAD_DIM
            o_ref[:, kv, h, :] = acc[:, c0:c0 + HEAD_DIM]


def _mem_kv(mem_rows, g, w, *, batch):
    m, k = mem_rows.shape
    layers = w.shape[0]
    out = pl.pallas_call(
        _mem_kv_kernel,
        grid=(layers,),
        in_specs=[
            pl.BlockSpec((m, k), lambda l: (0, 0)),
            pl.BlockSpec((None, 1, k), lambda l: (l, 0, 0)),
            pl.BlockSpec((None, k, 2 * MEM_WIDTH), lambda l: (l, 0, 0)),
        ],
        out_specs=pl.BlockSpec((None, m, 2, N_MEM_HEADS, HEAD_DIM), lambda l: (l, 0, 0, 0, 0)),
        out_shape=jax.ShapeDtypeStruct((layers, m, 2, N_MEM_HEADS, HEAD_DIM), F32),
        compiler_params=_params("parallel"),
        name="mem_kv",
    )(mem_rows, g.reshape(layers, 1, k), w)
    return out.reshape(layers, batch, m // batch, 2, N_MEM_HEADS, HEAD_DIM)


def _in_proj_b_kernel(x_ref, xs_ref, g_ref, w_ref, o_ref, zs_ref, wb_ref, xn_ref):
    tm = x_ref.shape[0]
    n_tiles = o_ref.shape[0]
    n_qkv = 3 * N_SWA_GROUPS
    s = pl.program_id(0)

    def row_orders():
        xn = _rms(x_ref[...], g_ref[...]).astype(BF16)
        xn_ref[0, :tm, :] = xn
        for g in range(1, N_SWA_GROUPS):
            perm = _residue_major_perm(PERM_BLOCK, SWA_PATTERN[g][1])
            for r in range(0, tm, PERM_BLOCK):
                xn_ref[g, r:r + PERM_BLOCK, :] = jnp.dot(
                    perm, xn[r:r + PERM_BLOCK, :], preferred_element_type=F32).astype(BF16)

    def prompt_tile(t, acc):
        if t < N_SWA_GROUPS or t == n_qkv:
            acc = acc * ATTN_SCALE
        o_ref[t] = acc.astype(o_ref.dtype)

    src = lambda t: t % N_SWA_GROUPS if t < n_qkv else 0

    @pl.when(s == 0)
    def _():
        row_orders()
        xs = _rms(xs_ref[...], g_ref[...]).astype(BF16)
        for g in range(N_SWA_GROUPS):
            xn_ref[g, tm:, :] = xs

    for t in range(n_tiles):
        @pl.when(s == t)
        def _(t=t):
            cols = slice(t * GROUP_WIDTH, (t + 1) * GROUP_WIDTH)
            w = w_ref[...].astype(BF16)
            wb_ref[:, cols] = w
            acc = jnp.dot(xn_ref[src(t)], w, preferred_element_type=F32)
            prompt_tile(t, acc[:tm, :])
            zs_ref[:, cols] = acc[tm:, :]

    @pl.when(s >= n_tiles)
    def _():
        row_orders()
        for t in range(n_tiles):
            w = wb_ref[:, t * GROUP_WIDTH:(t + 1) * GROUP_WIDTH]
            prompt_tile(t, jnp.dot(xn_ref[src(t), :tm, :], w, preferred_element_type=F32))


def _in_proj_b(x, xs, g, w, layer, *, tm):
    m, k = x.shape
    ms = xs.shape[0]
    n = w.shape[2]
    n_tiles = n // GROUP_WIDTH
    row = lambda s: jnp.maximum(s - (n_tiles - 1), 0)
    return pl.pallas_call(
        _in_proj_b_kernel,
        grid=(n_tiles + m // tm - 1,),
        in_specs=[
            pl.BlockSpec((tm, k), lambda s: (row(s), 0)),
            pl.BlockSpec((ms, k), lambda s: (0, 0)),
            pl.BlockSpec((1, k), lambda s: (0, 0)),
            pl.BlockSpec((None, k, GROUP_WIDTH), lambda s: (layer, 0, jnp.minimum(s, n_tiles - 1))),
        ],
        out_specs=[
            pl.BlockSpec((n_tiles, tm, GROUP_WIDTH), lambda s: (0, row(s), 0)),
            pl.BlockSpec((ms, n), lambda s: (0, 0)),
        ],
        out_shape=[
            jax.ShapeDtypeStruct((n_tiles, m, GROUP_WIDTH), BF16),
            jax.ShapeDtypeStruct((ms, n), F32),
        ],
        scratch_shapes=[pltpu.VMEM((k, n), BF16), pltpu.VMEM((N_SWA_GROUPS, tm + ms, k), BF16)],
        compiler_params=_params("arbitrary"),
        name="in_proj_b",
    )(x, xs, g.reshape(1, k), w)


MEM_KV_SCRATCH = [pltpu.VMEM((N_MEM_HEADS, N_MEM, HEAD_DIM), BF16),
                  pltpu.VMEM((N_MEM_HEADS, N_MEM, 2 * HEAD_DIM), BF16)]


def _prepare_mem_kv(kv_ref, kb_ref, vb_ref):
    for h in range(N_MEM_HEADS):
        kb_ref[h] = kv_ref[:, 0, h, :].astype(BF16)
        vb_ref[h, :, :HEAD_DIM] = kv_ref[:, 1, h, :].astype(BF16)
        vb_ref[h, :, HEAD_DIM:] = jnp.ones((N_MEM, HEAD_DIM), BF16)


def _mem_attention_parts(q_ref, kb_ref, vb_ref, o_ref, col0):
    def head(h):
        lo, hi = h * HEAD_DIM, (h + 1) * HEAD_DIM
        s = lax.dot_general(q_ref[:, lo:hi], kb_ref[h], (((1,), (1,)), ((), ())), preferred_element_type=F32)
        m = jnp.max(s, axis=1, keepdims=True)
        p = jnp.exp(s - m).astype(BF16)
        ov = jnp.dot(p, vb_ref[h], preferred_element_type=F32)
        o_ref[:, col0 + lo:col0 + hi] = (ov[:, :HEAD_DIM] / ov[:, HEAD_DIM:]).astype(o_ref.dtype)

    return [functools.partial(head, h) for h in range(N_MEM_HEADS)]


def _gmlp_stage(u_ref, v_ref, q_ref, gv_ref, ws_ref, bs_ref, kb_ref, vb_ref, vn_ref, o_ref):
    tm = u_ref.shape[0]

    def norm_v():
        vn_ref[...] = _rms(v_ref[...].astype(F32), gv_ref[...]).astype(BF16)

    def group(g):
        row = lax.broadcasted_iota(jnp.int32, (CHUNK, CHUNK), 0)
        col = lax.broadcasted_iota(jnp.int32, (CHUNK, CHUNK), 1)
        w = jnp.where(row >= col, ws_ref[g], 0.0).astype(BF16)
        b = bs_ref[:, g:g + 1]
        c0, c1 = g * GROUP_DIM_A, (g + 1) * GROUP_DIM_A
        for c in range(tm // CHUNK):
            r0, r1 = c * CHUNK, (c + 1) * CHUNK
            s = jnp.dot(w, vn_ref[r0:r1, c0:c1], preferred_element_type=F32) + b
            o_ref[r0:r1, c0:c1] = (u_ref[r0:r1, c0:c1].astype(F32) * s).astype(o_ref.dtype)

    return ([norm_v] + [functools.partial(group, g) for g in range(N_GROUPS_A)]
            + _mem_attention_parts(q_ref, kb_ref, vb_ref, o_ref, MIXER_WIDTH))


def _mem_kv_spec(layer, batch_of):
    return pl.BlockSpec((None, None, N_MEM, 2, N_MEM_HEADS, HEAD_DIM),
                        lambda i: (layer, batch_of(i), 0, 0, 0, 0))


def _mix_out_kernel(*refs, stage, n_in, n_tiles, tiles_per_batch):
    mix_in = refs[:n_in]
    kv_ref, w_ref, x_ref, g_ref, o_ref, buf0_ref, buf1_ref, acc_ref, kb_ref, vb_ref = refs[n_in:n_in + 10]
    extra = refs[n_in + 10:]
    s = pl.program_id(0)

    @pl.when(s == 0)
    def _():
        buf1_ref[...] = jnp.zeros_like(buf1_ref)

    @pl.when(jnp.minimum(s, n_tiles - 1) % tiles_per_batch == 0)
    def _():
        _prepare_mem_kv(kv_ref, kb_ref, vb_ref)

    def step(dst_ref, src_ref):
        parts = stage(*mix_in, kb_ref, vb_ref, *extra, dst_ref)
        n_chunks = D_MODEL // OUT_CHUNK
        for c in range(n_chunks):
            cols = slice(c * OUT_CHUNK, (c + 1) * OUT_CHUNK)
            acc_ref[:, cols] = jnp.dot(src_ref[...], w_ref[:, cols], preferred_element_type=F32)
            for part in parts[c * len(parts) // n_chunks:(c + 1) * len(parts) // n_chunks]:
                part()
        o_ref[...] = x_ref[...] + _rms(acc_ref[...], g_ref[...])

    @pl.when(s % 2 == 0)
    def _():
        step(buf0_ref, buf1_ref)

    @pl.when(s % 2 == 1)
    def _():
        step(buf1_ref, buf0_ref)


def _mix_out(stage, mix_inputs, mix_specs, extra_scratch, mem_kv, layer, w, x, g, *, tm, rows_per_batch, name):
    m = x.shape[0]
    n_tiles = m // tm
    tiles_per_batch = rows_per_batch // tm
    mix_tile = lambda s: jnp.minimum(s, n_tiles - 1)
    out_tile = lambda s: jnp.maximum(s - 1, 0)
    row_spec = pl.BlockSpec((tm, D_MODEL), lambda s: (out_tile(s), 0))
    return pl.pallas_call(
        functools.partial(_mix_out_kernel, stage=stage, n_in=len(mix_inputs), n_tiles=n_tiles,
                          tiles_per_batch=tiles_per_batch),
        grid=(n_tiles + 1,),
        in_specs=[spec(mix_tile) for spec in mix_specs] + [
            _mem_kv_spec(layer, lambda s: mix_tile(s) // tiles_per_batch),
            pl.BlockSpec((D_MODEL, D_MODEL), lambda s: (0, 0), pipeline_mode=pl.Buffered(1)),
            row_spec,
            pl.BlockSpec((1, D_MODEL), lambda s: (0, 0)),
        ],
        out_specs=row_spec,
        out_shape=jax.ShapeDtypeStruct((m, D_MODEL), F32),
        scratch_shapes=[pltpu.VMEM((tm, D_MODEL), BF16), pltpu.VMEM((tm, D_MODEL), BF16),
                        pltpu.VMEM((tm, D_MODEL), F32)] + MEM_KV_SCRATCH + list(extra_scratch),
        compiler_params=_params("arbitrary"),
        name=name,
    )(*mix_inputs, mem_kv, w, x, g.reshape(1, D_MODEL))


def _gmlp_mix_out(zact, g_v, w_s, b_s, mem_kv, layer, w, x, g, *, tm, rows_per_batch):
    const = lambda shape: (lambda tile: pl.BlockSpec(shape, lambda s: (0,) * len(shape)))
    specs = [
        lambda tile: pl.BlockSpec((tm, MIXER_WIDTH), lambda s: (tile(s), 0)),
        lambda tile: pl.BlockSpec((tm, MIXER_WIDTH), lambda s: (tile(s), 1)),
        lambda tile: pl.BlockSpec((tm, MEM_WIDTH), lambda s: (tile(s), 2 * MIXER_WIDTH // MEM_WIDTH)),
        const((1, MIXER_WIDTH)),
        const((N_GROUPS_A, CHUNK, CHUNK)),
        const((CHUNK, N_GROUPS_A)),
    ]
    return _mix_out(_gmlp_stage, (zact, zact, zact, g_v.reshape(1, MIXER_WIDTH), w_s, b_s.T), specs,
                    [pltpu.VMEM((tm, MIXER_WIDTH), BF16)], mem_kv, layer, w, x, g,
                    tm=tm, rows_per_batch=rows_per_batch, name="gmlp_mix_out")


def _out_proj_cast_kernel(mix_ref, w_ref, x_ref, g_ref, o_ref, wb_ref, acc_ref):
    j = pl.program_id(0)
    w = w_ref[...].astype(BF16)
    wb_ref[...] = w
    acc_ref[j] = jnp.dot(mix_ref[...], w, preferred_element_type=F32)

    @pl.when(j == pl.num_programs(0) - 1)
    def _():
        o = jnp.concatenate([acc_ref[t] for t in range(acc_ref.shape[0])], axis=1)
        o_ref[...] = x_ref[...] + _rms(o, g_ref[...])


def _sample_out_proj(mix, w, layer, x, g):
    bd = mix.shape[0]
    mix = jnp.pad(mix.reshape(bd, D_MODEL), ((0, x.shape[0] - bd), (0, 0))).astype(BF16)
    return _out_proj_cast(mix, w, layer, x, g, tn=CAST_TILE)


def _out_proj_cast(mix, w, layer, x, g, *, tn):
    m = x.shape[0]
    n_tiles = D_MODEL // tn
    return pl.pallas_call(
        _out_proj_cast_kernel,
        grid=(n_tiles,),
        in_specs=[
            pl.BlockSpec((m, D_MODEL), lambda j: (0, 0)),
            pl.BlockSpec((None, D_MODEL, tn), lambda j: (layer, 0, j)),
            pl.BlockSpec((m, D_MODEL), lambda j: (0, 0)),
            pl.BlockSpec((1, D_MODEL), lambda j: (0, 0)),
        ],
        out_specs=[
            pl.BlockSpec((m, D_MODEL), lambda j: (0, 0)),
            pl.BlockSpec((D_MODEL, tn), lambda j: (0, j)),
        ],
        out_shape=[
            jax.ShapeDtypeStruct((m, D_MODEL), F32),
            jax.ShapeDtypeStruct((D_MODEL, D_MODEL), BF16),
        ],
        scratch_shapes=[pltpu.VMEM((n_tiles, m, tn), F32)],
        compiler_params=_params("arbitrary"),
        name="out_proj_cast",
    )(mix, w, x, g.reshape(1, D_MODEL))


def _accumulate(o_ref, ssq_ref, part, first):
    new = part if first else o_ref[...] + part
    o_ref[...] = new
    sq = new * new
    ssq_ref[...] = functools.reduce(
        jnp.add, [sq[:, c:c + HEAD_DIM] for c in range(0, sq.shape[1], HEAD_DIM)])


def _residual_norm(x_ref, o_ref, ssq_ref, g_ref):
    ms = jnp.sum(ssq_ref[...], axis=-1, keepdims=True) * (1.0 / o_ref.shape[1])
    o_ref[...] = x_ref[...] + o_ref[...] * lax.rsqrt(ms + EPS) * g_ref[...]


def _ffn_kernel(x_ref, gpre_ref, gpost_ref, wg_ref, wl_ref, wd_ref, o_ref, xn_ref, ssq_ref):
    f = pl.program_id(1)

    def step(first):
        xn = xn_ref[...]
        parts = []
        for j in range(wg_ref.shape[0]):
            hg = jnp.dot(xn, wg_ref[j], preferred_element_type=F32)
            hl = jnp.dot(xn, wl_ref[j], preferred_element_type=F32)
            parts.append((hg * jax.nn.sigmoid(hg) * hl).astype(BF16))
        a = jnp.concatenate(parts, axis=1)
        _accumulate(o_ref, ssq_ref, jnp.dot(a, wd_ref[...], preferred_element_type=F32), first)

    @pl.when(f == 0)
    def _():
        xn_ref[...] = _rms(x_ref[...], gpre_ref[...]).astype(BF16)
        step(True)

    @pl.when(f > 0)
    def _():
        step(False)

    @pl.when(f == pl.num_programs(1) - 1)
    def _():
        _residual_norm(x_ref, o_ref, ssq_ref, gpost_ref)


def _ffn(x, g_pre, g_post, weights, *, tm, tf, first_tile):
    m = x.shape[0]
    nf = D_FF // tf
    up = pl.BlockSpec((tf // weights[0].shape[2], D_MODEL, weights[0].shape[2]), lambda i, f: (f, 0, 0))
    rows = pl.BlockSpec((tm, D_MODEL), lambda i, f: (i + first_tile, 0))
    return pl.pallas_call(
        _ffn_kernel,
        grid=(m // tm - first_tile, nf),
        in_specs=[
            rows,
            pl.BlockSpec((1, D_MODEL), lambda i, f: (0, 0)),
            pl.BlockSpec((1, D_MODEL), lambda i, f: (0, 0)),
            up, up,
            pl.BlockSpec((tf, D_MODEL), lambda i, f: (f, 0)),
        ],
        out_specs=rows,
        out_shape=jax.ShapeDtypeStruct((m, D_MODEL), F32),
        input_output_aliases={0: 0},
        scratch_shapes=[pltpu.VMEM((tm, D_MODEL), BF16), pltpu.VMEM((tm, HEAD_DIM), F32)],
        compiler_params=_params("parallel", "arbitrary", vmem_limit=FFN_VMEM_LIMIT),
        name="ffn",
    )(x, g_pre.reshape(1, D_MODEL), g_post.reshape(1, D_MODEL), *weights)


def _ffn_head_kernel(x_ref, xs_ref, gpre_ref, gpost_ref, wup_hbm, wdown_hbm,
                     o_ref, os_ref, wgb_ref, wlb_ref, wdb_ref, xn_ref, ssq_ref, ssqs_ref,
                     wg_buf, wl_buf, wd_buf, sem, *, layer):
    tm = x_ref.shape[0]
    tf = wd_buf.shape[1]
    f = pl.program_id(0)
    nf = pl.num_programs(0)

    def copies(t, slot):
        return (
            pltpu.make_async_copy(wup_hbm.at[layer, :, pl.ds(t * tf, tf)], wg_buf.at[slot], sem.at[0, slot]),
            pltpu.make_async_copy(wup_hbm.at[layer, :, pl.ds(D_FF + t * tf, tf)], wl_buf.at[slot],
                                  sem.at[1, slot]),
            pltpu.make_async_copy(wdown_hbm.at[layer, pl.ds(t * tf, tf), :], wd_buf.at[slot], sem.at[2, slot]),
        )

    @pl.when(f == 0)
    def _():
        for t in range(N_HEAD_SLOTS - 1):
            for c in copies(t, t):
                c.start()

    ahead = f + (N_HEAD_SLOTS - 1)

    @pl.when(ahead < nf)
    def _():
        for c in copies(ahead, lax.rem(ahead, N_HEAD_SLOTS)):
            c.start()

    slot = lax.rem(f, N_HEAD_SLOTS)
    for c in copies(f, slot):
        c.wait()

    def step(first):
        wg, wl, wd = (r[slot].astype(BF16) for r in (wg_buf, wl_buf, wd_buf))
        wgb_ref[...] = wg
        wlb_ref[...] = wl
        wdb_ref[...] = wd
        xn = xn_ref[...]
        hg = jnp.dot(xn, wg, preferred_element_type=F32)
        hl = jnp.dot(xn, wl, preferred_element_type=F32)
        a = (hg * jax.nn.sigmoid(hg) * hl).astype(BF16)
        part = jnp.dot(a, wd, preferred_element_type=F32)
        _accumulate(o_ref, ssq_ref, part[:tm, :], first)
        _accumulate(os_ref, ssqs_ref, part[tm:, :], first)

    @pl.when(f == 0)
    def _():
        xn_ref[:tm, :] = _rms(x_ref[...], gpre_ref[...]).astype(BF16)
        xn_ref[tm:, :] = _rms(xs_ref[...], gpre_ref[...]).astype(BF16)
        step(True)

    @pl.when(f > 0)
    def _():
        step(False)

    @pl.when(f == pl.num_programs(0) - 1)
    def _():
        _residual_norm(x_ref, o_ref, ssq_ref, gpost_ref)
        _residual_norm(xs_ref, os_ref, ssqs_ref, gpost_ref)


def _ffn_head(x, xs, g_pre, g_post, w_up, w_down, layer, *, tm, tf):
    m = x.shape[0]
    ms = xs.shape[0]
    nf = D_FF // tf
    once = dict(pipeline_mode=pl.Buffered(1))
    head = pl.BlockSpec((tm, D_MODEL), lambda f: (0, 0), **once)
    sample = pl.BlockSpec((ms, D_MODEL), lambda f: (0, 0))
    vec = pl.BlockSpec((1, D_MODEL), lambda f: (0, 0))
    hbm = pl.BlockSpec(memory_space=pl.ANY)
    outs = pl.pallas_call(
        functools.partial(_ffn_head_kernel, layer=layer),
        grid=(nf,),
        in_specs=[head, sample, vec, vec, hbm, hbm],
        out_specs=[
            head, sample,
            pl.BlockSpec((None, D_MODEL, tf), lambda f: (f, 0, 0)),
            pl.BlockSpec((None, D_MODEL, tf), lambda f: (f, 0, 0)),
            pl.BlockSpec((tf, D_MODEL), lambda f: (f, 0)),
        ],
        out_shape=[
            jax.ShapeDtypeStruct((m, D_MODEL), F32),
            jax.ShapeDtypeStruct((ms, D_MODEL), F32),
            jax.ShapeDtypeStruct((nf, D_MODEL, tf), BF16),
            jax.ShapeDtypeStruct((nf, D_MODEL, tf), BF16),
            jax.ShapeDtypeStruct((D_FF, D_MODEL), BF16),
        ],
        input_output_aliases={0: 0},
        scratch_shapes=[pltpu.VMEM((tm + ms, D_MODEL), BF16), pltpu.VMEM((tm, HEAD_DIM), F32),
                        pltpu.VMEM((ms, HEAD_DIM), F32),
                        pltpu.VMEM((N_HEAD_SLOTS, D_MODEL, tf), F32),
                        pltpu.VMEM((N_HEAD_SLOTS, D_MODEL, tf), F32),
                        pltpu.VMEM((N_HEAD_SLOTS, tf, D_MODEL), F32),
                        pltpu.SemaphoreType.DMA((3, N_HEAD_SLOTS))],
        compiler_params=_params("arbitrary"),
        name="ffn_head",
    )(x, xs, g_pre.reshape(1, D_MODEL), g_post.reshape(1, D_MODEL), w_up, w_down)
    return outs[0], outs[1], tuple(outs[2:])


def _swa_kernel(q_ref, k_ref, v_ref, tb_ref, o_ref, lse_ref):
    n_units, n_res, u, _ = q_ref.shape
    per_blk = N_BACK // u
    n_blk = n_units // per_blk
    lane = lax.broadcasted_iota(jnp.int32, (N_BACK, HEAD_DIM), 1)
    ones = jnp.ones((2 * N_BACK, HEAD_DIM), BF16)

    def rows(ref, res, unit0, n_rows, lo, hi):
        return ref[pl.ds(unit0, n_rows // u), res, :, lo:hi].reshape(n_rows, hi - lo)

    def block(res, qu, ku, table):
        n_keys = N_BACK if table == 0 else 2 * N_BACK
        lse_tile = jnp.zeros((N_BACK, HEAD_DIM), F32)
        for h in range(HEADS_PER_GROUP):
            lo, hi = h * HEAD_DIM, (h + 1) * HEAD_DIM
            q = rows(q_ref, res, qu, N_BACK, lo, hi)
            kw = rows(k_ref, res, ku, n_keys, lo, hi)
            vw = rows(v_ref, res, ku, n_keys, lo, hi)
            s = lax.dot_general(q, kw, (((1,), (1,)), ((), ())), preferred_element_type=F32)
            s = s + tb_ref[table, h][:, :n_keys]
            m = jnp.max(s, axis=1, keepdims=True)
            p = jnp.exp(s - m).astype(BF16)
            ov = jnp.dot(p, jnp.concatenate([vw, ones[:n_keys]], axis=1), preferred_element_type=F32)
            den = ov[:, HEAD_DIM:]
            o = ov[:, :HEAD_DIM] / den
            o_ref[pl.ds(qu, per_blk), res, :, lo:hi] = o.reshape(per_blk, u, HEAD_DIM).astype(o_ref.dtype)
            lse_tile = jnp.where(lane == h, m, jnp.where(lane == HEADS_PER_GROUP + h, den, lse_tile))
        lse_ref[pl.ds(qu, per_blk), res, :, :] = lse_tile.reshape(per_blk, u, HEAD_DIM)

    def block_at(res, n):
        block(res, n * per_blk, (n - 1) * per_blk, 1)

    n_trips = (n_blk - 1) // SWA_BLOCKS_PER_TRIP
    for res in range(n_res):
        block(res, 0, 0, 0)

        def body(i, carry, res=res):
            for b in range(SWA_BLOCKS_PER_TRIP):
                block_at(res, 1 + SWA_BLOCKS_PER_TRIP * i + b)
            return carry

        if n_trips > 0:
            lax.fori_loop(0, n_trips, body, 0)
        for n in range(1 + n_trips * SWA_BLOCKS_PER_TRIP, n_blk):
            block_at(res, n)


def _sub_block(dil):
    return N_BACK if dil == 1 else PERM_BLOCK


def _swa_group(zb, tables, g, *, batch, seq):
    dil = SWA_PATTERN[g][1]
    sub = _sub_block(dil)
    n_units, u = seq // sub, sub // dil
    n_blk = n_units * u // N_BACK
    n_res = min(dil, max(1, SWA_BLOCKS_PER_TRIP // n_blk))
    view = zb.reshape(zb.shape[0], batch, n_units, dil, u, GROUP_WIDTH)

    def rows_in(tile):
        return pl.BlockSpec((None, None, n_units, n_res, u, GROUP_WIDTH), lambda i, r: (tile, i, 0, r, 0, 0))

    def rows_out(width):
        return pl.BlockSpec((None, n_units, n_res, u, width), lambda i, r: (i, 0, r, 0, 0))

    return pl.pallas_call(
        _swa_kernel,
        grid=(batch, dil // n_res),
        in_specs=[rows_in(g), rows_in(N_SWA_GROUPS + g), rows_in(2 * N_SWA_GROUPS + g),
                  pl.BlockSpec((None, 2, HEADS_PER_GROUP, N_BACK, 2 * N_BACK), lambda i, r: (g, 0, 0, 0, 0))],
        out_specs=[rows_out(GROUP_WIDTH), rows_out(HEAD_DIM)],
        out_shape=[
            jax.ShapeDtypeStruct((batch, n_units, dil, u, GROUP_WIDTH), BF16),
            jax.ShapeDtypeStruct((batch, n_units, dil, u, HEAD_DIM), F32),
        ],
        compiler_params=_params("parallel", "parallel"),
        name=f"swa_group{g}",
    )(view, view, view, tables)


def _split3(x):
    hi = x.astype(BF16)
    rest = x - hi.astype(F32)
    mid = rest.astype(BF16)
    lo = (rest - mid.astype(F32)).astype(BF16)
    return hi, mid, lo


def _merge_stage(o0_ref, o1_ref, o2_ref, l0_ref, l1_ref, l2_ref, q_ref, kb_ref, vb_ref, o_ref):
    tm = o_ref.shape[0]
    group_refs = ((o0_ref, l0_ref), (o1_ref, l1_ref), (o2_ref, l2_ref))
    outs, lses = [None] * N_SWA_GROUPS, [None] * N_SWA_GROUPS

    def token_order(g):
        o_g, l_g = group_refs[g]
        dil = SWA_PATTERN[g][1]
        o = o_g[...].reshape(tm, GROUP_WIDTH)
        l = l_g[...].reshape(tm, HEAD_DIM)
        if dil > 1:
            inv = _residue_major_perm(PERM_BLOCK, dil, transpose=True)
            l3 = _split3(l)
            o_nat, l_nat = [], []
            for s in range(0, tm, PERM_BLOCK):
                o_nat.append(jnp.dot(inv, o[s:s + PERM_BLOCK, :], preferred_element_type=F32))
                l_nat.append(sum(jnp.dot(inv, t[s:s + PERM_BLOCK, :], preferred_element_type=F32) for t in l3))
            o = jnp.concatenate(o_nat, axis=0)
            l = jnp.concatenate(l_nat, axis=0)
        outs[g] = o.astype(F32)
        lses[g] = l

    def merge_head(h):
        lo, hi = h * HEAD_DIM, (h + 1) * HEAD_DIM
        ls = [l[:, h:h + 1] for l in lses]
        ds = [l[:, HEADS_PER_GROUP + h:HEADS_PER_GROUP + h + 1] for l in lses]
        mx = jnp.maximum(jnp.maximum(ls[0], ls[1]), ls[2])
        es = [d * jnp.exp(l - mx) for l, d in zip(ls, ds)]
        tot = es[0] + es[1] + es[2]
        for g in range(N_SWA_GROUPS):
            alpha = es[g] / tot
            o_ref[:, g * GROUP_WIDTH + lo:g * GROUP_WIDTH + hi] = (outs[g][:, lo:hi] * alpha).astype(o_ref.dtype)

    return ([functools.partial(token_order, g) for g in range(N_SWA_GROUPS)]
            + [functools.partial(merge_head, h) for h in range(HEADS_PER_GROUP)]
            + _mem_attention_parts(q_ref, kb_ref, vb_ref, o_ref, MIXER_WIDTH))


def _swa_merge_out(outs, lses, zb, mem_kv, layer, w, x, g, *, rows_per_batch):
    tm = ROW_TILE
    tiles_per_batch = rows_per_batch // tm

    def group_tile(width, dil):
        sub = _sub_block(dil)
        return lambda tile: pl.BlockSpec(
            (None, tm // sub, dil, sub // dil, width),
            lambda s: (tile(s) // tiles_per_batch, tile(s) % tiles_per_batch, 0, 0, 0))

    specs = ([group_tile(GROUP_WIDTH, dil) for _, dil in SWA_PATTERN]
             + [group_tile(HEAD_DIM, dil) for _, dil in SWA_PATTERN]
             + [lambda tile: pl.BlockSpec((None, tm, MEM_WIDTH), lambda s: (3 * N_SWA_GROUPS, tile(s), 0))])
    return _mix_out(_merge_stage, (*outs, *lses, zb), specs, [], mem_kv, layer, w, x, g,
                    tm=tm, rows_per_batch=rows_per_batch, name="swa_merge_out")


def _head_rows(row, col0):
    return jnp.concatenate([row[:, col0 + h * HEAD_DIM:col0 + (h + 1) * HEAD_DIM]
                            for h in range(HEADS_PER_GROUP)], axis=0)


def _sample_attention(q4, kv_ref, bias=None, new=None):
    k3, v3 = kv_ref[:, 0], kv_ref[:, 1]
    s = jnp.sum(k3 * q4[None], axis=-1, keepdims=True) * ATTN_SCALE
    if bias is not None:
        s = s + bias
    m = jnp.max(s, axis=0)
    if new is not None:
        k_new, v_new, b_new = new
        s_new = jnp.sum(k_new * q4, axis=-1, keepdims=True) * ATTN_SCALE + b_new
        m = jnp.maximum(m, s_new)
    p = jnp.exp(s - m[None])
    den = jnp.sum(p, axis=0)
    o = jnp.sum(p * v3, axis=0)
    if new is not None:
        p_new = jnp.exp(s_new - m)
        den = den + p_new
        o = o + p_new * v_new
    return o / den, m + jnp.log(den)


def _store_head_rows(o_ref, col0, x4):
    for h in range(HEADS_PER_GROUP):
        o_ref[:, col0 + h * HEAD_DIM:col0 + (h + 1) * HEAD_DIM] = x4[h:h + 1, :]


def _sample_mem_attention(q_row, kv_ref, o_ref, col0):
    o, _ = _sample_attention(_head_rows(q_row, 0), kv_ref)
    _store_head_rows(o_ref, col0, o)


def _sample_mix_a_kernel(z_ref, gv_ref, w0_ref, b0_ref, kv_ref, o_ref, vrow_ref):
    u = z_ref[:, 0:MIXER_WIDTH]
    v = _rms(z_ref[:, MIXER_WIDTH:2 * MIXER_WIDTH], gv_ref[...])
    vrow_ref[...] = v
    o_ref[:, 0:MIXER_WIDTH] = u * (w0_ref[...] * v + b0_ref[...])
    _sample_mem_attention(z_ref[:, 2 * MIXER_WIDTH:2 * MIXER_WIDTH + MEM_WIDTH], kv_ref, o_ref, MIXER_WIDTH)


def _sample_mix_a(z, g_v, w_s, b_s, mem_kv, layer):
    bd = mem_kv.shape[1]
    w0 = jnp.repeat(w_s[:, 0, 0], GROUP_DIM_A).reshape(1, MIXER_WIDTH)
    b0 = jnp.repeat(b_s[:, 0], GROUP_DIM_A).reshape(1, MIXER_WIDTH)
    width = z.shape[1]
    vec = lambda i: (0, 0)
    return pl.pallas_call(
        _sample_mix_a_kernel,
        grid=(bd,),
        in_specs=[
            pl.BlockSpec((None, 1, width), lambda i: (i, 0, 0)),
            pl.BlockSpec((1, MIXER_WIDTH), vec),
            pl.BlockSpec((1, MIXER_WIDTH), vec),
            pl.BlockSpec((1, MIXER_WIDTH), vec),
            _mem_kv_spec(layer, lambda i: i),
        ],
        out_specs=[
            pl.BlockSpec((None, 1, D_MODEL), lambda i: (i, 0, 0)),
            pl.BlockSpec((None, 1, MIXER_WIDTH), lambda i: (i, 0, 0)),
        ],
        out_shape=[
            jax.ShapeDtypeStruct((bd, 1, D_MODEL), F32),
            jax.ShapeDtypeStruct((bd, 1, MIXER_WIDTH), F32),
        ],
        compiler_params=_params("parallel"),
        name="sample_mix_a",
    )(z[:bd].reshape(bd, 1, width), g_v.reshape(1, MIXER_WIDTH), w0, b0, mem_kv)


def _sample_mix_b_kernel(z_ref, c0_ref, c1_ref, c2_ref, bcol_ref, bnew_ref, kv_ref, o_ref):
    caches = (c0_ref, c1_ref, c2_ref)
    z = z_ref[...]
    outs, lses = [], []
    for g in range(N_SWA_GROUPS):
        c0 = g * GROUP_WIDTH
        new = (_head_rows(z, MIXER_WIDTH + c0), _head_rows(z, 2 * MIXER_WIDTH + c0), bnew_ref[g])
        o, lse = _sample_attention(_head_rows(z, c0), caches[g], bias=bcol_ref[g], new=new)
        outs.append(o)
        lses.append(lse)
    mx = jnp.maximum(jnp.maximum(lses[0], lses[1]), lses[2])
    es = [jnp.exp(l - mx) for l in lses]
    tot = es[0] + es[1] + es[2]
    for g in range(N_SWA_GROUPS):
        _store_head_rows(o_ref, g * GROUP_WIDTH, outs[g] * (es[g] / tot))
    _sample_mem_attention(z[:, 3 * MIXER_WIDTH:3 * MIXER_WIDTH + MEM_WIDTH], kv_ref, o_ref, MIXER_WIDTH)


def _sample_mix_b(z, win_caches, swa_layer, bias_groups, mem_kv, layer):
    bd = mem_kv.shape[1]
    width = z.shape[1]
    cache_views, cache_specs = [], []
    for g, (win, dil) in enumerate(SWA_PATTERN):
        c = win_caches[g]
        cache_views.append(c.reshape(c.shape[0], bd, win // dil, dil, 2, HEADS_PER_GROUP, HEAD_DIM))
        cache_specs.append(pl.BlockSpec((None, None, N_BACK, None, 2, HEADS_PER_GROUP, HEAD_DIM),
                                        lambda i: (swa_layer, i, 0, 0, 0, 0, 0)))
    bcol = jnp.stack([bg[:, N_BACK:0:-1].T for bg in bias_groups], axis=0)
    bcol = jnp.broadcast_to(bcol[..., None], bcol.shape + (HEAD_DIM,))
    bnew = jnp.stack([bg[:, 0] for bg in bias_groups], axis=0)
    bnew = jnp.broadcast_to(bnew[..., None], bnew.shape + (HEAD_DIM,))
    return pl.pallas_call(
        _sample_mix_b_kernel,
        grid=(bd,),
        in_specs=[pl.BlockSpec((None, 1, width), lambda i: (i, 0, 0))] + cache_specs + [
            pl.BlockSpec((N_SWA_GROUPS, N_BACK, HEADS_PER_GROUP, HEAD_DIM), lambda i: (0, 0, 0, 0)),
            pl.BlockSpec((N_SWA_GROUPS, HEADS_PER_GROUP, HEAD_DIM), lambda i: (0, 0, 0)),
            _mem_kv_spec(layer, lambda i: i),
        ],
        out_specs=pl.BlockSpec((None, 1, D_MODEL), lambda i: (i, 0, 0)),
        out_shape=jax.ShapeDtypeStruct((bd, 1, D_MODEL), F32),
        compiler_params=_params("parallel"),
        name="sample_mix_b",
    )(z[:bd].reshape(bd, 1, width), *cache_views, bcol, bnew, mem_kv)


def _t5_bucket(dist):
    nf = jnp.maximum(dist, MAX_EXACT).astype(F32)
    large = MAX_EXACT + (jnp.log(nf / MAX_EXACT) / math.log(MAX_DISTANCE / MAX_EXACT)
                         * (N_BUCKETS - MAX_EXACT)).astype(jnp.int32)
    large = jnp.minimum(large, N_BUCKETS - 1)
    return jnp.where(dist < MAX_EXACT, dist, large)


def _group_bias(rel_bias, g, dil):
    dist = jnp.arange(N_BACK + 1, dtype=jnp.int32) * dil
    b = rel_bias[_t5_bucket(dist)][:, g * HEADS_PER_GROUP:(g + 1) * HEADS_PER_GROUP]
    return b.T.astype(F32)


def _band_tables_kernel(b_ref, o_ref):
    n, rows, width = o_ref.shape
    for x in range(n):
        row = jnp.broadcast_to(b_ref[x:x + 1, :], (rows, width))
        o_ref[x] = pltpu.roll(row, 0, 1, stride=1, stride_axis=0)


def _band_tables(bias_groups):
    width = 2 * N_BACK
    rows = []
    for bias_j in bias_groups:
        masked = jnp.full((HEADS_PER_GROUP, N_BACK - 1), NEG_INF, F32)
        rows.append(jnp.concatenate([bias_j[:, :1], masked, bias_j[:, N_BACK:0:-1]], axis=1))
        rows.append(jnp.concatenate([bias_j[:, ::-1], masked], axis=1))
    base = jnp.stack(rows, axis=0).reshape(-1, width)
    tabs = pl.pallas_call(
        _band_tables_kernel,
        out_shape=jax.ShapeDtypeStruct((base.shape[0], N_BACK, width), F32),
        name="band_tables",
    )(base)
    return tabs.reshape(N_SWA_GROUPS, 2, HEADS_PER_GROUP, N_BACK, width)


def _kv_tail_kernel(k_ref, v_ref, o_ref, *, dil):
    rows = k_ref.shape[0]
    for kv, ref in enumerate((k_ref, v_ref)):
        x = ref[...]
        if dil > 1:
            inv = _residue_major_perm(PERM_BLOCK, dil, transpose=True)
            x = jnp.concatenate([jnp.dot(inv, x[s:s + PERM_BLOCK, :], preferred_element_type=F32)
                                 for s in range(0, rows, PERM_BLOCK)], axis=0)
        x = x.astype(F32)
        for h in range(HEADS_PER_GROUP):
            o_ref[:, kv, h, :] = x[:, h * HEAD_DIM:(h + 1) * HEAD_DIM]


def _kv_tail(zb, g, *, batch, seq):
    win, dil = SWA_PATTERN[g]
    rows = min(win, FFN_ROW_TILE)
    first = (seq - win) // rows
    per_batch = seq // rows

    def tile(t):
        return pl.BlockSpec((None, rows, GROUP_WIDTH), lambda b, s: (t, b * per_batch + first + s, 0))

    return pl.pallas_call(
        functools.partial(_kv_tail_kernel, dil=dil),
        grid=(batch, win // rows),
        in_specs=[tile(N_SWA_GROUPS + g), tile(2 * N_SWA_GROUPS + g)],
        out_specs=pl.BlockSpec((None, rows, 2, HEADS_PER_GROUP, HEAD_DIM), lambda b, s: (b, s, 0, 0, 0)),
        out_shape=jax.ShapeDtypeStruct((batch, win, 2, HEADS_PER_GROUP, HEAD_DIM), F32),
        compiler_params=_params("parallel", "parallel"),
        name=f"kv_tail{g}",
    )(zb, zb)


def kernel(x_prompt, x_sample, mem_prompt, cache_mem_kv, cache_win128_kv, cache_win512_kv, cache_win2048_kv, rel_bias, norm_mix_pre, norm_mix_post, norm_ffn_pre, norm_ffn_post, norm_mem, w_mem_kv, w_in_a, norm_v_a, w_spatial_a, b_spatial_a, w_in_b, w_out, w_ffn_up, w_ffn_down):
    batch, seq, _ = x_prompt.shape
    bd = x_sample.shape[0]
    depth = w_out.shape[0]
    m_p = batch * seq
    win_caches = (cache_win128_kv, cache_win512_kv, cache_win2048_kv)

    bias_groups = [_group_bias(rel_bias, g, dil) for g, (_, dil) in enumerate(SWA_PATTERN)]
    band_tables = _band_tables(bias_groups)

    yp = x_prompt.reshape(m_p, D_MODEL)
    ys = jnp.pad(x_sample.reshape(bd, D_MODEL), ((0, SAMPLE_PAD - bd), (0, 0)))
    mem_rows = mem_prompt.reshape(batch * N_MEM, D_MODEL)

    mem_kv_p = _mem_kv(mem_rows, norm_mem, w_mem_kv, batch=batch)
    chunk_v_s = []
    win_p = [[] for _ in SWA_PATTERN]
    win_s = [[] for _ in SWA_PATTERN]
    for i in range(depth):
        li = i // 2
        if i % 2 == 0:
            zp, zs = _in_proj_a(yp, ys, norm_mix_pre[i], w_in_a, li, tm=ROW_TILE, tn=CAST_TILE,
                                gelu_cols=2 * MIXER_WIDTH)
            mix_s, v_rows = _sample_mix_a(zs, norm_v_a[li], w_spatial_a[li], b_spatial_a[li], cache_mem_kv, i)
            chunk_v_s.append(v_rows)
            ys, w_o = _sample_out_proj(mix_s, w_out, i, ys, norm_mix_post[i])
            yp = _gmlp_mix_out(zp, norm_v_a[li], w_spatial_a[li], b_spatial_a[li], mem_kv_p, i,
                               w_o, yp, norm_mix_post[i], tm=ROW_TILE, rows_per_batch=seq)
        else:
            zb, zs = _in_proj_b(yp, ys, norm_mix_pre[i], w_in_b, li, tm=ROW_TILE)
            outs, lses = [], []
            for g, (win, dil) in enumerate(SWA_PATTERN):
                o, lse = _swa_group(zb, band_tables, g, batch=batch, seq=seq)
                outs.append(o)
                lses.append(lse)
                win_p[g].append(_kv_tail(zb, g, batch=batch, seq=seq))
                kv_new = zs[:bd, MIXER_WIDTH:3 * MIXER_WIDTH]
                kv_new = kv_new.reshape(bd, 1, 2, N_SWA_GROUPS, HEADS_PER_GROUP, HEAD_DIM)[:, :, :, g]
                win_s[g].append(kv_new)
            mix_s = _sample_mix_b(zs, win_caches, li, bias_groups, cache_mem_kv, i)
            ys, w_o = _sample_out_proj(mix_s, w_out, i, ys, norm_mix_post[i])
            yp = _swa_merge_out(outs, lses, zb, mem_kv_p, i, w_o, yp, norm_mix_post[i], rows_per_batch=seq)
        yp, ys, w_ffn = _ffn_head(yp, ys, norm_ffn_pre[i], norm_ffn_post[i], w_ffn_up, w_ffn_down, i,
                                  tm=FFN_ROW_TILE, tf=HEAD_FF_TILE)
        yp = _ffn(yp, norm_ffn_pre[i], norm_ffn_post[i], w_ffn, tm=FFN_ROW_TILE, tf=FF_TILE, first_tile=1)

    return (
        yp.reshape(batch, seq, D_MODEL),
        ys[:bd].reshape(bd, 1, D_MODEL),
        mem_kv_p,
        jnp.stack(chunk_v_s, axis=0),
        jnp.stack(win_p[0], axis=0),
        jnp.stack(win_p[1], axis=0),
        jnp.stack(win_p[2], axis=0),
        jnp.stack(win_s[0], axis=0),
        jnp.stack(win_s[1], axis=0),
        jnp.stack(win_s[2], axis=0),
    )
```
